```python
import jax, jax.numpy as jnp
from jax import lax
import numpy as np

D_MODEL = 1024
BATCH = 2
SEQ = 8192
DEPTH = 1

GRID_W = 64
CTX_LEN = 256
RET_HEADS = 8
RET_QK_DIM = 64
RET_V_DIM = 128
RET_QK_WIDTH = RET_HEADS * RET_QK_DIM
RET_V_WIDTH = RET_HEADS * RET_V_DIM
CHUNK = 128
ROPE_BASE = 10000.0
FOURIER_GROUPS = 4
FOURIER_GROUP_DIM = 128
FOURIER_WIDTH = FOURIER_GROUPS * FOURIER_GROUP_DIM
Q_OFF = 0
K_OFF = Q_OFF + RET_QK_WIDTH
V_OFF = K_OFF + RET_QK_WIDTH
G_OFF = V_OFF + RET_V_WIDTH
F_OFF = G_OFF + RET_V_WIDTH
IN_COLS = F_OFF + FOURIER_WIDTH
N_BRANCHES = 2
FFN_DIM = 2816
CONV_W = 3
N_MOD = 6
EPS = 1e-6

kernel_name = "hybrid_retention_fourier_convffn_dit_block"


def rms_norm(x, w):
    x32 = x.astype(jnp.float32)
    y = x32 * lax.rsqrt(jnp.mean(x32 * x32, axis=-1, keepdims=True) + EPS)
    return (y * w.astype(jnp.float32)).astype(x.dtype)


def modulate(h, shift, scale):
    return h * (1 + scale) + shift


def split_heads(t, n_heads):
    b, l, w = t.shape
    return t.reshape(b, l, n_heads, w // n_heads).transpose(0, 2, 1, 3)


def rope_2d(t, row, col):
    n_freq = RET_QK_DIM // 4
    inv = ROPE_BASE ** (-jnp.arange(n_freq, dtype=jnp.float32) / n_freq)
    ang = jnp.concatenate([row[:, None] * inv, col[:, None] * inv], axis=-1)
    cos, sin = jnp.cos(ang), jnp.sin(ang)
    half = RET_QK_DIM // 2
    t1, t2 = t[..., :half], t[..., half:]
    return jnp.concatenate([t1 * cos - t2 * sin, t1 * sin + t2 * cos], axis=-1).astype(t.dtype)


def log_gamma(a):
    return -jnp.exp(a.astype(jnp.float32))


def chunk_retention(q, k, v, a, s0):
    b, h, l, dk = q.shape
    dv = v.shape[-1]
    n = l // CHUNK
    lg = log_gamma(a)
    qc = q.reshape(b, h, n, CHUNK, dk)
    kc = k.reshape(b, h, n, CHUNK, dk)
    vc = v.reshape(b, h, n, CHUNK, dv)
    pos = jnp.arange(CHUNK, dtype=jnp.float32)
    diff = pos[:, None] - pos[None, :]
    intra_decay = jnp.where(diff >= 0, jnp.exp(lg[:, None, None] * jnp.maximum(diff, 0.0)), 0.0)
    scores = jnp.einsum('bhncd,bhnsd->bhncs', qc, kc) * intra_decay[None, :, None]
    intra = jnp.einsum('bhncs,bhnse->bhnce', scores, vc)
    k_dec = kc * jnp.exp(lg[:, None] * (CHUNK - 1 - pos))[None, :, None, :, None]
    kv = jnp.einsum('bhnsd,bhnse->nbhde', k_dec, vc)
    chunk_decay = jnp.exp(lg * CHUNK)[None, :, None, None]

    def step(s, kv_n):
        return chunk_decay * s + kv_n, s

    s_final, s_prev = lax.scan(step, s0.astype(jnp.float32), kv)
    q_dec = qc * jnp.exp(lg[:, None] * (pos + 1))[None, :, None, :, None]
    inter = jnp.einsum('bhncd,nbhde->bhnce', q_dec, s_prev)
    return (intra + inter).reshape(b, h, l, dv), s_final


def retention_state(k, v, a):
    l = k.shape[2]
    lg = log_gamma(a)
    w = jnp.exp(lg[:, None] * (l - 1 - jnp.arange(l, dtype=jnp.float32)))
    return jnp.einsum('bhld,bhle,hl->bhde', k, v, w)


def bidir_retention(q, k, v, a_f, a_b, s_f0, s_b0):
    o_f, s_f = chunk_retention(q, k, v, a_f, s_f0)
    o_b, s_b = chunk_retention(jnp.flip(q, 2), jnp.flip(k, 2), jnp.flip(v, 2), a_b, s_b0)
    return o_f + jnp.flip(o_b, 2), s_f, s_b


def ret_readout(o, g, w_o):
    mu = jnp.mean(o, axis=-1, keepdims=True)
    var = jnp.mean(jnp.square(o - mu), axis=-1, keepdims=True)
    o = (o - mu) * lax.rsqrt(var + EPS)
    b, h, l, dv = o.shape
    o = o.transpose(0, 2, 1, 3).reshape(b, l, h * dv).astype(g.dtype)
    return (o * jax.nn.silu(g)) @ w_o


def fourier_mix(f, w_f):
    b, l, _ = f.shape
    fg = f.reshape(b, l, FOURIER_GROUPS, FOURIER_GROUP_DIM).astype(jnp.float32)
    z = jnp.fft.fft2(fg, axes=(1, 3), norm='ortho').real
    return z.reshape(b, l, FOURIER_WIDTH).astype(f.dtype) @ w_f


def merge_branches(h, ret_d, four_d, w_bg, b_bg, w_out):
    gates = jax.nn.sigmoid(h @ w_bg + b_bg)
    g_r, g_f = jnp.split(gates, N_BRANCHES, axis=-1)
    return (g_r * ret_d + g_f * four_d) @ w_out


def depthwise_conv3(u, w, b):
    up = jnp.pad(u, ((0, 0), (1, 1), (0, 0)))
    return up[:, :-2] * w[0] + up[:, 1:-1] * w[1] + up[:, 2:] * w[2] + b


def conv_ffn(h, w_up, conv_w, conv_b, w_down):
    u = depthwise_conv3(h @ w_up, conv_w, conv_b)
    a, val = jnp.split(u, 2, axis=-1)
    return (jax.nn.gelu(a) * val) @ w_down


def split_proj(p):
    return (p[..., Q_OFF:K_OFF], p[..., K_OFF:V_OFF], p[..., V_OFF:G_OFF],
            p[..., G_OFF:F_OFF], p[..., F_OFF:IN_COLS])


def setup_inputs(seed: int = 0) -> dict:
    key = jax.random.key(seed)
    ks = jax.random.split(key, 24)

    def nrm(k, shape, scale):
        return jax.random.normal(k, shape, jnp.float32) * scale

    d = D_MODEL
    base_decay = jnp.log(-jnp.log1p(-(2.0 ** (-5.0 - jnp.arange(RET_HEADS, dtype=jnp.float32)))))
    return {
        "x": nrm(ks[0], (BATCH, SEQ, d), 1.0),
        "c": nrm(ks[1], (BATCH, d), 1.0),
        "ctx": nrm(ks[2], (BATCH, CTX_LEN, d), 1.0),
        "c_ctx": nrm(ks[3], (d,), 1.0),
        "w_mod": nrm(ks[4], (DEPTH, d, N_MOD * d), 0.5 * d ** -0.5),
        "b_mod": nrm(ks[5], (DEPTH, N_MOD * d), 0.01),
        "norm1_w": 1.0 + nrm(ks[6], (DEPTH, d), 0.05),
        "w_in": nrm(ks[7], (DEPTH, d, IN_COLS), d ** -0.5),
        "ret_decay_f": base_decay[None] + nrm(ks[8], (DEPTH, RET_HEADS), 0.1),
        "ret_decay_b": base_decay[None] + nrm(ks[9], (DEPTH, RET_HEADS), 0.1),
        "w_ret_out": nrm(ks[10], (DEPTH, RET_V_WIDTH, d), RET_V_WIDTH ** -0.5),
        "w_four_out": nrm(ks[11], (DEPTH, FOURIER_WIDTH, d), FOURIER_WIDTH ** -0.5),
        "w_branch_gate": nrm(ks[12], (DEPTH, d, N_BRANCHES * d), d ** -0.5),
        "b_branch_gate": nrm(ks[13], (DEPTH, N_BRANCHES * d), 0.01),
        "w_out": nrm(ks[14], (DEPTH, d, d), d ** -0.5),
        "norm2_w": 1.0 + nrm(ks[15], (DEPTH, d), 0.05),
        "w_up": nrm(ks[16], (DEPTH, d, 2 * FFN_DIM), d ** -0.5),
        "conv_w": nrm(ks[17], (DEPTH, CONV_W, 2 * FFN_DIM), CONV_W ** -0.5),
        "conv_b": nrm(ks[18], (DEPTH, 2 * FFN_DIM), 0.01),
        "w_down": nrm(ks[19], (DEPTH, FFN_DIM, d), FFN_DIM ** -0.5),
        "final_norm_w": 1.0 + nrm(ks[20], (d,), 0.05),
    }


def reference(x, c, ctx, c_ctx, w_mod, b_mod, norm1_w, w_in, ret_decay_f, ret_decay_b,
              w_ret_out, w_four_out, w_branch_gate, b_branch_gate, w_out, norm2_w,
              w_up, conv_w, conv_b, w_down, final_norm_w):
    b, l, _ = x.shape
    rows = l // GRID_W
    row = jnp.repeat(jnp.arange(rows, dtype=jnp.float32), GRID_W)
    col = jnp.tile(jnp.arange(GRID_W, dtype=jnp.float32), rows)
    q_scale = RET_QK_DIM ** -0.5
    silu_c = jax.nn.silu(c)
    silu_cc = jax.nn.silu(c_ctx)
    xc = ctx
    for layer in range(DEPTH):
        last = layer == DEPTH - 1
        wi = w_in[layer]
        a_f, a_b = ret_decay_f[layer], ret_decay_b[layer]
        mod = silu_c @ w_mod[layer] + b_mod[layer]
        sh1, sc1, g1, sh2, sc2, g2 = [m[:, None, :] for m in jnp.split(mod, N_MOD, axis=-1)]
        mc = jnp.split(silu_cc @ w_mod[layer] + b_mod[layer], N_MOD)

        h = modulate(rms_norm(x, norm1_w[layer]), sh1, sc1)
        hc = modulate(rms_norm(xc, norm1_w[layer]), mc[0], mc[1])

        if last:
            kc = split_heads(hc @ wi[:, K_OFF:V_OFF], RET_HEADS)
            vc = split_heads(hc @ wi[:, V_OFF:G_OFF], RET_HEADS)
            s_f = retention_state(kc, vc, a_f)
            s_b = retention_state(jnp.flip(kc, 2), jnp.flip(vc, 2), a_b)
        else:
            qc, kc, vc, gc, fc = split_proj(hc @ wi)
            qc = split_heads(qc, RET_HEADS) * q_scale
            kc = split_heads(kc, RET_HEADS)
            vc = split_heads(vc, RET_HEADS)
            zeros = jnp.zeros((b, RET_HEADS, RET_QK_DIM, RET_V_DIM), jnp.float32)
            oc, s_f, s_b = bidir_retention(qc, kc, vc, a_f, a_b, zeros, zeros)
            yc = merge_branches(hc, ret_readout(oc, gc, w_ret_out[layer]),
                                fourier_mix(fc, w_four_out[layer]),
                                w_branch_gate[layer], b_branch_gate[layer], w_out[layer])
            xc = xc + mc[2] * yc
            hc2 = modulate(rms_norm(xc, norm2_w[layer]), mc[3], mc[4])
            xc = xc + mc[5] * conv_ffn(hc2, w_up[layer], conv_w[layer], conv_b[layer], w_down[layer])

        q, k, v, g, f = split_proj(h @ wi)
        q = rope_2d(split_heads(q, RET_HEADS), row, col) * q_scale
        k = rope_2d(split_heads(k, RET_HEADS), row, col)
        v = split_heads(v, RET_HEADS)
        o, _, _ = bidir_retention(q, k, v, a_f, a_b, s_f, s_b)
        y = merge_branches(h, ret_readout(o, g, w_ret_out[layer]),
                           fourier_mix(f, w_four_out[layer]),
                           w_branch_gate[layer], b_branch_gate[layer], w_out[layer])
        x = x + g1 * y
        h2 = modulate(rms_norm(x, norm2_w[layer]), sh2, sc2)
        x = x + g2 * conv_ffn(h2, w_up[layer], conv_w[layer], conv_b[layer], w_down[layer])
    return rms_norm(x, final_norm_w)
```

```python
import functools

import numpy as np
import jax
import jax.numpy as jnp
from jax import lax
from jax.experimental import pallas as pl
from jax.experimental.pallas import tpu as pltpu

F32 = jnp.float32
BF16 = jnp.bfloat16

D_MODEL = 1024
GRID_W = 64
HEADS = 8
QK_DIM = 64
V_DIM = 128
QK_WIDTH = HEADS * QK_DIM
V_WIDTH = HEADS * V_DIM
ROPE_BASE = 10000.0
F_GROUPS = 4
F_GROUP_DIM = 128
F_WIDTH = F_GROUPS * F_GROUP_DIM
K_OFF = QK_WIDTH
V_OFF = K_OFF + QK_WIDTH
G_OFF = V_OFF + V_WIDTH
F_OFF = G_OFF + V_WIDTH
IN_COLS = F_OFF + F_WIDTH
FFN_DIM = 2816
N_MOD = 6
EPS = 1e-6

LANES = 128
RET_CHUNK = 256
FFN_CHUNK = 256
N_FFN_CHUNKS = FFN_DIM // FFN_CHUNK
TOKEN_TILE = 512
DFT_N1 = 64
DFT_K1_PER_STEP = 8
VMEM_LIMIT = 56 * 1024 * 1024


def _params(sem):
    return pltpu.CompilerParams(dimension_semantics=sem, vmem_limit_bytes=VMEM_LIMIT)


def _dot(a, b):
    return jnp.dot(a, b, preferred_element_type=F32)


def _rms_norm(x, w):
    return x * lax.rsqrt(jnp.mean(x * x, axis=-1, keepdims=True) + EPS) * w


def _const_spec(shape):
    zeros = (0,) * len(shape)
    return pl.BlockSpec(shape, lambda *_: zeros)


def _mod_kernel(c_ref, w_ref, b_ref, o_ref):
    c = c_ref[...]
    s = c * jax.nn.sigmoid(c)
    o_ref[...] = jnp.dot(s, w_ref[...], preferred_element_type=F32,
                         precision=lax.Precision.HIGHEST) + b_ref[...]


def _modulation(c8, w_mod, b_mod):
    n = w_mod.shape[1]
    tn = 1536
    return pl.pallas_call(
        _mod_kernel,
        grid=(n // tn,),
        in_specs=[_const_spec((8, D_MODEL)),
                  pl.BlockSpec((D_MODEL, tn), lambda j: (0, j)),
                  pl.BlockSpec((1, tn), lambda j: (0, j))],
        out_specs=pl.BlockSpec((8, tn), lambda j: (0, j)),
        out_shape=jax.ShapeDtypeStruct((8, n), F32),
        compiler_params=_params(("parallel",)),
        name="mod",
    )(c8, w_mod, b_mod.reshape(1, n))


def _ctx_kernel(x_ref, mod_ref, nw_ref, w_ref, k_ref, v_ref):
    x = x_ref[0]
    h = _rms_norm(x, nw_ref[...]) * (1.0 + mod_ref[0, 1:2, :]) + mod_ref[0, 0:1, :]
    p = _dot(h.astype(BF16), w_ref[...])
    k_ref[0] = p[:, :QK_WIDTH].astype(BF16)
    v_ref[0] = p[:, QK_WIDTH:].astype(BF16)


def _ctx_proj(ctx, mod3, norm_w, w_kv):
    b, lc, d = ctx.shape
    return pl.pallas_call(
        _ctx_kernel,
        grid=(b,),
        in_specs=[pl.BlockSpec((1, lc, d), lambda i: (i, 0, 0)),
                  pl.BlockSpec((1, N_MOD, d), lambda i: (2, 0, 0)),
                  _const_spec((1, d)),
                  _const_spec(w_kv.shape)],
        out_specs=[pl.BlockSpec((1, lc, QK_WIDTH), lambda i: (i, 0, 0)),
                   pl.BlockSpec((1, lc, V_WIDTH), lambda i: (i, 0, 0))],
        out_shape=[jax.ShapeDtypeStruct((b, lc, QK_WIDTH), BF16),
                   jax.ShapeDtypeStruct((b, lc, V_WIDTH), BF16)],
        compiler_params=_params(("parallel",)),
        name="ctx_proj",
    )(ctx, mod3, norm_w, w_kv)


def _proj_kernel(x_ref, mod_ref, nw_ref, w_ref, cos_ref, sin_ref,
                 q_ref, k_ref, v_ref, sg_ref, f_ref):
    x = x_ref[...]
    h = _rms_norm(x, nw_ref[...]) * (1.0 + mod_ref[0, 1:2, :]) + mod_ref[0, 0:1, :]
    hb = h.astype(BF16)
    cos = cos_ref[...]
    sin = sin_ref[...]

    def rope(t, scale, out_ref):
        for j in range(QK_WIDTH // LANES):
            tj = t[:, j * LANES:(j + 1) * LANES]
            r = tj * cos + pltpu.roll(tj, LANES // 2, 1) * sin
            out_ref[:, j * LANES:(j + 1) * LANES] = (r * scale).astype(BF16)

    rope(_dot(hb, w_ref[:, 0:K_OFF]), QK_DIM ** -0.5, q_ref)
    rope(_dot(hb, w_ref[:, K_OFF:V_OFF]), 1.0, k_ref)
    v_ref[...] = _dot(hb, w_ref[:, V_OFF:G_OFF]).astype(BF16)
    g = _dot(hb, w_ref[:, G_OFF:F_OFF])
    sg_ref[...] = (g * jax.nn.sigmoid(g)).astype(BF16)
    f_ref[...] = _dot(hb, w_ref[:, F_OFF:IN_COLS]).astype(BF16)


def _proj(x2, mod3, norm_w, w_in, cos, sin, seq):
    t, d = x2.shape
    tl = TOKEN_TILE
    tpb = seq // tl
    tok = lambda w: pl.BlockSpec((tl, w), lambda i: (i, 0))
    tab = pl.BlockSpec((tl, LANES), lambda i: (i % tpb, 0))
    return pl.pallas_call(
        _proj_kernel,
        grid=(t // tl,),
        in_specs=[tok(d),
                  pl.BlockSpec((1, N_MOD, d), lambda i: (i // tpb, 0, 0)),
                  _const_spec((1, d)),
                  _const_spec(w_in.shape),
                  tab, tab],
        out_specs=[tok(QK_WIDTH), tok(QK_WIDTH), tok(V_WIDTH), tok(V_WIDTH), tok(F_WIDTH)],
        out_shape=[jax.ShapeDtypeStruct((t, QK_WIDTH), BF16),
                   jax.ShapeDtypeStruct((t, QK_WIDTH), BF16),
                   jax.ShapeDtypeStruct((t, V_WIDTH), BF16),
                   jax.ShapeDtypeStruct((t, V_WIDTH), BF16),
                   jax.ShapeDtypeStruct((t, F_WIDTH), BF16)],
        compiler_params=_params(("parallel",)),
        name="proj",
    )(x2, mod3, norm_w, w_in, cos, sin)


def _tn_dot(a_f32, b):
    return _dot(jnp.transpose(a_f32).astype(BF16), b)


def _retention_kernel(af_ref, ab_ref, q_ref, k_ref, v_ref, sg_ref, kc_ref, vc_ref,
                      o_ref, sf_scr, d_scr, tab_scr):
    c = RET_CHUNK
    seq = q_ref.shape[1]
    lc = kc_ref.shape[1]
    n = seq // c
    head = pl.program_id(1)
    lgf = -jnp.exp(af_ref[0])
    lgb = -jnp.exp(ab_ref[0])

    lane = lax.broadcasted_iota(jnp.int32, (1, LANES), 1)
    mask = (((lane // (QK_DIM // 2)) % 2) == (head % 2)).astype(F32)
    maskb = mask.astype(BF16)

    pos = lax.broadcasted_iota(jnp.int32, (c, LANES), 0).astype(F32)
    tab_scr[0] = jnp.exp(lgf * (pos + 1.0))
    tab_scr[1] = jnp.exp(lgb * (c - pos))
    tab_scr[2] = jnp.exp(lgf * (c - 1.0 - pos)) * mask
    tab_scr[3] = jnp.exp(lgb * pos) * mask
    ri = lax.broadcasted_iota(jnp.int32, (c, c), 0)
    ci = lax.broadcasted_iota(jnp.int32, (c, c), 1)
    diff = (ri - ci).astype(F32)
    lgf_s = lgf[:, 0:1]
    lgb_s = lgb[:, 0:1]
    d_scr[...] = (jnp.where(diff >= 0, jnp.exp(lgf_s * jnp.maximum(diff, 0.0)), 0.0)
                  + jnp.where(diff <= 0, jnp.exp(lgb_s * jnp.maximum(-diff, 0.0)), 0.0))
    chunk_decay_f = jnp.exp(lgf * c)
    chunk_decay_b = jnp.exp(lgb * c)

    cpos = lax.broadcasted_iota(jnp.int32, (lc, LANES), 0).astype(F32)
    kctx = kc_ref[0].astype(F32) * mask
    vctx = vc_ref[0]
    s_f0 = _tn_dot(kctx * jnp.exp(lgf * (lc - 1.0 - cpos)), vctx)
    s_b0 = _tn_dot(kctx * jnp.exp(lgb * cpos), vctx)

    def fwd(i, s_f):
        rows = pl.ds(pl.multiple_of(i * c, c), c)
        sf_scr[i] = s_f.astype(BF16)
        kd = k_ref[0, rows, :].astype(F32) * tab_scr[2]
        return chunk_decay_f * s_f + _tn_dot(kd, v_ref[0, rows, :])

    lax.fori_loop(0, n, fwd, s_f0)

    def bwd(t, s_b):
        i = n - 1 - t
        rows = pl.ds(pl.multiple_of(i * c, c), c)
        q = q_ref[0, rows, :]
        k = k_ref[0, rows, :]
        v = v_ref[0, rows, :]
        scores = lax.dot_general(q, k * maskb, (((1,), (1,)), ((), ())),
                                 preferred_element_type=F32)
        o = _dot((scores * d_scr[...]).astype(BF16), v)
        qf = q.astype(F32)
        o = o + _dot((qf * tab_scr[0]).astype(BF16), sf_scr[i])
        o = o + _dot((qf * tab_scr[1]).astype(BF16), s_b.astype(BF16))
        mu = jnp.mean(o, axis=-1, keepdims=True)
        oc = o - mu
        var = jnp.mean(oc * oc, axis=-1, keepdims=True)
        on = oc * lax.rsqrt(var + EPS)
        o_ref[0, rows, :] = (on * sg_ref[0, rows, :].astype(F32)).astype(BF16)
        kd = k.astype(F32) * tab_scr[3]
        return chunk_decay_b * s_b + _tn_dot(kd, v)

    lax.fori_loop(0, n, bwd, s_b0)


def _retention(a_f, a_b, q, k, v, sg, kc, vc):
    b, seq, _ = v.shape
    lc = kc.shape[1]
    c = RET_CHUNK
    dec = pl.BlockSpec((1, 1, LANES), lambda i, h: (h, 0, 0))
    pair = lambda rows: pl.BlockSpec((1, rows, LANES), lambda i, h: (i, 0, h // 2))
    single = lambda rows: pl.BlockSpec((1, rows, LANES), lambda i, h: (i, 0, h))
    return pl.pallas_call(
        _retention_kernel,
        grid=(b, HEADS),
        in_specs=[dec, dec, pair(seq), pair(seq), single(seq), single(seq),
                  pair(lc), single(lc)],
        out_specs=single(seq),
        out_shape=jax.ShapeDtypeStruct((b, seq, V_WIDTH), BF16),
        scratch_shapes=[pltpu.VMEM((seq // c, LANES, V_DIM), BF16),
                        pltpu.VMEM((c, c), F32),
                        pltpu.VMEM((4, c, LANES), F32)],
        compiler_params=_params(("parallel", "parallel")),
        name="retention",
    )(a_f, a_b, q, k, v, sg, kc, vc)


def _dft_a_kernel(w_ref, x_ref, y_ref):
    y_ref[0] = _dot(w_ref[...].astype(BF16), x_ref[0]).astype(BF16)


def _dft_b_kernel(tw_ref, w2_ref, cc_ref, y_ref, z_ref):
    n2 = y_ref.shape[3]
    cc = cc_ref[...].astype(BF16)
    w2c = w2_ref[0]
    w2s = w2_ref[1]
    for j in range(DFT_K1_PER_STEP):
        tc = tw_ref[j, 0:1, :]
        ts = tw_ref[j, 1:2, :]
        ec = w2c * tc - w2s * ts
        es = w2s * tc + w2c * ts
        m = jnp.concatenate([jnp.concatenate([ec, -es], axis=1),
                             jnp.concatenate([es, ec], axis=1)], axis=0).astype(BF16)
        y = jnp.concatenate([y_ref[0, 0, j], y_ref[0, 1, j]], axis=0)
        zz = _dot(m, y)
        for g in range(F_GROUPS):
            cols = slice(g * F_GROUP_DIM, (g + 1) * F_GROUP_DIM)
            zcs = jnp.concatenate([zz[:n2, cols], zz[n2:, cols]], axis=1).astype(BF16)
            out = _dot(zcs, cc)
            lo = j * F_WIDTH + g * F_GROUP_DIM
            z_ref[0, :, lo:lo + F_GROUP_DIM] = out.astype(BF16)


def _dft_tables(seq):
    n1 = DFT_N1
    n2 = seq // n1

    def cs(num, den):
        ang = 2.0 * np.pi * (num % den) / den
        return np.cos(ang), np.sin(ang)

    a = np.arange(n1)
    w_a = np.concatenate(cs(a[:, None] * a[None, :], n1), axis=0)
    m = np.arange(n2)
    tw = np.stack(cs(a[:, None] * m[None, :], seq), axis=1)
    w2 = np.stack(cs(m[:, None] * m[None, :], n2), axis=0)
    ch = np.arange(F_GROUP_DIM)
    cc, sc = cs(ch[:, None] * ch[None, :], F_GROUP_DIM)
    scale = 1.0 / np.sqrt(seq * F_GROUP_DIM)
    w_c = np.concatenate([cc, -sc], axis=0) * scale
    return [jnp.asarray(t, dtype=F32) for t in (w_a, tw, w2, w_c)]


def _fourier(f):
    b, seq, w = f.shape
    n1 = DFT_N1
    n2 = seq // n1
    w_a, tw, w2, w_c = _dft_tables(seq)
    cols = n2 * w
    tn = 8192
    y = pl.pallas_call(
        _dft_a_kernel,
        grid=(b, cols // tn),
        in_specs=[_const_spec((2 * n1, n1)),
                  pl.BlockSpec((1, n1, tn), lambda i, j: (i, 0, j))],
        out_specs=pl.BlockSpec((1, 2 * n1, tn), lambda i, j: (i, 0, j)),
        out_shape=jax.ShapeDtypeStruct((b, 2 * n1, cols), BF16),
        compiler_params=_params(("parallel", "parallel")),
        name="dft_a",
    )(w_a, f.reshape(b, n1, cols))
    ks = DFT_K1_PER_STEP
    z = pl.pallas_call(
        _dft_b_kernel,
        grid=(b, n1 // ks),
        in_specs=[pl.BlockSpec((ks, 2, n2), lambda i, j: (j, 0, 0)),
                  _const_spec((2, n2, n2)),
                  _const_spec((2 * F_GROUP_DIM, F_GROUP_DIM)),
                  pl.BlockSpec((1, 2, ks, n2, w), lambda i, j: (i, 0, j, 0, 0))],
        out_specs=pl.BlockSpec((1, n2, ks * w), lambda i, j: (i, 0, j)),
        out_shape=jax.ShapeDtypeStruct((b, n2, n1 * w), BF16),
        compiler_params=_params(("parallel", "parallel")),
        name="dft_b",
    )(tw, w2, w_c, y.reshape(b, 2, n1, n2, w))
    return z.reshape(b, seq, w)


def _merge_kernel(x_ref, mod_ref, nw_ref, rg_ref, z_ref, wbg_ref, bbg_ref,
                  wro_ref, wfo_ref, wout_ref, o_ref):
    x = x_ref[...]
    h = _rms_norm(x, nw_ref[...]) * (1.0 + mod_ref[0, 1:2, :]) + mod_ref[0, 0:1, :]
    gates = jax.nn.sigmoid(_dot(h.astype(BF16), wbg_ref[...]) + bbg_ref[...])
    ret_d = _dot(rg_ref[...], wro_ref[...])
    four_d = _dot(z_ref[...], wfo_ref[...])
    m = gates[:, :D_MODEL] * ret_d + gates[:, D_MODEL:] * four_d
    y = _dot(m.astype(BF16), wout_ref[...])
    o_ref[...] = x + mod_ref[0, 2:3, :] * y


def _merge(x2, mod3, norm_w, rg, z, w_bg, b_bg, w_ro, w_fo, w_out, seq):
    t, d = x2.shape
    tl = TOKEN_TILE
    tpb = seq // tl
    tok = lambda w: pl.BlockSpec((tl, w), lambda i: (i, 0))
    return pl.pallas_call(
        _merge_kernel,
        grid=(t // tl,),
        in_specs=[tok(d),
                  pl.BlockSpec((1, N_MOD, d), lambda i: (i // tpb, 0, 0)),
                  _const_spec((1, d)),
                  tok(V_WIDTH), tok(F_WIDTH),
                  _const_spec(w_bg.shape), _const_spec((1, 2 * d)),
                  _const_spec(w_ro.shape), _const_spec(w_fo.shape), _const_spec(w_out.shape)],
        out_specs=tok(d),
        out_shape=jax.ShapeDtypeStruct((t, d), F32),
        compiler_params=_params(("parallel",)),
        name="merge",
    )(x2, mod3, norm_w, rg, z, w_bg, b_bg, w_ro, w_fo, w_out)


HALO = 8


def _gelu_tanh(x):
    cdf = 0.5 * (1.0 + jnp.tanh(np.sqrt(2.0 / np.pi) * (x + 0.044715 * (x * x * x))))
    return x * cdf


def _ffn_kernel(xm_ref, xp_ref, xn_ref, mod_ref, nw_ref, wa_ref, wv_ref, cw_ref, cb_ref,
                wd_ref, fnw_ref, o_ref, h_scr, acc_scr, *, tiles_per_seq):
    tl = xm_ref.shape[0]
    i = pl.program_id(0)
    xm = xm_ref[...]
    xe = jnp.concatenate([xm, xp_ref[...], xn_ref[...]], axis=0)
    h = _rms_norm(xe, nw_ref[...]) * (1.0 + mod_ref[0, 4:5, :]) + mod_ref[0, 3:4, :]
    h_scr[...] = h.astype(BF16)
    acc_scr[...] = jnp.zeros_like(acc_scr)
    keep_prev = ((i % tiles_per_seq) != 0).astype(F32)
    keep_next = ((i % tiles_per_seq) != tiles_per_seq - 1).astype(F32)
    row = lax.broadcasted_iota(jnp.int32, (tl, 1), 0)

    def conv(u, w, bias):
        um = u[0:tl]
        prev = u[tl + HALO - 1:tl + HALO] * keep_prev
        nxt = u[tl + HALO:tl + HALO + 1] * keep_next
        dn = jnp.where(row == 0, prev, pltpu.roll(um, 1, 0))
        up = jnp.where(row == tl - 1, nxt, pltpu.roll(um, tl - 1, 0))
        return dn * w[0:1] + um * w[1:2] + up * w[2:3] + bias

    def body(j, carry):
        hb = h_scr[...]
        cw = cw_ref[j]
        cb = cb_ref[j]
        a = conv(_dot(hb, wa_ref[j]), cw[0], cb[0:1])
        val = conv(_dot(hb, wv_ref[j]), cw[1], cb[1:2])
        act = (_gelu_tanh(a) * val).astype(BF16)
        acc_scr[...] += _dot(act, wd_ref[j])
        return carry

    lax.fori_loop(0, N_FFN_CHUNKS, body, 0)
    x2 = xm + mod_ref[0, 5:6, :] * acc_scr[...]
    o_ref[...] = _rms_norm(x2, fnw_ref[...])


def _ffn(x1, mod3, norm_w, w_a, w_v, cw, cb, w_d, fnorm_w, seq):
    t, d = x1.shape
    tl = TOKEN_TILE
    tpb = seq // tl
    hb = tl // HALO
    last = t // HALO - 1
    return pl.pallas_call(
        functools.partial(_ffn_kernel, tiles_per_seq=tpb),
        grid=(t // tl,),
        in_specs=[pl.BlockSpec((tl, d), lambda i: (i, 0)),
                  pl.BlockSpec((HALO, d), lambda i: (jnp.maximum(i * hb - 1, 0), 0)),
                  pl.BlockSpec((HALO, d), lambda i: (jnp.minimum((i + 1) * hb, last), 0)),
                  pl.BlockSpec((1, N_MOD, d), lambda i: (i // tpb, 0, 0)),
                  _const_spec((1, d)),
                  _const_spec(w_a.shape), _const_spec(w_v.shape),
                  _const_spec(cw.shape), _const_spec(cb.shape),
                  _const_spec(w_d.shape), _const_spec((1, d))],
        out_specs=pl.BlockSpec((tl, d), lambda i: (i, 0)),
        out_shape=jax.ShapeDtypeStruct((t, d), F32),
        scratch_shapes=[pltpu.VMEM((tl + 2 * HALO, d), BF16),
                        pltpu.VMEM((tl, d), F32)],
        compiler_params=_params(("parallel",)),
        name="ffn",
    )(x1, x1, x1, mod3, norm_w, w_a, w_v, cw, cb, w_d, fnorm_w)


def _qk_perm():
    lane = np.arange(QK_WIDTH)
    pair, l = lane // LANES, lane % LANES
    head = 2 * pair + (l // (QK_DIM // 2)) % 2
    half = l // (LANES // 2)
    return head * QK_DIM + half * (QK_DIM // 2) + l % (QK_DIM // 2)


def _rope_tables(seq):
    pos = jnp.arange(seq)
    row = (pos // GRID_W).astype(F32)
    col = (pos % GRID_W).astype(F32)
    n_freq = QK_DIM // 4
    inv = ROPE_BASE ** (-jnp.arange(n_freq, dtype=F32) / n_freq)
    ang = jnp.concatenate([row[:, None] * inv, col[:, None] * inv], axis=-1)
    ang = jnp.tile(ang, (1, LANES // (QK_DIM // 2)))
    sign = jnp.where(jnp.arange(LANES) < LANES // 2, -1.0, 1.0).astype(F32)
    return jnp.cos(ang), jnp.sin(ang) * sign


def kernel(x, c, ctx, c_ctx, w_mod, b_mod, norm1_w, w_in, ret_decay_f, ret_decay_b,
           w_ret_out, w_four_out, w_branch_gate, b_branch_gate, w_out, norm2_w,
           w_up, conv_w, conv_b, w_down, final_norm_w):
    assert w_mod.shape[0] == 1, "single-layer block"
    b, seq, d = x.shape
    t = b * seq
    assert b == 2 and seq % TOKEN_TILE == 0 and seq % RET_CHUNK == 0

    c8 = jnp.concatenate([c, c_ctx[None, :], jnp.zeros((8 - b - 1, d), F32)], axis=0)
    mod3 = _modulation(c8, w_mod[0], b_mod[0]).reshape(8, N_MOD, d)

    perm = _qk_perm()
    wi = w_in[0]
    w_in_p = jnp.concatenate([wi[:, perm], wi[:, K_OFF + perm], wi[:, V_OFF:]], axis=1).astype(BF16)
    n1w = norm1_w[0].reshape(1, d)

    kc, vc = _ctx_proj(ctx, mod3, n1w, w_in_p[:, K_OFF:G_OFF])
    cos, sin = _rope_tables(seq)
    x2 = x.reshape(t, d)
    q, k, v, sg, f = _proj(x2, mod3, n1w, w_in_p, cos, sin, seq)

    a_f = jnp.broadcast_to(ret_decay_f[0][:, None, None], (HEADS, 1, LANES))
    a_b = jnp.broadcast_to(ret_decay_b[0][:, None, None], (HEADS, 1, LANES))
    rg = _retention(a_f, a_b, q.reshape(b, seq, QK_WIDTH), k.reshape(b, seq, QK_WIDTH),
                    v.reshape(b, seq, V_WIDTH), sg.reshape(b, seq, V_WIDTH), kc, vc)
    z = _fourier(f.reshape(b, seq, F_WIDTH))

    x1 = _merge(x2, mod3, n1w, rg.reshape(t, V_WIDTH), z.reshape(t, F_WIDTH),
                w_branch_gate[0].astype(BF16), b_branch_gate[0].reshape(1, 2 * d),
                w_ret_out[0].astype(BF16), w_four_out[0].astype(BF16), w_out[0].astype(BF16), seq)

    nc = N_FFN_CHUNKS
    wu = w_up[0].astype(BF16)
    w_a = wu[:, :FFN_DIM].reshape(d, nc, FFN_CHUNK).transpose(1, 0, 2)
    w_v = wu[:, FFN_DIM:].reshape(d, nc, FFN_CHUNK).transpose(1, 0, 2)
    cw = conv_w[0].reshape(3, 2, nc, FFN_CHUNK).transpose(2, 1, 0, 3)
    cb = conv_b[0].reshape(2, nc, FFN_CHUNK).transpose(1, 0, 2)
    w_d = w_down[0].astype(BF16).reshape(nc, FFN_CHUNK, d)
    out = _ffn(x1, mod3, norm2_w[0].reshape(1, d), w_a, w_v, cw, cb, w_d,
               final_norm_w.reshape(1, d), seq)
    return out.reshape(b, seq, d)
```

```python
import functools

import numpy as np
import jax
import jax.numpy as jnp
from jax import lax
from jax.experimental import pallas as pl
from jax.experimental.pallas import tpu as pltpu

F32 = jnp.float32
BF16 = jnp.bfloat16

D_MODEL = 1024
GRID_W = 64
HEADS = 8
QK_DIM = 64
V_DIM = 128
QK_WIDTH = HEADS * QK_DIM
V_WIDTH = HEADS * V_DIM
ROPE_BASE = 10000.0
F_GROUPS = 4
F_GROUP_DIM = 128
F_WIDTH = F_GROUPS * F_GROUP_DIM
K_OFF = QK_WIDTH
V_OFF = K_OFF + QK_WIDTH
G_OFF = V_OFF + V_WIDTH
F_OFF = G_OFF + V_WIDTH
IN_COLS = F_OFF + F_WIDTH
FFN_DIM = 2816
N_MOD = 6
EPS = 1e-6

LANES = 128
RET_CHUNK = 256
FFN_CHUNK = 256
N_FFN_CHUNKS = FFN_DIM // FFN_CHUNK
TOKEN_TILE = 512
DFT_N1 = 64
DFT_K1_PER_STEP = 8
VMEM_LIMIT = 56 * 1024 * 1024


def _params(sem):
    return pltpu.CompilerParams(dimension_semantics=sem, vmem_limit_bytes=VMEM_LIMIT)


def _dot(a, b):
    return jnp.dot(a, b, preferred_element_type=F32)


def _rms_norm(x, w):
    return x * lax.rsqrt(jnp.mean(x * x, axis=-1, keepdims=True) + EPS) * w


def _const_spec(shape):
    zeros = (0,) * len(shape)
    return pl.BlockSpec(shape, lambda *_: zeros)


def _mod_kernel(c_ref, w_ref, b_ref, o_ref):
    c = c_ref[...]
    s = c * jax.nn.sigmoid(c)
    o_ref[...] = jnp.dot(s, w_ref[...], preferred_element_type=F32,
                         precision=lax.Precision.HIGHEST) + b_ref[...]


def _modulation(c8, w_mod, b_mod):
    n = w_mod.shape[1]
    tn = 1536
    return pl.pallas_call(
        _mod_kernel,
        grid=(n // tn,),
        in_specs=[_const_spec((8, D_MODEL)),
                  pl.BlockSpec((D_MODEL, tn), lambda j: (0, j)),
                  pl.BlockSpec((1, tn), lambda j: (0, j))],
        out_specs=pl.BlockSpec((8, tn), lambda j: (0, j)),
        out_shape=jax.ShapeDtypeStruct((8, n), F32),
        compiler_params=_params(("parallel",)),
        name="mod",
    )(c8, w_mod, b_mod.reshape(1, n))


def _ctx_kernel(x_ref, mod_ref, nw_ref, w_ref, k_ref, v_ref):
    x = x_ref[0]
    h = _rms_norm(x, nw_ref[...]) * (1.0 + mod_ref[0, 1:2, :]) + mod_ref[0, 0:1, :]
    p = _dot(h.astype(BF16), w_ref[...])
    k_ref[0] = p[:, :QK_WIDTH].astype(BF16)
    v_ref[0] = p[:, QK_WIDTH:].astype(BF16)


def _ctx_proj(ctx, mod3, norm_w, w_kv):
    b, lc, d = ctx.shape
    return pl.pallas_call(
        _ctx_kernel,
        grid=(b,),
        in_specs=[pl.BlockSpec((1, lc, d), lambda i: (i, 0, 0)),
                  pl.BlockSpec((1, N_MOD, d), lambda i: (2, 0, 0)),
                  _const_spec((1, d)),
                  _const_spec(w_kv.shape)],
        out_specs=[pl.BlockSpec((1, lc, QK_WIDTH), lambda i: (i, 0, 0)),
                   pl.BlockSpec((1, lc, V_WIDTH), lambda i: (i, 0, 0))],
        out_shape=[jax.ShapeDtypeStruct((b, lc, QK_WIDTH), BF16),
                   jax.ShapeDtypeStruct((b, lc, V_WIDTH), BF16)],
        compiler_params=_params(("parallel",)),
        name="ctx_proj",
    )(ctx, mod3, norm_w, w_kv)


def _proj_kernel(x_ref, mod_ref, nw_ref, w_ref, cos_ref, sin_ref,
                 q_ref, k_ref, v_ref, sg_ref, f_ref):
    x = x_ref[...]
    h = _rms_norm(x, nw_ref[...]) * (1.0 + mod_ref[0, 1:2, :]) + mod_ref[0, 0:1, :]
    hb = h.astype(BF16)
    cos = cos_ref[...]
    sin = sin_ref[...]

    def rope(t, scale, out_ref):
        for j in range(QK_WIDTH // LANES):
            tj = t[:, j * LANES:(j + 1) * LANES]
            r = tj * cos + pltpu.roll(tj, LANES // 2, 1) * sin
            out_ref[:, j * LANES:(j + 1) * LANES] = (r * scale).astype(BF16)

    rope(_dot(hb, w_ref[:, 0:K_OFF]), QK_DIM ** -0.5, q_ref)
    rope(_dot(hb, w_ref[:, K_OFF:V_OFF]), 1.0, k_ref)
    v_ref[...] = _dot(hb, w_ref[:, V_OFF:G_OFF]).astype(BF16)
    g = _dot(hb, w_ref[:, G_OFF:F_OFF])
    sg_ref[...] = (g * jax.nn.sigmoid(g)).astype(BF16)
    f_ref[...] = _dot(hb, w_ref[:, F_OFF:IN_COLS]).astype(BF16)


def _proj(x2, mod3, norm_w, w_in, cos, sin, seq):
    t, d = x2.shape
    tl = TOKEN_TILE
    tpb = seq // tl
    tok = lambda w: pl.BlockSpec((tl, w), lambda i: (i, 0))
    tab = pl.BlockSpec((tl, LANES), lambda i: (i % tpb, 0))
    return pl.pallas_call(
        _proj_kernel,
        grid=(t // tl,),
        in_specs=[tok(d),
                  pl.BlockSpec((1, N_MOD, d), lambda i: (i // tpb, 0, 0)),
                  _const_spec((1, d)),
                  _const_spec(w_in.shape),
                  tab, tab],
        out_specs=[tok(QK_WIDTH), tok(QK_WIDTH), tok(V_WIDTH), tok(V_WIDTH), tok(F_WIDTH)],
        out_shape=[jax.ShapeDtypeStruct((t, QK_WIDTH), BF16),
                   jax.ShapeDtypeStruct((t, QK_WIDTH), BF16),
                   jax.ShapeDtypeStruct((t, V_WIDTH), BF16),
                   jax.ShapeDtypeStruct((t, V_WIDTH), BF16),
                   jax.ShapeDtypeStruct((t, F_WIDTH), BF16)],
        compiler_params=_params(("parallel",)),
        name="proj",
    )(x2, mod3, norm_w, w_in, cos, sin)


def _retention_kernel(af_ref, ab_ref, q_ref, k_ref, v_ref, sg_ref, kc_ref, vc_ref,
                      o_ref, ds_scr, st_scr, d_scr, qtab_scr):
    c = RET_CHUNK
    seq = q_ref.shape[1]
    lc = kc_ref.shape[1]
    n = seq // c
    lane = lax.broadcasted_iota(jnp.int32, (1, LANES), 1)
    pos = lax.broadcasted_iota(jnp.int32, (c, LANES), 0).astype(F32)
    posl = lax.broadcasted_iota(jnp.int32, (1, c), 1).astype(F32)
    cposl = lax.broadcasted_iota(jnp.int32, (1, lc), 1).astype(F32)
    diff = (lax.broadcasted_iota(jnp.int32, (c, c), 0)
            - lax.broadcasted_iota(jnp.int32, (c, c), 1)).astype(F32)

    masks, kdec, kdec_ctx, chunk_decay = [], [], [], []
    for hh in range(2):
        lgf = -jnp.exp(af_ref[hh])
        lgb = -jnp.exp(ab_ref[hh])
        lgf_s, lgb_s = lgf[:, 0:1], lgb[:, 0:1]
        mask = (((lane // (QK_DIM // 2)) % 2) == hh).astype(F32)
        masks.append(mask.astype(BF16))
        qtab_scr[hh, 0] = jnp.exp(lgf * (pos + 1.0)) * mask
        qtab_scr[hh, 1] = jnp.exp(lgb * (c - pos)) * mask
        d_scr[hh] = (jnp.where(diff >= 0, jnp.exp(lgf_s * jnp.maximum(diff, 0.0)), 0.0)
                     + jnp.where(diff <= 0, jnp.exp(lgb_s * jnp.maximum(-diff, 0.0)), 0.0))
        kdec.append((jnp.exp(lgf_s * (c - 1.0 - posl)), jnp.exp(lgb_s * posl)))
        kdec_ctx.append((jnp.exp(lgf_s * (lc - 1.0 - cposl)), jnp.exp(lgb_s * cposl)))
        chunk_decay.append((jnp.exp(lgf * c), jnp.exp(lgb * c)))

    def state_increments(k_rows, v_rows, decays):
        kt = jnp.transpose(k_rows.astype(F32))
        out = []
        for hh in range(2):
            lhs = jnp.concatenate([kt * decays[hh][0], kt * decays[hh][1]], axis=0).astype(BF16)
            out.append(_dot(lhs, v_rows[:, hh * V_DIM:(hh + 1) * V_DIM]))
        return out

    def incr(i, carry):
        rows = pl.ds(pl.multiple_of(i * c, c), c)
        ds = state_increments(k_ref[0, rows, :], v_ref[0, rows, :], kdec)
        ds_scr[i, 0] = ds[0]
        ds_scr[i, 1] = ds[1]
        return carry

    lax.fori_loop(0, n, incr, 0, unroll=2)

    s0 = state_increments(kc_ref[0], vc_ref[0], kdec_ctx)

    def scan_f(i, s):
        out = []
        for hh in range(2):
            st_scr[i, hh, 0:LANES, :] = s[hh].astype(BF16)
            out.append(chunk_decay[hh][0] * s[hh] + ds_scr[i, hh, 0:LANES, :])
        return tuple(out)

    lax.fori_loop(0, n, scan_f, (s0[0][0:LANES], s0[1][0:LANES]))

    def scan_b(t, s):
        i = n - 1 - t
        out = []
        for hh in range(2):
            st_scr[i, hh, LANES:2 * LANES, :] = s[hh].astype(BF16)
            out.append(chunk_decay[hh][1] * s[hh] + ds_scr[i, hh, LANES:2 * LANES, :])
        return tuple(out)

    lax.fori_loop(0, n, scan_b, (s0[0][LANES:2 * LANES], s0[1][LANES:2 * LANES]))

    def outputs(i, carry):
        rows = pl.ds(pl.multiple_of(i * c, c), c)
        q = q_ref[0, rows, :]
        k = k_ref[0, rows, :]
        qf = q.astype(F32)
        for hh in range(2):
            cols = slice(hh * V_DIM, (hh + 1) * V_DIM)
            v = v_ref[0, rows, cols]
            scores = lax.dot_general(q * masks[hh], k, (((1,), (1,)), ((), ())),
                                     preferred_element_type=F32)
            o = _dot((scores * d_scr[hh]).astype(BF16), v)
            qd = jnp.concatenate([qf * qtab_scr[hh, 0], qf * qtab_scr[hh, 1]], axis=1).astype(BF16)
            o = o + _dot(qd, st_scr[i, hh])
            mu = jnp.mean(o, axis=-1, keepdims=True)
            oc = o - mu
            var = jnp.mean(oc * oc, axis=-1, keepdims=True)
            on = oc * lax.rsqrt(var + EPS)
            o_ref[0, rows, cols] = (on * sg_ref[0, rows, cols].astype(F32)).astype(BF16)
        return carry

    lax.fori_loop(0, n, outputs, 0, unroll=2)


def _retention(a_f, a_b, q, k, v, sg, kc, vc):
    b, seq, _ = v.shape
    lc = kc.shape[1]
    c = RET_CHUNK
    n = seq // c
    dec = pl.BlockSpec((2, 1, LANES), lambda i, p: (p, 0, 0))
    qk = lambda rows: pl.BlockSpec((1, rows, LANES), lambda i, p: (i, 0, p))
    vv = lambda rows: pl.BlockSpec((1, rows, 2 * V_DIM), lambda i, p: (i, 0, p))
    return pl.pallas_call(
        _retention_kernel,
        grid=(b, HEADS // 2),
        in_specs=[dec, dec, qk(seq), qk(seq), vv(seq), vv(seq), qk(lc), vv(lc)],
        out_specs=vv(seq),
        out_shape=jax.ShapeDtypeStruct((b, seq, V_WIDTH), BF16),
        scratch_shapes=[pltpu.VMEM((n, 2, 2 * LANES, V_DIM), F32),
                        pltpu.VMEM((n, 2, 2 * LANES, V_DIM), BF16),
                        pltpu.VMEM((2, c, c), F32),
                        pltpu.VMEM((2, 2, c, LANES), F32)],
        compiler_params=_params(("parallel", "parallel")),
        name="retention",
    )(a_f, a_b, q, k, v, sg, kc, vc)


def _dft_a_kernel(w_ref, x_ref, y_ref):
    y_ref[0] = _dot(w_ref[...].astype(BF16), x_ref[0]).astype(BF16)


def _dft_b_kernel(tw_ref, w2_ref, cc_ref, y_ref, z_ref):
    n2 = y_ref.shape[3]
    cc = cc_ref[...].astype(BF16)
    w2c = w2_ref[0]
    w2s = w2_ref[1]
    for j in range(DFT_K1_PER_STEP):
        tc = tw_ref[j, 0:1, :]
        ts = tw_ref[j, 1:2, :]
        ec = w2c * tc - w2s * ts
        es = w2s * tc + w2c * ts
        m = jnp.concatenate([jnp.concatenate([ec, -es], axis=1),
                             jnp.concatenate([es, ec], axis=1)], axis=0).astype(BF16)
        y = jnp.concatenate([y_ref[0, 0, j], y_ref[0, 1, j]], axis=0)
        zz = _dot(m, y)
        for g in range(F_GROUPS):
            cols = slice(g * F_GROUP_DIM, (g + 1) * F_GROUP_DIM)
            zcs = jnp.concatenate([zz[:n2, cols], zz[n2:, cols]], axis=1).astype(BF16)
            out = _dot(zcs, cc)
            lo = j * F_WIDTH + g * F_GROUP_DIM
            z_ref[0, :, lo:lo + F_GROUP_DIM] = out.astype(BF16)


def _dft_tables(seq):
    n1 = DFT_N1
    n2 = seq // n1

    def cs(num, den):
        ang = 2.0 * np.pi * (num % den) / den
        return np.cos(ang), np.sin(ang)

    a = np.arange(n1)
    w_a = np.concatenate(cs(a[:, None] * a[None, :], n1), axis=0)
    m = np.arange(n2)
    tw = np.stack(cs(a[:, None] * m[None, :], seq), axis=1)
    w2 = np.stack(cs(m[:, None] * m[None, :], n2), axis=0)
    ch = np.arange(F_GROUP_DIM)
    cc, sc = cs(ch[:, None] * ch[None, :], F_GROUP_DIM)
    scale = 1.0 / np.sqrt(seq * F_GROUP_DIM)
    w_c = np.concatenate([cc, -sc], axis=0) * scale
    return [jnp.asarray(t, dtype=F32) for t in (w_a, tw, w2, w_c)]


def _fourier(f):
    b, seq, w = f.shape
    n1 = DFT_N1
    n2 = seq // n1
    w_a, tw, w2, w_c = _dft_tables(seq)
    cols = n2 * w
    tn = 8192
    y = pl.pallas_call(
        _dft_a_kernel,
        grid=(b, cols // tn),
        in_specs=[_const_spec((2 * n1, n1)),
                  pl.BlockSpec((1, n1, tn), lambda i, j: (i, 0, j))],
        out_specs=pl.BlockSpec((1, 2 * n1, tn), lambda i, j: (i, 0, j)),
        out_shape=jax.ShapeDtypeStruct((b, 2 * n1, cols), BF16),
        compiler_params=_params(("parallel", "parallel")),
        name="dft_a",
    )(w_a, f.reshape(b, n1, cols))
    ks = DFT_K1_PER_STEP
    z = pl.pallas_call(
        _dft_b_kernel,
        grid=(b, n1 // ks),
        in_specs=[pl.BlockSpec((ks, 2, n2), lambda i, j: (j, 0, 0)),
                  _const_spec((2, n2, n2)),
                  _const_spec((2 * F_GROUP_DIM, F_GROUP_DIM)),
                  pl.BlockSpec((1, 2, ks, n2, w), lambda i, j: (i, 0, j, 0, 0))],
        out_specs=pl.BlockSpec((1, n2, ks * w), lambda i, j: (i, 0, j)),
        out_shape=jax.ShapeDtypeStruct((b, n2, n1 * w), BF16),
        compiler_params=_params(("parallel", "parallel")),
        name="dft_b",
    )(tw, w2, w_c, y.reshape(b, 2, n1, n2, w))
    return z.reshape(b, seq, w)


def _merge_kernel(x_ref, mod_ref, nw_ref, rg_ref, z_ref, wbg_ref, bbg_ref,
                  wro_ref, wfo_ref, wout_ref, o_ref):
    x = x_ref[...]
    h = _rms_norm(x, nw_ref[...]) * (1.0 + mod_ref[0, 1:2, :]) + mod_ref[0, 0:1, :]
    gates = jax.nn.sigmoid(_dot(h.astype(BF16), wbg_ref[...]) + bbg_ref[...])
    ret_d = _dot(rg_ref[...], wro_ref[...])
    four_d = _dot(z_ref[...], wfo_ref[...])
    m = gates[:, :D_MODEL] * ret_d + gates[:, D_MODEL:] * four_d
    y = _dot(m.astype(BF16), wout_ref[...])
    o_ref[...] = x + mod_ref[0, 2:3, :] * y


def _merge(x2, mod3, norm_w, rg, z, w_bg, b_bg, w_ro, w_fo, w_out, seq):
    t, d = x2.shape
    tl = TOKEN_TILE
    tpb = seq // tl
    tok = lambda w: pl.BlockSpec((tl, w), lambda i: (i, 0))
    return pl.pallas_call(
        _merge_kernel,
        grid=(t // tl,),
        in_specs=[tok(d),
                  pl.BlockSpec((1, N_MOD, d), lambda i: (i // tpb, 0, 0)),
                  _const_spec((1, d)),
                  tok(V_WIDTH), tok(F_WIDTH),
                  _const_spec(w_bg.shape), _const_spec((1, 2 * d)),
                  _const_spec(w_ro.shape), _const_spec(w_fo.shape), _const_spec(w_out.shape)],
        out_specs=tok(d),
        out_shape=jax.ShapeDtypeStruct((t, d), F32),
        compiler_params=_params(("parallel",)),
        name="merge",
    )(x2, mod3, norm_w, rg, z, w_bg, b_bg, w_ro, w_fo, w_out)


HALO = 8


def _gelu_tanh(x):
    cdf = 0.5 * (1.0 + jnp.tanh(np.sqrt(2.0 / np.pi) * (x + 0.044715 * (x * x * x))))
    return x * cdf


def _ffn_kernel(xm_ref, xp_ref, xn_ref, mod_ref, nw_ref, wa_ref, wv_ref, cw_ref, cb_ref,
                wd_ref, fnw_ref, o_ref, h_scr, acc_scr, *, tiles_per_seq):
    tl = xm_ref.shape[0]
    i = pl.program_id(0)
    xm = xm_ref[...]
    xe = jnp.concatenate([xm, xp_ref[...], xn_ref[...]], axis=0)
    h = _rms_norm(xe, nw_ref[...]) * (1.0 + mod_ref[0, 4:5, :]) + mod_ref[0, 3:4, :]
    h_scr[...] = h.astype(BF16)
    acc_scr[...] = jnp.zeros_like(acc_scr)
    keep_prev = ((i % tiles_per_seq) != 0).astype(F32)
    keep_next = ((i % tiles_per_seq) != tiles_per_seq - 1).astype(F32)
    row = lax.broadcasted_iota(jnp.int32, (tl, 1), 0)

    def conv(u, w, bias):
        um = u[0:tl]
        prev = u[tl + HALO - 1:tl + HALO] * keep_prev
        nxt = u[tl + HALO:tl + HALO + 1] * keep_next
        dn = jnp.where(row == 0, prev, pltpu.roll(um, 1, 0))
        up = jnp.where(row == tl - 1, nxt, pltpu.roll(um, tl - 1, 0))
        return dn * w[0:1] + um * w[1:2] + up * w[2:3] + bias

    def body(j, carry):
        hb = h_scr[...]
        cw = cw_ref[j]
        cb = cb_ref[j]
        a = conv(_dot(hb, wa_ref[j]), cw[0], cb[0:1])
        val = conv(_dot(hb, wv_ref[j]), cw[1], cb[1:2])
        act = (_gelu_tanh(a) * val).astype(BF16)
        acc_scr[...] += _dot(act, wd_ref[j])
        return carry

    lax.fori_loop(0, N_FFN_CHUNKS, body, 0)
    x2 = xm + mod_ref[0, 5:6, :] * acc_scr[...]
    o_ref[...] = _rms_norm(x2, fnw_ref[...])


def _ffn(x1, mod3, norm_w, w_a, w_v, cw, cb, w_d, fnorm_w, seq):
    t, d = x1.shape
    tl = TOKEN_TILE
    tpb = seq // tl
    hb = tl // HALO
    last = t // HALO - 1
    return pl.pallas_call(
        functools.partial(_ffn_kernel, tiles_per_seq=tpb),
        grid=(t // tl,),
        in_specs=[pl.BlockSpec((tl, d), lambda i: (i, 0)),
                  pl.BlockSpec((HALO, d), lambda i: (jnp.maximum(i * hb - 1, 0), 0)),
                  pl.BlockSpec((HALO, d), lambda i: (jnp.minimum((i + 1) * hb, last), 0)),
                  pl.BlockSpec((1, N_MOD, d), lambda i: (i // tpb, 0, 0)),
                  _const_spec((1, d)),
                  _const_spec(w_a.shape), _const_spec(w_v.shape),
                  _const_spec(cw.shape), _const_spec(cb.shape),
                  _const_spec(w_d.shape), _const_spec((1, d))],
        out_specs=pl.BlockSpec((tl, d), lambda i: (i, 0)),
        out_shape=jax.ShapeDtypeStruct((t, d), F32),
        scratch_shapes=[pltpu.VMEM((tl + 2 * HALO, d), BF16),
                        pltpu.VMEM((tl, d), F32)],
        compiler_params=_params(("parallel",)),
        name="ffn",
    )(x1, x1, x1, mod3, norm_w, w_a, w_v, cw, cb, w_d, fnorm_w)


def _qk_perm():
    lane = np.arange(QK_WIDTH)
    pair, l = lane // LANES, lane % LANES
    head = 2 * pair + (l // (QK_DIM // 2)) % 2
    half = l // (LANES // 2)
    return head * QK_DIM + half * (QK_DIM // 2) + l % (QK_DIM // 2)


def _rope_tables(seq):
    pos = jnp.arange(seq)
    row = (pos // GRID_W).astype(F32)
    col = (pos % GRID_W).astype(F32)
    n_freq = QK_DIM // 4
    inv = ROPE_BASE ** (-jnp.arange(n_freq, dtype=F32) / n_freq)
    ang = jnp.concatenate([row[:, None] * inv, col[:, None] * inv], axis=-1)
    ang = jnp.tile(ang, (1, LANES // (QK_DIM // 2)))
    sign = jnp.where(jnp.arange(LANES) < LANES // 2, -1.0, 1.0).astype(F32)
    return jnp.cos(ang), jnp.sin(ang) * sign


def kernel(x, c, ctx, c_ctx, w_mod, b_mod, norm1_w, w_in, ret_decay_f, ret_decay_b,
           w_ret_out, w_four_out, w_branch_gate, b_branch_gate, w_out, norm2_w,
           w_up, conv_w, conv_b, w_down, final_norm_w):
    assert w_mod.shape[0] == 1, "single-layer block"
    b, seq, d = x.shape
    t = b * seq
    assert b == 2 and seq % TOKEN_TILE == 0 and seq % RET_CHUNK == 0

    c8 = jnp.concatenate([c, c_ctx[None, :], jnp.zeros((8 - b - 1, d), F32)], axis=0)
    mod3 = _modulation(c8, w_mod[0], b_mod[0]).reshape(8, N_MOD, d)

    perm = _qk_perm()
    wi = w_in[0]
    w_in_p = jnp.concatenate([wi[:, perm], wi[:, K_OFF + perm], wi[:, V_OFF:]], axis=1).astype(BF16)
    n1w = norm1_w[0].reshape(1, d)

    kc, vc = _ctx_proj(ctx, mod3, n1w, w_in_p[:, K_OFF:G_OFF])
    cos, sin = _rope_tables(seq)
    x2 = x.reshape(t, d)
    q, k, v, sg, f = _proj(x2, mod3, n1w, w_in_p, cos, sin, seq)

    a_f = jnp.broadcast_to(ret_decay_f[0][:, None, None], (HEADS, 1, LANES))
    a_b = jnp.broadcast_to(ret_decay_b[0][:, None, None], (HEADS, 1, LANES))
    rg = _retention(a_f, a_b, q.reshape(b, seq, QK_WIDTH), k.reshape(b, seq, QK_WIDTH),
                    v.reshape(b, seq, V_WIDTH), sg.reshape(b, seq, V_WIDTH), kc, vc)
    z = _fourier(f.reshape(b, seq, F_WIDTH))

    x1 = _merge(x2, mod3, n1w, rg.reshape(t, V_WIDTH), z.reshape(t, F_WIDTH),
                w_branch_gate[0].astype(BF16), b_branch_gate[0].reshape(1, 2 * d),
                w_ret_out[0].astype(BF16), w_four_out[0].astype(BF16), w_out[0].astype(BF16), seq)

    nc = N_FFN_CHUNKS
    wu = w_up[0].astype(BF16)
    w_a = wu[:, :FFN_DIM].reshape(d, nc, FFN_CHUNK).transpose(1, 0, 2)
    w_v = wu[:, FFN_DIM:].reshape(d, nc, FFN_CHUNK).transpose(1, 0, 2)
    cw = conv_w[0].reshape(3, 2, nc, FFN_CHUNK).transpose(2, 1, 0, 3)
    cb = conv_b[0].reshape(2, nc, FFN_CHUNK).transpose(1, 0, 2)
    w_d = w_down[0].astype(BF16).reshape(nc, FFN_CHUNK, d)
    out = _ffn(x1, mod3, norm2_w[0].reshape(1, d), w_a, w_v, cw, cb, w_d,
               final_norm_w.reshape(1, d), seq)
    return out.reshape(b, seq, d)
```

```python
import functools

import numpy as np
import jax
import jax.numpy as jnp
from jax import lax
from jax.experimental import pallas as pl
from jax.experimental.pallas import tpu as pltpu

F32 = jnp.float32
BF16 = jnp.bfloat16

D_MODEL = 1024
GRID_W = 64
HEADS = 8
QK_DIM = 64
V_DIM = 128
QK_WIDTH = HEADS * QK_DIM
V_WIDTH = HEADS * V_DIM
ROPE_BASE = 10000.0
F_GROUPS = 4
F_GROUP_DIM = 128
F_WIDTH = F_GROUPS * F_GROUP_DIM
K_OFF = QK_WIDTH
V_OFF = K_OFF + QK_WIDTH
G_OFF = V_OFF + V_WIDTH
F_OFF = G_OFF + V_WIDTH
IN_COLS = F_OFF + F_WIDTH
FFN_DIM = 2816
N_MOD = 6
EPS = 1e-6

LANES = 128
RET_CHUNK = 256
FFN_CHUNK = 256
N_FFN_CHUNKS = FFN_DIM // FFN_CHUNK
TOKEN_TILE = 512
DFT_N1 = 64
DFT_K1_PER_STEP = 8
VMEM_LIMIT = 56 * 1024 * 1024


def _params(sem):
    return pltpu.CompilerParams(dimension_semantics=sem, vmem_limit_bytes=VMEM_LIMIT)


def _dot(a, b):
    return jnp.dot(a, b, preferred_element_type=F32)


def _rms_norm(x, w):
    return x * lax.rsqrt(jnp.mean(x * x, axis=-1, keepdims=True) + EPS) * w


def _const_spec(shape):
    zeros = (0,) * len(shape)
    return pl.BlockSpec(shape, lambda *_: zeros)


def _mod_kernel(c_ref, w_ref, b_ref, o_ref):
    c = c_ref[...]
    s = c * jax.nn.sigmoid(c)
    o_ref[...] = jnp.dot(s, w_ref[...], preferred_element_type=F32,
                         precision=lax.Precision.HIGHEST) + b_ref[...]


def _modulation(c8, w_mod, b_mod):
    n = w_mod.shape[1]
    tn = 1536
    return pl.pallas_call(
        _mod_kernel,
        grid=(n // tn,),
        in_specs=[_const_spec((8, D_MODEL)),
                  pl.BlockSpec((D_MODEL, tn), lambda j: (0, j)),
                  pl.BlockSpec((1, tn), lambda j: (0, j))],
        out_specs=pl.BlockSpec((8, tn), lambda j: (0, j)),
        out_shape=jax.ShapeDtypeStruct((8, n), F32),
        compiler_params=_params(("parallel",)),
        name="mod",
    )(c8, w_mod, b_mod.reshape(1, n))


def _ctx_kernel(x_ref, mod_ref, nw_ref, w_ref, k_ref, v_ref):
    x = x_ref[0]
    h = _rms_norm(x, nw_ref[...]) * (1.0 + mod_ref[0, 1:2, :]) + mod_ref[0, 0:1, :]
    p = _dot(h.astype(BF16), w_ref[...])
    k_ref[0] = p[:, :QK_WIDTH].astype(BF16)
    v_ref[0] = p[:, QK_WIDTH:].astype(BF16)


def _ctx_proj(ctx, mod3, norm_w, w_kv):
    b, lc, d = ctx.shape
    return pl.pallas_call(
        _ctx_kernel,
        grid=(b,),
        in_specs=[pl.BlockSpec((1, lc, d), lambda i: (i, 0, 0)),
                  pl.BlockSpec((1, N_MOD, d), lambda i: (2, 0, 0)),
                  _const_spec((1, d)),
                  _const_spec(w_kv.shape)],
        out_specs=[pl.BlockSpec((1, lc, QK_WIDTH), lambda i: (i, 0, 0)),
                   pl.BlockSpec((1, lc, V_WIDTH), lambda i: (i, 0, 0))],
        out_shape=[jax.ShapeDtypeStruct((b, lc, QK_WIDTH), BF16),
                   jax.ShapeDtypeStruct((b, lc, V_WIDTH), BF16)],
        compiler_params=_params(("parallel",)),
        name="ctx_proj",
    )(ctx, mod3, norm_w, w_kv)


def _proj_kernel(x_ref, mod_ref, nw_ref, w_ref, cos_ref, sin_ref,
                 q_ref, k_ref, v_ref, sg_ref, f_ref):
    x = x_ref[...]
    h = _rms_norm(x, nw_ref[...]) * (1.0 + mod_ref[0, 1:2, :]) + mod_ref[0, 0:1, :]
    hb = h.astype(BF16)
    cos = cos_ref[...]
    sin = sin_ref[...]

    def rope(t, scale, out_ref):
        for j in range(QK_WIDTH // LANES):
            tj = t[:, j * LANES:(j + 1) * LANES]
            r = tj * cos + pltpu.roll(tj, LANES // 2, 1) * sin
            out_ref[:, j * LANES:(j + 1) * LANES] = (r * scale).astype(BF16)

    rope(_dot(hb, w_ref[:, 0:K_OFF]), QK_DIM ** -0.5, q_ref)
    rope(_dot(hb, w_ref[:, K_OFF:V_OFF]), 1.0, k_ref)
    v_ref[...] = _dot(hb, w_ref[:, V_OFF:G_OFF]).astype(BF16)
    g = _dot(hb, w_ref[:, G_OFF:F_OFF])
    sg_ref[...] = (g * jax.nn.sigmoid(g)).astype(BF16)
    f_ref[...] = _dot(hb, w_ref[:, F_OFF:IN_COLS]).astype(BF16)


def _proj(x2, mod3, norm_w, w_in, cos, sin, seq):
    t, d = x2.shape
    tl = TOKEN_TILE
    tpb = seq // tl
    tok = lambda w: pl.BlockSpec((tl, w), lambda i: (i, 0))
    tab = pl.BlockSpec((tl, LANES), lambda i: (i % tpb, 0))
    return pl.pallas_call(
        _proj_kernel,
        grid=(t // tl,),
        in_specs=[tok(d),
                  pl.BlockSpec((1, N_MOD, d), lambda i: (i // tpb, 0, 0)),
                  _const_spec((1, d)),
                  _const_spec(w_in.shape),
                  tab, tab],
        out_specs=[tok(QK_WIDTH), tok(QK_WIDTH), tok(V_WIDTH), tok(V_WIDTH), tok(F_WIDTH)],
        out_shape=[jax.ShapeDtypeStruct((t, QK_WIDTH), BF16),
                   jax.ShapeDtypeStruct((t, QK_WIDTH), BF16),
                   jax.ShapeDtypeStruct((t, V_WIDTH), BF16),
                   jax.ShapeDtypeStruct((t, V_WIDTH), BF16),
                   jax.ShapeDtypeStruct((t, F_WIDTH), BF16)],
        compiler_params=_params(("parallel",)),
        name="proj",
    )(x2, mod3, norm_w, w_in, cos, sin)


def _retention_kernel(af_ref, ab_ref, q_ref, k_ref, v_ref, sg_ref, kc_ref, vc_ref,
                      o_ref, ds_scr, st_scr, d_scr, qtab_scr):
    c = RET_CHUNK
    seq = q_ref.shape[1]
    lc = kc_ref.shape[1]
    n = seq // c
    lane = lax.broadcasted_iota(jnp.int32, (1, LANES), 1)
    pos = lax.broadcasted_iota(jnp.int32, (c, LANES), 0).astype(F32)
    posl = lax.broadcasted_iota(jnp.int32, (1, c), 1).astype(F32)
    cposl = lax.broadcasted_iota(jnp.int32, (1, lc), 1).astype(F32)
    diff = (lax.broadcasted_iota(jnp.int32, (c, c), 0)
            - lax.broadcasted_iota(jnp.int32, (c, c), 1)).astype(F32)

    masks, kdec, kdec_ctx, chunk_decay = [], [], [], []
    for hh in range(2):
        lgf = -jnp.exp(af_ref[hh])
        lgb = -jnp.exp(ab_ref[hh])
        lgf_s, lgb_s = lgf[:, 0:1], lgb[:, 0:1]
        mask = (((lane // (QK_DIM // 2)) % 2) == hh).astype(F32)
        masks.append(mask.astype(BF16))
        qtab_scr[hh, 0] = jnp.exp(lgf * (pos + 1.0)) * mask
        qtab_scr[hh, 1] = jnp.exp(lgb * (c - pos)) * mask
        d_scr[hh] = (jnp.where(diff >= 0, jnp.exp(lgf_s * jnp.maximum(diff, 0.0)), 0.0)
                     + jnp.where(diff <= 0, jnp.exp(lgb_s * jnp.maximum(-diff, 0.0)), 0.0))
        kdec.append((jnp.exp(lgf_s * (c - 1.0 - posl)), jnp.exp(lgb_s * posl)))
        kdec_ctx.append((jnp.exp(lgf_s * (lc - 1.0 - cposl)), jnp.exp(lgb_s * cposl)))
        chunk_decay.append((jnp.exp(lgf * c), jnp.exp(lgb * c)))

    def state_increments(k_rows, v_rows, decays):
        kt = jnp.transpose(k_rows.astype(F32))
        out = []
        for hh in range(2):
            lhs = jnp.concatenate([kt * decays[hh][0], kt * decays[hh][1]], axis=0).astype(BF16)
            out.append(_dot(lhs, v_rows[:, hh * V_DIM:(hh + 1) * V_DIM]))
        return out

    def incr(i, carry):
        rows = pl.ds(pl.multiple_of(i * c, c), c)
        ds = state_increments(k_ref[0, rows, :], v_ref[0, rows, :], kdec)
        ds_scr[i, 0] = ds[0]
        ds_scr[i, 1] = ds[1]
        return carry

    lax.fori_loop(0, n, incr, 0, unroll=2)

    s0 = state_increments(kc_ref[0], vc_ref[0], kdec_ctx)

    def scan_f(i, s):
        out = []
        for hh in range(2):
            st_scr[i, hh, 0:LANES, :] = s[hh].astype(BF16)
            out.append(chunk_decay[hh][0] * s[hh] + ds_scr[i, hh, 0:LANES, :])
        return tuple(out)

    lax.fori_loop(0, n, scan_f, (s0[0][0:LANES], s0[1][0:LANES]))

    def scan_b(t, s):
        i = n - 1 - t
        out = []
        for hh in range(2):
            st_scr[i, hh, LANES:2 * LANES, :] = s[hh].astype(BF16)
            out.append(chunk_decay[hh][1] * s[hh] + ds_scr[i, hh, LANES:2 * LANES, :])
        return tuple(out)

    lax.fori_loop(0, n, scan_b, (s0[0][LANES:2 * LANES], s0[1][LANES:2 * LANES]))

    def outputs(i, carry):
        rows = pl.ds(pl.multiple_of(i * c, c), c)
        q = q_ref[0, rows, :]
        k = k_ref[0, rows, :]
        qf = q.astype(F32)
        for hh in range(2):
            cols = slice(hh * V_DIM, (hh + 1) * V_DIM)
            v = v_ref[0, rows, cols]
            scores = lax.dot_general(q * masks[hh], k, (((1,), (1,)), ((), ())),
                                     preferred_element_type=F32)
            o = _dot((scores * d_scr[hh]).astype(BF16), v)
            qd = jnp.concatenate([qf * qtab_scr[hh, 0], qf * qtab_scr[hh, 1]], axis=1).astype(BF16)
            o = o + _dot(qd, st_scr[i, hh])
            mu = jnp.mean(o, axis=-1, keepdims=True)
            oc = o - mu
            var = jnp.mean(oc * oc, axis=-1, keepdims=True)
            on = oc * lax.rsqrt(var + EPS)
            o_ref[0, rows, cols] = (on * sg_ref[0, rows, cols].astype(F32)).astype(BF16)
        return carry

    lax.fori_loop(0, n, outputs, 0, unroll=2)


def _retention(a_f, a_b, q, k, v, sg, kc, vc):
    b, seq, _ = v.shape
    lc = kc.shape[1]
    c = RET_CHUNK
    n = seq // c
    dec = pl.BlockSpec((2, 1, LANES), lambda i, p: (p, 0, 0))
    qk = lambda rows: pl.BlockSpec((1, rows, LANES), lambda i, p: (i, 0, p))
    vv = lambda rows: pl.BlockSpec((1, rows, 2 * V_DIM), lambda i, p: (i, 0, p))
    return pl.pallas_call(
        _retention_kernel,
        grid=(b, HEADS // 2),
        in_specs=[dec, dec, qk(seq), qk(seq), vv(seq), vv(seq), qk(lc), vv(lc)],
        out_specs=vv(seq),
        out_shape=jax.ShapeDtypeStruct((b, seq, V_WIDTH), BF16),
        scratch_shapes=[pltpu.VMEM((n, 2, 2 * LANES, V_DIM), F32),
                        pltpu.VMEM((n, 2, 2 * LANES, V_DIM), BF16),
                        pltpu.VMEM((2, c, c), F32),
                        pltpu.VMEM((2, 2, c, LANES), F32)],
        compiler_params=_params(("parallel", "parallel")),
        name="retention",
    )(a_f, a_b, q, k, v, sg, kc, vc)


def _dft_a_kernel(w_ref, x_ref, y_ref):
    y_ref[0] = _dot(w_ref[...].astype(BF16), x_ref[0]).astype(BF16)


def _dft_b_kernel(tw_ref, w2_ref, cc_ref, y_ref, z_ref):
    n2 = y_ref.shape[3]
    cc = cc_ref[...].astype(BF16)
    w2c = w2_ref[0]
    w2s = w2_ref[1]
    for j in range(DFT_K1_PER_STEP):
        tc = tw_ref[j, 0:1, :]
        ts = tw_ref[j, 1:2, :]
        ec = w2c * tc - w2s * ts
        es = w2s * tc + w2c * ts
        m = jnp.concatenate([jnp.concatenate([ec, -es], axis=1),
                             jnp.concatenate([es, ec], axis=1)], axis=0).astype(BF16)
        y = jnp.concatenate([y_ref[0, 0, j], y_ref[0, 1, j]], axis=0)
        zz = _dot(m, y)
        for g in range(F_GROUPS):
            cols = slice(g * F_GROUP_DIM, (g + 1) * F_GROUP_DIM)
            zcs = jnp.concatenate([zz[:n2, cols], zz[n2:, cols]], axis=1).astype(BF16)
            out = _dot(zcs, cc)
            lo = j * F_WIDTH + g * F_GROUP_DIM
            z_ref[0, :, lo:lo + F_GROUP_DIM] = out.astype(BF16)


def _dft_tables(seq):
    n1 = DFT_N1
    n2 = seq // n1

    def cs(num, den):
        ang = 2.0 * np.pi * (num % den) / den
        return np.cos(ang), np.sin(ang)

    a = np.arange(n1)
    w_a = np.concatenate(cs(a[:, None] * a[None, :], n1), axis=0)
    m = np.arange(n2)
    tw = np.stack(cs(a[:, None] * m[None, :], seq), axis=1)
    w2 = np.stack(cs(m[:, None] * m[None, :], n2), axis=0)
    ch = np.arange(F_GROUP_DIM)
    cc, sc = cs(ch[:, None] * ch[None, :], F_GROUP_DIM)
    scale = 1.0 / np.sqrt(seq * F_GROUP_DIM)
    w_c = np.concatenate([cc, -sc], axis=0) * scale
    return [jnp.asarray(t, dtype=F32) for t in (w_a, tw, w2, w_c)]


def _fourier(f):
    b, seq, w = f.shape
    n1 = DFT_N1
    n2 = seq // n1
    w_a, tw, w2, w_c = _dft_tables(seq)
    cols = n2 * w
    tn = 8192
    y = pl.pallas_call(
        _dft_a_kernel,
        grid=(b, cols // tn),
        in_specs=[_const_spec((2 * n1, n1)),
                  pl.BlockSpec((1, n1, tn), lambda i, j: (i, 0, j))],
        out_specs=pl.BlockSpec((1, 2 * n1, tn), lambda i, j: (i, 0, j)),
        out_shape=jax.ShapeDtypeStruct((b, 2 * n1, cols), BF16),
        compiler_params=_params(("parallel", "parallel")),
        name="dft_a",
    )(w_a, f.reshape(b, n1, cols))
    ks = DFT_K1_PER_STEP
    z = pl.pallas_call(
        _dft_b_kernel,
        grid=(b, n1 // ks),
        in_specs=[pl.BlockSpec((ks, 2, n2), lambda i, j: (j, 0, 0)),
                  _const_spec((2, n2, n2)),
                  _const_spec((2 * F_GROUP_DIM, F_GROUP_DIM)),
                  pl.BlockSpec((1, 2, ks, n2, w), lambda i, j: (i, 0, j, 0, 0))],
        out_specs=pl.BlockSpec((1, n2, ks * w), lambda i, j: (i, 0, j)),
        out_shape=jax.ShapeDtypeStruct((b, n2, n1 * w), BF16),
        compiler_params=_params(("parallel", "parallel")),
        name="dft_b",
    )(tw, w2, w_c, y.reshape(b, 2, n1, n2, w))
    return z.reshape(b, seq, w)


def _merge_kernel(x_ref, mod_ref, nw_ref, rg_ref, z_ref, wbg_ref, bbg_ref,
                  wro_ref, wfo_ref, wout_ref, o_ref):
    x = x_ref[...]
    h = _rms_norm(x, nw_ref[...]) * (1.0 + mod_ref[0, 1:2, :]) + mod_ref[0, 0:1, :]
    gates = jax.nn.sigmoid(_dot(h.astype(BF16), wbg_ref[...]) + bbg_ref[...])
    ret_d = _dot(rg_ref[...], wro_ref[...])
    four_d = _dot(z_ref[...], wfo_ref[...])
    m = gates[:, :D_MODEL] * ret_d + gates[:, D_MODEL:] * four_d
    y = _dot(m.astype(BF16), wout_ref[...])
    o_ref[...] = x + mod_ref[0, 2:3, :] * y


def _merge(x2, mod3, norm_w, rg, z, w_bg, b_bg, w_ro, w_fo, w_out, seq):
    t, d = x2.shape
    tl = TOKEN_TILE
    tpb = seq // tl
    tok = lambda w: pl.BlockSpec((tl, w), lambda i: (i, 0))
    return pl.pallas_call(
        _merge_kernel,
        grid=(t // tl,),
        in_specs=[tok(d),
                  pl.BlockSpec((1, N_MOD, d), lambda i: (i // tpb, 0, 0)),
                  _const_spec((1, d)),
                  tok(V_WIDTH), tok(F_WIDTH),
                  _const_spec(w_bg.shape), _const_spec((1, 2 * d)),
                  _const_spec(w_ro.shape), _const_spec(w_fo.shape), _const_spec(w_out.shape)],
        out_specs=tok(d),
        out_shape=jax.ShapeDtypeStruct((t, d), F32),
        compiler_params=_params(("parallel",)),
        name="merge",
    )(x2, mod3, norm_w, rg, z, w_bg, b_bg, w_ro, w_fo, w_out)


HALO = 8


def _two_gelu_tanh(x):
    c1 = np.sqrt(2.0 / np.pi)
    return x + x * jnp.tanh(x * (c1 + (c1 * 0.044715) * (x * x)))


def _ffn_kernel(xm_ref, xp_ref, xn_ref, mod_ref, nw_ref, wu_ref, cw_ref, cb_ref,
                wd_ref, fnw_ref, o_ref, h_scr, u_scr, act_scr, y_scr, *, tiles_per_seq):
    tl = xm_ref.shape[0]
    half = tl // 2
    nc = FFN_CHUNK
    i = pl.program_id(0)
    xm = xm_ref[...]
    keep_prev = ((i % tiles_per_seq) != 0).astype(F32)
    keep_next = ((i % tiles_per_seq) != tiles_per_seq - 1).astype(F32)

    def pre(x):
        return _rms_norm(x, nw_ref[...]) * (1.0 + mod_ref[0, 4:5, :]) + mod_ref[0, 3:4, :]

    h_scr[...] = jnp.concatenate([pre(xp_ref[...]) * keep_prev, pre(xm),
                                  pre(xn_ref[...]) * keep_next], axis=0).astype(BF16)

    def project(j):
        hb = h_scr[...]
        for part in range(2):
            lo = part * FFN_DIM + j * nc
            u = _dot(hb, wu_ref[:, lo:lo + nc])
            for s in range(nc // LANES):
                u_scr[j % 2, part, s] = u[:, s * LANES:(s + 1) * LANES]

    def conv(j, part, s, scale):
        lo = part * FFN_DIM + j * nc + s * LANES
        w = cw_ref[:, lo:lo + LANES] * scale
        bias = cb_ref[:, lo:lo + LANES] * scale
        rows = lambda start: u_scr[j % 2, part, s, pl.ds(start, half, stride=2), :]
        before, even, odd, after = rows(HALO - 1), rows(HALO), rows(HALO + 1), rows(HALO + 2)
        return (before * w[0:1] + even * w[1:2] + odd * w[2:3] + bias,
                even * w[0:1] + odd * w[1:2] + after * w[2:3] + bias)

    def activate(j):
        for s in range(nc // LANES):
            gate = conv(j, 0, s, 1.0)
            val = conv(j, 1, s, 0.5)
            cols = slice(j * nc + s * LANES, j * nc + (s + 1) * LANES)
            act_scr[0:half, cols] = (_two_gelu_tanh(gate[0]) * val[0]).astype(BF16)
            act_scr[half:tl, cols] = (_two_gelu_tanh(gate[1]) * val[1]).astype(BF16)

    project(0)
    for j in range(N_FFN_CHUNKS):
        if j + 1 < N_FFN_CHUNKS:
            project(j + 1)
        activate(j)
    y = _dot(act_scr[...], wd_ref[...])
    for s in range(y_scr.shape[0]):
        y_scr[s, pl.ds(0, half, stride=2), :] = y[0:half, s * LANES:(s + 1) * LANES]
        y_scr[s, pl.ds(1, half, stride=2), :] = y[half:tl, s * LANES:(s + 1) * LANES]
    y = jnp.concatenate([y_scr[s] for s in range(y_scr.shape[0])], axis=1)
    x2 = xm + mod_ref[0, 5:6, :] * y
    o_ref[...] = _rms_norm(x2, fnw_ref[...])


def _ffn(x1, mod3, norm_w, w_u, cw, cb, w_d, fnorm_w, seq):
    t, d = x1.shape
    tl = TOKEN_TILE
    tpb = seq // tl
    hb = tl // HALO
    last = t // HALO - 1
    return pl.pallas_call(
        functools.partial(_ffn_kernel, tiles_per_seq=tpb),
        grid=(t // tl,),
        in_specs=[pl.BlockSpec((tl, d), lambda i: (i, 0)),
                  pl.BlockSpec((HALO, d), lambda i: (jnp.maximum(i * hb - 1, 0), 0)),
                  pl.BlockSpec((HALO, d), lambda i: (jnp.minimum((i + 1) * hb, last), 0)),
                  pl.BlockSpec((1, N_MOD, d), lambda i: (i // tpb, 0, 0)),
                  _const_spec((1, d)),
                  _const_spec(w_u.shape), _const_spec(cw.shape), _const_spec(cb.shape),
                  _const_spec(w_d.shape), _const_spec((1, d))],
        out_specs=pl.BlockSpec((tl, d), lambda i: (i, 0)),
        out_shape=jax.ShapeDtypeStruct((t, d), F32),
        scratch_shapes=[pltpu.VMEM((tl + 2 * HALO, d), BF16),
                        pltpu.VMEM((2, 2, FFN_CHUNK // LANES, tl + 2 * HALO, LANES), F32),
                        pltpu.VMEM((tl, FFN_DIM), BF16),
                        pltpu.VMEM((d // LANES, tl, LANES), F32)],
        compiler_params=_params(("parallel",)),
        name="ffn",
    )(x1, x1, x1, mod3, norm_w, w_u, cw, cb, w_d, fnorm_w)


def _qk_perm():
    lane = np.arange(QK_WIDTH)
    pair, l = lane // LANES, lane % LANES
    head = 2 * pair + (l // (QK_DIM // 2)) % 2
    half = l // (LANES // 2)
    return head * QK_DIM + half * (QK_DIM // 2) + l % (QK_DIM // 2)


def _rope_tables(seq):
    pos = jnp.arange(seq)
    row = (pos // GRID_W).astype(F32)
    col = (pos % GRID_W).astype(F32)
    n_freq = QK_DIM // 4
    inv = ROPE_BASE ** (-jnp.arange(n_freq, dtype=F32) / n_freq)
    ang = jnp.concatenate([row[:, None] * inv, col[:, None] * inv], axis=-1)
    ang = jnp.tile(ang, (1, LANES // (QK_DIM // 2)))
    sign = jnp.where(jnp.arange(LANES) < LANES // 2, -1.0, 1.0).astype(F32)
    return jnp.cos(ang), jnp.sin(ang) * sign


def kernel(x, c, ctx, c_ctx, w_mod, b_mod, norm1_w, w_in, ret_decay_f, ret_decay_b,
           w_ret_out, w_four_out, w_branch_gate, b_branch_gate, w_out, norm2_w,
           w_up, conv_w, conv_b, w_down, final_norm_w):
    assert w_mod.shape[0] == 1, "single-layer block"
    b, seq, d = x.shape
    t = b * seq
    assert b == 2 and seq % TOKEN_TILE == 0 and seq % RET_CHUNK == 0

    c8 = jnp.concatenate([c, c_ctx[None, :], jnp.zeros((8 - b - 1, d), F32)], axis=0)
    mod3 = _modulation(c8, w_mod[0], b_mod[0]).reshape(8, N_MOD, d)

    perm = _qk_perm()
    wi = w_in[0]
    w_in_p = jnp.concatenate([wi[:, perm], wi[:, K_OFF + perm], wi[:, V_OFF:]], axis=1).astype(BF16)
    n1w = norm1_w[0].reshape(1, d)

    kc, vc = _ctx_proj(ctx, mod3, n1w, w_in_p[:, K_OFF:G_OFF])
    cos, sin = _rope_tables(seq)
    x2 = x.reshape(t, d)
    q, k, v, sg, f = _proj(x2, mod3, n1w, w_in_p, cos, sin, seq)

    a_f = jnp.broadcast_to(ret_decay_f[0][:, None, None], (HEADS, 1, LANES))
    a_b = jnp.broadcast_to(ret_decay_b[0][:, None, None], (HEADS, 1, LANES))
    rg = _retention(a_f, a_b, q.reshape(b, seq, QK_WIDTH), k.reshape(b, seq, QK_WIDTH),
                    v.reshape(b, seq, V_WIDTH), sg.reshape(b, seq, V_WIDTH), kc, vc)
    z = _fourier(f.reshape(b, seq, F_WIDTH))

    x1 = _merge(x2, mod3, n1w, rg.reshape(t, V_WIDTH), z.reshape(t, F_WIDTH),
                w_branch_gate[0].astype(BF16), b_branch_gate[0].reshape(1, 2 * d),
                w_ret_out[0].astype(BF16), w_four_out[0].astype(BF16), w_out[0].astype(BF16), seq)

    out = _ffn(x1, mod3, norm2_w[0].reshape(1, d), w_up[0].astype(BF16), conv_w[0],
               conv_b[0].reshape(1, 2 * FFN_DIM), w_down[0].astype(BF16),
               final_norm_w.reshape(1, d), seq)
    return out.reshape(b, seq, d)
```

```python
import functools

import numpy as np
import jax
import jax.numpy as jnp
from jax import lax
from jax.experimental import pallas as pl
from jax.experimental.pallas import tpu as pltpu

F32 = jnp.float32
BF16 = jnp.bfloat16

D_MODEL = 1024
GRID_W = 64
HEADS = 8
QK_DIM = 64
V_DIM = 128
QK_WIDTH = HEADS * QK_DIM
V_WIDTH = HEADS * V_DIM
ROPE_BASE = 10000.0
F_GROUPS = 4
F_GROUP_DIM = 128
F_WIDTH = F_GROUPS * F_GROUP_DIM
K_OFF = QK_WIDTH
V_OFF = K_OFF + QK_WIDTH
G_OFF = V_OFF + V_WIDTH
F_OFF = G_OFF + V_WIDTH
IN_COLS = F_OFF + F_WIDTH
FFN_DIM = 2816
N_MOD = 6
EPS = 1e-6

LANES = 128
RET_CHUNK = 256
FFN_CHUNK = 256
N_FFN_CHUNKS = FFN_DIM // FFN_CHUNK
TOKEN_TILE = 512
DFT_N1 = 64
DFT_ROWS = 16
VMEM_LIMIT = 56 * 1024 * 1024


def _params(sem):
    return pltpu.CompilerParams(dimension_semantics=sem, vmem_limit_bytes=VMEM_LIMIT)


def _dot(a, b):
    return jnp.dot(a, b, preferred_element_type=F32)


def _rms_norm(x, w):
    return x * lax.rsqrt(jnp.mean(x * x, axis=-1, keepdims=True) + EPS) * w


def _const_spec(shape):
    zeros = (0,) * len(shape)
    return pl.BlockSpec(shape, lambda *_: zeros)


def _mod_kernel(c_ref, w_ref, b_ref, o_ref):
    c = c_ref[...]
    s = c * jax.nn.sigmoid(c)
    o_ref[...] = jnp.dot(s, w_ref[...], preferred_element_type=F32,
                         precision=lax.Precision.HIGHEST) + b_ref[...]


def _modulation(c8, w_mod, b_mod):
    n = w_mod.shape[1]
    tn = 1536
    return pl.pallas_call(
        _mod_kernel,
        grid=(n // tn,),
        in_specs=[_const_spec((8, D_MODEL)),
                  pl.BlockSpec((D_MODEL, tn), lambda j: (0, j)),
                  pl.BlockSpec((1, tn), lambda j: (0, j))],
        out_specs=pl.BlockSpec((8, tn), lambda j: (0, j)),
        out_shape=jax.ShapeDtypeStruct((8, n), F32),
        compiler_params=_params(("parallel",)),
        name="mod",
    )(c8, w_mod, b_mod.reshape(1, n))


def _ctx_kernel(x_ref, mod_ref, nw_ref, w_ref, k_ref, v_ref):
    x = x_ref[0]
    h = _rms_norm(x, nw_ref[...]) * (1.0 + mod_ref[0, 1:2, :]) + mod_ref[0, 0:1, :]
    p = _dot(h.astype(BF16), w_ref[...])
    k_ref[0] = p[:, :QK_WIDTH].astype(BF16)
    v_ref[0] = p[:, QK_WIDTH:].astype(BF16)


def _ctx_proj(ctx, mod3, norm_w, w_kv):
    b, lc, d = ctx.shape
    return pl.pallas_call(
        _ctx_kernel,
        grid=(b,),
        in_specs=[pl.BlockSpec((1, lc, d), lambda i: (i, 0, 0)),
                  pl.BlockSpec((1, N_MOD, d), lambda i: (2, 0, 0)),
                  _const_spec((1, d)),
                  _const_spec(w_kv.shape)],
        out_specs=[pl.BlockSpec((1, lc, QK_WIDTH), lambda i: (i, 0, 0)),
                   pl.BlockSpec((1, lc, V_WIDTH), lambda i: (i, 0, 0))],
        out_shape=[jax.ShapeDtypeStruct((b, lc, QK_WIDTH), BF16),
                   jax.ShapeDtypeStruct((b, lc, V_WIDTH), BF16)],
        compiler_params=_params(("parallel",)),
        name="ctx_proj",
    )(ctx, mod3, norm_w, w_kv)


def _proj_kernel(x_ref, mod_ref, nw_ref, w_ref, cos_ref, sin_ref,
                 q_ref, k_ref, v_ref, sg_ref, f_ref):
    x = x_ref[...]
    h = _rms_norm(x, nw_ref[...]) * (1.0 + mod_ref[0, 1:2, :]) + mod_ref[0, 0:1, :]
    hb = h.astype(BF16)
    cos = cos_ref[...]
    sin = sin_ref[...]

    def rope(t, scale, out_ref):
        for j in range(QK_WIDTH // LANES):
            tj = t[:, j * LANES:(j + 1) * LANES]
            r = tj * cos + pltpu.roll(tj, LANES // 2, 1) * sin
            out_ref[:, j * LANES:(j + 1) * LANES] = (r * scale).astype(BF16)

    rope(_dot(hb, w_ref[:, 0:K_OFF]), QK_DIM ** -0.5, q_ref)
    rope(_dot(hb, w_ref[:, K_OFF:V_OFF]), 1.0, k_ref)
    v_ref[...] = _dot(hb, w_ref[:, V_OFF:G_OFF]).astype(BF16)
    g = _dot(hb, w_ref[:, G_OFF:F_OFF])
    sg_ref[...] = (g * jax.nn.sigmoid(g)).astype(BF16)
    f_ref[...] = _dot(hb, w_ref[:, F_OFF:IN_COLS]).astype(BF16)


def _proj(x2, mod3, norm_w, w_in, cos, sin, seq):
    t, d = x2.shape
    tl = TOKEN_TILE
    tpb = seq // tl
    tok = lambda w: pl.BlockSpec((tl, w), lambda i: (i, 0))
    tab = pl.BlockSpec((tl, LANES), lambda i: (i % tpb, 0))
    return pl.pallas_call(
        _proj_kernel,
        grid=(t // tl,),
        in_specs=[tok(d),
                  pl.BlockSpec((1, N_MOD, d), lambda i: (i // tpb, 0, 0)),
                  _const_spec((1, d)),
                  _const_spec(w_in.shape),
                  tab, tab],
        out_specs=[tok(QK_WIDTH), tok(QK_WIDTH), tok(V_WIDTH), tok(V_WIDTH), tok(F_WIDTH)],
        out_shape=[jax.ShapeDtypeStruct((t, QK_WIDTH), BF16),
                   jax.ShapeDtypeStruct((t, QK_WIDTH), BF16),
                   jax.ShapeDtypeStruct((t, V_WIDTH), BF16),
                   jax.ShapeDtypeStruct((t, V_WIDTH), BF16),
                   jax.ShapeDtypeStruct((t, F_WIDTH), BF16)],
        compiler_params=_params(("parallel",)),
        name="proj",
    )(x2, mod3, norm_w, w_in, cos, sin)


def _retention_kernel(af_ref, ab_ref, q_ref, k_ref, v_ref, sg_ref, kc_ref, vc_ref,
                      o_ref, ds_scr, st_scr, d_scr, qtab_scr):
    c = RET_CHUNK
    seq = q_ref.shape[1]
    lc = kc_ref.shape[1]
    n = seq // c
    lane = lax.broadcasted_iota(jnp.int32, (1, LANES), 1)
    pos = lax.broadcasted_iota(jnp.int32, (c, LANES), 0).astype(F32)
    posl = lax.broadcasted_iota(jnp.int32, (1, c), 1).astype(F32)
    cposl = lax.broadcasted_iota(jnp.int32, (1, lc), 1).astype(F32)
    diff = (lax.broadcasted_iota(jnp.int32, (c, c), 0)
            - lax.broadcasted_iota(jnp.int32, (c, c), 1)).astype(F32)

    masks, kdec, kdec_ctx, chunk_decay = [], [], [], []
    for hh in range(2):
        lgf = -jnp.exp(af_ref[hh])
        lgb = -jnp.exp(ab_ref[hh])
        lgf_s, lgb_s = lgf[:, 0:1], lgb[:, 0:1]
        mask = (((lane // (QK_DIM // 2)) % 2) == hh).astype(F32)
        masks.append(mask.astype(BF16))
        qtab_scr[hh, 0] = jnp.exp(lgf * (pos + 1.0)) * mask
        qtab_scr[hh, 1] = jnp.exp(lgb * (c - pos)) * mask
        d_scr[hh] = (jnp.where(diff >= 0, jnp.exp(lgf_s * jnp.maximum(diff, 0.0)), 0.0)
                     + jnp.where(diff <= 0, jnp.exp(lgb_s * jnp.maximum(-diff, 0.0)), 0.0))
        kdec.append((jnp.exp(lgf_s * (c - 1.0 - posl)), jnp.exp(lgb_s * posl)))
        kdec_ctx.append((jnp.exp(lgf_s * (lc - 1.0 - cposl)), jnp.exp(lgb_s * cposl)))
        chunk_decay.append((jnp.exp(lgf * c), jnp.exp(lgb * c)))

    def state_increments(k_rows, v_rows, decays):
        kt = jnp.transpose(k_rows.astype(F32))
        out = []
        for hh in range(2):
            lhs = jnp.concatenate([kt * decays[hh][0], kt * decays[hh][1]], axis=0).astype(BF16)
            out.append(_dot(lhs, v_rows[:, hh * V_DIM:(hh + 1) * V_DIM]))
        return out

    def incr(i, carry):
        rows = pl.ds(pl.multiple_of(i * c, c), c)
        ds = state_increments(k_ref[0, rows, :], v_ref[0, rows, :], kdec)
        ds_scr[i, 0] = ds[0]
        ds_scr[i, 1] = ds[1]
        return carry

    lax.fori_loop(0, n, incr, 0, unroll=2)

    s0 = state_increments(kc_ref[0], vc_ref[0], kdec_ctx)

    def scan_f(i, s):
        out = []
        for hh in range(2):
            st_scr[i, hh, 0:LANES, :] = s[hh].astype(BF16)
            out.append(chunk_decay[hh][0] * s[hh] + ds_scr[i, hh, 0:LANES, :])
        return tuple(out)

    lax.fori_loop(0, n, scan_f, (s0[0][0:LANES], s0[1][0:LANES]))

    def scan_b(t, s):
        i = n - 1 - t
        out = []
        for hh in range(2):
            st_scr[i, hh, LANES:2 * LANES, :] = s[hh].astype(BF16)
            out.append(chunk_decay[hh][1] * s[hh] + ds_scr[i, hh, LANES:2 * LANES, :])
        return tuple(out)

    lax.fori_loop(0, n, scan_b, (s0[0][LANES:2 * LANES], s0[1][LANES:2 * LANES]))

    def outputs(i, carry):
        rows = pl.ds(pl.multiple_of(i * c, c), c)
        q = q_ref[0, rows, :]
        k = k_ref[0, rows, :]
        qf = q.astype(F32)
        for hh in range(2):
            cols = slice(hh * V_DIM, (hh + 1) * V_DIM)
            v = v_ref[0, rows, cols]
            scores = lax.dot_general(q * masks[hh], k, (((1,), (1,)), ((), ())),
                                     preferred_element_type=F32)
            o = _dot((scores * d_scr[hh]).astype(BF16), v)
            qd = jnp.concatenate([qf * qtab_scr[hh, 0], qf * qtab_scr[hh, 1]], axis=1).astype(BF16)
            o = o + _dot(qd, st_scr[i, hh])
            mu = jnp.mean(o, axis=-1, keepdims=True)
            oc = o - mu
            var = jnp.mean(oc * oc, axis=-1, keepdims=True)
            on = oc * lax.rsqrt(var + EPS)
            o_ref[0, rows, cols] = (on * sg_ref[0, rows, cols].astype(F32)).astype(BF16)
        return carry

    lax.fori_loop(0, n, outputs, 0, unroll=2)


def _retention(a_f, a_b, q, k, v, sg, kc, vc):
    b, seq, _ = v.shape
    lc = kc.shape[1]
    c = RET_CHUNK
    n = seq // c
    dec = pl.BlockSpec((2, 1, LANES), lambda i, p: (p, 0, 0))
    qk = lambda rows: pl.BlockSpec((1, rows, LANES), lambda i, p: (i, 0, p))
    vv = lambda rows: pl.BlockSpec((1, rows, 2 * V_DIM), lambda i, p: (i, 0, p))
    return pl.pallas_call(
        _retention_kernel,
        grid=(b, HEADS // 2),
        in_specs=[dec, dec, qk(seq), qk(seq), vv(seq), vv(seq), qk(lc), vv(lc)],
        out_specs=vv(seq),
        out_shape=jax.ShapeDtypeStruct((b, seq, V_WIDTH), BF16),
        scratch_shapes=[pltpu.VMEM((n, 2, 2 * LANES, V_DIM), F32),
                        pltpu.VMEM((n, 2, 2 * LANES, V_DIM), BF16),
                        pltpu.VMEM((2, c, c), F32),
                        pltpu.VMEM((2, 2, c, LANES), F32)],
        compiler_params=_params(("parallel", "parallel")),
        name="retention",
    )(a_f, a_b, q, k, v, sg, kc, vc)


def _dft_a_kernel(w_ref, x_ref, y_ref):
    _, n1, r, w = x_ref.shape
    y = _dot(w_ref[...], x_ref[0].reshape(n1 * r, w))
    y_ref[0] = y.reshape(2, n1, r, w).astype(BF16)


def _dft_b_kernel(tw_ref, w2_ref, cc_ref, y_ref, z_ref, z_scr):
    n2 = y_ref.shape[3]
    cc = cc_ref[...].astype(BF16)
    w2c = w2_ref[0]
    w2s = w2_ref[1]
    for j in range(DFT_ROWS):
        tc = tw_ref[j, 0:1, :]
        ts = tw_ref[j, 1:2, :]
        ec = w2c * tc - w2s * ts
        es = w2s * tc + w2c * ts
        m = jnp.concatenate([jnp.concatenate([ec, -es], axis=1),
                             jnp.concatenate([es, ec], axis=1)], axis=0).astype(BF16)
        y = jnp.concatenate([y_ref[0, 0, j], y_ref[0, 1, j]], axis=0)
        zz = _dot(m, y)
        for g in range(F_GROUPS):
            cols = slice(g * F_GROUP_DIM, (g + 1) * F_GROUP_DIM)
            zcs = jnp.concatenate([zz[:n2, cols], zz[n2:, cols]], axis=1).astype(BF16)
            z_scr[g, pl.ds(j, n2, stride=DFT_ROWS), :] = _dot(zcs, cc)
    z = jnp.concatenate([z_scr[g] for g in range(F_GROUPS)], axis=1)
    z_ref[0] = z.reshape(n2, DFT_ROWS, F_WIDTH).astype(BF16)


def _dft_tables(seq):
    n1 = DFT_N1
    n2 = seq // n1

    def cs(num, den):
        ang = 2.0 * np.pi * (num % den) / den
        return np.cos(ang), np.sin(ang)

    a = np.arange(n1)
    w_a = np.concatenate(cs(a[:, None] * a[None, :], n1), axis=0)
    m = np.arange(n2)
    tw = np.stack(cs(a[:, None] * m[None, :], seq), axis=1)
    w2 = np.stack(cs(m[:, None] * m[None, :], n2), axis=0)
    ch = np.arange(F_GROUP_DIM)
    cc, sc = cs(ch[:, None] * ch[None, :], F_GROUP_DIM)
    scale = 1.0 / np.sqrt(seq * F_GROUP_DIM)
    w_c = np.concatenate([cc, -sc], axis=0) * scale
    return [jnp.asarray(t, dtype=F32) for t in (w_a, tw, w2, w_c)]


def _fourier(f):
    b, seq, w = f.shape
    n1 = DFT_N1
    n2 = seq // n1
    w_a, tw, w2, w_c = _dft_tables(seq)
    r = DFT_ROWS
    w_a = jnp.kron(w_a, jnp.eye(r, dtype=F32)).astype(BF16)
    y = pl.pallas_call(
        _dft_a_kernel,
        grid=(b, n2 // r),
        in_specs=[_const_spec(w_a.shape),
                  pl.BlockSpec((1, n1, r, w), lambda i, j: (i, 0, j, 0))],
        out_specs=pl.BlockSpec((1, 2, n1, r, w), lambda i, j: (i, 0, 0, j, 0)),
        out_shape=jax.ShapeDtypeStruct((b, 2, n1, n2, w), BF16),
        compiler_params=_params(("parallel", "parallel")),
        name="dft_a",
    )(w_a, f.reshape(b, n1, n2, w))
    z = pl.pallas_call(
        _dft_b_kernel,
        grid=(b, n1 // r),
        in_specs=[pl.BlockSpec((r, 2, n2), lambda i, j: (j, 0, 0)),
                  _const_spec((2, n2, n2)),
                  _const_spec((2 * F_GROUP_DIM, F_GROUP_DIM)),
                  pl.BlockSpec((1, 2, r, n2, w), lambda i, j: (i, 0, j, 0, 0))],
        out_specs=pl.BlockSpec((1, n2, r, w), lambda i, j: (i, 0, j, 0)),
        out_shape=jax.ShapeDtypeStruct((b, n2, n1, w), BF16),
        scratch_shapes=[pltpu.VMEM((F_GROUPS, n2 * r, F_GROUP_DIM), F32)],
        compiler_params=_params(("parallel", "parallel")),
        name="dft_b",
    )(tw, w2, w_c, y)
    return z.reshape(b, seq, w)


def _merge_kernel(x_ref, mod_ref, nw_ref, rg_ref, z_ref, wbg_ref, bbg_ref,
                  wro_ref, wfo_ref, wout_ref, o_ref):
    x = x_ref[...]
    h = _rms_norm(x, nw_ref[...]) * (1.0 + mod_ref[0, 1:2, :]) + mod_ref[0, 0:1, :]
    gates = jax.nn.sigmoid(_dot(h.astype(BF16), wbg_ref[...]) + bbg_ref[...])
    ret_d = _dot(rg_ref[...], wro_ref[...])
    four_d = _dot(z_ref[...], wfo_ref[...])
    m = gates[:, :D_MODEL] * ret_d + gates[:, D_MODEL:] * four_d
    y = _dot(m.astype(BF16), wout_ref[...])
    o_ref[...] = x + mod_ref[0, 2:3, :] * y


def _merge(x2, mod3, norm_w, rg, z, w_bg, b_bg, w_ro, w_fo, w_out, seq):
    t, d = x2.shape
    tl = TOKEN_TILE
    tpb = seq // tl
    tok = lambda w: pl.BlockSpec((tl, w), lambda i: (i, 0))
    return pl.pallas_call(
        _merge_kernel,
        grid=(t // tl,),
        in_specs=[tok(d),
                  pl.BlockSpec((1, N_MOD, d), lambda i: (i // tpb, 0, 0)),
                  _const_spec((1, d)),
                  tok(V_WIDTH), tok(F_WIDTH),
                  _const_spec(w_bg.shape), _const_spec((1, 2 * d)),
                  _const_spec(w_ro.shape), _const_spec(w_fo.shape), _const_spec(w_out.shape)],
        out_specs=tok(d),
        out_shape=jax.ShapeDtypeStruct((t, d), F32),
        compiler_params=_params(("parallel",)),
        name="merge",
    )(x2, mod3, norm_w, rg, z, w_bg, b_bg, w_ro, w_fo, w_out)


HALO = 8


def _two_gelu_tanh(x):
    c1 = np.sqrt(2.0 / np.pi)
    return x + x * jnp.tanh(x * (c1 + (c1 * 0.044715) * (x * x)))


def _ffn_kernel(xm_ref, xp_ref, xn_ref, mod_ref, nw_ref, wu_ref, cw_ref, cb_ref,
                wd_ref, fnw_ref, o_ref, h_scr, u_scr, act_scr, y_scr, *, tiles_per_seq):
    tl = xm_ref.shape[0]
    half = tl // 2
    nc = FFN_CHUNK
    i = pl.program_id(0)
    xm = xm_ref[...]
    keep_prev = ((i % tiles_per_seq) != 0).astype(F32)
    keep_next = ((i % tiles_per_seq) != tiles_per_seq - 1).astype(F32)

    def pre(x):
        return _rms_norm(x, nw_ref[...]) * (1.0 + mod_ref[0, 4:5, :]) + mod_ref[0, 3:4, :]

    h_scr[...] = jnp.concatenate([pre(xp_ref[...]) * keep_prev, pre(xm),
                                  pre(xn_ref[...]) * keep_next], axis=0).astype(BF16)

    def project(j):
        hb = h_scr[...]
        for part in range(2):
            lo = part * FFN_DIM + j * nc
            u = _dot(hb, wu_ref[:, lo:lo + nc])
            for s in range(nc // LANES):
                u_scr[j % 2, part, s] = u[:, s * LANES:(s + 1) * LANES]

    def conv(j, part, s, scale):
        lo = part * FFN_DIM + j * nc + s * LANES
        w = cw_ref[:, lo:lo + LANES] * scale
        bias = cb_ref[:, lo:lo + LANES] * scale
        rows = lambda start: u_scr[j % 2, part, s, pl.ds(start, half, stride=2), :]
        before, even, odd, after = rows(HALO - 1), rows(HALO), rows(HALO + 1), rows(HALO + 2)
        return (before * w[0:1] + even * w[1:2] + odd * w[2:3] + bias,
                even * w[0:1] + odd * w[1:2] + after * w[2:3] + bias)

    def activate(j):
        for s in range(nc // LANES):
            gate = conv(j, 0, s, 1.0)
            val = conv(j, 1, s, 0.5)
            cols = slice(j * nc + s * LANES, j * nc + (s + 1) * LANES)
            act_scr[0:half, cols] = (_two_gelu_tanh(gate[0]) * val[0]).astype(BF16)
            act_scr[half:tl, cols] = (_two_gelu_tanh(gate[1]) * val[1]).astype(BF16)

    project(0)
    for j in range(N_FFN_CHUNKS):
        if j + 1 < N_FFN_CHUNKS:
            project(j + 1)
        activate(j)
    y = _dot(act_scr[...], wd_ref[...])
    for s in range(y_scr.shape[0]):
        y_scr[s, pl.ds(0, half, stride=2), :] = y[0:half, s * LANES:(s + 1) * LANES]
        y_scr[s, pl.ds(1, half, stride=2), :] = y[half:tl, s * LANES:(s + 1) * LANES]
    y = jnp.concatenate([y_scr[s] for s in range(y_scr.shape[0])], axis=1)
    x2 = xm + mod_ref[0, 5:6, :] * y
    o_ref[...] = _rms_norm(x2, fnw_ref[...])


def _ffn(x1, mod3, norm_w, w_u, cw, cb, w_d, fnorm_w, seq):
    t, d = x1.shape
    tl = TOKEN_TILE
    tpb = seq // tl
    hb = tl // HALO
    last = t // HALO - 1
    return pl.pallas_call(
        functools.partial(_ffn_kernel, tiles_per_seq=tpb),
        grid=(t // tl,),
        in_specs=[pl.BlockSpec((tl, d), lambda i: (i, 0)),
                  pl.BlockSpec((HALO, d), lambda i: (jnp.maximum(i * hb - 1, 0), 0)),
                  pl.BlockSpec((HALO, d), lambda i: (jnp.minimum((i + 1) * hb, last), 0)),
                  pl.BlockSpec((1, N_MOD, d), lambda i: (i // tpb, 0, 0)),
                  _const_spec((1, d)),
                  _const_spec(w_u.shape), _const_spec(cw.shape), _const_spec(cb.shape),
                  _const_spec(w_d.shape), _const_spec((1, d))],
        out_specs=pl.BlockSpec((tl, d), lambda i: (i, 0)),
        out_shape=jax.ShapeDtypeStruct((t, d), F32),
        scratch_shapes=[pltpu.VMEM((tl + 2 * HALO, d), BF16),
                        pltpu.VMEM((2, 2, FFN_CHUNK // LANES, tl + 2 * HALO, LANES), F32),
                        pltpu.VMEM((tl, FFN_DIM), BF16),
                        pltpu.VMEM((d // LANES, tl, LANES), F32)],
        compiler_params=_params(("parallel",)),
        name="ffn",
    )(x1, x1, x1, mod3, norm_w, w_u, cw, cb, w_d, fnorm_w)


def _qk_perm():
    lane = np.arange(QK_WIDTH)
    pair, l = lane // LANES, lane % LANES
    head = 2 * pair + (l // (QK_DIM // 2)) % 2
    half = l // (LANES // 2)
    return head * QK_DIM + half * (QK_DIM // 2) + l % (QK_DIM // 2)


def _rope_tables(seq):
    pos = jnp.arange(seq)
    row = (pos // GRID_W).astype(F32)
    col = (pos % GRID_W).astype(F32)
    n_freq = QK_DIM // 4
    inv = ROPE_BASE ** (-jnp.arange(n_freq, dtype=F32) / n_freq)
    ang = jnp.concatenate([row[:, None] * inv, col[:, None] * inv], axis=-1)
    ang = jnp.tile(ang, (1, LANES // (QK_DIM // 2)))
    sign = jnp.where(jnp.arange(LANES) < LANES // 2, -1.0, 1.0).astype(F32)
    return jnp.cos(ang), jnp.sin(ang) * sign


def kernel(x, c, ctx, c_ctx, w_mod, b_mod, norm1_w, w_in, ret_decay_f, ret_decay_b,
           w_ret_out, w_four_out, w_branch_gate, b_branch_gate, w_out, norm2_w,
           w_up, conv_w, conv_b, w_down, final_norm_w):
    assert w_mod.shape[0] == 1, "single-layer block"
    b, seq, d = x.shape
    t = b * seq
    assert b == 2 and seq % TOKEN_TILE == 0 and seq % RET_CHUNK == 0

    c8 = jnp.concatenate([c, c_ctx[None, :], jnp.zeros((8 - b - 1, d), F32)], axis=0)
    mod3 = _modulation(c8, w_mod[0], b_mod[0]).reshape(8, N_MOD, d)

    perm = _qk_perm()
    wi = w_in[0]
    w_in_p = jnp.concatenate([wi[:, perm], wi[:, K_OFF + perm], wi[:, V_OFF:]], axis=1).astype(BF16)
    n1w = norm1_w[0].reshape(1, d)

    kc, vc = _ctx_proj(ctx, mod3, n1w, w_in_p[:, K_OFF:G_OFF])
    cos, sin = _rope_tables(seq)
    x2 = x.reshape(t, d)
    q, k, v, sg, f = _proj(x2, mod3, n1w, w_in_p, cos, sin, seq)

    a_f = jnp.broadcast_to(ret_decay_f[0][:, None, None], (HEADS, 1, LANES))
    a_b = jnp.broadcast_to(ret_decay_b[0][:, None, None], (HEADS, 1, LANES))
    rg = _retention(a_f, a_b, q.reshape(b, seq, QK_WIDTH), k.reshape(b, seq, QK_WIDTH),
                    v.reshape(b, seq, V_WIDTH), sg.reshape(b, seq, V_WIDTH), kc, vc)
    z = _fourier(f.reshape(b, seq, F_WIDTH))

    x1 = _merge(x2, mod3, n1w, rg.reshape(t, V_WIDTH), z.reshape(t, F_WIDTH),
                w_branch_gate[0].astype(BF16), b_branch_gate[0].reshape(1, 2 * d),
                w_ret_out[0].astype(BF16), w_four_out[0].astype(BF16), w_out[0].astype(BF16), seq)

    out = _ffn(x1, mod3, norm2_w[0].reshape(1, d), w_up[0].astype(BF16), conv_w[0],
               conv_b[0].reshape(1, 2 * FFN_DIM), w_down[0].astype(BF16),
               final_norm_w.reshape(1, d), seq)
    return out.reshape(b, seq, d)
```

```python
import functools

import numpy as np
import jax
import jax.numpy as jnp
from jax import lax
from jax.experimental import pallas as pl
from jax.experimental.pallas import tpu as pltpu

F32 = jnp.float32
BF16 = jnp.bfloat16

D_MODEL = 1024
GRID_W = 64
HEADS = 8
QK_DIM = 64
V_DIM = 128
QK_WIDTH = HEADS * QK_DIM
V_WIDTH = HEADS * V_DIM
ROPE_BASE = 10000.0
F_GROUPS = 4
F_GROUP_DIM = 128
F_WIDTH = F_GROUPS * F_GROUP_DIM
K_OFF = QK_WIDTH
V_OFF = K_OFF + QK_WIDTH
G_OFF = V_OFF + V_WIDTH
F_OFF = G_OFF + V_WIDTH
IN_COLS = F_OFF + F_WIDTH
FFN_DIM = 2816
N_MOD = 6
EPS = 1e-6

LANES = 128
RET_CHUNK = 256
FFN_CHUNK = 256
N_FFN_CHUNKS = FFN_DIM // FFN_CHUNK
TOKEN_TILE = 512
DFT_N1 = 64
DFT_ROWS = 16
VMEM_LIMIT = 56 * 1024 * 1024


def _params(sem):
    return pltpu.CompilerParams(dimension_semantics=sem, vmem_limit_bytes=VMEM_LIMIT)


def _dot(a, b):
    return jnp.dot(a, b, preferred_element_type=F32)


def _rms_norm(x, w):
    return x * lax.rsqrt(jnp.mean(x * x, axis=-1, keepdims=True) + EPS) * w


def _const_spec(shape):
    zeros = (0,) * len(shape)
    return pl.BlockSpec(shape, lambda *_: zeros)


def _mod_kernel(c_ref, w_ref, b_ref, o_ref):
    c = c_ref[...]
    s = c * jax.nn.sigmoid(c)
    o_ref[...] = jnp.dot(s, w_ref[...], preferred_element_type=F32,
                         precision=lax.Precision.HIGHEST) + b_ref[...]


def _modulation(c8, w_mod, b_mod):
    n = w_mod.shape[1]
    tn = 1536
    return pl.pallas_call(
        _mod_kernel,
        grid=(n // tn,),
        in_specs=[_const_spec((8, D_MODEL)),
                  pl.BlockSpec((D_MODEL, tn), lambda j: (0, j)),
                  pl.BlockSpec((1, tn), lambda j: (0, j))],
        out_specs=pl.BlockSpec((8, tn), lambda j: (0, j)),
        out_shape=jax.ShapeDtypeStruct((8, n), F32),
        compiler_params=_params(("parallel",)),
        name="mod",
    )(c8, w_mod, b_mod.reshape(1, n))


def _ctx_kernel(x_ref, mod_ref, nw_ref, wqk_ref, w_ref, k_ref, v_ref):
    x = x_ref[0]
    h = _rms_norm(x, nw_ref[...]) * (1.0 + mod_ref[0, 1:2, :]) + mod_ref[0, 0:1, :]
    hb = h.astype(BF16)
    k_ref[0] = _dot(hb, wqk_ref[:, QK_WIDTH:]).astype(BF16)
    v_ref[0] = _dot(hb, w_ref[:, V_OFF:G_OFF]).astype(BF16)


def _ctx_proj(ctx, mod3, norm_w, w_qk, w_in):
    b, lc, d = ctx.shape
    return pl.pallas_call(
        _ctx_kernel,
        grid=(b,),
        in_specs=[pl.BlockSpec((1, lc, d), lambda i: (i, 0, 0)),
                  pl.BlockSpec((1, N_MOD, d), lambda i: (2, 0, 0)),
                  _const_spec((1, d)),
                  _const_spec(w_qk.shape), _const_spec(w_in.shape)],
        out_specs=[pl.BlockSpec((1, lc, QK_WIDTH), lambda i: (i, 0, 0)),
                   pl.BlockSpec((1, lc, V_WIDTH), lambda i: (i, 0, 0))],
        out_shape=[jax.ShapeDtypeStruct((b, lc, QK_WIDTH), BF16),
                   jax.ShapeDtypeStruct((b, lc, V_WIDTH), BF16)],
        compiler_params=_params(("parallel",)),
        name="ctx_proj",
    )(ctx, mod3, norm_w, w_qk, w_in)


def _proj_kernel(x_ref, mod_ref, nw_ref, wqk_ref, w_ref, cos_ref, sin_ref,
                 q_ref, k_ref, v_ref, sg_ref, f_ref):
    x = x_ref[...]
    h = _rms_norm(x, nw_ref[...]) * (1.0 + mod_ref[0, 1:2, :]) + mod_ref[0, 0:1, :]
    hb = h.astype(BF16)
    cos = cos_ref[...]
    sin = sin_ref[...]

    def rope(t, scale, out_ref):
        for j in range(QK_WIDTH // LANES):
            tj = t[:, j * LANES:(j + 1) * LANES]
            r = tj * cos + pltpu.roll(tj, LANES // 2, 1) * sin
            out_ref[:, j * LANES:(j + 1) * LANES] = (r * scale).astype(BF16)

    rope(_dot(hb, wqk_ref[:, :QK_WIDTH]), QK_DIM ** -0.5, q_ref)
    rope(_dot(hb, wqk_ref[:, QK_WIDTH:]), 1.0, k_ref)
    v_ref[...] = _dot(hb, w_ref[:, V_OFF:G_OFF]).astype(BF16)
    g = _dot(hb, w_ref[:, G_OFF:F_OFF])
    sg_ref[...] = (g * jax.nn.sigmoid(g)).astype(BF16)
    f_ref[...] = _dot(hb, w_ref[:, F_OFF:IN_COLS]).astype(BF16)


def _proj(x2, mod3, norm_w, w_qk, w_in, cos, sin, seq):
    t, d = x2.shape
    tl = TOKEN_TILE
    tpb = seq // tl
    tok = lambda w: pl.BlockSpec((tl, w), lambda i: (i, 0))
    tab = pl.BlockSpec((tl, LANES), lambda i: (i % tpb, 0))
    return pl.pallas_call(
        _proj_kernel,
        grid=(t // tl,),
        in_specs=[tok(d),
                  pl.BlockSpec((1, N_MOD, d), lambda i: (i // tpb, 0, 0)),
                  _const_spec((1, d)),
                  _const_spec(w_qk.shape), _const_spec(w_in.shape),
                  tab, tab],
        out_specs=[tok(QK_WIDTH), tok(QK_WIDTH), tok(V_WIDTH), tok(V_WIDTH), tok(F_WIDTH)],
        out_shape=[jax.ShapeDtypeStruct((t, QK_WIDTH), BF16),
                   jax.ShapeDtypeStruct((t, QK_WIDTH), BF16),
                   jax.ShapeDtypeStruct((t, V_WIDTH), BF16),
                   jax.ShapeDtypeStruct((t, V_WIDTH), BF16),
                   jax.ShapeDtypeStruct((t, F_WIDTH), BF16)],
        compiler_params=_params(("parallel",)),
        name="proj",
    )(x2, mod3, norm_w, w_qk, w_in, cos, sin)


def _retention_kernel(af_ref, ab_ref, q_ref, k_ref, v_ref, sg_ref, kc_ref, vc_ref,
                      o_ref, ds_scr, st_scr, d_scr, qtab_scr):
    c = RET_CHUNK
    seq = q_ref.shape[1]
    lc = kc_ref.shape[1]
    n = seq // c
    lane = lax.broadcasted_iota(jnp.int32, (1, LANES), 1)
    pos = lax.broadcasted_iota(jnp.int32, (c, LANES), 0).astype(F32)
    posl = lax.broadcasted_iota(jnp.int32, (1, c), 1).astype(F32)
    cposl = lax.broadcasted_iota(jnp.int32, (1, lc), 1).astype(F32)
    diff = (lax.broadcasted_iota(jnp.int32, (c, c), 0)
            - lax.broadcasted_iota(jnp.int32, (c, c), 1)).astype(F32)

    masks, kdec, kdec_ctx, chunk_decay = [], [], [], []
    for hh in range(2):
        lgf = -jnp.exp(af_ref[hh])
        lgb = -jnp.exp(ab_ref[hh])
        lgf_s, lgb_s = lgf[:, 0:1], lgb[:, 0:1]
        mask = (((lane // (QK_DIM // 2)) % 2) == hh).astype(F32)
        masks.append(mask.astype(BF16))
        qtab_scr[hh, 0] = jnp.exp(lgf * (pos + 1.0)) * mask
        qtab_scr[hh, 1] = jnp.exp(lgb * (c - pos)) * mask
        d_scr[hh] = (jnp.where(diff >= 0, jnp.exp(lgf_s * jnp.maximum(diff, 0.0)), 0.0)
                     + jnp.where(diff <= 0, jnp.exp(lgb_s * jnp.maximum(-diff, 0.0)), 0.0))
        kdec.append((jnp.exp(lgf_s * (c - 1.0 - posl)), jnp.exp(lgb_s * posl)))
        kdec_ctx.append((jnp.exp(lgf_s * (lc - 1.0 - cposl)), jnp.exp(lgb_s * cposl)))
        chunk_decay.append((jnp.exp(lgf * c), jnp.exp(lgb * c)))

    def state_increments(k_rows, v_rows, decays):
        kt = jnp.transpose(k_rows.astype(F32))
        out = []
        for hh in range(2):
            lhs = jnp.concatenate([kt * decays[hh][0], kt * decays[hh][1]], axis=0).astype(BF16)
            out.append(_dot(lhs, v_rows[:, hh * V_DIM:(hh + 1) * V_DIM]))
        return out

    def incr(i, carry):
        rows = pl.ds(pl.multiple_of(i * c, c), c)
        ds = state_increments(k_ref[0, rows, :], v_ref[0, rows, :], kdec)
        ds_scr[i, 0] = ds[0]
        ds_scr[i, 1] = ds[1]
        return carry

    lax.fori_loop(0, n, incr, 0, unroll=2)

    s0 = state_increments(kc_ref[0], vc_ref[0], kdec_ctx)

    def scan_f(i, s):
        out = []
        for hh in range(2):
            st_scr[i, hh, 0:LANES, :] = s[hh].astype(BF16)
            out.append(chunk_decay[hh][0] * s[hh] + ds_scr[i, hh, 0:LANES, :])
        return tuple(out)

    lax.fori_loop(0, n, scan_f, (s0[0][0:LANES], s0[1][0:LANES]))

    def scan_b(t, s):
        i = n - 1 - t
        out = []
        for hh in range(2):
            st_scr[i, hh, LANES:2 * LANES, :] = s[hh].astype(BF16)
            out.append(chunk_decay[hh][1] * s[hh] + ds_scr[i, hh, LANES:2 * LANES, :])
        return tuple(out)

    lax.fori_loop(0, n, scan_b, (s0[0][LANES:2 * LANES], s0[1][LANES:2 * LANES]))

    def outputs(i, carry):
        rows = pl.ds(pl.multiple_of(i * c, c), c)
        q = q_ref[0, rows, :]
        k = k_ref[0, rows, :]
        qf = q.astype(F32)
        for hh in range(2):
            cols = slice(hh * V_DIM, (hh + 1) * V_DIM)
            v = v_ref[0, rows, cols]
            scores = lax.dot_general(q * masks[hh], k, (((1,), (1,)), ((), ())),
                                     preferred_element_type=F32)
            o = _dot((scores * d_scr[hh]).astype(BF16), v)
            qd = jnp.concatenate([qf * qtab_scr[hh, 0], qf * qtab_scr[hh, 1]], axis=1).astype(BF16)
            o = o + _dot(qd, st_scr[i, hh])
            mu = jnp.mean(o, axis=-1, keepdims=True)
            oc = o - mu
            var = jnp.mean(oc * oc, axis=-1, keepdims=True)
            on = oc * lax.rsqrt(var + EPS)
            o_ref[0, rows, cols] = (on * sg_ref[0, rows, cols].astype(F32)).astype(BF16)
        return carry

    lax.fori_loop(0, n, outputs, 0, unroll=2)


def _retention(a_f, a_b, q, k, v, sg, kc, vc):
    b, seq, _ = v.shape
    lc = kc.shape[1]
    c = RET_CHUNK
    n = seq // c
    dec = pl.BlockSpec((2, 1, LANES), lambda i, p: (p, 0, 0))
    qk = lambda rows: pl.BlockSpec((1, rows, LANES), lambda i, p: (i, 0, p))
    vv = lambda rows: pl.BlockSpec((1, rows, 2 * V_DIM), lambda i, p: (i, 0, p))
    return pl.pallas_call(
        _retention_kernel,
        grid=(b, HEADS // 2),
        in_specs=[dec, dec, qk(seq), qk(seq), vv(seq), vv(seq), qk(lc), vv(lc)],
        out_specs=vv(seq),
        out_shape=jax.ShapeDtypeStruct((b, seq, V_WIDTH), BF16),
        scratch_shapes=[pltpu.VMEM((n, 2, 2 * LANES, V_DIM), F32),
                        pltpu.VMEM((n, 2, 2 * LANES, V_DIM), BF16),
                        pltpu.VMEM((2, c, c), F32),
                        pltpu.VMEM((2, 2, c, LANES), F32)],
        compiler_params=_params(("parallel", "parallel")),
        name="retention",
    )(a_f, a_b, q, k, v, sg, kc, vc)


def _dft_a_kernel(w_ref, x_ref, y_ref):
    _, n1, r, w = x_ref.shape
    y = _dot(w_ref[...], x_ref[0].reshape(n1 * r, w))
    y_ref[0] = y.reshape(2, n1, r, w).astype(BF16)


def _dft_b_kernel(tw_ref, w2_ref, cc_ref, y_ref, z_ref, z_scr):
    n2 = y_ref.shape[3]
    cc = cc_ref[...].astype(BF16)
    w2c = w2_ref[0]
    w2s = w2_ref[1]
    for j in range(DFT_ROWS):
        tc = tw_ref[j, 0:1, :]
        ts = tw_ref[j, 1:2, :]
        ec = w2c * tc - w2s * ts
        es = w2s * tc + w2c * ts
        m = jnp.concatenate([jnp.concatenate([ec, -es], axis=1),
                             jnp.concatenate([es, ec], axis=1)], axis=0).astype(BF16)
        y = jnp.concatenate([y_ref[0, 0, j], y_ref[0, 1, j]], axis=0)
        zz = _dot(m, y)
        for g in range(F_GROUPS):
            cols = slice(g * F_GROUP_DIM, (g + 1) * F_GROUP_DIM)
            zcs = jnp.concatenate([zz[:n2, cols], zz[n2:, cols]], axis=1).astype(BF16)
            z_scr[g, pl.ds(j, n2, stride=DFT_ROWS), :] = _dot(zcs, cc)
    z = jnp.concatenate([z_scr[g] for g in range(F_GROUPS)], axis=1)
    z_ref[0] = z.reshape(n2, DFT_ROWS, F_WIDTH).astype(BF16)


def _dft_tables(seq):
    n1 = DFT_N1
    n2 = seq // n1

    def cs(num, den):
        ang = 2.0 * np.pi * (num % den) / den
        return np.cos(ang), np.sin(ang)

    a = np.arange(n1)
    w_a = np.concatenate(cs(a[:, None] * a[None, :], n1), axis=0)
    m = np.arange(n2)
    tw = np.stack(cs(a[:, None] * m[None, :], seq), axis=1)
    w2 = np.stack(cs(m[:, None] * m[None, :], n2), axis=0)
    ch = np.arange(F_GROUP_DIM)
    cc, sc = cs(ch[:, None] * ch[None, :], F_GROUP_DIM)
    scale = 1.0 / np.sqrt(seq * F_GROUP_DIM)
    w_c = np.concatenate([cc, -sc], axis=0) * scale
    return [jnp.asarray(t, dtype=F32) for t in (w_a, tw, w2, w_c)]


def _fourier(f):
    b, seq, w = f.shape
    n1 = DFT_N1
    n2 = seq // n1
    w_a, tw, w2, w_c = _dft_tables(seq)
    r = DFT_ROWS
    spread = (jnp.arange(n1 * r)[None, :] // r == jnp.arange(n1)[:, None]).astype(F32)
    w_rep = jnp.dot(w_a, spread, precision=lax.Precision.HIGHEST)
    same_r = jnp.arange(2 * n1 * r)[:, None] % r == jnp.arange(n1 * r)[None, :] % r
    w_a = jnp.where(same_r, jnp.repeat(w_rep, r, axis=0), 0.0).astype(BF16)
    y = pl.pallas_call(
        _dft_a_kernel,
        grid=(b, n2 // r),
        in_specs=[_const_spec(w_a.shape),
                  pl.BlockSpec((1, n1, r, w), lambda i, j: (i, 0, j, 0))],
        out_specs=pl.BlockSpec((1, 2, n1, r, w), lambda i, j: (i, 0, 0, j, 0)),
        out_shape=jax.ShapeDtypeStruct((b, 2, n1, n2, w), BF16),
        compiler_params=_params(("parallel", "parallel")),
        name="dft_a",
    )(w_a, f.reshape(b, n1, n2, w))
    z = pl.pallas_call(
        _dft_b_kernel,
        grid=(b, n1 // r),
        in_specs=[pl.BlockSpec((r, 2, n2), lambda i, j: (j, 0, 0)),
                  _const_spec((2, n2, n2)),
                  _const_spec((2 * F_GROUP_DIM, F_GROUP_DIM)),
                  pl.BlockSpec((1, 2, r, n2, w), lambda i, j: (i, 0, j, 0, 0))],
        out_specs=pl.BlockSpec((1, n2, r, w), lambda i, j: (i, 0, j, 0)),
        out_shape=jax.ShapeDtypeStruct((b, n2, n1, w), BF16),
        scratch_shapes=[pltpu.VMEM((F_GROUPS, n2 * r, F_GROUP_DIM), F32)],
        compiler_params=_params(("parallel", "parallel")),
        name="dft_b",
    )(tw, w2, w_c, y)
    return z.reshape(b, seq, w)


def _merge_kernel(x_ref, mod_ref, nw_ref, rg_ref, z_ref, wbg_ref, bbg_ref,
                  wro_ref, wfo_ref, wout_ref, o_ref):
    x = x_ref[...]
    h = _rms_norm(x, nw_ref[...]) * (1.0 + mod_ref[0, 1:2, :]) + mod_ref[0, 0:1, :]
    gates = jax.nn.sigmoid(_dot(h.astype(BF16), wbg_ref[...]) + bbg_ref[...])
    ret_d = _dot(rg_ref[...], wro_ref[...])
    four_d = _dot(z_ref[...], wfo_ref[...])
    m = gates[:, :D_MODEL] * ret_d + gates[:, D_MODEL:] * four_d
    y = _dot(m.astype(BF16), wout_ref[...])
    o_ref[...] = x + mod_ref[0, 2:3, :] * y


def _merge(x2, mod3, norm_w, rg, z, w_bg, b_bg, w_ro, w_fo, w_out, seq):
    t, d = x2.shape
    tl = TOKEN_TILE
    tpb = seq // tl
    tok = lambda w: pl.BlockSpec((tl, w), lambda i: (i, 0))
    return pl.pallas_call(
        _merge_kernel,
        grid=(t // tl,),
        in_specs=[tok(d),
                  pl.BlockSpec((1, N_MOD, d), lambda i: (i // tpb, 0, 0)),
                  _const_spec((1, d)),
                  tok(V_WIDTH), tok(F_WIDTH),
                  _const_spec(w_bg.shape), _const_spec((1, 2 * d)),
                  _const_spec(w_ro.shape), _const_spec(w_fo.shape), _const_spec(w_out.shape)],
        out_specs=tok(d),
        out_shape=jax.ShapeDtypeStruct((t, d), F32),
        compiler_params=_params(("parallel",)),
        name="merge",
    )(x2, mod3, norm_w, rg, z, w_bg, b_bg, w_ro, w_fo, w_out)


HALO = 8


def _two_gelu_tanh(x):
    c1 = np.sqrt(2.0 / np.pi)
    return x + x * jnp.tanh(x * (c1 + (c1 * 0.044715) * (x * x)))


def _ffn_kernel(xm_ref, xp_ref, xn_ref, mod_ref, nw_ref, wu_ref, cw_ref, cb_ref,
                wd_ref, fnw_ref, o_ref, h_scr, u_scr, act_scr, y_scr, *, tiles_per_seq):
    tl = xm_ref.shape[0]
    half = tl // 2
    nc = FFN_CHUNK
    i = pl.program_id(0)
    xm = xm_ref[...]
    keep_prev = ((i % tiles_per_seq) != 0).astype(F32)
    keep_next = ((i % tiles_per_seq) != tiles_per_seq - 1).astype(F32)

    def pre(x):
        return _rms_norm(x, nw_ref[...]) * (1.0 + mod_ref[0, 4:5, :]) + mod_ref[0, 3:4, :]

    h_scr[...] = jnp.concatenate([pre(xp_ref[...]) * keep_prev, pre(xm),
                                  pre(xn_ref[...]) * keep_next], axis=0).astype(BF16)

    def project(j):
        hb = h_scr[...]
        for part in range(2):
            lo = part * FFN_DIM + j * nc
            u = _dot(hb, wu_ref[:, lo:lo + nc])
            for s in range(nc // LANES):
                u_scr[j % 2, part, s] = u[:, s * LANES:(s + 1) * LANES]

    def conv(j, part, s, scale):
        lo = part * FFN_DIM + j * nc + s * LANES
        w = cw_ref[:, lo:lo + LANES] * scale
        bias = cb_ref[:, lo:lo + LANES] * scale
        rows = lambda start: u_scr[j % 2, part, s, pl.ds(start, half, stride=2), :]
        before, even, odd, after = rows(HALO - 1), rows(HALO), rows(HALO + 1), rows(HALO + 2)
        return (before * w[0:1] + even * w[1:2] + odd * w[2:3] + bias,
                even * w[0:1] + odd * w[1:2] + after * w[2:3] + bias)

    def activate(j):
        for s in range(nc // LANES):
            gate = conv(j, 0, s, 1.0)
            val = conv(j, 1, s, 0.5)
            cols = slice(j * nc + s * LANES, j * nc + (s + 1) * LANES)
            act_scr[0:half, cols] = (_two_gelu_tanh(gate[0]) * val[0]).astype(BF16)
            act_scr[half:tl, cols] = (_two_gelu_tanh(gate[1]) * val[1]).astype(BF16)

    project(0)
    for j in range(N_FFN_CHUNKS):
        if j + 1 < N_FFN_CHUNKS:
            project(j + 1)
        activate(j)
    y = _dot(act_scr[...], wd_ref[...])
    for s in range(y_scr.shape[0]):
        y_scr[s, pl.ds(0, half, stride=2), :] = y[0:half, s * LANES:(s + 1) * LANES]
        y_scr[s, pl.ds(1, half, stride=2), :] = y[half:tl, s * LANES:(s + 1) * LANES]
    y = jnp.concatenate([y_scr[s] for s in range(y_scr.shape[0])], axis=1)
    x2 = xm + mod_ref[0, 5:6, :] * y
    o_ref[...] = _rms_norm(x2, fnw_ref[...])


def _ffn(x1, mod3, norm_w, w_u, cw, cb, w_d, fnorm_w, seq):
    t, d = x1.shape
    tl = TOKEN_TILE
    tpb = seq // tl
    hb = tl // HALO
    last = t // HALO - 1
    return pl.pallas_call(
        functools.partial(_ffn_kernel, tiles_per_seq=tpb),
        grid=(t // tl,),
        in_specs=[pl.BlockSpec((tl, d), lambda i: (i, 0)),
                  pl.BlockSpec((HALO, d), lambda i: (jnp.maximum(i * hb - 1, 0), 0)),
                  pl.BlockSpec((HALO, d), lambda i: (jnp.minimum((i + 1) * hb, last), 0)),
                  pl.BlockSpec((1, N_MOD, d), lambda i: (i // tpb, 0, 0)),
                  _const_spec((1, d)),
                  _const_spec(w_u.shape), _const_spec(cw.shape), _const_spec(cb.shape),
                  _const_spec(w_d.shape), _const_spec((1, d))],
        out_specs=pl.BlockSpec((tl, d), lambda i: (i, 0)),
        out_shape=jax.ShapeDtypeStruct((t, d), F32),
        scratch_shapes=[pltpu.VMEM((tl + 2 * HALO, d), BF16),
                        pltpu.VMEM((2, 2, FFN_CHUNK // LANES, tl + 2 * HALO, LANES), F32),
                        pltpu.VMEM((tl, FFN_DIM), BF16),
                        pltpu.VMEM((d // LANES, tl, LANES), F32)],
        compiler_params=_params(("parallel",)),
        name="ffn",
    )(x1, x1, x1, mod3, norm_w, w_u, cw, cb, w_d, fnorm_w)


def _qk_perm():
    lane = np.arange(QK_WIDTH)
    pair, l = lane // LANES, lane % LANES
    head = 2 * pair + (l // (QK_DIM // 2)) % 2
    half = l // (LANES // 2)
    return head * QK_DIM + half * (QK_DIM // 2) + l % (QK_DIM // 2)


def _rope_tables(seq):
    pos = jnp.arange(seq)
    row = (pos // GRID_W).astype(F32)
    col = (pos % GRID_W).astype(F32)
    n_freq = QK_DIM // 4
    lane = jnp.arange(LANES)
    inv = ROPE_BASE ** (-(lane % n_freq).astype(F32) / n_freq)
    by_row = (lane % (2 * n_freq)) < n_freq
    ang = jnp.where(by_row[None, :], row[:, None], col[:, None]) * inv[None, :]
    sign = jnp.where(lane < LANES // 2, -1.0, 1.0).astype(F32)
    return jnp.cos(ang), jnp.sin(ang) * sign


def kernel(x, c, ctx, c_ctx, w_mod, b_mod, norm1_w, w_in, ret_decay_f, ret_decay_b,
           w_ret_out, w_four_out, w_branch_gate, b_branch_gate, w_out, norm2_w,
           w_up, conv_w, conv_b, w_down, final_norm_w):
    assert w_mod.shape[0] == 1, "single-layer block"
    b, seq, d = x.shape
    t = b * seq
    assert b == 2 and seq % TOKEN_TILE == 0 and seq % RET_CHUNK == 0

    c8 = jnp.concatenate([c, c_ctx[None, :], jnp.zeros((8 - b - 1, d), F32)], axis=0)
    mod3 = _modulation(c8, w_mod[0], b_mod[0]).reshape(8, N_MOD, d)

    perm = _qk_perm()
    w_in_b = w_in[0].astype(BF16)
    w_qk = jnp.take(w_in_b, np.concatenate([perm, K_OFF + perm]), axis=1)
    n1w = norm1_w[0].reshape(1, d)

    kc, vc = _ctx_proj(ctx, mod3, n1w, w_qk, w_in_b)
    cos, sin = _rope_tables(seq)
    x2 = x.reshape(t, d)
    q, k, v, sg, f = _proj(x2, mod3, n1w, w_qk, w_in_b, cos, sin, seq)

    a_f = jnp.broadcast_to(ret_decay_f[0][:, None, None], (HEADS, 1, LANES))
    a_b = jnp.broadcast_to(ret_decay_b[0][:, None, None], (HEADS, 1, LANES))
    rg = _retention(a_f, a_b, q.reshape(b, seq, QK_WIDTH), k.reshape(b, seq, QK_WIDTH),
                    v.reshape(b, seq, V_WIDTH), sg.reshape(b, seq, V_WIDTH), kc, vc)
    z = _fourier(f.reshape(b, seq, F_WIDTH))

    x1 = _merge(x2, mod3, n1w, rg.reshape(t, V_WIDTH), z.reshape(t, F_WIDTH),
                w_branch_gate[0].astype(BF16), b_branch_gate[0].reshape(1, 2 * d),
                w_ret_out[0].astype(BF16), w_four_out[0].astype(BF16), w_out[0].astype(BF16), seq)

    out = _ffn(x1, mod3, norm2_w[0].reshape(1, d), w_up[0].astype(BF16), conv_w[0],
               conv_b[0].reshape(1, 2 * FFN_DIM), w_down[0].astype(BF16),
               final_norm_w.reshape(1, d), seq)
    return out.reshape(b, seq, d)
```

```python
import functools

import numpy as np
import jax
import jax.numpy as jnp
from jax import lax
from jax.experimental import pallas as pl
from jax.experimental.pallas import tpu as pltpu

F32 = jnp.float32
BF16 = jnp.bfloat16

D_MODEL = 1024
GRID_W = 64
HEADS = 8
QK_DIM = 64
V_DIM = 128
QK_WIDTH = HEADS * QK_DIM
V_WIDTH = HEADS * V_DIM
ROPE_BASE = 10000.0
F_GROUPS = 4
F_GROUP_DIM = 128
F_WIDTH = F_GROUPS * F_GROUP_DIM
K_OFF = QK_WIDTH
V_OFF = K_OFF + QK_WIDTH
G_OFF = V_OFF + V_WIDTH
F_OFF = G_OFF + V_WIDTH
IN_COLS = F_OFF + F_WIDTH
FFN_DIM = 2816
N_MOD = 6
EPS = 1e-6

LANES = 128
RET_CHUNK = 256
RET_UNROLL = 8
FFN_CHUNK = 256
N_FFN_CHUNKS = FFN_DIM // FFN_CHUNK
TOKEN_TILE = 512
DFT_N1 = 64
DFT_ROWS = 16
VMEM_LIMIT = 56 * 1024 * 1024


def _params(sem):
    return pltpu.CompilerParams(dimension_semantics=sem, vmem_limit_bytes=VMEM_LIMIT)


def _dot(a, b):
    return jnp.dot(a, b, preferred_element_type=F32)


def _rms_norm(x, w):
    return x * lax.rsqrt(jnp.mean(x * x, axis=-1, keepdims=True) + EPS) * w


def _const_spec(shape):
    zeros = (0,) * len(shape)
    return pl.BlockSpec(shape, lambda *_: zeros)


def _mod_kernel(c_ref, w_ref, b_ref, o_ref):
    c = c_ref[...]
    s = c * jax.nn.sigmoid(c)
    o_ref[...] = jnp.dot(s, w_ref[...], preferred_element_type=F32,
                         precision=lax.Precision.HIGHEST) + b_ref[...]


def _modulation(c8, w_mod, b_mod):
    n = w_mod.shape[1]
    tn = 1536
    return pl.pallas_call(
        _mod_kernel,
        grid=(n // tn,),
        in_specs=[_const_spec((8, D_MODEL)),
                  pl.BlockSpec((D_MODEL, tn), lambda j: (0, j)),
                  pl.BlockSpec((1, tn), lambda j: (0, j))],
        out_specs=pl.BlockSpec((8, tn), lambda j: (0, j)),
        out_shape=jax.ShapeDtypeStruct((8, n), F32),
        compiler_params=_params(("parallel",)),
        name="mod",
    )(c8, w_mod, b_mod.reshape(1, n))


def _ctx_kernel(x_ref, mod_ref, nw_ref, wqk_ref, w_ref, k_ref, v_ref):
    x = x_ref[0]
    h = _rms_norm(x, nw_ref[...]) * (1.0 + mod_ref[0, 1:2, :]) + mod_ref[0, 0:1, :]
    hb = h.astype(BF16)
    k_ref[0] = _dot(hb, wqk_ref[:, QK_WIDTH:]).astype(BF16)
    v_ref[0] = _dot(hb, w_ref[:, V_OFF:G_OFF]).astype(BF16)


def _ctx_proj(ctx, mod3, norm_w, w_qk, w_in):
    b, lc, d = ctx.shape
    return pl.pallas_call(
        _ctx_kernel,
        grid=(b,),
        in_specs=[pl.BlockSpec((1, lc, d), lambda i: (i, 0, 0)),
                  pl.BlockSpec((1, N_MOD, d), lambda i: (2, 0, 0)),
                  _const_spec((1, d)),
                  _const_spec(w_qk.shape), _const_spec(w_in.shape)],
        out_specs=[pl.BlockSpec((1, lc, QK_WIDTH), lambda i: (i, 0, 0)),
                   pl.BlockSpec((1, lc, V_WIDTH), lambda i: (i, 0, 0))],
        out_shape=[jax.ShapeDtypeStruct((b, lc, QK_WIDTH), BF16),
                   jax.ShapeDtypeStruct((b, lc, V_WIDTH), BF16)],
        compiler_params=_params(("parallel",)),
        name="ctx_proj",
    )(ctx, mod3, norm_w, w_qk, w_in)


def _proj_kernel(x_ref, mod_ref, nw_ref, wqk_ref, w_ref, cos_ref, sin_ref,
                 q_ref, k_ref, v_ref, sg_ref, f_ref):
    x = x_ref[...]
    h = _rms_norm(x, nw_ref[...]) * (1.0 + mod_ref[0, 1:2, :]) + mod_ref[0, 0:1, :]
    hb = h.astype(BF16)
    cos = cos_ref[...]
    sin = sin_ref[...]

    def rope(t, scale, out_ref):
        for j in range(QK_WIDTH // LANES):
            tj = t[:, j * LANES:(j + 1) * LANES]
            r = tj * cos + pltpu.roll(tj, LANES // 2, 1) * sin
            out_ref[:, j * LANES:(j + 1) * LANES] = (r * scale).astype(BF16)

    rope(_dot(hb, wqk_ref[:, :QK_WIDTH]), QK_DIM ** -0.5, q_ref)
    rope(_dot(hb, wqk_ref[:, QK_WIDTH:]), 1.0, k_ref)
    v_ref[...] = _dot(hb, w_ref[:, V_OFF:G_OFF]).astype(BF16)
    g = _dot(hb, w_ref[:, G_OFF:F_OFF])
    sg_ref[...] = (g * jax.nn.sigmoid(g)).astype(BF16)
    f_ref[...] = _dot(hb, w_ref[:, F_OFF:IN_COLS]).astype(BF16)


def _proj(x2, mod3, norm_w, w_qk, w_in, cos, sin, seq):
    t, d = x2.shape
    tl = TOKEN_TILE
    tpb = seq // tl
    tok = lambda w: pl.BlockSpec((tl, w), lambda i: (i, 0))
    tab = pl.BlockSpec((tl, LANES), lambda i: (i % tpb, 0))
    return pl.pallas_call(
        _proj_kernel,
        grid=(t // tl,),
        in_specs=[tok(d),
                  pl.BlockSpec((1, N_MOD, d), lambda i: (i // tpb, 0, 0)),
                  _const_spec((1, d)),
                  _const_spec(w_qk.shape), _const_spec(w_in.shape),
                  tab, tab],
        out_specs=[tok(QK_WIDTH), tok(QK_WIDTH), tok(V_WIDTH), tok(V_WIDTH), tok(F_WIDTH)],
        out_shape=[jax.ShapeDtypeStruct((t, QK_WIDTH), BF16),
                   jax.ShapeDtypeStruct((t, QK_WIDTH), BF16),
                   jax.ShapeDtypeStruct((t, V_WIDTH), BF16),
                   jax.ShapeDtypeStruct((t, V_WIDTH), BF16),
                   jax.ShapeDtypeStruct((t, F_WIDTH), BF16)],
        compiler_params=_params(("parallel",)),
        name="proj",
    )(x2, mod3, norm_w, w_qk, w_in, cos, sin)


def _retention_kernel(af_ref, ab_ref, q_ref, k_ref, v_ref, kc_ref, vc_ref,
                      o_ref, ds_scr, st_scr, d_scr, tab_scr):
    c = RET_CHUNK
    seq = q_ref.shape[1]
    lc = kc_ref.shape[1]
    n = seq // c
    lgf = [-jnp.exp(af_ref[hh]) for hh in range(2)]
    lgb = [-jnp.exp(ab_ref[hh]) for hh in range(2)]
    half = QK_DIM // 2

    lane_head = (lax.broadcasted_iota(jnp.int32, (1, LANES), 1) // half) % 2
    masks = [(lane_head == hh).astype(BF16) for hh in range(2)]
    lgf_lane = jnp.where(lane_head == 0, lgf[0], lgf[1])
    lgb_lane = jnp.where(lane_head == 0, lgb[0], lgb[1])
    row_head = (lax.broadcasted_iota(jnp.int32, (LANES, 1), 0) // half) % 2
    lgf_row = jnp.where(row_head == 0, lgf[0][:, 0:1], lgf[1][:, 0:1])
    lgb_row = jnp.where(row_head == 0, lgb[0][:, 0:1], lgb[1][:, 0:1])

    pos = lax.broadcasted_iota(jnp.int32, (c, LANES), 0).astype(F32)
    tab_scr[0] = jnp.exp(lgf_lane * (pos + 1.0))
    tab_scr[1] = jnp.exp(lgb_lane * (c - pos))
    diff = (lax.broadcasted_iota(jnp.int32, (c, c), 0)
            - lax.broadcasted_iota(jnp.int32, (c, c), 1)).astype(F32)
    for hh in range(2):
        d_scr[hh] = (jnp.where(diff >= 0, jnp.exp(lgf[hh][:, 0:1] * jnp.maximum(diff, 0.0)), 0.0)
                     + jnp.where(diff <= 0, jnp.exp(lgb[hh][:, 0:1] * jnp.maximum(-diff, 0.0)), 0.0))

    def k_decays(tokens):
        t = lax.broadcasted_iota(jnp.int32, (1, tokens), 1).astype(F32)
        return jnp.exp(lgf_row * (tokens - 1.0 - t)), jnp.exp(lgb_row * t)

    def state_increment(k_rows, v_rows, decays):
        kt = jnp.transpose(k_rows.astype(F32))
        lhs = jnp.concatenate([kt * decays[0], kt * decays[1]], axis=0).astype(BF16)
        return _dot(lhs, v_rows)

    kdec = k_decays(c)

    def incr(i, carry):
        rows = pl.ds(pl.multiple_of(i * c, c), c)
        ds_scr[i] = state_increment(k_ref[0, rows, :], v_ref[0, rows, :], kdec)
        return carry

    lax.fori_loop(0, n, incr, 0, unroll=RET_UNROLL)

    s0 = state_increment(kc_ref[0], vc_ref[0], k_decays(lc))
    col_head = lax.broadcasted_iota(jnp.int32, (1, 2 * V_DIM), 1) // V_DIM
    own = (row_head == col_head).astype(F32)
    decay_f = jnp.exp(lgf_row * c) * own
    decay_b = jnp.exp(lgb_row * c) * own

    def scan_f(i, s):
        st_scr[i, 0:LANES, :] = (s * own).astype(BF16)
        return decay_f * s + ds_scr[i, 0:LANES, :]

    lax.fori_loop(0, n, scan_f, s0[0:LANES])

    def scan_b(t, s):
        i = n - 1 - t
        st_scr[i, LANES:2 * LANES, :] = (s * own).astype(BF16)
        return decay_b * s + ds_scr[i, LANES:2 * LANES, :]

    lax.fori_loop(0, n, scan_b, s0[LANES:2 * LANES])

    def outputs(i, carry):
        rows = pl.ds(pl.multiple_of(i * c, c), c)
        q = q_ref[0, rows, :]
        k = k_ref[0, rows, :]
        qf = q.astype(F32)
        qd = jnp.concatenate([qf * tab_scr[0], qf * tab_scr[1]], axis=1).astype(BF16)
        inter = _dot(qd, st_scr[i])
        for hh in range(2):
            cols = slice(hh * V_DIM, (hh + 1) * V_DIM)
            scores = lax.dot_general(q * masks[hh], k, (((1,), (1,)), ((), ())),
                                     preferred_element_type=F32)
            o = _dot((scores * d_scr[hh]).astype(BF16), v_ref[0, rows, cols]) + inter[:, cols]
            o_ref[0, rows, cols] = o.astype(BF16)
        return carry

    lax.fori_loop(0, n, outputs, 0, unroll=RET_UNROLL)


def _retention(a_f, a_b, q, k, v, kc, vc):
    b, seq, _ = v.shape
    lc = kc.shape[1]
    c = RET_CHUNK
    n = seq // c
    dec = pl.BlockSpec((2, 1, LANES), lambda i, p: (p, 0, 0))
    qk = lambda rows: pl.BlockSpec((1, rows, LANES), lambda i, p: (i, 0, p))
    vv = lambda rows: pl.BlockSpec((1, rows, 2 * V_DIM), lambda i, p: (i, 0, p))
    return pl.pallas_call(
        _retention_kernel,
        grid=(b, HEADS // 2),
        in_specs=[dec, dec, qk(seq), qk(seq), vv(seq), qk(lc), vv(lc)],
        out_specs=vv(seq),
        out_shape=jax.ShapeDtypeStruct((b, seq, V_WIDTH), BF16),
        scratch_shapes=[pltpu.VMEM((n, 2 * LANES, 2 * V_DIM), F32),
                        pltpu.VMEM((n, 2 * LANES, 2 * V_DIM), BF16),
                        pltpu.VMEM((2, c, c), F32),
                        pltpu.VMEM((2, c, LANES), F32)],
        compiler_params=_params(("parallel", "parallel")),
        name="retention",
    )(a_f, a_b, q, k, v, kc, vc)


def _dft_a_kernel(w_ref, x_ref, y_ref):
    _, n1, r, w = x_ref.shape
    y = _dot(w_ref[...], x_ref[0].reshape(n1 * r, w))
    y_ref[0] = y.reshape(2, n1, r, w).astype(BF16)


def _dft_b_kernel(tw_ref, w2_ref, cc_ref, y_ref, z_ref, z_scr):
    n2 = y_ref.shape[3]
    cc = cc_ref[...].astype(BF16)
    w2c = w2_ref[0]
    w2s = w2_ref[1]
    for j in range(DFT_ROWS):
        tc = tw_ref[j, 0:1, :]
        ts = tw_ref[j, 1:2, :]
        ec = w2c * tc - w2s * ts
        es = w2s * tc + w2c * ts
        m = jnp.concatenate([jnp.concatenate([ec, -es], axis=1),
                             jnp.concatenate([es, ec], axis=1)], axis=0).astype(BF16)
        y = jnp.concatenate([y_ref[0, 0, j], y_ref[0, 1, j]], axis=0)
        zz = _dot(m, y)
        for g in range(F_GROUPS):
            cols = slice(g * F_GROUP_DIM, (g + 1) * F_GROUP_DIM)
            zcs = jnp.concatenate([zz[:n2, cols], zz[n2:, cols]], axis=1).astype(BF16)
            z_scr[g, pl.ds(j, n2, stride=DFT_ROWS), :] = _dot(zcs, cc)
    z = jnp.concatenate([z_scr[g] for g in range(F_GROUPS)], axis=1)
    z_ref[0] = z.reshape(n2, DFT_ROWS, F_WIDTH).astype(BF16)


def _dft_tables(seq):
    n1 = DFT_N1
    n2 = seq // n1

    def cs(num, den):
        ang = 2.0 * np.pi * (num % den) / den
        return np.cos(ang), np.sin(ang)

    a = np.arange(n1)
    w_a = np.concatenate(cs(a[:, None] * a[None, :], n1), axis=0)
    m = np.arange(n2)
    tw = np.stack(cs(a[:, None] * m[None, :], seq), axis=1)
    w2 = np.stack(cs(m[:, None] * m[None, :], n2), axis=0)
    ch = np.arange(F_GROUP_DIM)
    cc, sc = cs(ch[:, None] * ch[None, :], F_GROUP_DIM)
    scale = 1.0 / np.sqrt(seq * F_GROUP_DIM)
    w_c = np.concatenate([cc, -sc], axis=0) * scale
    return [jnp.asarray(t, dtype=F32) for t in (w_a, tw, w2, w_c)]


def _fourier(f):
    b, seq, w = f.shape
    n1 = DFT_N1
    n2 = seq // n1
    w_a, tw, w2, w_c = _dft_tables(seq)
    r = DFT_ROWS
    spread = (jnp.arange(n1 * r)[None, :] // r == jnp.arange(n1)[:, None]).astype(F32)
    w_rep = jnp.dot(w_a, spread, precision=lax.Precision.HIGHEST)
    same_r = jnp.arange(2 * n1 * r)[:, None] % r == jnp.arange(n1 * r)[None, :] % r
    w_a = jnp.where(same_r, jnp.repeat(w_rep, r, axis=0), 0.0).astype(BF16)
    y = pl.pallas_call(
        _dft_a_kernel,
        grid=(b, n2 // r),
        in_specs=[_const_spec(w_a.shape),
                  pl.BlockSpec((1, n1, r, w), lambda i, j: (i, 0, j, 0))],
        out_specs=pl.BlockSpec((1, 2, n1, r, w), lambda i, j: (i, 0, 0, j, 0)),
        out_shape=jax.ShapeDtypeStruct((b, 2, n1, n2, w), BF16),
        compiler_params=_params(("parallel", "parallel")),
        name="dft_a",
    )(w_a, f.reshape(b, n1, n2, w))
    z = pl.pallas_call(
        _dft_b_kernel,
        grid=(b, n1 // r),
        in_specs=[pl.BlockSpec((r, 2, n2), lambda i, j: (j, 0, 0)),
                  _const_spec((2, n2, n2)),
                  _const_spec((2 * F_GROUP_DIM, F_GROUP_DIM)),
                  pl.BlockSpec((1, 2, r, n2, w), lambda i, j: (i, 0, j, 0, 0))],
        out_specs=pl.BlockSpec((1, n2, r, w), lambda i, j: (i, 0, j, 0)),
        out_shape=jax.ShapeDtypeStruct((b, n2, n1, w), BF16),
        scratch_shapes=[pltpu.VMEM((F_GROUPS, n2 * r, F_GROUP_DIM), F32)],
        compiler_params=_params(("parallel", "parallel")),
        name="dft_b",
    )(tw, w2, w_c, y)
    return z.reshape(b, seq, w)


def _merge_kernel(x_ref, mod_ref, nw_ref, ro_ref, sg_ref, z_ref, wbg_ref, bbg_ref,
                  wro_ref, wfo_ref, wout_ref, o_ref):
    x = x_ref[...]
    h = _rms_norm(x, nw_ref[...]) * (1.0 + mod_ref[0, 1:2, :]) + mod_ref[0, 0:1, :]
    gates = jax.nn.sigmoid(_dot(h.astype(BF16), wbg_ref[...]) + bbg_ref[...])
    gated = []
    for hd in range(HEADS):
        cols = slice(hd * V_DIM, (hd + 1) * V_DIM)
        o = ro_ref[:, cols].astype(F32)
        oc = o - jnp.mean(o, axis=-1, keepdims=True)
        var = jnp.mean(oc * oc, axis=-1, keepdims=True)
        gated.append((oc * lax.rsqrt(var + EPS) * sg_ref[:, cols].astype(F32)).astype(BF16))
    ret_d = _dot(jnp.concatenate(gated, axis=1), wro_ref[...])
    four_d = _dot(z_ref[...], wfo_ref[...])
    m = gates[:, :D_MODEL] * ret_d + gates[:, D_MODEL:] * four_d
    y = _dot(m.astype(BF16), wout_ref[...])
    o_ref[...] = x + mod_ref[0, 2:3, :] * y


def _merge(x2, mod3, norm_w, ro, sg, z, w_bg, b_bg, w_ro, w_fo, w_out, seq):
    t, d = x2.shape
    tl = TOKEN_TILE
    tpb = seq // tl
    tok = lambda w: pl.BlockSpec((tl, w), lambda i: (i, 0))
    return pl.pallas_call(
        _merge_kernel,
        grid=(t // tl,),
        in_specs=[tok(d),
                  pl.BlockSpec((1, N_MOD, d), lambda i: (i // tpb, 0, 0)),
                  _const_spec((1, d)),
                  tok(V_WIDTH), tok(V_WIDTH), tok(F_WIDTH),
                  _const_spec(w_bg.shape), _const_spec((1, 2 * d)),
                  _const_spec(w_ro.shape), _const_spec(w_fo.shape), _const_spec(w_out.shape)],
        out_specs=tok(d),
        out_shape=jax.ShapeDtypeStruct((t, d), F32),
        compiler_params=_params(("parallel",)),
        name="merge",
    )(x2, mod3, norm_w, ro, sg, z, w_bg, b_bg, w_ro, w_fo, w_out)


HALO = 8


def _two_gelu_tanh(x):
    c1 = np.sqrt(2.0 / np.pi)
    return x + x * jnp.tanh(x * (c1 + (c1 * 0.044715) * (x * x)))


def _ffn_kernel(xm_ref, xp_ref, xn_ref, mod_ref, nw_ref, wu_ref, cw_ref, cb_ref,
                wd_ref, fnw_ref, o_ref, h_scr, u_scr, act_scr, y_scr, *, tiles_per_seq):
    tl = xm_ref.shape[0]
    half = tl // 2
    nc = FFN_CHUNK
    i = pl.program_id(0)
    xm = xm_ref[...]
    keep_prev = ((i % tiles_per_seq) != 0).astype(F32)
    keep_next = ((i % tiles_per_seq) != tiles_per_seq - 1).astype(F32)

    def pre(x):
        return _rms_norm(x, nw_ref[...]) * (1.0 + mod_ref[0, 4:5, :]) + mod_ref[0, 3:4, :]

    h_scr[...] = jnp.concatenate([pre(xp_ref[...]) * keep_prev, pre(xm),
                                  pre(xn_ref[...]) * keep_next], axis=0).astype(BF16)

    def project(j):
        hb = h_scr[...]
        for part in range(2):
            lo = part * FFN_DIM + j * nc
            u = _dot(hb, wu_ref[:, lo:lo + nc])
            for s in range(nc // LANES):
                u_scr[j % 2, part, s] = u[:, s * LANES:(s + 1) * LANES]

    def conv(j, part, s, scale):
        lo = part * FFN_DIM + j * nc + s * LANES
        w = cw_ref[:, lo:lo + LANES] * scale
        bias = cb_ref[:, lo:lo + LANES] * scale
        rows = lambda start: u_scr[j % 2, part, s, pl.ds(start, half, stride=2), :]
        before, even, odd, after = rows(HALO - 1), rows(HALO), rows(HALO + 1), rows(HALO + 2)
        return (before * w[0:1] + even * w[1:2] + odd * w[2:3] + bias,
                even * w[0:1] + odd * w[1:2] + after * w[2:3] + bias)

    def activate(j):
        for s in range(nc // LANES):
            gate = conv(j, 0, s, 1.0)
            val = conv(j, 1, s, 0.5)
            cols = slice(j * nc + s * LANES, j * nc + (s + 1) * LANES)
            act_scr[0:half, cols] = (_two_gelu_tanh(gate[0]) * val[0]).astype(BF16)
            act_scr[half:tl, cols] = (_two_gelu_tanh(gate[1]) * val[1]).astype(BF16)

    project(0)
    for j in range(N_FFN_CHUNKS):
        if j + 1 < N_FFN_CHUNKS:
            project(j + 1)
        activate(j)
    y = _dot(act_scr[...], wd_ref[...])
    for s in range(y_scr.shape[0]):
        y_scr[s, pl.ds(0, half, stride=2), :] = y[0:half, s * LANES:(s + 1) * LANES]
        y_scr[s, pl.ds(1, half, stride=2), :] = y[half:tl, s * LANES:(s + 1) * LANES]
    y = jnp.concatenate([y_scr[s] for s in range(y_scr.shape[0])], axis=1)
    x2 = xm + mod_ref[0, 5:6, :] * y
    o_ref[...] = _rms_norm(x2, fnw_ref[...])


def _ffn(x1, mod3, norm_w, w_u, cw, cb, w_d, fnorm_w, seq):
    t, d = x1.shape
    tl = TOKEN_TILE
    tpb = seq // tl
    hb = tl // HALO
    last = t // HALO - 1
    return pl.pallas_call(
        functools.partial(_ffn_kernel, tiles_per_seq=tpb),
        grid=(t // tl,),
        in_specs=[pl.BlockSpec((tl, d), lambda i: (i, 0)),
                  pl.BlockSpec((HALO, d), lambda i: (jnp.maximum(i * hb - 1, 0), 0)),
                  pl.BlockSpec((HALO, d), lambda i: (jnp.minimum((i + 1) * hb, last), 0)),
                  pl.BlockSpec((1, N_MOD, d), lambda i: (i // tpb, 0, 0)),
                  _const_spec((1, d)),
                  _const_spec(w_u.shape), _const_spec(cw.shape), _const_spec(cb.shape),
                  _const_spec(w_d.shape), _const_spec((1, d))],
        out_specs=pl.BlockSpec((tl, d), lambda i: (i, 0)),
        out_shape=jax.ShapeDtypeStruct((t, d), F32),
        scratch_shapes=[pltpu.VMEM((tl + 2 * HALO, d), BF16),
                        pltpu.VMEM((2, 2, FFN_CHUNK // LANES, tl + 2 * HALO, LANES), F32),
                        pltpu.VMEM((tl, FFN_DIM), BF16),
                        pltpu.VMEM((d // LANES, tl, LANES), F32)],
        compiler_params=_params(("parallel",)),
        name="ffn",
    )(x1, x1, x1, mod3, norm_w, w_u, cw, cb, w_d, fnorm_w)


def _qk_perm():
    lane = np.arange(QK_WIDTH)
    pair, l = lane // LANES, lane % LANES
    head = 2 * pair + (l // (QK_DIM // 2)) % 2
    half = l // (LANES // 2)
    return head * QK_DIM + half * (QK_DIM // 2) + l % (QK_DIM // 2)


def _rope_tables(seq):
    pos = jnp.arange(seq)
    row = (pos // GRID_W).astype(F32)
    col = (pos % GRID_W).astype(F32)
    n_freq = QK_DIM // 4
    lane = jnp.arange(LANES)
    inv = ROPE_BASE ** (-(lane % n_freq).astype(F32) / n_freq)
    by_row = (lane % (2 * n_freq)) < n_freq
    ang = jnp.where(by_row[None, :], row[:, None], col[:, None]) * inv[None, :]
    sign = jnp.where(lane < LANES // 2, -1.0, 1.0).astype(F32)
    return jnp.cos(ang), jnp.sin(ang) * sign


def kernel(x, c, ctx, c_ctx, w_mod, b_mod, norm1_w, w_in, ret_decay_f, ret_decay_b,
           w_ret_out, w_four_out, w_branch_gate, b_branch_gate, w_out, norm2_w,
           w_up, conv_w, conv_b, w_down, final_norm_w):
    assert w_mod.shape[0] == 1, "single-layer block"
    b, seq, d = x.shape
    t = b * seq
    assert b == 2 and seq % TOKEN_TILE == 0 and seq % RET_CHUNK == 0

    c8 = jnp.concatenate([c, c_ctx[None, :], jnp.zeros((8 - b - 1, d), F32)], axis=0)
    mod3 = _modulation(c8, w_mod[0], b_mod[0]).reshape(8, N_MOD, d)

    perm = _qk_perm()
    w_in_b = w_in[0].astype(BF16)
    w_qk = jnp.take(w_in_b, np.concatenate([perm, K_OFF + perm]), axis=1)
    n1w = norm1_w[0].reshape(1, d)

    kc, vc = _ctx_proj(ctx, mod3, n1w, w_qk, w_in_b)
    cos, sin = _rope_tables(seq)
    x2 = x.reshape(t, d)
    q, k, v, sg, f = _proj(x2, mod3, n1w, w_qk, w_in_b, cos, sin, seq)

    a_f = jnp.broadcast_to(ret_decay_f[0][:, None, None], (HEADS, 1, LANES))
    a_b = jnp.broadcast_to(ret_decay_b[0][:, None, None], (HEADS, 1, LANES))
    ro = _retention(a_f, a_b, q.reshape(b, seq, QK_WIDTH), k.reshape(b, seq, QK_WIDTH),
                    v.reshape(b, seq, V_WIDTH), kc, vc)
    z = _fourier(f.reshape(b, seq, F_WIDTH))

    x1 = _merge(x2, mod3, n1w, ro.reshape(t, V_WIDTH), sg, z.reshape(t, F_WIDTH),
                w_branch_gate[0].astype(BF16), b_branch_gate[0].reshape(1, 2 * d),
                w_ret_out[0].astype(BF16), w_four_out[0].astype(BF16), w_out[0].astype(BF16), seq)

    out = _ffn(x1, mod3, norm2_w[0].reshape(1, d), w_up[0].astype(BF16), conv_w[0],
               conv_b[0].reshape(1, 2 * FFN_DIM), w_down[0].astype(BF16),
               final_norm_w.reshape(1, d), seq)
    return out.reshape(b, seq, d)
```

```python
import functools

import numpy as np
import jax
import jax.numpy as jnp
from jax import lax
from jax.experimental import pallas as pl
from jax.experimental.pallas import tpu as pltpu

F32 = jnp.float32
BF16 = jnp.bfloat16

D_MODEL = 1024
GRID_W = 64
HEADS = 8
QK_DIM = 64
V_DIM = 128
QK_WIDTH = HEADS * QK_DIM
V_WIDTH = HEADS * V_DIM
ROPE_BASE = 10000.0
F_GROUPS = 4
F_GROUP_DIM = 128
F_WIDTH = F_GROUPS * F_GROUP_DIM
K_OFF = QK_WIDTH
V_OFF = K_OFF + QK_WIDTH
G_OFF = V_OFF + V_WIDTH
F_OFF = G_OFF + V_WIDTH
IN_COLS = F_OFF + F_WIDTH
FFN_DIM = 2816
N_MOD = 6
EPS = 1e-6

LANES = 128
RET_CHUNK = 256
RET_UNROLL = 8
FFN_CHUNK = 256
N_FFN_CHUNKS = FFN_DIM // FFN_CHUNK
TOKEN_TILE = 512
DFT_N1 = 64
DFT_ROWS = 16
VMEM_LIMIT = 56 * 1024 * 1024


def _params(sem):
    return pltpu.CompilerParams(dimension_semantics=sem, vmem_limit_bytes=VMEM_LIMIT)


def _dot(a, b):
    return jnp.dot(a, b, preferred_element_type=F32)


def _rms_norm(x, w):
    return x * lax.rsqrt(jnp.mean(x * x, axis=-1, keepdims=True) + EPS) * w


def _const_spec(shape):
    zeros = (0,) * len(shape)
    return pl.BlockSpec(shape, lambda *_: zeros, pipeline_mode=pl.Buffered(1))


def _mod_kernel(c_ref, w_ref, b_ref, o_ref):
    c = c_ref[...]
    s = c * jax.nn.sigmoid(c)
    o_ref[...] = _dot(s.astype(BF16), w_ref[...].astype(BF16)) + b_ref[...]


def _modulation(c8, w_mod, b_mod):
    n = w_mod.shape[1]
    tn = 1536
    return pl.pallas_call(
        _mod_kernel,
        grid=(n // tn,),
        in_specs=[_const_spec((8, D_MODEL)),
                  pl.BlockSpec((D_MODEL, tn), lambda j: (0, j)),
                  pl.BlockSpec((1, tn), lambda j: (0, j))],
        out_specs=pl.BlockSpec((8, tn), lambda j: (0, j)),
        out_shape=jax.ShapeDtypeStruct((8, n), F32),
        compiler_params=_params(("parallel",)),
        name="mod",
    )(c8, w_mod, b_mod.reshape(1, n))


def _ctx_kernel(x_ref, mod_ref, nw_ref, wqk_ref, w_ref, k_ref, v_ref):
    x = x_ref[0]
    h = _rms_norm(x, nw_ref[...]) * (1.0 + mod_ref[0, 1:2, :]) + mod_ref[0, 0:1, :]
    hb = h.astype(BF16)
    k_ref[0] = _dot(hb, wqk_ref[:, QK_WIDTH:]).astype(BF16)
    v_ref[0] = _dot(hb, w_ref[:, V_OFF:G_OFF]).astype(BF16)


def _ctx_proj(ctx, mod3, norm_w, w_qk, w_in):
    b, lc, d = ctx.shape
    return pl.pallas_call(
        _ctx_kernel,
        grid=(b,),
        in_specs=[pl.BlockSpec((1, lc, d), lambda i: (i, 0, 0)),
                  pl.BlockSpec((1, N_MOD, d), lambda i: (2, 0, 0)),
                  _const_spec((1, d)),
                  _const_spec(w_qk.shape), _const_spec(w_in.shape)],
        out_specs=[pl.BlockSpec((1, lc, QK_WIDTH), lambda i: (i, 0, 0)),
                   pl.BlockSpec((1, lc, V_WIDTH), lambda i: (i, 0, 0))],
        out_shape=[jax.ShapeDtypeStruct((b, lc, QK_WIDTH), BF16),
                   jax.ShapeDtypeStruct((b, lc, V_WIDTH), BF16)],
        compiler_params=_params(("parallel",)),
        name="ctx_proj",
    )(ctx, mod3, norm_w, w_qk, w_in)


def _proj_kernel(x_ref, mod_ref, nw_ref, wqk_ref, w_ref, cos_ref, sin_ref,
                 q_ref, k_ref, v_ref, sg_ref, f_ref):
    x = x_ref[...]
    h = _rms_norm(x, nw_ref[...]) * (1.0 + mod_ref[0, 1:2, :]) + mod_ref[0, 0:1, :]
    hb = h.astype(BF16)
    cos = cos_ref[...]
    sin = sin_ref[...]

    def rope(t, scale, out_ref):
        for j in range(QK_WIDTH // LANES):
            tj = t[:, j * LANES:(j + 1) * LANES]
            r = tj * cos + pltpu.roll(tj, LANES // 2, 1) * sin
            out_ref[:, j * LANES:(j + 1) * LANES] = (r * scale).astype(BF16)

    rope(_dot(hb, wqk_ref[:, :QK_WIDTH]), QK_DIM ** -0.5, q_ref)
    rope(_dot(hb, wqk_ref[:, QK_WIDTH:]), 1.0, k_ref)
    v_ref[...] = _dot(hb, w_ref[:, V_OFF:G_OFF]).astype(BF16)
    g = _dot(hb, w_ref[:, G_OFF:F_OFF])
    sg_ref[...] = (g * jax.nn.sigmoid(g)).astype(BF16)
    f_ref[...] = _dot(hb, w_ref[:, F_OFF:IN_COLS]).astype(BF16)


def _proj(x2, mod3, norm_w, w_qk, w_in, cos, sin, seq):
    t, d = x2.shape
    tl = TOKEN_TILE
    tpb = seq // tl
    tok = lambda w: pl.BlockSpec((tl, w), lambda i: (i, 0))
    tab = pl.BlockSpec((tl, LANES), lambda i: (i % tpb, 0))
    return pl.pallas_call(
        _proj_kernel,
        grid=(t // tl,),
        in_specs=[tok(d),
                  pl.BlockSpec((1, N_MOD, d), lambda i: (i // tpb, 0, 0)),
                  _const_spec((1, d)),
                  _const_spec(w_qk.shape), _const_spec(w_in.shape),
                  tab, tab],
        out_specs=[tok(QK_WIDTH), tok(QK_WIDTH), tok(V_WIDTH), tok(V_WIDTH), tok(F_WIDTH)],
        out_shape=[jax.ShapeDtypeStruct((t, QK_WIDTH), BF16),
                   jax.ShapeDtypeStruct((t, QK_WIDTH), BF16),
                   jax.ShapeDtypeStruct((t, V_WIDTH), BF16),
                   jax.ShapeDtypeStruct((t, V_WIDTH), BF16),
                   jax.ShapeDtypeStruct((t, F_WIDTH), BF16)],
        compiler_params=_params(("parallel",)),
        name="proj",
    )(x2, mod3, norm_w, w_qk, w_in, cos, sin)


def _retention_kernel(af_ref, ab_ref, q_ref, k_ref, v_ref, kc_ref, vc_ref,
                      o_ref, ds_scr, st_scr, d_scr, tab_scr):
    c = RET_CHUNK
    seq = q_ref.shape[1]
    lc = kc_ref.shape[1]
    n = seq // c
    lgf = [-jnp.exp(af_ref[hh]) for hh in range(2)]
    lgb = [-jnp.exp(ab_ref[hh]) for hh in range(2)]
    half = QK_DIM // 2

    lane_head = (lax.broadcasted_iota(jnp.int32, (1, LANES), 1) // half) % 2
    masks = [(lane_head == hh).astype(BF16) for hh in range(2)]
    lgf_lane = jnp.where(lane_head == 0, lgf[0], lgf[1])
    lgb_lane = jnp.where(lane_head == 0, lgb[0], lgb[1])
    row_head = (lax.broadcasted_iota(jnp.int32, (LANES, 1), 0) // half) % 2
    lgf_row = jnp.where(row_head == 0, lgf[0][:, 0:1], lgf[1][:, 0:1])
    lgb_row = jnp.where(row_head == 0, lgb[0][:, 0:1], lgb[1][:, 0:1])

    pos = lax.broadcasted_iota(jnp.int32, (c, LANES), 0).astype(F32)
    tab_scr[0] = jnp.exp(lgf_lane * (pos + 1.0))
    tab_scr[1] = jnp.exp(lgb_lane * (c - pos))
    diff = (lax.broadcasted_iota(jnp.int32, (c, c), 0)
            - lax.broadcasted_iota(jnp.int32, (c, c), 1)).astype(F32)
    for hh in range(2):
        d_scr[hh] = (jnp.where(diff >= 0, jnp.exp(lgf[hh][:, 0:1] * jnp.maximum(diff, 0.0)), 0.0)
                     + jnp.where(diff <= 0, jnp.exp(lgb[hh][:, 0:1] * jnp.maximum(-diff, 0.0)), 0.0))

    def k_decays(tokens):
        t = lax.broadcasted_iota(jnp.int32, (1, tokens), 1).astype(F32)
        return jnp.exp(lgf_row * (tokens - 1.0 - t)), jnp.exp(lgb_row * t)

    def state_increment(k_rows, v_rows, decays):
        kt = jnp.transpose(k_rows.astype(F32))
        lhs = jnp.concatenate([kt * decays[0], kt * decays[1]], axis=0).astype(BF16)
        return _dot(lhs, v_rows)

    kdec = k_decays(c)

    def incr(i, carry):
        rows = pl.ds(pl.multiple_of(i * c, c), c)
        ds_scr[i] = state_increment(k_ref[0, rows, :], v_ref[0, rows, :], kdec)
        return carry

    lax.fori_loop(0, n, incr, 0, unroll=RET_UNROLL)

    s0 = state_increment(kc_ref[0], vc_ref[0], k_decays(lc))
    col_head = lax.broadcasted_iota(jnp.int32, (1, 2 * V_DIM), 1) // V_DIM
    own = (row_head == col_head).astype(F32)
    decay_f = jnp.exp(lgf_row * c) * own
    decay_b = jnp.exp(lgb_row * c) * own

    def scan_f(i, s):
        st_scr[i, 0:LANES, :] = (s * own).astype(BF16)
        return decay_f * s + ds_scr[i, 0:LANES, :]

    lax.fori_loop(0, n, scan_f, s0[0:LANES])

    def scan_b(t, s):
        i = n - 1 - t
        st_scr[i, LANES:2 * LANES, :] = (s * own).astype(BF16)
        return decay_b * s + ds_scr[i, LANES:2 * LANES, :]

    lax.fori_loop(0, n, scan_b, s0[LANES:2 * LANES])

    def outputs(i, carry):
        rows = pl.ds(pl.multiple_of(i * c, c), c)
        q = q_ref[0, rows, :]
        k = k_ref[0, rows, :]
        qf = q.astype(F32)
        qd = jnp.concatenate([qf * tab_scr[0], qf * tab_scr[1]], axis=1).astype(BF16)
        inter = _dot(qd, st_scr[i])
        for hh in range(2):
            cols = slice(hh * V_DIM, (hh + 1) * V_DIM)
            scores = lax.dot_general(q * masks[hh], k, (((1,), (1,)), ((), ())),
                                     preferred_element_type=F32)
            o = _dot((scores * d_scr[hh]).astype(BF16), v_ref[0, rows, cols]) + inter[:, cols]
            o_ref[0, rows, cols] = o.astype(BF16)
        return carry

    lax.fori_loop(0, n, outputs, 0, unroll=RET_UNROLL)


def _retention(a_f, a_b, q, k, v, kc, vc):
    b, seq, _ = v.shape
    lc = kc.shape[1]
    c = RET_CHUNK
    n = seq // c
    dec = pl.BlockSpec((2, 1, LANES), lambda i, p: (p, 0, 0))
    qk = lambda rows: pl.BlockSpec((1, rows, LANES), lambda i, p: (i, 0, p))
    vv = lambda rows: pl.BlockSpec((1, rows, 2 * V_DIM), lambda i, p: (i, 0, p))
    return pl.pallas_call(
        _retention_kernel,
        grid=(b, HEADS // 2),
        in_specs=[dec, dec, qk(seq), qk(seq), vv(seq), qk(lc), vv(lc)],
        out_specs=vv(seq),
        out_shape=jax.ShapeDtypeStruct((b, seq, V_WIDTH), BF16),
        scratch_shapes=[pltpu.VMEM((n, 2 * LANES, 2 * V_DIM), F32),
                        pltpu.VMEM((n, 2 * LANES, 2 * V_DIM), BF16),
                        pltpu.VMEM((2, c, c), F32),
                        pltpu.VMEM((2, c, LANES), F32)],
        compiler_params=_params(("parallel", "parallel")),
        name="retention",
    )(a_f, a_b, q, k, v, kc, vc)


def _dft_a_kernel(w_ref, x_ref, y_ref):
    _, n1, r, w = x_ref.shape
    y = _dot(w_ref[...], x_ref[0].reshape(n1 * r, w))
    y_ref[0] = y.reshape(2, n1, r, w).astype(BF16)


def _dft_b_kernel(tw_ref, w2_ref, cc_ref, y_ref, z_ref, z_scr):
    n2 = y_ref.shape[3]
    cc = cc_ref[...].astype(BF16)
    w2c = w2_ref[0]
    w2s = w2_ref[1]
    for j in range(DFT_ROWS):
        tc = tw_ref[j, 0:1, :]
        ts = tw_ref[j, 1:2, :]
        ec = w2c * tc - w2s * ts
        es = w2s * tc + w2c * ts
        m = jnp.concatenate([jnp.concatenate([ec, -es], axis=1),
                             jnp.concatenate([es, ec], axis=1)], axis=0).astype(BF16)
        y = jnp.concatenate([y_ref[0, 0, j], y_ref[0, 1, j]], axis=0)
        zz = _dot(m, y)
        for g in range(F_GROUPS):
            cols = slice(g * F_GROUP_DIM, (g + 1) * F_GROUP_DIM)
            zcs = jnp.concatenate([zz[:n2, cols], zz[n2:, cols]], axis=1).astype(BF16)
            z_scr[g, pl.ds(j, n2, stride=DFT_ROWS), :] = _dot(zcs, cc)
    z = jnp.concatenate([z_scr[g] for g in range(F_GROUPS)], axis=1)
    z_ref[0] = z.reshape(n2, DFT_ROWS, F_WIDTH).astype(BF16)


def _dft_tables(seq):
    n1 = DFT_N1
    n2 = seq // n1

    def cs(num, den):
        ang = 2.0 * np.pi * (num % den) / den
        return np.cos(ang), np.sin(ang)

    a = np.arange(n1)
    w_a = np.concatenate(cs(a[:, None] * a[None, :], n1), axis=0)
    m = np.arange(n2)
    tw = np.stack(cs(a[:, None] * m[None, :], seq), axis=1)
    w2 = np.stack(cs(m[:, None] * m[None, :], n2), axis=0)
    ch = np.arange(F_GROUP_DIM)
    cc, sc = cs(ch[:, None] * ch[None, :], F_GROUP_DIM)
    scale = 1.0 / np.sqrt(seq * F_GROUP_DIM)
    w_c = np.concatenate([cc, -sc], axis=0) * scale
    return [jnp.asarray(t, dtype=F32) for t in (w_a, tw, w2, w_c)]


def _fourier(f):
    b, seq, w = f.shape
    n1 = DFT_N1
    n2 = seq // n1
    w_a, tw, w2, w_c = _dft_tables(seq)
    r = DFT_ROWS
    spread = (jnp.arange(n1 * r)[None, :] // r == jnp.arange(n1)[:, None]).astype(F32)
    w_rep = jnp.dot(w_a, spread, precision=lax.Precision.HIGHEST)
    same_r = jnp.arange(2 * n1 * r)[:, None] % r == jnp.arange(n1 * r)[None, :] % r
    w_a = jnp.where(same_r, jnp.repeat(w_rep, r, axis=0), 0.0).astype(BF16)
    y = pl.pallas_call(
        _dft_a_kernel,
        grid=(b, n2 // r),
        in_specs=[_const_spec(w_a.shape),
                  pl.BlockSpec((1, n1, r, w), lambda i, j: (i, 0, j, 0))],
        out_specs=pl.BlockSpec((1, 2, n1, r, w), lambda i, j: (i, 0, 0, j, 0)),
        out_shape=jax.ShapeDtypeStruct((b, 2, n1, n2, w), BF16),
        compiler_params=_params(("parallel", "parallel")),
        name="dft_a",
    )(w_a, f.reshape(b, n1, n2, w))
    z = pl.pallas_call(
        _dft_b_kernel,
        grid=(b, n1 // r),
        in_specs=[pl.BlockSpec((r, 2, n2), lambda i, j: (j, 0, 0)),
                  _const_spec((2, n2, n2)),
                  _const_spec((2 * F_GROUP_DIM, F_GROUP_DIM)),
                  pl.BlockSpec((1, 2, r, n2, w), lambda i, j: (i, 0, j, 0, 0))],
        out_specs=pl.BlockSpec((1, n2, r, w), lambda i, j: (i, 0, j, 0)),
        out_shape=jax.ShapeDtypeStruct((b, n2, n1, w), BF16),
        scratch_shapes=[pltpu.VMEM((F_GROUPS, n2 * r, F_GROUP_DIM), F32)],
        compiler_params=_params(("parallel", "parallel")),
        name="dft_b",
    )(tw, w2, w_c, y)
    return z.reshape(b, seq, w)


def _merge_kernel(x_ref, mod_ref, nw_ref, n2w_ref, ro_ref, sg_ref, z_ref, wbg_ref, bbg_ref,
                  wro_ref, wfo_ref, wout_ref, o_ref, h2_ref):
    x = x_ref[...]
    h = _rms_norm(x, nw_ref[...]) * (1.0 + mod_ref[0, 1:2, :]) + mod_ref[0, 0:1, :]
    gates = jax.nn.sigmoid(_dot(h.astype(BF16), wbg_ref[...]) + bbg_ref[...])
    gated = []
    for hd in range(HEADS):
        cols = slice(hd * V_DIM, (hd + 1) * V_DIM)
        o = ro_ref[:, cols].astype(F32)
        oc = o - jnp.mean(o, axis=-1, keepdims=True)
        var = jnp.mean(oc * oc, axis=-1, keepdims=True)
        gated.append((oc * lax.rsqrt(var + EPS) * sg_ref[:, cols].astype(F32)).astype(BF16))
    ret_d = _dot(jnp.concatenate(gated, axis=1), wro_ref[...])
    four_d = _dot(z_ref[...], wfo_ref[...])
    m = gates[:, :D_MODEL] * ret_d + gates[:, D_MODEL:] * four_d
    y = _dot(m.astype(BF16), wout_ref[...])
    x1 = x + mod_ref[0, 2:3, :] * y
    o_ref[...] = x1
    h2 = _rms_norm(x1, n2w_ref[...]) * (1.0 + mod_ref[0, 4:5, :]) + mod_ref[0, 3:4, :]
    h2_ref[...] = h2.astype(BF16)


def _merge(x2, mod3, norm_w, norm2_w, ro, sg, z, w_bg, b_bg, w_ro, w_fo, w_out, seq):
    t, d = x2.shape
    tl = TOKEN_TILE
    tpb = seq // tl
    tok = lambda w: pl.BlockSpec((tl, w), lambda i: (i, 0))
    return pl.pallas_call(
        _merge_kernel,
        grid=(t // tl,),
        in_specs=[tok(d),
                  pl.BlockSpec((1, N_MOD, d), lambda i: (i // tpb, 0, 0)),
                  _const_spec((1, d)), _const_spec((1, d)),
                  tok(V_WIDTH), tok(V_WIDTH), tok(F_WIDTH),
                  _const_spec(w_bg.shape), _const_spec((1, 2 * d)),
                  _const_spec(w_ro.shape), _const_spec(w_fo.shape), _const_spec(w_out.shape)],
        out_specs=[tok(d), tok(d)],
        out_shape=[jax.ShapeDtypeStruct((t, d), F32), jax.ShapeDtypeStruct((t, d), BF16)],
        compiler_params=_params(("parallel",)),
        name="merge",
    )(x2, mod3, norm_w, norm2_w, ro, sg, z, w_bg, b_bg, w_ro, w_fo, w_out)


HALO = 8
FFN_LEAD = 16
FFN_SUB = 256


def _two_gelu_tanh(x):
    c1 = np.sqrt(2.0 / np.pi)
    return x + x * jnp.tanh(x * (c1 + (c1 * 0.044715) * (x * x)))


def _ffn_kernel(xm_ref, hm_ref, xp_ref, xn_ref, mod_ref, nw_ref, wu_ref, cw_ref, cb_ref,
                wd_ref, fnw_ref, o_ref, h_scr, u_scr, act_scr, y_scr, *, tiles_per_seq):
    tl = xm_ref.shape[0]
    sub = FFN_SUB
    half = sub // 2
    nc = FFN_CHUNK
    i = pl.program_id(0)
    keep_prev = ((i % tiles_per_seq) != 0).astype(F32)
    keep_next = ((i % tiles_per_seq) != tiles_per_seq - 1).astype(F32)

    def pre(x):
        return _rms_norm(x, nw_ref[...]) * (1.0 + mod_ref[0, 4:5, :]) + mod_ref[0, 3:4, :]

    pad = jnp.zeros((FFN_LEAD - HALO, xm_ref.shape[1]), F32)
    h_scr[0:FFN_LEAD] = jnp.concatenate([pad, pre(xp_ref[...]) * keep_prev], axis=0).astype(BF16)
    h_scr[FFN_LEAD:FFN_LEAD + tl] = hm_ref[...]
    h_scr[FFN_LEAD + tl:] = jnp.concatenate([pre(xn_ref[...]) * keep_next, pad], axis=0).astype(BF16)

    def project(j):
        hb = h_scr[...]
        for part in range(2):
            lo = part * FFN_DIM + j * nc
            u = _dot(hb, wu_ref[:, lo:lo + nc])
            for s in range(nc // LANES):
                u_scr[j % 2, part, s] = u[:, s * LANES:(s + 1) * LANES]

    def conv(j, part, s, blk, scale):
        lo = part * FFN_DIM + j * nc + s * LANES
        w = cw_ref[:, lo:lo + LANES] * scale
        bias = cb_ref[:, lo:lo + LANES] * scale
        first = FFN_LEAD + blk * sub
        rows = lambda start: u_scr[j % 2, part, s, pl.ds(start, half, stride=2), :]
        before, even, odd, after = rows(first - 1), rows(first), rows(first + 1), rows(first + 2)
        return (before * w[0:1] + even * w[1:2] + odd * w[2:3] + bias,
                even * w[0:1] + odd * w[1:2] + after * w[2:3] + bias)

    def activate(j):
        for s in range(nc // LANES):
            cols = slice(j * nc + s * LANES, j * nc + (s + 1) * LANES)
            for blk in range(tl // sub):
                gate = conv(j, 0, s, blk, 1.0)
                val = conv(j, 1, s, blk, 0.5)
                for par in range(2):
                    r0 = blk * sub + par * half
                    act_scr[r0:r0 + half, cols] = (_two_gelu_tanh(gate[par]) * val[par]).astype(BF16)

    project(0)
    for j in range(N_FFN_CHUNKS):
        if j + 1 < N_FFN_CHUNKS:
            project(j + 1)
        activate(j)
    for blk in range(tl // sub):
        r0 = blk * sub
        y = _dot(act_scr[r0:r0 + sub, :], wd_ref[...])
        for s in range(y_scr.shape[0]):
            for par in range(2):
                y_scr[s, pl.ds(r0 + par, half, stride=2), :] = (
                    y[par * half:(par + 1) * half, s * LANES:(s + 1) * LANES])
        y = jnp.concatenate([y_scr[s, r0:r0 + sub, :] for s in range(y_scr.shape[0])], axis=1)
        x2 = xm_ref[r0:r0 + sub, :] + mod_ref[0, 5:6, :] * y
        o_ref[r0:r0 + sub, :] = _rms_norm(x2, fnw_ref[...])


def _ffn(x1, h2, mod3, norm_w, w_u, cw, cb, w_d, fnorm_w, seq):
    t, d = x1.shape
    tl = TOKEN_TILE
    tpb = seq // tl
    hb = tl // HALO
    last = t // HALO - 1
    rows = FFN_LEAD + tl + FFN_LEAD
    return pl.pallas_call(
        functools.partial(_ffn_kernel, tiles_per_seq=tpb),
        grid=(t // tl,),
        in_specs=[pl.BlockSpec((tl, d), lambda i: (i, 0)),
                  pl.BlockSpec((tl, d), lambda i: (i, 0)),
                  pl.BlockSpec((HALO, d), lambda i: (jnp.maximum(i * hb - 1, 0), 0)),
                  pl.BlockSpec((HALO, d), lambda i: (jnp.minimum((i + 1) * hb, last), 0)),
                  pl.BlockSpec((1, N_MOD, d), lambda i: (i // tpb, 0, 0)),
                  _const_spec((1, d)),
                  _const_spec(w_u.shape), _const_spec(cw.shape), _const_spec(cb.shape),
                  _const_spec(w_d.shape), _const_spec((1, d))],
        out_specs=pl.BlockSpec((tl, d), lambda i: (i, 0)),
        out_shape=jax.ShapeDtypeStruct((t, d), F32),
        scratch_shapes=[pltpu.VMEM((rows, d), BF16),
                        pltpu.VMEM((2, 2, FFN_CHUNK // LANES, rows, LANES), F32),
                        pltpu.VMEM((tl, FFN_DIM), BF16),
                        pltpu.VMEM((d // LANES, tl, LANES), F32)],
        compiler_params=_params(("parallel",)),
        name="ffn",
    )(x1, h2, x1, x1, mod3, norm_w, w_u, cw, cb, w_d, fnorm_w)


def _qk_perm():
    lane = np.arange(QK_WIDTH)
    pair, l = lane // LANES, lane % LANES
    head = 2 * pair + (l // (QK_DIM // 2)) % 2
    half = l // (LANES // 2)
    return head * QK_DIM + half * (QK_DIM // 2) + l % (QK_DIM // 2)


def _rope_tables(seq):
    n_rows = seq // GRID_W
    n_freq = QK_DIM // 4
    lane = jnp.arange(LANES)
    inv = ROPE_BASE ** (-(lane % n_freq).astype(F32) / n_freq)
    by_row = (lane % (2 * n_freq)) < n_freq
    sign = jnp.where(lane < LANES // 2, -1.0, 1.0).astype(F32)

    def expand(fn):
        of_row = fn(jnp.arange(n_rows, dtype=F32)[:, None] * inv[None, :])
        of_col = fn(jnp.arange(GRID_W, dtype=F32)[:, None] * inv[None, :])
        table = jnp.where(by_row[None, None, :], of_row[:, None, :], of_col[None, :, :])
        return table.reshape(seq, LANES)

    return expand(jnp.cos), expand(jnp.sin) * sign


def kernel(x, c, ctx, c_ctx, w_mod, b_mod, norm1_w, w_in, ret_decay_f, ret_decay_b,
           w_ret_out, w_four_out, w_branch_gate, b_branch_gate, w_out, norm2_w,
           w_up, conv_w, conv_b, w_down, final_norm_w):
    assert w_mod.shape[0] == 1, "single-layer block"
    b, seq, d = x.shape
    t = b * seq
    assert b == 2 and seq % TOKEN_TILE == 0 and seq % RET_CHUNK == 0

    c8 = jnp.concatenate([c, c_ctx[None, :], jnp.zeros((8 - b - 1, d), F32)], axis=0)
    mod3 = _modulation(c8, w_mod[0], b_mod[0]).reshape(8, N_MOD, d)

    perm = _qk_perm()
    w_in_b = w_in[0].astype(BF16)
    w_qk = jnp.take(w_in_b, np.concatenate([perm, K_OFF + perm]), axis=1)
    n1w = norm1_w[0].reshape(1, d)

    kc, vc = _ctx_proj(ctx, mod3, n1w, w_qk, w_in_b)
    cos, sin = _rope_tables(seq)
    x2 = x.reshape(t, d)
    q, k, v, sg, f = _proj(x2, mod3, n1w, w_qk, w_in_b, cos, sin, seq)

    a_f = jnp.broadcast_to(ret_decay_f[0][:, None, None], (HEADS, 1, LANES))
    a_b = jnp.broadcast_to(ret_decay_b[0][:, None, None], (HEADS, 1, LANES))
    ro = _retention(a_f, a_b, q.reshape(b, seq, QK_WIDTH), k.reshape(b, seq, QK_WIDTH),
                    v.reshape(b, seq, V_WIDTH), kc, vc)
    z = _fourier(f.reshape(b, seq, F_WIDTH))

    n2w = norm2_w[0].reshape(1, d)
    x1, h2 = _merge(x2, mod3, n1w, n2w, ro.reshape(t, V_WIDTH), sg, z.reshape(t, F_WIDTH),
                    w_branch_gate[0].astype(BF16), b_branch_gate[0].reshape(1, 2 * d),
                    w_ret_out[0].astype(BF16), w_four_out[0].astype(BF16),
                    w_out[0].astype(BF16), seq)

    out = _ffn(x1, h2, mod3, n2w, w_up[0].astype(BF16), conv_w[0],
               conv_b[0].reshape(1, 2 * FFN_DIM), w_down[0].astype(BF16),
               final_norm_w.reshape(1, d), seq)
    return out.reshape(b, seq, d)
```

```python
import functools

import numpy as np
import jax
import jax.numpy as jnp
from jax import lax
from jax.experimental import pallas as pl
from jax.experimental.pallas import tpu as pltpu

F32 = jnp.float32
BF16 = jnp.bfloat16

D_MODEL = 1024
GRID_W = 64
HEADS = 8
QK_DIM = 64
V_DIM = 128
QK_WIDTH = HEADS * QK_DIM
V_WIDTH = HEADS * V_DIM
ROPE_BASE = 10000.0
F_GROUPS = 4
F_GROUP_DIM = 128
F_WIDTH = F_GROUPS * F_GROUP_DIM
K_OFF = QK_WIDTH
V_OFF = K_OFF + QK_WIDTH
G_OFF = V_OFF + V_WIDTH
F_OFF = G_OFF + V_WIDTH
IN_COLS = F_OFF + F_WIDTH
FFN_DIM = 2816
N_MOD = 6
EPS = 1e-6

LANES = 128
RET_CHUNK = 256
RET_UNROLL = 8
FFN_CHUNK = 256
N_FFN_CHUNKS = FFN_DIM // FFN_CHUNK
TOKEN_TILE = 512
FFN_TILE = 512
DFT_N1 = 64
BF16_ROWS = 16
DFT_ROWS = BF16_ROWS
VMEM_LIMIT = 56 * 1024 * 1024


def _params(sem):
    return pltpu.CompilerParams(dimension_semantics=sem, vmem_limit_bytes=VMEM_LIMIT)


def _dot(a, b):
    return jnp.dot(a, b, preferred_element_type=F32)


def _rms_norm(x, w):
    return x * lax.rsqrt(jnp.mean(x * x, axis=-1, keepdims=True) + EPS) * w


def _const_spec(shape):
    zeros = (0,) * len(shape)
    return pl.BlockSpec(shape, lambda *_: zeros, pipeline_mode=pl.Buffered(1))


def _cast_jobs(weights, steps):
    specs, shapes = [], []
    for w in weights:
        n_rows = w.shape[0]
        rows = -(-n_rows // steps)
        while rows % BF16_ROWS or n_rows % rows:
            rows += 1
        last = n_rows // rows - 1
        specs.append(pl.BlockSpec((rows, w.shape[1]), lambda i, last=last: (jnp.minimum(i, last), 0)))
        shapes.append(jax.ShapeDtypeStruct(w.shape, BF16))
    return specs, shapes


def _run_casts(refs):
    n = len(refs) // 2
    for src, dst in zip(refs[:n], refs[n:]):
        dst[...] = src[...].astype(BF16)


def _mod_kernel(c_ref, w_ref, b_ref, o_ref):
    c = c_ref[...]
    s = c * jax.nn.sigmoid(c)
    o_ref[...] = _dot(s.astype(BF16), w_ref[...].astype(BF16)) + b_ref[...]


def _modulation(c8, w_mod, b_mod):
    n = w_mod.shape[1]
    tn = 1536
    return pl.pallas_call(
        _mod_kernel,
        grid=(n // tn,),
        in_specs=[_const_spec((8, D_MODEL)),
                  pl.BlockSpec((D_MODEL, tn), lambda j: (0, j)),
                  pl.BlockSpec((1, tn), lambda j: (0, j))],
        out_specs=pl.BlockSpec((8, tn), lambda j: (0, j)),
        out_shape=jax.ShapeDtypeStruct((8, n), F32),
        compiler_params=_params(("parallel",)),
        name="mod",
    )(c8, w_mod, b_mod.reshape(1, n))


def _ctx_kernel(x_ref, mod_ref, nw_ref, wqk_ref, w_ref, k_ref, v_ref):
    x = x_ref[0]
    h = _rms_norm(x, nw_ref[...]) * (1.0 + mod_ref[0, 1:2, :]) + mod_ref[0, 0:1, :]
    hb = h.astype(BF16)
    k_ref[0] = _dot(hb, wqk_ref[:, QK_WIDTH:]).astype(BF16)
    v_ref[0] = _dot(hb, w_ref[:, V_OFF:G_OFF]).astype(BF16)


def _ctx_proj(ctx, mod3, norm_w, w_qk, w_in):
    b, lc, d = ctx.shape
    return pl.pallas_call(
        _ctx_kernel,
        grid=(b,),
        in_specs=[pl.BlockSpec((1, lc, d), lambda i: (i, 0, 0)),
                  pl.BlockSpec((1, N_MOD, d), lambda i: (2, 0, 0)),
                  _const_spec((1, d)),
                  _const_spec(w_qk.shape), _const_spec(w_in.shape)],
        out_specs=[pl.BlockSpec((1, lc, QK_WIDTH), lambda i: (i, 0, 0)),
                   pl.BlockSpec((1, lc, V_WIDTH), lambda i: (i, 0, 0))],
        out_shape=[jax.ShapeDtypeStruct((b, lc, QK_WIDTH), BF16),
                   jax.ShapeDtypeStruct((b, lc, V_WIDTH), BF16)],
        compiler_params=_params(("parallel",)),
        name="ctx_proj",
    )(ctx, mod3, norm_w, w_qk, w_in)


def _proj_kernel(x_ref, mod_ref, nw_ref, wqk_ref, w_ref, cos_ref, sin_ref, *refs, n_cast):
    q_ref, k_ref, v_ref, sg_ref, f_ref = refs[n_cast:n_cast + 5]
    _run_casts(refs[:n_cast] + refs[n_cast + 5:])
    x = x_ref[...]
    h = _rms_norm(x, nw_ref[...]) * (1.0 + mod_ref[0, 1:2, :]) + mod_ref[0, 0:1, :]
    hb = h.astype(BF16)
    cos = cos_ref[...]
    sin = sin_ref[...]

    def rope(t, scale, out_ref):
        for j in range(QK_WIDTH // LANES):
            tj = t[:, j * LANES:(j + 1) * LANES]
            r = tj * cos + pltpu.roll(tj, LANES // 2, 1) * sin
            out_ref[:, j * LANES:(j + 1) * LANES] = (r * scale).astype(BF16)

    rope(_dot(hb, wqk_ref[:, :QK_WIDTH]), QK_DIM ** -0.5, q_ref)
    rope(_dot(hb, wqk_ref[:, QK_WIDTH:]), 1.0, k_ref)
    v_ref[...] = _dot(hb, w_ref[:, V_OFF:G_OFF]).astype(BF16)
    g = _dot(hb, w_ref[:, G_OFF:F_OFF])
    sg_ref[...] = (g * jax.nn.sigmoid(g)).astype(BF16)
    f_ref[...] = _dot(hb, w_ref[:, F_OFF:IN_COLS]).astype(BF16)


def _proj(x2, mod3, norm_w, w_qk, w_in, cos, sin, seq, to_cast):
    t, d = x2.shape
    tl = TOKEN_TILE
    tpb = seq // tl
    tok = lambda w: pl.BlockSpec((tl, w), lambda i: (i, 0))
    tab = pl.BlockSpec((tl, LANES), lambda i: (i % tpb, 0))
    cast_specs, cast_shapes = _cast_jobs(to_cast, t // tl)
    out = pl.pallas_call(
        functools.partial(_proj_kernel, n_cast=len(to_cast)),
        grid=(t // tl,),
        in_specs=[tok(d),
                  pl.BlockSpec((1, N_MOD, d), lambda i: (i // tpb, 0, 0)),
                  _const_spec((1, d)),
                  _const_spec(w_qk.shape), _const_spec(w_in.shape),
                  tab, tab] + cast_specs,
        out_specs=[tok(QK_WIDTH), tok(QK_WIDTH), tok(V_WIDTH), tok(V_WIDTH),
                   tok(F_WIDTH)] + cast_specs,
        out_shape=[jax.ShapeDtypeStruct((t, QK_WIDTH), BF16),
                   jax.ShapeDtypeStruct((t, QK_WIDTH), BF16),
                   jax.ShapeDtypeStruct((t, V_WIDTH), BF16),
                   jax.ShapeDtypeStruct((t, V_WIDTH), BF16),
                   jax.ShapeDtypeStruct((t, F_WIDTH), BF16)] + cast_shapes,
        compiler_params=_params(("arbitrary",)),
        name="proj",
    )(x2, mod3, norm_w, w_qk, w_in, cos, sin, *to_cast)
    return out[:5], out[5:]


def _retention_kernel(af_ref, ab_ref, q_ref, k_ref, v_ref, kc_ref, vc_ref,
                      o_ref, ds_scr, st_scr, d_scr, tab_scr):
    c = RET_CHUNK
    seq = q_ref.shape[1]
    lc = kc_ref.shape[1]
    n = seq // c
    lgf = [-jnp.exp(af_ref[hh]) for hh in range(2)]
    lgb = [-jnp.exp(ab_ref[hh]) for hh in range(2)]
    half = QK_DIM // 2

    lane_head = (lax.broadcasted_iota(jnp.int32, (1, LANES), 1) // half) % 2
    masks = [(lane_head == hh).astype(BF16) for hh in range(2)]
    lgf_lane = jnp.where(lane_head == 0, lgf[0], lgf[1])
    lgb_lane = jnp.where(lane_head == 0, lgb[0], lgb[1])
    row_head = (lax.broadcasted_iota(jnp.int32, (LANES, 1), 0) // half) % 2
    lgf_row = jnp.where(row_head == 0, lgf[0][:, 0:1], lgf[1][:, 0:1])
    lgb_row = jnp.where(row_head == 0, lgb[0][:, 0:1], lgb[1][:, 0:1])

    pos = lax.broadcasted_iota(jnp.int32, (c, LANES), 0).astype(F32)
    tab_scr[0] = jnp.exp(lgf_lane * (pos + 1.0))
    tab_scr[1] = jnp.exp(lgb_lane * (c - pos))
    diff = (lax.broadcasted_iota(jnp.int32, (c, c), 0)
            - lax.broadcasted_iota(jnp.int32, (c, c), 1)).astype(F32)
    for hh in range(2):
        d_scr[hh] = (jnp.where(diff >= 0, jnp.exp(lgf[hh][:, 0:1] * jnp.maximum(diff, 0.0)), 0.0)
                     + jnp.where(diff <= 0, jnp.exp(lgb[hh][:, 0:1] * jnp.maximum(-diff, 0.0)), 0.0))

    def k_decays(tokens):
        t = lax.broadcasted_iota(jnp.int32, (1, tokens), 1).astype(F32)
        return jnp.exp(lgf_row * (tokens - 1.0 - t)), jnp.exp(lgb_row * t)

    def state_increment(k_rows, v_rows, decays):
        kt = jnp.transpose(k_rows.astype(F32))
        lhs = jnp.concatenate([kt * decays[0], kt * decays[1]], axis=0).astype(BF16)
        return _dot(lhs, v_rows)

    kdec = k_decays(c)

    def incr(i, carry):
        rows = pl.ds(pl.multiple_of(i * c, c), c)
        ds_scr[i] = state_increment(k_ref[0, rows, :], v_ref[0, rows, :], kdec)
        return carry

    lax.fori_loop(0, n, incr, 0, unroll=RET_UNROLL)

    s0 = state_increment(kc_ref[0], vc_ref[0], k_decays(lc))
    col_head = lax.broadcasted_iota(jnp.int32, (1, 2 * V_DIM), 1) // V_DIM
    own = (row_head == col_head).astype(F32)
    decay_f = jnp.exp(lgf_row * c) * own
    decay_b = jnp.exp(lgb_row * c) * own

    def scan_f(i, s):
        st_scr[i, 0:LANES, :] = (s * own).astype(BF16)
        return decay_f * s + ds_scr[i, 0:LANES, :]

    lax.fori_loop(0, n, scan_f, s0[0:LANES])

    def scan_b(t, s):
        i = n - 1 - t
        st_scr[i, LANES:2 * LANES, :] = (s * own).astype(BF16)
        return decay_b * s + ds_scr[i, LANES:2 * LANES, :]

    lax.fori_loop(0, n, scan_b, s0[LANES:2 * LANES])

    def outputs(i, carry):
        rows = pl.ds(pl.multiple_of(i * c, c), c)
        q = q_ref[0, rows, :]
        k = k_ref[0, rows, :]
        qf = q.astype(F32)
        qd = jnp.concatenate([qf * tab_scr[0], qf * tab_scr[1]], axis=1).astype(BF16)
        inter = _dot(qd, st_scr[i])
        for hh in range(2):
            cols = slice(hh * V_DIM, (hh + 1) * V_DIM)
            scores = lax.dot_general(q * masks[hh], k, (((1,), (1,)), ((), ())),
                                     preferred_element_type=F32)
            o = _dot((scores * d_scr[hh]).astype(BF16), v_ref[0, rows, cols]) + inter[:, cols]
            o_ref[0, rows, cols] = o.astype(BF16)
        return carry

    lax.fori_loop(0, n, outputs, 0, unroll=RET_UNROLL)


def _retention(a_f, a_b, q, k, v, kc, vc):
    b, seq, _ = v.shape
    lc = kc.shape[1]
    c = RET_CHUNK
    n = seq // c
    dec = pl.BlockSpec((2, 1, LANES), lambda i, p: (p, 0, 0))
    qk = lambda rows: pl.BlockSpec((1, rows, LANES), lambda i, p: (i, 0, p))
    vv = lambda rows: pl.BlockSpec((1, rows, 2 * V_DIM), lambda i, p: (i, 0, p))
    return pl.pallas_call(
        _retention_kernel,
        grid=(b, HEADS // 2),
        in_specs=[dec, dec, qk(seq), qk(seq), vv(seq), qk(lc), vv(lc)],
        out_specs=vv(seq),
        out_shape=jax.ShapeDtypeStruct((b, seq, V_WIDTH), BF16),
        scratch_shapes=[pltpu.VMEM((n, 2 * LANES, 2 * V_DIM), F32),
                        pltpu.VMEM((n, 2 * LANES, 2 * V_DIM), BF16),
                        pltpu.VMEM((2, c, c), F32),
                        pltpu.VMEM((2, c, LANES), F32)],
        compiler_params=_params(("parallel", "parallel")),
        name="retention",
    )(a_f, a_b, q, k, v, kc, vc)


def _dft_a_kernel(w_ref, x_ref, y_ref):
    _, n1, r, w = x_ref.shape
    y = _dot(w_ref[...], x_ref[0].reshape(n1 * r, w))
    y_ref[0] = y.reshape(2, n1, r, w).astype(BF16)


def _dft_b_kernel(tw_ref, w2_ref, cc_ref, y_ref, z_ref, z_scr):
    n2 = y_ref.shape[3]
    cc = cc_ref[...].astype(BF16)
    w2c = w2_ref[0]
    w2s = w2_ref[1]
    for j in range(DFT_ROWS):
        tc = tw_ref[j, 0:1, :]
        ts = tw_ref[j, 1:2, :]
        ec = w2c * tc - w2s * ts
        es = w2s * tc + w2c * ts
        m = jnp.concatenate([jnp.concatenate([ec, -es], axis=1),
                             jnp.concatenate([es, ec], axis=1)], axis=0).astype(BF16)
        y = jnp.concatenate([y_ref[0, 0, j], y_ref[0, 1, j]], axis=0)
        zz = _dot(m, y)
        for g in range(F_GROUPS):
            cols = slice(g * F_GROUP_DIM, (g + 1) * F_GROUP_DIM)
            zcs = jnp.concatenate([zz[:n2, cols], zz[n2:, cols]], axis=1).astype(BF16)
            z_scr[g, pl.ds(j, n2, stride=DFT_ROWS), :] = _dot(zcs, cc)
    z = jnp.concatenate([z_scr[g] for g in range(F_GROUPS)], axis=1)
    z_ref[0] = z.reshape(n2, DFT_ROWS, F_WIDTH).astype(BF16)


def _dft_tables(seq):
    n1 = DFT_N1
    n2 = seq // n1

    def cs(num, den):
        ang = 2.0 * np.pi * (num % den) / den
        return np.cos(ang), np.sin(ang)

    a = np.arange(n1)
    w_a = np.concatenate(cs(a[:, None] * a[None, :], n1), axis=0)
    m = np.arange(n2)
    tw = np.stack(cs(a[:, None] * m[None, :], seq), axis=1)
    w2 = np.stack(cs(m[:, None] * m[None, :], n2), axis=0)
    ch = np.arange(F_GROUP_DIM)
    cc, sc = cs(ch[:, None] * ch[None, :], F_GROUP_DIM)
    scale = 1.0 / np.sqrt(seq * F_GROUP_DIM)
    w_c = np.concatenate([cc, -sc], axis=0) * scale
    return [jnp.asarray(t, dtype=F32) for t in (w_a, tw, w2, w_c)]


def _fourier(f):
    b, seq, w = f.shape
    n1 = DFT_N1
    n2 = seq // n1
    w_a, tw, w2, w_c = _dft_tables(seq)
    r = DFT_ROWS
    spread = (jnp.arange(n1 * r)[None, :] // r == jnp.arange(n1)[:, None]).astype(F32)
    w_rep = jnp.dot(w_a, spread, precision=lax.Precision.HIGHEST)
    same_r = jnp.arange(2 * n1 * r)[:, None] % r == jnp.arange(n1 * r)[None, :] % r
    w_a = jnp.where(same_r, jnp.repeat(w_rep, r, axis=0), 0.0).astype(BF16)
    y = pl.pallas_call(
        _dft_a_kernel,
        grid=(b, n2 // r),
        in_specs=[_const_spec(w_a.shape),
                  pl.BlockSpec((1, n1, r, w), lambda i, j: (i, 0, j, 0))],
        out_specs=pl.BlockSpec((1, 2, n1, r, w), lambda i, j: (i, 0, 0, j, 0)),
        out_shape=jax.ShapeDtypeStruct((b, 2, n1, n2, w), BF16),
        compiler_params=_params(("parallel", "parallel")),
        name="dft_a",
    )(w_a, f.reshape(b, n1, n2, w))
    z = pl.pallas_call(
        _dft_b_kernel,
        grid=(b, n1 // r),
        in_specs=[pl.BlockSpec((r, 2, n2), lambda i, j: (j, 0, 0)),
                  _const_spec((2, n2, n2)),
                  _const_spec((2 * F_GROUP_DIM, F_GROUP_DIM)),
                  pl.BlockSpec((1, 2, r, n2, w), lambda i, j: (i, 0, j, 0, 0))],
        out_specs=pl.BlockSpec((1, n2, r, w), lambda i, j: (i, 0, j, 0)),
        out_shape=jax.ShapeDtypeStruct((b, n2, n1, w), BF16),
        scratch_shapes=[pltpu.VMEM((F_GROUPS, n2 * r, F_GROUP_DIM), F32)],
        compiler_params=_params(("parallel", "parallel")),
        name="dft_b",
    )(tw, w2, w_c, y)
    return z.reshape(b, seq, w)


def _merge_kernel(x_ref, mod_ref, nw_ref, n2w_ref, ro_ref, sg_ref, z_ref, wbg_ref, bbg_ref,
                  wro_ref, wfo_ref, wout_ref, *refs, n_cast):
    o_ref, h2_ref = refs[n_cast:n_cast + 2]
    _run_casts(refs[:n_cast] + refs[n_cast + 2:])
    x = x_ref[...]
    h = _rms_norm(x, nw_ref[...]) * (1.0 + mod_ref[0, 1:2, :]) + mod_ref[0, 0:1, :]
    gates = jax.nn.sigmoid(_dot(h.astype(BF16), wbg_ref[...]) + bbg_ref[...])
    gated = []
    for hd in range(HEADS):
        cols = slice(hd * V_DIM, (hd + 1) * V_DIM)
        o = ro_ref[:, cols].astype(F32)
        oc = o - jnp.mean(o, axis=-1, keepdims=True)
        var = jnp.mean(oc * oc, axis=-1, keepdims=True)
        gated.append((oc * lax.rsqrt(var + EPS) * sg_ref[:, cols].astype(F32)).astype(BF16))
    ret_d = _dot(jnp.concatenate(gated, axis=1), wro_ref[...])
    four_d = _dot(z_ref[...], wfo_ref[...])
    m = gates[:, :D_MODEL] * ret_d + gates[:, D_MODEL:] * four_d
    y = _dot(m.astype(BF16), wout_ref[...])
    x1 = x + mod_ref[0, 2:3, :] * y
    o_ref[...] = x1
    h2 = _rms_norm(x1, n2w_ref[...]) * (1.0 + mod_ref[0, 4:5, :]) + mod_ref[0, 3:4, :]
    h2_ref[...] = h2.astype(BF16)


def _merge(x2, mod3, norm_w, norm2_w, ro, sg, z, w_bg, b_bg, w_ro, w_fo, w_out, seq, to_cast):
    t, d = x2.shape
    tl = TOKEN_TILE
    tpb = seq // tl
    tok = lambda w: pl.BlockSpec((tl, w), lambda i: (i, 0))
    cast_specs, cast_shapes = _cast_jobs(to_cast, t // tl)
    out = pl.pallas_call(
        functools.partial(_merge_kernel, n_cast=len(to_cast)),
        grid=(t // tl,),
        in_specs=[tok(d),
                  pl.BlockSpec((1, N_MOD, d), lambda i: (i // tpb, 0, 0)),
                  _const_spec((1, d)), _const_spec((1, d)),
                  tok(V_WIDTH), tok(V_WIDTH), tok(F_WIDTH),
                  _const_spec(w_bg.shape), _const_spec((1, 2 * d)),
                  _const_spec(w_ro.shape), _const_spec(w_fo.shape),
                  _const_spec(w_out.shape)] + cast_specs,
        out_specs=[tok(d), tok(d)] + cast_specs,
        out_shape=[jax.ShapeDtypeStruct((t, d), F32),
                   jax.ShapeDtypeStruct((t, d), BF16)] + cast_shapes,
        compiler_params=_params(("arbitrary",)),
        name="merge",
    )(x2, mod3, norm_w, norm2_w, ro, sg, z, w_bg, b_bg, w_ro, w_fo, w_out, *to_cast)
    return out[:2], out[2:]


HALO = 8
FFN_LEAD = 16
FFN_SUB = 256


def _two_gelu_tanh(x):
    c1 = np.sqrt(2.0 / np.pi)
    return x + x * jnp.tanh(x * (c1 + (c1 * 0.044715) * (x * x)))


def _ffn_kernel(xm_ref, hm_ref, xp_ref, xn_ref, mod_ref, nw_ref, wu_ref, cw_ref, cb_ref,
                wd_ref, fnw_ref, o_ref, h_scr, u_scr, act_scr, y_scr, *, tiles_per_seq):
    tl = xm_ref.shape[0]
    sub = FFN_SUB
    half = sub // 2
    nc = FFN_CHUNK
    i = pl.program_id(0)
    keep_prev = ((i % tiles_per_seq) != 0).astype(F32)
    keep_next = ((i % tiles_per_seq) != tiles_per_seq - 1).astype(F32)

    def pre(x):
        return _rms_norm(x, nw_ref[...]) * (1.0 + mod_ref[0, 4:5, :]) + mod_ref[0, 3:4, :]

    pad = jnp.zeros((FFN_LEAD - HALO, xm_ref.shape[1]), F32)
    h_scr[0:FFN_LEAD] = jnp.concatenate([pad, pre(xp_ref[...]) * keep_prev], axis=0).astype(BF16)
    h_scr[FFN_LEAD:FFN_LEAD + tl] = hm_ref[...]
    h_scr[FFN_LEAD + tl:] = jnp.concatenate([pre(xn_ref[...]) * keep_next, pad], axis=0).astype(BF16)

    def project(j):
        hb = h_scr[...]
        for part in range(2):
            lo = part * FFN_DIM + j * nc
            u = _dot(hb, wu_ref[:, lo:lo + nc])
            for s in range(nc // LANES):
                u_scr[j % 2, part, s] = u[:, s * LANES:(s + 1) * LANES]

    def conv(j, part, s, blk, scale):
        lo = part * FFN_DIM + j * nc + s * LANES
        w = cw_ref[:, lo:lo + LANES] * scale
        bias = cb_ref[:, lo:lo + LANES] * scale
        first = FFN_LEAD + blk * sub
        rows = lambda start: u_scr[j % 2, part, s, pl.ds(start, half, stride=2), :]
        before, even, odd, after = rows(first - 1), rows(first), rows(first + 1), rows(first + 2)
        return (before * w[0:1] + even * w[1:2] + odd * w[2:3] + bias,
                even * w[0:1] + odd * w[1:2] + after * w[2:3] + bias)

    def activate(j):
        for s in range(nc // LANES):
            cols = slice(j * nc + s * LANES, j * nc + (s + 1) * LANES)
            for blk in range(tl // sub):
                gate = conv(j, 0, s, blk, 1.0)
                val = conv(j, 1, s, blk, 0.5)
                for par in range(2):
                    r0 = blk * sub + par * half
                    act_scr[r0:r0 + half, cols] = (_two_gelu_tanh(gate[par]) * val[par]).astype(BF16)

    project(0)
    for j in range(N_FFN_CHUNKS):
        if j + 1 < N_FFN_CHUNKS:
            project(j + 1)
        activate(j)
    for blk in range(tl // sub):
        r0 = blk * sub
        y = _dot(act_scr[r0:r0 + sub, :], wd_ref[...])
        for s in range(y_scr.shape[0]):
            for par in range(2):
                y_scr[s, pl.ds(par, half, stride=2), :] = (
                    y[par * half:(par + 1) * half, s * LANES:(s + 1) * LANES])
        y = jnp.concatenate([y_scr[s] for s in range(y_scr.shape[0])], axis=1)
        x2 = xm_ref[r0:r0 + sub, :] + mod_ref[0, 5:6, :] * y
        o_ref[r0:r0 + sub, :] = _rms_norm(x2, fnw_ref[...])


def _ffn(x1, h2, mod3, norm_w, w_u, cw, cb, w_d, fnorm_w, seq):
    t, d = x1.shape
    tl = FFN_TILE
    tpb = seq // tl
    hb = tl // HALO
    last = t // HALO - 1
    rows = FFN_LEAD + tl + FFN_LEAD
    return pl.pallas_call(
        functools.partial(_ffn_kernel, tiles_per_seq=tpb),
        grid=(t // tl,),
        in_specs=[pl.BlockSpec((tl, d), lambda i: (i, 0)),
                  pl.BlockSpec((tl, d), lambda i: (i, 0)),
                  pl.BlockSpec((HALO, d), lambda i: (jnp.maximum(i * hb - 1, 0), 0)),
                  pl.BlockSpec((HALO, d), lambda i: (jnp.minimum((i + 1) * hb, last), 0)),
                  pl.BlockSpec((1, N_MOD, d), lambda i: (i // tpb, 0, 0)),
                  _const_spec((1, d)),
                  _const_spec(w_u.shape), _const_spec(cw.shape), _const_spec(cb.shape),
                  _const_spec(w_d.shape), _const_spec((1, d))],
        out_specs=pl.BlockSpec((tl, d), lambda i: (i, 0)),
        out_shape=jax.ShapeDtypeStruct((t, d), F32),
        scratch_shapes=[pltpu.VMEM((rows, d), BF16),
                        pltpu.VMEM((2, 2, FFN_CHUNK // LANES, rows, LANES), F32),
                        pltpu.VMEM((tl, FFN_DIM), BF16),
                        pltpu.VMEM((d // LANES, FFN_SUB, LANES), F32)],
        compiler_params=_params(("parallel",)),
        name="ffn",
    )(x1, h2, x1, x1, mod3, norm_w, w_u, cw, cb, w_d, fnorm_w)


def _qk_perm():
    lane = np.arange(QK_WIDTH)
    pair, l = lane // LANES, lane % LANES
    head = 2 * pair + (l // (QK_DIM // 2)) % 2
    half = l // (LANES // 2)
    return head * QK_DIM + half * (QK_DIM // 2) + l % (QK_DIM // 2)


def _rope_tables(seq):
    n_rows = seq // GRID_W
    n_freq = QK_DIM // 4
    lane = jnp.arange(LANES)
    inv = ROPE_BASE ** (-(lane % n_freq).astype(F32) / n_freq)
    by_row = (lane % (2 * n_freq)) < n_freq
    sign = jnp.where(lane < LANES // 2, -1.0, 1.0).astype(F32)

    def expand(fn):
        of_row = fn(jnp.arange(n_rows, dtype=F32)[:, None] * inv[None, :])
        of_col = fn(jnp.arange(GRID_W, dtype=F32)[:, None] * inv[None, :])
        table = jnp.where(by_row[None, None, :], of_row[:, None, :], of_col[None, :, :])
        return table.reshape(seq, LANES)

    return expand(jnp.cos), expand(jnp.sin) * sign


def kernel(x, c, ctx, c_ctx, w_mod, b_mod, norm1_w, w_in, ret_decay_f, ret_decay_b,
           w_ret_out, w_four_out, w_branch_gate, b_branch_gate, w_out, norm2_w,
           w_up, conv_w, conv_b, w_down, final_norm_w):
    assert w_mod.shape[0] == 1, "single-layer block"
    b, seq, d = x.shape
    t = b * seq
    assert b == 2 and seq % FFN_TILE == 0 and seq % TOKEN_TILE == 0 and seq % RET_CHUNK == 0

    c8 = jnp.concatenate([c, c_ctx[None, :], jnp.zeros((8 - b - 1, d), F32)], axis=0)
    mod3 = _modulation(c8, w_mod[0], b_mod[0]).reshape(8, N_MOD, d)

    perm = _qk_perm()
    w_in_b = w_in[0].astype(BF16)
    w_qk = jnp.take(w_in_b, np.concatenate([perm, K_OFF + perm]), axis=1)
    n1w = norm1_w[0].reshape(1, d)

    kc, vc = _ctx_proj(ctx, mod3, n1w, w_qk, w_in_b)
    cos, sin = _rope_tables(seq)
    x2 = x.reshape(t, d)
    (q, k, v, sg, f), (w_bg, w_ro, w_fo, w_o) = _proj(
        x2, mod3, n1w, w_qk, w_in_b, cos, sin, seq,
        [w_branch_gate[0], w_ret_out[0], w_four_out[0], w_out[0]])

    a_f = jnp.broadcast_to(ret_decay_f[0][:, None, None], (HEADS, 1, LANES))
    a_b = jnp.broadcast_to(ret_decay_b[0][:, None, None], (HEADS, 1, LANES))
    ro = _retention(a_f, a_b, q.reshape(b, seq, QK_WIDTH), k.reshape(b, seq, QK_WIDTH),
                    v.reshape(b, seq, V_WIDTH), kc, vc)
    z = _fourier(f.reshape(b, seq, F_WIDTH))

    n2w = norm2_w[0].reshape(1, d)
    (x1, h2), (w_u, w_d) = _merge(
        x2, mod3, n1w, n2w, ro.reshape(t, V_WIDTH), sg, z.reshape(t, F_WIDTH),
        w_bg, b_branch_gate[0].reshape(1, 2 * d), w_ro, w_fo, w_o, seq, [w_up[0], w_down[0]])

    out = _ffn(x1, h2, mod3, n2w, w_u, conv_w[0], conv_b[0].reshape(1, 2 * FFN_DIM), w_d,
               final_norm_w.reshape(1, d), seq)
    return out.reshape(b, seq, d)
```

```python
import functools

import numpy as np
import jax
import jax.numpy as jnp
from jax import lax
from jax.experimental import pallas as pl
from jax.experimental.pallas import tpu as pltpu

F32 = jnp.float32
BF16 = jnp.bfloat16

D_MODEL = 1024
GRID_W = 64
HEADS = 8
QK_DIM = 64
V_DIM = 128
QK_WIDTH = HEADS * QK_DIM
V_WIDTH = HEADS * V_DIM
ROPE_BASE = 10000.0
F_GROUPS = 4
F_GROUP_DIM = 128
F_WIDTH = F_GROUPS * F_GROUP_DIM
K_OFF = QK_WIDTH
V_OFF = K_OFF + QK_WIDTH
G_OFF = V_OFF + V_WIDTH
F_OFF = G_OFF + V_WIDTH
IN_COLS = F_OFF + F_WIDTH
FFN_DIM = 2816
N_MOD = 6
EPS = 1e-6

LANES = 128
RET_CHUNK = 256
RET_UNROLL = 8
FFN_CHUNK = 256
N_FFN_CHUNKS = FFN_DIM // FFN_CHUNK
FFN_TILE = 512
MERGE_TILE = 1024
MERGE_SUB = 512
PROJ_TILE = 1024
PROJ_SUB = 512
DFT_N1 = 64
BF16_ROWS = 16
DFT_ROWS = BF16_ROWS
VMEM_LIMIT = 56 * 1024 * 1024


def _params(sem):
    return pltpu.CompilerParams(dimension_semantics=sem, vmem_limit_bytes=VMEM_LIMIT)


def _dot(a, b):
    return jnp.dot(a, b, preferred_element_type=F32)


def _rms_norm(x, w):
    return x * lax.rsqrt(jnp.mean(x * x, axis=-1, keepdims=True) + EPS) * w


def _const_spec(shape):
    zeros = (0,) * len(shape)
    return pl.BlockSpec(shape, lambda *_: zeros, pipeline_mode=pl.Buffered(1))


def _cast_jobs(weights, steps):
    specs, shapes = [], []
    for w in weights:
        n_rows = w.shape[0]
        rows = -(-n_rows // steps)
        while rows % BF16_ROWS or n_rows % rows:
            rows += 1
        last = n_rows // rows - 1
        specs.append(pl.BlockSpec((rows, w.shape[1]), lambda i, last=last: (jnp.minimum(i, last), 0)))
        shapes.append(jax.ShapeDtypeStruct(w.shape, BF16))
    return specs, shapes


def _run_casts(refs):
    n = len(refs) // 2
    for src, dst in zip(refs[:n], refs[n:]):
        dst[...] = src[...].astype(BF16)


def _mod_kernel(c_ref, w_ref, b_ref, win_ref, o_ref, winb_ref, wqk_ref):
    c = c_ref[...]
    s = c * jax.nn.sigmoid(c)
    o_ref[...] = _dot(s.astype(BF16), w_ref[...].astype(BF16)) + b_ref[...]
    wb = win_ref[...].astype(BF16)
    winb_ref[...] = wb
    n_qk = 2 * QK_WIDTH
    src = lax.broadcasted_iota(jnp.int32, (n_qk, n_qk), 0)
    dst = lax.broadcasted_iota(jnp.int32, (n_qk, n_qk), 1)
    select = (src == _qk_source_column(dst)).astype(BF16)
    wqk_ref[...] = _dot(wb[:, :n_qk], select).astype(BF16)


def _qk_source_column(col):
    half = QK_DIM // 2
    lane = col % LANES
    head = 2 * ((col % QK_WIDTH) // LANES) + (lane // half) % 2
    return (col // QK_WIDTH) * QK_WIDTH + head * QK_DIM + (lane // (2 * half)) * half + lane % half


def _modulation(c8, w_mod, b_mod, w_in):
    n = w_mod.shape[1]
    steps = 4
    tn = n // steps
    d, n_in = w_in.shape
    rows = d // steps
    return pl.pallas_call(
        _mod_kernel,
        grid=(steps,),
        in_specs=[_const_spec((8, D_MODEL)),
                  pl.BlockSpec((D_MODEL, tn), lambda j: (0, j)),
                  pl.BlockSpec((1, tn), lambda j: (0, j)),
                  pl.BlockSpec((rows, n_in), lambda j: (j, 0))],
        out_specs=[pl.BlockSpec((8, tn), lambda j: (0, j)),
                   pl.BlockSpec((rows, n_in), lambda j: (j, 0)),
                   pl.BlockSpec((rows, 2 * QK_WIDTH), lambda j: (j, 0))],
        out_shape=[jax.ShapeDtypeStruct((8, n), F32),
                   jax.ShapeDtypeStruct((d, n_in), BF16),
                   jax.ShapeDtypeStruct((d, 2 * QK_WIDTH), BF16)],
        compiler_params=_params(("parallel",)),
        name="mod",
    )(c8, w_mod, b_mod.reshape(1, n), w_in)


def _ctx_kernel(x_ref, mod_ref, nw_ref, wqk_ref, w_ref, k_ref, v_ref):
    x = x_ref[0]
    h = _rms_norm(x, nw_ref[...]) * (1.0 + mod_ref[0, 1:2, :]) + mod_ref[0, 0:1, :]
    hb = h.astype(BF16)
    k_ref[0] = _dot(hb, wqk_ref[:, QK_WIDTH:]).astype(BF16)
    v_ref[0] = _dot(hb, w_ref[:, V_OFF:G_OFF]).astype(BF16)


def _ctx_proj(ctx, mod3, norm_w, w_qk, w_in):
    b, lc, d = ctx.shape
    return pl.pallas_call(
        _ctx_kernel,
        grid=(b,),
        in_specs=[pl.BlockSpec((1, lc, d), lambda i: (i, 0, 0)),
                  pl.BlockSpec((1, N_MOD, d), lambda i: (2, 0, 0)),
                  _const_spec((1, d)),
                  _const_spec(w_qk.shape), _const_spec(w_in.shape)],
        out_specs=[pl.BlockSpec((1, lc, QK_WIDTH), lambda i: (i, 0, 0)),
                   pl.BlockSpec((1, lc, V_WIDTH), lambda i: (i, 0, 0))],
        out_shape=[jax.ShapeDtypeStruct((b, lc, QK_WIDTH), BF16),
                   jax.ShapeDtypeStruct((b, lc, V_WIDTH), BF16)],
        compiler_params=_params(("parallel",)),
        name="ctx_proj",
    )(ctx, mod3, norm_w, w_qk, w_in)


def _proj_kernel(x_ref, mod_ref, nw_ref, wqk_ref, w_ref, rtab_ref, ctab_ref, *refs, n_cast):
    q_ref, k_ref, v_ref, sg_ref, f_ref = refs[n_cast:n_cast + 5]
    _run_casts(refs[:n_cast] + refs[n_cast + 5:])
    lane = lax.broadcasted_iota(jnp.int32, (1, LANES), 1)
    by_row = (lane % (QK_DIM // 2)) < QK_DIM // 4

    def rope(t, trig, scale, out_ref, rows):
        for j in range(QK_WIDTH // LANES):
            tj = t[:, j * LANES:(j + 1) * LANES]
            r = tj * trig[0] + pltpu.roll(tj, LANES // 2, 1) * trig[1]
            out_ref[rows, j * LANES:(j + 1) * LANES] = (r * scale).astype(BF16)

    for r0 in range(0, x_ref.shape[0], PROJ_SUB):
        rows = slice(r0, r0 + PROJ_SUB)
        trig = []
        for cs in range(2):
            trig.append(jnp.concatenate(
                [jnp.where(by_row, rtab_ref[cs, g:g + 1, :], ctab_ref[cs])
                 for g in range(r0 // GRID_W, (r0 + PROJ_SUB) // GRID_W)], axis=0))
        x = x_ref[rows, :]
        h = _rms_norm(x, nw_ref[...]) * (1.0 + mod_ref[0, 1:2, :]) + mod_ref[0, 0:1, :]
        hb = h.astype(BF16)
        rope(_dot(hb, wqk_ref[:, :QK_WIDTH]), trig, QK_DIM ** -0.5, q_ref, rows)
        rope(_dot(hb, wqk_ref[:, QK_WIDTH:]), trig, 1.0, k_ref, rows)
        v_ref[rows, :] = _dot(hb, w_ref[:, V_OFF:G_OFF]).astype(BF16)
        g = _dot(hb, w_ref[:, G_OFF:F_OFF])
        sg_ref[rows, :] = (g * jax.nn.sigmoid(g)).astype(BF16)
        f_ref[rows, :] = _dot(hb, w_ref[:, F_OFF:IN_COLS]).astype(BF16)


def _proj(x2, mod3, norm_w, w_qk, w_in, rtab, ctab, seq, to_cast):
    t, d = x2.shape
    tl = PROJ_TILE
    tpb = seq // tl
    tok = lambda w: pl.BlockSpec((tl, w), lambda i: (i, 0))
    cast_specs, cast_shapes = _cast_jobs(to_cast, t // tl)
    out = pl.pallas_call(
        functools.partial(_proj_kernel, n_cast=len(to_cast)),
        grid=(t // tl,),
        in_specs=[tok(d),
                  pl.BlockSpec((1, N_MOD, d), lambda i: (i // tpb, 0, 0)),
                  _const_spec((1, d)),
                  _const_spec(w_qk.shape), _const_spec(w_in.shape),
                  pl.BlockSpec((2, tl // GRID_W, LANES), lambda i: (0, i % tpb, 0)),
                  _const_spec(ctab.shape)] + cast_specs,
        out_specs=[tok(QK_WIDTH), tok(QK_WIDTH), tok(V_WIDTH), tok(V_WIDTH),
                   tok(F_WIDTH)] + cast_specs,
        out_shape=[jax.ShapeDtypeStruct((t, QK_WIDTH), BF16),
                   jax.ShapeDtypeStruct((t, QK_WIDTH), BF16),
                   jax.ShapeDtypeStruct((t, V_WIDTH), BF16),
                   jax.ShapeDtypeStruct((t, V_WIDTH), BF16),
                   jax.ShapeDtypeStruct((t, F_WIDTH), BF16)] + cast_shapes,
        compiler_params=_params(("arbitrary",)),
        name="proj",
    )(x2, mod3, norm_w, w_qk, w_in, rtab, ctab, *to_cast)
    return out[:5], out[5:]


def _retention_kernel(af_ref, ab_ref, q_ref, k_ref, v_ref, kc_ref, vc_ref,
                      o_ref, ds_scr, st_scr, d_scr, tab_scr):
    c = RET_CHUNK
    seq = q_ref.shape[1]
    lc = kc_ref.shape[1]
    n = seq // c
    lgf = [-jnp.exp(af_ref[hh]) for hh in range(2)]
    lgb = [-jnp.exp(ab_ref[hh]) for hh in range(2)]
    half = QK_DIM // 2

    lane_head = (lax.broadcasted_iota(jnp.int32, (1, LANES), 1) // half) % 2
    masks = [(lane_head == hh).astype(BF16) for hh in range(2)]
    lgf_lane = jnp.where(lane_head == 0, lgf[0], lgf[1])
    lgb_lane = jnp.where(lane_head == 0, lgb[0], lgb[1])
    row_head = (lax.broadcasted_iota(jnp.int32, (LANES, 1), 0) // half) % 2
    lgf_row = jnp.where(row_head == 0, lgf[0][:, 0:1], lgf[1][:, 0:1])
    lgb_row = jnp.where(row_head == 0, lgb[0][:, 0:1], lgb[1][:, 0:1])

    pos = lax.broadcasted_iota(jnp.int32, (c, LANES), 0).astype(F32)
    tab_scr[0] = jnp.exp(lgf_lane * (pos + 1.0))
    tab_scr[1] = jnp.exp(lgb_lane * (c - pos))
    diff = (lax.broadcasted_iota(jnp.int32, (c, c), 0)
            - lax.broadcasted_iota(jnp.int32, (c, c), 1)).astype(F32)
    for hh in range(2):
        d_scr[hh] = (jnp.where(diff >= 0, jnp.exp(lgf[hh][:, 0:1] * jnp.maximum(diff, 0.0)), 0.0)
                     + jnp.where(diff <= 0, jnp.exp(lgb[hh][:, 0:1] * jnp.maximum(-diff, 0.0)), 0.0))

    def k_decays(tokens):
        t = lax.broadcasted_iota(jnp.int32, (1, tokens), 1).astype(F32)
        return jnp.exp(lgf_row * (tokens - 1.0 - t)), jnp.exp(lgb_row * t)

    def state_increment(k_rows, v_rows, decays):
        kt = jnp.transpose(k_rows.astype(F32))
        lhs = jnp.concatenate([kt * decays[0], kt * decays[1]], axis=0).astype(BF16)
        return _dot(lhs, v_rows)

    kdec = k_decays(c)

    def incr(i, carry):
        rows = pl.ds(pl.multiple_of(i * c, c), c)
        ds_scr[i] = state_increment(k_ref[0, rows, :], v_ref[0, rows, :], kdec)
        return carry

    lax.fori_loop(0, n, incr, 0, unroll=RET_UNROLL)

    s0 = state_increment(kc_ref[0], vc_ref[0], k_decays(lc))
    col_head = lax.broadcasted_iota(jnp.int32, (1, 2 * V_DIM), 1) // V_DIM
    own = (row_head == col_head).astype(F32)
    decay_f = jnp.exp(lgf_row * c) * own
    decay_b = jnp.exp(lgb_row * c) * own

    def scan_f(i, s):
        st_scr[i, 0:LANES, :] = (s * own).astype(BF16)
        return decay_f * s + ds_scr[i, 0:LANES, :]

    lax.fori_loop(0, n, scan_f, s0[0:LANES])

    def scan_b(t, s):
        i = n - 1 - t
        st_scr[i, LANES:2 * LANES, :] = (s * own).astype(BF16)
        return decay_b * s + ds_scr[i, LANES:2 * LANES, :]

    lax.fori_loop(0, n, scan_b, s0[LANES:2 * LANES])

    def outputs(i, carry):
        rows = pl.ds(pl.multiple_of(i * c, c), c)
        q = q_ref[0, rows, :]
        k = k_ref[0, rows, :]
        qf = q.astype(F32)
        qd = jnp.concatenate([qf * tab_scr[0], qf * tab_scr[1]], axis=1).astype(BF16)
        inter = _dot(qd, st_scr[i])
        for hh in range(2):
            cols = slice(hh * V_DIM, (hh + 1) * V_DIM)
            scores = lax.dot_general(q * masks[hh], k, (((1,), (1,)), ((), ())),
                                     preferred_element_type=F32)
            o = _dot((scores * d_scr[hh]).astype(BF16), v_ref[0, rows, cols]) + inter[:, cols]
            o_ref[0, rows, cols] = o.astype(BF16)
        return carry

    lax.fori_loop(0, n, outputs, 0, unroll=RET_UNROLL)


def _retention(a_f, a_b, q, k, v, kc, vc):
    b, seq, _ = v.shape
    lc = kc.shape[1]
    c = RET_CHUNK
    n = seq // c
    dec = pl.BlockSpec((2, 1, LANES), lambda i, p: (p, 0, 0))
    qk = lambda rows: pl.BlockSpec((1, rows, LANES), lambda i, p: (i, 0, p))
    vv = lambda rows: pl.BlockSpec((1, rows, 2 * V_DIM), lambda i, p: (i, 0, p))
    return pl.pallas_call(
        _retention_kernel,
        grid=(b, HEADS // 2),
        in_specs=[dec, dec, qk(seq), qk(seq), vv(seq), qk(lc), vv(lc)],
        out_specs=vv(seq),
        out_shape=jax.ShapeDtypeStruct((b, seq, V_WIDTH), BF16),
        scratch_shapes=[pltpu.VMEM((n, 2 * LANES, 2 * V_DIM), F32),
                        pltpu.VMEM((n, 2 * LANES, 2 * V_DIM), BF16),
                        pltpu.VMEM((2, c, c), F32),
                        pltpu.VMEM((2, c, LANES), F32)],
        compiler_params=_params(("parallel", "parallel")),
        name="retention",
    )(a_f, a_b, q, k, v, kc, vc)


def _dft_a_kernel(w_ref, x_ref, y_ref):
    _, n1, r, w = x_ref.shape
    y = _dot(w_ref[...], x_ref[0].reshape(n1 * r, w))
    y_ref[0] = y.reshape(2, n1, r, w).astype(BF16)


def _dft_b_kernel(tw_ref, w2_ref, cc_ref, y_ref, z_ref, z_scr):
    n2 = y_ref.shape[3]
    cc = cc_ref[...].astype(BF16)
    w2c = w2_ref[0]
    w2s = w2_ref[1]
    for j in range(DFT_ROWS):
        tc = tw_ref[j, 0:1, :]
        ts = tw_ref[j, 1:2, :]
        ec = w2c * tc - w2s * ts
        es = w2s * tc + w2c * ts
        m = jnp.concatenate([jnp.concatenate([ec, -es], axis=1),
                             jnp.concatenate([es, ec], axis=1)], axis=0).astype(BF16)
        y = jnp.concatenate([y_ref[0, 0, j], y_ref[0, 1, j]], axis=0)
        zz = _dot(m, y)
        for g in range(F_GROUPS):
            cols = slice(g * F_GROUP_DIM, (g + 1) * F_GROUP_DIM)
            zcs = jnp.concatenate([zz[:n2, cols], zz[n2:, cols]], axis=1).astype(BF16)
            z_scr[g, pl.ds(j, n2, stride=DFT_ROWS), :] = _dot(zcs, cc)
    z = jnp.concatenate([z_scr[g] for g in range(F_GROUPS)], axis=1)
    z_ref[0] = z.reshape(n2, DFT_ROWS, F_WIDTH).astype(BF16)


def _dft_tables(seq):
    n1 = DFT_N1
    n2 = seq // n1

    def cs(num, den):
        ang = 2.0 * np.pi * (num % den) / den
        return np.cos(ang), np.sin(ang)

    a = np.arange(n1)
    w_a = np.concatenate(cs(a[:, None] * a[None, :], n1), axis=0)
    m = np.arange(n2)
    tw = np.stack(cs(a[:, None] * m[None, :], seq), axis=1)
    w2 = np.stack(cs(m[:, None] * m[None, :], n2), axis=0)
    ch = np.arange(F_GROUP_DIM)
    cc, sc = cs(ch[:, None] * ch[None, :], F_GROUP_DIM)
    scale = 1.0 / np.sqrt(seq * F_GROUP_DIM)
    w_c = np.concatenate([cc, -sc], axis=0) * scale
    return [jnp.asarray(t, dtype=F32) for t in (w_a, tw, w2, w_c)]


def _fourier(f):
    b, seq, w = f.shape
    n1 = DFT_N1
    n2 = seq // n1
    w_a, tw, w2, w_c = _dft_tables(seq)
    r = DFT_ROWS
    spread = (jnp.arange(n1 * r)[None, :] // r == jnp.arange(n1)[:, None]).astype(F32)
    w_rep = jnp.dot(w_a, spread, precision=lax.Precision.HIGHEST)
    same_r = jnp.arange(2 * n1 * r)[:, None] % r == jnp.arange(n1 * r)[None, :] % r
    w_a = jnp.where(same_r, jnp.repeat(w_rep, r, axis=0), 0.0).astype(BF16)
    y = pl.pallas_call(
        _dft_a_kernel,
        grid=(b, n2 // r),
        in_specs=[_const_spec(w_a.shape),
                  pl.BlockSpec((1, n1, r, w), lambda i, j: (i, 0, j, 0))],
        out_specs=pl.BlockSpec((1, 2, n1, r, w), lambda i, j: (i, 0, 0, j, 0)),
        out_shape=jax.ShapeDtypeStruct((b, 2, n1, n2, w), BF16),
        compiler_params=_params(("parallel", "parallel")),
        name="dft_a",
    )(w_a, f.reshape(b, n1, n2, w))
    z = pl.pallas_call(
        _dft_b_kernel,
        grid=(b, n1 // r),
        in_specs=[pl.BlockSpec((r, 2, n2), lambda i, j: (j, 0, 0)),
                  _const_spec((2, n2, n2)),
                  _const_spec((2 * F_GROUP_DIM, F_GROUP_DIM)),
                  pl.BlockSpec((1, 2, r, n2, w), lambda i, j: (i, 0, j, 0, 0))],
        out_specs=pl.BlockSpec((1, n2, r, w), lambda i, j: (i, 0, j, 0)),
        out_shape=jax.ShapeDtypeStruct((b, n2, n1, w), BF16),
        scratch_shapes=[pltpu.VMEM((F_GROUPS, n2 * r, F_GROUP_DIM), F32)],
        compiler_params=_params(("parallel", "parallel")),
        name="dft_b",
    )(tw, w2, w_c, y)
    return z.reshape(b, seq, w)


def _merge_kernel(x_ref, mod_ref, nw_ref, n2w_ref, ro_ref, sg_ref, z_ref, wbg_ref, bbg_ref,
                  wro_ref, wfo_ref, wout_ref, *refs, n_cast):
    o_ref, h2_ref = refs[n_cast:n_cast + 2]
    _run_casts(refs[:n_cast] + refs[n_cast + 2:])
    for r0 in range(0, x_ref.shape[0], MERGE_SUB):
        rows = slice(r0, r0 + MERGE_SUB)
        x = x_ref[rows, :]
        h = _rms_norm(x, nw_ref[...]) * (1.0 + mod_ref[0, 1:2, :]) + mod_ref[0, 0:1, :]
        gates = jax.nn.sigmoid(_dot(h.astype(BF16), wbg_ref[...]) + bbg_ref[...])
        gated = []
        for hd in range(HEADS):
            cols = slice(hd * V_DIM, (hd + 1) * V_DIM)
            o = ro_ref[rows, cols].astype(F32)
            oc = o - jnp.mean(o, axis=-1, keepdims=True)
            var = jnp.mean(oc * oc, axis=-1, keepdims=True)
            gated.append((oc * lax.rsqrt(var + EPS) * sg_ref[rows, cols].astype(F32)).astype(BF16))
        ret_d = _dot(jnp.concatenate(gated, axis=1), wro_ref[...])
        four_d = _dot(z_ref[rows, :], wfo_ref[...])
        m = gates[:, :D_MODEL] * ret_d + gates[:, D_MODEL:] * four_d
        y = _dot(m.astype(BF16), wout_ref[...])
        x1 = x + mod_ref[0, 2:3, :] * y
        o_ref[rows, :] = x1
        h2 = _rms_norm(x1, n2w_ref[...]) * (1.0 + mod_ref[0, 4:5, :]) + mod_ref[0, 3:4, :]
        h2_ref[rows, :] = h2.astype(BF16)


def _merge(x2, mod3, norm_w, norm2_w, ro, sg, z, w_bg, b_bg, w_ro, w_fo, w_out, seq, to_cast):
    t, d = x2.shape
    tl = MERGE_TILE
    tpb = seq // tl
    tok = lambda w: pl.BlockSpec((tl, w), lambda i: (i, 0))
    cast_specs, cast_shapes = _cast_jobs(to_cast, t // tl)
    out = pl.pallas_call(
        functools.partial(_merge_kernel, n_cast=len(to_cast)),
        grid=(t // tl,),
        in_specs=[tok(d),
                  pl.BlockSpec((1, N_MOD, d), lambda i: (i // tpb, 0, 0)),
                  _const_spec((1, d)), _const_spec((1, d)),
                  tok(V_WIDTH), tok(V_WIDTH), tok(F_WIDTH),
                  _const_spec(w_bg.shape), _const_spec((1, 2 * d)),
                  _const_spec(w_ro.shape), _const_spec(w_fo.shape),
                  _const_spec(w_out.shape)] + cast_specs,
        out_specs=[tok(d), tok(d)] + cast_specs,
        out_shape=[jax.ShapeDtypeStruct((t, d), F32),
                   jax.ShapeDtypeStruct((t, d), BF16)] + cast_shapes,
        compiler_params=_params(("arbitrary",)),
        name="merge",
    )(x2, mod3, norm_w, norm2_w, ro, sg, z, w_bg, b_bg, w_ro, w_fo, w_out, *to_cast)
    return out[:2], out[2:]


HALO = 8
FFN_LEAD = 16
FFN_SUB = 256


def _two_gelu_tanh(x):
    c1 = np.sqrt(2.0 / np.pi)
    return x + x * jnp.tanh(x * (c1 + (c1 * 0.044715) * (x * x)))


def _ffn_kernel(xm_ref, hm_ref, xp_ref, xn_ref, mod_ref, nw_ref, wu_ref, cw_ref, cb_ref,
                wd_ref, fnw_ref, o_ref, h_scr, u_scr, act_scr, y_scr, *, tiles_per_seq):
    tl = xm_ref.shape[0]
    sub = FFN_SUB
    half = sub // 2
    nc = FFN_CHUNK
    i = pl.program_id(0)
    keep_prev = ((i % tiles_per_seq) != 0).astype(F32)
    keep_next = ((i % tiles_per_seq) != tiles_per_seq - 1).astype(F32)

    def pre(x):
        return _rms_norm(x, nw_ref[...]) * (1.0 + mod_ref[0, 4:5, :]) + mod_ref[0, 3:4, :]

    pad = jnp.zeros((FFN_LEAD - HALO, xm_ref.shape[1]), F32)
    h_scr[0:FFN_LEAD] = jnp.concatenate([pad, pre(xp_ref[...]) * keep_prev], axis=0).astype(BF16)
    h_scr[FFN_LEAD:FFN_LEAD + tl] = hm_ref[...]
    h_scr[FFN_LEAD + tl:] = jnp.concatenate([pre(xn_ref[...]) * keep_next, pad], axis=0).astype(BF16)

    def project(j):
        hb = h_scr[...]
        for part in range(2):
            lo = part * FFN_DIM + j * nc
            u = _dot(hb, wu_ref[:, lo:lo + nc])
            for s in range(nc // LANES):
                u_scr[j % 2, part, s] = u[:, s * LANES:(s + 1) * LANES]

    def conv(j, part, s, blk, scale):
        lo = part * FFN_DIM + j * nc + s * LANES
        w = cw_ref[:, lo:lo + LANES] * scale
        bias = cb_ref[:, lo:lo + LANES] * scale
        first = FFN_LEAD + blk * sub
        rows = lambda start: u_scr[j % 2, part, s, pl.ds(start, half, stride=2), :]
        before, even, odd, after = rows(first - 1), rows(first), rows(first + 1), rows(first + 2)
        return (before * w[0:1] + even * w[1:2] + odd * w[2:3] + bias,
                even * w[0:1] + odd * w[1:2] + after * w[2:3] + bias)

    def activate(j):
        for s in range(nc // LANES):
            cols = slice(j * nc + s * LANES, j * nc + (s + 1) * LANES)
            for blk in range(tl // sub):
                gate = conv(j, 0, s, blk, 1.0)
                val = conv(j, 1, s, blk, 0.5)
                for par in range(2):
                    r0 = blk * sub + par * half
                    act_scr[r0:r0 + half, cols] = (_two_gelu_tanh(gate[par]) * val[par]).astype(BF16)

    project(0)
    for j in range(N_FFN_CHUNKS):
        if j + 1 < N_FFN_CHUNKS:
            project(j + 1)
        activate(j)
    for blk in range(tl // sub):
        r0 = blk * sub
        y = _dot(act_scr[r0:r0 + sub, :], wd_ref[...])
        for s in range(y_scr.shape[0]):
            for par in range(2):
                y_scr[s, pl.ds(par, half, stride=2), :] = (
                    y[par * half:(par + 1) * half, s * LANES:(s + 1) * LANES])
        y = jnp.concatenate([y_scr[s] for s in range(y_scr.shape[0])], axis=1)
        x2 = xm_ref[r0:r0 + sub, :] + mod_ref[0, 5:6, :] * y
        o_ref[r0:r0 + sub, :] = _rms_norm(x2, fnw_ref[...])


def _ffn(x1, h2, mod3, norm_w, w_u, cw, cb, w_d, fnorm_w, seq):
    t, d = x1.shape
    tl = FFN_TILE
    tpb = seq // tl
    hb = tl // HALO
    last = t // HALO - 1
    rows = FFN_LEAD + tl + FFN_LEAD
    return pl.pallas_call(
        functools.partial(_ffn_kernel, tiles_per_seq=tpb),
        grid=(t // tl,),
        in_specs=[pl.BlockSpec((tl, d), lambda i: (i, 0)),
                  pl.BlockSpec((tl, d), lambda i: (i, 0)),
                  pl.BlockSpec((HALO, d), lambda i: (jnp.maximum(i * hb - 1, 0), 0)),
                  pl.BlockSpec((HALO, d), lambda i: (jnp.minimum((i + 1) * hb, last), 0)),
                  pl.BlockSpec((1, N_MOD, d), lambda i: (i // tpb, 0, 0)),
                  _const_spec((1, d)),
                  _const_spec(w_u.shape), _const_spec(cw.shape), _const_spec(cb.shape),
                  _const_spec(w_d.shape), _const_spec((1, d))],
        out_specs=pl.BlockSpec((tl, d), lambda i: (i, 0)),
        out_shape=jax.ShapeDtypeStruct((t, d), F32),
        scratch_shapes=[pltpu.VMEM((rows, d), BF16),
                        pltpu.VMEM((2, 2, FFN_CHUNK // LANES, rows, LANES), F32),
                        pltpu.VMEM((tl, FFN_DIM), BF16),
                        pltpu.VMEM((d // LANES, FFN_SUB, LANES), F32)],
        compiler_params=_params(("parallel",)),
        name="ffn",
    )(x1, h2, x1, x1, mod3, norm_w, w_u, cw, cb, w_d, fnorm_w)


def _rope_tables(seq):
    n_freq = QK_DIM // 4
    lane = jnp.arange(LANES)
    inv = ROPE_BASE ** (-(lane % n_freq).astype(F32) / n_freq)
    sign = jnp.where(lane < LANES // 2, -1.0, 1.0).astype(F32)

    def table(n):
        ang = jnp.arange(n, dtype=F32)[:, None] * inv[None, :]
        return jnp.stack([jnp.cos(ang), jnp.sin(ang) * sign])

    return table(seq // GRID_W), table(GRID_W)


def kernel(x, c, ctx, c_ctx, w_mod, b_mod, norm1_w, w_in, ret_decay_f, ret_decay_b,
           w_ret_out, w_four_out, w_branch_gate, b_branch_gate, w_out, norm2_w,
           w_up, conv_w, conv_b, w_down, final_norm_w):
    assert w_mod.shape[0] == 1, "single-layer block"
    b, seq, d = x.shape
    t = b * seq
    assert b == 2 and seq % RET_CHUNK == 0
    assert seq % FFN_TILE == 0 and seq % MERGE_TILE == 0 and seq % PROJ_TILE == 0

    c8 = jnp.concatenate([c, c_ctx[None, :], jnp.zeros((8 - b - 1, d), F32)], axis=0)
    mod, w_in_b, w_qk = _modulation(c8, w_mod[0], b_mod[0], w_in[0])
    mod3 = mod.reshape(8, N_MOD, d)
    n1w = norm1_w[0].reshape(1, d)

    kc, vc = _ctx_proj(ctx, mod3, n1w, w_qk, w_in_b)
    rtab, ctab = _rope_tables(seq)
    x2 = x.reshape(t, d)
    (q, k, v, sg, f), (w_bg, w_ro, w_fo, w_o) = _proj(
        x2, mod3, n1w, w_qk, w_in_b, rtab, ctab, seq,
        [w_branch_gate[0], w_ret_out[0], w_four_out[0], w_out[0]])

    a_f = jnp.broadcast_to(ret_decay_f[0][:, None, None], (HEADS, 1, LANES))
    a_b = jnp.broadcast_to(ret_decay_b[0][:, None, None], (HEADS, 1, LANES))
    ro = _retention(a_f, a_b, q.reshape(b, seq, QK_WIDTH), k.reshape(b, seq, QK_WIDTH),
                    v.reshape(b, seq, V_WIDTH), kc, vc)
    z = _fourier(f.reshape(b, seq, F_WIDTH))

    n2w = norm2_w[0].reshape(1, d)
    (x1, h2), (w_u, w_d) = _merge(
        x2, mod3, n1w, n2w, ro.reshape(t, V_WIDTH), sg, z.reshape(t, F_WIDTH),
        w_bg, b_branch_gate[0].reshape(1, 2 * d), w_ro, w_fo, w_o, seq, [w_up[0], w_down[0]])

    out = _ffn(x1, h2, mod3, n2w, w_u, conv_w[0], conv_b[0].reshape(1, 2 * FFN_DIM), w_d,
               final_norm_w.reshape(1, d), seq)
    return out.reshape(b, seq, d)
```

```python
import functools

import numpy as np
import jax
import jax.numpy as jnp
from jax import lax
from jax.experimental import pallas as pl
from jax.experimental.pallas import tpu as pltpu

F32 = jnp.float32
BF16 = jnp.bfloat16

D_MODEL = 1024
GRID_W = 64
HEADS = 8
QK_DIM = 64
V_DIM = 128
QK_WIDTH = HEADS * QK_DIM
V_WIDTH = HEADS * V_DIM
ROPE_BASE = 10000.0
F_GROUPS = 4
F_GROUP_DIM = 128
F_WIDTH = F_GROUPS * F_GROUP_DIM
K_OFF = QK_WIDTH
V_OFF = K_OFF + QK_WIDTH
G_OFF = V_OFF + V_WIDTH
F_OFF = G_OFF + V_WIDTH
IN_COLS = F_OFF + F_WIDTH
FFN_DIM = 2816
N_MOD = 6
EPS = 1e-6

LANES = 128
RET_CHUNK = 256
RET_UNROLL = 16
FFN_CHUNK = 256
N_FFN_CHUNKS = FFN_DIM // FFN_CHUNK
FFN_TILE = 512
MERGE_TILE = 1024
MERGE_SUB = 512
PROJ_TILE = 1024
PROJ_SUB = 512
DFT_N1 = 64
BF16_ROWS = 16
DFT_ROWS = BF16_ROWS
VMEM_LIMIT = 56 * 1024 * 1024


def _params(sem):
    return pltpu.CompilerParams(dimension_semantics=sem, vmem_limit_bytes=VMEM_LIMIT)


def _dot(a, b):
    return jnp.dot(a, b, preferred_element_type=F32)


def _rms_norm(x, w):
    return x * lax.rsqrt(jnp.mean(x * x, axis=-1, keepdims=True) + EPS) * w


def _const_spec(shape):
    zeros = (0,) * len(shape)
    return pl.BlockSpec(shape, lambda *_: zeros, pipeline_mode=pl.Buffered(1))


def _cast_jobs(weights, steps):
    specs, shapes = [], []
    for w in weights:
        n_rows = w.shape[0]
        rows = -(-n_rows // steps)
        while rows % BF16_ROWS or n_rows % rows:
            rows += 1
        last = n_rows // rows - 1
        specs.append(pl.BlockSpec((rows, w.shape[1]), lambda i, last=last: (jnp.minimum(i, last), 0)))
        shapes.append(jax.ShapeDtypeStruct(w.shape, BF16))
    return specs, shapes


def _run_casts(refs):
    n = len(refs) // 2
    for src, dst in zip(refs[:n], refs[n:]):
        dst[...] = src[...].astype(BF16)


def _mod_kernel(c_ref, w_ref, b_ref, win_ref, o_ref, winb_ref, wqk_ref):
    c = c_ref[...]
    s = c * jax.nn.sigmoid(c)
    o_ref[...] = _dot(s.astype(BF16), w_ref[...].astype(BF16)) + b_ref[...]
    wb = win_ref[...].astype(BF16)
    winb_ref[...] = wb
    n_qk = 2 * QK_WIDTH
    src = lax.broadcasted_iota(jnp.int32, (n_qk, n_qk), 0)
    dst = lax.broadcasted_iota(jnp.int32, (n_qk, n_qk), 1)
    select = (src == _qk_source_column(dst)).astype(BF16)
    wqk_ref[...] = _dot(wb[:, :n_qk], select).astype(BF16)


def _qk_source_column(col):
    half = QK_DIM // 2
    lane = col % LANES
    head = 2 * ((col % QK_WIDTH) // LANES) + (lane // half) % 2
    return (col // QK_WIDTH) * QK_WIDTH + head * QK_DIM + (lane // (2 * half)) * half + lane % half


def _modulation(c8, w_mod, b_mod, w_in):
    n = w_mod.shape[1]
    steps = 4
    tn = n // steps
    d, n_in = w_in.shape
    rows = d // steps
    return pl.pallas_call(
        _mod_kernel,
        grid=(steps,),
        in_specs=[_const_spec((8, D_MODEL)),
                  pl.BlockSpec((D_MODEL, tn), lambda j: (0, j)),
                  pl.BlockSpec((1, tn), lambda j: (0, j)),
                  pl.BlockSpec((rows, n_in), lambda j: (j, 0))],
        out_specs=[pl.BlockSpec((8, tn), lambda j: (0, j)),
                   pl.BlockSpec((rows, n_in), lambda j: (j, 0)),
                   pl.BlockSpec((rows, 2 * QK_WIDTH), lambda j: (j, 0))],
        out_shape=[jax.ShapeDtypeStruct((8, n), F32),
                   jax.ShapeDtypeStruct((d, n_in), BF16),
                   jax.ShapeDtypeStruct((d, 2 * QK_WIDTH), BF16)],
        compiler_params=_params(("parallel",)),
        name="mod",
    )(c8, w_mod, b_mod.reshape(1, n), w_in)


def _ctx_kernel(x_ref, mod_ref, nw_ref, wqk_ref, w_ref, k_ref, v_ref):
    x = x_ref[0]
    h = _rms_norm(x, nw_ref[...]) * (1.0 + mod_ref[0, 1:2, :]) + mod_ref[0, 0:1, :]
    hb = h.astype(BF16)
    k_ref[0] = _dot(hb, wqk_ref[:, QK_WIDTH:]).astype(BF16)
    v_ref[0] = _dot(hb, w_ref[:, V_OFF:G_OFF]).astype(BF16)


def _ctx_proj(ctx, mod3, norm_w, w_qk, w_in):
    b, lc, d = ctx.shape
    return pl.pallas_call(
        _ctx_kernel,
        grid=(b,),
        in_specs=[pl.BlockSpec((1, lc, d), lambda i: (i, 0, 0)),
                  pl.BlockSpec((1, N_MOD, d), lambda i: (2, 0, 0)),
                  _const_spec((1, d)),
                  _const_spec(w_qk.shape), _const_spec(w_in.shape)],
        out_specs=[pl.BlockSpec((1, lc, QK_WIDTH), lambda i: (i, 0, 0)),
                   pl.BlockSpec((1, lc, V_WIDTH), lambda i: (i, 0, 0))],
        out_shape=[jax.ShapeDtypeStruct((b, lc, QK_WIDTH), BF16),
                   jax.ShapeDtypeStruct((b, lc, V_WIDTH), BF16)],
        compiler_params=_params(("parallel",)),
        name="ctx_proj",
    )(ctx, mod3, norm_w, w_qk, w_in)


def _proj_kernel(x_ref, mod_ref, nw_ref, wqk_ref, w_ref, rtab_ref, ctab_ref, *refs, n_cast):
    q_ref, k_ref, v_ref, sg_ref, f_ref = refs[n_cast:n_cast + 5]
    _run_casts(refs[:n_cast] + refs[n_cast + 5:])
    lane = lax.broadcasted_iota(jnp.int32, (1, LANES), 1)
    by_row = (lane % (QK_DIM // 2)) < QK_DIM // 4

    def rope(t, trig, scale, out_ref, rows):
        for j in range(QK_WIDTH // LANES):
            tj = t[:, j * LANES:(j + 1) * LANES]
            r = tj * trig[0] + pltpu.roll(tj, LANES // 2, 1) * trig[1]
            out_ref[rows, j * LANES:(j + 1) * LANES] = (r * scale).astype(BF16)

    for r0 in range(0, x_ref.shape[0], PROJ_SUB):
        rows = slice(r0, r0 + PROJ_SUB)
        trig = []
        for cs in range(2):
            trig.append(jnp.concatenate(
                [jnp.where(by_row, rtab_ref[cs, g:g + 1, :], ctab_ref[cs])
                 for g in range(r0 // GRID_W, (r0 + PROJ_SUB) // GRID_W)], axis=0))
        x = x_ref[rows, :]
        h = _rms_norm(x, nw_ref[...]) * (1.0 + mod_ref[0, 1:2, :]) + mod_ref[0, 0:1, :]
        hb = h.astype(BF16)
        rope(_dot(hb, wqk_ref[:, :QK_WIDTH]), trig, QK_DIM ** -0.5, q_ref, rows)
        rope(_dot(hb, wqk_ref[:, QK_WIDTH:]), trig, 1.0, k_ref, rows)
        v_ref[rows, :] = _dot(hb, w_ref[:, V_OFF:G_OFF]).astype(BF16)
        g = _dot(hb, w_ref[:, G_OFF:F_OFF])
        sg_ref[rows, :] = (g * jax.nn.sigmoid(g)).astype(BF16)
        f_ref[rows, :] = _dot(hb, w_ref[:, F_OFF:IN_COLS]).astype(BF16)


def _proj(x2, mod3, norm_w, w_qk, w_in, rtab, ctab, seq, to_cast):
    t, d = x2.shape
    tl = PROJ_TILE
    tpb = seq // tl
    tok = lambda w: pl.BlockSpec((tl, w), lambda i: (i, 0))
    cast_specs, cast_shapes = _cast_jobs(to_cast, t // tl)
    out = pl.pallas_call(
        functools.partial(_proj_kernel, n_cast=len(to_cast)),
        grid=(t // tl,),
        in_specs=[tok(d),
                  pl.BlockSpec((1, N_MOD, d), lambda i: (i // tpb, 0, 0)),
                  _const_spec((1, d)),
                  _const_spec(w_qk.shape), _const_spec(w_in.shape),
                  pl.BlockSpec((2, tl // GRID_W, LANES), lambda i: (0, i % tpb, 0)),
                  _const_spec(ctab.shape)] + cast_specs,
        out_specs=[tok(QK_WIDTH), tok(QK_WIDTH), tok(V_WIDTH), tok(V_WIDTH),
                   tok(F_WIDTH)] + cast_specs,
        out_shape=[jax.ShapeDtypeStruct((t, QK_WIDTH), BF16),
                   jax.ShapeDtypeStruct((t, QK_WIDTH), BF16),
                   jax.ShapeDtypeStruct((t, V_WIDTH), BF16),
                   jax.ShapeDtypeStruct((t, V_WIDTH), BF16),
                   jax.ShapeDtypeStruct((t, F_WIDTH), BF16)] + cast_shapes,
        compiler_params=_params(("arbitrary",)),
        name="proj",
    )(x2, mod3, norm_w, w_qk, w_in, rtab, ctab, *to_cast)
    return out[:5], out[5:]


def _retention_kernel(af_ref, ab_ref, q_ref, k_ref, v_ref, kc_ref, vc_ref,
                      o_ref, ds_scr, st_scr, d_scr, tab_scr):
    c = RET_CHUNK
    seq = q_ref.shape[1]
    lc = kc_ref.shape[1]
    n = seq // c
    lgf = [-jnp.exp(af_ref[hh]) for hh in range(2)]
    lgb = [-jnp.exp(ab_ref[hh]) for hh in range(2)]
    half = QK_DIM // 2

    lane_head = (lax.broadcasted_iota(jnp.int32, (1, LANES), 1) // half) % 2
    masks = [(lane_head == hh).astype(BF16) for hh in range(2)]
    lgf_lane = jnp.where(lane_head == 0, lgf[0], lgf[1])
    lgb_lane = jnp.where(lane_head == 0, lgb[0], lgb[1])
    row_head = (lax.broadcasted_iota(jnp.int32, (LANES, 1), 0) // half) % 2
    lgf_row = jnp.where(row_head == 0, lgf[0][:, 0:1], lgf[1][:, 0:1])
    lgb_row = jnp.where(row_head == 0, lgb[0][:, 0:1], lgb[1][:, 0:1])

    pos = lax.broadcasted_iota(jnp.int32, (c, LANES), 0).astype(F32)
    tab_scr[0] = jnp.exp(lgf_lane * (pos + 1.0))
    tab_scr[1] = jnp.exp(lgb_lane * (c - pos))
    diff = (lax.broadcasted_iota(jnp.int32, (c, c), 0)
            - lax.broadcasted_iota(jnp.int32, (c, c), 1)).astype(F32)
    for hh in range(2):
        d_scr[hh] = (jnp.where(diff >= 0, jnp.exp(lgf[hh][:, 0:1] * jnp.maximum(diff, 0.0)), 0.0)
                     + jnp.where(diff <= 0, jnp.exp(lgb[hh][:, 0:1] * jnp.maximum(-diff, 0.0)), 0.0))

    def k_decays(tokens):
        t = lax.broadcasted_iota(jnp.int32, (1, tokens), 1).astype(F32)
        return jnp.exp(lgf_row * (tokens - 1.0 - t)), jnp.exp(lgb_row * t)

    def state_increment(k_rows, v_rows, decays):
        kt = jnp.transpose(k_rows.astype(F32))
        lhs = jnp.concatenate([kt * decays[0], kt * decays[1]], axis=0).astype(BF16)
        return _dot(lhs, v_rows)

    kdec = k_decays(c)

    def incr(i, carry):
        rows = pl.ds(pl.multiple_of(i * c, c), c)
        ds_scr[i] = state_increment(k_ref[0, rows, :], v_ref[0, rows, :], kdec)
        return carry

    lax.fori_loop(0, n, incr, 0, unroll=RET_UNROLL)

    s0 = state_increment(kc_ref[0], vc_ref[0], k_decays(lc))
    col_head = lax.broadcasted_iota(jnp.int32, (1, 2 * V_DIM), 1) // V_DIM
    own = (row_head == col_head).astype(F32)
    decay_f = jnp.exp(lgf_row * c) * own
    decay_b = jnp.exp(lgb_row * c) * own

    def scan_f(i, s):
        st_scr[i, 0:LANES, :] = (s * own).astype(BF16)
        return decay_f * s + ds_scr[i, 0:LANES, :]

    lax.fori_loop(0, n, scan_f, s0[0:LANES])

    def scan_b(t, s):
        i = n - 1 - t
        st_scr[i, LANES:2 * LANES, :] = (s * own).astype(BF16)
        return decay_b * s + ds_scr[i, LANES:2 * LANES, :]

    lax.fori_loop(0, n, scan_b, s0[LANES:2 * LANES])

    def outputs(i, carry):
        rows = pl.ds(pl.multiple_of(i * c, c), c)
        q = q_ref[0, rows, :]
        k = k_ref[0, rows, :]
        qf = q.astype(F32)
        qd = jnp.concatenate([qf * tab_scr[0], qf * tab_scr[1]], axis=1).astype(BF16)
        inter = _dot(qd, st_scr[i])
        for hh in range(2):
            cols = slice(hh * V_DIM, (hh + 1) * V_DIM)
            scores = lax.dot_general(q * masks[hh], k, (((1,), (1,)), ((), ())),
                                     preferred_element_type=F32)
            o = _dot((scores * d_scr[hh]).astype(BF16), v_ref[0, rows, cols]) + inter[:, cols]
            o_ref[0, rows, cols] = o.astype(BF16)
        return carry

    lax.fori_loop(0, n, outputs, 0, unroll=RET_UNROLL)


def _retention(a_f, a_b, q, k, v, kc, vc):
    b, seq, _ = v.shape
    lc = kc.shape[1]
    c = RET_CHUNK
    n = seq // c
    dec = pl.BlockSpec((2, 1, LANES), lambda i, p: (p, 0, 0))
    qk = lambda rows: pl.BlockSpec((1, rows, LANES), lambda i, p: (i, 0, p))
    vv = lambda rows: pl.BlockSpec((1, rows, 2 * V_DIM), lambda i, p: (i, 0, p))
    return pl.pallas_call(
        _retention_kernel,
        grid=(b, HEADS // 2),
        in_specs=[dec, dec, qk(seq), qk(seq), vv(seq), qk(lc), vv(lc)],
        out_specs=vv(seq),
        out_shape=jax.ShapeDtypeStruct((b, seq, V_WIDTH), BF16),
        scratch_shapes=[pltpu.VMEM((n, 2 * LANES, 2 * V_DIM), F32),
                        pltpu.VMEM((n, 2 * LANES, 2 * V_DIM), BF16),
                        pltpu.VMEM((2, c, c), F32),
                        pltpu.VMEM((2, c, LANES), F32)],
        compiler_params=_params(("parallel", "parallel")),
        name="retention",
    )(a_f, a_b, q, k, v, kc, vc)


def _dft_a_kernel(w_ref, x_ref, y_ref):
    _, n1, r, w = x_ref.shape
    nh = n1 // 2 + 1
    y = _dot(w_ref[...], x_ref[0].reshape(n1 * r, w)).reshape(2, nh, r, w).astype(BF16)
    y_ref[0, :, 0:nh] = y
    for k1 in range(1, n1 // 2):
        y_ref[0, 0, n1 - k1] = y[0, k1]
        y_ref[0, 1, n1 - k1] = -y[1, k1]


def _dft_b_kernel(tw_ref, w2_ref, cc_ref, y_ref, z_ref, z_scr):
    n2 = y_ref.shape[3]
    cc = cc_ref[...].astype(BF16)
    w2c = w2_ref[0]
    w2s = w2_ref[1]
    for j in range(DFT_ROWS):
        tc = tw_ref[j, 0:1, :]
        ts = tw_ref[j, 1:2, :]
        ec = w2c * tc - w2s * ts
        es = w2s * tc + w2c * ts
        m = jnp.concatenate([jnp.concatenate([ec, -es], axis=1),
                             jnp.concatenate([es, ec], axis=1)], axis=0).astype(BF16)
        y = jnp.concatenate([y_ref[0, 0, j], y_ref[0, 1, j]], axis=0)
        zz = _dot(m, y)
        for g in range(F_GROUPS):
            cols = slice(g * F_GROUP_DIM, (g + 1) * F_GROUP_DIM)
            zcs = jnp.concatenate([zz[:n2, cols], zz[n2:, cols]], axis=1).astype(BF16)
            z_scr[g, pl.ds(j, n2, stride=DFT_ROWS), :] = _dot(zcs, cc)
    z = jnp.concatenate([z_scr[g] for g in range(F_GROUPS)], axis=1)
    z_ref[0] = z.reshape(n2, DFT_ROWS, F_WIDTH).astype(BF16)


def _dft_tables(seq):
    n1 = DFT_N1
    n2 = seq // n1

    def cs(num, den):
        ang = 2.0 * np.pi * (num % den) / den
        return np.cos(ang), np.sin(ang)

    a = np.arange(n1)
    w_a = np.concatenate(cs(a[:n1 // 2 + 1, None] * a[None, :], n1), axis=0)
    m = np.arange(n2)
    tw = np.stack(cs(a[:, None] * m[None, :], seq), axis=1)
    w2 = np.stack(cs(m[:, None] * m[None, :], n2), axis=0)
    ch = np.arange(F_GROUP_DIM)
    cc, sc = cs(ch[:, None] * ch[None, :], F_GROUP_DIM)
    scale = 1.0 / np.sqrt(seq * F_GROUP_DIM)
    w_c = np.concatenate([cc, -sc], axis=0) * scale
    return [jnp.asarray(t, dtype=F32) for t in (w_a, tw, w2, w_c)]


def _fourier(f):
    b, seq, w = f.shape
    n1 = DFT_N1
    n2 = seq // n1
    w_a, tw, w2, w_c = _dft_tables(seq)
    r = DFT_ROWS
    spread = (jnp.arange(n1 * r)[None, :] // r == jnp.arange(n1)[:, None]).astype(F32)
    w_rep = jnp.dot(w_a, spread, precision=lax.Precision.HIGHEST)
    same_r = jnp.arange(w_a.shape[0] * r)[:, None] % r == jnp.arange(n1 * r)[None, :] % r
    w_a = jnp.where(same_r, jnp.repeat(w_rep, r, axis=0), 0.0).astype(BF16)
    y = pl.pallas_call(
        _dft_a_kernel,
        grid=(b, n2 // r),
        in_specs=[_const_spec(w_a.shape),
                  pl.BlockSpec((1, n1, r, w), lambda i, j: (i, 0, j, 0))],
        out_specs=pl.BlockSpec((1, 2, n1, r, w), lambda i, j: (i, 0, 0, j, 0)),
        out_shape=jax.ShapeDtypeStruct((b, 2, n1, n2, w), BF16),
        compiler_params=_params(("parallel", "parallel")),
        name="dft_a",
    )(w_a, f.reshape(b, n1, n2, w))
    z = pl.pallas_call(
        _dft_b_kernel,
        grid=(b, n1 // r),
        in_specs=[pl.BlockSpec((r, 2, n2), lambda i, j: (j, 0, 0)),
                  _const_spec((2, n2, n2)),
                  _const_spec((2 * F_GROUP_DIM, F_GROUP_DIM)),
                  pl.BlockSpec((1, 2, r, n2, w), lambda i, j: (i, 0, j, 0, 0))],
        out_specs=pl.BlockSpec((1, n2, r, w), lambda i, j: (i, 0, j, 0)),
        out_shape=jax.ShapeDtypeStruct((b, n2, n1, w), BF16),
        scratch_shapes=[pltpu.VMEM((F_GROUPS, n2 * r, F_GROUP_DIM), F32)],
        compiler_params=_params(("parallel", "parallel")),
        name="dft_b",
    )(tw, w2, w_c, y)
    return z.reshape(b, seq, w)


def _merge_kernel(x_ref, mod_ref, nw_ref, n2w_ref, ro_ref, sg_ref, z_ref, wbg_ref, bbg_ref,
                  wro_ref, wfo_ref, wout_ref, *refs, n_cast):
    o_ref, h2_ref = refs[n_cast:n_cast + 2]
    _run_casts(refs[:n_cast] + refs[n_cast + 2:])
    for r0 in range(0, x_ref.shape[0], MERGE_SUB):
        rows = slice(r0, r0 + MERGE_SUB)
        x = x_ref[rows, :]
        h = _rms_norm(x, nw_ref[...]) * (1.0 + mod_ref[0, 1:2, :]) + mod_ref[0, 0:1, :]
        gates = jax.nn.sigmoid(_dot(h.astype(BF16), wbg_ref[...]) + bbg_ref[...])
        gated = []
        for hd in range(HEADS):
            cols = slice(hd * V_DIM, (hd + 1) * V_DIM)
            o = ro_ref[rows, cols].astype(F32)
            oc = o - jnp.mean(o, axis=-1, keepdims=True)
            var = jnp.mean(oc * oc, axis=-1, keepdims=True)
            gated.append((oc * lax.rsqrt(var + EPS) * sg_ref[rows, cols].astype(F32)).astype(BF16))
        ret_d = _dot(jnp.concatenate(gated, axis=1), wro_ref[...])
        four_d = _dot(z_ref[rows, :], wfo_ref[...])
        m = gates[:, :D_MODEL] * ret_d + gates[:, D_MODEL:] * four_d
        y = _dot(m.astype(BF16), wout_ref[...])
        x1 = x + mod_ref[0, 2:3, :] * y
        o_ref[rows, :] = x1
        h2 = _rms_norm(x1, n2w_ref[...]) * (1.0 + mod_ref[0, 4:5, :]) + mod_ref[0, 3:4, :]
        h2_ref[rows, :] = h2.astype(BF16)


def _merge(x2, mod3, norm_w, norm2_w, ro, sg, z, w_bg, b_bg, w_ro, w_fo, w_out, seq, to_cast):
    t, d = x2.shape
    tl = MERGE_TILE
    tpb = seq // tl
    tok = lambda w: pl.BlockSpec((tl, w), lambda i: (i, 0))
    cast_specs, cast_shapes = _cast_jobs(to_cast, t // tl)
    out = pl.pallas_call(
        functools.partial(_merge_kernel, n_cast=len(to_cast)),
        grid=(t // tl,),
        in_specs=[tok(d),
                  pl.BlockSpec((1, N_MOD, d), lambda i: (i // tpb, 0, 0)),
                  _const_spec((1, d)), _const_spec((1, d)),
                  tok(V_WIDTH), tok(V_WIDTH), tok(F_WIDTH),
                  _const_spec(w_bg.shape), _const_spec((1, 2 * d)),
                  _const_spec(w_ro.shape), _const_spec(w_fo.shape),
                  _const_spec(w_out.shape)] + cast_specs,
        out_specs=[tok(d), tok(d)] + cast_specs,
        out_shape=[jax.ShapeDtypeStruct((t, d), F32),
                   jax.ShapeDtypeStruct((t, d), BF16)] + cast_shapes,
        compiler_params=_params(("arbitrary",)),
        name="merge",
    )(x2, mod3, norm_w, norm2_w, ro, sg, z, w_bg, b_bg, w_ro, w_fo, w_out, *to_cast)
    return out[:2], out[2:]


HALO = 8
FFN_LEAD = 16
FFN_SUB = 256


def _two_gelu_tanh(x):
    c1 = np.sqrt(2.0 / np.pi)
    return x + x * jnp.tanh(x * (c1 + (c1 * 0.044715) * (x * x)))


def _ffn_kernel(xm_ref, hm_ref, xp_ref, xn_ref, mod_ref, nw_ref, wu_ref, cw_ref, cb_ref,
                wd_ref, fnw_ref, o_ref, h_scr, u_scr, act_scr, y_scr, *, tiles_per_seq):
    tl = xm_ref.shape[0]
    sub = FFN_SUB
    half = sub // 2
    nc = FFN_CHUNK
    i = pl.program_id(0)
    keep_prev = ((i % tiles_per_seq) != 0).astype(F32)
    keep_next = ((i % tiles_per_seq) != tiles_per_seq - 1).astype(F32)

    def pre(x):
        return _rms_norm(x, nw_ref[...]) * (1.0 + mod_ref[0, 4:5, :]) + mod_ref[0, 3:4, :]

    pad = jnp.zeros((FFN_LEAD - HALO, xm_ref.shape[1]), F32)
    h_scr[0:FFN_LEAD] = jnp.concatenate([pad, pre(xp_ref[...]) * keep_prev], axis=0).astype(BF16)
    h_scr[FFN_LEAD:FFN_LEAD + tl] = hm_ref[...]
    h_scr[FFN_LEAD + tl:] = jnp.concatenate([pre(xn_ref[...]) * keep_next, pad], axis=0).astype(BF16)

    def project(j):
        hb = h_scr[...]
        for part in range(2):
            lo = part * FFN_DIM + j * nc
            u = _dot(hb, wu_ref[:, lo:lo + nc])
            for s in range(nc // LANES):
                u_scr[j % 2, part, s] = u[:, s * LANES:(s + 1) * LANES]

    def conv(j, part, s, blk, scale):
        lo = part * FFN_DIM + j * nc + s * LANES
        w = cw_ref[:, lo:lo + LANES] * scale
        bias = cb_ref[:, lo:lo + LANES] * scale
        first = FFN_LEAD + blk * sub
        rows = lambda start: u_scr[j % 2, part, s, pl.ds(start, half, stride=2), :]
        before, even, odd, after = rows(first - 1), rows(first), rows(first + 1), rows(first + 2)
        return (before * w[0:1] + even * w[1:2] + odd * w[2:3] + bias,
                even * w[0:1] + odd * w[1:2] + after * w[2:3] + bias)

    def activate(j):
        for s in range(nc // LANES):
            cols = slice(j * nc + s * LANES, j * nc + (s + 1) * LANES)
            for blk in range(tl // sub):
                gate = conv(j, 0, s, blk, 1.0)
                val = conv(j, 1, s, blk, 0.5)
                for par in range(2):
                    r0 = blk * sub + par * half
                    act_scr[r0:r0 + half, cols] = (_two_gelu_tanh(gate[par]) * val[par]).astype(BF16)

    project(0)
    for j in range(N_FFN_CHUNKS):
        if j + 1 < N_FFN_CHUNKS:
            project(j + 1)
        activate(j)
    for blk in range(tl // sub):
        r0 = blk * sub
        y = _dot(act_scr[r0:r0 + sub, :], wd_ref[...])
        for s in range(y_scr.shape[0]):
            for par in range(2):
                y_scr[s, pl.ds(par, half, stride=2), :] = (
                    y[par * half:(par + 1) * half, s * LANES:(s + 1) * LANES])
        y = jnp.concatenate([y_scr[s] for s in range(y_scr.shape[0])], axis=1)
        x2 = xm_ref[r0:r0 + sub, :] + mod_ref[0, 5:6, :] * y
        o_ref[r0:r0 + sub, :] = _rms_norm(x2, fnw_ref[...])


def _ffn(x1, h2, mod3, norm_w, w_u, cw, cb, w_d, fnorm_w, seq):
    t, d = x1.shape
    tl = FFN_TILE
    tpb = seq // tl
    hb = tl // HALO
    last = t // HALO - 1
    rows = FFN_LEAD + tl + FFN_LEAD
    return pl.pallas_call(
        functools.partial(_ffn_kernel, tiles_per_seq=tpb),
        grid=(t // tl,),
        in_specs=[pl.BlockSpec((tl, d), lambda i: (i, 0)),
                  pl.BlockSpec((tl, d), lambda i: (i, 0)),
                  pl.BlockSpec((HALO, d), lambda i: (jnp.maximum(i * hb - 1, 0), 0)),
                  pl.BlockSpec((HALO, d), lambda i: (jnp.minimum((i + 1) * hb, last), 0)),
                  pl.BlockSpec((1, N_MOD, d), lambda i: (i // tpb, 0, 0)),
                  _const_spec((1, d)),
                  _const_spec(w_u.shape), _const_spec(cw.shape), _const_spec(cb.shape),
                  _const_spec(w_d.shape), _const_spec((1, d))],
        out_specs=pl.BlockSpec((tl, d), lambda i: (i, 0)),
        out_shape=jax.ShapeDtypeStruct((t, d), F32),
        scratch_shapes=[pltpu.VMEM((rows, d), BF16),
                        pltpu.VMEM((2, 2, FFN_CHUNK // LANES, rows, LANES), F32),
                        pltpu.VMEM((tl, FFN_DIM), BF16),
                        pltpu.VMEM((d // LANES, FFN_SUB, LANES), F32)],
        compiler_params=_params(("parallel",)),
        name="ffn",
    )(x1, h2, x1, x1, mod3, norm_w, w_u, cw, cb, w_d, fnorm_w)


def _rope_tables(seq):
    n_freq = QK_DIM // 4
    lane = jnp.arange(LANES)
    inv = ROPE_BASE ** (-(lane % n_freq).astype(F32) / n_freq)
    sign = jnp.where(lane < LANES // 2, -1.0, 1.0).astype(F32)

    def table(n):
        ang = jnp.arange(n, dtype=F32)[:, None] * inv[None, :]
        return jnp.stack([jnp.cos(ang), jnp.sin(ang) * sign])

    return table(seq // GRID_W), table(GRID_W)


def kernel(x, c, ctx, c_ctx, w_mod, b_mod, norm1_w, w_in, ret_decay_f, ret_decay_b,
           w_ret_out, w_four_out, w_branch_gate, b_branch_gate, w_out, norm2_w,
           w_up, conv_w, conv_b, w_down, final_norm_w):
    assert w_mod.shape[0] == 1, "single-layer block"
    b, seq, d = x.shape
    t = b * seq
    assert b == 2 and seq % RET_CHUNK == 0
    assert seq % FFN_TILE == 0 and seq % MERGE_TILE == 0 and seq % PROJ_TILE == 0

    c8 = jnp.concatenate([c, c_ctx[None, :], jnp.zeros((8 - b - 1, d), F32)], axis=0)
    mod, w_in_b, w_qk = _modulation(c8, w_mod[0], b_mod[0], w_in[0])
    mod3 = mod.reshape(8, N_MOD, d)
    n1w = norm1_w[0].reshape(1, d)

    kc, vc = _ctx_proj(ctx, mod3, n1w, w_qk, w_in_b)
    rtab, ctab = _rope_tables(seq)
    x2 = x.reshape(t, d)
    (q, k, v, sg, f), (w_bg, w_ro, w_fo, w_o) = _proj(
        x2, mod3, n1w, w_qk, w_in_b, rtab, ctab, seq,
        [w_branch_gate[0], w_ret_out[0], w_four_out[0], w_out[0]])

    a_f = jnp.broadcast_to(ret_decay_f[0][:, None, None], (HEADS, 1, LANES))
    a_b = jnp.broadcast_to(ret_decay_b[0][:, None, None], (HEADS, 1, LANES))
    ro = _retention(a_f, a_b, q.reshape(b, seq, QK_WIDTH), k.reshape(b, seq, QK_WIDTH),
                    v.reshape(b, seq, V_WIDTH), kc, vc)
    z = _fourier(f.reshape(b, seq, F_WIDTH))

    n2w = norm2_w[0].reshape(1, d)
    (x1, h2), (w_u, w_d) = _merge(
        x2, mod3, n1w, n2w, ro.reshape(t, V_WIDTH), sg, z.reshape(t, F_WIDTH),
        w_bg, b_branch_gate[0].reshape(1, 2 * d), w_ro, w_fo, w_o, seq, [w_up[0], w_down[0]])

    out = _ffn(x1, h2, mod3, n2w, w_u, conv_w[0], conv_b[0].reshape(1, 2 * FFN_DIM), w_d,
               final_norm_w.reshape(1, d), seq)
    return out.reshape(b, seq, d)
```

```python
import functools

import numpy as np
import jax
import jax.numpy as jnp
from jax import lax
from jax.experimental import pallas as pl
from jax.experimental.pallas import tpu as pltpu

F32 = jnp.float32
BF16 = jnp.bfloat16

D_MODEL = 1024
GRID_W = 64
HEADS = 8
QK_DIM = 64
V_DIM = 128
QK_WIDTH = HEADS * QK_DIM
V_WIDTH = HEADS * V_DIM
ROPE_BASE = 10000.0
F_GROUPS = 4
F_GROUP_DIM = 128
F_WIDTH = F_GROUPS * F_GROUP_DIM
K_OFF = QK_WIDTH
V_OFF = K_OFF + QK_WIDTH
G_OFF = V_OFF + V_WIDTH
F_OFF = G_OFF + V_WIDTH
IN_COLS = F_OFF + F_WIDTH
FFN_DIM = 2816
N_MOD = 6
EPS = 1e-6

LANES = 128
RET_CHUNK = 256
RET_UNROLL = 16
FFN_CHUNK = 256
N_FFN_CHUNKS = FFN_DIM // FFN_CHUNK
FFN_TILE = 512
MERGE_TILE = 1024
MERGE_SUB = 512
PROJ_TILE = 1024
PROJ_SUB = 512
DFT_N1 = 64
BF16_ROWS = 16
DFT_ROWS = BF16_ROWS
VMEM_LIMIT = 56 * 1024 * 1024


def _params(sem):
    return pltpu.CompilerParams(dimension_semantics=sem, vmem_limit_bytes=VMEM_LIMIT)


def _dot(a, b):
    return jnp.dot(a, b, preferred_element_type=F32)


def _rms_norm(x, w):
    return x * lax.rsqrt(jnp.mean(x * x, axis=-1, keepdims=True) + EPS) * w


def _const_spec(shape):
    zeros = (0,) * len(shape)
    return pl.BlockSpec(shape, lambda *_: zeros, pipeline_mode=pl.Buffered(1))


def _cast_jobs(weights, steps):
    specs, shapes = [], []
    for w in weights:
        n_rows = w.shape[0]
        rows = -(-n_rows // steps)
        while rows % BF16_ROWS or n_rows % rows:
            rows += 1
        last = n_rows // rows - 1
        specs.append(pl.BlockSpec((rows, w.shape[1]), lambda i, last=last: (jnp.minimum(i, last), 0)))
        shapes.append(jax.ShapeDtypeStruct(w.shape, BF16))
    return specs, shapes


def _run_casts(refs):
    n = len(refs) // 2
    for src, dst in zip(refs[:n], refs[n:]):
        dst[...] = src[...].astype(BF16)


def _mod_kernel(c_ref, w_ref, b_ref, win_ref, o_ref, winb_ref, wqk_ref):
    c = c_ref[...]
    s = c * jax.nn.sigmoid(c)
    o_ref[...] = _dot(s.astype(BF16), w_ref[...].astype(BF16)) + b_ref[...]
    wb = win_ref[...].astype(BF16)
    winb_ref[...] = wb
    n_qk = 2 * QK_WIDTH
    src = lax.broadcasted_iota(jnp.int32, (n_qk, n_qk), 0)
    dst = lax.broadcasted_iota(jnp.int32, (n_qk, n_qk), 1)
    select = (src == _qk_source_column(dst)).astype(BF16)
    wqk_ref[...] = _dot(wb[:, :n_qk], select).astype(BF16)


def _qk_source_column(col):
    half = QK_DIM // 2
    lane = col % LANES
    head = 2 * ((col % QK_WIDTH) // LANES) + (lane // half) % 2
    return (col // QK_WIDTH) * QK_WIDTH + head * QK_DIM + (lane // (2 * half)) * half + lane % half


def _modulation(c8, w_mod, b_mod, w_in):
    n = w_mod.shape[1]
    steps = 4
    tn = n // steps
    d, n_in = w_in.shape
    rows = d // steps
    return pl.pallas_call(
        _mod_kernel,
        grid=(steps,),
        in_specs=[_const_spec((8, D_MODEL)),
                  pl.BlockSpec((D_MODEL, tn), lambda j: (0, j)),
                  pl.BlockSpec((1, tn), lambda j: (0, j)),
                  pl.BlockSpec((rows, n_in), lambda j: (j, 0))],
        out_specs=[pl.BlockSpec((8, tn), lambda j: (0, j)),
                   pl.BlockSpec((rows, n_in), lambda j: (j, 0)),
                   pl.BlockSpec((rows, 2 * QK_WIDTH), lambda j: (j, 0))],
        out_shape=[jax.ShapeDtypeStruct((8, n), F32),
                   jax.ShapeDtypeStruct((d, n_in), BF16),
                   jax.ShapeDtypeStruct((d, 2 * QK_WIDTH), BF16)],
        compiler_params=_params(("parallel",)),
        name="mod",
    )(c8, w_mod, b_mod.reshape(1, n), w_in)


def _ctx_kernel(x_ref, mod_ref, nw_ref, wqk_ref, w_ref, k_ref, v_ref):
    x = x_ref[0]
    h = _rms_norm(x, nw_ref[...]) * (1.0 + mod_ref[0, 1:2, :]) + mod_ref[0, 0:1, :]
    hb = h.astype(BF16)
    k_ref[0] = _dot(hb, wqk_ref[:, QK_WIDTH:]).astype(BF16)
    v_ref[0] = _dot(hb, w_ref[:, V_OFF:G_OFF]).astype(BF16)


def _ctx_proj(ctx, mod3, norm_w, w_qk, w_in):
    b, lc, d = ctx.shape
    return pl.pallas_call(
        _ctx_kernel,
        grid=(b,),
        in_specs=[pl.BlockSpec((1, lc, d), lambda i: (i, 0, 0)),
                  pl.BlockSpec((1, N_MOD, d), lambda i: (2, 0, 0)),
                  _const_spec((1, d)),
                  _const_spec(w_qk.shape), _const_spec(w_in.shape)],
        out_specs=[pl.BlockSpec((1, lc, QK_WIDTH), lambda i: (i, 0, 0)),
                   pl.BlockSpec((1, lc, V_WIDTH), lambda i: (i, 0, 0))],
        out_shape=[jax.ShapeDtypeStruct((b, lc, QK_WIDTH), BF16),
                   jax.ShapeDtypeStruct((b, lc, V_WIDTH), BF16)],
        compiler_params=_params(("parallel",)),
        name="ctx_proj",
    )(ctx, mod3, norm_w, w_qk, w_in)


def _proj_kernel(x_ref, mod_ref, nw_ref, wqk_ref, w_ref, rtab_ref, ctab_ref, *refs, n_cast):
    q_ref, k_ref, v_ref, sg_ref, f_ref = refs[n_cast:n_cast + 5]
    _run_casts(refs[:n_cast] + refs[n_cast + 5:])
    lane = lax.broadcasted_iota(jnp.int32, (1, LANES), 1)
    by_row = (lane % (QK_DIM // 2)) < QK_DIM // 4

    def rope(t, trig, scale, out_ref, rows):
        for j in range(QK_WIDTH // LANES):
            tj = t[:, j * LANES:(j + 1) * LANES]
            r = tj * trig[0] + pltpu.roll(tj, LANES // 2, 1) * trig[1]
            out_ref[rows, j * LANES:(j + 1) * LANES] = (r * scale).astype(BF16)

    for r0 in range(0, x_ref.shape[0], PROJ_SUB):
        rows = slice(r0, r0 + PROJ_SUB)
        trig = []
        for cs in range(2):
            trig.append(jnp.concatenate(
                [jnp.where(by_row, rtab_ref[cs, g:g + 1, :], ctab_ref[cs])
                 for g in range(r0 // GRID_W, (r0 + PROJ_SUB) // GRID_W)], axis=0))
        x = x_ref[rows, :]
        h = _rms_norm(x, nw_ref[...]) * (1.0 + mod_ref[0, 1:2, :]) + mod_ref[0, 0:1, :]
        hb = h.astype(BF16)
        rope(_dot(hb, wqk_ref[:, :QK_WIDTH]), trig, QK_DIM ** -0.5, q_ref, rows)
        rope(_dot(hb, wqk_ref[:, QK_WIDTH:]), trig, 1.0, k_ref, rows)
        v_ref[rows, :] = _dot(hb, w_ref[:, V_OFF:G_OFF]).astype(BF16)
        g = _dot(hb, w_ref[:, G_OFF:F_OFF])
        sg_ref[rows, :] = (g * jax.nn.sigmoid(g)).astype(BF16)
        f_ref[rows, :] = _dot(hb, w_ref[:, F_OFF:IN_COLS]).astype(BF16)


def _proj(x2, mod3, norm_w, w_qk, w_in, rtab, ctab, seq, to_cast):
    t, d = x2.shape
    tl = PROJ_TILE
    tpb = seq // tl
    tok = lambda w: pl.BlockSpec((tl, w), lambda i: (i, 0))
    cast_specs, cast_shapes = _cast_jobs(to_cast, t // tl)
    out = pl.pallas_call(
        functools.partial(_proj_kernel, n_cast=len(to_cast)),
        grid=(t // tl,),
        in_specs=[tok(d),
                  pl.BlockSpec((1, N_MOD, d), lambda i: (i // tpb, 0, 0)),
                  _const_spec((1, d)),
                  _const_spec(w_qk.shape), _const_spec(w_in.shape),
                  pl.BlockSpec((2, tl // GRID_W, LANES), lambda i: (0, i % tpb, 0)),
                  _const_spec(ctab.shape)] + cast_specs,
        out_specs=[tok(QK_WIDTH), tok(QK_WIDTH), tok(V_WIDTH), tok(V_WIDTH),
                   tok(F_WIDTH)] + cast_specs,
        out_shape=[jax.ShapeDtypeStruct((t, QK_WIDTH), BF16),
                   jax.ShapeDtypeStruct((t, QK_WIDTH), BF16),
                   jax.ShapeDtypeStruct((t, V_WIDTH), BF16),
                   jax.ShapeDtypeStruct((t, V_WIDTH), BF16),
                   jax.ShapeDtypeStruct((t, F_WIDTH), BF16)] + cast_shapes,
        compiler_params=_params(("arbitrary",)),
        name="proj",
    )(x2, mod3, norm_w, w_qk, w_in, rtab, ctab, *to_cast)
    return out[:5], out[5:]


def _retention_kernel(af_ref, ab_ref, q_ref, k_ref, v_ref, kc_ref, vc_ref,
                      tw_ref, w2_ref, cc_ref, y_ref,
                      o_ref, z_ref, ds_scr, st_scr, d_scr, tab_scr, z_scr):
    c = RET_CHUNK
    seq = q_ref.shape[1]
    lc = kc_ref.shape[1]
    n = seq // c
    lgf = [-jnp.exp(af_ref[hh]) for hh in range(2)]
    lgb = [-jnp.exp(ab_ref[hh]) for hh in range(2)]
    half = QK_DIM // 2

    lane_head = (lax.broadcasted_iota(jnp.int32, (1, LANES), 1) // half) % 2
    masks = [(lane_head == hh).astype(BF16) for hh in range(2)]
    lgf_lane = jnp.where(lane_head == 0, lgf[0], lgf[1])
    lgb_lane = jnp.where(lane_head == 0, lgb[0], lgb[1])
    row_head = (lax.broadcasted_iota(jnp.int32, (LANES, 1), 0) // half) % 2
    lgf_row = jnp.where(row_head == 0, lgf[0][:, 0:1], lgf[1][:, 0:1])
    lgb_row = jnp.where(row_head == 0, lgb[0][:, 0:1], lgb[1][:, 0:1])

    pos = lax.broadcasted_iota(jnp.int32, (c, LANES), 0).astype(F32)
    tab_scr[0] = jnp.exp(lgf_lane * (pos + 1.0))
    tab_scr[1] = jnp.exp(lgb_lane * (c - pos))
    diff = (lax.broadcasted_iota(jnp.int32, (c, c), 0)
            - lax.broadcasted_iota(jnp.int32, (c, c), 1)).astype(F32)
    for hh in range(2):
        d_scr[hh] = (jnp.where(diff >= 0, jnp.exp(lgf[hh][:, 0:1] * jnp.maximum(diff, 0.0)), 0.0)
                     + jnp.where(diff <= 0, jnp.exp(lgb[hh][:, 0:1] * jnp.maximum(-diff, 0.0)), 0.0))

    def k_decays(tokens):
        t = lax.broadcasted_iota(jnp.int32, (1, tokens), 1).astype(F32)
        return jnp.exp(lgf_row * (tokens - 1.0 - t)), jnp.exp(lgb_row * t)

    def state_increment(k_rows, v_rows, decays):
        kt = jnp.transpose(k_rows.astype(F32))
        lhs = jnp.concatenate([kt * decays[0], kt * decays[1]], axis=0).astype(BF16)
        return _dot(lhs, v_rows)

    kdec = k_decays(c)

    def incr(i, carry):
        rows = pl.ds(pl.multiple_of(i * c, c), c)
        ds_scr[i] = state_increment(k_ref[0, rows, :], v_ref[0, rows, :], kdec)
        return carry

    lax.fori_loop(0, n, incr, 0, unroll=RET_UNROLL)

    s0 = state_increment(kc_ref[0], vc_ref[0], k_decays(lc))
    col_head = lax.broadcasted_iota(jnp.int32, (1, 2 * V_DIM), 1) // V_DIM
    own = (row_head == col_head).astype(F32)
    decay_f = jnp.exp(lgf_row * c) * own
    decay_b = jnp.exp(lgb_row * c) * own

    def scan_f(i, s):
        st_scr[i, 0:LANES, :] = (s * own).astype(BF16)
        return decay_f * s + ds_scr[i, 0:LANES, :]

    lax.fori_loop(0, n, scan_f, s0[0:LANES])

    def scan_b(t, s):
        i = n - 1 - t
        st_scr[i, LANES:2 * LANES, :] = (s * own).astype(BF16)
        return decay_b * s + ds_scr[i, LANES:2 * LANES, :]

    lax.fori_loop(0, n, scan_b, s0[LANES:2 * LANES])

    def outputs(i):
        rows = pl.ds(pl.multiple_of(i * c, c), c)
        q = q_ref[0, rows, :]
        k = k_ref[0, rows, :]
        qf = q.astype(F32)
        qd = jnp.concatenate([qf * tab_scr[0], qf * tab_scr[1]], axis=1).astype(BF16)
        inter = _dot(qd, st_scr[i])
        for hh in range(2):
            cols = slice(hh * V_DIM, (hh + 1) * V_DIM)
            scores = lax.dot_general(q * masks[hh], k, (((1,), (1,)), ((), ())),
                                     preferred_element_type=F32)
            o = _dot((scores * d_scr[hh]).astype(BF16), v_ref[0, rows, cols]) + inter[:, cols]
            o_ref[0, rows, cols] = o.astype(BF16)

    per_k1 = n // DFT_ROWS
    cc = cc_ref[...].astype(BF16)

    def outputs_and_dft(j, carry):
        for u in range(per_k1):
            outputs(j * per_k1 + u)
        _dft_second_stage(j, tw_ref, w2_ref, cc, y_ref, z_scr)
        return carry

    lax.fori_loop(0, DFT_ROWS, outputs_and_dft, 0, unroll=RET_UNROLL // per_k1)
    _dft_emit(z_scr, z_ref)


def _retention_and_dft(a_f, a_b, q, k, v, kc, vc, y, tw, w2, w_c):
    b, seq, _ = v.shape
    lc = kc.shape[1]
    c = RET_CHUNK
    n = seq // c
    r = DFT_ROWS
    _, _, n1, n2, w = y.shape
    assert n % r == 0 and n1 // r == HEADS // 2
    dec = pl.BlockSpec((2, 1, LANES), lambda i, p: (p, 0, 0))
    qk = lambda rows: pl.BlockSpec((1, rows, LANES), lambda i, p: (i, 0, p))
    vv = lambda rows: pl.BlockSpec((1, rows, 2 * V_DIM), lambda i, p: (i, 0, p))
    return pl.pallas_call(
        _retention_kernel,
        grid=(b, HEADS // 2),
        in_specs=[dec, dec, qk(seq), qk(seq), vv(seq), qk(lc), vv(lc),
                  pl.BlockSpec((r, 2, n2), lambda i, p: (p, 0, 0)),
                  _const_spec((2, n2, n2)),
                  _const_spec((2 * F_GROUP_DIM, F_GROUP_DIM)),
                  pl.BlockSpec((1, 2, r, n2, w), lambda i, p: (i, 0, p, 0, 0))],
        out_specs=[vv(seq), pl.BlockSpec((1, n2, r, w), lambda i, p: (i, 0, p, 0))],
        out_shape=[jax.ShapeDtypeStruct((b, seq, V_WIDTH), BF16),
                   jax.ShapeDtypeStruct((b, n2, n1, w), BF16)],
        scratch_shapes=[pltpu.VMEM((n, 2 * LANES, 2 * V_DIM), F32),
                        pltpu.VMEM((n, 2 * LANES, 2 * V_DIM), BF16),
                        pltpu.VMEM((2, c, c), F32),
                        pltpu.VMEM((2, c, LANES), F32),
                        pltpu.VMEM((F_GROUPS, n2 * r, F_GROUP_DIM), F32)],
        compiler_params=_params(("parallel", "parallel")),
        name="retention_dft",
    )(a_f, a_b, q, k, v, kc, vc, tw, w2, w_c, y)


def _dft_a_kernel(w_ref, x_ref, y_ref):
    _, n1, r, w = x_ref.shape
    nh = n1 // 2 + 1
    y = _dot(w_ref[...], x_ref[0].reshape(n1 * r, w)).reshape(2, nh, r, w).astype(BF16)
    y_ref[0, :, 0:nh] = y
    for k1 in range(1, n1 // 2):
        y_ref[0, 0, n1 - k1] = y[0, k1]
        y_ref[0, 1, n1 - k1] = -y[1, k1]


def _dft_second_stage(j, tw_ref, w2_ref, cc, y_ref, z_scr):
    n2 = y_ref.shape[3]
    w2c = w2_ref[0]
    w2s = w2_ref[1]
    tc = tw_ref[j, 0:1, :]
    ts = tw_ref[j, 1:2, :]
    ec = w2c * tc - w2s * ts
    es = w2s * tc + w2c * ts
    m = jnp.concatenate([jnp.concatenate([ec, -es], axis=1),
                         jnp.concatenate([es, ec], axis=1)], axis=0).astype(BF16)
    y = jnp.concatenate([y_ref[0, 0, j], y_ref[0, 1, j]], axis=0)
    zz = _dot(m, y)
    for g in range(F_GROUPS):
        cols = slice(g * F_GROUP_DIM, (g + 1) * F_GROUP_DIM)
        zcs = jnp.concatenate([zz[:n2, cols], zz[n2:, cols]], axis=1).astype(BF16)
        z_scr[g, pl.ds(j, n2, stride=DFT_ROWS), :] = _dot(zcs, cc)


def _dft_emit(z_scr, z_ref):
    n2 = z_ref.shape[1]
    z = jnp.concatenate([z_scr[g] for g in range(F_GROUPS)], axis=1)
    z_ref[0] = z.reshape(n2, DFT_ROWS, F_WIDTH).astype(BF16)


def _dft_tables(seq):
    n1 = DFT_N1
    n2 = seq // n1

    def cs(num, den):
        ang = 2.0 * np.pi * (num % den) / den
        return np.cos(ang), np.sin(ang)

    a = np.arange(n1)
    w_a = np.concatenate(cs(a[:n1 // 2 + 1, None] * a[None, :], n1), axis=0)
    m = np.arange(n2)
    tw = np.stack(cs(a[:, None] * m[None, :], seq), axis=1)
    w2 = np.stack(cs(m[:, None] * m[None, :], n2), axis=0)
    ch = np.arange(F_GROUP_DIM)
    cc, sc = cs(ch[:, None] * ch[None, :], F_GROUP_DIM)
    scale = 1.0 / np.sqrt(seq * F_GROUP_DIM)
    w_c = np.concatenate([cc, -sc], axis=0) * scale
    return [jnp.asarray(t, dtype=F32) for t in (w_a, tw, w2, w_c)]


def _dft_first_stage(f):
    b, seq, w = f.shape
    n1 = DFT_N1
    n2 = seq // n1
    w_a, tw, w2, w_c = _dft_tables(seq)
    r = DFT_ROWS
    spread = (jnp.arange(n1 * r)[None, :] // r == jnp.arange(n1)[:, None]).astype(F32)
    w_rep = jnp.dot(w_a, spread, precision=lax.Precision.HIGHEST)
    same_r = jnp.arange(w_a.shape[0] * r)[:, None] % r == jnp.arange(n1 * r)[None, :] % r
    w_a = jnp.where(same_r, jnp.repeat(w_rep, r, axis=0), 0.0).astype(BF16)
    y = pl.pallas_call(
        _dft_a_kernel,
        grid=(b, n2 // r),
        in_specs=[_const_spec(w_a.shape),
                  pl.BlockSpec((1, n1, r, w), lambda i, j: (i, 0, j, 0))],
        out_specs=pl.BlockSpec((1, 2, n1, r, w), lambda i, j: (i, 0, 0, j, 0)),
        out_shape=jax.ShapeDtypeStruct((b, 2, n1, n2, w), BF16),
        compiler_params=_params(("parallel", "parallel")),
        name="dft_a",
    )(w_a, f.reshape(b, n1, n2, w))
    return y, tw, w2, w_c


def _merge_kernel(x_ref, mod_ref, nw_ref, n2w_ref, ro_ref, sg_ref, z_ref, wbg_ref, bbg_ref,
                  wro_ref, wfo_ref, wout_ref, *refs, n_cast):
    o_ref, h2_ref = refs[n_cast:n_cast + 2]
    _run_casts(refs[:n_cast] + refs[n_cast + 2:])
    for r0 in range(0, x_ref.shape[0], MERGE_SUB):
        rows = slice(r0, r0 + MERGE_SUB)
        x = x_ref[rows, :]
        h = _rms_norm(x, nw_ref[...]) * (1.0 + mod_ref[0, 1:2, :]) + mod_ref[0, 0:1, :]
        gates = jax.nn.sigmoid(_dot(h.astype(BF16), wbg_ref[...]) + bbg_ref[...])
        gated = []
        for hd in range(HEADS):
            cols = slice(hd * V_DIM, (hd + 1) * V_DIM)
            o = ro_ref[rows, cols].astype(F32)
            oc = o - jnp.mean(o, axis=-1, keepdims=True)
            var = jnp.mean(oc * oc, axis=-1, keepdims=True)
            gated.append((oc * lax.rsqrt(var + EPS) * sg_ref[rows, cols].astype(F32)).astype(BF16))
        ret_d = _dot(jnp.concatenate(gated, axis=1), wro_ref[...])
        four_d = _dot(z_ref[rows, :], wfo_ref[...])
        m = gates[:, :D_MODEL] * ret_d + gates[:, D_MODEL:] * four_d
        y = _dot(m.astype(BF16), wout_ref[...])
        x1 = x + mod_ref[0, 2:3, :] * y
        o_ref[rows, :] = x1
        h2 = _rms_norm(x1, n2w_ref[...]) * (1.0 + mod_ref[0, 4:5, :]) + mod_ref[0, 3:4, :]
        h2_ref[rows, :] = h2.astype(BF16)


def _merge(x2, mod3, norm_w, norm2_w, ro, sg, z, w_bg, b_bg, w_ro, w_fo, w_out, seq, to_cast):
    t, d = x2.shape
    tl = MERGE_TILE
    tpb = seq // tl
    tok = lambda w: pl.BlockSpec((tl, w), lambda i: (i, 0))
    cast_specs, cast_shapes = _cast_jobs(to_cast, t // tl)
    out = pl.pallas_call(
        functools.partial(_merge_kernel, n_cast=len(to_cast)),
        grid=(t // tl,),
        in_specs=[tok(d),
                  pl.BlockSpec((1, N_MOD, d), lambda i: (i // tpb, 0, 0)),
                  _const_spec((1, d)), _const_spec((1, d)),
                  tok(V_WIDTH), tok(V_WIDTH), tok(F_WIDTH),
                  _const_spec(w_bg.shape), _const_spec((1, 2 * d)),
                  _const_spec(w_ro.shape), _const_spec(w_fo.shape),
                  _const_spec(w_out.shape)] + cast_specs,
        out_specs=[tok(d), tok(d)] + cast_specs,
        out_shape=[jax.ShapeDtypeStruct((t, d), F32),
                   jax.ShapeDtypeStruct((t, d), BF16)] + cast_shapes,
        compiler_params=_params(("arbitrary",)),
        name="merge",
    )(x2, mod3, norm_w, norm2_w, ro, sg, z, w_bg, b_bg, w_ro, w_fo, w_out, *to_cast)
    return out[:2], out[2:]


HALO = 8
FFN_LEAD = 16
FFN_SUB = 256


def _two_gelu_tanh(x):
    c1 = np.sqrt(2.0 / np.pi)
    return x + x * jnp.tanh(x * (c1 + (c1 * 0.044715) * (x * x)))


def _ffn_kernel(xm_ref, hm_ref, xp_ref, xn_ref, mod_ref, nw_ref, wu_ref, cw_ref, cb_ref,
                wd_ref, fnw_ref, o_ref, h_scr, u_scr, act_scr, y_scr, *, tiles_per_seq):
    tl = xm_ref.shape[0]
    sub = FFN_SUB
    half = sub // 2
    nc = FFN_CHUNK
    i = pl.program_id(0)
    keep_prev = ((i % tiles_per_seq) != 0).astype(F32)
    keep_next = ((i % tiles_per_seq) != tiles_per_seq - 1).astype(F32)

    def pre(x):
        return _rms_norm(x, nw_ref[...]) * (1.0 + mod_ref[0, 4:5, :]) + mod_ref[0, 3:4, :]

    pad = jnp.zeros((FFN_LEAD - HALO, xm_ref.shape[1]), F32)
    h_scr[0:FFN_LEAD] = jnp.concatenate([pad, pre(xp_ref[...]) * keep_prev], axis=0).astype(BF16)
    h_scr[FFN_LEAD:FFN_LEAD + tl] = hm_ref[...]
    h_scr[FFN_LEAD + tl:] = jnp.concatenate([pre(xn_ref[...]) * keep_next, pad], axis=0).astype(BF16)

    def project(j):
        hb = h_scr[...]
        for part in range(2):
            lo = part * FFN_DIM + j * nc
            u = _dot(hb, wu_ref[:, lo:lo + nc])
            for s in range(nc // LANES):
                u_scr[j % 2, part, s] = u[:, s * LANES:(s + 1) * LANES]

    def conv(j, part, s, blk, scale):
        lo = part * FFN_DIM + j * nc + s * LANES
        w = cw_ref[:, lo:lo + LANES] * scale
        bias = cb_ref[:, lo:lo + LANES] * scale
        first = FFN_LEAD + blk * sub
        rows = lambda start: u_scr[j % 2, part, s, pl.ds(start, half, stride=2), :]
        before, even, odd, after = rows(first - 1), rows(first), rows(first + 1), rows(first + 2)
        return (before * w[0:1] + even * w[1:2] + odd * w[2:3] + bias,
                even * w[0:1] + odd * w[1:2] + after * w[2:3] + bias)

    def activate(j):
        for s in range(nc // LANES):
            cols = slice(j * nc + s * LANES, j * nc + (s + 1) * LANES)
            for blk in range(tl // sub):
                gate = conv(j, 0, s, blk, 1.0)
                val = conv(j, 1, s, blk, 0.5)
                for par in range(2):
                    r0 = blk * sub + par * half
                    act_scr[r0:r0 + half, cols] = (_two_gelu_tanh(gate[par]) * val[par]).astype(BF16)

    project(0)
    for j in range(N_FFN_CHUNKS):
        if j + 1 < N_FFN_CHUNKS:
            project(j + 1)
        activate(j)
    for blk in range(tl // sub):
        r0 = blk * sub
        y = _dot(act_scr[r0:r0 + sub, :], wd_ref[...])
        for s in range(y_scr.shape[0]):
            for par in range(2):
                y_scr[s, pl.ds(par, half, stride=2), :] = (
                    y[par * half:(par + 1) * half, s * LANES:(s + 1) * LANES])
        y = jnp.concatenate([y_scr[s] for s in range(y_scr.shape[0])], axis=1)
        x2 = xm_ref[r0:r0 + sub, :] + mod_ref[0, 5:6, :] * y
        o_ref[r0:r0 + sub, :] = _rms_norm(x2, fnw_ref[...])


def _ffn(x1, h2, mod3, norm_w, w_u, cw, cb, w_d, fnorm_w, seq):
    t, d = x1.shape
    tl = FFN_TILE
    tpb = seq // tl
    hb = tl // HALO
    last = t // HALO - 1
    rows = FFN_LEAD + tl + FFN_LEAD
    return pl.pallas_call(
        functools.partial(_ffn_kernel, tiles_per_seq=tpb),
        grid=(t // tl,),
        in_specs=[pl.BlockSpec((tl, d), lambda i: (i, 0)),
                  pl.BlockSpec((tl, d), lambda i: (i, 0)),
                  pl.BlockSpec((HALO, d), lambda i: (jnp.maximum(i * hb - 1, 0), 0)),
                  pl.BlockSpec((HALO, d), lambda i: (jnp.minimum((i + 1) * hb, last), 0)),
                  pl.BlockSpec((1, N_MOD, d), lambda i: (i // tpb, 0, 0)),
                  _const_spec((1, d)),
                  _const_spec(w_u.shape), _const_spec(cw.shape), _const_spec(cb.shape),
                  _const_spec(w_d.shape), _const_spec((1, d))],
        out_specs=pl.BlockSpec((tl, d), lambda i: (i, 0)),
        out_shape=jax.ShapeDtypeStruct((t, d), F32),
        scratch_shapes=[pltpu.VMEM((rows, d), BF16),
                        pltpu.VMEM((2, 2, FFN_CHUNK // LANES, rows, LANES), F32),
                        pltpu.VMEM((tl, FFN_DIM), BF16),
                        pltpu.VMEM((d // LANES, FFN_SUB, LANES), F32)],
        compiler_params=_params(("parallel",)),
        name="ffn",
    )(x1, h2, x1, x1, mod3, norm_w, w_u, cw, cb, w_d, fnorm_w)


def _rope_tables(seq):
    n_freq = QK_DIM // 4
    lane = jnp.arange(LANES)
    inv = ROPE_BASE ** (-(lane % n_freq).astype(F32) / n_freq)
    sign = jnp.where(lane < LANES // 2, -1.0, 1.0).astype(F32)

    def table(n):
        ang = jnp.arange(n, dtype=F32)[:, None] * inv[None, :]
        return jnp.stack([jnp.cos(ang), jnp.sin(ang) * sign])

    return table(seq // GRID_W), table(GRID_W)


def kernel(x, c, ctx, c_ctx, w_mod, b_mod, norm1_w, w_in, ret_decay_f, ret_decay_b,
           w_ret_out, w_four_out, w_branch_gate, b_branch_gate, w_out, norm2_w,
           w_up, conv_w, conv_b, w_down, final_norm_w):
    assert w_mod.shape[0] == 1, "single-layer block"
    b, seq, d = x.shape
    t = b * seq
    assert b == 2 and seq % RET_CHUNK == 0
    assert seq % FFN_TILE == 0 and seq % MERGE_TILE == 0 and seq % PROJ_TILE == 0

    c8 = jnp.concatenate([c, c_ctx[None, :], jnp.zeros((8 - b - 1, d), F32)], axis=0)
    mod, w_in_b, w_qk = _modulation(c8, w_mod[0], b_mod[0], w_in[0])
    mod3 = mod.reshape(8, N_MOD, d)
    n1w = norm1_w[0].reshape(1, d)

    kc, vc = _ctx_proj(ctx, mod3, n1w, w_qk, w_in_b)
    rtab, ctab = _rope_tables(seq)
    x2 = x.reshape(t, d)
    (q, k, v, sg, f), (w_bg, w_ro, w_fo, w_o) = _proj(
        x2, mod3, n1w, w_qk, w_in_b, rtab, ctab, seq,
        [w_branch_gate[0], w_ret_out[0], w_four_out[0], w_out[0]])

    a_f = jnp.broadcast_to(ret_decay_f[0][:, None, None], (HEADS, 1, LANES))
    a_b = jnp.broadcast_to(ret_decay_b[0][:, None, None], (HEADS, 1, LANES))
    y, tw, w2, w_c = _dft_first_stage(f.reshape(b, seq, F_WIDTH))
    ro, z = _retention_and_dft(a_f, a_b, q.reshape(b, seq, QK_WIDTH), k.reshape(b, seq, QK_WIDTH),
                               v.reshape(b, seq, V_WIDTH), kc, vc, y, tw, w2, w_c)

    n2w = norm2_w[0].reshape(1, d)
    (x1, h2), (w_u, w_d) = _merge(
        x2, mod3, n1w, n2w, ro.reshape(t, V_WIDTH), sg, z.reshape(t, F_WIDTH),
        w_bg, b_branch_gate[0].reshape(1, 2 * d), w_ro, w_fo, w_o, seq, [w_up[0], w_down[0]])

    out = _ffn(x1, h2, mod3, n2w, w_u, conv_w[0], conv_b[0].reshape(1, 2 * FFN_DIM), w_d,
               final_norm_w.reshape(1, d), seq)
    return out.reshape(b, seq, d)
```

```python
import functools

import numpy as np
import jax
import jax.numpy as jnp
from jax import lax
from jax.experimental import pallas as pl
from jax.experimental.pallas import tpu as pltpu

F32 = jnp.float32
BF16 = jnp.bfloat16

D_MODEL = 1024
GRID_W = 64
HEADS = 8
QK_DIM = 64
V_DIM = 128
QK_WIDTH = HEADS * QK_DIM
V_WIDTH = HEADS * V_DIM
ROPE_BASE = 10000.0
F_GROUPS = 4
F_GROUP_DIM = 128
F_WIDTH = F_GROUPS * F_GROUP_DIM
K_OFF = QK_WIDTH
V_OFF = K_OFF + QK_WIDTH
G_OFF = V_OFF + V_WIDTH
F_OFF = G_OFF + V_WIDTH
IN_COLS = F_OFF + F_WIDTH
FFN_DIM = 2816
N_MOD = 6
EPS = 1e-6

LANES = 128
RET_CHUNK = 256
RET_UNROLL = 16
FFN_CHUNK = 256
N_FFN_CHUNKS = FFN_DIM // FFN_CHUNK
FFN_TILE = 512
MERGE_TILE = 1024
MERGE_SUB = 512
PROJ_TILE = 1024
PROJ_SUB = 512
DFT_N1 = 64
BF16_ROWS = 16
DFT_ROWS = BF16_ROWS
DFT_A_ROWS = 2 * DFT_ROWS
VMEM_LIMIT = 56 * 1024 * 1024


def _params(sem):
    return pltpu.CompilerParams(dimension_semantics=sem, vmem_limit_bytes=VMEM_LIMIT)


def _dot(a, b):
    return jnp.dot(a, b, preferred_element_type=F32)


def _rms_norm(x, w):
    return x * lax.rsqrt(jnp.mean(x * x, axis=-1, keepdims=True) + EPS) * w


def _const_spec(shape):
    zeros = (0,) * len(shape)
    return pl.BlockSpec(shape, lambda *_: zeros, pipeline_mode=pl.Buffered(1))


def _cast_jobs(weights, steps):
    specs, shapes = [], []
    for w in weights:
        n_rows = w.shape[0]
        rows = -(-n_rows // steps)
        while rows % BF16_ROWS or n_rows % rows:
            rows += 1
        last = n_rows // rows - 1
        specs.append(pl.BlockSpec((rows, w.shape[1]), lambda i, last=last: (jnp.minimum(i, last), 0)))
        shapes.append(jax.ShapeDtypeStruct(w.shape, BF16))
    return specs, shapes


def _run_casts(refs):
    n = len(refs) // 2
    for src, dst in zip(refs[:n], refs[n:]):
        dst[...] = src[...].astype(BF16)


def _mod_kernel(c_ref, w_ref, b_ref, win_ref, o_ref, winb_ref, wqk_ref):
    c = c_ref[...]
    s = c * jax.nn.sigmoid(c)
    o_ref[...] = _dot(s.astype(BF16), w_ref[...].astype(BF16)) + b_ref[...]
    wb = win_ref[...].astype(BF16)
    winb_ref[...] = wb
    n_qk = 2 * QK_WIDTH
    src = lax.broadcasted_iota(jnp.int32, (n_qk, n_qk), 0)
    dst = lax.broadcasted_iota(jnp.int32, (n_qk, n_qk), 1)
    select = (src == _qk_source_column(dst)).astype(BF16)
    wqk_ref[...] = _dot(wb[:, :n_qk], select).astype(BF16)


def _qk_source_column(col):
    half = QK_DIM // 2
    lane = col % LANES
    head = 2 * ((col % QK_WIDTH) // LANES) + (lane // half) % 2
    return (col // QK_WIDTH) * QK_WIDTH + head * QK_DIM + (lane // (2 * half)) * half + lane % half


def _modulation(c8, w_mod, b_mod, w_in):
    n = w_mod.shape[1]
    steps = 4
    tn = n // steps
    d, n_in = w_in.shape
    rows = d // steps
    return pl.pallas_call(
        _mod_kernel,
        grid=(steps,),
        in_specs=[_const_spec((8, D_MODEL)),
                  pl.BlockSpec((D_MODEL, tn), lambda j: (0, j)),
                  pl.BlockSpec((1, tn), lambda j: (0, j)),
                  pl.BlockSpec((rows, n_in), lambda j: (j, 0))],
        out_specs=[pl.BlockSpec((8, tn), lambda j: (0, j)),
                   pl.BlockSpec((rows, n_in), lambda j: (j, 0)),
                   pl.BlockSpec((rows, 2 * QK_WIDTH), lambda j: (j, 0))],
        out_shape=[jax.ShapeDtypeStruct((8, n), F32),
                   jax.ShapeDtypeStruct((d, n_in), BF16),
                   jax.ShapeDtypeStruct((d, 2 * QK_WIDTH), BF16)],
        compiler_params=_params(("parallel",)),
        name="mod",
    )(c8, w_mod, b_mod.reshape(1, n), w_in)


def _ctx_kernel(x_ref, mod_ref, nw_ref, wk_ref, wv_ref, k_ref, v_ref):
    x = x_ref[0]
    h = _rms_norm(x, nw_ref[...]) * (1.0 + mod_ref[0, 1:2, :]) + mod_ref[0, 0:1, :]
    hb = h.astype(BF16)
    k_ref[0] = _dot(hb, wk_ref[...]).astype(BF16)
    v_ref[0] = _dot(hb, wv_ref[...]).astype(BF16)


def _ctx_proj(ctx, mod3, norm_w, w_qk, w_in):
    b, lc, d = ctx.shape
    assert V_OFF % V_WIDTH == 0
    return pl.pallas_call(
        _ctx_kernel,
        grid=(b,),
        in_specs=[pl.BlockSpec((1, lc, d), lambda i: (i, 0, 0)),
                  pl.BlockSpec((1, N_MOD, d), lambda i: (2, 0, 0)),
                  _const_spec((1, d)),
                  pl.BlockSpec((d, QK_WIDTH), lambda i: (0, 1), pipeline_mode=pl.Buffered(1)),
                  pl.BlockSpec((d, V_WIDTH), lambda i: (0, V_OFF // V_WIDTH),
                               pipeline_mode=pl.Buffered(1))],
        out_specs=[pl.BlockSpec((1, lc, QK_WIDTH), lambda i: (i, 0, 0)),
                   pl.BlockSpec((1, lc, V_WIDTH), lambda i: (i, 0, 0))],
        out_shape=[jax.ShapeDtypeStruct((b, lc, QK_WIDTH), BF16),
                   jax.ShapeDtypeStruct((b, lc, V_WIDTH), BF16)],
        compiler_params=_params(("parallel",)),
        name="ctx_proj",
    )(ctx, mod3, norm_w, w_qk, w_in)


def _proj_kernel(x_ref, mod_ref, nw_ref, wqk_ref, w_ref, rtab_ref, ctab_ref, *refs, n_cast):
    q_ref, k_ref, v_ref, sg_ref, f_ref = refs[n_cast:n_cast + 5]
    _run_casts(refs[:n_cast] + refs[n_cast + 5:])
    lane = lax.broadcasted_iota(jnp.int32, (1, LANES), 1)
    by_row = (lane % (QK_DIM // 2)) < QK_DIM // 4

    def rope(t, trig, scale, out_ref, rows):
        for j in range(QK_WIDTH // LANES):
            tj = t[:, j * LANES:(j + 1) * LANES]
            r = tj * trig[0] + pltpu.roll(tj, LANES // 2, 1) * trig[1]
            out_ref[rows, j * LANES:(j + 1) * LANES] = (r * scale).astype(BF16)

    for r0 in range(0, x_ref.shape[0], PROJ_SUB):
        rows = slice(r0, r0 + PROJ_SUB)
        trig = []
        for cs in range(2):
            trig.append(jnp.concatenate(
                [jnp.where(by_row, rtab_ref[cs, g:g + 1, :], ctab_ref[cs])
                 for g in range(r0 // GRID_W, (r0 + PROJ_SUB) // GRID_W)], axis=0))
        x = x_ref[rows, :]
        h = _rms_norm(x, nw_ref[...]) * (1.0 + mod_ref[0, 1:2, :]) + mod_ref[0, 0:1, :]
        hb = h.astype(BF16)
        rope(_dot(hb, wqk_ref[:, :QK_WIDTH]), trig, QK_DIM ** -0.5, q_ref, rows)
        rope(_dot(hb, wqk_ref[:, QK_WIDTH:]), trig, 1.0, k_ref, rows)
        v_ref[rows, :] = _dot(hb, w_ref[:, V_OFF:G_OFF]).astype(BF16)
        g = _dot(hb, w_ref[:, G_OFF:F_OFF])
        sg_ref[rows, :] = (g * jax.nn.sigmoid(g)).astype(BF16)
        f_ref[rows, :] = _dot(hb, w_ref[:, F_OFF:IN_COLS]).astype(BF16)


def _proj(x2, mod3, norm_w, w_qk, w_in, rtab, ctab, seq, to_cast):
    t, d = x2.shape
    tl = PROJ_TILE
    tpb = seq // tl
    tok = lambda w: pl.BlockSpec((tl, w), lambda i: (i, 0))
    cast_specs, cast_shapes = _cast_jobs(to_cast, t // tl)
    out = pl.pallas_call(
        functools.partial(_proj_kernel, n_cast=len(to_cast)),
        grid=(t // tl,),
        in_specs=[tok(d),
                  pl.BlockSpec((1, N_MOD, d), lambda i: (i // tpb, 0, 0)),
                  _const_spec((1, d)),
                  _const_spec(w_qk.shape), _const_spec(w_in.shape),
                  pl.BlockSpec((2, tl // GRID_W, LANES), lambda i: (0, i % tpb, 0)),
                  _const_spec(ctab.shape)] + cast_specs,
        out_specs=[tok(QK_WIDTH), tok(QK_WIDTH), tok(V_WIDTH), tok(V_WIDTH),
                   tok(F_WIDTH)] + cast_specs,
        out_shape=[jax.ShapeDtypeStruct((t, QK_WIDTH), BF16),
                   jax.ShapeDtypeStruct((t, QK_WIDTH), BF16),
                   jax.ShapeDtypeStruct((t, V_WIDTH), BF16),
                   jax.ShapeDtypeStruct((t, V_WIDTH), BF16),
                   jax.ShapeDtypeStruct((t, F_WIDTH), BF16)] + cast_shapes,
        compiler_params=_params(("arbitrary",)),
        name="proj",
    )(x2, mod3, norm_w, w_qk, w_in, rtab, ctab, *to_cast)
    return out[:5], out[5:]


def _retention_kernel(af_ref, ab_ref, q_ref, k_ref, v_ref, kc_ref, vc_ref,
                      tw_ref, w2_ref, cc_ref, y_ref,
                      o_ref, z_ref, ds_scr, st_scr, d_scr, tab_scr, z_scr):
    c = RET_CHUNK
    seq = q_ref.shape[1]
    lc = kc_ref.shape[1]
    n = seq // c
    lgf = [-jnp.exp(af_ref[hh]) for hh in range(2)]
    lgb = [-jnp.exp(ab_ref[hh]) for hh in range(2)]
    half = QK_DIM // 2

    lane_head = (lax.broadcasted_iota(jnp.int32, (1, LANES), 1) // half) % 2
    masks = [(lane_head == hh).astype(BF16) for hh in range(2)]
    lgf_lane = jnp.where(lane_head == 0, lgf[0], lgf[1])
    lgb_lane = jnp.where(lane_head == 0, lgb[0], lgb[1])
    row_head = (lax.broadcasted_iota(jnp.int32, (LANES, 1), 0) // half) % 2
    lgf_row = jnp.where(row_head == 0, lgf[0][:, 0:1], lgf[1][:, 0:1])
    lgb_row = jnp.where(row_head == 0, lgb[0][:, 0:1], lgb[1][:, 0:1])

    pos = lax.broadcasted_iota(jnp.int32, (c, LANES), 0).astype(F32)
    tab_scr[0] = jnp.exp(lgf_lane * (pos + 1.0))
    tab_scr[1] = jnp.exp(lgb_lane * (c - pos))
    diff = (lax.broadcasted_iota(jnp.int32, (c, c), 0)
            - lax.broadcasted_iota(jnp.int32, (c, c), 1)).astype(F32)
    for hh in range(2):
        d_scr[hh] = (jnp.where(diff >= 0, jnp.exp(lgf[hh][:, 0:1] * jnp.maximum(diff, 0.0)), 0.0)
                     + jnp.where(diff <= 0, jnp.exp(lgb[hh][:, 0:1] * jnp.maximum(-diff, 0.0)), 0.0))

    def k_decays(tokens):
        t = lax.broadcasted_iota(jnp.int32, (1, tokens), 1).astype(F32)
        return jnp.exp(lgf_row * (tokens - 1.0 - t)), jnp.exp(lgb_row * t)

    def state_increment(k_rows, v_rows, decays):
        kt = jnp.transpose(k_rows.astype(F32))
        lhs = jnp.concatenate([kt * decays[0], kt * decays[1]], axis=0).astype(BF16)
        return _dot(lhs, v_rows)

    kdec = k_decays(c)

    def incr(i, carry):
        rows = pl.ds(pl.multiple_of(i * c, c), c)
        ds_scr[i] = state_increment(k_ref[0, rows, :], v_ref[0, rows, :], kdec)
        return carry

    lax.fori_loop(0, n, incr, 0, unroll=RET_UNROLL)

    s0 = state_increment(kc_ref[0], vc_ref[0], k_decays(lc))
    col_head = lax.broadcasted_iota(jnp.int32, (1, 2 * V_DIM), 1) // V_DIM
    own = (row_head == col_head).astype(F32)
    decay_f = jnp.exp(lgf_row * c) * own
    decay_b = jnp.exp(lgb_row * c) * own

    def scan_f(i, s):
        st_scr[i, 0:LANES, :] = (s * own).astype(BF16)
        return decay_f * s + ds_scr[i, 0:LANES, :]

    lax.fori_loop(0, n, scan_f, s0[0:LANES])

    def scan_b(t, s):
        i = n - 1 - t
        st_scr[i, LANES:2 * LANES, :] = (s * own).astype(BF16)
        return decay_b * s + ds_scr[i, LANES:2 * LANES, :]

    lax.fori_loop(0, n, scan_b, s0[LANES:2 * LANES])

    def outputs(i):
        rows = pl.ds(pl.multiple_of(i * c, c), c)
        q = q_ref[0, rows, :]
        k = k_ref[0, rows, :]
        qf = q.astype(F32)
        qd = jnp.concatenate([qf * tab_scr[0], qf * tab_scr[1]], axis=1).astype(BF16)
        inter = _dot(qd, st_scr[i])
        for hh in range(2):
            cols = slice(hh * V_DIM, (hh + 1) * V_DIM)
            scores = lax.dot_general(q * masks[hh], k, (((1,), (1,)), ((), ())),
                                     preferred_element_type=F32)
            o = _dot((scores * d_scr[hh]).astype(BF16), v_ref[0, rows, cols]) + inter[:, cols]
            o_ref[0, rows, cols] = o.astype(BF16)

    per_k1 = n // DFT_ROWS
    cc = cc_ref[...].astype(BF16)

    def outputs_and_dft(j, carry):
        for u in range(per_k1):
            outputs(j * per_k1 + u)
        _dft_second_stage(j, tw_ref, w2_ref, cc, y_ref, z_scr)
        return carry

    lax.fori_loop(0, DFT_ROWS, outputs_and_dft, 0, unroll=RET_UNROLL // per_k1)
    _dft_emit(z_scr, z_ref)


def _retention_and_dft(a_f, a_b, q, k, v, kc, vc, y, tw, w2, w_c):
    b, seq, _ = v.shape
    lc = kc.shape[1]
    c = RET_CHUNK
    n = seq // c
    r = DFT_ROWS
    _, _, n1, n2, w = y.shape
    assert n % r == 0 and n1 // r == HEADS // 2
    dec = pl.BlockSpec((2, 1, LANES), lambda i, p: (p, 0, 0))
    qk = lambda rows: pl.BlockSpec((1, rows, LANES), lambda i, p: (i, 0, p))
    vv = lambda rows: pl.BlockSpec((1, rows, 2 * V_DIM), lambda i, p: (i, 0, p))
    return pl.pallas_call(
        _retention_kernel,
        grid=(b, HEADS // 2),
        in_specs=[dec, dec, qk(seq), qk(seq), vv(seq), qk(lc), vv(lc),
                  pl.BlockSpec((r, 2, n2), lambda i, p: (p, 0, 0)),
                  _const_spec((2, n2, n2)),
                  _const_spec((2 * F_GROUP_DIM, F_GROUP_DIM)),
                  pl.BlockSpec((1, 2, r, n2, w), lambda i, p: (i, 0, p, 0, 0))],
        out_specs=[vv(seq), pl.BlockSpec((1, n2, r, w), lambda i, p: (i, 0, p, 0))],
        out_shape=[jax.ShapeDtypeStruct((b, seq, V_WIDTH), BF16),
                   jax.ShapeDtypeStruct((b, n2, n1, w), BF16)],
        scratch_shapes=[pltpu.VMEM((n, 2 * LANES, 2 * V_DIM), F32),
                        pltpu.VMEM((n, 2 * LANES, 2 * V_DIM), BF16),
                        pltpu.VMEM((2, c, c), F32),
                        pltpu.VMEM((2, c, LANES), F32),
                        pltpu.VMEM((F_GROUPS, n2 * r, F_GROUP_DIM), F32)],
        compiler_params=_params(("parallel", "parallel")),
        name="retention_dft",
    )(a_f, a_b, q, k, v, kc, vc, tw, w2, w_c, y)


def _dft_a_kernel(w_ref, x_ref, y_ref):
    _, n1, rows, w = x_ref.shape
    r = DFT_ROWS
    nh = n1 // 2 + 1
    for r0 in range(0, rows, r):
        x = x_ref[0, :, r0:r0 + r, :].reshape(n1 * r, w)
        y = _dot(w_ref[...], x).reshape(2, nh, r, w).astype(BF16)
        y_ref[0, :, 0:nh, r0:r0 + r, :] = y
        for k1 in range(1, n1 // 2):
            y_ref[0, 0, n1 - k1, r0:r0 + r, :] = y[0, k1]
            y_ref[0, 1, n1 - k1, r0:r0 + r, :] = -y[1, k1]


def _dft_second_stage(j, tw_ref, w2_ref, cc, y_ref, z_scr):
    n2 = y_ref.shape[3]
    w2c = w2_ref[0]
    w2s = w2_ref[1]
    tc = tw_ref[j, 0:1, :]
    ts = tw_ref[j, 1:2, :]
    ec = w2c * tc - w2s * ts
    es = w2s * tc + w2c * ts
    m = jnp.concatenate([jnp.concatenate([ec, -es], axis=1),
                         jnp.concatenate([es, ec], axis=1)], axis=0).astype(BF16)
    y = jnp.concatenate([y_ref[0, 0, j], y_ref[0, 1, j]], axis=0)
    zz = _dot(m, y)
    for g in range(F_GROUPS):
        cols = slice(g * F_GROUP_DIM, (g + 1) * F_GROUP_DIM)
        zcs = jnp.concatenate([zz[:n2, cols], zz[n2:, cols]], axis=1).astype(BF16)
        z_scr[g, pl.ds(j, n2, stride=DFT_ROWS), :] = _dot(zcs, cc)


def _dft_emit(z_scr, z_ref):
    n2 = z_ref.shape[1]
    z = jnp.concatenate([z_scr[g] for g in range(F_GROUPS)], axis=1)
    z_ref[0] = z.reshape(n2, DFT_ROWS, F_WIDTH).astype(BF16)


def _dft_tables(seq):
    n1 = DFT_N1
    n2 = seq // n1

    def cs(num, den):
        ang = 2.0 * np.pi * (num % den) / den
        return np.cos(ang), np.sin(ang)

    a = np.arange(n1)
    w_a = np.concatenate(cs(a[:n1 // 2 + 1, None] * a[None, :], n1), axis=0)
    m = np.arange(n2)
    tw = np.stack(cs(a[:, None] * m[None, :], seq), axis=1)
    w2 = np.stack(cs(m[:, None] * m[None, :], n2), axis=0)
    ch = np.arange(F_GROUP_DIM)
    cc, sc = cs(ch[:, None] * ch[None, :], F_GROUP_DIM)
    scale = 1.0 / np.sqrt(seq * F_GROUP_DIM)
    w_c = np.concatenate([cc, -sc], axis=0) * scale
    return [jnp.asarray(t, dtype=F32) for t in (w_a, tw, w2, w_c)]


def _dft_first_stage(f):
    b, seq, w = f.shape
    n1 = DFT_N1
    n2 = seq // n1
    w_a, tw, w2, w_c = _dft_tables(seq)
    r = DFT_ROWS
    spread = (jnp.arange(n1 * r)[None, :] // r == jnp.arange(n1)[:, None]).astype(F32)
    w_rep = jnp.dot(w_a, spread, precision=lax.Precision.HIGHEST)
    same_r = jnp.arange(w_a.shape[0] * r)[:, None] % r == jnp.arange(n1 * r)[None, :] % r
    w_a = jnp.where(same_r, jnp.repeat(w_rep, r, axis=0), 0.0).astype(BF16)
    y = pl.pallas_call(
        _dft_a_kernel,
        grid=(b, n2 // DFT_A_ROWS),
        in_specs=[_const_spec(w_a.shape),
                  pl.BlockSpec((1, n1, DFT_A_ROWS, w), lambda i, j: (i, 0, j, 0))],
        out_specs=pl.BlockSpec((1, 2, n1, DFT_A_ROWS, w), lambda i, j: (i, 0, 0, j, 0)),
        out_shape=jax.ShapeDtypeStruct((b, 2, n1, n2, w), BF16),
        compiler_params=_params(("parallel", "parallel")),
        name="dft_a",
    )(w_a, f.reshape(b, n1, n2, w))
    return y, tw, w2, w_c


def _merge_kernel(x_ref, mod_ref, nw_ref, n2w_ref, ro_ref, sg_ref, z_ref, wbg_ref, bbg_ref,
                  wro_ref, wfo_ref, wout_ref, *refs, n_cast):
    o_ref, h2_ref = refs[n_cast:n_cast + 2]
    _run_casts(refs[:n_cast] + refs[n_cast + 2:])
    for r0 in range(0, x_ref.shape[0], MERGE_SUB):
        rows = slice(r0, r0 + MERGE_SUB)
        x = x_ref[rows, :]
        h = _rms_norm(x, nw_ref[...]) * (1.0 + mod_ref[0, 1:2, :]) + mod_ref[0, 0:1, :]
        gates = jax.nn.sigmoid(_dot(h.astype(BF16), wbg_ref[...]) + bbg_ref[...])
        gated = []
        for hd in range(HEADS):
            cols = slice(hd * V_DIM, (hd + 1) * V_DIM)
            o = ro_ref[rows, cols].astype(F32)
            oc = o - jnp.mean(o, axis=-1, keepdims=True)
            var = jnp.mean(oc * oc, axis=-1, keepdims=True)
            gated.append((oc * lax.rsqrt(var + EPS) * sg_ref[rows, cols].astype(F32)).astype(BF16))
        ret_d = _dot(jnp.concatenate(gated, axis=1), wro_ref[...])
        four_d = _dot(z_ref[rows, :], wfo_ref[...])
        m = gates[:, :D_MODEL] * ret_d + gates[:, D_MODEL:] * four_d
        y = _dot(m.astype(BF16), wout_ref[...])
        x1 = x + mod_ref[0, 2:3, :] * y
        o_ref[rows, :] = x1
        h2 = _rms_norm(x1, n2w_ref[...]) * (1.0 + mod_ref[0, 4:5, :]) + mod_ref[0, 3:4, :]
        h2_ref[rows, :] = h2.astype(BF16)


def _merge(x2, mod3, norm_w, norm2_w, ro, sg, z, w_bg, b_bg, w_ro, w_fo, w_out, seq, to_cast):
    t, d = x2.shape
    tl = MERGE_TILE
    tpb = seq // tl
    tok = lambda w: pl.BlockSpec((tl, w), lambda i: (i, 0))
    cast_specs, cast_shapes = _cast_jobs(to_cast, t // tl)
    out = pl.pallas_call(
        functools.partial(_merge_kernel, n_cast=len(to_cast)),
        grid=(t // tl,),
        in_specs=[tok(d),
                  pl.BlockSpec((1, N_MOD, d), lambda i: (i // tpb, 0, 0)),
                  _const_spec((1, d)), _const_spec((1, d)),
                  tok(V_WIDTH), tok(V_WIDTH), tok(F_WIDTH),
                  _const_spec(w_bg.shape), _const_spec((1, 2 * d)),
                  _const_spec(w_ro.shape), _const_spec(w_fo.shape),
                  _const_spec(w_out.shape)] + cast_specs,
        out_specs=[tok(d), tok(d)] + cast_specs,
        out_shape=[jax.ShapeDtypeStruct((t, d), F32),
                   jax.ShapeDtypeStruct((t, d), BF16)] + cast_shapes,
        compiler_params=_params(("arbitrary",)),
        name="merge",
    )(x2, mod3, norm_w, norm2_w, ro, sg, z, w_bg, b_bg, w_ro, w_fo, w_out, *to_cast)
    return out[:2], out[2:]


HALO = 8
FFN_LEAD = 16
FFN_SUB = 256


def _two_gelu_tanh(x):
    c1 = np.sqrt(2.0 / np.pi)
    return x + x * jnp.tanh(x * (c1 + (c1 * 0.044715) * (x * x)))


def _ffn_kernel(xm_ref, hm_ref, xn_ref, mod_ref, nw_ref, wu_ref, cw_ref, cb_ref,
                wd_ref, fnw_ref, o_ref, h_scr, u_scr, act_scr, y_scr, last_scr, *, tiles_per_seq):
    tl = xm_ref.shape[0]
    sub = FFN_SUB
    half = sub // 2
    nc = FFN_CHUNK
    i = pl.program_id(0)
    seq_start = (i % tiles_per_seq) == 0
    keep_next = ((i % tiles_per_seq) != tiles_per_seq - 1).astype(F32)

    @pl.when(i == 0)
    def _():
        last_scr[...] = jnp.zeros_like(last_scr)

    def pre(x):
        return _rms_norm(x, nw_ref[...]) * (1.0 + mod_ref[0, 4:5, :]) + mod_ref[0, 3:4, :]

    pad = jnp.zeros((FFN_LEAD - HALO, xm_ref.shape[1]), F32)
    h_scr[0:tl] = hm_ref[...]
    h_scr[tl:] = jnp.concatenate([pre(xn_ref[...]) * keep_next, pad], axis=0).astype(BF16)

    def project(j):
        hb = h_scr[...]
        for part in range(2):
            lo = part * FFN_DIM + j * nc
            u = _dot(hb, wu_ref[:, lo:lo + nc])
            for s in range(nc // LANES):
                slab = u[:, s * LANES:(s + 1) * LANES]
                u_scr[j % 2, part, s, HALO:] = slab
                u_scr[j % 2, part, s, HALO - 1:HALO] = jnp.where(seq_start, 0.0,
                                                                  last_scr[j, part, s, 0:1])
                last_scr[j, part, s, 0:1] = slab[tl - 1:tl]

    def conv(j, part, s, blk, scale):
        lo = part * FFN_DIM + j * nc + s * LANES
        w = cw_ref[:, lo:lo + LANES] * scale
        bias = cb_ref[:, lo:lo + LANES] * scale
        first = HALO + blk * sub
        rows = lambda start: u_scr[j % 2, part, s, pl.ds(start, half, stride=2), :]
        before, even, odd, after = rows(first - 1), rows(first), rows(first + 1), rows(first + 2)
        return (before * w[0:1] + even * w[1:2] + odd * w[2:3] + bias,
                even * w[0:1] + odd * w[1:2] + after * w[2:3] + bias)

    def activate(j):
        for s in range(nc // LANES):
            cols = slice(j * nc + s * LANES, j * nc + (s + 1) * LANES)
            for blk in range(tl // sub):
                gate = conv(j, 0, s, blk, 1.0)
                val = conv(j, 1, s, blk, 0.5)
                for par in range(2):
                    r0 = blk * sub + par * half
                    act_scr[r0:r0 + half, cols] = (_two_gelu_tanh(gate[par]) * val[par]).astype(BF16)

    project(0)
    for j in range(N_FFN_CHUNKS):
        if j + 1 < N_FFN_CHUNKS:
            project(j + 1)
        activate(j)
    for blk in range(tl // sub):
        r0 = blk * sub
        y = _dot(act_scr[r0:r0 + sub, :], wd_ref[...])
        for s in range(y_scr.shape[0]):
            for par in range(2):
                y_scr[s, pl.ds(par, half, stride=2), :] = (
                    y[par * half:(par + 1) * half, s * LANES:(s + 1) * LANES])
        y = jnp.concatenate([y_scr[s] for s in range(y_scr.shape[0])], axis=1)
        x2 = xm_ref[r0:r0 + sub, :] + mod_ref[0, 5:6, :] * y
        o_ref[r0:r0 + sub, :] = _rms_norm(x2, fnw_ref[...])


def _ffn(x1, h2, mod3, norm_w, w_u, cw, cb, w_d, fnorm_w, seq):
    t, d = x1.shape
    tl = FFN_TILE
    tpb = seq // tl
    hb = tl // HALO
    last = t // HALO - 1
    rows = tl + FFN_LEAD
    slabs = FFN_CHUNK // LANES
    return pl.pallas_call(
        functools.partial(_ffn_kernel, tiles_per_seq=tpb),
        grid=(t // tl,),
        in_specs=[pl.BlockSpec((tl, d), lambda i: (i, 0)),
                  pl.BlockSpec((tl, d), lambda i: (i, 0)),
                  pl.BlockSpec((HALO, d), lambda i: (jnp.minimum((i + 1) * hb, last), 0)),
                  pl.BlockSpec((1, N_MOD, d), lambda i: (i // tpb, 0, 0)),
                  _const_spec((1, d)),
                  _const_spec(w_u.shape), _const_spec(cw.shape), _const_spec(cb.shape),
                  _const_spec(w_d.shape), _const_spec((1, d))],
        out_specs=pl.BlockSpec((tl, d), lambda i: (i, 0)),
        out_shape=jax.ShapeDtypeStruct((t, d), F32),
        scratch_shapes=[pltpu.VMEM((rows, d), BF16),
                        pltpu.VMEM((2, 2, slabs, HALO + rows, LANES), F32),
                        pltpu.VMEM((tl, FFN_DIM), BF16),
                        pltpu.VMEM((d // LANES, FFN_SUB, LANES), F32),
                        pltpu.VMEM((N_FFN_CHUNKS, 2, slabs, HALO, LANES), F32)],
        compiler_params=_params(("arbitrary",)),
        name="ffn",
    )(x1, h2, x1, mod3, norm_w, w_u, cw, cb, w_d, fnorm_w)


def _rope_tables(seq):
    n_freq = QK_DIM // 4
    lane = jnp.arange(LANES)
    inv = ROPE_BASE ** (-(lane % n_freq).astype(F32) / n_freq)
    sign = jnp.where(lane < LANES // 2, -1.0, 1.0).astype(F32)

    def table(n):
        ang = jnp.arange(n, dtype=F32)[:, None] * inv[None, :]
        return jnp.stack([jnp.cos(ang), jnp.sin(ang) * sign])

    return table(seq // GRID_W), table(GRID_W)


def kernel(x, c, ctx, c_ctx, w_mod, b_mod, norm1_w, w_in, ret_decay_f, ret_decay_b,
           w_ret_out, w_four_out, w_branch_gate, b_branch_gate, w_out, norm2_w,
           w_up, conv_w, conv_b, w_down, final_norm_w):
    assert w_mod.shape[0] == 1, "single-layer block"
    b, seq, d = x.shape
    t = b * seq
    assert b == 2 and seq % RET_CHUNK == 0
    assert seq % FFN_TILE == 0 and seq % MERGE_TILE == 0 and seq % PROJ_TILE == 0

    c8 = jnp.concatenate([c, c_ctx[None, :], jnp.zeros((8 - b - 1, d), F32)], axis=0)
    mod, w_in_b, w_qk = _modulation(c8, w_mod[0], b_mod[0], w_in[0])
    mod3 = mod.reshape(8, N_MOD, d)
    n1w = norm1_w[0].reshape(1, d)

    kc, vc = _ctx_proj(ctx, mod3, n1w, w_qk, w_in_b)
    rtab, ctab = _rope_tables(seq)
    x2 = x.reshape(t, d)
    (q, k, v, sg, f), (w_bg, w_ro, w_fo, w_o) = _proj(
        x2, mod3, n1w, w_qk, w_in_b, rtab, ctab, seq,
        [w_branch_gate[0], w_ret_out[0], w_four_out[0], w_out[0]])

    a_f = jnp.broadcast_to(ret_decay_f[0][:, None, None], (HEADS, 1, LANES))
    a_b = jnp.broadcast_to(ret_decay_b[0][:, None, None], (HEADS, 1, LANES))
    y, tw, w2, w_c = _dft_first_stage(f.reshape(b, seq, F_WIDTH))
    ro, z = _retention_and_dft(a_f, a_b, q.reshape(b, seq, QK_WIDTH), k.reshape(b, seq, QK_WIDTH),
                               v.reshape(b, seq, V_WIDTH), kc, vc, y, tw, w2, w_c)

    n2w = norm2_w[0].reshape(1, d)
    (x1, h2), (w_u, w_d) = _merge(
        x2, mod3, n1w, n2w, ro.reshape(t, V_WIDTH), sg, z.reshape(t, F_WIDTH),
        w_bg, b_branch_gate[0].reshape(1, 2 * d), w_ro, w_fo, w_o, seq, [w_up[0], w_down[0]])

    out = _ffn(x1, h2, mod3, n2w, w_u, conv_w[0], conv_b[0].reshape(1, 2 * FFN_DIM), w_d,
               final_norm_w.reshape(1, d), seq)
    return out.reshape(b, seq, d)
```

```python
import functools

import numpy as np
import jax
import jax.numpy as jnp
from jax import lax
from jax.experimental import pallas as pl
from jax.experimental.pallas import tpu as pltpu

F32 = jnp.float32
BF16 = jnp.bfloat16

D_MODEL = 1024
GRID_W = 64
HEADS = 8
QK_DIM = 64
V_DIM = 128
QK_WIDTH = HEADS * QK_DIM
V_WIDTH = HEADS * V_DIM
ROPE_BASE = 10000.0
F_GROUPS = 4
F_GROUP_DIM = 128
F_WIDTH = F_GROUPS * F_GROUP_DIM
K_OFF = QK_WIDTH
V_OFF = K_OFF + QK_WIDTH
G_OFF = V_OFF + V_WIDTH
F_OFF = G_OFF + V_WIDTH
IN_COLS = F_OFF + F_WIDTH
FFN_DIM = 2816
N_MOD = 6
EPS = 1e-6

LANES = 128
RET_CHUNK = 256
RET_UNROLL = 16
FFN_CHUNK = 256
N_FFN_CHUNKS = FFN_DIM // FFN_CHUNK
FFN_TILE = 1024
MERGE_TILE = 1024
MERGE_SUB = 512
PROJ_TILE = 1024
PROJ_SUB = 512
DFT_N1 = 64
BF16_ROWS = 16
DFT_ROWS = BF16_ROWS
DFT_A_ROWS = 2 * DFT_ROWS
VMEM_LIMIT = 56 * 1024 * 1024


def _params(sem):
    return pltpu.CompilerParams(dimension_semantics=sem, vmem_limit_bytes=VMEM_LIMIT)


def _dot(a, b):
    return jnp.dot(a, b, preferred_element_type=F32)


def _rms_norm(x, w):
    return x * lax.rsqrt(jnp.mean(x * x, axis=-1, keepdims=True) + EPS) * w


def _const_spec(shape):
    zeros = (0,) * len(shape)
    return pl.BlockSpec(shape, lambda *_: zeros, pipeline_mode=pl.Buffered(1))


def _cast_jobs(weights, steps):
    specs, shapes = [], []
    for w in weights:
        n_rows = w.shape[0]
        rows = -(-n_rows // steps)
        while rows % BF16_ROWS or n_rows % rows:
            rows += 1
        last = n_rows // rows - 1
        specs.append(pl.BlockSpec((rows, w.shape[1]), lambda i, last=last: (jnp.minimum(i, last), 0)))
        shapes.append(jax.ShapeDtypeStruct(w.shape, BF16))
    return specs, shapes


def _run_casts(refs):
    n = len(refs) // 2
    for src, dst in zip(refs[:n], refs[n:]):
        dst[...] = src[...].astype(BF16)


def _mod_kernel(c_ref, w_ref, b_ref, win_ref, o_ref, winb_ref, wqk_ref):
    c = c_ref[...]
    s = c * jax.nn.sigmoid(c)
    o_ref[...] = _dot(s.astype(BF16), w_ref[...].astype(BF16)) + b_ref[...]
    wb = win_ref[...].astype(BF16)
    winb_ref[...] = wb
    n_qk = 2 * QK_WIDTH
    src = lax.broadcasted_iota(jnp.int32, (n_qk, n_qk), 0)
    dst = lax.broadcasted_iota(jnp.int32, (n_qk, n_qk), 1)
    select = (src == _qk_source_column(dst)).astype(BF16)
    wqk_ref[...] = _dot(wb[:, :n_qk], select).astype(BF16)


def _qk_source_column(col):
    half = QK_DIM // 2
    lane = col % LANES
    head = 2 * ((col % QK_WIDTH) // LANES) + (lane // half) % 2
    return (col // QK_WIDTH) * QK_WIDTH + head * QK_DIM + (lane // (2 * half)) * half + lane % half


def _modulation(c8, w_mod, b_mod, w_in):
    n = w_mod.shape[1]
    steps = 4
    tn = n // steps
    d, n_in = w_in.shape
    rows = d // steps
    return pl.pallas_call(
        _mod_kernel,
        grid=(steps,),
        in_specs=[_const_spec((8, D_MODEL)),
                  pl.BlockSpec((D_MODEL, tn), lambda j: (0, j)),
                  pl.BlockSpec((1, tn), lambda j: (0, j)),
                  pl.BlockSpec((rows, n_in), lambda j: (j, 0))],
        out_specs=[pl.BlockSpec((8, tn), lambda j: (0, j)),
                   pl.BlockSpec((rows, n_in), lambda j: (j, 0)),
                   pl.BlockSpec((rows, 2 * QK_WIDTH), lambda j: (j, 0))],
        out_shape=[jax.ShapeDtypeStruct((8, n), F32),
                   jax.ShapeDtypeStruct((d, n_in), BF16),
                   jax.ShapeDtypeStruct((d, 2 * QK_WIDTH), BF16)],
        compiler_params=_params(("parallel",)),
        name="mod",
    )(c8, w_mod, b_mod.reshape(1, n), w_in)


def _ctx_kernel(x_ref, mod_ref, nw_ref, wk_ref, wv_ref, k_ref, v_ref):
    x = x_ref[0]
    h = _rms_norm(x, nw_ref[...]) * (1.0 + mod_ref[0, 1:2, :]) + mod_ref[0, 0:1, :]
    hb = h.astype(BF16)
    k_ref[0] = _dot(hb, wk_ref[...]).astype(BF16)
    v_ref[0] = _dot(hb, wv_ref[...]).astype(BF16)


def _ctx_proj(ctx, mod3, norm_w, w_qk, w_in):
    b, lc, d = ctx.shape
    assert V_OFF % V_WIDTH == 0
    return pl.pallas_call(
        _ctx_kernel,
        grid=(b,),
        in_specs=[pl.BlockSpec((1, lc, d), lambda i: (i, 0, 0)),
                  pl.BlockSpec((1, N_MOD, d), lambda i: (2, 0, 0)),
                  _const_spec((1, d)),
                  pl.BlockSpec((d, QK_WIDTH), lambda i: (0, 1), pipeline_mode=pl.Buffered(1)),
                  pl.BlockSpec((d, V_WIDTH), lambda i: (0, V_OFF // V_WIDTH),
                               pipeline_mode=pl.Buffered(1))],
        out_specs=[pl.BlockSpec((1, lc, QK_WIDTH), lambda i: (i, 0, 0)),
                   pl.BlockSpec((1, lc, V_WIDTH), lambda i: (i, 0, 0))],
        out_shape=[jax.ShapeDtypeStruct((b, lc, QK_WIDTH), BF16),
                   jax.ShapeDtypeStruct((b, lc, V_WIDTH), BF16)],
        compiler_params=_params(("parallel",)),
        name="ctx_proj",
    )(ctx, mod3, norm_w, w_qk, w_in)


def _proj_kernel(x_ref, mod_ref, nw_ref, wqk_ref, w_ref, rtab_ref, ctab_ref, *refs, n_cast):
    q_ref, k_ref, v_ref, sg_ref, f_ref = refs[n_cast:n_cast + 5]
    _run_casts(refs[:n_cast] + refs[n_cast + 5:])
    lane = lax.broadcasted_iota(jnp.int32, (1, LANES), 1)
    by_row = (lane % (QK_DIM // 2)) < QK_DIM // 4

    def rope(t, trig, scale, out_ref, rows):
        for j in range(QK_WIDTH // LANES):
            tj = t[:, j * LANES:(j + 1) * LANES]
            r = tj * trig[0] + pltpu.roll(tj, LANES // 2, 1) * trig[1]
            out_ref[rows, j * LANES:(j + 1) * LANES] = (r * scale).astype(BF16)

    for r0 in range(0, x_ref.shape[0], PROJ_SUB):
        rows = slice(r0, r0 + PROJ_SUB)
        trig = []
        for cs in range(2):
            trig.append(jnp.concatenate(
                [jnp.where(by_row, rtab_ref[cs, g:g + 1, :], ctab_ref[cs])
                 for g in range(r0 // GRID_W, (r0 + PROJ_SUB) // GRID_W)], axis=0))
        x = x_ref[rows, :]
        h = _rms_norm(x, nw_ref[...]) * (1.0 + mod_ref[0, 1:2, :]) + mod_ref[0, 0:1, :]
        hb = h.astype(BF16)
        rope(_dot(hb, wqk_ref[:, :QK_WIDTH]), trig, QK_DIM ** -0.5, q_ref, rows)
        rope(_dot(hb, wqk_ref[:, QK_WIDTH:]), trig, 1.0, k_ref, rows)
        v_ref[rows, :] = _dot(hb, w_ref[:, V_OFF:G_OFF]).astype(BF16)
        g = _dot(hb, w_ref[:, G_OFF:F_OFF])
        sg_ref[rows, :] = (g * jax.nn.sigmoid(g)).astype(BF16)
        f_ref[rows, :] = _dot(hb, w_ref[:, F_OFF:IN_COLS]).astype(BF16)


def _proj(x2, mod3, norm_w, w_qk, w_in, rtab, ctab, seq, to_cast):
    t, d = x2.shape
    tl = PROJ_TILE
    tpb = seq // tl
    tok = lambda w: pl.BlockSpec((tl, w), lambda i: (i, 0))
    cast_specs, cast_shapes = _cast_jobs(to_cast, t // tl)
    out = pl.pallas_call(
        functools.partial(_proj_kernel, n_cast=len(to_cast)),
        grid=(t // tl,),
        in_specs=[tok(d),
                  pl.BlockSpec((1, N_MOD, d), lambda i: (i // tpb, 0, 0)),
                  _const_spec((1, d)),
                  _const_spec(w_qk.shape), _const_spec(w_in.shape),
                  pl.BlockSpec((2, tl // GRID_W, LANES), lambda i: (0, i % tpb, 0)),
                  _const_spec(ctab.shape)] + cast_specs,
        out_specs=[tok(QK_WIDTH), tok(QK_WIDTH), tok(V_WIDTH), tok(V_WIDTH),
                   tok(F_WIDTH)] + cast_specs,
        out_shape=[jax.ShapeDtypeStruct((t, QK_WIDTH), BF16),
                   jax.ShapeDtypeStruct((t, QK_WIDTH), BF16),
                   jax.ShapeDtypeStruct((t, V_WIDTH), BF16),
                   jax.ShapeDtypeStruct((t, V_WIDTH), BF16),
                   jax.ShapeDtypeStruct((t, F_WIDTH), BF16)] + cast_shapes,
        compiler_params=_params(("arbitrary",)),
        name="proj",
    )(x2, mod3, norm_w, w_qk, w_in, rtab, ctab, *to_cast)
    return out[:5], out[5:]


def _retention_kernel(af_ref, ab_ref, q_ref, k_ref, v_ref, kc_ref, vc_ref,
                      tw_ref, w2_ref, cc_ref, y_ref,
                      o_ref, z_ref, ds_scr, st_scr, d_scr, tab_scr, z_scr):
    c = RET_CHUNK
    seq = q_ref.shape[1]
    lc = kc_ref.shape[1]
    n = seq // c
    lgf = [-jnp.exp(af_ref[hh]) for hh in range(2)]
    lgb = [-jnp.exp(ab_ref[hh]) for hh in range(2)]
    half = QK_DIM // 2

    lane_head = (lax.broadcasted_iota(jnp.int32, (1, LANES), 1) // half) % 2
    masks = [(lane_head == hh).astype(BF16) for hh in range(2)]
    lgf_lane = jnp.where(lane_head == 0, lgf[0], lgf[1])
    lgb_lane = jnp.where(lane_head == 0, lgb[0], lgb[1])
    row_head = (lax.broadcasted_iota(jnp.int32, (LANES, 1), 0) // half) % 2
    lgf_row = jnp.where(row_head == 0, lgf[0][:, 0:1], lgf[1][:, 0:1])
    lgb_row = jnp.where(row_head == 0, lgb[0][:, 0:1], lgb[1][:, 0:1])

    pos = lax.broadcasted_iota(jnp.int32, (c, LANES), 0).astype(F32)
    tab_scr[0] = jnp.exp(lgf_lane * (pos + 1.0))
    tab_scr[1] = jnp.exp(lgb_lane * (c - pos))
    diff = (lax.broadcasted_iota(jnp.int32, (c, c), 0)
            - lax.broadcasted_iota(jnp.int32, (c, c), 1)).astype(F32)
    for hh in range(2):
        d_scr[hh] = (jnp.where(diff >= 0, jnp.exp(lgf[hh][:, 0:1] * jnp.maximum(diff, 0.0)), 0.0)
                     + jnp.where(diff <= 0, jnp.exp(lgb[hh][:, 0:1] * jnp.maximum(-diff, 0.0)), 0.0))

    def k_decays(tokens):
        t = lax.broadcasted_iota(jnp.int32, (1, tokens), 1).astype(F32)
        return jnp.exp(lgf_row * (tokens - 1.0 - t)), jnp.exp(lgb_row * t)

    def state_increment(k_rows, v_rows, decays):
        kt = jnp.transpose(k_rows.astype(F32))
        lhs = jnp.concatenate([kt * decays[0], kt * decays[1]], axis=0).astype(BF16)
        return _dot(lhs, v_rows)

    kdec = k_decays(c)

    def incr(i, carry):
        rows = pl.ds(pl.multiple_of(i * c, c), c)
        ds_scr[i] = state_increment(k_ref[0, rows, :], v_ref[0, rows, :], kdec)
        return carry

    lax.fori_loop(0, n, incr, 0, unroll=RET_UNROLL)

    s0 = state_increment(kc_ref[0], vc_ref[0], k_decays(lc))
    col_head = lax.broadcasted_iota(jnp.int32, (1, 2 * V_DIM), 1) // V_DIM
    own = (row_head == col_head).astype(F32)
    decay_f = jnp.exp(lgf_row * c) * own
    decay_b = jnp.exp(lgb_row * c) * own

    def scan_f(i, s):
        st_scr[i, 0:LANES, :] = (s * own).astype(BF16)
        return decay_f * s + ds_scr[i, 0:LANES, :]

    lax.fori_loop(0, n, scan_f, s0[0:LANES])

    def scan_b(t, s):
        i = n - 1 - t
        st_scr[i, LANES:2 * LANES, :] = (s * own).astype(BF16)
        return decay_b * s + ds_scr[i, LANES:2 * LANES, :]

    lax.fori_loop(0, n, scan_b, s0[LANES:2 * LANES])

    def outputs(i):
        rows = pl.ds(pl.multiple_of(i * c, c), c)
        q = q_ref[0, rows, :]
        k = k_ref[0, rows, :]
        qf = q.astype(F32)
        qd = jnp.concatenate([qf * tab_scr[0], qf * tab_scr[1]], axis=1).astype(BF16)
        inter = _dot(qd, st_scr[i])
        for hh in range(2):
            cols = slice(hh * V_DIM, (hh + 1) * V_DIM)
            scores = lax.dot_general(q * masks[hh], k, (((1,), (1,)), ((), ())),
                                     preferred_element_type=F32)
            o = _dot((scores * d_scr[hh]).astype(BF16), v_ref[0, rows, cols]) + inter[:, cols]
            o_ref[0, rows, cols] = o.astype(BF16)

    per_k1 = n // DFT_ROWS
    cc = cc_ref[...].astype(BF16)

    def outputs_and_dft(j, carry):
        for u in range(per_k1):
            outputs(j * per_k1 + u)
        _dft_second_stage(j, tw_ref, w2_ref, cc, y_ref, z_scr)
        return carry

    lax.fori_loop(0, DFT_ROWS, outputs_and_dft, 0, unroll=RET_UNROLL // per_k1)
    _dft_emit(z_scr, z_ref)


def _retention_and_dft(a_f, a_b, q, k, v, kc, vc, y, tw, w2, w_c):
    b, seq, _ = v.shape
    lc = kc.shape[1]
    c = RET_CHUNK
    n = seq // c
    r = DFT_ROWS
    _, _, n1, n2, w = y.shape
    assert n % r == 0 and n1 // r == HEADS // 2
    dec = pl.BlockSpec((2, 1, LANES), lambda i, p: (p, 0, 0))
    qk = lambda rows: pl.BlockSpec((1, rows, LANES), lambda i, p: (i, 0, p))
    vv = lambda rows: pl.BlockSpec((1, rows, 2 * V_DIM), lambda i, p: (i, 0, p))
    return pl.pallas_call(
        _retention_kernel,
        grid=(b, HEADS // 2),
        in_specs=[dec, dec, qk(seq), qk(seq), vv(seq), qk(lc), vv(lc),
                  pl.BlockSpec((r, 2, n2), lambda i, p: (p, 0, 0)),
                  _const_spec((2, n2, n2)),
                  _const_spec((2 * F_GROUP_DIM, F_GROUP_DIM)),
                  pl.BlockSpec((1, 2, r, n2, w), lambda i, p: (i, 0, p, 0, 0))],
        out_specs=[vv(seq), pl.BlockSpec((1, n2, r, w), lambda i, p: (i, 0, p, 0))],
        out_shape=[jax.ShapeDtypeStruct((b, seq, V_WIDTH), BF16),
                   jax.ShapeDtypeStruct((b, n2, n1, w), BF16)],
        scratch_shapes=[pltpu.VMEM((n, 2 * LANES, 2 * V_DIM), F32),
                        pltpu.VMEM((n, 2 * LANES, 2 * V_DIM), BF16),
                        pltpu.VMEM((2, c, c), F32),
                        pltpu.VMEM((2, c, LANES), F32),
                        pltpu.VMEM((F_GROUPS, n2 * r, F_GROUP_DIM), F32)],
        compiler_params=_params(("parallel", "parallel")),
        name="retention_dft",
    )(a_f, a_b, q, k, v, kc, vc, tw, w2, w_c, y)


def _dft_a_kernel(w_ref, x_ref, y_ref):
    _, n1, rows, w = x_ref.shape
    r = DFT_ROWS
    nh = n1 // 2 + 1
    for r0 in range(0, rows, r):
        x = x_ref[0, :, r0:r0 + r, :].reshape(n1 * r, w)
        y = _dot(w_ref[...], x).reshape(2, nh, r, w).astype(BF16)
        y_ref[0, :, 0:nh, r0:r0 + r, :] = y
        for k1 in range(1, n1 // 2):
            y_ref[0, 0, n1 - k1, r0:r0 + r, :] = y[0, k1]
            y_ref[0, 1, n1 - k1, r0:r0 + r, :] = -y[1, k1]


def _dft_second_stage(j, tw_ref, w2_ref, cc, y_ref, z_scr):
    n2 = y_ref.shape[3]
    w2c = w2_ref[0]
    w2s = w2_ref[1]
    tc = tw_ref[j, 0:1, :]
    ts = tw_ref[j, 1:2, :]
    ec = w2c * tc - w2s * ts
    es = w2s * tc + w2c * ts
    m = jnp.concatenate([jnp.concatenate([ec, -es], axis=1),
                         jnp.concatenate([es, ec], axis=1)], axis=0).astype(BF16)
    y = jnp.concatenate([y_ref[0, 0, j], y_ref[0, 1, j]], axis=0)
    zz = _dot(m, y)
    for g in range(F_GROUPS):
        cols = slice(g * F_GROUP_DIM, (g + 1) * F_GROUP_DIM)
        zcs = jnp.concatenate([zz[:n2, cols], zz[n2:, cols]], axis=1).astype(BF16)
        z_scr[g, pl.ds(j, n2, stride=DFT_ROWS), :] = _dot(zcs, cc)


def _dft_emit(z_scr, z_ref):
    n2 = z_ref.shape[1]
    z = jnp.concatenate([z_scr[g] for g in range(F_GROUPS)], axis=1)
    z_ref[0] = z.reshape(n2, DFT_ROWS, F_WIDTH).astype(BF16)


def _dft_tables(seq):
    n1 = DFT_N1
    n2 = seq // n1

    def cs(num, den):
        ang = 2.0 * np.pi * (num % den) / den
        return np.cos(ang), np.sin(ang)

    a = np.arange(n1)
    w_a = np.concatenate(cs(a[:n1 // 2 + 1, None] * a[None, :], n1), axis=0)
    m = np.arange(n2)
    tw = np.stack(cs(a[:, None] * m[None, :], seq), axis=1)
    w2 = np.stack(cs(m[:, None] * m[None, :], n2), axis=0)
    ch = np.arange(F_GROUP_DIM)
    cc, sc = cs(ch[:, None] * ch[None, :], F_GROUP_DIM)
    scale = 1.0 / np.sqrt(seq * F_GROUP_DIM)
    w_c = np.concatenate([cc, -sc], axis=0) * scale
    return [jnp.asarray(t, dtype=F32) for t in (w_a, tw, w2, w_c)]


def _dft_first_stage(f):
    b, seq, w = f.shape
    n1 = DFT_N1
    n2 = seq // n1
    w_a, tw, w2, w_c = _dft_tables(seq)
    r = DFT_ROWS
    spread = (jnp.arange(n1 * r)[None, :] // r == jnp.arange(n1)[:, None]).astype(F32)
    w_rep = jnp.dot(w_a, spread, precision=lax.Precision.HIGHEST)
    same_r = jnp.arange(w_a.shape[0] * r)[:, None] % r == jnp.arange(n1 * r)[None, :] % r
    w_a = jnp.where(same_r, jnp.repeat(w_rep, r, axis=0), 0.0).astype(BF16)
    y = pl.pallas_call(
        _dft_a_kernel,
        grid=(b, n2 // DFT_A_ROWS),
        in_specs=[_const_spec(w_a.shape),
                  pl.BlockSpec((1, n1, DFT_A_ROWS, w), lambda i, j: (i, 0, j, 0))],
        out_specs=pl.BlockSpec((1, 2, n1, DFT_A_ROWS, w), lambda i, j: (i, 0, 0, j, 0)),
        out_shape=jax.ShapeDtypeStruct((b, 2, n1, n2, w), BF16),
        compiler_params=_params(("parallel", "parallel")),
        name="dft_a",
    )(w_a, f.reshape(b, n1, n2, w))
    return y, tw, w2, w_c


def _merge_kernel(x_ref, mod_ref, nw_ref, n2w_ref, ro_ref, sg_ref, z_ref, wbg_ref, bbg_ref,
                  wro_ref, wfo_ref, wout_ref, *refs, n_cast):
    o_ref, h2_ref = refs[n_cast:n_cast + 2]
    _run_casts(refs[:n_cast] + refs[n_cast + 2:])
    for r0 in range(0, x_ref.shape[0], MERGE_SUB):
        rows = slice(r0, r0 + MERGE_SUB)
        x = x_ref[rows, :]
        h = _rms_norm(x, nw_ref[...]) * (1.0 + mod_ref[0, 1:2, :]) + mod_ref[0, 0:1, :]
        gates = jax.nn.sigmoid(_dot(h.astype(BF16), wbg_ref[...]) + bbg_ref[...])
        gated = []
        for hd in range(HEADS):
            cols = slice(hd * V_DIM, (hd + 1) * V_DIM)
            o = ro_ref[rows, cols].astype(F32)
            oc = o - jnp.mean(o, axis=-1, keepdims=True)
            var = jnp.mean(oc * oc, axis=-1, keepdims=True)
            gated.append((oc * lax.rsqrt(var + EPS) * sg_ref[rows, cols].astype(F32)).astype(BF16))
        ret_d = _dot(jnp.concatenate(gated, axis=1), wro_ref[...])
        four_d = _dot(z_ref[rows, :], wfo_ref[...])
        m = gates[:, :D_MODEL] * ret_d + gates[:, D_MODEL:] * four_d
        y = _dot(m.astype(BF16), wout_ref[...])
        x1 = x + mod_ref[0, 2:3, :] * y
        o_ref[rows, :] = x1
        h2 = _rms_norm(x1, n2w_ref[...]) * (1.0 + mod_ref[0, 4:5, :]) + mod_ref[0, 3:4, :]
        h2_ref[rows, :] = h2.astype(BF16)


def _merge(x2, mod3, norm_w, norm2_w, ro, sg, z, w_bg, b_bg, w_ro, w_fo, w_out, seq, to_cast):
    t, d = x2.shape
    tl = MERGE_TILE
    tpb = seq // tl
    tok = lambda w: pl.BlockSpec((tl, w), lambda i: (i, 0))
    cast_specs, cast_shapes = _cast_jobs(to_cast, t // tl)
    out = pl.pallas_call(
        functools.partial(_merge_kernel, n_cast=len(to_cast)),
        grid=(t // tl,),
        in_specs=[tok(d),
                  pl.BlockSpec((1, N_MOD, d), lambda i: (i // tpb, 0, 0)),
                  _const_spec((1, d)), _const_spec((1, d)),
                  tok(V_WIDTH), tok(V_WIDTH), tok(F_WIDTH),
                  _const_spec(w_bg.shape), _const_spec((1, 2 * d)),
                  _const_spec(w_ro.shape), _const_spec(w_fo.shape),
                  _const_spec(w_out.shape)] + cast_specs,
        out_specs=[tok(d), tok(d)] + cast_specs,
        out_shape=[jax.ShapeDtypeStruct((t, d), F32),
                   jax.ShapeDtypeStruct((t, d), BF16)] + cast_shapes,
        compiler_params=_params(("arbitrary",)),
        name="merge",
    )(x2, mod3, norm_w, norm2_w, ro, sg, z, w_bg, b_bg, w_ro, w_fo, w_out, *to_cast)
    return out[:2], out[2:]


HALO = 8
FFN_LEAD = 16
FFN_SUB = 512


def _two_gelu_tanh(x):
    c1 = np.sqrt(2.0 / np.pi)
    return x + x * jnp.tanh(x * (c1 + (c1 * 0.044715) * (x * x)))


def _ffn_kernel(xm_ref, hm_ref, xp_ref, xn_ref, mod_ref, nw_ref, wu_ref, cw_ref, cb_ref,
                wd_ref, fnw_ref, o_ref, h_scr, u_scr, act_scr, y_scr, *, tiles_per_seq):
    tl = xm_ref.shape[0]
    sub = FFN_SUB
    half = sub // 2
    nc = FFN_CHUNK
    i = pl.program_id(0)
    keep_prev = ((i % tiles_per_seq) != 0).astype(F32)
    keep_next = ((i % tiles_per_seq) != tiles_per_seq - 1).astype(F32)

    def pre(x):
        return _rms_norm(x, nw_ref[...]) * (1.0 + mod_ref[0, 4:5, :]) + mod_ref[0, 3:4, :]

    pad = jnp.zeros((FFN_LEAD - HALO, xm_ref.shape[1]), F32)
    h_scr[0:FFN_LEAD] = jnp.concatenate([pad, pre(xp_ref[...]) * keep_prev], axis=0).astype(BF16)
    h_scr[FFN_LEAD:FFN_LEAD + tl] = hm_ref[...]
    h_scr[FFN_LEAD + tl:] = jnp.concatenate([pre(xn_ref[...]) * keep_next, pad], axis=0).astype(BF16)

    def project(j):
        hb = h_scr[...]
        for part in range(2):
            lo = part * FFN_DIM + j * nc
            u = _dot(hb, wu_ref[:, lo:lo + nc])
            for s in range(nc // LANES):
                u_scr[j % 2, part, s] = u[:, s * LANES:(s + 1) * LANES]

    def conv(j, part, s, blk, scale):
        lo = part * FFN_DIM + j * nc + s * LANES
        w = cw_ref[:, lo:lo + LANES] * scale
        bias = cb_ref[:, lo:lo + LANES] * scale
        first = FFN_LEAD + blk * sub
        rows = lambda start: u_scr[j % 2, part, s, pl.ds(start, half, stride=2), :]
        before, even, odd, after = rows(first - 1), rows(first), rows(first + 1), rows(first + 2)
        return (before * w[0:1] + even * w[1:2] + odd * w[2:3] + bias,
                even * w[0:1] + odd * w[1:2] + after * w[2:3] + bias)

    def activate(j):
        for s in range(nc // LANES):
            cols = slice(j * nc + s * LANES, j * nc + (s + 1) * LANES)
            for blk in range(tl // sub):
                gate = conv(j, 0, s, blk, 1.0)
                val = conv(j, 1, s, blk, 0.5)
                for par in range(2):
                    r0 = blk * sub + par * half
                    act_scr[r0:r0 + half, cols] = (_two_gelu_tanh(gate[par]) * val[par]).astype(BF16)

    project(0)
    for j in range(N_FFN_CHUNKS):
        if j + 1 < N_FFN_CHUNKS:
            project(j + 1)
        activate(j)
    for blk in range(tl // sub):
        r0 = blk * sub
        y = _dot(act_scr[r0:r0 + sub, :], wd_ref[...])
        for s in range(y_scr.shape[0]):
            for par in range(2):
                y_scr[s, pl.ds(par, half, stride=2), :] = (
                    y[par * half:(par + 1) * half, s * LANES:(s + 1) * LANES])
        y = jnp.concatenate([y_scr[s] for s in range(y_scr.shape[0])], axis=1)
        x2 = xm_ref[r0:r0 + sub, :] + mod_ref[0, 5:6, :] * y
        o_ref[r0:r0 + sub, :] = _rms_norm(x2, fnw_ref[...])


def _ffn(x1, h2, mod3, norm_w, w_u, cw, cb, w_d, fnorm_w, seq):
    t, d = x1.shape
    tl = FFN_TILE
    tpb = seq // tl
    hb = tl // HALO
    last = t // HALO - 1
    rows = FFN_LEAD + tl + FFN_LEAD
    return pl.pallas_call(
        functools.partial(_ffn_kernel, tiles_per_seq=tpb),
        grid=(t // tl,),
        in_specs=[pl.BlockSpec((tl, d), lambda i: (i, 0)),
                  pl.BlockSpec((tl, d), lambda i: (i, 0)),
                  pl.BlockSpec((HALO, d), lambda i: (jnp.maximum(i * hb - 1, 0), 0)),
                  pl.BlockSpec((HALO, d), lambda i: (jnp.minimum((i + 1) * hb, last), 0)),
                  pl.BlockSpec((1, N_MOD, d), lambda i: (i // tpb, 0, 0)),
                  _const_spec((1, d)),
                  _const_spec(w_u.shape), _const_spec(cw.shape), _const_spec(cb.shape),
                  _const_spec(w_d.shape), _const_spec((1, d))],
        out_specs=pl.BlockSpec((tl, d), lambda i: (i, 0)),
        out_shape=jax.ShapeDtypeStruct((t, d), F32),
        scratch_shapes=[pltpu.VMEM((rows, d), BF16),
                        pltpu.VMEM((2, 2, FFN_CHUNK // LANES, rows, LANES), F32),
                        pltpu.VMEM((tl, FFN_DIM), BF16),
                        pltpu.VMEM((d // LANES, FFN_SUB, LANES), F32)],
        compiler_params=_params(("parallel",)),
        name="ffn",
    )(x1, h2, x1, x1, mod3, norm_w, w_u, cw, cb, w_d, fnorm_w)


def _rope_tables(seq):
    n_freq = QK_DIM // 4
    lane = jnp.arange(LANES)
    inv = ROPE_BASE ** (-(lane % n_freq).astype(F32) / n_freq)
    sign = jnp.where(lane < LANES // 2, -1.0, 1.0).astype(F32)

    def table(n):
        ang = jnp.arange(n, dtype=F32)[:, None] * inv[None, :]
        return jnp.stack([jnp.cos(ang), jnp.sin(ang) * sign])

    return table(seq // GRID_W), table(GRID_W)


def kernel(x, c, ctx, c_ctx, w_mod, b_mod, norm1_w, w_in, ret_decay_f, ret_decay_b,
           w_ret_out, w_four_out, w_branch_gate, b_branch_gate, w_out, norm2_w,
           w_up, conv_w, conv_b, w_down, final_norm_w):
    assert w_mod.shape[0] == 1, "single-layer block"
    b, seq, d = x.shape
    t = b * seq
    assert b == 2 and seq % RET_CHUNK == 0
    assert seq % FFN_TILE == 0 and seq % MERGE_TILE == 0 and seq % PROJ_TILE == 0

    c8 = jnp.concatenate([c, c_ctx[None, :], jnp.zeros((8 - b - 1, d), F32)], axis=0)
    mod, w_in_b, w_qk = _modulation(c8, w_mod[0], b_mod[0], w_in[0])
    mod3 = mod.reshape(8, N_MOD, d)
    n1w = norm1_w[0].reshape(1, d)

    kc, vc = _ctx_proj(ctx, mod3, n1w, w_qk, w_in_b)
    rtab, ctab = _rope_tables(seq)
    x2 = x.reshape(t, d)
    (q, k, v, sg, f), (w_bg, w_ro, w_fo, w_o) = _proj(
        x2, mod3, n1w, w_qk, w_in_b, rtab, ctab, seq,
        [w_branch_gate[0], w_ret_out[0], w_four_out[0], w_out[0]])

    a_f = jnp.broadcast_to(ret_decay_f[0][:, None, None], (HEADS, 1, LANES))
    a_b = jnp.broadcast_to(ret_decay_b[0][:, None, None], (HEADS, 1, LANES))
    y, tw, w2, w_c = _dft_first_stage(f.reshape(b, seq, F_WIDTH))
    ro, z = _retention_and_dft(a_f, a_b, q.reshape(b, seq, QK_WIDTH), k.reshape(b, seq, QK_WIDTH),
                               v.reshape(b, seq, V_WIDTH), kc, vc, y, tw, w2, w_c)

    n2w = norm2_w[0].reshape(1, d)
    (x1, h2), (w_u, w_d) = _merge(
        x2, mod3, n1w, n2w, ro.reshape(t, V_WIDTH), sg, z.reshape(t, F_WIDTH),
        w_bg, b_branch_gate[0].reshape(1, 2 * d), w_ro, w_fo, w_o, seq, [w_up[0], w_down[0]])

    out = _ffn(x1, h2, mod3, n2w, w_u, conv_w[0], conv_b[0].reshape(1, 2 * FFN_DIM), w_d,
               final_norm_w.reshape(1, d), seq)
    return out.reshape(b, seq, d)
```

```python
import functools

import numpy as np
import jax
import jax.numpy as jnp
from jax import lax
from jax.experimental import pallas as pl
from jax.experimental.pallas import tpu as pltpu

F32 = jnp.float32
BF16 = jnp.bfloat16

D_MODEL = 1024
GRID_W = 64
HEADS = 8
QK_DIM = 64
V_DIM = 128
QK_WIDTH = HEADS * QK_DIM
V_WIDTH = HEADS * V_DIM
ROPE_BASE = 10000.0
F_GROUPS = 4
F_GROUP_DIM = 128
F_WIDTH = F_GROUPS * F_GROUP_DIM
K_OFF = QK_WIDTH
V_OFF = K_OFF + QK_WIDTH
G_OFF = V_OFF + V_WIDTH
F_OFF = G_OFF + V_WIDTH
IN_COLS = F_OFF + F_WIDTH
FFN_DIM = 2816
N_MOD = 6
EPS = 1e-6

LANES = 128
RET_CHUNK = 256
RET_UNROLL = 16
FFN_CHUNK = 256
N_FFN_CHUNKS = FFN_DIM // FFN_CHUNK
FFN_TILE = 512
MERGE_TILE = 1024
MERGE_SUB = 512
PROJ_TILE = 1024
PROJ_SUB = 256
DFT_N1 = 64
BF16_ROWS = 16
DFT_ROWS = BF16_ROWS
DFT_A_ROWS = 2 * DFT_ROWS
VMEM_LIMIT = 56 * 1024 * 1024


def _params(sem):
    return pltpu.CompilerParams(dimension_semantics=sem, vmem_limit_bytes=VMEM_LIMIT)


def _dot(a, b):
    return jnp.dot(a, b, preferred_element_type=F32)


def _rms_norm(x, w):
    return x * lax.rsqrt(jnp.mean(x * x, axis=-1, keepdims=True) + EPS) * w


def _const_spec(shape):
    zeros = (0,) * len(shape)
    return pl.BlockSpec(shape, lambda *_: zeros, pipeline_mode=pl.Buffered(1))


def _cast_jobs(weights, steps):
    specs, shapes = [], []
    for w in weights:
        n_rows = w.shape[0]
        rows = -(-n_rows // steps)
        while rows % BF16_ROWS or n_rows % rows:
            rows += 1
        last = n_rows // rows - 1
        specs.append(pl.BlockSpec((rows, w.shape[1]), lambda i, last=last: (jnp.minimum(i, last), 0)))
        shapes.append(jax.ShapeDtypeStruct(w.shape, BF16))
    return specs, shapes


def _run_casts(refs):
    n = len(refs) // 2
    for src, dst in zip(refs[:n], refs[n:]):
        dst[...] = src[...].astype(BF16)


def _mod_kernel(c_ref, w_ref, b_ref, win_ref, o_ref, winb_ref, wqk_ref):
    c = c_ref[...]
    s = c * jax.nn.sigmoid(c)
    o_ref[...] = _dot(s.astype(BF16), w_ref[...].astype(BF16)) + b_ref[...]
    wb = win_ref[...].astype(BF16)
    winb_ref[...] = wb
    n_qk = 2 * QK_WIDTH
    src = lax.broadcasted_iota(jnp.int32, (n_qk, n_qk), 0)
    dst = lax.broadcasted_iota(jnp.int32, (n_qk, n_qk), 1)
    select = (src == _qk_source_column(dst)).astype(BF16)
    wqk_ref[...] = _dot(wb[:, :n_qk], select).astype(BF16)


def _qk_source_column(col):
    half = QK_DIM // 2
    lane = col % LANES
    head = 2 * ((col % QK_WIDTH) // LANES) + (lane // half) % 2
    return (col // QK_WIDTH) * QK_WIDTH + head * QK_DIM + (lane // (2 * half)) * half + lane % half


def _modulation(c8, w_mod, b_mod, w_in):
    n = w_mod.shape[1]
    steps = 4
    tn = n // steps
    d, n_in = w_in.shape
    rows = d // steps
    return pl.pallas_call(
        _mod_kernel,
        grid=(steps,),
        in_specs=[_const_spec((8, D_MODEL)),
                  pl.BlockSpec((D_MODEL, tn), lambda j: (0, j)),
                  pl.BlockSpec((1, tn), lambda j: (0, j)),
                  pl.BlockSpec((rows, n_in), lambda j: (j, 0))],
        out_specs=[pl.BlockSpec((8, tn), lambda j: (0, j)),
                   pl.BlockSpec((rows, n_in), lambda j: (j, 0)),
                   pl.BlockSpec((rows, 2 * QK_WIDTH), lambda j: (j, 0))],
        out_shape=[jax.ShapeDtypeStruct((8, n), F32),
                   jax.ShapeDtypeStruct((d, n_in), BF16),
                   jax.ShapeDtypeStruct((d, 2 * QK_WIDTH), BF16)],
        compiler_params=_params(("parallel",)),
        name="mod",
    )(c8, w_mod, b_mod.reshape(1, n), w_in)


def _ctx_kernel(x_ref, mod_ref, nw_ref, wk_ref, wv_ref, k_ref, v_ref):
    x = x_ref[0]
    h = _rms_norm(x, nw_ref[...]) * (1.0 + mod_ref[0, 1:2, :]) + mod_ref[0, 0:1, :]
    hb = h.astype(BF16)
    k_ref[0] = _dot(hb, wk_ref[...]).astype(BF16)
    v_ref[0] = _dot(hb, wv_ref[...]).astype(BF16)


def _ctx_proj(ctx, mod3, norm_w, w_qk, w_in):
    b, lc, d = ctx.shape
    assert V_OFF % V_WIDTH == 0
    return pl.pallas_call(
        _ctx_kernel,
        grid=(b,),
        in_specs=[pl.BlockSpec((1, lc, d), lambda i: (i, 0, 0)),
                  pl.BlockSpec((1, N_MOD, d), lambda i: (2, 0, 0)),
                  _const_spec((1, d)),
                  pl.BlockSpec((d, QK_WIDTH), lambda i: (0, 1), pipeline_mode=pl.Buffered(1)),
                  pl.BlockSpec((d, V_WIDTH), lambda i: (0, V_OFF // V_WIDTH),
                               pipeline_mode=pl.Buffered(1))],
        out_specs=[pl.BlockSpec((1, lc, QK_WIDTH), lambda i: (i, 0, 0)),
                   pl.BlockSpec((1, lc, V_WIDTH), lambda i: (i, 0, 0))],
        out_shape=[jax.ShapeDtypeStruct((b, lc, QK_WIDTH), BF16),
                   jax.ShapeDtypeStruct((b, lc, V_WIDTH), BF16)],
        compiler_params=_params(("parallel",)),
        name="ctx_proj",
    )(ctx, mod3, norm_w, w_qk, w_in)


def _proj_kernel(x_ref, mod_ref, nw_ref, wqk_ref, w_ref, rtab_ref, ctab_ref, *refs, n_cast):
    q_ref, k_ref, v_ref, sg_ref, f_ref = refs[n_cast:n_cast + 5]
    _run_casts(refs[:n_cast] + refs[n_cast + 5:])
    lane = lax.broadcasted_iota(jnp.int32, (1, LANES), 1)
    by_row = (lane % (QK_DIM // 2)) < QK_DIM // 4

    def rope(t, trig, scale, out_ref, rows):
        for j in range(QK_WIDTH // LANES):
            tj = t[:, j * LANES:(j + 1) * LANES]
            r = tj * trig[0] + pltpu.roll(tj, LANES // 2, 1) * trig[1]
            out_ref[rows, j * LANES:(j + 1) * LANES] = (r * scale).astype(BF16)

    for r0 in range(0, x_ref.shape[0], PROJ_SUB):
        rows = slice(r0, r0 + PROJ_SUB)
        trig = []
        for cs in range(2):
            trig.append(jnp.concatenate(
                [jnp.where(by_row, rtab_ref[cs, g:g + 1, :], ctab_ref[cs])
                 for g in range(r0 // GRID_W, (r0 + PROJ_SUB) // GRID_W)], axis=0))
        x = x_ref[rows, :]
        h = _rms_norm(x, nw_ref[...]) * (1.0 + mod_ref[0, 1:2, :]) + mod_ref[0, 0:1, :]
        hb = h.astype(BF16)
        rope(_dot(hb, wqk_ref[:, :QK_WIDTH]), trig, QK_DIM ** -0.5, q_ref, rows)
        rope(_dot(hb, wqk_ref[:, QK_WIDTH:]), trig, 1.0, k_ref, rows)
        v_ref[rows, :] = _dot(hb, w_ref[:, V_OFF:G_OFF]).astype(BF16)
        g = _dot(hb, w_ref[:, G_OFF:F_OFF])
        sg_ref[rows, :] = (g * jax.nn.sigmoid(g)).astype(BF16)
        f_ref[rows, :] = _dot(hb, w_ref[:, F_OFF:IN_COLS]).astype(BF16)


def _proj(x2, mod3, norm_w, w_qk, w_in, rtab, ctab, seq, to_cast):
    t, d = x2.shape
    tl = PROJ_TILE
    tpb = seq // tl
    tok = lambda w: pl.BlockSpec((tl, w), lambda i: (i, 0))
    cast_specs, cast_shapes = _cast_jobs(to_cast, t // tl)
    out = pl.pallas_call(
        functools.partial(_proj_kernel, n_cast=len(to_cast)),
        grid=(t // tl,),
        in_specs=[tok(d),
                  pl.BlockSpec((1, N_MOD, d), lambda i: (i // tpb, 0, 0)),
                  _const_spec((1, d)),
                  _const_spec(w_qk.shape), _const_spec(w_in.shape),
                  pl.BlockSpec((2, tl // GRID_W, LANES), lambda i: (0, i % tpb, 0)),
                  _const_spec(ctab.shape)] + cast_specs,
        out_specs=[tok(QK_WIDTH), tok(QK_WIDTH), tok(V_WIDTH), tok(V_WIDTH),
                   tok(F_WIDTH)] + cast_specs,
        out_shape=[jax.ShapeDtypeStruct((t, QK_WIDTH), BF16),
                   jax.ShapeDtypeStruct((t, QK_WIDTH), BF16),
                   jax.ShapeDtypeStruct((t, V_WIDTH), BF16),
                   jax.ShapeDtypeStruct((t, V_WIDTH), BF16),
                   jax.ShapeDtypeStruct((t, F_WIDTH), BF16)] + cast_shapes,
        compiler_params=_params(("arbitrary",)),
        name="proj",
    )(x2, mod3, norm_w, w_qk, w_in, rtab, ctab, *to_cast)
    return out[:5], out[5:]


def _retention_kernel(af_ref, ab_ref, q_ref, k_ref, v_ref, kc_ref, vc_ref,
                      tw_ref, w2_ref, cc_ref, y_ref,
                      o_ref, z_ref, ds_scr, st_scr, d_scr, tab_scr, z_scr):
    c = RET_CHUNK
    seq = q_ref.shape[1]
    lc = kc_ref.shape[1]
    n = seq // c
    lgf = [-jnp.exp(af_ref[hh]) for hh in range(2)]
    lgb = [-jnp.exp(ab_ref[hh]) for hh in range(2)]
    half = QK_DIM // 2

    lane_head = (lax.broadcasted_iota(jnp.int32, (1, LANES), 1) // half) % 2
    masks = [(lane_head == hh).astype(BF16) for hh in range(2)]
    lgf_lane = jnp.where(lane_head == 0, lgf[0], lgf[1])
    lgb_lane = jnp.where(lane_head == 0, lgb[0], lgb[1])
    row_head = (lax.broadcasted_iota(jnp.int32, (LANES, 1), 0) // half) % 2
    lgf_row = jnp.where(row_head == 0, lgf[0][:, 0:1], lgf[1][:, 0:1])
    lgb_row = jnp.where(row_head == 0, lgb[0][:, 0:1], lgb[1][:, 0:1])

    pos = lax.broadcasted_iota(jnp.int32, (c, LANES), 0).astype(F32)
    tab_scr[0] = jnp.exp(lgf_lane * (pos + 1.0))
    tab_scr[1] = jnp.exp(lgb_lane * (c - pos))
    diff = (lax.broadcasted_iota(jnp.int32, (c, c), 0)
            - lax.broadcasted_iota(jnp.int32, (c, c), 1)).astype(F32)
    for hh in range(2):
        d_scr[hh] = (jnp.where(diff >= 0, jnp.exp(lgf[hh][:, 0:1] * jnp.maximum(diff, 0.0)), 0.0)
                     + jnp.where(diff <= 0, jnp.exp(lgb[hh][:, 0:1] * jnp.maximum(-diff, 0.0)), 0.0))

    def k_decays(tokens):
        t = lax.broadcasted_iota(jnp.int32, (1, tokens), 1).astype(F32)
        return jnp.exp(lgf_row * (tokens - 1.0 - t)), jnp.exp(lgb_row * t)

    def state_increment(k_rows, v_rows, decays):
        kt = jnp.transpose(k_rows.astype(F32))
        lhs = jnp.concatenate([kt * decays[0], kt * decays[1]], axis=0).astype(BF16)
        return _dot(lhs, v_rows)

    kdec = k_decays(c)

    def incr(i, carry):
        rows = pl.ds(pl.multiple_of(i * c, c), c)
        ds_scr[i] = state_increment(k_ref[0, rows, :], v_ref[0, rows, :], kdec)
        return carry

    lax.fori_loop(0, n, incr, 0, unroll=RET_UNROLL)

    s0 = state_increment(kc_ref[0], vc_ref[0], k_decays(lc))
    col_head = lax.broadcasted_iota(jnp.int32, (1, 2 * V_DIM), 1) // V_DIM
    own = (row_head == col_head).astype(F32)
    decay_f = jnp.exp(lgf_row * c) * own
    decay_b = jnp.exp(lgb_row * c) * own

    def scan_f(i, s):
        st_scr[i, 0:LANES, :] = (s * own).astype(BF16)
        return decay_f * s + ds_scr[i, 0:LANES, :]

    lax.fori_loop(0, n, scan_f, s0[0:LANES])

    def scan_b(t, s):
        i = n - 1 - t
        st_scr[i, LANES:2 * LANES, :] = (s * own).astype(BF16)
        return decay_b * s + ds_scr[i, LANES:2 * LANES, :]

    lax.fori_loop(0, n, scan_b, s0[LANES:2 * LANES])

    def outputs(i):
        rows = pl.ds(pl.multiple_of(i * c, c), c)
        q = q_ref[0, rows, :]
        k = k_ref[0, rows, :]
        qf = q.astype(F32)
        qd = jnp.concatenate([qf * tab_scr[0], qf * tab_scr[1]], axis=1).astype(BF16)
        inter = _dot(qd, st_scr[i])
        for hh in range(2):
            cols = slice(hh * V_DIM, (hh + 1) * V_DIM)
            scores = lax.dot_general(q * masks[hh], k, (((1,), (1,)), ((), ())),
                                     preferred_element_type=F32)
            o = _dot((scores * d_scr[hh]).astype(BF16), v_ref[0, rows, cols]) + inter[:, cols]
            o_ref[0, rows, cols] = o.astype(BF16)

    per_k1 = n // DFT_ROWS
    cc = cc_ref[...].astype(BF16)

    def outputs_and_dft(j, carry):
        for u in range(per_k1):
            outputs(j * per_k1 + u)
        _dft_second_stage(j, tw_ref, w2_ref, cc, y_ref, z_scr)
        return carry

    lax.fori_loop(0, DFT_ROWS, outputs_and_dft, 0, unroll=RET_UNROLL // per_k1)
    _dft_emit(z_scr, z_ref)


def _retention_and_dft(a_f, a_b, q, k, v, kc, vc, y, tw, w2, w_c):
    b, seq, _ = v.shape
    lc = kc.shape[1]
    c = RET_CHUNK
    n = seq // c
    r = DFT_ROWS
    _, _, n1, n2, w = y.shape
    assert n % r == 0 and n1 // r == HEADS // 2
    dec = pl.BlockSpec((2, 1, LANES), lambda i, p: (p, 0, 0))
    qk = lambda rows: pl.BlockSpec((1, rows, LANES), lambda i, p: (i, 0, p))
    vv = lambda rows: pl.BlockSpec((1, rows, 2 * V_DIM), lambda i, p: (i, 0, p))
    return pl.pallas_call(
        _retention_kernel,
        grid=(b, HEADS // 2),
        in_specs=[dec, dec, qk(seq), qk(seq), vv(seq), qk(lc), vv(lc),
                  pl.BlockSpec((r, 2, n2), lambda i, p: (p, 0, 0)),
                  _const_spec((2, n2, n2)),
                  _const_spec((2 * F_GROUP_DIM, F_GROUP_DIM)),
                  pl.BlockSpec((1, 2, r, n2, w), lambda i, p: (i, 0, p, 0, 0))],
        out_specs=[vv(seq), pl.BlockSpec((1, n2, r, w), lambda i, p: (i, 0, p, 0))],
        out_shape=[jax.ShapeDtypeStruct((b, seq, V_WIDTH), BF16),
                   jax.ShapeDtypeStruct((b, n2, n1, w), BF16)],
        scratch_shapes=[pltpu.VMEM((n, 2 * LANES, 2 * V_DIM), F32),
                        pltpu.VMEM((n, 2 * LANES, 2 * V_DIM), BF16),
                        pltpu.VMEM((2, c, c), F32),
                        pltpu.VMEM((2, c, LANES), F32),
                        pltpu.VMEM((F_GROUPS, n2 * r, F_GROUP_DIM), F32)],
        compiler_params=_params(("parallel", "parallel")),
        name="retention_dft",
    )(a_f, a_b, q, k, v, kc, vc, tw, w2, w_c, y)


def _dft_a_kernel(w_ref, x_ref, y_ref):
    _, n1, rows, w = x_ref.shape
    r = DFT_ROWS
    nh = n1 // 2 + 1
    for r0 in range(0, rows, r):
        x = x_ref[0, :, r0:r0 + r, :].reshape(n1 * r, w)
        y = _dot(w_ref[...], x).reshape(2, nh, r, w).astype(BF16)
        y_ref[0, :, 0:nh, r0:r0 + r, :] = y
        for k1 in range(1, n1 // 2):
            y_ref[0, 0, n1 - k1, r0:r0 + r, :] = y[0, k1]
            y_ref[0, 1, n1 - k1, r0:r0 + r, :] = -y[1, k1]


def _dft_second_stage(j, tw_ref, w2_ref, cc, y_ref, z_scr):
    n2 = y_ref.shape[3]
    w2c = w2_ref[0]
    w2s = w2_ref[1]
    tc = tw_ref[j, 0:1, :]
    ts = tw_ref[j, 1:2, :]
    ec = w2c * tc - w2s * ts
    es = w2s * tc + w2c * ts
    m = jnp.concatenate([jnp.concatenate([ec, -es], axis=1),
                         jnp.concatenate([es, ec], axis=1)], axis=0).astype(BF16)
    y = jnp.concatenate([y_ref[0, 0, j], y_ref[0, 1, j]], axis=0)
    zz = _dot(m, y)
    for g in range(F_GROUPS):
        cols = slice(g * F_GROUP_DIM, (g + 1) * F_GROUP_DIM)
        zcs = jnp.concatenate([zz[:n2, cols], zz[n2:, cols]], axis=1).astype(BF16)
        z_scr[g, pl.ds(j, n2, stride=DFT_ROWS), :] = _dot(zcs, cc)


def _dft_emit(z_scr, z_ref):
    n2 = z_ref.shape[1]
    z = jnp.concatenate([z_scr[g] for g in range(F_GROUPS)], axis=1)
    z_ref[0] = z.reshape(n2, DFT_ROWS, F_WIDTH).astype(BF16)


def _dft_tables(seq):
    n1 = DFT_N1
    n2 = seq // n1

    def cs(num, den):
        ang = 2.0 * np.pi * (num % den) / den
        return np.cos(ang), np.sin(ang)

    a = np.arange(n1)
    w_a = np.concatenate(cs(a[:n1 // 2 + 1, None] * a[None, :], n1), axis=0)
    m = np.arange(n2)
    tw = np.stack(cs(a[:, None] * m[None, :], seq), axis=1)
    w2 = np.stack(cs(m[:, None] * m[None, :], n2), axis=0)
    ch = np.arange(F_GROUP_DIM)
    cc, sc = cs(ch[:, None] * ch[None, :], F_GROUP_DIM)
    scale = 1.0 / np.sqrt(seq * F_GROUP_DIM)
    w_c = np.concatenate([cc, -sc], axis=0) * scale
    return [jnp.asarray(t, dtype=F32) for t in (w_a, tw, w2, w_c)]


def _dft_first_stage(f):
    b, seq, w = f.shape
    n1 = DFT_N1
    n2 = seq // n1
    w_a, tw, w2, w_c = _dft_tables(seq)
    r = DFT_ROWS
    spread = (jnp.arange(n1 * r)[None, :] // r == jnp.arange(n1)[:, None]).astype(F32)
    w_rep = jnp.dot(w_a, spread, precision=lax.Precision.HIGHEST)
    same_r = jnp.arange(w_a.shape[0] * r)[:, None] % r == jnp.arange(n1 * r)[None, :] % r
    w_a = jnp.where(same_r, jnp.repeat(w_rep, r, axis=0), 0.0).astype(BF16)
    y = pl.pallas_call(
        _dft_a_kernel,
        grid=(b, n2 // DFT_A_ROWS),
        in_specs=[_const_spec(w_a.shape),
                  pl.BlockSpec((1, n1, DFT_A_ROWS, w), lambda i, j: (i, 0, j, 0))],
        out_specs=pl.BlockSpec((1, 2, n1, DFT_A_ROWS, w), lambda i, j: (i, 0, 0, j, 0)),
        out_shape=jax.ShapeDtypeStruct((b, 2, n1, n2, w), BF16),
        compiler_params=_params(("parallel", "parallel")),
        name="dft_a",
    )(w_a, f.reshape(b, n1, n2, w))
    return y, tw, w2, w_c


def _merge_kernel(x_ref, mod_ref, nw_ref, n2w_ref, ro_ref, sg_ref, z_ref, wbg_ref, bbg_ref,
                  wro_ref, wfo_ref, wout_ref, *refs, n_cast):
    o_ref, h2_ref = refs[n_cast:n_cast + 2]
    _run_casts(refs[:n_cast] + refs[n_cast + 2:])
    for r0 in range(0, x_ref.shape[0], MERGE_SUB):
        rows = slice(r0, r0 + MERGE_SUB)
        x = x_ref[rows, :]
        h = _rms_norm(x, nw_ref[...]) * (1.0 + mod_ref[0, 1:2, :]) + mod_ref[0, 0:1, :]
        gates = jax.nn.sigmoid(_dot(h.astype(BF16), wbg_ref[...]) + bbg_ref[...])
        gated = []
        for hd in range(HEADS):
            cols = slice(hd * V_DIM, (hd + 1) * V_DIM)
            o = ro_ref[rows, cols].astype(F32)
            oc = o - jnp.mean(o, axis=-1, keepdims=True)
            var = jnp.mean(oc * oc, axis=-1, keepdims=True)
            gated.append((oc * lax.rsqrt(var + EPS) * sg_ref[rows, cols].astype(F32)).astype(BF16))
        ret_d = _dot(jnp.concatenate(gated, axis=1), wro_ref[...])
        four_d = _dot(z_ref[rows, :], wfo_ref[...])
        m = gates[:, :D_MODEL] * ret_d + gates[:, D_MODEL:] * four_d
        y = _dot(m.astype(BF16), wout_ref[...])
        x1 = x + mod_ref[0, 2:3, :] * y
        o_ref[rows, :] = x1
        h2 = _rms_norm(x1, n2w_ref[...]) * (1.0 + mod_ref[0, 4:5, :]) + mod_ref[0, 3:4, :]
        h2_ref[rows, :] = h2.astype(BF16)


def _merge(x2, mod3, norm_w, norm2_w, ro, sg, z, w_bg, b_bg, w_ro, w_fo, w_out, seq, to_cast):
    t, d = x2.shape
    tl = MERGE_TILE
    tpb = seq // tl
    tok = lambda w: pl.BlockSpec((tl, w), lambda i: (i, 0))
    cast_specs, cast_shapes = _cast_jobs(to_cast, t // tl)
    out = pl.pallas_call(
        functools.partial(_merge_kernel, n_cast=len(to_cast)),
        grid=(t // tl,),
        in_specs=[tok(d),
                  pl.BlockSpec((1, N_MOD, d), lambda i: (i // tpb, 0, 0)),
                  _const_spec((1, d)), _const_spec((1, d)),
                  tok(V_WIDTH), tok(V_WIDTH), tok(F_WIDTH),
                  _const_spec(w_bg.shape), _const_spec((1, 2 * d)),
                  _const_spec(w_ro.shape), _const_spec(w_fo.shape),
                  _const_spec(w_out.shape)] + cast_specs,
        out_specs=[tok(d), tok(d)] + cast_specs,
        out_shape=[jax.ShapeDtypeStruct((t, d), F32),
                   jax.ShapeDtypeStruct((t, d), BF16)] + cast_shapes,
        compiler_params=_params(("arbitrary",)),
        name="merge",
    )(x2, mod3, norm_w, norm2_w, ro, sg, z, w_bg, b_bg, w_ro, w_fo, w_out, *to_cast)
    return out[:2], out[2:]


HALO = 8
FFN_LEAD = 16
FFN_SUB = 256


def _two_gelu_tanh(x):
    c1 = np.sqrt(2.0 / np.pi)
    return x + x * jnp.tanh(x * (c1 + (c1 * 0.044715) * (x * x)))


def _ffn_kernel(xm_ref, hm_ref, xp_ref, xn_ref, mod_ref, nw_ref, wu_ref, cw_ref, cb_ref,
                wd_ref, fnw_ref, o_ref, h_scr, u_scr, act_scr, y_scr, *, tiles_per_seq):
    tl = xm_ref.shape[0]
    sub = FFN_SUB
    half = sub // 2
    nc = FFN_CHUNK
    i = pl.program_id(0)
    keep_prev = ((i % tiles_per_seq) != 0).astype(F32)
    keep_next = ((i % tiles_per_seq) != tiles_per_seq - 1).astype(F32)

    def pre(x):
        return _rms_norm(x, nw_ref[...]) * (1.0 + mod_ref[0, 4:5, :]) + mod_ref[0, 3:4, :]

    pad = jnp.zeros((FFN_LEAD - HALO, xm_ref.shape[1]), F32)
    h_scr[0:FFN_LEAD] = jnp.concatenate([pad, pre(xp_ref[...]) * keep_prev], axis=0).astype(BF16)
    h_scr[FFN_LEAD:FFN_LEAD + tl] = hm_ref[...]
    h_scr[FFN_LEAD + tl:] = jnp.concatenate([pre(xn_ref[...]) * keep_next, pad], axis=0).astype(BF16)

    def project(j):
        hb = h_scr[...]
        for part in range(2):
            lo = part * FFN_DIM + j * nc
            u = _dot(hb, wu_ref[:, lo:lo + nc])
            for s in range(nc // LANES):
                u_scr[j % 2, part, s] = u[:, s * LANES:(s + 1) * LANES]

    def conv(j, part, s, blk, scale):
        lo = part * FFN_DIM + j * nc + s * LANES
        w = cw_ref[:, lo:lo + LANES] * scale
        bias = cb_ref[:, lo:lo + LANES] * scale
        first = FFN_LEAD + blk * sub
        rows = lambda start: u_scr[j % 2, part, s, pl.ds(start, half, stride=2), :]
        before, even, odd, after = rows(first - 1), rows(first), rows(first + 1), rows(first + 2)
        return (before * w[0:1] + even * w[1:2] + odd * w[2:3] + bias,
                even * w[0:1] + odd * w[1:2] + after * w[2:3] + bias)

    def activate(j):
        for s in range(nc // LANES):
            cols = slice(j * nc + s * LANES, j * nc + (s + 1) * LANES)
            for blk in range(tl // sub):
                gate = conv(j, 0, s, blk, 1.0)
                val = conv(j, 1, s, blk, 0.5)
                for par in range(2):
                    r0 = blk * sub + par * half
                    act_scr[r0:r0 + half, cols] = (_two_gelu_tanh(gate[par]) * val[par]).astype(BF16)

    project(0)
    for j in range(N_FFN_CHUNKS):
        if j + 1 < N_FFN_CHUNKS:
            project(j + 1)
        activate(j)
    for blk in range(tl // sub):
        r0 = blk * sub
        y = _dot(act_scr[r0:r0 + sub, :], wd_ref[...])
        for s in range(y_scr.shape[0]):
            for par in range(2):
                y_scr[s, pl.ds(par, half, stride=2), :] = (
                    y[par * half:(par + 1) * half, s * LANES:(s + 1) * LANES])
        y = jnp.concatenate([y_scr[s] for s in range(y_scr.shape[0])], axis=1)
        x2 = xm_ref[r0:r0 + sub, :] + mod_ref[0, 5:6, :] * y
        o_ref[r0:r0 + sub, :] = _rms_norm(x2, fnw_ref[...])


def _ffn(x1, h2, mod3, norm_w, w_u, cw, cb, w_d, fnorm_w, seq):
    t, d = x1.shape
    tl = FFN_TILE
    tpb = seq // tl
    hb = tl // HALO
    last = t // HALO - 1
    rows = FFN_LEAD + tl + FFN_LEAD
    return pl.pallas_call(
        functools.partial(_ffn_kernel, tiles_per_seq=tpb),
        grid=(t // tl,),
        in_specs=[pl.BlockSpec((tl, d), lambda i: (i, 0)),
                  pl.BlockSpec((tl, d), lambda i: (i, 0)),
                  pl.BlockSpec((HALO, d), lambda i: (jnp.maximum(i * hb - 1, 0), 0)),
                  pl.BlockSpec((HALO, d), lambda i: (jnp.minimum((i + 1) * hb, last), 0)),
                  pl.BlockSpec((1, N_MOD, d), lambda i: (i // tpb, 0, 0)),
                  _const_spec((1, d)),
                  _const_spec(w_u.shape), _const_spec(cw.shape), _const_spec(cb.shape),
                  _const_spec(w_d.shape), _const_spec((1, d))],
        out_specs=pl.BlockSpec((tl, d), lambda i: (i, 0)),
        out_shape=jax.ShapeDtypeStruct((t, d), F32),
        scratch_shapes=[pltpu.VMEM((rows, d), BF16),
                        pltpu.VMEM((2, 2, FFN_CHUNK // LANES, rows, LANES), F32),
                        pltpu.VMEM((tl, FFN_DIM), BF16),
                        pltpu.VMEM((d // LANES, FFN_SUB, LANES), F32)],
        compiler_params=_params(("parallel",)),
        name="ffn",
    )(x1, h2, x1, x1, mod3, norm_w, w_u, cw, cb, w_d, fnorm_w)


def _rope_tables(seq):
    n_freq = QK_DIM // 4
    lane = jnp.arange(LANES)
    inv = ROPE_BASE ** (-(lane % n_freq).astype(F32) / n_freq)
    sign = jnp.where(lane < LANES // 2, -1.0, 1.0).astype(F32)

    def table(n):
        ang = jnp.arange(n, dtype=F32)[:, None] * inv[None, :]
        return jnp.stack([jnp.cos(ang), jnp.sin(ang) * sign])

    return table(seq // GRID_W), table(GRID_W)


def kernel(x, c, ctx, c_ctx, w_mod, b_mod, norm1_w, w_in, ret_decay_f, ret_decay_b,
           w_ret_out, w_four_out, w_branch_gate, b_branch_gate, w_out, norm2_w,
           w_up, conv_w, conv_b, w_down, final_norm_w):
    assert w_mod.shape[0] == 1, "single-layer block"
    b, seq, d = x.shape
    t = b * seq
    assert b == 2 and seq % RET_CHUNK == 0
    assert seq % FFN_TILE == 0 and seq % MERGE_TILE == 0 and seq % PROJ_TILE == 0

    c8 = jnp.concatenate([c, c_ctx[None, :], jnp.zeros((8 - b - 1, d), F32)], axis=0)
    mod, w_in_b, w_qk = _modulation(c8, w_mod[0], b_mod[0], w_in[0])
    mod3 = mod.reshape(8, N_MOD, d)
    n1w = norm1_w[0].reshape(1, d)

    kc, vc = _ctx_proj(ctx, mod3, n1w, w_qk, w_in_b)
    rtab, ctab = _rope_tables(seq)
    x2 = x.reshape(t, d)
    (q, k, v, sg, f), (w_bg, w_ro, w_fo, w_o) = _proj(
        x2, mod3, n1w, w_qk, w_in_b, rtab, ctab, seq,
        [w_branch_gate[0], w_ret_out[0], w_four_out[0], w_out[0]])

    a_f = jnp.broadcast_to(ret_decay_f[0][:, None, None], (HEADS, 1, LANES))
    a_b = jnp.broadcast_to(ret_decay_b[0][:, None, None], (HEADS, 1, LANES))
    y, tw, w2, w_c = _dft_first_stage(f.reshape(b, seq, F_WIDTH))
    ro, z = _retention_and_dft(a_f, a_b, q.reshape(b, seq, QK_WIDTH), k.reshape(b, seq, QK_WIDTH),
                               v.reshape(b, seq, V_WIDTH), kc, vc, y, tw, w2, w_c)

    n2w = norm2_w[0].reshape(1, d)
    (x1, h2), (w_u, w_d) = _merge(
        x2, mod3, n1w, n2w, ro.reshape(t, V_WIDTH), sg, z.reshape(t, F_WIDTH),
        w_bg, b_branch_gate[0].reshape(1, 2 * d), w_ro, w_fo, w_o, seq, [w_up[0], w_down[0]])

    out = _ffn(x1, h2, mod3, n2w, w_u, conv_w[0], conv_b[0].reshape(1, 2 * FFN_DIM), w_d,
               final_norm_w.reshape(1, d), seq)
    return out.reshape(b, seq, d)
```

```python
import functools

import numpy as np
import jax
import jax.numpy as jnp
from jax import lax
from jax.experimental import pallas as pl
from jax.experimental.pallas import tpu as pltpu

F32 = jnp.float32
BF16 = jnp.bfloat16

D_MODEL = 1024
GRID_W = 64
HEADS = 8
QK_DIM = 64
V_DIM = 128
QK_WIDTH = HEADS * QK_DIM
V_WIDTH = HEADS * V_DIM
ROPE_BASE = 10000.0
F_GROUPS = 4
F_GROUP_DIM = 128
F_WIDTH = F_GROUPS * F_GROUP_DIM
K_OFF = QK_WIDTH
V_OFF = K_OFF + QK_WIDTH
G_OFF = V_OFF + V_WIDTH
F_OFF = G_OFF + V_WIDTH
IN_COLS = F_OFF + F_WIDTH
FFN_DIM = 2816
N_MOD = 6
EPS = 1e-6

LANES = 128
RET_CHUNK = 256
RET_UNROLL = 16
FFN_CHUNK = 256
N_FFN_CHUNKS = FFN_DIM // FFN_CHUNK
FFN_TILE = 512
MERGE_TILE = 1024
MERGE_SUB = 512
PROJ_TILE = 1024
PROJ_SUB = 256
DFT_N1 = 64
BF16_ROWS = 16
DFT_ROWS = BF16_ROWS
DFT_A_ROWS = 2 * DFT_ROWS
VMEM_LIMIT = 56 * 1024 * 1024


def _params(sem):
    return pltpu.CompilerParams(dimension_semantics=sem, vmem_limit_bytes=VMEM_LIMIT)


def _dot(a, b):
    return jnp.dot(a, b, preferred_element_type=F32)


def _sigmoid(x):
    return 0.5 * jnp.tanh(0.5 * x) + 0.5


def _rms_norm(x, w):
    return x * lax.rsqrt(jnp.mean(x * x, axis=-1, keepdims=True) + EPS) * w


def _const_spec(shape):
    zeros = (0,) * len(shape)
    return pl.BlockSpec(shape, lambda *_: zeros, pipeline_mode=pl.Buffered(1))


def _cast_jobs(weights, steps):
    specs, shapes = [], []
    for w in weights:
        n_rows = w.shape[0]
        rows = -(-n_rows // steps)
        while rows % BF16_ROWS or n_rows % rows:
            rows += 1
        last = n_rows // rows - 1
        specs.append(pl.BlockSpec((rows, w.shape[1]), lambda i, last=last: (jnp.minimum(i, last), 0)))
        shapes.append(jax.ShapeDtypeStruct(w.shape, BF16))
    return specs, shapes


def _run_casts(refs):
    n = len(refs) // 2
    for src, dst in zip(refs[:n], refs[n:]):
        dst[...] = src[...].astype(BF16)


def _mod_kernel(c_ref, w_ref, b_ref, win_ref, o_ref, winb_ref, wqk_ref):
    c = c_ref[...]
    s = c * jax.nn.sigmoid(c)
    o_ref[...] = _dot(s.astype(BF16), w_ref[...].astype(BF16)) + b_ref[...]
    wb = win_ref[...].astype(BF16)
    winb_ref[...] = wb
    n_qk = 2 * QK_WIDTH
    src = lax.broadcasted_iota(jnp.int32, (n_qk, n_qk), 0)
    dst = lax.broadcasted_iota(jnp.int32, (n_qk, n_qk), 1)
    select = (src == _qk_source_column(dst)).astype(BF16)
    wqk_ref[...] = _dot(wb[:, :n_qk], select).astype(BF16)


def _qk_source_column(col):
    half = QK_DIM // 2
    lane = col % LANES
    head = 2 * ((col % QK_WIDTH) // LANES) + (lane // half) % 2
    return (col // QK_WIDTH) * QK_WIDTH + head * QK_DIM + (lane // (2 * half)) * half + lane % half


def _modulation(c8, w_mod, b_mod, w_in):
    n = w_mod.shape[1]
    steps = 4
    tn = n // steps
    d, n_in = w_in.shape
    rows = d // steps
    return pl.pallas_call(
        _mod_kernel,
        grid=(steps,),
        in_specs=[_const_spec((8, D_MODEL)),
                  pl.BlockSpec((D_MODEL, tn), lambda j: (0, j)),
                  pl.BlockSpec((1, tn), lambda j: (0, j)),
                  pl.BlockSpec((rows, n_in), lambda j: (j, 0))],
        out_specs=[pl.BlockSpec((8, tn), lambda j: (0, j)),
                   pl.BlockSpec((rows, n_in), lambda j: (j, 0)),
                   pl.BlockSpec((rows, 2 * QK_WIDTH), lambda j: (j, 0))],
        out_shape=[jax.ShapeDtypeStruct((8, n), F32),
                   jax.ShapeDtypeStruct((d, n_in), BF16),
                   jax.ShapeDtypeStruct((d, 2 * QK_WIDTH), BF16)],
        compiler_params=_params(("parallel",)),
        name="mod",
    )(c8, w_mod, b_mod.reshape(1, n), w_in)


def _ctx_kernel(x_ref, mod_ref, nw_ref, wk_ref, wv_ref, k_ref, v_ref):
    x = x_ref[0]
    h = _rms_norm(x, nw_ref[...]) * (1.0 + mod_ref[0, 1:2, :]) + mod_ref[0, 0:1, :]
    hb = h.astype(BF16)
    k_ref[0] = _dot(hb, wk_ref[...]).astype(BF16)
    v_ref[0] = _dot(hb, wv_ref[...]).astype(BF16)


def _ctx_proj(ctx, mod3, norm_w, w_qk, w_in):
    b, lc, d = ctx.shape
    assert V_OFF % V_WIDTH == 0
    return pl.pallas_call(
        _ctx_kernel,
        grid=(b,),
        in_specs=[pl.BlockSpec((1, lc, d), lambda i: (i, 0, 0)),
                  pl.BlockSpec((1, N_MOD, d), lambda i: (2, 0, 0)),
                  _const_spec((1, d)),
                  pl.BlockSpec((d, QK_WIDTH), lambda i: (0, 1), pipeline_mode=pl.Buffered(1)),
                  pl.BlockSpec((d, V_WIDTH), lambda i: (0, V_OFF // V_WIDTH),
                               pipeline_mode=pl.Buffered(1))],
        out_specs=[pl.BlockSpec((1, lc, QK_WIDTH), lambda i: (i, 0, 0)),
                   pl.BlockSpec((1, lc, V_WIDTH), lambda i: (i, 0, 0))],
        out_shape=[jax.ShapeDtypeStruct((b, lc, QK_WIDTH), BF16),
                   jax.ShapeDtypeStruct((b, lc, V_WIDTH), BF16)],
        compiler_params=_params(("parallel",)),
        name="ctx_proj",
    )(ctx, mod3, norm_w, w_qk, w_in)


def _proj_kernel(x_ref, mod_ref, nw_ref, wqk_ref, w_ref, rtab_ref, ctab_ref, *refs, n_cast):
    q_ref, k_ref, v_ref, sg_ref, f_ref = refs[n_cast:n_cast + 5]
    _run_casts(refs[:n_cast] + refs[n_cast + 5:])
    lane = lax.broadcasted_iota(jnp.int32, (1, LANES), 1)
    by_row = (lane % (QK_DIM // 2)) < QK_DIM // 4

    def rope(t, trig, scale, out_ref, rows):
        for j in range(QK_WIDTH // LANES):
            tj = t[:, j * LANES:(j + 1) * LANES]
            r = tj * trig[0] + pltpu.roll(tj, LANES // 2, 1) * trig[1]
            out_ref[rows, j * LANES:(j + 1) * LANES] = (r * scale).astype(BF16)

    for r0 in range(0, x_ref.shape[0], PROJ_SUB):
        rows = slice(r0, r0 + PROJ_SUB)
        trig = []
        for cs in range(2):
            trig.append(jnp.concatenate(
                [jnp.where(by_row, rtab_ref[cs, g:g + 1, :], ctab_ref[cs])
                 for g in range(r0 // GRID_W, (r0 + PROJ_SUB) // GRID_W)], axis=0))
        x = x_ref[rows, :]
        h = _rms_norm(x, nw_ref[...]) * (1.0 + mod_ref[0, 1:2, :]) + mod_ref[0, 0:1, :]
        hb = h.astype(BF16)
        rope(_dot(hb, wqk_ref[:, :QK_WIDTH]), trig, QK_DIM ** -0.5, q_ref, rows)
        rope(_dot(hb, wqk_ref[:, QK_WIDTH:]), trig, 1.0, k_ref, rows)
        v_ref[rows, :] = _dot(hb, w_ref[:, V_OFF:G_OFF]).astype(BF16)
        g = _dot(hb, w_ref[:, G_OFF:F_OFF])
        sg_ref[rows, :] = (g * _sigmoid(g)).astype(BF16)
        f_ref[rows, :] = _dot(hb, w_ref[:, F_OFF:IN_COLS]).astype(BF16)


def _proj(x2, mod3, norm_w, w_qk, w_in, rtab, ctab, seq, to_cast):
    t, d = x2.shape
    tl = PROJ_TILE
    tpb = seq // tl
    tok = lambda w: pl.BlockSpec((tl, w), lambda i: (i, 0))
    cast_specs, cast_shapes = _cast_jobs(to_cast, t // tl)
    out = pl.pallas_call(
        functools.partial(_proj_kernel, n_cast=len(to_cast)),
        grid=(t // tl,),
        in_specs=[tok(d),
                  pl.BlockSpec((1, N_MOD, d), lambda i: (i // tpb, 0, 0)),
                  _const_spec((1, d)),
                  _const_spec(w_qk.shape), _const_spec(w_in.shape),
                  pl.BlockSpec((2, tl // GRID_W, LANES), lambda i: (0, i % tpb, 0)),
                  _const_spec(ctab.shape)] + cast_specs,
        out_specs=[tok(QK_WIDTH), tok(QK_WIDTH), tok(V_WIDTH), tok(V_WIDTH),
                   tok(F_WIDTH)] + cast_specs,
        out_shape=[jax.ShapeDtypeStruct((t, QK_WIDTH), BF16),
                   jax.ShapeDtypeStruct((t, QK_WIDTH), BF16),
                   jax.ShapeDtypeStruct((t, V_WIDTH), BF16),
                   jax.ShapeDtypeStruct((t, V_WIDTH), BF16),
                   jax.ShapeDtypeStruct((t, F_WIDTH), BF16)] + cast_shapes,
        compiler_params=_params(("arbitrary",)),
        name="proj",
    )(x2, mod3, norm_w, w_qk, w_in, rtab, ctab, *to_cast)
    return out[:5], out[5:]


def _retention_kernel(af_ref, ab_ref, q_ref, k_ref, v_ref, kc_ref, vc_ref,
                      tw_ref, w2_ref, cc_ref, y_ref,
                      o_ref, z_ref, ds_scr, st_scr, d_scr, tab_scr, z_scr):
    c = RET_CHUNK
    seq = q_ref.shape[1]
    lc = kc_ref.shape[1]
    n = seq // c
    lgf = [-jnp.exp(af_ref[hh]) for hh in range(2)]
    lgb = [-jnp.exp(ab_ref[hh]) for hh in range(2)]
    half = QK_DIM // 2

    lane_head = (lax.broadcasted_iota(jnp.int32, (1, LANES), 1) // half) % 2
    masks = [(lane_head == hh).astype(BF16) for hh in range(2)]
    lgf_lane = jnp.where(lane_head == 0, lgf[0], lgf[1])
    lgb_lane = jnp.where(lane_head == 0, lgb[0], lgb[1])
    row_head = (lax.broadcasted_iota(jnp.int32, (LANES, 1), 0) // half) % 2
    lgf_row = jnp.where(row_head == 0, lgf[0][:, 0:1], lgf[1][:, 0:1])
    lgb_row = jnp.where(row_head == 0, lgb[0][:, 0:1], lgb[1][:, 0:1])

    pos = lax.broadcasted_iota(jnp.int32, (c, LANES), 0).astype(F32)
    tab_scr[0] = jnp.exp(lgf_lane * (pos + 1.0))
    tab_scr[1] = jnp.exp(lgb_lane * (c - pos))
    diff = (lax.broadcasted_iota(jnp.int32, (c, c), 0)
            - lax.broadcasted_iota(jnp.int32, (c, c), 1)).astype(F32)
    for hh in range(2):
        d_scr[hh] = (jnp.where(diff >= 0, jnp.exp(lgf[hh][:, 0:1] * jnp.maximum(diff, 0.0)), 0.0)
                     + jnp.where(diff <= 0, jnp.exp(lgb[hh][:, 0:1] * jnp.maximum(-diff, 0.0)), 0.0))

    def k_decays(tokens):
        t = lax.broadcasted_iota(jnp.int32, (1, tokens), 1).astype(F32)
        return jnp.exp(lgf_row * (tokens - 1.0 - t)), jnp.exp(lgb_row * t)

    def state_increment(k_rows, v_rows, decays):
        kt = jnp.transpose(k_rows.astype(F32))
        lhs = jnp.concatenate([kt * decays[0], kt * decays[1]], axis=0).astype(BF16)
        return _dot(lhs, v_rows)

    kdec = k_decays(c)

    def incr(i, carry):
        rows = pl.ds(pl.multiple_of(i * c, c), c)
        ds_scr[i] = state_increment(k_ref[0, rows, :], v_ref[0, rows, :], kdec)
        return carry

    lax.fori_loop(0, n, incr, 0, unroll=RET_UNROLL)

    s0 = state_increment(kc_ref[0], vc_ref[0], k_decays(lc))
    col_head = lax.broadcasted_iota(jnp.int32, (1, 2 * V_DIM), 1) // V_DIM
    own = (row_head == col_head).astype(F32)
    decay_f = jnp.exp(lgf_row * c) * own
    decay_b = jnp.exp(lgb_row * c) * own

    def scan_f(i, s):
        st_scr[i, 0:LANES, :] = (s * own).astype(BF16)
        return decay_f * s + ds_scr[i, 0:LANES, :]

    lax.fori_loop(0, n, scan_f, s0[0:LANES])

    def scan_b(t, s):
        i = n - 1 - t
        st_scr[i, LANES:2 * LANES, :] = (s * own).astype(BF16)
        return decay_b * s + ds_scr[i, LANES:2 * LANES, :]

    lax.fori_loop(0, n, scan_b, s0[LANES:2 * LANES])

    def outputs(i):
        rows = pl.ds(pl.multiple_of(i * c, c), c)
        q = q_ref[0, rows, :]
        k = k_ref[0, rows, :]
        qf = q.astype(F32)
        qd = jnp.concatenate([qf * tab_scr[0], qf * tab_scr[1]], axis=1).astype(BF16)
        inter = _dot(qd, st_scr[i])
        for hh in range(2):
            cols = slice(hh * V_DIM, (hh + 1) * V_DIM)
            scores = lax.dot_general(q * masks[hh], k, (((1,), (1,)), ((), ())),
                                     preferred_element_type=F32)
            o = _dot((scores * d_scr[hh]).astype(BF16), v_ref[0, rows, cols]) + inter[:, cols]
            o_ref[0, rows, cols] = o.astype(BF16)

    per_k1 = n // DFT_ROWS
    cc = cc_ref[...].astype(BF16)

    def outputs_and_dft(j, carry):
        for u in range(per_k1):
            outputs(j * per_k1 + u)
        _dft_second_stage(j, tw_ref, w2_ref, cc, y_ref, z_scr)
        return carry

    lax.fori_loop(0, DFT_ROWS, outputs_and_dft, 0, unroll=RET_UNROLL // per_k1)
    _dft_emit(z_scr, z_ref)


def _retention_and_dft(a_f, a_b, q, k, v, kc, vc, y, tw, w2, w_c):
    b, seq, _ = v.shape
    lc = kc.shape[1]
    c = RET_CHUNK
    n = seq // c
    r = DFT_ROWS
    _, _, n1, n2, w = y.shape
    assert n % r == 0 and n1 // r == HEADS // 2
    dec = pl.BlockSpec((2, 1, LANES), lambda i, p: (p, 0, 0))
    qk = lambda rows: pl.BlockSpec((1, rows, LANES), lambda i, p: (i, 0, p))
    vv = lambda rows: pl.BlockSpec((1, rows, 2 * V_DIM), lambda i, p: (i, 0, p))
    return pl.pallas_call(
        _retention_kernel,
        grid=(b, HEADS // 2),
        in_specs=[dec, dec, qk(seq), qk(seq), vv(seq), qk(lc), vv(lc),
                  pl.BlockSpec((r, 2, n2), lambda i, p: (p, 0, 0)),
                  _const_spec((2, n2, n2)),
                  _const_spec((2 * F_GROUP_DIM, F_GROUP_DIM)),
                  pl.BlockSpec((1, 2, r, n2, w), lambda i, p: (i, 0, p, 0, 0))],
        out_specs=[vv(seq), pl.BlockSpec((1, n2, r, w), lambda i, p: (i, 0, p, 0))],
        out_shape=[jax.ShapeDtypeStruct((b, seq, V_WIDTH), BF16),
                   jax.ShapeDtypeStruct((b, n2, n1, w), BF16)],
        scratch_shapes=[pltpu.VMEM((n, 2 * LANES, 2 * V_DIM), F32),
                        pltpu.VMEM((n, 2 * LANES, 2 * V_DIM), BF16),
                        pltpu.VMEM((2, c, c), F32),
                        pltpu.VMEM((2, c, LANES), F32),
                        pltpu.VMEM((F_GROUPS, n2 * r, F_GROUP_DIM), F32)],
        compiler_params=_params(("parallel", "parallel")),
        name="retention_dft",
    )(a_f, a_b, q, k, v, kc, vc, tw, w2, w_c, y)


def _dft_a_kernel(w_ref, x_ref, y_ref):
    _, n1, rows, w = x_ref.shape
    r = DFT_ROWS
    nh = n1 // 2 + 1
    for r0 in range(0, rows, r):
        x = x_ref[0, :, r0:r0 + r, :].reshape(n1 * r, w)
        y = _dot(w_ref[...], x).reshape(2, nh, r, w).astype(BF16)
        y_ref[0, :, 0:nh, r0:r0 + r, :] = y
        for k1 in range(1, n1 // 2):
            y_ref[0, 0, n1 - k1, r0:r0 + r, :] = y[0, k1]
            y_ref[0, 1, n1 - k1, r0:r0 + r, :] = -y[1, k1]


def _dft_second_stage(j, tw_ref, w2_ref, cc, y_ref, z_scr):
    n2 = y_ref.shape[3]
    w2c = w2_ref[0]
    w2s = w2_ref[1]
    tc = tw_ref[j, 0:1, :]
    ts = tw_ref[j, 1:2, :]
    ec = w2c * tc - w2s * ts
    es = w2s * tc + w2c * ts
    m = jnp.concatenate([jnp.concatenate([ec, -es], axis=1),
                         jnp.concatenate([es, ec], axis=1)], axis=0).astype(BF16)
    y = jnp.concatenate([y_ref[0, 0, j], y_ref[0, 1, j]], axis=0)
    zz = _dot(m, y)
    for g in range(F_GROUPS):
        cols = slice(g * F_GROUP_DIM, (g + 1) * F_GROUP_DIM)
        zcs = jnp.concatenate([zz[:n2, cols], zz[n2:, cols]], axis=1).astype(BF16)
        z_scr[g, pl.ds(j, n2, stride=DFT_ROWS), :] = _dot(zcs, cc)


def _dft_emit(z_scr, z_ref):
    n2 = z_ref.shape[1]
    z = jnp.concatenate([z_scr[g] for g in range(F_GROUPS)], axis=1)
    z_ref[0] = z.reshape(n2, DFT_ROWS, F_WIDTH).astype(BF16)


def _dft_tables(seq):
    n1 = DFT_N1
    n2 = seq // n1

    def cs(num, den):
        ang = 2.0 * np.pi * (num % den) / den
        return np.cos(ang), np.sin(ang)

    a = np.arange(n1)
    w_a = np.concatenate(cs(a[:n1 // 2 + 1, None] * a[None, :], n1), axis=0)
    m = np.arange(n2)
    tw = np.stack(cs(a[:, None] * m[None, :], seq), axis=1)
    w2 = np.stack(cs(m[:, None] * m[None, :], n2), axis=0)
    ch = np.arange(F_GROUP_DIM)
    cc, sc = cs(ch[:, None] * ch[None, :], F_GROUP_DIM)
    scale = 1.0 / np.sqrt(seq * F_GROUP_DIM)
    w_c = np.concatenate([cc, -sc], axis=0) * scale
    return [jnp.asarray(t, dtype=F32) for t in (w_a, tw, w2, w_c)]


def _dft_first_stage(f):
    b, seq, w = f.shape
    n1 = DFT_N1
    n2 = seq // n1
    w_a, tw, w2, w_c = _dft_tables(seq)
    r = DFT_ROWS
    spread = (jnp.arange(n1 * r)[None, :] // r == jnp.arange(n1)[:, None]).astype(F32)
    w_rep = jnp.dot(w_a, spread, precision=lax.Precision.HIGHEST)
    same_r = jnp.arange(w_a.shape[0] * r)[:, None] % r == jnp.arange(n1 * r)[None, :] % r
    w_a = jnp.where(same_r, jnp.repeat(w_rep, r, axis=0), 0.0).astype(BF16)
    y = pl.pallas_call(
        _dft_a_kernel,
        grid=(b, n2 // DFT_A_ROWS),
        in_specs=[_const_spec(w_a.shape),
                  pl.BlockSpec((1, n1, DFT_A_ROWS, w), lambda i, j: (i, 0, j, 0))],
        out_specs=pl.BlockSpec((1, 2, n1, DFT_A_ROWS, w), lambda i, j: (i, 0, 0, j, 0)),
        out_shape=jax.ShapeDtypeStruct((b, 2, n1, n2, w), BF16),
        compiler_params=_params(("parallel", "parallel")),
        name="dft_a",
    )(w_a, f.reshape(b, n1, n2, w))
    return y, tw, w2, w_c


def _merge_kernel(x_ref, mod_ref, nw_ref, n2w_ref, ro_ref, sg_ref, z_ref, wbg_ref, bbg_ref,
                  wro_ref, wfo_ref, wout_ref, *refs, n_cast):
    o_ref, h2_ref = refs[n_cast:n_cast + 2]
    _run_casts(refs[:n_cast] + refs[n_cast + 2:])
    for r0 in range(0, x_ref.shape[0], MERGE_SUB):
        rows = slice(r0, r0 + MERGE_SUB)
        x = x_ref[rows, :]
        h = _rms_norm(x, nw_ref[...]) * (1.0 + mod_ref[0, 1:2, :]) + mod_ref[0, 0:1, :]
        gates = _sigmoid(_dot(h.astype(BF16), wbg_ref[...]) + bbg_ref[...])
        gated = []
        for hd in range(HEADS):
            cols = slice(hd * V_DIM, (hd + 1) * V_DIM)
            o = ro_ref[rows, cols].astype(F32)
            oc = o - jnp.mean(o, axis=-1, keepdims=True)
            var = jnp.mean(oc * oc, axis=-1, keepdims=True)
            gated.append((oc * lax.rsqrt(var + EPS) * sg_ref[rows, cols].astype(F32)).astype(BF16))
        ret_d = _dot(jnp.concatenate(gated, axis=1), wro_ref[...])
        four_d = _dot(z_ref[rows, :], wfo_ref[...])
        m = gates[:, :D_MODEL] * ret_d + gates[:, D_MODEL:] * four_d
        y = _dot(m.astype(BF16), wout_ref[...])
        x1 = x + mod_ref[0, 2:3, :] * y
        o_ref[rows, :] = x1
        h2 = _rms_norm(x1, n2w_ref[...]) * (1.0 + mod_ref[0, 4:5, :]) + mod_ref[0, 3:4, :]
        h2_ref[rows, :] = h2.astype(BF16)


def _merge(x2, mod3, norm_w, norm2_w, ro, sg, z, w_bg, b_bg, w_ro, w_fo, w_out, seq, to_cast):
    t, d = x2.shape
    tl = MERGE_TILE
    tpb = seq // tl
    tok = lambda w: pl.BlockSpec((tl, w), lambda i: (i, 0))
    cast_specs, cast_shapes = _cast_jobs(to_cast, t // tl)
    out = pl.pallas_call(
        functools.partial(_merge_kernel, n_cast=len(to_cast)),
        grid=(t // tl,),
        in_specs=[tok(d),
                  pl.BlockSpec((1, N_MOD, d), lambda i: (i // tpb, 0, 0)),
                  _const_spec((1, d)), _const_spec((1, d)),
                  tok(V_WIDTH), tok(V_WIDTH), tok(F_WIDTH),
                  _const_spec(w_bg.shape), _const_spec((1, 2 * d)),
                  _const_spec(w_ro.shape), _const_spec(w_fo.shape),
                  _const_spec(w_out.shape)] + cast_specs,
        out_specs=[tok(d), tok(d)] + cast_specs,
        out_shape=[jax.ShapeDtypeStruct((t, d), F32),
                   jax.ShapeDtypeStruct((t, d), BF16)] + cast_shapes,
        compiler_params=_params(("arbitrary",)),
        name="merge",
    )(x2, mod3, norm_w, norm2_w, ro, sg, z, w_bg, b_bg, w_ro, w_fo, w_out, *to_cast)
    return out[:2], out[2:]


HALO = 8
FFN_LEAD = 16
FFN_SUB = 256


def _two_gelu_tanh(x):
    c1 = np.sqrt(2.0 / np.pi)
    return x + x * jnp.tanh(x * (c1 + (c1 * 0.044715) * (x * x)))


def _ffn_kernel(xm_ref, hm_ref, xp_ref, xn_ref, mod_ref, nw_ref, wu_ref, cw_ref, cb_ref,
                wd_ref, fnw_ref, o_ref, h_scr, u_scr, act_scr, y_scr, *, tiles_per_seq):
    tl = xm_ref.shape[0]
    sub = FFN_SUB
    half = sub // 2
    nc = FFN_CHUNK
    i = pl.program_id(0)
    keep_prev = ((i % tiles_per_seq) != 0).astype(F32)
    keep_next = ((i % tiles_per_seq) != tiles_per_seq - 1).astype(F32)

    def pre(x):
        return _rms_norm(x, nw_ref[...]) * (1.0 + mod_ref[0, 4:5, :]) + mod_ref[0, 3:4, :]

    pad = jnp.zeros((FFN_LEAD - HALO, xm_ref.shape[1]), F32)
    h_scr[0:FFN_LEAD] = jnp.concatenate([pad, pre(xp_ref[...]) * keep_prev], axis=0).astype(BF16)
    h_scr[FFN_LEAD:FFN_LEAD + tl] = hm_ref[...]
    h_scr[FFN_LEAD + tl:] = jnp.concatenate([pre(xn_ref[...]) * keep_next, pad], axis=0).astype(BF16)

    def project(j):
        hb = h_scr[...]
        for part in range(2):
            lo = part * FFN_DIM + j * nc
            u = _dot(hb, wu_ref[:, lo:lo + nc])
            for s in range(nc // LANES):
                u_scr[j % 2, part, s] = u[:, s * LANES:(s + 1) * LANES]

    def conv(j, part, s, blk, scale):
        lo = part * FFN_DIM + j * nc + s * LANES
        w = cw_ref[:, lo:lo + LANES] * scale
        bias = cb_ref[:, lo:lo + LANES] * scale
        first = FFN_LEAD + blk * sub
        rows = lambda start: u_scr[j % 2, part, s, pl.ds(start, half, stride=2), :]
        before, even, odd, after = rows(first - 1), rows(first), rows(first + 1), rows(first + 2)
        return (before * w[0:1] + even * w[1:2] + odd * w[2:3] + bias,
                even * w[0:1] + odd * w[1:2] + after * w[2:3] + bias)

    def activate(j):
        for s in range(nc // LANES):
            cols = slice(j * nc + s * LANES, j * nc + (s + 1) * LANES)
            for blk in range(tl // sub):
                gate = conv(j, 0, s, blk, 1.0)
                val = conv(j, 1, s, blk, 0.5)
                for par in range(2):
                    r0 = blk * sub + par * half
                    act_scr[r0:r0 + half, cols] = (_two_gelu_tanh(gate[par]) * val[par]).astype(BF16)

    project(0)
    for j in range(N_FFN_CHUNKS):
        if j + 1 < N_FFN_CHUNKS:
            project(j + 1)
        activate(j)
    for blk in range(tl // sub):
        r0 = blk * sub
        y = _dot(act_scr[r0:r0 + sub, :], wd_ref[...])
        for s in range(y_scr.shape[0]):
            for par in range(2):
                y_scr[s, pl.ds(par, half, stride=2), :] = (
                    y[par * half:(par + 1) * half, s * LANES:(s + 1) * LANES])
        y = jnp.concatenate([y_scr[s] for s in range(y_scr.shape[0])], axis=1)
        x2 = xm_ref[r0:r0 + sub, :] + mod_ref[0, 5:6, :] * y
        o_ref[r0:r0 + sub, :] = _rms_norm(x2, fnw_ref[...])


def _ffn(x1, h2, mod3, norm_w, w_u, cw, cb, w_d, fnorm_w, seq):
    t, d = x1.shape
    tl = FFN_TILE
    tpb = seq // tl
    hb = tl // HALO
    last = t // HALO - 1
    rows = FFN_LEAD + tl + FFN_LEAD
    return pl.pallas_call(
        functools.partial(_ffn_kernel, tiles_per_seq=tpb),
        grid=(t // tl,),
        in_specs=[pl.BlockSpec((tl, d), lambda i: (i, 0)),
                  pl.BlockSpec((tl, d), lambda i: (i, 0)),
                  pl.BlockSpec((HALO, d), lambda i: (jnp.maximum(i * hb - 1, 0), 0)),
                  pl.BlockSpec((HALO, d), lambda i: (jnp.minimum((i + 1) * hb, last), 0)),
                  pl.BlockSpec((1, N_MOD, d), lambda i: (i // tpb, 0, 0)),
                  _const_spec((1, d)),
                  _const_spec(w_u.shape), _const_spec(cw.shape), _const_spec(cb.shape),
                  _const_spec(w_d.shape), _const_spec((1, d))],
        out_specs=pl.BlockSpec((tl, d), lambda i: (i, 0)),
        out_shape=jax.ShapeDtypeStruct((t, d), F32),
        scratch_shapes=[pltpu.VMEM((rows, d), BF16),
                        pltpu.VMEM((2, 2, FFN_CHUNK // LANES, rows, LANES), F32),
                        pltpu.VMEM((tl, FFN_DIM), BF16),
                        pltpu.VMEM((d // LANES, FFN_SUB, LANES), F32)],
        compiler_params=_params(("parallel",)),
        name="ffn",
    )(x1, h2, x1, x1, mod3, norm_w, w_u, cw, cb, w_d, fnorm_w)


def _rope_tables(seq):
    n_freq = QK_DIM // 4
    lane = jnp.arange(LANES)
    inv = ROPE_BASE ** (-(lane % n_freq).astype(F32) / n_freq)
    sign = jnp.where(lane < LANES // 2, -1.0, 1.0).astype(F32)

    def table(n):
        ang = jnp.arange(n, dtype=F32)[:, None] * inv[None, :]
        return jnp.stack([jnp.cos(ang), jnp.sin(ang) * sign])

    return table(seq // GRID_W), table(GRID_W)


def kernel(x, c, ctx, c_ctx, w_mod, b_mod, norm1_w, w_in, ret_decay_f, ret_decay_b,
           w_ret_out, w_four_out, w_branch_gate, b_branch_gate, w_out, norm2_w,
           w_up, conv_w, conv_b, w_down, final_norm_w):
    assert w_mod.shape[0] == 1, "single-layer block"
    b, seq, d = x.shape
    t = b * seq
    assert b == 2 and seq % RET_CHUNK == 0
    assert seq % FFN_TILE == 0 and seq % MERGE_TILE == 0 and seq % PROJ_TILE == 0

    c8 = jnp.concatenate([c, c_ctx[None, :], jnp.zeros((8 - b - 1, d), F32)], axis=0)
    mod, w_in_b, w_qk = _modulation(c8, w_mod[0], b_mod[0], w_in[0])
    mod3 = mod.reshape(8, N_MOD, d)
    n1w = norm1_w[0].reshape(1, d)

    kc, vc = _ctx_proj(ctx, mod3, n1w, w_qk, w_in_b)
    rtab, ctab = _rope_tables(seq)
    x2 = x.reshape(t, d)
    (q, k, v, sg, f), (w_bg, w_ro, w_fo, w_o) = _proj(
        x2, mod3, n1w, w_qk, w_in_b, rtab, ctab, seq,
        [w_branch_gate[0], w_ret_out[0], w_four_out[0], w_out[0]])

    a_f = jnp.broadcast_to(ret_decay_f[0][:, None, None], (HEADS, 1, LANES))
    a_b = jnp.broadcast_to(ret_decay_b[0][:, None, None], (HEADS, 1, LANES))
    y, tw, w2, w_c = _dft_first_stage(f.reshape(b, seq, F_WIDTH))
    ro, z = _retention_and_dft(a_f, a_b, q.reshape(b, seq, QK_WIDTH), k.reshape(b, seq, QK_WIDTH),
                               v.reshape(b, seq, V_WIDTH), kc, vc, y, tw, w2, w_c)

    n2w = norm2_w[0].reshape(1, d)
    (x1, h2), (w_u, w_d) = _merge(
        x2, mod3, n1w, n2w, ro.reshape(t, V_WIDTH), sg, z.reshape(t, F_WIDTH),
        w_bg, b_branch_gate[0].reshape(1, 2 * d), w_ro, w_fo, w_o, seq, [w_up[0], w_down[0]])

    out = _ffn(x1, h2, mod3, n2w, w_u, conv_w[0], conv_b[0].reshape(1, 2 * FFN_DIM), w_d,
               final_norm_w.reshape(1, d), seq)
    return out.reshape(b, seq, d)
```

```python
import functools

import numpy as np
import jax
import jax.numpy as jnp
from jax import lax
from jax.experimental import pallas as pl
from jax.experimental.pallas import tpu as pltpu

F32 = jnp.float32
BF16 = jnp.bfloat16

D_MODEL = 1024
GRID_W = 64
HEADS = 8
QK_DIM = 64
V_DIM = 128
QK_WIDTH = HEADS * QK_DIM
V_WIDTH = HEADS * V_DIM
ROPE_BASE = 10000.0
F_GROUPS = 4
F_GROUP_DIM = 128
F_WIDTH = F_GROUPS * F_GROUP_DIM
K_OFF = QK_WIDTH
V_OFF = K_OFF + QK_WIDTH
G_OFF = V_OFF + V_WIDTH
F_OFF = G_OFF + V_WIDTH
IN_COLS = F_OFF + F_WIDTH
FFN_DIM = 2816
N_MOD = 6
EPS = 1e-6

LANES = 128
RET_CHUNK = 256
RET_UNROLL = 16
FFN_CHUNK = 256
N_FFN_CHUNKS = FFN_DIM // FFN_CHUNK
FFN_TILE = 512
MERGE_TILE = 1024
MERGE_SUB = 512
PROJ_TILE = 1024
PROJ_SUB = 256
DFT_N1 = 64
BF16_ROWS = 16
DFT_ROWS = BF16_ROWS
DFT_A_ROWS = 2 * DFT_ROWS
VMEM_LIMIT = 56 * 1024 * 1024


def _params(sem):
    return pltpu.CompilerParams(dimension_semantics=sem, vmem_limit_bytes=VMEM_LIMIT)


def _dot(a, b):
    return jnp.dot(a, b, preferred_element_type=F32)


def _sigmoid(x):
    return 0.5 * jnp.tanh(0.5 * x) + 0.5


def _rms_norm(x, w):
    return x * lax.rsqrt(jnp.mean(x * x, axis=-1, keepdims=True) + EPS) * w


def _mod_norm(x, w, shift, scale):
    return x * lax.rsqrt(jnp.mean(x * x, axis=-1, keepdims=True) + EPS) * (w * (1.0 + scale)) + shift


def _const_spec(shape):
    zeros = (0,) * len(shape)
    return pl.BlockSpec(shape, lambda *_: zeros, pipeline_mode=pl.Buffered(1))


def _cast_jobs(weights, steps):
    specs, shapes = [], []
    for w in weights:
        n_rows = w.shape[0]
        rows = -(-n_rows // steps)
        while rows % BF16_ROWS or n_rows % rows:
            rows += 1
        last = n_rows // rows - 1
        specs.append(pl.BlockSpec((rows, w.shape[1]), lambda i, last=last: (jnp.minimum(i, last), 0)))
        shapes.append(jax.ShapeDtypeStruct(w.shape, BF16))
    return specs, shapes


def _run_casts(refs):
    n = len(refs) // 2
    for src, dst in zip(refs[:n], refs[n:]):
        dst[...] = src[...].astype(BF16)


def _mod_kernel(c_ref, w_ref, b_ref, win_ref, o_ref, winb_ref, wqk_ref):
    c = c_ref[...]
    s = c * jax.nn.sigmoid(c)
    o_ref[...] = _dot(s.astype(BF16), w_ref[...].astype(BF16)) + b_ref[...]
    wb = win_ref[...].astype(BF16)
    winb_ref[...] = wb
    n_qk = 2 * QK_WIDTH
    src = lax.broadcasted_iota(jnp.int32, (n_qk, n_qk), 0)
    dst = lax.broadcasted_iota(jnp.int32, (n_qk, n_qk), 1)
    select = (src == _qk_source_column(dst)).astype(BF16)
    wqk_ref[...] = _dot(wb[:, :n_qk], select).astype(BF16)


def _qk_source_column(col):
    half = QK_DIM // 2
    lane = col % LANES
    head = 2 * ((col % QK_WIDTH) // LANES) + (lane // half) % 2
    return (col // QK_WIDTH) * QK_WIDTH + head * QK_DIM + (lane // (2 * half)) * half + lane % half


def _modulation(c8, w_mod, b_mod, w_in):
    n = w_mod.shape[1]
    steps = 4
    tn = n // steps
    d, n_in = w_in.shape
    rows = d // steps
    return pl.pallas_call(
        _mod_kernel,
        grid=(steps,),
        in_specs=[_const_spec((8, D_MODEL)),
                  pl.BlockSpec((D_MODEL, tn), lambda j: (0, j)),
                  pl.BlockSpec((1, tn), lambda j: (0, j)),
                  pl.BlockSpec((rows, n_in), lambda j: (j, 0))],
        out_specs=[pl.BlockSpec((8, tn), lambda j: (0, j)),
                   pl.BlockSpec((rows, n_in), lambda j: (j, 0)),
                   pl.BlockSpec((rows, 2 * QK_WIDTH), lambda j: (j, 0))],
        out_shape=[jax.ShapeDtypeStruct((8, n), F32),
                   jax.ShapeDtypeStruct((d, n_in), BF16),
                   jax.ShapeDtypeStruct((d, 2 * QK_WIDTH), BF16)],
        compiler_params=_params(("parallel",)),
        name="mod",
    )(c8, w_mod, b_mod.reshape(1, n), w_in)


def _ctx_kernel(x_ref, mod_ref, nw_ref, wk_ref, wv_ref, k_ref, v_ref):
    x = x_ref[0]
    h = _mod_norm(x, nw_ref[...], mod_ref[0, 0:1, :], mod_ref[0, 1:2, :])
    hb = h.astype(BF16)
    k_ref[0] = _dot(hb, wk_ref[...]).astype(BF16)
    v_ref[0] = _dot(hb, wv_ref[...]).astype(BF16)


def _ctx_proj(ctx, mod3, norm_w, w_qk, w_in):
    b, lc, d = ctx.shape
    assert V_OFF % V_WIDTH == 0
    return pl.pallas_call(
        _ctx_kernel,
        grid=(b,),
        in_specs=[pl.BlockSpec((1, lc, d), lambda i: (i, 0, 0)),
                  pl.BlockSpec((1, N_MOD, d), lambda i: (2, 0, 0)),
                  _const_spec((1, d)),
                  pl.BlockSpec((d, QK_WIDTH), lambda i: (0, 1), pipeline_mode=pl.Buffered(1)),
                  pl.BlockSpec((d, V_WIDTH), lambda i: (0, V_OFF // V_WIDTH),
                               pipeline_mode=pl.Buffered(1))],
        out_specs=[pl.BlockSpec((1, lc, QK_WIDTH), lambda i: (i, 0, 0)),
                   pl.BlockSpec((1, lc, V_WIDTH), lambda i: (i, 0, 0))],
        out_shape=[jax.ShapeDtypeStruct((b, lc, QK_WIDTH), BF16),
                   jax.ShapeDtypeStruct((b, lc, V_WIDTH), BF16)],
        compiler_params=_params(("parallel",)),
        name="ctx_proj",
    )(ctx, mod3, norm_w, w_qk, w_in)


def _proj_kernel(x_ref, mod_ref, nw_ref, wqk_ref, w_ref, rtab_ref, ctab_ref, *refs, n_cast):
    q_ref, k_ref, v_ref, sg_ref, f_ref = refs[n_cast:n_cast + 5]
    _run_casts(refs[:n_cast] + refs[n_cast + 5:])
    lane = lax.broadcasted_iota(jnp.int32, (1, LANES), 1)
    by_row = (lane % (QK_DIM // 2)) < QK_DIM // 4

    def rope(t, trig, scale, out_ref, rows):
        for j in range(QK_WIDTH // LANES):
            tj = t[:, j * LANES:(j + 1) * LANES]
            r = tj * trig[0] + pltpu.roll(tj, LANES // 2, 1) * trig[1]
            out_ref[rows, j * LANES:(j + 1) * LANES] = (r * scale).astype(BF16)

    for r0 in range(0, x_ref.shape[0], PROJ_SUB):
        rows = slice(r0, r0 + PROJ_SUB)
        trig = []
        for cs in range(2):
            trig.append(jnp.concatenate(
                [jnp.where(by_row, rtab_ref[cs, g:g + 1, :], ctab_ref[cs])
                 for g in range(r0 // GRID_W, (r0 + PROJ_SUB) // GRID_W)], axis=0))
        x = x_ref[rows, :]
        h = _mod_norm(x, nw_ref[...], mod_ref[0, 0:1, :], mod_ref[0, 1:2, :])
        hb = h.astype(BF16)
        rope(_dot(hb, wqk_ref[:, :QK_WIDTH]), trig, QK_DIM ** -0.5, q_ref, rows)
        rope(_dot(hb, wqk_ref[:, QK_WIDTH:]), trig, 1.0, k_ref, rows)
        v_ref[rows, :] = _dot(hb, w_ref[:, V_OFF:G_OFF]).astype(BF16)
        g = _dot(hb, w_ref[:, G_OFF:F_OFF])
        sg_ref[rows, :] = (g * _sigmoid(g)).astype(BF16)
        f_ref[rows, :] = _dot(hb, w_ref[:, F_OFF:IN_COLS]).astype(BF16)


def _proj(x2, mod3, norm_w, w_qk, w_in, rtab, ctab, seq, to_cast):
    t, d = x2.shape
    tl = PROJ_TILE
    tpb = seq // tl
    tok = lambda w: pl.BlockSpec((tl, w), lambda i: (i, 0))
    cast_specs, cast_shapes = _cast_jobs(to_cast, t // tl)
    out = pl.pallas_call(
        functools.partial(_proj_kernel, n_cast=len(to_cast)),
        grid=(t // tl,),
        in_specs=[tok(d),
                  pl.BlockSpec((1, N_MOD, d), lambda i: (i // tpb, 0, 0)),
                  _const_spec((1, d)),
                  _const_spec(w_qk.shape), _const_spec(w_in.shape),
                  pl.BlockSpec((2, tl // GRID_W, LANES), lambda i: (0, i % tpb, 0)),
                  _const_spec(ctab.shape)] + cast_specs,
        out_specs=[tok(QK_WIDTH), tok(QK_WIDTH), tok(V_WIDTH), tok(V_WIDTH),
                   tok(F_WIDTH)] + cast_specs,
        out_shape=[jax.ShapeDtypeStruct((t, QK_WIDTH), BF16),
                   jax.ShapeDtypeStruct((t, QK_WIDTH), BF16),
                   jax.ShapeDtypeStruct((t, V_WIDTH), BF16),
                   jax.ShapeDtypeStruct((t, V_WIDTH), BF16),
                   jax.ShapeDtypeStruct((t, F_WIDTH), BF16)] + cast_shapes,
        compiler_params=_params(("arbitrary",)),
        name="proj",
    )(x2, mod3, norm_w, w_qk, w_in, rtab, ctab, *to_cast)
    return out[:5], out[5:]


def _retention_kernel(af_ref, ab_ref, q_ref, k_ref, v_ref, kc_ref, vc_ref,
                      tw_ref, w2_ref, cc_ref, y_ref,
                      o_ref, z_ref, ds_scr, st_scr, d_scr, tab_scr, z_scr):
    c = RET_CHUNK
    seq = q_ref.shape[1]
    lc = kc_ref.shape[1]
    n = seq // c
    lgf = [-jnp.exp(af_ref[hh]) for hh in range(2)]
    lgb = [-jnp.exp(ab_ref[hh]) for hh in range(2)]
    half = QK_DIM // 2

    lane_head = (lax.broadcasted_iota(jnp.int32, (1, LANES), 1) // half) % 2
    masks = [(lane_head == hh).astype(BF16) for hh in range(2)]
    lgf_lane = jnp.where(lane_head == 0, lgf[0], lgf[1])
    lgb_lane = jnp.where(lane_head == 0, lgb[0], lgb[1])
    row_head = (lax.broadcasted_iota(jnp.int32, (LANES, 1), 0) // half) % 2
    lgf_row = jnp.where(row_head == 0, lgf[0][:, 0:1], lgf[1][:, 0:1])
    lgb_row = jnp.where(row_head == 0, lgb[0][:, 0:1], lgb[1][:, 0:1])

    pos = lax.broadcasted_iota(jnp.int32, (c, LANES), 0).astype(F32)
    tab_scr[0] = jnp.exp(lgf_lane * (pos + 1.0))
    tab_scr[1] = jnp.exp(lgb_lane * (c - pos))
    diff = (lax.broadcasted_iota(jnp.int32, (c, c), 0)
            - lax.broadcasted_iota(jnp.int32, (c, c), 1)).astype(F32)
    for hh in range(2):
        d_scr[hh] = (jnp.where(diff >= 0, jnp.exp(lgf[hh][:, 0:1] * jnp.maximum(diff, 0.0)), 0.0)
                     + jnp.where(diff <= 0, jnp.exp(lgb[hh][:, 0:1] * jnp.maximum(-diff, 0.0)), 0.0))

    def k_decays(tokens):
        t = lax.broadcasted_iota(jnp.int32, (1, tokens), 1).astype(F32)
        return jnp.exp(lgf_row * (tokens - 1.0 - t)), jnp.exp(lgb_row * t)

    def state_increment(k_rows, v_rows, decays):
        kt = jnp.transpose(k_rows.astype(F32))
        lhs = jnp.concatenate([kt * decays[0], kt * decays[1]], axis=0).astype(BF16)
        return _dot(lhs, v_rows)

    kdec = k_decays(c)

    def incr(i, carry):
        rows = pl.ds(pl.multiple_of(i * c, c), c)
        ds_scr[i] = state_increment(k_ref[0, rows, :], v_ref[0, rows, :], kdec)
        return carry

    lax.fori_loop(0, n, incr, 0, unroll=RET_UNROLL)

    s0 = state_increment(kc_ref[0], vc_ref[0], k_decays(lc))
    col_head = lax.broadcasted_iota(jnp.int32, (1, 2 * V_DIM), 1) // V_DIM
    own = (row_head == col_head).astype(F32)
    decay_f = jnp.exp(lgf_row * c) * own
    decay_b = jnp.exp(lgb_row * c) * own

    def scan_f(i, s):
        st_scr[i, 0:LANES, :] = (s * own).astype(BF16)
        return decay_f * s + ds_scr[i, 0:LANES, :]

    lax.fori_loop(0, n, scan_f, s0[0:LANES])

    def scan_b(t, s):
        i = n - 1 - t
        st_scr[i, LANES:2 * LANES, :] = (s * own).astype(BF16)
        return decay_b * s + ds_scr[i, LANES:2 * LANES, :]

    lax.fori_loop(0, n, scan_b, s0[LANES:2 * LANES])

    def outputs(i):
        rows = pl.ds(pl.multiple_of(i * c, c), c)
        q = q_ref[0, rows, :]
        k = k_ref[0, rows, :]
        qf = q.astype(F32)
        qd = jnp.concatenate([qf * tab_scr[0], qf * tab_scr[1]], axis=1).astype(BF16)
        inter = _dot(qd, st_scr[i])
        for hh in range(2):
            cols = slice(hh * V_DIM, (hh + 1) * V_DIM)
            scores = lax.dot_general(q * masks[hh], k, (((1,), (1,)), ((), ())),
                                     preferred_element_type=F32)
            o = _dot((scores * d_scr[hh]).astype(BF16), v_ref[0, rows, cols]) + inter[:, cols]
            o_ref[0, rows, cols] = o.astype(BF16)

    per_k1 = n // DFT_ROWS
    cc = cc_ref[...].astype(BF16)

    def outputs_and_dft(j, carry):
        for u in range(per_k1):
            outputs(j * per_k1 + u)
        _dft_second_stage(j, tw_ref, w2_ref, cc, y_ref, z_scr)
        return carry

    lax.fori_loop(0, DFT_ROWS, outputs_and_dft, 0, unroll=RET_UNROLL // per_k1)
    _dft_emit(z_scr, z_ref)


def _retention_and_dft(a_f, a_b, q, k, v, kc, vc, y, tw, w2, w_c):
    b, seq, _ = v.shape
    lc = kc.shape[1]
    c = RET_CHUNK
    n = seq // c
    r = DFT_ROWS
    _, _, n1, n2, w = y.shape
    assert n % r == 0 and n1 // r == HEADS // 2
    dec = pl.BlockSpec((2, 1, LANES), lambda i, p: (p, 0, 0))
    qk = lambda rows: pl.BlockSpec((1, rows, LANES), lambda i, p: (i, 0, p))
    vv = lambda rows: pl.BlockSpec((1, rows, 2 * V_DIM), lambda i, p: (i, 0, p))
    return pl.pallas_call(
        _retention_kernel,
        grid=(b, HEADS // 2),
        in_specs=[dec, dec, qk(seq), qk(seq), vv(seq), qk(lc), vv(lc),
                  pl.BlockSpec((r, 2, n2), lambda i, p: (p, 0, 0)),
                  _const_spec((2, n2, n2)),
                  _const_spec((2 * F_GROUP_DIM, F_GROUP_DIM)),
                  pl.BlockSpec((1, 2, r, n2, w), lambda i, p: (i, 0, p, 0, 0))],
        out_specs=[vv(seq), pl.BlockSpec((1, n2, r, w), lambda i, p: (i, 0, p, 0))],
        out_shape=[jax.ShapeDtypeStruct((b, seq, V_WIDTH), BF16),
                   jax.ShapeDtypeStruct((b, n2, n1, w), BF16)],
        scratch_shapes=[pltpu.VMEM((n, 2 * LANES, 2 * V_DIM), F32),
                        pltpu.VMEM((n, 2 * LANES, 2 * V_DIM), BF16),
                        pltpu.VMEM((2, c, c), F32),
                        pltpu.VMEM((2, c, LANES), F32),
                        pltpu.VMEM((F_GROUPS, n2 * r, F_GROUP_DIM), F32)],
        compiler_params=_params(("parallel", "parallel")),
        name="retention_dft",
    )(a_f, a_b, q, k, v, kc, vc, tw, w2, w_c, y)


def _dft_a_kernel(w_ref, x_ref, y_ref):
    _, n1, rows, w = x_ref.shape
    r = DFT_ROWS
    nh = n1 // 2 + 1
    for r0 in range(0, rows, r):
        x = x_ref[0, :, r0:r0 + r, :].reshape(n1 * r, w)
        y = _dot(w_ref[...], x).reshape(2, nh, r, w).astype(BF16)
        y_ref[0, :, 0:nh, r0:r0 + r, :] = y
        for k1 in range(1, n1 // 2):
            y_ref[0, 0, n1 - k1, r0:r0 + r, :] = y[0, k1]
            y_ref[0, 1, n1 - k1, r0:r0 + r, :] = -y[1, k1]


def _dft_second_stage(j, tw_ref, w2_ref, cc, y_ref, z_scr):
    n2 = y_ref.shape[3]
    w2c = w2_ref[0]
    w2s = w2_ref[1]
    tc = tw_ref[j, 0:1, :]
    ts = tw_ref[j, 1:2, :]
    ec = w2c * tc - w2s * ts
    es = w2s * tc + w2c * ts
    m = jnp.concatenate([jnp.concatenate([ec, -es], axis=1),
                         jnp.concatenate([es, ec], axis=1)], axis=0).astype(BF16)
    y = jnp.concatenate([y_ref[0, 0, j], y_ref[0, 1, j]], axis=0)
    zz = _dot(m, y)
    for g in range(F_GROUPS):
        cols = slice(g * F_GROUP_DIM, (g + 1) * F_GROUP_DIM)
        zcs = jnp.concatenate([zz[:n2, cols], zz[n2:, cols]], axis=1).astype(BF16)
        z_scr[g, pl.ds(j, n2, stride=DFT_ROWS), :] = _dot(zcs, cc)


def _dft_emit(z_scr, z_ref):
    n2 = z_ref.shape[1]
    z = jnp.concatenate([z_scr[g] for g in range(F_GROUPS)], axis=1)
    z_ref[0] = z.reshape(n2, DFT_ROWS, F_WIDTH).astype(BF16)


def _dft_tables(seq):
    n1 = DFT_N1
    n2 = seq // n1

    def cs(num, den):
        ang = 2.0 * np.pi * (num % den) / den
        return np.cos(ang), np.sin(ang)

    a = np.arange(n1)
    w_a = np.concatenate(cs(a[:n1 // 2 + 1, None] * a[None, :], n1), axis=0)
    m = np.arange(n2)
    tw = np.stack(cs(a[:, None] * m[None, :], seq), axis=1)
    w2 = np.stack(cs(m[:, None] * m[None, :], n2), axis=0)
    ch = np.arange(F_GROUP_DIM)
    cc, sc = cs(ch[:, None] * ch[None, :], F_GROUP_DIM)
    scale = 1.0 / np.sqrt(seq * F_GROUP_DIM)
    w_c = np.concatenate([cc, -sc], axis=0) * scale
    return [jnp.asarray(t, dtype=F32) for t in (w_a, tw, w2, w_c)]


def _dft_first_stage(f):
    b, seq, w = f.shape
    n1 = DFT_N1
    n2 = seq // n1
    w_a, tw, w2, w_c = _dft_tables(seq)
    r = DFT_ROWS
    spread = (jnp.arange(n1 * r)[None, :] // r == jnp.arange(n1)[:, None]).astype(F32)
    w_rep = jnp.dot(w_a, spread, precision=lax.Precision.HIGHEST)
    same_r = jnp.arange(w_a.shape[0] * r)[:, None] % r == jnp.arange(n1 * r)[None, :] % r
    w_a = jnp.where(same_r, jnp.repeat(w_rep, r, axis=0), 0.0).astype(BF16)
    y = pl.pallas_call(
        _dft_a_kernel,
        grid=(b, n2 // DFT_A_ROWS),
        in_specs=[_const_spec(w_a.shape),
                  pl.BlockSpec((1, n1, DFT_A_ROWS, w), lambda i, j: (i, 0, j, 0))],
        out_specs=pl.BlockSpec((1, 2, n1, DFT_A_ROWS, w), lambda i, j: (i, 0, 0, j, 0)),
        out_shape=jax.ShapeDtypeStruct((b, 2, n1, n2, w), BF16),
        compiler_params=_params(("parallel", "parallel")),
        name="dft_a",
    )(w_a, f.reshape(b, n1, n2, w))
    return y, tw, w2, w_c


def _merge_kernel(x_ref, mod_ref, nw_ref, n2w_ref, ro_ref, sg_ref, z_ref, wbg_ref, bbg_ref,
                  wro_ref, wfo_ref, wout_ref, *refs, n_cast):
    o_ref, h2_ref = refs[n_cast:n_cast + 2]
    _run_casts(refs[:n_cast] + refs[n_cast + 2:])
    for r0 in range(0, x_ref.shape[0], MERGE_SUB):
        rows = slice(r0, r0 + MERGE_SUB)
        x = x_ref[rows, :]
        h = _mod_norm(x, nw_ref[...], mod_ref[0, 0:1, :], mod_ref[0, 1:2, :])
        gates = _sigmoid(_dot(h.astype(BF16), wbg_ref[...]) + bbg_ref[...])
        gated = []
        for hd in range(HEADS):
            cols = slice(hd * V_DIM, (hd + 1) * V_DIM)
            o = ro_ref[rows, cols].astype(F32)
            oc = o - jnp.mean(o, axis=-1, keepdims=True)
            var = jnp.mean(oc * oc, axis=-1, keepdims=True)
            gated.append((oc * lax.rsqrt(var + EPS) * sg_ref[rows, cols].astype(F32)).astype(BF16))
        ret_d = _dot(jnp.concatenate(gated, axis=1), wro_ref[...])
        four_d = _dot(z_ref[rows, :], wfo_ref[...])
        m = gates[:, :D_MODEL] * ret_d + gates[:, D_MODEL:] * four_d
        y = _dot(m.astype(BF16), wout_ref[...])
        x1 = x + mod_ref[0, 2:3, :] * y
        o_ref[rows, :] = x1
        h2 = _mod_norm(x1, n2w_ref[...], mod_ref[0, 3:4, :], mod_ref[0, 4:5, :])
        h2_ref[rows, :] = h2.astype(BF16)


def _merge(x2, mod3, norm_w, norm2_w, ro, sg, z, w_bg, b_bg, w_ro, w_fo, w_out, seq, to_cast):
    t, d = x2.shape
    tl = MERGE_TILE
    tpb = seq // tl
    tok = lambda w: pl.BlockSpec((tl, w), lambda i: (i, 0))
    cast_specs, cast_shapes = _cast_jobs(to_cast, t // tl)
    out = pl.pallas_call(
        functools.partial(_merge_kernel, n_cast=len(to_cast)),
        grid=(t // tl,),
        in_specs=[tok(d),
                  pl.BlockSpec((1, N_MOD, d), lambda i: (i // tpb, 0, 0)),
                  _const_spec((1, d)), _const_spec((1, d)),
                  tok(V_WIDTH), tok(V_WIDTH), tok(F_WIDTH),
                  _const_spec(w_bg.shape), _const_spec((1, 2 * d)),
                  _const_spec(w_ro.shape), _const_spec(w_fo.shape),
                  _const_spec(w_out.shape)] + cast_specs,
        out_specs=[tok(d), tok(d)] + cast_specs,
        out_shape=[jax.ShapeDtypeStruct((t, d), F32),
                   jax.ShapeDtypeStruct((t, d), BF16)] + cast_shapes,
        compiler_params=_params(("arbitrary",)),
        name="merge",
    )(x2, mod3, norm_w, norm2_w, ro, sg, z, w_bg, b_bg, w_ro, w_fo, w_out, *to_cast)
    return out[:2], out[2:]


HALO = 8
FFN_LEAD = 16
FFN_SUB = 256


def _two_gelu_tanh(x):
    c1 = np.sqrt(2.0 / np.pi)
    return x + x * jnp.tanh(x * (c1 + (c1 * 0.044715) * (x * x)))


def _ffn_kernel(xm_ref, hm_ref, xp_ref, xn_ref, mod_ref, nw_ref, wu_ref, cw_ref, cb_ref,
                wd_ref, fnw_ref, o_ref, h_scr, u_scr, act_scr, y_scr, *, tiles_per_seq):
    tl = xm_ref.shape[0]
    sub = FFN_SUB
    half = sub // 2
    nc = FFN_CHUNK
    i = pl.program_id(0)
    keep_prev = ((i % tiles_per_seq) != 0).astype(F32)
    keep_next = ((i % tiles_per_seq) != tiles_per_seq - 1).astype(F32)

    def pre(x):
        return _mod_norm(x, nw_ref[...], mod_ref[0, 3:4, :], mod_ref[0, 4:5, :])

    pad = jnp.zeros((FFN_LEAD - HALO, xm_ref.shape[1]), F32)
    h_scr[0:FFN_LEAD] = jnp.concatenate([pad, pre(xp_ref[...]) * keep_prev], axis=0).astype(BF16)
    h_scr[FFN_LEAD:FFN_LEAD + tl] = hm_ref[...]
    h_scr[FFN_LEAD + tl:] = jnp.concatenate([pre(xn_ref[...]) * keep_next, pad], axis=0).astype(BF16)

    def project(j):
        hb = h_scr[...]
        for part in range(2):
            lo = part * FFN_DIM + j * nc
            u = _dot(hb, wu_ref[:, lo:lo + nc])
            for s in range(nc // LANES):
                u_scr[j % 2, part, s] = u[:, s * LANES:(s + 1) * LANES]

    def conv(j, part, s, blk, scale):
        lo = part * FFN_DIM + j * nc + s * LANES
        w = cw_ref[:, lo:lo + LANES] * scale
        bias = cb_ref[:, lo:lo + LANES] * scale
        first = FFN_LEAD + blk * sub
        rows = lambda start: u_scr[j % 2, part, s, pl.ds(start, half, stride=2), :]
        before, even, odd, after = rows(first - 1), rows(first), rows(first + 1), rows(first + 2)
        return (before * w[0:1] + even * w[1:2] + odd * w[2:3] + bias,
                even * w[0:1] + odd * w[1:2] + after * w[2:3] + bias)

    def activate(j):
        for s in range(nc // LANES):
            cols = slice(j * nc + s * LANES, j * nc + (s + 1) * LANES)
            for blk in range(tl // sub):
                gate = conv(j, 0, s, blk, 1.0)
                val = conv(j, 1, s, blk, 0.5)
                for par in range(2):
                    r0 = blk * sub + par * half
                    act_scr[r0:r0 + half, cols] = (_two_gelu_tanh(gate[par]) * val[par]).astype(BF16)

    project(0)
    for j in range(N_FFN_CHUNKS):
        if j + 1 < N_FFN_CHUNKS:
            project(j + 1)
        activate(j)
    for blk in range(tl // sub):
        r0 = blk * sub
        y = _dot(act_scr[r0:r0 + sub, :], wd_ref[...])
        for s in range(y_scr.shape[0]):
            for par in range(2):
                y_scr[s, pl.ds(par, half, stride=2), :] = (
                    y[par * half:(par + 1) * half, s * LANES:(s + 1) * LANES])
        y = jnp.concatenate([y_scr[s] for s in range(y_scr.shape[0])], axis=1)
        x2 = xm_ref[r0:r0 + sub, :] + mod_ref[0, 5:6, :] * y
        o_ref[r0:r0 + sub, :] = _rms_norm(x2, fnw_ref[...])


def _ffn(x1, h2, mod3, norm_w, w_u, cw, cb, w_d, fnorm_w, seq):
    t, d = x1.shape
    tl = FFN_TILE
    tpb = seq // tl
    hb = tl // HALO
    last = t // HALO - 1
    rows = FFN_LEAD + tl + FFN_LEAD
    return pl.pallas_call(
        functools.partial(_ffn_kernel, tiles_per_seq=tpb),
        grid=(t // tl,),
        in_specs=[pl.BlockSpec((tl, d), lambda i: (i, 0)),
                  pl.BlockSpec((tl, d), lambda i: (i, 0)),
                  pl.BlockSpec((HALO, d), lambda i: (jnp.maximum(i * hb - 1, 0), 0)),
                  pl.BlockSpec((HALO, d), lambda i: (jnp.minimum((i + 1) * hb, last), 0)),
                  pl.BlockSpec((1, N_MOD, d), lambda i: (i // tpb, 0, 0)),
                  _const_spec((1, d)),
                  _const_spec(w_u.shape), _const_spec(cw.shape), _const_spec(cb.shape),
                  _const_spec(w_d.shape), _const_spec((1, d))],
        out_specs=pl.BlockSpec((tl, d), lambda i: (i, 0)),
        out_shape=jax.ShapeDtypeStruct((t, d), F32),
        scratch_shapes=[pltpu.VMEM((rows, d), BF16),
                        pltpu.VMEM((2, 2, FFN_CHUNK // LANES, rows, LANES), F32),
                        pltpu.VMEM((tl, FFN_DIM), BF16),
                        pltpu.VMEM((d // LANES, FFN_SUB, LANES), F32)],
        compiler_params=_params(("parallel",)),
        name="ffn",
    )(x1, h2, x1, x1, mod3, norm_w, w_u, cw, cb, w_d, fnorm_w)


def _rope_tables(seq):
    n_freq = QK_DIM // 4
    lane = jnp.arange(LANES)
    inv = ROPE_BASE ** (-(lane % n_freq).astype(F32) / n_freq)
    sign = jnp.where(lane < LANES // 2, -1.0, 1.0).astype(F32)

    def table(n):
        ang = jnp.arange(n, dtype=F32)[:, None] * inv[None, :]
        return jnp.stack([jnp.cos(ang), jnp.sin(ang) * sign])

    return table(seq // GRID_W), table(GRID_W)


def kernel(x, c, ctx, c_ctx, w_mod, b_mod, norm1_w, w_in, ret_decay_f, ret_decay_b,
           w_ret_out, w_four_out, w_branch_gate, b_branch_gate, w_out, norm2_w,
           w_up, conv_w, conv_b, w_down, final_norm_w):
    assert w_mod.shape[0] == 1, "single-layer block"
    b, seq, d = x.shape
    t = b * seq
    assert b == 2 and seq % RET_CHUNK == 0
    assert seq % FFN_TILE == 0 and seq % MERGE_TILE == 0 and seq % PROJ_TILE == 0

    c8 = jnp.concatenate([c, c_ctx[None, :], jnp.zeros((8 - b - 1, d), F32)], axis=0)
    mod, w_in_b, w_qk = _modulation(c8, w_mod[0], b_mod[0], w_in[0])
    mod3 = mod.reshape(8, N_MOD, d)
    n1w = norm1_w[0].reshape(1, d)

    kc, vc = _ctx_proj(ctx, mod3, n1w, w_qk, w_in_b)
    rtab, ctab = _rope_tables(seq)
    x2 = x.reshape(t, d)
    (q, k, v, sg, f), (w_bg, w_ro, w_fo, w_o) = _proj(
        x2, mod3, n1w, w_qk, w_in_b, rtab, ctab, seq,
        [w_branch_gate[0], w_ret_out[0], w_four_out[0], w_out[0]])

    a_f = jnp.broadcast_to(ret_decay_f[0][:, None, None], (HEADS, 1, LANES))
    a_b = jnp.broadcast_to(ret_decay_b[0][:, None, None], (HEADS, 1, LANES))
    y, tw, w2, w_c = _dft_first_stage(f.reshape(b, seq, F_WIDTH))
    ro, z = _retention_and_dft(a_f, a_b, q.reshape(b, seq, QK_WIDTH), k.reshape(b, seq, QK_WIDTH),
                               v.reshape(b, seq, V_WIDTH), kc, vc, y, tw, w2, w_c)

    n2w = norm2_w[0].reshape(1, d)
    (x1, h2), (w_u, w_d) = _merge(
        x2, mod3, n1w, n2w, ro.reshape(t, V_WIDTH), sg, z.reshape(t, F_WIDTH),
        w_bg, b_branch_gate[0].reshape(1, 2 * d), w_ro, w_fo, w_o, seq, [w_up[0], w_down[0]])

    out = _ffn(x1, h2, mod3, n2w, w_u, conv_w[0], conv_b[0].reshape(1, 2 * FFN_DIM), w_d,
               final_norm_w.reshape(1, d), seq)
    return out.reshape(b, seq, d)
```

```python
import functools

import numpy as np
import jax
import jax.numpy as jnp
from jax import lax
from jax.experimental import pallas as pl
from jax.experimental.pallas import tpu as pltpu

F32 = jnp.float32
BF16 = jnp.bfloat16

D_MODEL = 1024
GRID_W = 64
HEADS = 8
QK_DIM = 64
V_DIM = 128
QK_WIDTH = HEADS * QK_DIM
V_WIDTH = HEADS * V_DIM
ROPE_BASE = 10000.0
F_GROUPS = 4
F_GROUP_DIM = 128
F_WIDTH = F_GROUPS * F_GROUP_DIM
K_OFF = QK_WIDTH
V_OFF = K_OFF + QK_WIDTH
G_OFF = V_OFF + V_WIDTH
F_OFF = G_OFF + V_WIDTH
IN_COLS = F_OFF + F_WIDTH
FFN_DIM = 2816
N_MOD = 6
EPS = 1e-6

LANES = 128
RET_CHUNK = 256
RET_UNROLL = 16
FFN_CHUNK = 256
N_FFN_CHUNKS = FFN_DIM // FFN_CHUNK
FFN_TILE = 512
MERGE_TILE = 1024
MERGE_SUB = 512
PROJ_TILE = 1024
PROJ_SUB = 256
DFT_N1 = 64
BF16_ROWS = 16
DFT_ROWS = BF16_ROWS
DFT_A_ROWS = 2 * DFT_ROWS
VMEM_LIMIT = 56 * 1024 * 1024


def _params(sem):
    return pltpu.CompilerParams(dimension_semantics=sem, vmem_limit_bytes=VMEM_LIMIT)


def _dot(a, b):
    return jnp.dot(a, b, preferred_element_type=F32)


def _rms_norm(x, w):
    return x * lax.rsqrt(jnp.mean(x * x, axis=-1, keepdims=True) + EPS) * w


def _mod_norm(x, w, shift, scale):
    return x * lax.rsqrt(jnp.mean(x * x, axis=-1, keepdims=True) + EPS) * (w * (1.0 + scale)) + shift


def _const_spec(shape):
    zeros = (0,) * len(shape)
    return pl.BlockSpec(shape, lambda *_: zeros, pipeline_mode=pl.Buffered(1))


def _cast_jobs(weights, steps):
    specs, shapes = [], []
    for w in weights:
        n_rows = w.shape[0]
        rows = -(-n_rows // steps)
        while rows % BF16_ROWS or n_rows % rows:
            rows += 1
        last = n_rows // rows - 1
        specs.append(pl.BlockSpec((rows, w.shape[1]), lambda i, last=last: (jnp.minimum(i, last), 0)))
        shapes.append(jax.ShapeDtypeStruct(w.shape, BF16))
    return specs, shapes


def _run_casts(refs):
    n = len(refs) // 2
    for src, dst in zip(refs[:n], refs[n:]):
        dst[...] = src[...].astype(BF16)


def _mod_kernel(c_ref, w_ref, b_ref, win_ref, o_ref, winb_ref, wqk_ref):
    c = c_ref[...]
    s = c * jax.nn.sigmoid(c)
    o_ref[...] = _dot(s.astype(BF16), w_ref[...].astype(BF16)) + b_ref[...]
    wb = win_ref[...].astype(BF16)
    winb_ref[...] = wb
    n_qk = 2 * QK_WIDTH
    src = lax.broadcasted_iota(jnp.int32, (n_qk, n_qk), 0)
    dst = lax.broadcasted_iota(jnp.int32, (n_qk, n_qk), 1)
    select = (src == _qk_source_column(dst)).astype(BF16)
    wqk_ref[...] = _dot(wb[:, :n_qk], select).astype(BF16)


def _qk_source_column(col):
    half = QK_DIM // 2
    lane = col % LANES
    head = 2 * ((col % QK_WIDTH) // LANES) + (lane // half) % 2
    return (col // QK_WIDTH) * QK_WIDTH + head * QK_DIM + (lane // (2 * half)) * half + lane % half


def _modulation(c8, w_mod, b_mod, w_in):
    n = w_mod.shape[1]
    steps = 4
    tn = n // steps
    d, n_in = w_in.shape
    rows = d // steps
    return pl.pallas_call(
        _mod_kernel,
        grid=(steps,),
        in_specs=[_const_spec((8, D_MODEL)),
                  pl.BlockSpec((D_MODEL, tn), lambda j: (0, j)),
                  pl.BlockSpec((1, tn), lambda j: (0, j)),
                  pl.BlockSpec((rows, n_in), lambda j: (j, 0))],
        out_specs=[pl.BlockSpec((8, tn), lambda j: (0, j)),
                   pl.BlockSpec((rows, n_in), lambda j: (j, 0)),
                   pl.BlockSpec((rows, 2 * QK_WIDTH), lambda j: (j, 0))],
        out_shape=[jax.ShapeDtypeStruct((8, n), F32),
                   jax.ShapeDtypeStruct((d, n_in), BF16),
                   jax.ShapeDtypeStruct((d, 2 * QK_WIDTH), BF16)],
        compiler_params=_params(("parallel",)),
        name="mod",
    )(c8, w_mod, b_mod.reshape(1, n), w_in)


def _ctx_kernel(x_ref, mod_ref, nw_ref, wk_ref, wv_ref, k_ref, v_ref):
    x = x_ref[0]
    h = _mod_norm(x, nw_ref[...], mod_ref[0, 0:1, :], mod_ref[0, 1:2, :])
    hb = h.astype(BF16)
    k_ref[0] = _dot(hb, wk_ref[...]).astype(BF16)
    v_ref[0] = _dot(hb, wv_ref[...]).astype(BF16)


def _ctx_proj(ctx, mod3, norm_w, w_qk, w_in):
    b, lc, d = ctx.shape
    assert V_OFF % V_WIDTH == 0
    return pl.pallas_call(
        _ctx_kernel,
        grid=(b,),
        in_specs=[pl.BlockSpec((1, lc, d), lambda i: (i, 0, 0)),
                  pl.BlockSpec((1, N_MOD, d), lambda i: (2, 0, 0)),
                  _const_spec((1, d)),
                  pl.BlockSpec((d, QK_WIDTH), lambda i: (0, 1), pipeline_mode=pl.Buffered(1)),
                  pl.BlockSpec((d, V_WIDTH), lambda i: (0, V_OFF // V_WIDTH),
                               pipeline_mode=pl.Buffered(1))],
        out_specs=[pl.BlockSpec((1, lc, QK_WIDTH), lambda i: (i, 0, 0)),
                   pl.BlockSpec((1, lc, V_WIDTH), lambda i: (i, 0, 0))],
        out_shape=[jax.ShapeDtypeStruct((b, lc, QK_WIDTH), BF16),
                   jax.ShapeDtypeStruct((b, lc, V_WIDTH), BF16)],
        compiler_params=_params(("parallel",)),
        name="ctx_proj",
    )(ctx, mod3, norm_w, w_qk, w_in)


def _proj_kernel(x_ref, mod_ref, nw_ref, wqk_ref, w_ref, rtab_ref, ctab_ref, *refs, n_cast):
    q_ref, k_ref, v_ref, sg_ref, f_ref = refs[n_cast:n_cast + 5]
    _run_casts(refs[:n_cast] + refs[n_cast + 5:])
    lane = lax.broadcasted_iota(jnp.int32, (1, LANES), 1)
    by_row = (lane % (QK_DIM // 2)) < QK_DIM // 4

    def rope(t, trig, scale, out_ref, rows):
        for j in range(QK_WIDTH // LANES):
            tj = t[:, j * LANES:(j + 1) * LANES]
            r = tj * trig[0] + pltpu.roll(tj, LANES // 2, 1) * trig[1]
            out_ref[rows, j * LANES:(j + 1) * LANES] = (r * scale).astype(BF16)

    for r0 in range(0, x_ref.shape[0], PROJ_SUB):
        rows = slice(r0, r0 + PROJ_SUB)
        trig = []
        for cs in range(2):
            trig.append(jnp.concatenate(
                [jnp.where(by_row, rtab_ref[cs, g:g + 1, :], ctab_ref[cs])
                 for g in range(r0 // GRID_W, (r0 + PROJ_SUB) // GRID_W)], axis=0))
        x = x_ref[rows, :]
        h = _mod_norm(x, nw_ref[...], mod_ref[0, 0:1, :], mod_ref[0, 1:2, :])
        hb = h.astype(BF16)
        rope(_dot(hb, wqk_ref[:, :QK_WIDTH]), trig, QK_DIM ** -0.5, q_ref, rows)
        rope(_dot(hb, wqk_ref[:, QK_WIDTH:]), trig, 1.0, k_ref, rows)
        v_ref[rows, :] = _dot(hb, w_ref[:, V_OFF:G_OFF]).astype(BF16)
        g = _dot(hb, w_ref[:, G_OFF:F_OFF])
        hg = 0.5 * g
        sg_ref[rows, :] = (hg * jnp.tanh(hg) + hg).astype(BF16)
        f_ref[rows, :] = _dot(hb, w_ref[:, F_OFF:IN_COLS]).astype(BF16)


def _proj(x2, mod3, norm_w, w_qk, w_in, rtab, ctab, seq, to_cast):
    t, d = x2.shape
    tl = PROJ_TILE
    tpb = seq // tl
    tok = lambda w: pl.BlockSpec((tl, w), lambda i: (i, 0))
    cast_specs, cast_shapes = _cast_jobs(to_cast, t // tl)
    out = pl.pallas_call(
        functools.partial(_proj_kernel, n_cast=len(to_cast)),
        grid=(t // tl,),
        in_specs=[tok(d),
                  pl.BlockSpec((1, N_MOD, d), lambda i: (i // tpb, 0, 0)),
                  _const_spec((1, d)),
                  _const_spec(w_qk.shape), _const_spec(w_in.shape),
                  pl.BlockSpec((2, tl // GRID_W, LANES), lambda i: (0, i % tpb, 0)),
                  _const_spec(ctab.shape)] + cast_specs,
        out_specs=[tok(QK_WIDTH), tok(QK_WIDTH), tok(V_WIDTH), tok(V_WIDTH),
                   tok(F_WIDTH)] + cast_specs,
        out_shape=[jax.ShapeDtypeStruct((t, QK_WIDTH), BF16),
                   jax.ShapeDtypeStruct((t, QK_WIDTH), BF16),
                   jax.ShapeDtypeStruct((t, V_WIDTH), BF16),
                   jax.ShapeDtypeStruct((t, V_WIDTH), BF16),
                   jax.ShapeDtypeStruct((t, F_WIDTH), BF16)] + cast_shapes,
        compiler_params=_params(("arbitrary",)),
        name="proj",
    )(x2, mod3, norm_w, w_qk, w_in, rtab, ctab, *to_cast)
    return out[:5], out[5:]


def _retention_kernel(af_ref, ab_ref, q_ref, k_ref, v_ref, kc_ref, vc_ref,
                      tw_ref, w2_ref, cc_ref, y_ref,
                      o_ref, z_ref, ds_scr, st_scr, d_scr, tab_scr, z_scr):
    c = RET_CHUNK
    seq = q_ref.shape[1]
    lc = kc_ref.shape[1]
    n = seq // c
    lgf = [-jnp.exp(af_ref[hh]) for hh in range(2)]
    lgb = [-jnp.exp(ab_ref[hh]) for hh in range(2)]
    half = QK_DIM // 2

    lane_head = (lax.broadcasted_iota(jnp.int32, (1, LANES), 1) // half) % 2
    masks = [(lane_head == hh).astype(BF16) for hh in range(2)]
    lgf_lane = jnp.where(lane_head == 0, lgf[0], lgf[1])
    lgb_lane = jnp.where(lane_head == 0, lgb[0], lgb[1])
    row_head = (lax.broadcasted_iota(jnp.int32, (LANES, 1), 0) // half) % 2
    lgf_row = jnp.where(row_head == 0, lgf[0][:, 0:1], lgf[1][:, 0:1])
    lgb_row = jnp.where(row_head == 0, lgb[0][:, 0:1], lgb[1][:, 0:1])

    pos = lax.broadcasted_iota(jnp.int32, (c, LANES), 0).astype(F32)
    tab_scr[0] = jnp.exp(lgf_lane * (pos + 1.0))
    tab_scr[1] = jnp.exp(lgb_lane * (c - pos))
    diff = (lax.broadcasted_iota(jnp.int32, (c, c), 0)
            - lax.broadcasted_iota(jnp.int32, (c, c), 1)).astype(F32)
    for hh in range(2):
        d_scr[hh] = (jnp.where(diff >= 0, jnp.exp(lgf[hh][:, 0:1] * jnp.maximum(diff, 0.0)), 0.0)
                     + jnp.where(diff <= 0, jnp.exp(lgb[hh][:, 0:1] * jnp.maximum(-diff, 0.0)), 0.0))

    def k_decays(tokens):
        t = lax.broadcasted_iota(jnp.int32, (1, tokens), 1).astype(F32)
        return jnp.exp(lgf_row * (tokens - 1.0 - t)), jnp.exp(lgb_row * t)

    def state_increment(k_rows, v_rows, decays):
        kt = jnp.transpose(k_rows.astype(F32))
        lhs = jnp.concatenate([kt * decays[0], kt * decays[1]], axis=0).astype(BF16)
        return _dot(lhs, v_rows)

    kdec = k_decays(c)

    def incr(i, carry):
        rows = pl.ds(pl.multiple_of(i * c, c), c)
        ds_scr[i] = state_increment(k_ref[0, rows, :], v_ref[0, rows, :], kdec)
        return carry

    lax.fori_loop(0, n, incr, 0, unroll=RET_UNROLL)

    s0 = state_increment(kc_ref[0], vc_ref[0], k_decays(lc))
    col_head = lax.broadcasted_iota(jnp.int32, (1, 2 * V_DIM), 1) // V_DIM
    own = (row_head == col_head).astype(F32)
    decay_f = jnp.exp(lgf_row * c) * own
    decay_b = jnp.exp(lgb_row * c) * own

    def scan_f(i, s):
        st_scr[i, 0:LANES, :] = (s * own).astype(BF16)
        return decay_f * s + ds_scr[i, 0:LANES, :]

    lax.fori_loop(0, n, scan_f, s0[0:LANES])

    def scan_b(t, s):
        i = n - 1 - t
        st_scr[i, LANES:2 * LANES, :] = (s * own).astype(BF16)
        return decay_b * s + ds_scr[i, LANES:2 * LANES, :]

    lax.fori_loop(0, n, scan_b, s0[LANES:2 * LANES])

    def outputs(i):
        rows = pl.ds(pl.multiple_of(i * c, c), c)
        q = q_ref[0, rows, :]
        k = k_ref[0, rows, :]
        qf = q.astype(F32)
        qd = jnp.concatenate([qf * tab_scr[0], qf * tab_scr[1]], axis=1).astype(BF16)
        inter = _dot(qd, st_scr[i])
        for hh in range(2):
            cols = slice(hh * V_DIM, (hh + 1) * V_DIM)
            scores = lax.dot_general(q * masks[hh], k, (((1,), (1,)), ((), ())),
                                     preferred_element_type=F32)
            o = _dot((scores * d_scr[hh]).astype(BF16), v_ref[0, rows, cols]) + inter[:, cols]
            o_ref[0, rows, cols] = o.astype(BF16)

    per_k1 = n // DFT_ROWS
    cc = cc_ref[...].astype(BF16)

    def outputs_and_dft(j, carry):
        for u in range(per_k1):
            outputs(j * per_k1 + u)
        _dft_second_stage(j, tw_ref, w2_ref, cc, y_ref, z_scr)
        return carry

    lax.fori_loop(0, DFT_ROWS, outputs_and_dft, 0, unroll=RET_UNROLL // per_k1)
    _dft_emit(z_scr, z_ref)


def _retention_and_dft(a_f, a_b, q, k, v, kc, vc, y, tw, w2, w_c):
    b, seq, _ = v.shape
    lc = kc.shape[1]
    c = RET_CHUNK
    n = seq // c
    r = DFT_ROWS
    _, _, n1, n2, w = y.shape
    assert n % r == 0 and n1 // r == HEADS // 2
    dec = pl.BlockSpec((2, 1, LANES), lambda i, p: (p, 0, 0))
    qk = lambda rows: pl.BlockSpec((1, rows, LANES), lambda i, p: (i, 0, p))
    vv = lambda rows: pl.BlockSpec((1, rows, 2 * V_DIM), lambda i, p: (i, 0, p))
    return pl.pallas_call(
        _retention_kernel,
        grid=(b, HEADS // 2),
        in_specs=[dec, dec, qk(seq), qk(seq), vv(seq), qk(lc), vv(lc),
                  pl.BlockSpec((r, 2, n2), lambda i, p: (p, 0, 0)),
                  _const_spec((2, n2, n2)),
                  _const_spec((2 * F_GROUP_DIM, F_GROUP_DIM)),
                  pl.BlockSpec((1, 2, r, n2, w), lambda i, p: (i, 0, p, 0, 0))],
        out_specs=[vv(seq), pl.BlockSpec((1, n2, r, w), lambda i, p: (i, 0, p, 0))],
        out_shape=[jax.ShapeDtypeStruct((b, seq, V_WIDTH), BF16),
                   jax.ShapeDtypeStruct((b, n2, n1, w), BF16)],
        scratch_shapes=[pltpu.VMEM((n, 2 * LANES, 2 * V_DIM), F32),
                        pltpu.VMEM((n, 2 * LANES, 2 * V_DIM), BF16),
                        pltpu.VMEM((2, c, c), F32),
                        pltpu.VMEM((2, c, LANES), F32),
                        pltpu.VMEM((F_GROUPS, n2 * r, F_GROUP_DIM), F32)],
        compiler_params=_params(("parallel", "parallel")),
        name="retention_dft",
    )(a_f, a_b, q, k, v, kc, vc, tw, w2, w_c, y)


def _dft_a_kernel(w_ref, x_ref, y_ref):
    _, n1, rows, w = x_ref.shape
    r = DFT_ROWS
    nh = n1 // 2 + 1
    for r0 in range(0, rows, r):
        x = x_ref[0, :, r0:r0 + r, :].reshape(n1 * r, w)
        y = _dot(w_ref[...], x).reshape(2, nh, r, w).astype(BF16)
        y_ref[0, :, 0:nh, r0:r0 + r, :] = y
        for k1 in range(1, n1 // 2):
            y_ref[0, 0, n1 - k1, r0:r0 + r, :] = y[0, k1]
            y_ref[0, 1, n1 - k1, r0:r0 + r, :] = -y[1, k1]


def _dft_second_stage(j, tw_ref, w2_ref, cc, y_ref, z_scr):
    n2 = y_ref.shape[3]
    w2c = w2_ref[0]
    w2s = w2_ref[1]
    tc = tw_ref[j, 0:1, :]
    ts = tw_ref[j, 1:2, :]
    ec = w2c * tc - w2s * ts
    es = w2s * tc + w2c * ts
    m = jnp.concatenate([jnp.concatenate([ec, -es], axis=1),
                         jnp.concatenate([es, ec], axis=1)], axis=0).astype(BF16)
    y = jnp.concatenate([y_ref[0, 0, j], y_ref[0, 1, j]], axis=0)
    zz = _dot(m, y)
    for g in range(F_GROUPS):
        cols = slice(g * F_GROUP_DIM, (g + 1) * F_GROUP_DIM)
        zcs = jnp.concatenate([zz[:n2, cols], zz[n2:, cols]], axis=1).astype(BF16)
        z_scr[g, pl.ds(j, n2, stride=DFT_ROWS), :] = _dot(zcs, cc)


def _dft_emit(z_scr, z_ref):
    n2 = z_ref.shape[1]
    z = jnp.concatenate([z_scr[g] for g in range(F_GROUPS)], axis=1)
    z_ref[0] = z.reshape(n2, DFT_ROWS, F_WIDTH).astype(BF16)


def _dft_tables(seq):
    n1 = DFT_N1
    n2 = seq // n1

    def cs(num, den):
        ang = 2.0 * np.pi * (num % den) / den
        return np.cos(ang), np.sin(ang)

    a = np.arange(n1)
    w_a = np.concatenate(cs(a[:n1 // 2 + 1, None] * a[None, :], n1), axis=0)
    m = np.arange(n2)
    tw = np.stack(cs(a[:, None] * m[None, :], seq), axis=1)
    w2 = np.stack(cs(m[:, None] * m[None, :], n2), axis=0)
    ch = np.arange(F_GROUP_DIM)
    cc, sc = cs(ch[:, None] * ch[None, :], F_GROUP_DIM)
    scale = 1.0 / np.sqrt(seq * F_GROUP_DIM)
    w_c = np.concatenate([cc, -sc], axis=0) * scale
    return [jnp.asarray(t, dtype=F32) for t in (w_a, tw, w2, w_c)]


def _dft_first_stage(f):
    b, seq, w = f.shape
    n1 = DFT_N1
    n2 = seq // n1
    w_a, tw, w2, w_c = _dft_tables(seq)
    r = DFT_ROWS
    spread = (jnp.arange(n1 * r)[None, :] // r == jnp.arange(n1)[:, None]).astype(F32)
    w_rep = jnp.dot(w_a, spread, precision=lax.Precision.HIGHEST)
    same_r = jnp.arange(w_a.shape[0] * r)[:, None] % r == jnp.arange(n1 * r)[None, :] % r
    w_a = jnp.where(same_r, jnp.repeat(w_rep, r, axis=0), 0.0).astype(BF16)
    y = pl.pallas_call(
        _dft_a_kernel,
        grid=(b, n2 // DFT_A_ROWS),
        in_specs=[_const_spec(w_a.shape),
                  pl.BlockSpec((1, n1, DFT_A_ROWS, w), lambda i, j: (i, 0, j, 0))],
        out_specs=pl.BlockSpec((1, 2, n1, DFT_A_ROWS, w), lambda i, j: (i, 0, 0, j, 0)),
        out_shape=jax.ShapeDtypeStruct((b, 2, n1, n2, w), BF16),
        compiler_params=_params(("parallel", "parallel")),
        name="dft_a",
    )(w_a, f.reshape(b, n1, n2, w))
    return y, tw, w2, w_c


def _merge_kernel(x_ref, mod_ref, nw_ref, n2w_ref, ro_ref, sg_ref, z_ref, wbg_ref, bbg_ref,
                  wro_ref, wfo_ref, wout_ref, *refs, n_cast):
    o_ref, h2_ref = refs[n_cast:n_cast + 2]
    _run_casts(refs[:n_cast] + refs[n_cast + 2:])
    for r0 in range(0, x_ref.shape[0], MERGE_SUB):
        rows = slice(r0, r0 + MERGE_SUB)
        x = x_ref[rows, :]
        h = _mod_norm(x, nw_ref[...], mod_ref[0, 0:1, :], mod_ref[0, 1:2, :])
        t = jnp.tanh(0.5 * _dot(h.astype(BF16), wbg_ref[...]) + 0.5 * bbg_ref[...])
        gated = []
        for hd in range(HEADS):
            cols = slice(hd * V_DIM, (hd + 1) * V_DIM)
            o = ro_ref[rows, cols].astype(F32)
            oc = o - jnp.mean(o, axis=-1, keepdims=True)
            var = jnp.mean(oc * oc, axis=-1, keepdims=True)
            gated.append((oc * lax.rsqrt(var + EPS) * sg_ref[rows, cols].astype(F32)).astype(BF16))
        ret_d = _dot(jnp.concatenate(gated, axis=1), wro_ref[...])
        four_d = _dot(z_ref[rows, :], wfo_ref[...])
        m2 = (t[:, :D_MODEL] * ret_d + ret_d) + (t[:, D_MODEL:] * four_d + four_d)
        y2 = _dot(m2.astype(BF16), wout_ref[...])
        x1 = x + (0.5 * mod_ref[0, 2:3, :]) * y2
        o_ref[rows, :] = x1
        h2 = _mod_norm(x1, n2w_ref[...], mod_ref[0, 3:4, :], mod_ref[0, 4:5, :])
        h2_ref[rows, :] = h2.astype(BF16)


def _merge(x2, mod3, norm_w, norm2_w, ro, sg, z, w_bg, b_bg, w_ro, w_fo, w_out, seq, to_cast):
    t, d = x2.shape
    tl = MERGE_TILE
    tpb = seq // tl
    tok = lambda w: pl.BlockSpec((tl, w), lambda i: (i, 0))
    cast_specs, cast_shapes = _cast_jobs(to_cast, t // tl)
    out = pl.pallas_call(
        functools.partial(_merge_kernel, n_cast=len(to_cast)),
        grid=(t // tl,),
        in_specs=[tok(d),
                  pl.BlockSpec((1, N_MOD, d), lambda i: (i // tpb, 0, 0)),
                  _const_spec((1, d)), _const_spec((1, d)),
                  tok(V_WIDTH), tok(V_WIDTH), tok(F_WIDTH),
                  _const_spec(w_bg.shape), _const_spec((1, 2 * d)),
                  _const_spec(w_ro.shape), _const_spec(w_fo.shape),
                  _const_spec(w_out.shape)] + cast_specs,
        out_specs=[tok(d), tok(d)] + cast_specs,
        out_shape=[jax.ShapeDtypeStruct((t, d), F32),
                   jax.ShapeDtypeStruct((t, d), BF16)] + cast_shapes,
        compiler_params=_params(("arbitrary",)),
        name="merge",
    )(x2, mod3, norm_w, norm2_w, ro, sg, z, w_bg, b_bg, w_ro, w_fo, w_out, *to_cast)
    return out[:2], out[2:]


HALO = 8
FFN_LEAD = 16
FFN_SUB = 256


def _two_gelu_tanh(x):
    c1 = np.sqrt(2.0 / np.pi)
    return x + x * jnp.tanh(x * (c1 + (c1 * 0.044715) * (x * x)))


def _ffn_kernel(xm_ref, hm_ref, xp_ref, xn_ref, mod_ref, nw_ref, wu_ref, cw_ref, cb_ref,
                wd_ref, fnw_ref, o_ref, h_scr, u_scr, act_scr, y_scr, *, tiles_per_seq):
    tl = xm_ref.shape[0]
    sub = FFN_SUB
    half = sub // 2
    nc = FFN_CHUNK
    i = pl.program_id(0)
    keep_prev = ((i % tiles_per_seq) != 0).astype(F32)
    keep_next = ((i % tiles_per_seq) != tiles_per_seq - 1).astype(F32)

    def pre(x):
        return _mod_norm(x, nw_ref[...], mod_ref[0, 3:4, :], mod_ref[0, 4:5, :])

    pad = jnp.zeros((FFN_LEAD - HALO, xm_ref.shape[1]), F32)
    h_scr[0:FFN_LEAD] = jnp.concatenate([pad, pre(xp_ref[...]) * keep_prev], axis=0).astype(BF16)
    h_scr[FFN_LEAD:FFN_LEAD + tl] = hm_ref[...]
    h_scr[FFN_LEAD + tl:] = jnp.concatenate([pre(xn_ref[...]) * keep_next, pad], axis=0).astype(BF16)

    def project(j):
        hb = h_scr[...]
        for part in range(2):
            lo = part * FFN_DIM + j * nc
            u = _dot(hb, wu_ref[:, lo:lo + nc])
            for s in range(nc // LANES):
                u_scr[j % 2, part, s] = u[:, s * LANES:(s + 1) * LANES]

    def conv(j, part, s, blk, scale):
        lo = part * FFN_DIM + j * nc + s * LANES
        w = cw_ref[:, lo:lo + LANES] * scale
        bias = cb_ref[:, lo:lo + LANES] * scale
        first = FFN_LEAD + blk * sub
        rows = lambda start: u_scr[j % 2, part, s, pl.ds(start, half, stride=2), :]
        before, even, odd, after = rows(first - 1), rows(first), rows(first + 1), rows(first + 2)
        return (before * w[0:1] + even * w[1:2] + odd * w[2:3] + bias,
                even * w[0:1] + odd * w[1:2] + after * w[2:3] + bias)

    def activate(j):
        for s in range(nc // LANES):
            cols = slice(j * nc + s * LANES, j * nc + (s + 1) * LANES)
            for blk in range(tl // sub):
                gate = conv(j, 0, s, blk, 1.0)
                val = conv(j, 1, s, blk, 0.5)
                for par in range(2):
                    r0 = blk * sub + par * half
                    act_scr[r0:r0 + half, cols] = (_two_gelu_tanh(gate[par]) * val[par]).astype(BF16)

    project(0)
    for j in range(N_FFN_CHUNKS):
        if j + 1 < N_FFN_CHUNKS:
            project(j + 1)
        activate(j)
    for blk in range(tl // sub):
        r0 = blk * sub
        y = _dot(act_scr[r0:r0 + sub, :], wd_ref[...])
        for s in range(y_scr.shape[0]):
            for par in range(2):
                y_scr[s, pl.ds(par, half, stride=2), :] = (
                    y[par * half:(par + 1) * half, s * LANES:(s + 1) * LANES])
        y = jnp.concatenate([y_scr[s] for s in range(y_scr.shape[0])], axis=1)
        x2 = xm_ref[r0:r0 + sub, :] + mod_ref[0, 5:6, :] * y
        o_ref[r0:r0 + sub, :] = _rms_norm(x2, fnw_ref[...])


def _ffn(x1, h2, mod3, norm_w, w_u, cw, cb, w_d, fnorm_w, seq):
    t, d = x1.shape
    tl = FFN_TILE
    tpb = seq // tl
    hb = tl // HALO
    last = t // HALO - 1
    rows = FFN_LEAD + tl + FFN_LEAD
    return pl.pallas_call(
        functools.partial(_ffn_kernel, tiles_per_seq=tpb),
        grid=(t // tl,),
        in_specs=[pl.BlockSpec((tl, d), lambda i: (i, 0)),
                  pl.BlockSpec((tl, d), lambda i: (i, 0)),
                  pl.BlockSpec((HALO, d), lambda i: (jnp.maximum(i * hb - 1, 0), 0)),
                  pl.BlockSpec((HALO, d), lambda i: (jnp.minimum((i + 1) * hb, last), 0)),
                  pl.BlockSpec((1, N_MOD, d), lambda i: (i // tpb, 0, 0)),
                  _const_spec((1, d)),
                  _const_spec(w_u.shape), _const_spec(cw.shape), _const_spec(cb.shape),
                  _const_spec(w_d.shape), _const_spec((1, d))],
        out_specs=pl.BlockSpec((tl, d), lambda i: (i, 0)),
        out_shape=jax.ShapeDtypeStruct((t, d), F32),
        scratch_shapes=[pltpu.VMEM((rows, d), BF16),
                        pltpu.VMEM((2, 2, FFN_CHUNK // LANES, rows, LANES), F32),
                        pltpu.VMEM((tl, FFN_DIM), BF16),
                        pltpu.VMEM((d // LANES, FFN_SUB, LANES), F32)],
        compiler_params=_params(("parallel",)),
        name="ffn",
    )(x1, h2, x1, x1, mod3, norm_w, w_u, cw, cb, w_d, fnorm_w)


def _rope_tables(seq):
    n_freq = QK_DIM // 4
    lane = jnp.arange(LANES)
    inv = ROPE_BASE ** (-(lane % n_freq).astype(F32) / n_freq)
    sign = jnp.where(lane < LANES // 2, -1.0, 1.0).astype(F32)

    def table(n):
        ang = jnp.arange(n, dtype=F32)[:, None] * inv[None, :]
        return jnp.stack([jnp.cos(ang), jnp.sin(ang) * sign])

    return table(seq // GRID_W), table(GRID_W)


def kernel(x, c, ctx, c_ctx, w_mod, b_mod, norm1_w, w_in, ret_decay_f, ret_decay_b,
           w_ret_out, w_four_out, w_branch_gate, b_branch_gate, w_out, norm2_w,
           w_up, conv_w, conv_b, w_down, final_norm_w):
    assert w_mod.shape[0] == 1, "single-layer block"
    b, seq, d = x.shape
    t = b * seq
    assert b == 2 and seq % RET_CHUNK == 0
    assert seq % FFN_TILE == 0 and seq % MERGE_TILE == 0 and seq % PROJ_TILE == 0

    c8 = jnp.concatenate([c, c_ctx[None, :], jnp.zeros((8 - b - 1, d), F32)], axis=0)
    mod, w_in_b, w_qk = _modulation(c8, w_mod[0], b_mod[0], w_in[0])
    mod3 = mod.reshape(8, N_MOD, d)
    n1w = norm1_w[0].reshape(1, d)

    kc, vc = _ctx_proj(ctx, mod3, n1w, w_qk, w_in_b)
    rtab, ctab = _rope_tables(seq)
    x2 = x.reshape(t, d)
    (q, k, v, sg, f), (w_bg, w_ro, w_fo, w_o) = _proj(
        x2, mod3, n1w, w_qk, w_in_b, rtab, ctab, seq,
        [w_branch_gate[0], w_ret_out[0], w_four_out[0], w_out[0]])

    a_f = jnp.broadcast_to(ret_decay_f[0][:, None, None], (HEADS, 1, LANES))
    a_b = jnp.broadcast_to(ret_decay_b[0][:, None, None], (HEADS, 1, LANES))
    y, tw, w2, w_c = _dft_first_stage(f.reshape(b, seq, F_WIDTH))
    ro, z = _retention_and_dft(a_f, a_b, q.reshape(b, seq, QK_WIDTH), k.reshape(b, seq, QK_WIDTH),
                               v.reshape(b, seq, V_WIDTH), kc, vc, y, tw, w2, w_c)

    n2w = norm2_w[0].reshape(1, d)
    (x1, h2), (w_u, w_d) = _merge(
        x2, mod3, n1w, n2w, ro.reshape(t, V_WIDTH), sg, z.reshape(t, F_WIDTH),
        w_bg, b_branch_gate[0].reshape(1, 2 * d), w_ro, w_fo, w_o, seq, [w_up[0], w_down[0]])

    out = _ffn(x1, h2, mod3, n2w, w_u, conv_w[0], conv_b[0].reshape(1, 2 * FFN_DIM), w_d,
               final_norm_w.reshape(1, d), seq)
    return out.reshape(b, seq, d)
```

```python
import functools

import numpy as np
import jax
import jax.numpy as jnp
from jax import lax
from jax.experimental import pallas as pl
from jax.experimental.pallas import tpu as pltpu

F32 = jnp.float32
BF16 = jnp.bfloat16

D_MODEL = 1024
GRID_W = 64
HEADS = 8
QK_DIM = 64
V_DIM = 128
QK_WIDTH = HEADS * QK_DIM
V_WIDTH = HEADS * V_DIM
ROPE_BASE = 10000.0
F_GROUPS = 4
F_GROUP_DIM = 128
F_WIDTH = F_GROUPS * F_GROUP_DIM
K_OFF = QK_WIDTH
V_OFF = K_OFF + QK_WIDTH
G_OFF = V_OFF + V_WIDTH
F_OFF = G_OFF + V_WIDTH
IN_COLS = F_OFF + F_WIDTH
FFN_DIM = 2816
N_MOD = 6
EPS = 1e-6

LANES = 128
RET_CHUNK = 256
RET_UNROLL = 16
FFN_CHUNK = 256
N_FFN_CHUNKS = FFN_DIM // FFN_CHUNK
FFN_TILE = 512
MERGE_TILE = 1024
MERGE_SUB = 512
PROJ_TILE = 1024
PROJ_SUB = 256
DFT_N1 = 64
BF16_ROWS = 16
DFT_ROWS = BF16_ROWS
DFT_A_ROWS = 2 * DFT_ROWS
VMEM_LIMIT = 58 * 1024 * 1024


def _params(sem):
    return pltpu.CompilerParams(dimension_semantics=sem, vmem_limit_bytes=VMEM_LIMIT)


def _dot(a, b):
    return jnp.dot(a, b, preferred_element_type=F32)


def _rms_norm(x, w):
    return x * lax.rsqrt(jnp.mean(x * x, axis=-1, keepdims=True) + EPS) * w


def _mod_norm(x, w, shift, scale):
    return x * lax.rsqrt(jnp.mean(x * x, axis=-1, keepdims=True) + EPS) * (w * (1.0 + scale)) + shift


def _const_spec(shape):
    zeros = (0,) * len(shape)
    return pl.BlockSpec(shape, lambda *_: zeros, pipeline_mode=pl.Buffered(1))


def _cast_jobs(weights, steps):
    specs, shapes = [], []
    for w in weights:
        n_rows = w.shape[0]
        rows = -(-n_rows // steps)
        while rows % BF16_ROWS or n_rows % rows:
            rows += 1
        last = n_rows // rows - 1
        specs.append(pl.BlockSpec((rows, w.shape[1]), lambda i, last=last: (jnp.minimum(i, last), 0)))
        shapes.append(jax.ShapeDtypeStruct(w.shape, BF16))
    return specs, shapes


def _run_casts(refs):
    n = len(refs) // 2
    for src, dst in zip(refs[:n], refs[n:]):
        dst[...] = src[...].astype(BF16)


def _mod_kernel(c_ref, w_ref, b_ref, win_ref, o_ref, winb_ref, wqk_ref):
    c = c_ref[...]
    s = c * jax.nn.sigmoid(c)
    o_ref[...] = _dot(s.astype(BF16), w_ref[...].astype(BF16)) + b_ref[...]
    wb = win_ref[...].astype(BF16)
    winb_ref[...] = wb
    n_qk = 2 * QK_WIDTH
    src = lax.broadcasted_iota(jnp.int32, (n_qk, n_qk), 0)
    dst = lax.broadcasted_iota(jnp.int32, (n_qk, n_qk), 1)
    select = (src == _qk_source_column(dst)).astype(BF16)
    wqk_ref[...] = _dot(wb[:, :n_qk], select).astype(BF16)


def _qk_source_column(col):
    half = QK_DIM // 2
    lane = col % LANES
    head = 2 * ((col % QK_WIDTH) // LANES) + (lane // half) % 2
    return (col // QK_WIDTH) * QK_WIDTH + head * QK_DIM + (lane // (2 * half)) * half + lane % half


def _modulation(c8, w_mod, b_mod, w_in):
    n = w_mod.shape[1]
    steps = 4
    tn = n // steps
    d, n_in = w_in.shape
    rows = d // steps
    return pl.pallas_call(
        _mod_kernel,
        grid=(steps,),
        in_specs=[_const_spec((8, D_MODEL)),
                  pl.BlockSpec((D_MODEL, tn), lambda j: (0, j)),
                  pl.BlockSpec((1, tn), lambda j: (0, j)),
                  pl.BlockSpec((rows, n_in), lambda j: (j, 0))],
        out_specs=[pl.BlockSpec((8, tn), lambda j: (0, j)),
                   pl.BlockSpec((rows, n_in), lambda j: (j, 0)),
                   pl.BlockSpec((rows, 2 * QK_WIDTH), lambda j: (j, 0))],
        out_shape=[jax.ShapeDtypeStruct((8, n), F32),
                   jax.ShapeDtypeStruct((d, n_in), BF16),
                   jax.ShapeDtypeStruct((d, 2 * QK_WIDTH), BF16)],
        compiler_params=_params(("parallel",)),
        name="mod",
    )(c8, w_mod, b_mod.reshape(1, n), w_in)


def _ctx_kernel(x_ref, mod_ref, nw_ref, wk_ref, wv_ref, k_ref, v_ref):
    x = x_ref[0]
    h = _mod_norm(x, nw_ref[...], mod_ref[0, 0:1, :], mod_ref[0, 1:2, :])
    hb = h.astype(BF16)
    k_ref[0] = _dot(hb, wk_ref[...]).astype(BF16)
    v_ref[0] = _dot(hb, wv_ref[...]).astype(BF16)


def _ctx_proj(ctx, mod3, norm_w, w_qk, w_in):
    b, lc, d = ctx.shape
    assert V_OFF % V_WIDTH == 0
    return pl.pallas_call(
        _ctx_kernel,
        grid=(b,),
        in_specs=[pl.BlockSpec((1, lc, d), lambda i: (i, 0, 0)),
                  pl.BlockSpec((1, N_MOD, d), lambda i: (2, 0, 0)),
                  _const_spec((1, d)),
                  pl.BlockSpec((d, QK_WIDTH), lambda i: (0, 1), pipeline_mode=pl.Buffered(1)),
                  pl.BlockSpec((d, V_WIDTH), lambda i: (0, V_OFF // V_WIDTH),
                               pipeline_mode=pl.Buffered(1))],
        out_specs=[pl.BlockSpec((1, lc, QK_WIDTH), lambda i: (i, 0, 0)),
                   pl.BlockSpec((1, lc, V_WIDTH), lambda i: (i, 0, 0))],
        out_shape=[jax.ShapeDtypeStruct((b, lc, QK_WIDTH), BF16),
                   jax.ShapeDtypeStruct((b, lc, V_WIDTH), BF16)],
        compiler_params=_params(("parallel",)),
        name="ctx_proj",
    )(ctx, mod3, norm_w, w_qk, w_in)


def _proj_kernel(x_ref, mod_ref, nw_ref, wqk_ref, w_ref, rtab_ref, ctab_ref, *refs, n_cast):
    q_ref, k_ref, v_ref, sg_ref, f_ref, h_ref = refs[n_cast:n_cast + 6]
    _run_casts(refs[:n_cast] + refs[n_cast + 6:])
    lane = lax.broadcasted_iota(jnp.int32, (1, LANES), 1)
    by_row = (lane % (QK_DIM // 2)) < QK_DIM // 4

    def rope(t, trig, scale, out_ref, rows):
        for j in range(QK_WIDTH // LANES):
            tj = t[:, j * LANES:(j + 1) * LANES]
            r = tj * trig[0] + pltpu.roll(tj, LANES // 2, 1) * trig[1]
            out_ref[rows, j * LANES:(j + 1) * LANES] = (r * scale).astype(BF16)

    for r0 in range(0, x_ref.shape[0], PROJ_SUB):
        rows = slice(r0, r0 + PROJ_SUB)
        trig = []
        for cs in range(2):
            trig.append(jnp.concatenate(
                [jnp.where(by_row, rtab_ref[cs, g:g + 1, :], ctab_ref[cs])
                 for g in range(r0 // GRID_W, (r0 + PROJ_SUB) // GRID_W)], axis=0))
        x = x_ref[rows, :]
        h = _mod_norm(x, nw_ref[...], mod_ref[0, 0:1, :], mod_ref[0, 1:2, :])
        hb = h.astype(BF16)
        h_ref[rows, :] = hb
        rope(_dot(hb, wqk_ref[:, :QK_WIDTH]), trig, QK_DIM ** -0.5, q_ref, rows)
        rope(_dot(hb, wqk_ref[:, QK_WIDTH:]), trig, 1.0, k_ref, rows)
        v_ref[rows, :] = _dot(hb, w_ref[:, V_OFF:G_OFF]).astype(BF16)
        g = _dot(hb, w_ref[:, G_OFF:F_OFF])
        hg = 0.5 * g
        sg_ref[rows, :] = (hg * jnp.tanh(hg) + hg).astype(BF16)
        f_ref[rows, :] = _dot(hb, w_ref[:, F_OFF:IN_COLS]).astype(BF16)


def _proj(x2, mod3, norm_w, w_qk, w_in, rtab, ctab, seq, to_cast):
    t, d = x2.shape
    tl = PROJ_TILE
    tpb = seq // tl
    tok = lambda w: pl.BlockSpec((tl, w), lambda i: (i, 0))
    cast_specs, cast_shapes = _cast_jobs(to_cast, t // tl)
    out = pl.pallas_call(
        functools.partial(_proj_kernel, n_cast=len(to_cast)),
        grid=(t // tl,),
        in_specs=[tok(d),
                  pl.BlockSpec((1, N_MOD, d), lambda i: (i // tpb, 0, 0)),
                  _const_spec((1, d)),
                  _const_spec(w_qk.shape), _const_spec(w_in.shape),
                  pl.BlockSpec((2, tl // GRID_W, LANES), lambda i: (0, i % tpb, 0)),
                  _const_spec(ctab.shape)] + cast_specs,
        out_specs=[tok(QK_WIDTH), tok(QK_WIDTH), tok(V_WIDTH), tok(V_WIDTH),
                   tok(F_WIDTH), tok(d)] + cast_specs,
        out_shape=[jax.ShapeDtypeStruct((t, QK_WIDTH), BF16),
                   jax.ShapeDtypeStruct((t, QK_WIDTH), BF16),
                   jax.ShapeDtypeStruct((t, V_WIDTH), BF16),
                   jax.ShapeDtypeStruct((t, V_WIDTH), BF16),
                   jax.ShapeDtypeStruct((t, F_WIDTH), BF16),
                   jax.ShapeDtypeStruct((t, d), BF16)] + cast_shapes,
        compiler_params=_params(("arbitrary",)),
        name="proj",
    )(x2, mod3, norm_w, w_qk, w_in, rtab, ctab, *to_cast)
    return out[:6], out[6:]


def _retention_kernel(af_ref, ab_ref, q_ref, k_ref, v_ref, kc_ref, vc_ref,
                      tw_ref, w2_ref, cc_ref, y_ref,
                      o_ref, z_ref, ds_scr, st_scr, d_scr, tab_scr, z_scr):
    c = RET_CHUNK
    seq = q_ref.shape[1]
    lc = kc_ref.shape[1]
    n = seq // c
    lgf = [-jnp.exp(af_ref[hh]) for hh in range(2)]
    lgb = [-jnp.exp(ab_ref[hh]) for hh in range(2)]
    half = QK_DIM // 2

    lane_head = (lax.broadcasted_iota(jnp.int32, (1, LANES), 1) // half) % 2
    masks = [(lane_head == hh).astype(BF16) for hh in range(2)]
    lgf_lane = jnp.where(lane_head == 0, lgf[0], lgf[1])
    lgb_lane = jnp.where(lane_head == 0, lgb[0], lgb[1])
    row_head = (lax.broadcasted_iota(jnp.int32, (LANES, 1), 0) // half) % 2
    lgf_row = jnp.where(row_head == 0, lgf[0][:, 0:1], lgf[1][:, 0:1])
    lgb_row = jnp.where(row_head == 0, lgb[0][:, 0:1], lgb[1][:, 0:1])

    pos = lax.broadcasted_iota(jnp.int32, (c, LANES), 0).astype(F32)
    tab_scr[0] = jnp.exp(lgf_lane * (pos + 1.0))
    tab_scr[1] = jnp.exp(lgb_lane * (c - pos))
    diff = (lax.broadcasted_iota(jnp.int32, (c, c), 0)
            - lax.broadcasted_iota(jnp.int32, (c, c), 1)).astype(F32)
    for hh in range(2):
        d_scr[hh] = (jnp.where(diff >= 0, jnp.exp(lgf[hh][:, 0:1] * jnp.maximum(diff, 0.0)), 0.0)
                     + jnp.where(diff <= 0, jnp.exp(lgb[hh][:, 0:1] * jnp.maximum(-diff, 0.0)), 0.0))

    def k_decays(tokens):
        t = lax.broadcasted_iota(jnp.int32, (1, tokens), 1).astype(F32)
        return jnp.exp(lgf_row * (tokens - 1.0 - t)), jnp.exp(lgb_row * t)

    def state_increment(k_rows, v_rows, decays):
        kt = jnp.transpose(k_rows.astype(F32))
        lhs = jnp.concatenate([kt * decays[0], kt * decays[1]], axis=0).astype(BF16)
        return _dot(lhs, v_rows)

    kdec = k_decays(c)

    def incr(i, carry):
        rows = pl.ds(pl.multiple_of(i * c, c), c)
        ds_scr[i] = state_increment(k_ref[0, rows, :], v_ref[0, rows, :], kdec)
        return carry

    lax.fori_loop(0, n, incr, 0, unroll=RET_UNROLL)

    s0 = state_increment(kc_ref[0], vc_ref[0], k_decays(lc))
    col_head = lax.broadcasted_iota(jnp.int32, (1, 2 * V_DIM), 1) // V_DIM
    own = (row_head == col_head).astype(F32)
    decay_f = jnp.exp(lgf_row * c) * own
    decay_b = jnp.exp(lgb_row * c) * own

    def scan_f(i, s):
        st_scr[i, 0:LANES, :] = (s * own).astype(BF16)
        return decay_f * s + ds_scr[i, 0:LANES, :]

    lax.fori_loop(0, n, scan_f, s0[0:LANES])

    def scan_b(t, s):
        i = n - 1 - t
        st_scr[i, LANES:2 * LANES, :] = (s * own).astype(BF16)
        return decay_b * s + ds_scr[i, LANES:2 * LANES, :]

    lax.fori_loop(0, n, scan_b, s0[LANES:2 * LANES])

    def outputs(i):
        rows = pl.ds(pl.multiple_of(i * c, c), c)
        q = q_ref[0, rows, :]
        k = k_ref[0, rows, :]
        qf = q.astype(F32)
        qd = jnp.concatenate([qf * tab_scr[0], qf * tab_scr[1]], axis=1).astype(BF16)
        inter = _dot(qd, st_scr[i])
        for hh in range(2):
            cols = slice(hh * V_DIM, (hh + 1) * V_DIM)
            scores = lax.dot_general(q * masks[hh], k, (((1,), (1,)), ((), ())),
                                     preferred_element_type=F32)
            o = _dot((scores * d_scr[hh]).astype(BF16), v_ref[0, rows, cols]) + inter[:, cols]
            o_ref[0, rows, cols] = o.astype(BF16)

    per_k1 = n // DFT_ROWS
    cc = cc_ref[...].astype(BF16)

    def outputs_and_dft(j, carry):
        for u in range(per_k1):
            outputs(j * per_k1 + u)
        _dft_second_stage(j, tw_ref, w2_ref, cc, y_ref, z_scr)
        return carry

    lax.fori_loop(0, DFT_ROWS, outputs_and_dft, 0, unroll=RET_UNROLL // per_k1)
    _dft_emit(z_scr, z_ref)


def _retention_and_dft(a_f, a_b, q, k, v, kc, vc, y, tw, w2, w_c):
    b, seq, _ = v.shape
    lc = kc.shape[1]
    c = RET_CHUNK
    n = seq // c
    r = DFT_ROWS
    _, _, n1, n2, w = y.shape
    assert n % r == 0 and n1 // r == HEADS // 2
    dec = pl.BlockSpec((2, 1, LANES), lambda i, p: (p, 0, 0))
    qk = lambda rows: pl.BlockSpec((1, rows, LANES), lambda i, p: (i, 0, p))
    vv = lambda rows: pl.BlockSpec((1, rows, 2 * V_DIM), lambda i, p: (i, 0, p))
    return pl.pallas_call(
        _retention_kernel,
        grid=(b, HEADS // 2),
        in_specs=[dec, dec, qk(seq), qk(seq), vv(seq), qk(lc), vv(lc),
                  pl.BlockSpec((r, 2, n2), lambda i, p: (p, 0, 0)),
                  _const_spec((2, n2, n2)),
                  _const_spec((2 * F_GROUP_DIM, F_GROUP_DIM)),
                  pl.BlockSpec((1, 2, r, n2, w), lambda i, p: (i, 0, p, 0, 0))],
        out_specs=[vv(seq), pl.BlockSpec((1, n2, r, w), lambda i, p: (i, 0, p, 0))],
        out_shape=[jax.ShapeDtypeStruct((b, seq, V_WIDTH), BF16),
                   jax.ShapeDtypeStruct((b, n2, n1, w), BF16)],
        scratch_shapes=[pltpu.VMEM((n, 2 * LANES, 2 * V_DIM), F32),
                        pltpu.VMEM((n, 2 * LANES, 2 * V_DIM), BF16),
                        pltpu.VMEM((2, c, c), F32),
                        pltpu.VMEM((2, c, LANES), F32),
                        pltpu.VMEM((F_GROUPS, n2 * r, F_GROUP_DIM), F32)],
        compiler_params=_params(("parallel", "parallel")),
        name="retention_dft",
    )(a_f, a_b, q, k, v, kc, vc, tw, w2, w_c, y)


def _dft_a_kernel(w_ref, x_ref, y_ref):
    _, n1, rows, w = x_ref.shape
    r = DFT_ROWS
    nh = n1 // 2 + 1
    for r0 in range(0, rows, r):
        x = x_ref[0, :, r0:r0 + r, :].reshape(n1 * r, w)
        y = _dot(w_ref[...], x).reshape(2, nh, r, w).astype(BF16)
        y_ref[0, :, 0:nh, r0:r0 + r, :] = y
        for k1 in range(1, n1 // 2):
            y_ref[0, 0, n1 - k1, r0:r0 + r, :] = y[0, k1]
            y_ref[0, 1, n1 - k1, r0:r0 + r, :] = -y[1, k1]


def _dft_second_stage(j, tw_ref, w2_ref, cc, y_ref, z_scr):
    n2 = y_ref.shape[3]
    w2c = w2_ref[0]
    w2s = w2_ref[1]
    tc = tw_ref[j, 0:1, :]
    ts = tw_ref[j, 1:2, :]
    ec = w2c * tc - w2s * ts
    es = w2s * tc + w2c * ts
    m = jnp.concatenate([jnp.concatenate([ec, -es], axis=1),
                         jnp.concatenate([es, ec], axis=1)], axis=0).astype(BF16)
    y = jnp.concatenate([y_ref[0, 0, j], y_ref[0, 1, j]], axis=0)
    zz = _dot(m, y)
    for g in range(F_GROUPS):
        cols = slice(g * F_GROUP_DIM, (g + 1) * F_GROUP_DIM)
        zcs = jnp.concatenate([zz[:n2, cols], zz[n2:, cols]], axis=1).astype(BF16)
        z_scr[g, pl.ds(j, n2, stride=DFT_ROWS), :] = _dot(zcs, cc)


def _dft_emit(z_scr, z_ref):
    n2 = z_ref.shape[1]
    z = jnp.concatenate([z_scr[g] for g in range(F_GROUPS)], axis=1)
    z_ref[0] = z.reshape(n2, DFT_ROWS, F_WIDTH).astype(BF16)


def _dft_tables(seq):
    n1 = DFT_N1
    n2 = seq // n1

    def cs(num, den):
        ang = 2.0 * np.pi * (num % den) / den
        return np.cos(ang), np.sin(ang)

    a = np.arange(n1)
    w_a = np.concatenate(cs(a[:n1 // 2 + 1, None] * a[None, :], n1), axis=0)
    m = np.arange(n2)
    tw = np.stack(cs(a[:, None] * m[None, :], seq), axis=1)
    w2 = np.stack(cs(m[:, None] * m[None, :], n2), axis=0)
    ch = np.arange(F_GROUP_DIM)
    cc, sc = cs(ch[:, None] * ch[None, :], F_GROUP_DIM)
    scale = 1.0 / np.sqrt(seq * F_GROUP_DIM)
    w_c = np.concatenate([cc, -sc], axis=0) * scale
    return [jnp.asarray(t, dtype=F32) for t in (w_a, tw, w2, w_c)]


def _dft_first_stage(f):
    b, seq, w = f.shape
    n1 = DFT_N1
    n2 = seq // n1
    w_a, tw, w2, w_c = _dft_tables(seq)
    r = DFT_ROWS
    spread = (jnp.arange(n1 * r)[None, :] // r == jnp.arange(n1)[:, None]).astype(F32)
    w_rep = jnp.dot(w_a, spread, precision=lax.Precision.HIGHEST)
    same_r = jnp.arange(w_a.shape[0] * r)[:, None] % r == jnp.arange(n1 * r)[None, :] % r
    w_a = jnp.where(same_r, jnp.repeat(w_rep, r, axis=0), 0.0).astype(BF16)
    y = pl.pallas_call(
        _dft_a_kernel,
        grid=(b, n2 // DFT_A_ROWS),
        in_specs=[_const_spec(w_a.shape),
                  pl.BlockSpec((1, n1, DFT_A_ROWS, w), lambda i, j: (i, 0, j, 0))],
        out_specs=pl.BlockSpec((1, 2, n1, DFT_A_ROWS, w), lambda i, j: (i, 0, 0, j, 0)),
        out_shape=jax.ShapeDtypeStruct((b, 2, n1, n2, w), BF16),
        compiler_params=_params(("parallel", "parallel")),
        name="dft_a",
    )(w_a, f.reshape(b, n1, n2, w))
    return y, tw, w2, w_c


def _merge_kernel(x_ref, mod_ref, h_ref, n2w_ref, ro_ref, sg_ref, z_ref, wbg_ref, bbg_ref,
                  wro_ref, wfo_ref, wout_ref, *refs, n_cast):
    o_ref, h2_ref = refs[n_cast:n_cast + 2]
    _run_casts(refs[:n_cast] + refs[n_cast + 2:])
    for r0 in range(0, x_ref.shape[0], MERGE_SUB):
        rows = slice(r0, r0 + MERGE_SUB)
        x = x_ref[rows, :]
        t = jnp.tanh(0.5 * _dot(h_ref[rows, :], wbg_ref[...]) + 0.5 * bbg_ref[...])
        gated = []
        for hd in range(HEADS):
            cols = slice(hd * V_DIM, (hd + 1) * V_DIM)
            o = ro_ref[rows, cols].astype(F32)
            oc = o - jnp.mean(o, axis=-1, keepdims=True)
            var = jnp.mean(oc * oc, axis=-1, keepdims=True)
            gated.append((oc * lax.rsqrt(var + EPS) * sg_ref[rows, cols].astype(F32)).astype(BF16))
        ret_d = _dot(jnp.concatenate(gated, axis=1), wro_ref[...])
        four_d = _dot(z_ref[rows, :], wfo_ref[...])
        m2 = (t[:, :D_MODEL] * ret_d + ret_d) + (t[:, D_MODEL:] * four_d + four_d)
        y2 = _dot(m2.astype(BF16), wout_ref[...])
        x1 = x + (0.5 * mod_ref[0, 2:3, :]) * y2
        o_ref[rows, :] = x1
        h2 = _mod_norm(x1, n2w_ref[...], mod_ref[0, 3:4, :], mod_ref[0, 4:5, :])
        h2_ref[rows, :] = h2.astype(BF16)


def _merge(x2, mod3, h, norm2_w, ro, sg, z, w_bg, b_bg, w_ro, w_fo, w_out, seq, to_cast):
    t, d = x2.shape
    tl = MERGE_TILE
    tpb = seq // tl
    tok = lambda w: pl.BlockSpec((tl, w), lambda i: (i, 0))
    cast_specs, cast_shapes = _cast_jobs(to_cast, t // tl)
    out = pl.pallas_call(
        functools.partial(_merge_kernel, n_cast=len(to_cast)),
        grid=(t // tl,),
        in_specs=[tok(d),
                  pl.BlockSpec((1, N_MOD, d), lambda i: (i // tpb, 0, 0)),
                  tok(d), _const_spec((1, d)),
                  tok(V_WIDTH), tok(V_WIDTH), tok(F_WIDTH),
                  _const_spec(w_bg.shape), _const_spec((1, 2 * d)),
                  _const_spec(w_ro.shape), _const_spec(w_fo.shape),
                  _const_spec(w_out.shape)] + cast_specs,
        out_specs=[tok(d), tok(d)] + cast_specs,
        out_shape=[jax.ShapeDtypeStruct((t, d), F32),
                   jax.ShapeDtypeStruct((t, d), BF16)] + cast_shapes,
        compiler_params=_params(("arbitrary",)),
        name="merge",
    )(x2, mod3, h, norm2_w, ro, sg, z, w_bg, b_bg, w_ro, w_fo, w_out, *to_cast)
    return out[:2], out[2:]


HALO = 8
FFN_LEAD = 16
FFN_SUB = 256


def _two_gelu_tanh(x):
    c1 = np.sqrt(2.0 / np.pi)
    return x + x * jnp.tanh(x * (c1 + (c1 * 0.044715) * (x * x)))


def _ffn_kernel(xm_ref, hm_ref, xp_ref, xn_ref, mod_ref, nw_ref, wu_ref, cw_ref, cb_ref,
                wd_ref, fnw_ref, o_ref, h_scr, u_scr, act_scr, y_scr, *, tiles_per_seq):
    tl = xm_ref.shape[0]
    sub = FFN_SUB
    half = sub // 2
    nc = FFN_CHUNK
    i = pl.program_id(0)
    keep_prev = ((i % tiles_per_seq) != 0).astype(F32)
    keep_next = ((i % tiles_per_seq) != tiles_per_seq - 1).astype(F32)

    def pre(x):
        return _mod_norm(x, nw_ref[...], mod_ref[0, 3:4, :], mod_ref[0, 4:5, :])

    pad = jnp.zeros((FFN_LEAD - HALO, xm_ref.shape[1]), F32)
    h_scr[0:FFN_LEAD] = jnp.concatenate([pad, pre(xp_ref[...]) * keep_prev], axis=0).astype(BF16)
    h_scr[FFN_LEAD:FFN_LEAD + tl] = hm_ref[...]
    h_scr[FFN_LEAD + tl:] = jnp.concatenate([pre(xn_ref[...]) * keep_next, pad], axis=0).astype(BF16)

    def project(j):
        hb = h_scr[...]
        for part in range(2):
            lo = part * FFN_DIM + j * nc
            u = _dot(hb, wu_ref[:, lo:lo + nc])
            for s in range(nc // LANES):
                u_scr[j % 2, part, s] = u[:, s * LANES:(s + 1) * LANES]

    def conv(j, part, s, blk, scale):
        lo = part * FFN_DIM + j * nc + s * LANES
        w = cw_ref[:, lo:lo + LANES] * scale
        bias = cb_ref[:, lo:lo + LANES] * scale
        first = FFN_LEAD + blk * sub
        rows = lambda start: u_scr[j % 2, part, s, pl.ds(start, half, stride=2), :]
        before, even, odd, after = rows(first - 1), rows(first), rows(first + 1), rows(first + 2)
        return (before * w[0:1] + even * w[1:2] + odd * w[2:3] + bias,
                even * w[0:1] + odd * w[1:2] + after * w[2:3] + bias)

    def activate(j):
        for s in range(nc // LANES):
            cols = slice(j * nc + s * LANES, j * nc + (s + 1) * LANES)
            for blk in range(tl // sub):
                gate = conv(j, 0, s, blk, 1.0)
                val = conv(j, 1, s, blk, 0.5)
                for par in range(2):
                    r0 = blk * sub + par * half
                    act_scr[r0:r0 + half, cols] = (_two_gelu_tanh(gate[par]) * val[par]).astype(BF16)

    project(0)
    for j in range(N_FFN_CHUNKS):
        if j + 1 < N_FFN_CHUNKS:
            project(j + 1)
        activate(j)
    for blk in range(tl // sub):
        r0 = blk * sub
        y = _dot(act_scr[r0:r0 + sub, :], wd_ref[...])
        for s in range(y_scr.shape[0]):
            for par in range(2):
                y_scr[s, pl.ds(par, half, stride=2), :] = (
                    y[par * half:(par + 1) * half, s * LANES:(s + 1) * LANES])
        y = jnp.concatenate([y_scr[s] for s in range(y_scr.shape[0])], axis=1)
        x2 = xm_ref[r0:r0 + sub, :] + mod_ref[0, 5:6, :] * y
        o_ref[r0:r0 + sub, :] = _rms_norm(x2, fnw_ref[...])


def _ffn(x1, h2, mod3, norm_w, w_u, cw, cb, w_d, fnorm_w, seq):
    t, d = x1.shape
    tl = FFN_TILE
    tpb = seq // tl
    hb = tl // HALO
    last = t // HALO - 1
    rows = FFN_LEAD + tl + FFN_LEAD
    return pl.pallas_call(
        functools.partial(_ffn_kernel, tiles_per_seq=tpb),
        grid=(t // tl,),
        in_specs=[pl.BlockSpec((tl, d), lambda i: (i, 0)),
                  pl.BlockSpec((tl, d), lambda i: (i, 0)),
                  pl.BlockSpec((HALO, d), lambda i: (jnp.maximum(i * hb - 1, 0), 0)),
                  pl.BlockSpec((HALO, d), lambda i: (jnp.minimum((i + 1) * hb, last), 0)),
                  pl.BlockSpec((1, N_MOD, d), lambda i: (i // tpb, 0, 0)),
                  _const_spec((1, d)),
                  _const_spec(w_u.shape), _const_spec(cw.shape), _const_spec(cb.shape),
                  _const_spec(w_d.shape), _const_spec((1, d))],
        out_specs=pl.BlockSpec((tl, d), lambda i: (i, 0)),
        out_shape=jax.ShapeDtypeStruct((t, d), F32),
        scratch_shapes=[pltpu.VMEM((rows, d), BF16),
                        pltpu.VMEM((2, 2, FFN_CHUNK // LANES, rows, LANES), F32),
                        pltpu.VMEM((tl, FFN_DIM), BF16),
                        pltpu.VMEM((d // LANES, FFN_SUB, LANES), F32)],
        compiler_params=_params(("parallel",)),
        name="ffn",
    )(x1, h2, x1, x1, mod3, norm_w, w_u, cw, cb, w_d, fnorm_w)


def _rope_tables(seq):
    n_freq = QK_DIM // 4
    lane = jnp.arange(LANES)
    inv = ROPE_BASE ** (-(lane % n_freq).astype(F32) / n_freq)
    sign = jnp.where(lane < LANES // 2, -1.0, 1.0).astype(F32)

    def table(n):
        ang = jnp.arange(n, dtype=F32)[:, None] * inv[None, :]
        return jnp.stack([jnp.cos(ang), jnp.sin(ang) * sign])

    return table(seq // GRID_W), table(GRID_W)


def kernel(x, c, ctx, c_ctx, w_mod, b_mod, norm1_w, w_in, ret_decay_f, ret_decay_b,
           w_ret_out, w_four_out, w_branch_gate, b_branch_gate, w_out, norm2_w,
           w_up, conv_w, conv_b, w_down, final_norm_w):
    assert w_mod.shape[0] == 1, "single-layer block"
    b, seq, d = x.shape
    t = b * seq
    assert b == 2 and seq % RET_CHUNK == 0
    assert seq % FFN_TILE == 0 and seq % MERGE_TILE == 0 and seq % PROJ_TILE == 0

    c8 = jnp.concatenate([c, c_ctx[None, :], jnp.zeros((8 - b - 1, d), F32)], axis=0)
    mod, w_in_b, w_qk = _modulation(c8, w_mod[0], b_mod[0], w_in[0])
    mod3 = mod.reshape(8, N_MOD, d)
    n1w = norm1_w[0].reshape(1, d)

    kc, vc = _ctx_proj(ctx, mod3, n1w, w_qk, w_in_b)
    rtab, ctab = _rope_tables(seq)
    x2 = x.reshape(t, d)
    (q, k, v, sg, f, h), (w_bg, w_ro, w_fo, w_o) = _proj(
        x2, mod3, n1w, w_qk, w_in_b, rtab, ctab, seq,
        [w_branch_gate[0], w_ret_out[0], w_four_out[0], w_out[0]])

    a_f = jnp.broadcast_to(ret_decay_f[0][:, None, None], (HEADS, 1, LANES))
    a_b = jnp.broadcast_to(ret_decay_b[0][:, None, None], (HEADS, 1, LANES))
    y, tw, w2, w_c = _dft_first_stage(f.reshape(b, seq, F_WIDTH))
    ro, z = _retention_and_dft(a_f, a_b, q.reshape(b, seq, QK_WIDTH), k.reshape(b, seq, QK_WIDTH),
                               v.reshape(b, seq, V_WIDTH), kc, vc, y, tw, w2, w_c)

    n2w = norm2_w[0].reshape(1, d)
    (x1, h2), (w_u, w_d) = _merge(
        x2, mod3, h, n2w, ro.reshape(t, V_WIDTH), sg, z.reshape(t, F_WIDTH),
        w_bg, b_branch_gate[0].reshape(1, 2 * d), w_ro, w_fo, w_o, seq, [w_up[0], w_down[0]])

    out = _ffn(x1, h2, mod3, n2w, w_u, conv_w[0], conv_b[0].reshape(1, 2 * FFN_DIM), w_d,
               final_norm_w.reshape(1, d), seq)
    return out.reshape(b, seq, d)
```

```python
import functools

import numpy as np
import jax
import jax.numpy as jnp
from jax import lax
from jax.experimental import pallas as pl
from jax.experimental.pallas import tpu as pltpu

F32 = jnp.float32
BF16 = jnp.bfloat16

D_MODEL = 1024
GRID_W = 64
HEADS = 8
QK_DIM = 64
V_DIM = 128
QK_WIDTH = HEADS * QK_DIM
V_WIDTH = HEADS * V_DIM
ROPE_BASE = 10000.0
F_GROUPS = 4
F_GROUP_DIM = 128
F_WIDTH = F_GROUPS * F_GROUP_DIM
K_OFF = QK_WIDTH
V_OFF = K_OFF + QK_WIDTH
G_OFF = V_OFF + V_WIDTH
F_OFF = G_OFF + V_WIDTH
IN_COLS = F_OFF + F_WIDTH
FFN_DIM = 2816
N_MOD = 6
EPS = 1e-6

LANES = 128
RET_CHUNK = 256
RET_UNROLL = 16
FFN_CHUNK = 256
N_FFN_CHUNKS = FFN_DIM // FFN_CHUNK
FFN_TILE = 512
MERGE_TILE = 1024
MERGE_SUB = 512
PROJ_TILE = 1024
PROJ_SUB = 256
DFT_N1 = 64
BF16_ROWS = 16
DFT_ROWS = BF16_ROWS
DFT_A_ROWS = 2 * DFT_ROWS
VMEM_LIMIT = 56 * 1024 * 1024


def _params(sem):
    return pltpu.CompilerParams(dimension_semantics=sem, vmem_limit_bytes=VMEM_LIMIT)


def _dot(a, b):
    return jnp.dot(a, b, preferred_element_type=F32)


def _rms_norm(x, w):
    return x * lax.rsqrt(jnp.mean(x * x, axis=-1, keepdims=True) + EPS) * w


def _mod_norm(x, w, shift, scale):
    return x * lax.rsqrt(jnp.mean(x * x, axis=-1, keepdims=True) + EPS) * (w * (1.0 + scale)) + shift


def _const_spec(shape):
    zeros = (0,) * len(shape)
    return pl.BlockSpec(shape, lambda *_: zeros, pipeline_mode=pl.Buffered(1))


def _cast_jobs(weights, steps):
    specs, shapes = [], []
    for w in weights:
        n_rows = w.shape[0]
        rows = -(-n_rows // steps)
        while rows % BF16_ROWS or n_rows % rows:
            rows += 1
        last = n_rows // rows - 1
        specs.append(pl.BlockSpec((rows, w.shape[1]), lambda i, last=last: (jnp.minimum(i, last), 0)))
        shapes.append(jax.ShapeDtypeStruct(w.shape, BF16))
    return specs, shapes


def _run_casts(refs):
    n = len(refs) // 2
    for src, dst in zip(refs[:n], refs[n:]):
        dst[...] = src[...].astype(BF16)


def _mod_kernel(c_ref, w_ref, b_ref, win_ref, o_ref, winb_ref, wqk_ref):
    c = c_ref[...]
    s = c * jax.nn.sigmoid(c)
    o_ref[...] = _dot(s.astype(BF16), w_ref[...].astype(BF16)) + b_ref[...]
    wb = win_ref[...].astype(BF16)
    winb_ref[...] = wb
    n_qk = 2 * QK_WIDTH
    src = lax.broadcasted_iota(jnp.int32, (n_qk, n_qk), 0)
    dst = lax.broadcasted_iota(jnp.int32, (n_qk, n_qk), 1)
    select = (src == _qk_source_column(dst)).astype(BF16)
    wqk_ref[...] = _dot(wb[:, :n_qk], select).astype(BF16)


def _qk_source_column(col):
    half = QK_DIM // 2
    lane = col % LANES
    head = 2 * ((col % QK_WIDTH) // LANES) + (lane // half) % 2
    return (col // QK_WIDTH) * QK_WIDTH + head * QK_DIM + (lane // (2 * half)) * half + lane % half


def _modulation(c8, w_mod, b_mod, w_in):
    n = w_mod.shape[1]
    steps = 4
    tn = n // steps
    d, n_in = w_in.shape
    rows = d // steps
    return pl.pallas_call(
        _mod_kernel,
        grid=(steps,),
        in_specs=[_const_spec((8, D_MODEL)),
                  pl.BlockSpec((D_MODEL, tn), lambda j: (0, j)),
                  pl.BlockSpec((1, tn), lambda j: (0, j)),
                  pl.BlockSpec((rows, n_in), lambda j: (j, 0))],
        out_specs=[pl.BlockSpec((8, tn), lambda j: (0, j)),
                   pl.BlockSpec((rows, n_in), lambda j: (j, 0)),
                   pl.BlockSpec((rows, 2 * QK_WIDTH), lambda j: (j, 0))],
        out_shape=[jax.ShapeDtypeStruct((8, n), F32),
                   jax.ShapeDtypeStruct((d, n_in), BF16),
                   jax.ShapeDtypeStruct((d, 2 * QK_WIDTH), BF16)],
        compiler_params=_params(("parallel",)),
        name="mod",
    )(c8, w_mod, b_mod.reshape(1, n), w_in)


def _ctx_kernel(x_ref, mod_ref, nw_ref, wk_ref, wv_ref, k_ref, v_ref):
    x = x_ref[0]
    h = _mod_norm(x, nw_ref[...], mod_ref[0, 0:1, :], mod_ref[0, 1:2, :])
    hb = h.astype(BF16)
    k_ref[0] = _dot(hb, wk_ref[...]).astype(BF16)
    v_ref[0] = _dot(hb, wv_ref[...]).astype(BF16)


def _ctx_proj(ctx, mod3, norm_w, w_qk, w_in):
    b, lc, d = ctx.shape
    assert V_OFF % V_WIDTH == 0
    return pl.pallas_call(
        _ctx_kernel,
        grid=(b,),
        in_specs=[pl.BlockSpec((1, lc, d), lambda i: (i, 0, 0)),
                  pl.BlockSpec((1, N_MOD, d), lambda i: (2, 0, 0)),
                  _const_spec((1, d)),
                  pl.BlockSpec((d, QK_WIDTH), lambda i: (0, 1), pipeline_mode=pl.Buffered(1)),
                  pl.BlockSpec((d, V_WIDTH), lambda i: (0, V_OFF // V_WIDTH),
                               pipeline_mode=pl.Buffered(1))],
        out_specs=[pl.BlockSpec((1, lc, QK_WIDTH), lambda i: (i, 0, 0)),
                   pl.BlockSpec((1, lc, V_WIDTH), lambda i: (i, 0, 0))],
        out_shape=[jax.ShapeDtypeStruct((b, lc, QK_WIDTH), BF16),
                   jax.ShapeDtypeStruct((b, lc, V_WIDTH), BF16)],
        compiler_params=_params(("parallel",)),
        name="ctx_proj",
    )(ctx, mod3, norm_w, w_qk, w_in)


def _proj_kernel(x_ref, mod_ref, nw_ref, wqk_ref, w_ref, rtab_ref, ctab_ref, *refs, n_cast):
    q_ref, k_ref, v_ref, sg_ref, f_ref = refs[n_cast:n_cast + 5]
    _run_casts(refs[:n_cast] + refs[n_cast + 5:])
    lane = lax.broadcasted_iota(jnp.int32, (1, LANES), 1)
    by_row = (lane % (QK_DIM // 2)) < QK_DIM // 4

    def rope(t, trig, scale, out_ref, rows):
        for j in range(QK_WIDTH // LANES):
            tj = t[:, j * LANES:(j + 1) * LANES]
            r = tj * trig[0] + pltpu.roll(tj, LANES // 2, 1) * trig[1]
            out_ref[rows, j * LANES:(j + 1) * LANES] = (r * scale).astype(BF16)

    for r0 in range(0, x_ref.shape[0], PROJ_SUB):
        rows = slice(r0, r0 + PROJ_SUB)
        trig = []
        for cs in range(2):
            trig.append(jnp.concatenate(
                [jnp.where(by_row, rtab_ref[cs, g:g + 1, :], ctab_ref[cs])
                 for g in range(r0 // GRID_W, (r0 + PROJ_SUB) // GRID_W)], axis=0))
        x = x_ref[rows, :]
        h = _mod_norm(x, nw_ref[...], mod_ref[0, 0:1, :], mod_ref[0, 1:2, :])
        hb = h.astype(BF16)
        rope(_dot(hb, wqk_ref[:, :QK_WIDTH]), trig, QK_DIM ** -0.5, q_ref, rows)
        rope(_dot(hb, wqk_ref[:, QK_WIDTH:]), trig, 1.0, k_ref, rows)
        v_ref[rows, :] = _dot(hb, w_ref[:, V_OFF:G_OFF]).astype(BF16)
        g = _dot(hb, w_ref[:, G_OFF:F_OFF])
        hg = 0.5 * g
        sg_ref[rows, :] = (hg * jnp.tanh(hg) + hg).astype(BF16)
        f_ref[rows, :] = _dot(hb, w_ref[:, F_OFF:IN_COLS]).astype(BF16)


def _proj(x2, mod3, norm_w, w_qk, w_in, rtab, ctab, seq, to_cast):
    t, d = x2.shape
    tl = PROJ_TILE
    tpb = seq // tl
    tok = lambda w: pl.BlockSpec((tl, w), lambda i: (i, 0))
    cast_specs, cast_shapes = _cast_jobs(to_cast, t // tl)
    out = pl.pallas_call(
        functools.partial(_proj_kernel, n_cast=len(to_cast)),
        grid=(t // tl,),
        in_specs=[tok(d),
                  pl.BlockSpec((1, N_MOD, d), lambda i: (i // tpb, 0, 0)),
                  _const_spec((1, d)),
                  _const_spec(w_qk.shape), _const_spec(w_in.shape),
                  pl.BlockSpec((2, tl // GRID_W, LANES), lambda i: (0, i % tpb, 0)),
                  _const_spec(ctab.shape)] + cast_specs,
        out_specs=[tok(QK_WIDTH), tok(QK_WIDTH), tok(V_WIDTH), tok(V_WIDTH),
                   tok(F_WIDTH)] + cast_specs,
        out_shape=[jax.ShapeDtypeStruct((t, QK_WIDTH), BF16),
                   jax.ShapeDtypeStruct((t, QK_WIDTH), BF16),
                   jax.ShapeDtypeStruct((t, V_WIDTH), BF16),
                   jax.ShapeDtypeStruct((t, V_WIDTH), BF16),
                   jax.ShapeDtypeStruct((t, F_WIDTH), BF16)] + cast_shapes,
        compiler_params=_params(("arbitrary",)),
        name="proj",
    )(x2, mod3, norm_w, w_qk, w_in, rtab, ctab, *to_cast)
    return out[:5], out[5:]


def _retention_kernel(af_ref, ab_ref, q_ref, k_ref, v_ref, kc_ref, vc_ref,
                      tw_ref, w2_ref, cc_ref, y_ref,
                      o_ref, z_ref, ds_scr, st_scr, d_scr, tab_scr, z_scr):
    c = RET_CHUNK
    seq = q_ref.shape[1]
    lc = kc_ref.shape[1]
    n = seq // c
    lgf = [-jnp.exp(af_ref[hh]) for hh in range(2)]
    lgb = [-jnp.exp(ab_ref[hh]) for hh in range(2)]
    half = QK_DIM // 2

    lane_head = (lax.broadcasted_iota(jnp.int32, (1, LANES), 1) // half) % 2
    masks = [(lane_head == hh).astype(BF16) for hh in range(2)]
    lgf_lane = jnp.where(lane_head == 0, lgf[0], lgf[1])
    lgb_lane = jnp.where(lane_head == 0, lgb[0], lgb[1])
    row_head = (lax.broadcasted_iota(jnp.int32, (LANES, 1), 0) // half) % 2
    lgf_row = jnp.where(row_head == 0, lgf[0][:, 0:1], lgf[1][:, 0:1])
    lgb_row = jnp.where(row_head == 0, lgb[0][:, 0:1], lgb[1][:, 0:1])

    pos = lax.broadcasted_iota(jnp.int32, (c, LANES), 0).astype(F32)
    tab_scr[0] = jnp.exp(lgf_lane * (pos + 1.0))
    tab_scr[1] = jnp.exp(lgb_lane * (c - pos))
    diff = (lax.broadcasted_iota(jnp.int32, (c, c), 0)
            - lax.broadcasted_iota(jnp.int32, (c, c), 1)).astype(F32)
    for hh in range(2):
        d_scr[hh] = (jnp.where(diff >= 0, jnp.exp(lgf[hh][:, 0:1] * jnp.maximum(diff, 0.0)), 0.0)
                     + jnp.where(diff <= 0, jnp.exp(lgb[hh][:, 0:1] * jnp.maximum(-diff, 0.0)), 0.0)
                     ).astype(BF16)

    def k_decays(tokens):
        t = lax.broadcasted_iota(jnp.int32, (1, tokens), 1).astype(F32)
        return jnp.exp(lgf_row * (tokens - 1.0 - t)), jnp.exp(lgb_row * t)

    def state_increment(k_rows, v_rows, decays):
        kt = jnp.transpose(k_rows.astype(F32))
        lhs = jnp.concatenate([kt * decays[0], kt * decays[1]], axis=0).astype(BF16)
        return _dot(lhs, v_rows)

    kdec = k_decays(c)

    def incr(i, carry):
        rows = pl.ds(pl.multiple_of(i * c, c), c)
        ds_scr[i] = state_increment(k_ref[0, rows, :], v_ref[0, rows, :], kdec)
        return carry

    lax.fori_loop(0, n, incr, 0, unroll=RET_UNROLL)

    s0 = state_increment(kc_ref[0], vc_ref[0], k_decays(lc))
    col_head = lax.broadcasted_iota(jnp.int32, (1, 2 * V_DIM), 1) // V_DIM
    own = (row_head == col_head).astype(F32)
    decay_f = jnp.exp(lgf_row * c) * own
    decay_b = jnp.exp(lgb_row * c) * own

    def scan_f(i, s):
        st_scr[i, 0:LANES, :] = (s * own).astype(BF16)
        return decay_f * s + ds_scr[i, 0:LANES, :]

    lax.fori_loop(0, n, scan_f, s0[0:LANES])

    def scan_b(t, s):
        i = n - 1 - t
        st_scr[i, LANES:2 * LANES, :] = (s * own).astype(BF16)
        return decay_b * s + ds_scr[i, LANES:2 * LANES, :]

    lax.fori_loop(0, n, scan_b, s0[LANES:2 * LANES])

    def outputs(i):
        rows = pl.ds(pl.multiple_of(i * c, c), c)
        q = q_ref[0, rows, :]
        k = k_ref[0, rows, :]
        qf = q.astype(F32)
        qd = jnp.concatenate([qf * tab_scr[0], qf * tab_scr[1]], axis=1).astype(BF16)
        inter = _dot(qd, st_scr[i])
        for hh in range(2):
            cols = slice(hh * V_DIM, (hh + 1) * V_DIM)
            scores = lax.dot_general(q * masks[hh], k, (((1,), (1,)), ((), ())),
                                     preferred_element_type=F32)
            o = _dot(scores.astype(BF16) * d_scr[hh], v_ref[0, rows, cols]) + inter[:, cols]
            o_ref[0, rows, cols] = o.astype(BF16)

    per_k1 = n // DFT_ROWS
    cc = cc_ref[...].astype(BF16)

    def outputs_and_dft(j, carry):
        for u in range(per_k1):
            outputs(j * per_k1 + u)
        _dft_second_stage(j, tw_ref, w2_ref, cc, y_ref, z_scr)
        return carry

    lax.fori_loop(0, DFT_ROWS, outputs_and_dft, 0, unroll=RET_UNROLL // per_k1)
    _dft_emit(z_scr, z_ref)


def _retention_and_dft(a_f, a_b, q, k, v, kc, vc, y, tw, w2, w_c):
    b, seq, _ = v.shape
    lc = kc.shape[1]
    c = RET_CHUNK
    n = seq // c
    r = DFT_ROWS
    _, _, n1, n2, w = y.shape
    assert n % r == 0 and n1 // r == HEADS // 2
    dec = pl.BlockSpec((2, 1, LANES), lambda i, p: (p, 0, 0))
    qk = lambda rows: pl.BlockSpec((1, rows, LANES), lambda i, p: (i, 0, p))
    vv = lambda rows: pl.BlockSpec((1, rows, 2 * V_DIM), lambda i, p: (i, 0, p))
    return pl.pallas_call(
        _retention_kernel,
        grid=(b, HEADS // 2),
        in_specs=[dec, dec, qk(seq), qk(seq), vv(seq), qk(lc), vv(lc),
                  pl.BlockSpec((r, 2, n2), lambda i, p: (p, 0, 0)),
                  _const_spec((2, n2, n2)),
                  _const_spec((2 * F_GROUP_DIM, F_GROUP_DIM)),
                  pl.BlockSpec((1, 2, r, n2, w), lambda i, p: (i, 0, p, 0, 0))],
        out_specs=[vv(seq), pl.BlockSpec((1, n2, r, w), lambda i, p: (i, 0, p, 0))],
        out_shape=[jax.ShapeDtypeStruct((b, seq, V_WIDTH), BF16),
                   jax.ShapeDtypeStruct((b, n2, n1, w), BF16)],
        scratch_shapes=[pltpu.VMEM((n, 2 * LANES, 2 * V_DIM), F32),
                        pltpu.VMEM((n, 2 * LANES, 2 * V_DIM), BF16),
                        pltpu.VMEM((2, c, c), BF16),
                        pltpu.VMEM((2, c, LANES), F32),
                        pltpu.VMEM((F_GROUPS, n2 * r, F_GROUP_DIM), F32)],
        compiler_params=_params(("parallel", "parallel")),
        name="retention_dft",
    )(a_f, a_b, q, k, v, kc, vc, tw, w2, w_c, y)


def _dft_a_kernel(w_ref, x_ref, y_ref):
    _, n1, rows, w = x_ref.shape
    r = DFT_ROWS
    nh = n1 // 2 + 1
    for r0 in range(0, rows, r):
        x = x_ref[0, :, r0:r0 + r, :].reshape(n1 * r, w)
        y = _dot(w_ref[...], x).reshape(2, nh, r, w).astype(BF16)
        y_ref[0, :, 0:nh, r0:r0 + r, :] = y
        for k1 in range(1, n1 // 2):
            y_ref[0, 0, n1 - k1, r0:r0 + r, :] = y[0, k1]
            y_ref[0, 1, n1 - k1, r0:r0 + r, :] = -y[1, k1]


def _dft_second_stage(j, tw_ref, w2_ref, cc, y_ref, z_scr):
    n2 = y_ref.shape[3]
    w2c = w2_ref[0]
    w2s = w2_ref[1]
    tc = tw_ref[j, 0:1, :]
    ts = tw_ref[j, 1:2, :]
    ec = w2c * tc - w2s * ts
    es = w2s * tc + w2c * ts
    m = jnp.concatenate([jnp.concatenate([ec, -es], axis=1),
                         jnp.concatenate([es, ec], axis=1)], axis=0).astype(BF16)
    y = jnp.concatenate([y_ref[0, 0, j], y_ref[0, 1, j]], axis=0)
    zz = _dot(m, y)
    for g in range(F_GROUPS):
        cols = slice(g * F_GROUP_DIM, (g + 1) * F_GROUP_DIM)
        zcs = jnp.concatenate([zz[:n2, cols], zz[n2:, cols]], axis=1).astype(BF16)
        z_scr[g, pl.ds(j, n2, stride=DFT_ROWS), :] = _dot(zcs, cc)


def _dft_emit(z_scr, z_ref):
    n2 = z_ref.shape[1]
    z = jnp.concatenate([z_scr[g] for g in range(F_GROUPS)], axis=1)
    z_ref[0] = z.reshape(n2, DFT_ROWS, F_WIDTH).astype(BF16)


def _dft_tables(seq):
    n1 = DFT_N1
    n2 = seq // n1

    def cs(num, den):
        ang = 2.0 * np.pi * (num % den) / den
        return np.cos(ang), np.sin(ang)

    a = np.arange(n1)
    w_a = np.concatenate(cs(a[:n1 // 2 + 1, None] * a[None, :], n1), axis=0)
    m = np.arange(n2)
    tw = np.stack(cs(a[:, None] * m[None, :], seq), axis=1)
    w2 = np.stack(cs(m[:, None] * m[None, :], n2), axis=0)
    ch = np.arange(F_GROUP_DIM)
    cc, sc = cs(ch[:, None] * ch[None, :], F_GROUP_DIM)
    scale = 1.0 / np.sqrt(seq * F_GROUP_DIM)
    w_c = np.concatenate([cc, -sc], axis=0) * scale
    return [jnp.asarray(t, dtype=F32) for t in (w_a, tw, w2, w_c)]


def _dft_first_stage(f):
    b, seq, w = f.shape
    n1 = DFT_N1
    n2 = seq // n1
    w_a, tw, w2, w_c = _dft_tables(seq)
    r = DFT_ROWS
    spread = (jnp.arange(n1 * r)[None, :] // r == jnp.arange(n1)[:, None]).astype(F32)
    w_rep = jnp.dot(w_a, spread, precision=lax.Precision.HIGHEST)
    same_r = jnp.arange(w_a.shape[0] * r)[:, None] % r == jnp.arange(n1 * r)[None, :] % r
    w_a = jnp.where(same_r, jnp.repeat(w_rep, r, axis=0), 0.0).astype(BF16)
    y = pl.pallas_call(
        _dft_a_kernel,
        grid=(b, n2 // DFT_A_ROWS),
        in_specs=[_const_spec(w_a.shape),
                  pl.BlockSpec((1, n1, DFT_A_ROWS, w), lambda i, j: (i, 0, j, 0))],
        out_specs=pl.BlockSpec((1, 2, n1, DFT_A_ROWS, w), lambda i, j: (i, 0, 0, j, 0)),
        out_shape=jax.ShapeDtypeStruct((b, 2, n1, n2, w), BF16),
        compiler_params=_params(("parallel", "parallel")),
        name="dft_a",
    )(w_a, f.reshape(b, n1, n2, w))
    return y, tw, w2, w_c


def _merge_kernel(x_ref, mod_ref, nw_ref, n2w_ref, ro_ref, sg_ref, z_ref, wbg_ref, bbg_ref,
                  wro_ref, wfo_ref, wout_ref, *refs, n_cast):
    o_ref, h2_ref = refs[n_cast:n_cast + 2]
    _run_casts(refs[:n_cast] + refs[n_cast + 2:])
    for r0 in range(0, x_ref.shape[0], MERGE_SUB):
        rows = slice(r0, r0 + MERGE_SUB)
        x = x_ref[rows, :]
        h = _mod_norm(x, nw_ref[...], mod_ref[0, 0:1, :], mod_ref[0, 1:2, :])
        t = jnp.tanh(0.5 * _dot(h.astype(BF16), wbg_ref[...]) + 0.5 * bbg_ref[...])
        gated = []
        for hd in range(HEADS):
            cols = slice(hd * V_DIM, (hd + 1) * V_DIM)
            o = ro_ref[rows, cols].astype(F32)
            oc = o - jnp.mean(o, axis=-1, keepdims=True)
            var = jnp.mean(oc * oc, axis=-1, keepdims=True)
            gated.append((oc * lax.rsqrt(var + EPS) * sg_ref[rows, cols].astype(F32)).astype(BF16))
        ret_d = _dot(jnp.concatenate(gated, axis=1), wro_ref[...])
        four_d = _dot(z_ref[rows, :], wfo_ref[...])
        m2 = (t[:, :D_MODEL] * ret_d + ret_d) + (t[:, D_MODEL:] * four_d + four_d)
        y2 = _dot(m2.astype(BF16), wout_ref[...])
        x1 = x + (0.5 * mod_ref[0, 2:3, :]) * y2
        o_ref[rows, :] = x1
        h2 = _mod_norm(x1, n2w_ref[...], mod_ref[0, 3:4, :], mod_ref[0, 4:5, :])
        h2_ref[rows, :] = h2.astype(BF16)


def _merge(x2, mod3, norm_w, norm2_w, ro, sg, z, w_bg, b_bg, w_ro, w_fo, w_out, seq, to_cast):
    t, d = x2.shape
    tl = MERGE_TILE
    tpb = seq // tl
    tok = lambda w: pl.BlockSpec((tl, w), lambda i: (i, 0))
    cast_specs, cast_shapes = _cast_jobs(to_cast, t // tl)
    out = pl.pallas_call(
        functools.partial(_merge_kernel, n_cast=len(to_cast)),
        grid=(t // tl,),
        in_specs=[tok(d),
                  pl.BlockSpec((1, N_MOD, d), lambda i: (i // tpb, 0, 0)),
                  _const_spec((1, d)), _const_spec((1, d)),
                  tok(V_WIDTH), tok(V_WIDTH), tok(F_WIDTH),
                  _const_spec(w_bg.shape), _const_spec((1, 2 * d)),
                  _const_spec(w_ro.shape), _const_spec(w_fo.shape),
                  _const_spec(w_out.shape)] + cast_specs,
        out_specs=[tok(d), tok(d)] + cast_specs,
        out_shape=[jax.ShapeDtypeStruct((t, d), F32),
                   jax.ShapeDtypeStruct((t, d), BF16)] + cast_shapes,
        compiler_params=_params(("arbitrary",)),
        name="merge",
    )(x2, mod3, norm_w, norm2_w, ro, sg, z, w_bg, b_bg, w_ro, w_fo, w_out, *to_cast)
    return out[:2], out[2:]


HALO = 8
FFN_LEAD = 16
FFN_SUB = 256


def _two_gelu_tanh(x):
    c1 = np.sqrt(2.0 / np.pi)
    return x + x * jnp.tanh(x * (c1 + (c1 * 0.044715) * (x * x)))


def _ffn_kernel(xm_ref, hm_ref, xp_ref, xn_ref, mod_ref, nw_ref, wu_ref, cw_ref, cb_ref,
                wd_ref, fnw_ref, o_ref, h_scr, u_scr, act_scr, y_scr, *, tiles_per_seq):
    tl = xm_ref.shape[0]
    sub = FFN_SUB
    half = sub // 2
    nc = FFN_CHUNK
    i = pl.program_id(0)
    keep_prev = ((i % tiles_per_seq) != 0).astype(F32)
    keep_next = ((i % tiles_per_seq) != tiles_per_seq - 1).astype(F32)

    def pre(x):
        return _mod_norm(x, nw_ref[...], mod_ref[0, 3:4, :], mod_ref[0, 4:5, :])

    pad = jnp.zeros((FFN_LEAD - HALO, xm_ref.shape[1]), F32)
    h_scr[0:FFN_LEAD] = jnp.concatenate([pad, pre(xp_ref[...]) * keep_prev], axis=0).astype(BF16)
    h_scr[FFN_LEAD:FFN_LEAD + tl] = hm_ref[...]
    h_scr[FFN_LEAD + tl:] = jnp.concatenate([pre(xn_ref[...]) * keep_next, pad], axis=0).astype(BF16)

    def project(j):
        hb = h_scr[...]
        for part in range(2):
            lo = part * FFN_DIM + j * nc
            u = _dot(hb, wu_ref[:, lo:lo + nc])
            for s in range(nc // LANES):
                u_scr[j % 2, part, s] = u[:, s * LANES:(s + 1) * LANES]

    def conv(j, part, s, blk, scale):
        lo = part * FFN_DIM + j * nc + s * LANES
        w = cw_ref[:, lo:lo + LANES] * scale
        bias = cb_ref[:, lo:lo + LANES] * scale
        first = FFN_LEAD + blk * sub
        rows = lambda start: u_scr[j % 2, part, s, pl.ds(start, half, stride=2), :]
        before, even, odd, after = rows(first - 1), rows(first), rows(first + 1), rows(first + 2)
        return (before * w[0:1] + even * w[1:2] + odd * w[2:3] + bias,
                even * w[0:1] + odd * w[1:2] + after * w[2:3] + bias)

    def activate(j):
        for s in range(nc // LANES):
            cols = slice(j * nc + s * LANES, j * nc + (s + 1) * LANES)
            for blk in range(tl // sub):
                gate = conv(j, 0, s, blk, 1.0)
                val = conv(j, 1, s, blk, 0.5)
                for par in range(2):
                    r0 = blk * sub + par * half
                    act_scr[r0:r0 + half, cols] = (_two_gelu_tanh(gate[par]) * val[par]).astype(BF16)

    project(0)
    for j in range(N_FFN_CHUNKS):
        if j + 1 < N_FFN_CHUNKS:
            project(j + 1)
        activate(j)
    for blk in range(tl // sub):
        r0 = blk * sub
        y = _dot(act_scr[r0:r0 + sub, :], wd_ref[...])
        for s in range(y_scr.shape[0]):
            for par in range(2):
                y_scr[s, pl.ds(par, half, stride=2), :] = (
                    y[par * half:(par + 1) * half, s * LANES:(s + 1) * LANES])
        y = jnp.concatenate([y_scr[s] for s in range(y_scr.shape[0])], axis=1)
        x2 = xm_ref[r0:r0 + sub, :] + mod_ref[0, 5:6, :] * y
        o_ref[r0:r0 + sub, :] = _rms_norm(x2, fnw_ref[...])


def _ffn(x1, h2, mod3, norm_w, w_u, cw, cb, w_d, fnorm_w, seq):
    t, d = x1.shape
    tl = FFN_TILE
    tpb = seq // tl
    hb = tl // HALO
    last = t // HALO - 1
    rows = FFN_LEAD + tl + FFN_LEAD
    return pl.pallas_call(
        functools.partial(_ffn_kernel, tiles_per_seq=tpb),
        grid=(t // tl,),
        in_specs=[pl.BlockSpec((tl, d), lambda i: (i, 0)),
                  pl.BlockSpec((tl, d), lambda i: (i, 0)),
                  pl.BlockSpec((HALO, d), lambda i: (jnp.maximum(i * hb - 1, 0), 0)),
                  pl.BlockSpec((HALO, d), lambda i: (jnp.minimum((i + 1) * hb, last), 0)),
                  pl.BlockSpec((1, N_MOD, d), lambda i: (i // tpb, 0, 0)),
                  _const_spec((1, d)),
                  _const_spec(w_u.shape), _const_spec(cw.shape), _const_spec(cb.shape),
                  _const_spec(w_d.shape), _const_spec((1, d))],
        out_specs=pl.BlockSpec((tl, d), lambda i: (i, 0)),
        out_shape=jax.ShapeDtypeStruct((t, d), F32),
        scratch_shapes=[pltpu.VMEM((rows, d), BF16),
                        pltpu.VMEM((2, 2, FFN_CHUNK // LANES, rows, LANES), F32),
                        pltpu.VMEM((tl, FFN_DIM), BF16),
                        pltpu.VMEM((d // LANES, FFN_SUB, LANES), F32)],
        compiler_params=_params(("parallel",)),
        name="ffn",
    )(x1, h2, x1, x1, mod3, norm_w, w_u, cw, cb, w_d, fnorm_w)


def _rope_tables(seq):
    n_freq = QK_DIM // 4
    lane = jnp.arange(LANES)
    inv = ROPE_BASE ** (-(lane % n_freq).astype(F32) / n_freq)
    sign = jnp.where(lane < LANES // 2, -1.0, 1.0).astype(F32)

    def table(n):
        ang = jnp.arange(n, dtype=F32)[:, None] * inv[None, :]
        return jnp.stack([jnp.cos(ang), jnp.sin(ang) * sign])

    return table(seq // GRID_W), table(GRID_W)


def kernel(x, c, ctx, c_ctx, w_mod, b_mod, norm1_w, w_in, ret_decay_f, ret_decay_b,
           w_ret_out, w_four_out, w_branch_gate, b_branch_gate, w_out, norm2_w,
           w_up, conv_w, conv_b, w_down, final_norm_w):
    assert w_mod.shape[0] == 1, "single-layer block"
    b, seq, d = x.shape
    t = b * seq
    assert b == 2 and seq % RET_CHUNK == 0
    assert seq % FFN_TILE == 0 and seq % MERGE_TILE == 0 and seq % PROJ_TILE == 0

    c8 = jnp.concatenate([c, c_ctx[None, :], jnp.zeros((8 - b - 1, d), F32)], axis=0)
    mod, w_in_b, w_qk = _modulation(c8, w_mod[0], b_mod[0], w_in[0])
    mod3 = mod.reshape(8, N_MOD, d)
    n1w = norm1_w[0].reshape(1, d)

    kc, vc = _ctx_proj(ctx, mod3, n1w, w_qk, w_in_b)
    rtab, ctab = _rope_tables(seq)
    x2 = x.reshape(t, d)
    (q, k, v, sg, f), (w_bg, w_ro, w_fo, w_o) = _proj(
        x2, mod3, n1w, w_qk, w_in_b, rtab, ctab, seq,
        [w_branch_gate[0], w_ret_out[0], w_four_out[0], w_out[0]])

    a_f = jnp.broadcast_to(ret_decay_f[0][:, None, None], (HEADS, 1, LANES))
    a_b = jnp.broadcast_to(ret_decay_b[0][:, None, None], (HEADS, 1, LANES))
    y, tw, w2, w_c = _dft_first_stage(f.reshape(b, seq, F_WIDTH))
    ro, z = _retention_and_dft(a_f, a_b, q.reshape(b, seq, QK_WIDTH), k.reshape(b, seq, QK_WIDTH),
                               v.reshape(b, seq, V_WIDTH), kc, vc, y, tw, w2, w_c)

    n2w = norm2_w[0].reshape(1, d)
    (x1, h2), (w_u, w_d) = _merge(
        x2, mod3, n1w, n2w, ro.reshape(t, V_WIDTH), sg, z.reshape(t, F_WIDTH),
        w_bg, b_branch_gate[0].reshape(1, 2 * d), w_ro, w_fo, w_o, seq, [w_up[0], w_down[0]])

    out = _ffn(x1, h2, mod3, n2w, w_u, conv_w[0], conv_b[0].reshape(1, 2 * FFN_DIM), w_d,
               final_norm_w.reshape(1, d), seq)
    return out.reshape(b, seq, d)
```

```python
import functools

import numpy as np
import jax
import jax.numpy as jnp
from jax import lax
from jax.experimental import pallas as pl
from jax.experimental.pallas import tpu as pltpu

F32 = jnp.float32
BF16 = jnp.bfloat16

D_MODEL = 1024
GRID_W = 64
HEADS = 8
QK_DIM = 64
V_DIM = 128
QK_WIDTH = HEADS * QK_DIM
V_WIDTH = HEADS * V_DIM
ROPE_BASE = 10000.0
F_GROUPS = 4
F_GROUP_DIM = 128
F_WIDTH = F_GROUPS * F_GROUP_DIM
K_OFF = QK_WIDTH
V_OFF = K_OFF + QK_WIDTH
G_OFF = V_OFF + V_WIDTH
F_OFF = G_OFF + V_WIDTH
IN_COLS = F_OFF + F_WIDTH
FFN_DIM = 2816
N_MOD = 6
EPS = 1e-6

LANES = 128
RET_CHUNK = 256
RET_UNROLL = 16
FFN_CHUNK = 256
N_FFN_CHUNKS = FFN_DIM // FFN_CHUNK
FFN_TILE = 512
MERGE_TILE = 1024
MERGE_SUB = 512
PROJ_TILE = 1024
PROJ_SUB = 256
DFT_N1 = 64
BF16_ROWS = 16
DFT_ROWS = BF16_ROWS
DFT_A_ROWS = 2 * DFT_ROWS
VMEM_LIMIT = 56 * 1024 * 1024


def _params(sem):
    return pltpu.CompilerParams(dimension_semantics=sem, vmem_limit_bytes=VMEM_LIMIT)


def _dot(a, b):
    return jnp.dot(a, b, preferred_element_type=F32)


def _rms_norm(x, w):
    return x * lax.rsqrt(jnp.mean(x * x, axis=-1, keepdims=True) + EPS) * w


def _mod_norm(x, w, shift, scale):
    return x * lax.rsqrt(jnp.mean(x * x, axis=-1, keepdims=True) + EPS) * (w * (1.0 + scale)) + shift


def _const_spec(shape):
    zeros = (0,) * len(shape)
    return pl.BlockSpec(shape, lambda *_: zeros, pipeline_mode=pl.Buffered(1))


def _cast_jobs(weights, steps):
    specs, shapes = [], []
    for w in weights:
        n_rows = w.shape[0]
        rows = -(-n_rows // steps)
        while rows % BF16_ROWS or n_rows % rows:
            rows += 1
        last = n_rows // rows - 1
        specs.append(pl.BlockSpec((rows, w.shape[1]), lambda i, last=last: (jnp.minimum(i, last), 0)))
        shapes.append(jax.ShapeDtypeStruct(w.shape, BF16))
    return specs, shapes


def _run_casts(refs):
    n = len(refs) // 2
    for src, dst in zip(refs[:n], refs[n:]):
        dst[...] = src[...].astype(BF16)


def _mod_kernel(c_ref, w_ref, b_ref, win_ref, o_ref, winb_ref, wqk_ref):
    c = c_ref[...]
    s = c * jax.nn.sigmoid(c)
    o_ref[...] = _dot(s.astype(BF16), w_ref[...].astype(BF16)) + b_ref[...]
    wb = win_ref[...].astype(BF16)
    winb_ref[...] = wb
    n_qk = 2 * QK_WIDTH
    src = lax.broadcasted_iota(jnp.int32, (n_qk, n_qk), 0)
    dst = lax.broadcasted_iota(jnp.int32, (n_qk, n_qk), 1)
    select = (src == _qk_source_column(dst)).astype(BF16)
    wqk_ref[...] = _dot(wb[:, :n_qk], select).astype(BF16)


def _qk_source_column(col):
    half = QK_DIM // 2
    lane = col % LANES
    head = 2 * ((col % QK_WIDTH) // LANES) + (lane // half) % 2
    return (col // QK_WIDTH) * QK_WIDTH + head * QK_DIM + (lane // (2 * half)) * half + lane % half


def _modulation(c8, w_mod, b_mod, w_in):
    n = w_mod.shape[1]
    steps = 4
    tn = n // steps
    d, n_in = w_in.shape
    rows = d // steps
    return pl.pallas_call(
        _mod_kernel,
        grid=(steps,),
        in_specs=[_const_spec((8, D_MODEL)),
                  pl.BlockSpec((D_MODEL, tn), lambda j: (0, j)),
                  pl.BlockSpec((1, tn), lambda j: (0, j)),
                  pl.BlockSpec((rows, n_in), lambda j: (j, 0))],
        out_specs=[pl.BlockSpec((8, tn), lambda j: (0, j)),
                   pl.BlockSpec((rows, n_in), lambda j: (j, 0)),
                   pl.BlockSpec((rows, 2 * QK_WIDTH), lambda j: (j, 0))],
        out_shape=[jax.ShapeDtypeStruct((8, n), F32),
                   jax.ShapeDtypeStruct((d, n_in), BF16),
                   jax.ShapeDtypeStruct((d, 2 * QK_WIDTH), BF16)],
        compiler_params=_params(("parallel",)),
        name="mod",
    )(c8, w_mod, b_mod.reshape(1, n), w_in)


def _ctx_kernel(x_ref, mod_ref, nw_ref, wk_ref, wv_ref, k_ref, v_ref):
    x = x_ref[0]
    h = _mod_norm(x, nw_ref[...], mod_ref[0, 0:1, :], mod_ref[0, 1:2, :])
    hb = h.astype(BF16)
    k_ref[0] = _dot(hb, wk_ref[...]).astype(BF16)
    v_ref[0] = _dot(hb, wv_ref[...]).astype(BF16)


def _ctx_proj(ctx, mod3, norm_w, w_qk, w_in):
    b, lc, d = ctx.shape
    assert V_OFF % V_WIDTH == 0
    return pl.pallas_call(
        _ctx_kernel,
        grid=(b,),
        in_specs=[pl.BlockSpec((1, lc, d), lambda i: (i, 0, 0)),
                  pl.BlockSpec((1, N_MOD, d), lambda i: (2, 0, 0)),
                  _const_spec((1, d)),
                  pl.BlockSpec((d, QK_WIDTH), lambda i: (0, 1), pipeline_mode=pl.Buffered(1)),
                  pl.BlockSpec((d, V_WIDTH), lambda i: (0, V_OFF // V_WIDTH),
                               pipeline_mode=pl.Buffered(1))],
        out_specs=[pl.BlockSpec((1, lc, QK_WIDTH), lambda i: (i, 0, 0)),
                   pl.BlockSpec((1, lc, V_WIDTH), lambda i: (i, 0, 0))],
        out_shape=[jax.ShapeDtypeStruct((b, lc, QK_WIDTH), BF16),
                   jax.ShapeDtypeStruct((b, lc, V_WIDTH), BF16)],
        compiler_params=_params(("parallel",)),
        name="ctx_proj",
    )(ctx, mod3, norm_w, w_qk, w_in)


def _proj_kernel(x_ref, mod_ref, nw_ref, wqk_ref, w_ref, rtab_ref, ctab_ref, *refs, n_cast):
    q_ref, k_ref, v_ref, sg_ref, f_ref = refs[n_cast:n_cast + 5]
    _run_casts(refs[:n_cast] + refs[n_cast + 5:])
    lane = lax.broadcasted_iota(jnp.int32, (1, LANES), 1)
    by_row = (lane % (QK_DIM // 2)) < QK_DIM // 4

    def rope(t, trig, scale, out_ref, rows):
        for j in range(QK_WIDTH // LANES):
            tj = t[:, j * LANES:(j + 1) * LANES]
            r = tj * trig[0] + pltpu.roll(tj, LANES // 2, 1) * trig[1]
            out_ref[rows, j * LANES:(j + 1) * LANES] = (r * scale).astype(BF16)

    for r0 in range(0, x_ref.shape[0], PROJ_SUB):
        rows = slice(r0, r0 + PROJ_SUB)
        trig = []
        for cs in range(2):
            trig.append(jnp.concatenate(
                [jnp.where(by_row, rtab_ref[cs, g:g + 1, :], ctab_ref[cs])
                 for g in range(r0 // GRID_W, (r0 + PROJ_SUB) // GRID_W)], axis=0))
        x = x_ref[rows, :]
        h = _mod_norm(x, nw_ref[...], mod_ref[0, 0:1, :], mod_ref[0, 1:2, :])
        hb = h.astype(BF16)
        rope(_dot(hb, wqk_ref[:, :QK_WIDTH]), trig, QK_DIM ** -0.5, q_ref, rows)
        rope(_dot(hb, wqk_ref[:, QK_WIDTH:]), trig, 1.0, k_ref, rows)
        v_ref[rows, :] = _dot(hb, w_ref[:, V_OFF:G_OFF]).astype(BF16)
        g = _dot(hb, w_ref[:, G_OFF:F_OFF])
        hg = 0.5 * g
        sg_ref[rows, :] = (hg * jnp.tanh(hg) + hg).astype(BF16)
        f_ref[rows, :] = _dot(hb, w_ref[:, F_OFF:IN_COLS]).astype(BF16)


def _proj(x2, mod3, norm_w, w_qk, w_in, rtab, ctab, seq, to_cast):
    t, d = x2.shape
    tl = PROJ_TILE
    tpb = seq // tl
    tok = lambda w: pl.BlockSpec((tl, w), lambda i: (i, 0))
    cast_specs, cast_shapes = _cast_jobs(to_cast, t // tl)
    out = pl.pallas_call(
        functools.partial(_proj_kernel, n_cast=len(to_cast)),
        grid=(t // tl,),
        in_specs=[tok(d),
                  pl.BlockSpec((1, N_MOD, d), lambda i: (i // tpb, 0, 0)),
                  _const_spec((1, d)),
                  _const_spec(w_qk.shape), _const_spec(w_in.shape),
                  pl.BlockSpec((2, tl // GRID_W, LANES), lambda i: (0, i % tpb, 0)),
                  _const_spec(ctab.shape)] + cast_specs,
        out_specs=[tok(QK_WIDTH), tok(QK_WIDTH), tok(V_WIDTH), tok(V_WIDTH),
                   tok(F_WIDTH)] + cast_specs,
        out_shape=[jax.ShapeDtypeStruct((t, QK_WIDTH), BF16),
                   jax.ShapeDtypeStruct((t, QK_WIDTH), BF16),
                   jax.ShapeDtypeStruct((t, V_WIDTH), BF16),
                   jax.ShapeDtypeStruct((t, V_WIDTH), BF16),
                   jax.ShapeDtypeStruct((t, F_WIDTH), BF16)] + cast_shapes,
        compiler_params=_params(("arbitrary",)),
        name="proj",
    )(x2, mod3, norm_w, w_qk, w_in, rtab, ctab, *to_cast)
    return out[:5], out[5:]


def _retention_kernel(af_ref, ab_ref, q_ref, k_ref, v_ref, kc_ref, vc_ref,
                      tw_ref, w2_ref, cc_ref, y_ref,
                      o_ref, z_ref, ds_scr, st_scr, d_scr, tab_scr, z_scr):
    c = RET_CHUNK
    seq = q_ref.shape[1]
    lc = kc_ref.shape[1]
    n = seq // c
    lgf = [-jnp.exp(af_ref[hh]) for hh in range(2)]
    lgb = [-jnp.exp(ab_ref[hh]) for hh in range(2)]
    half = QK_DIM // 2

    lane_head = (lax.broadcasted_iota(jnp.int32, (1, LANES), 1) // half) % 2
    masks = [(lane_head == hh).astype(BF16) for hh in range(2)]
    lgf_lane = jnp.where(lane_head == 0, lgf[0], lgf[1])
    lgb_lane = jnp.where(lane_head == 0, lgb[0], lgb[1])
    row_head = (lax.broadcasted_iota(jnp.int32, (LANES, 1), 0) // half) % 2
    lgf_row = jnp.where(row_head == 0, lgf[0][:, 0:1], lgf[1][:, 0:1])
    lgb_row = jnp.where(row_head == 0, lgb[0][:, 0:1], lgb[1][:, 0:1])

    pos = lax.broadcasted_iota(jnp.int32, (c, LANES), 0).astype(F32)
    tab_scr[0] = jnp.exp(lgf_lane * (pos + 1.0))
    tab_scr[1] = jnp.exp(lgb_lane * (c - pos))
    diff = (lax.broadcasted_iota(jnp.int32, (c, c), 0)
            - lax.broadcasted_iota(jnp.int32, (c, c), 1)).astype(F32)
    for hh in range(2):
        d_scr[hh] = (jnp.where(diff >= 0, jnp.exp(lgf[hh][:, 0:1] * jnp.maximum(diff, 0.0)), 0.0)
                     + jnp.where(diff <= 0, jnp.exp(lgb[hh][:, 0:1] * jnp.maximum(-diff, 0.0)), 0.0))

    def k_decays(tokens):
        t = lax.broadcasted_iota(jnp.int32, (1, tokens), 1).astype(F32)
        return jnp.exp(lgf_row * (tokens - 1.0 - t)), jnp.exp(lgb_row * t)

    def state_increment(k_rows, v_rows, decays):
        kt = jnp.transpose(k_rows.astype(F32))
        lhs = jnp.concatenate([kt * decays[0], kt * decays[1]], axis=0).astype(BF16)
        return _dot(lhs, v_rows)

    kdec = k_decays(c)

    def incr(i, carry):
        rows = pl.ds(pl.multiple_of(i * c, c), c)
        ds_scr[i] = state_increment(k_ref[0, rows, :], v_ref[0, rows, :], kdec)
        return carry

    lax.fori_loop(0, n, incr, 0, unroll=RET_UNROLL)

    s0 = state_increment(kc_ref[0], vc_ref[0], k_decays(lc))
    col_head = lax.broadcasted_iota(jnp.int32, (1, 2 * V_DIM), 1) // V_DIM
    own = (row_head == col_head).astype(F32)
    decay_f = jnp.exp(lgf_row * c) * own
    decay_b = jnp.exp(lgb_row * c) * own

    def scans(t, carry):
        s_f, s_b = carry
        i = n - 1 - t
        st_scr[t, 0:LANES, :] = (s_f * own).astype(BF16)
        st_scr[i, LANES:2 * LANES, :] = (s_b * own).astype(BF16)
        return (decay_f * s_f + ds_scr[t, 0:LANES, :],
                decay_b * s_b + ds_scr[i, LANES:2 * LANES, :])

    lax.fori_loop(0, n, scans, (s0[0:LANES], s0[LANES:2 * LANES]), unroll=4)

    def outputs(i):
        rows = pl.ds(pl.multiple_of(i * c, c), c)
        q = q_ref[0, rows, :]
        k = k_ref[0, rows, :]
        qf = q.astype(F32)
        qd = jnp.concatenate([qf * tab_scr[0], qf * tab_scr[1]], axis=1).astype(BF16)
        inter = _dot(qd, st_scr[i])
        for hh in range(2):
            cols = slice(hh * V_DIM, (hh + 1) * V_DIM)
            scores = lax.dot_general(q * masks[hh], k, (((1,), (1,)), ((), ())),
                                     preferred_element_type=F32)
            o = _dot((scores * d_scr[hh]).astype(BF16), v_ref[0, rows, cols]) + inter[:, cols]
            o_ref[0, rows, cols] = o.astype(BF16)

    per_k1 = n // DFT_ROWS
    cc = cc_ref[...].astype(BF16)

    def outputs_and_dft(j, carry):
        for u in range(per_k1):
            outputs(j * per_k1 + u)
        _dft_second_stage(j, tw_ref, w2_ref, cc, y_ref, z_scr)
        return carry

    lax.fori_loop(0, DFT_ROWS, outputs_and_dft, 0, unroll=RET_UNROLL // per_k1)
    _dft_emit(z_scr, z_ref)


def _retention_and_dft(a_f, a_b, q, k, v, kc, vc, y, tw, w2, w_c):
    b, seq, _ = v.shape
    lc = kc.shape[1]
    c = RET_CHUNK
    n = seq // c
    r = DFT_ROWS
    _, _, n1, n2, w = y.shape
    assert n % r == 0 and n1 // r == HEADS // 2
    dec = pl.BlockSpec((2, 1, LANES), lambda i, p: (p, 0, 0))
    qk = lambda rows: pl.BlockSpec((1, rows, LANES), lambda i, p: (i, 0, p))
    vv = lambda rows: pl.BlockSpec((1, rows, 2 * V_DIM), lambda i, p: (i, 0, p))
    return pl.pallas_call(
        _retention_kernel,
        grid=(b, HEADS // 2),
        in_specs=[dec, dec, qk(seq), qk(seq), vv(seq), qk(lc), vv(lc),
                  pl.BlockSpec((r, 2, n2), lambda i, p: (p, 0, 0)),
                  _const_spec((2, n2, n2)),
                  _const_spec((2 * F_GROUP_DIM, F_GROUP_DIM)),
                  pl.BlockSpec((1, 2, r, n2, w), lambda i, p: (i, 0, p, 0, 0))],
        out_specs=[vv(seq), pl.BlockSpec((1, n2, r, w), lambda i, p: (i, 0, p, 0))],
        out_shape=[jax.ShapeDtypeStruct((b, seq, V_WIDTH), BF16),
                   jax.ShapeDtypeStruct((b, n2, n1, w), BF16)],
        scratch_shapes=[pltpu.VMEM((n, 2 * LANES, 2 * V_DIM), F32),
                        pltpu.VMEM((n, 2 * LANES, 2 * V_DIM), BF16),
                        pltpu.VMEM((2, c, c), F32),
                        pltpu.VMEM((2, c, LANES), F32),
                        pltpu.VMEM((F_GROUPS, n2 * r, F_GROUP_DIM), F32)],
        compiler_params=_params(("parallel", "parallel")),
        name="retention_dft",
    )(a_f, a_b, q, k, v, kc, vc, tw, w2, w_c, y)


def _dft_a_kernel(w_ref, x_ref, y_ref):
    _, n1, rows, w = x_ref.shape
    r = DFT_ROWS
    nh = n1 // 2 + 1
    for r0 in range(0, rows, r):
        x = x_ref[0, :, r0:r0 + r, :].reshape(n1 * r, w)
        y = _dot(w_ref[...], x).reshape(2, nh, r, w).astype(BF16)
        y_ref[0, :, 0:nh, r0:r0 + r, :] = y
        for k1 in range(1, n1 // 2):
            y_ref[0, 0, n1 - k1, r0:r0 + r, :] = y[0, k1]
            y_ref[0, 1, n1 - k1, r0:r0 + r, :] = -y[1, k1]


def _dft_second_stage(j, tw_ref, w2_ref, cc, y_ref, z_scr):
    n2 = y_ref.shape[3]
    w2c = w2_ref[0]
    w2s = w2_ref[1]
    tc = tw_ref[j, 0:1, :]
    ts = tw_ref[j, 1:2, :]
    ec = w2c * tc - w2s * ts
    es = w2s * tc + w2c * ts
    m = jnp.concatenate([jnp.concatenate([ec, -es], axis=1),
                         jnp.concatenate([es, ec], axis=1)], axis=0).astype(BF16)
    y = jnp.concatenate([y_ref[0, 0, j], y_ref[0, 1, j]], axis=0)
    zz = _dot(m, y)
    for g in range(F_GROUPS):
        cols = slice(g * F_GROUP_DIM, (g + 1) * F_GROUP_DIM)
        zcs = jnp.concatenate([zz[:n2, cols], zz[n2:, cols]], axis=1).astype(BF16)
        z_scr[g, pl.ds(j, n2, stride=DFT_ROWS), :] = _dot(zcs, cc)


def _dft_emit(z_scr, z_ref):
    n2 = z_ref.shape[1]
    z = jnp.concatenate([z_scr[g] for g in range(F_GROUPS)], axis=1)
    z_ref[0] = z.reshape(n2, DFT_ROWS, F_WIDTH).astype(BF16)


def _dft_tables(seq):
    n1 = DFT_N1
    n2 = seq // n1

    def cs(num, den):
        ang = 2.0 * np.pi * (num % den) / den
        return np.cos(ang), np.sin(ang)

    a = np.arange(n1)
    w_a = np.concatenate(cs(a[:n1 // 2 + 1, None] * a[None, :], n1), axis=0)
    m = np.arange(n2)
    tw = np.stack(cs(a[:, None] * m[None, :], seq), axis=1)
    w2 = np.stack(cs(m[:, None] * m[None, :], n2), axis=0)
    ch = np.arange(F_GROUP_DIM)
    cc, sc = cs(ch[:, None] * ch[None, :], F_GROUP_DIM)
    scale = 1.0 / np.sqrt(seq * F_GROUP_DIM)
    w_c = np.concatenate([cc, -sc], axis=0) * scale
    return [jnp.asarray(t, dtype=F32) for t in (w_a, tw, w2, w_c)]


def _dft_first_stage(f):
    b, seq, w = f.shape
    n1 = DFT_N1
    n2 = seq // n1
    w_a, tw, w2, w_c = _dft_tables(seq)
    r = DFT_ROWS
    spread = (jnp.arange(n1 * r)[None, :] // r == jnp.arange(n1)[:, None]).astype(F32)
    w_rep = jnp.dot(w_a, spread, precision=lax.Precision.HIGHEST)
    same_r = jnp.arange(w_a.shape[0] * r)[:, None] % r == jnp.arange(n1 * r)[None, :] % r
    w_a = jnp.where(same_r, jnp.repeat(w_rep, r, axis=0), 0.0).astype(BF16)
    y = pl.pallas_call(
        _dft_a_kernel,
        grid=(b, n2 // DFT_A_ROWS),
        in_specs=[_const_spec(w_a.shape),
                  pl.BlockSpec((1, n1, DFT_A_ROWS, w), lambda i, j: (i, 0, j, 0))],
        out_specs=pl.BlockSpec((1, 2, n1, DFT_A_ROWS, w), lambda i, j: (i, 0, 0, j, 0)),
        out_shape=jax.ShapeDtypeStruct((b, 2, n1, n2, w), BF16),
        compiler_params=_params(("parallel", "parallel")),
        name="dft_a",
    )(w_a, f.reshape(b, n1, n2, w))
    return y, tw, w2, w_c


def _merge_kernel(x_ref, mod_ref, nw_ref, n2w_ref, ro_ref, sg_ref, z_ref, wbg_ref, bbg_ref,
                  wro_ref, wfo_ref, wout_ref, *refs, n_cast):
    o_ref, h2_ref = refs[n_cast:n_cast + 2]
    _run_casts(refs[:n_cast] + refs[n_cast + 2:])
    for r0 in range(0, x_ref.shape[0], MERGE_SUB):
        rows = slice(r0, r0 + MERGE_SUB)
        x = x_ref[rows, :]
        h = _mod_norm(x, nw_ref[...], mod_ref[0, 0:1, :], mod_ref[0, 1:2, :])
        t = jnp.tanh(0.5 * _dot(h.astype(BF16), wbg_ref[...]) + 0.5 * bbg_ref[...])
        gated = []
        for hd in range(HEADS):
            cols = slice(hd * V_DIM, (hd + 1) * V_DIM)
            o = ro_ref[rows, cols].astype(F32)
            oc = o - jnp.mean(o, axis=-1, keepdims=True)
            var = jnp.mean(oc * oc, axis=-1, keepdims=True)
            gated.append((oc * lax.rsqrt(var + EPS) * sg_ref[rows, cols].astype(F32)).astype(BF16))
        ret_d = _dot(jnp.concatenate(gated, axis=1), wro_ref[...])
        four_d = _dot(z_ref[rows, :], wfo_ref[...])
        m2 = (t[:, :D_MODEL] * ret_d + ret_d) + (t[:, D_MODEL:] * four_d + four_d)
        y2 = _dot(m2.astype(BF16), wout_ref[...])
        x1 = x + (0.5 * mod_ref[0, 2:3, :]) * y2
        o_ref[rows, :] = x1
        h2 = _mod_norm(x1, n2w_ref[...], mod_ref[0, 3:4, :], mod_ref[0, 4:5, :])
        h2_ref[rows, :] = h2.astype(BF16)


def _merge(x2, mod3, norm_w, norm2_w, ro, sg, z, w_bg, b_bg, w_ro, w_fo, w_out, seq, to_cast):
    t, d = x2.shape
    tl = MERGE_TILE
    tpb = seq // tl
    tok = lambda w: pl.BlockSpec((tl, w), lambda i: (i, 0))
    cast_specs, cast_shapes = _cast_jobs(to_cast, t // tl)
    out = pl.pallas_call(
        functools.partial(_merge_kernel, n_cast=len(to_cast)),
        grid=(t // tl,),
        in_specs=[tok(d),
                  pl.BlockSpec((1, N_MOD, d), lambda i: (i // tpb, 0, 0)),
                  _const_spec((1, d)), _const_spec((1, d)),
                  tok(V_WIDTH), tok(V_WIDTH), tok(F_WIDTH),
                  _const_spec(w_bg.shape), _const_spec((1, 2 * d)),
                  _const_spec(w_ro.shape), _const_spec(w_fo.shape),
                  _const_spec(w_out.shape)] + cast_specs,
        out_specs=[tok(d), tok(d)] + cast_specs,
        out_shape=[jax.ShapeDtypeStruct((t, d), F32),
                   jax.ShapeDtypeStruct((t, d), BF16)] + cast_shapes,
        compiler_params=_params(("arbitrary",)),
        name="merge",
    )(x2, mod3, norm_w, norm2_w, ro, sg, z, w_bg, b_bg, w_ro, w_fo, w_out, *to_cast)
    return out[:2], out[2:]


HALO = 8
FFN_LEAD = 16
FFN_SUB = 256


def _two_gelu_tanh(x):
    c1 = np.sqrt(2.0 / np.pi)
    return x + x * jnp.tanh(x * (c1 + (c1 * 0.044715) * (x * x)))


def _ffn_kernel(xm_ref, hm_ref, xp_ref, xn_ref, mod_ref, nw_ref, wu_ref, cw_ref, cb_ref,
                wd_ref, fnw_ref, o_ref, h_scr, u_scr, act_scr, y_scr, *, tiles_per_seq):
    tl = xm_ref.shape[0]
    sub = FFN_SUB
    half = sub // 2
    nc = FFN_CHUNK
    i = pl.program_id(0)
    keep_prev = ((i % tiles_per_seq) != 0).astype(F32)
    keep_next = ((i % tiles_per_seq) != tiles_per_seq - 1).astype(F32)

    def pre(x):
        return _mod_norm(x, nw_ref[...], mod_ref[0, 3:4, :], mod_ref[0, 4:5, :])

    pad = jnp.zeros((FFN_LEAD - HALO, xm_ref.shape[1]), F32)
    h_scr[0:FFN_LEAD] = jnp.concatenate([pad, pre(xp_ref[...]) * keep_prev], axis=0).astype(BF16)
    h_scr[FFN_LEAD:FFN_LEAD + tl] = hm_ref[...]
    h_scr[FFN_LEAD + tl:] = jnp.concatenate([pre(xn_ref[...]) * keep_next, pad], axis=0).astype(BF16)

    def project(j):
        hb = h_scr[...]
        for part in range(2):
            lo = part * FFN_DIM + j * nc
            u = _dot(hb, wu_ref[:, lo:lo + nc])
            for s in range(nc // LANES):
                u_scr[j % 2, part, s] = u[:, s * LANES:(s + 1) * LANES]

    def conv(j, part, s, blk, scale):
        lo = part * FFN_DIM + j * nc + s * LANES
        w = cw_ref[:, lo:lo + LANES] * scale
        bias = cb_ref[:, lo:lo + LANES] * scale
        first = FFN_LEAD + blk * sub
        rows = lambda start: u_scr[j % 2, part, s, pl.ds(start, half, stride=2), :]
        before, even, odd, after = rows(first - 1), rows(first), rows(first + 1), rows(first + 2)
        return (before * w[0:1] + even * w[1:2] + odd * w[2:3] + bias,
                even * w[0:1] + odd * w[1:2] + after * w[2:3] + bias)

    def activate(j):
        for s in range(nc // LANES):
            cols = slice(j * nc + s * LANES, j * nc + (s + 1) * LANES)
            for blk in range(tl // sub):
                gate = conv(j, 0, s, blk, 1.0)
                val = conv(j, 1, s, blk, 0.5)
                for par in range(2):
                    r0 = blk * sub + par * half
                    act_scr[r0:r0 + half, cols] = (_two_gelu_tanh(gate[par]) * val[par]).astype(BF16)

    project(0)
    for j in range(N_FFN_CHUNKS):
        if j + 1 < N_FFN_CHUNKS:
            project(j + 1)
        activate(j)
    for blk in range(tl // sub):
        r0 = blk * sub
        y = _dot(act_scr[r0:r0 + sub, :], wd_ref[...])
        for s in range(y_scr.shape[0]):
            for par in range(2):
                y_scr[s, pl.ds(par, half, stride=2), :] = (
                    y[par * half:(par + 1) * half, s * LANES:(s + 1) * LANES])
        y = jnp.concatenate([y_scr[s] for s in range(y_scr.shape[0])], axis=1)
        x2 = xm_ref[r0:r0 + sub, :] + mod_ref[0, 5:6, :] * y
        o_ref[r0:r0 + sub, :] = _rms_norm(x2, fnw_ref[...])


def _ffn(x1, h2, mod3, norm_w, w_u, cw, cb, w_d, fnorm_w, seq):
    t, d = x1.shape
    tl = FFN_TILE
    tpb = seq // tl
    hb = tl // HALO
    last = t // HALO - 1
    rows = FFN_LEAD + tl + FFN_LEAD
    return pl.pallas_call(
        functools.partial(_ffn_kernel, tiles_per_seq=tpb),
        grid=(t // tl,),
        in_specs=[pl.BlockSpec((tl, d), lambda i: (i, 0)),
                  pl.BlockSpec((tl, d), lambda i: (i, 0)),
                  pl.BlockSpec((HALO, d), lambda i: (jnp.maximum(i * hb - 1, 0), 0)),
                  pl.BlockSpec((HALO, d), lambda i: (jnp.minimum((i + 1) * hb, last), 0)),
                  pl.BlockSpec((1, N_MOD, d), lambda i: (i // tpb, 0, 0)),
                  _const_spec((1, d)),
                  _const_spec(w_u.shape), _const_spec(cw.shape), _const_spec(cb.shape),
                  _const_spec(w_d.shape), _const_spec((1, d))],
        out_specs=pl.BlockSpec((tl, d), lambda i: (i, 0)),
        out_shape=jax.ShapeDtypeStruct((t, d), F32),
        scratch_shapes=[pltpu.VMEM((rows, d), BF16),
                        pltpu.VMEM((2, 2, FFN_CHUNK // LANES, rows, LANES), F32),
                        pltpu.VMEM((tl, FFN_DIM), BF16),
                        pltpu.VMEM((d // LANES, FFN_SUB, LANES), F32)],
        compiler_params=_params(("parallel",)),
        name="ffn",
    )(x1, h2, x1, x1, mod3, norm_w, w_u, cw, cb, w_d, fnorm_w)


def _rope_tables(seq):
    n_freq = QK_DIM // 4
    lane = jnp.arange(LANES)
    inv = ROPE_BASE ** (-(lane % n_freq).astype(F32) / n_freq)
    sign = jnp.where(lane < LANES // 2, -1.0, 1.0).astype(F32)

    def table(n):
        ang = jnp.arange(n, dtype=F32)[:, None] * inv[None, :]
        return jnp.stack([jnp.cos(ang), jnp.sin(ang) * sign])

    return table(seq // GRID_W), table(GRID_W)


def kernel(x, c, ctx, c_ctx, w_mod, b_mod, norm1_w, w_in, ret_decay_f, ret_decay_b,
           w_ret_out, w_four_out, w_branch_gate, b_branch_gate, w_out, norm2_w,
           w_up, conv_w, conv_b, w_down, final_norm_w):
    assert w_mod.shape[0] == 1, "single-layer block"
    b, seq, d = x.shape
    t = b * seq
    assert b == 2 and seq % RET_CHUNK == 0
    assert seq % FFN_TILE == 0 and seq % MERGE_TILE == 0 and seq % PROJ_TILE == 0

    c8 = jnp.concatenate([c, c_ctx[None, :], jnp.zeros((8 - b - 1, d), F32)], axis=0)
    mod, w_in_b, w_qk = _modulation(c8, w_mod[0], b_mod[0], w_in[0])
    mod3 = mod.reshape(8, N_MOD, d)
    n1w = norm1_w[0].reshape(1, d)

    kc, vc = _ctx_proj(ctx, mod3, n1w, w_qk, w_in_b)
    rtab, ctab = _rope_tables(seq)
    x2 = x.reshape(t, d)
    (q, k, v, sg, f), (w_bg, w_ro, w_fo, w_o) = _proj(
        x2, mod3, n1w, w_qk, w_in_b, rtab, ctab, seq,
        [w_branch_gate[0], w_ret_out[0], w_four_out[0], w_out[0]])

    a_f = jnp.broadcast_to(ret_decay_f[0][:, None, None], (HEADS, 1, LANES))
    a_b = jnp.broadcast_to(ret_decay_b[0][:, None, None], (HEADS, 1, LANES))
    y, tw, w2, w_c = _dft_first_stage(f.reshape(b, seq, F_WIDTH))
    ro, z = _retention_and_dft(a_f, a_b, q.reshape(b, seq, QK_WIDTH), k.reshape(b, seq, QK_WIDTH),
                               v.reshape(b, seq, V_WIDTH), kc, vc, y, tw, w2, w_c)

    n2w = norm2_w[0].reshape(1, d)
    (x1, h2), (w_u, w_d) = _merge(
        x2, mod3, n1w, n2w, ro.reshape(t, V_WIDTH), sg, z.reshape(t, F_WIDTH),
        w_bg, b_branch_gate[0].reshape(1, 2 * d), w_ro, w_fo, w_o, seq, [w_up[0], w_down[0]])

    out = _ffn(x1, h2, mod3, n2w, w_u, conv_w[0], conv_b[0].reshape(1, 2 * FFN_DIM), w_d,
               final_norm_w.reshape(1, d), seq)
    return out.reshape(b, seq, d)
```

```python
import functools

import numpy as np
import jax
import jax.numpy as jnp
from jax import lax
from jax.experimental import pallas as pl
from jax.experimental.pallas import tpu as pltpu

F32 = jnp.float32
BF16 = jnp.bfloat16

D_MODEL = 1024
GRID_W = 64
HEADS = 8
QK_DIM = 64
V_DIM = 128
QK_WIDTH = HEADS * QK_DIM
V_WIDTH = HEADS * V_DIM
ROPE_BASE = 10000.0
F_GROUPS = 4
F_GROUP_DIM = 128
F_WIDTH = F_GROUPS * F_GROUP_DIM
K_OFF = QK_WIDTH
V_OFF = K_OFF + QK_WIDTH
G_OFF = V_OFF + V_WIDTH
F_OFF = G_OFF + V_WIDTH
IN_COLS = F_OFF + F_WIDTH
FFN_DIM = 2816
N_MOD = 6
EPS = 1e-6

LANES = 128
RET_CHUNK = 256
RET_UNROLL = 16
FFN_CHUNK = 256
N_FFN_CHUNKS = FFN_DIM // FFN_CHUNK
FFN_TILE = 512
MERGE_TILE = 1024
MERGE_SUB = 512
PROJ_TILE = 1024
PROJ_SUB = 256
DFT_N1 = 64
BF16_ROWS = 16
DFT_ROWS = BF16_ROWS
DFT_A_ROWS = 2 * DFT_ROWS
VMEM_LIMIT = 56 * 1024 * 1024


def _params(sem):
    return pltpu.CompilerParams(dimension_semantics=sem, vmem_limit_bytes=VMEM_LIMIT)


def _dot(a, b):
    return jnp.dot(a, b, preferred_element_type=F32)


def _rms_norm(x, w):
    return x * lax.rsqrt(jnp.mean(x * x, axis=-1, keepdims=True) + EPS) * w


def _mod_norm(x, w, shift, scale):
    return x * lax.rsqrt(jnp.mean(x * x, axis=-1, keepdims=True) + EPS) * (w * (1.0 + scale)) + shift


def _const_spec(shape):
    zeros = (0,) * len(shape)
    return pl.BlockSpec(shape, lambda *_: zeros, pipeline_mode=pl.Buffered(1))


def _cast_jobs(weights, steps):
    specs, shapes = [], []
    for w in weights:
        n_rows = w.shape[0]
        rows = -(-n_rows // steps)
        while rows % BF16_ROWS or n_rows % rows:
            rows += 1
        last = n_rows // rows - 1
        specs.append(pl.BlockSpec((rows, w.shape[1]), lambda i, last=last: (jnp.minimum(i, last), 0)))
        shapes.append(jax.ShapeDtypeStruct(w.shape, BF16))
    return specs, shapes


def _run_casts(refs, scales=None):
    n = len(refs) // 2
    for j, (src, dst) in enumerate(zip(refs[:n], refs[n:])):
        scale = 1.0 if scales is None else scales[j]
        dst[...] = (src[...] if scale == 1.0 else src[...] * scale).astype(BF16)


def _mod_kernel(c_ref, w_ref, b_ref, win_ref, o_ref, winb_ref, wqk_ref):
    c = c_ref[...]
    s = c * jax.nn.sigmoid(c)
    o_ref[...] = _dot(s.astype(BF16), w_ref[...].astype(BF16)) + b_ref[...]
    wb = win_ref[...].astype(BF16)
    winb_ref[...] = wb
    n_qk = 2 * QK_WIDTH
    src = lax.broadcasted_iota(jnp.int32, (n_qk, n_qk), 0)
    dst = lax.broadcasted_iota(jnp.int32, (n_qk, n_qk), 1)
    select = (src == _qk_source_column(dst)).astype(BF16)
    wqk_ref[...] = _dot(wb[:, :n_qk], select).astype(BF16)


def _qk_source_column(col):
    half = QK_DIM // 2
    lane = col % LANES
    head = 2 * ((col % QK_WIDTH) // LANES) + (lane // half) % 2
    return (col // QK_WIDTH) * QK_WIDTH + head * QK_DIM + (lane // (2 * half)) * half + lane % half


def _modulation(c8, w_mod, b_mod, w_in):
    n = w_mod.shape[1]
    steps = 2
    tn = n // steps
    d, n_in = w_in.shape
    rows = d // steps
    return pl.pallas_call(
        _mod_kernel,
        grid=(steps,),
        in_specs=[_const_spec((8, D_MODEL)),
                  pl.BlockSpec((D_MODEL, tn), lambda j: (0, j)),
                  pl.BlockSpec((1, tn), lambda j: (0, j)),
                  pl.BlockSpec((rows, n_in), lambda j: (j, 0))],
        out_specs=[pl.BlockSpec((8, tn), lambda j: (0, j)),
                   pl.BlockSpec((rows, n_in), lambda j: (j, 0)),
                   pl.BlockSpec((rows, 2 * QK_WIDTH), lambda j: (j, 0))],
        out_shape=[jax.ShapeDtypeStruct((8, n), F32),
                   jax.ShapeDtypeStruct((d, n_in), BF16),
                   jax.ShapeDtypeStruct((d, 2 * QK_WIDTH), BF16)],
        compiler_params=_params(("parallel",)),
        name="mod",
    )(c8, w_mod, b_mod.reshape(1, n), w_in)


def _ctx_kernel(x_ref, mod_ref, nw_ref, wk_ref, wv_ref, k_ref, v_ref):
    x = x_ref[0]
    h = _mod_norm(x, nw_ref[...], mod_ref[0, 0:1, :], mod_ref[0, 1:2, :])
    hb = h.astype(BF16)
    k_ref[0] = _dot(hb, wk_ref[...]).astype(BF16)
    v_ref[0] = _dot(hb, wv_ref[...]).astype(BF16)


def _ctx_proj(ctx, mod3, norm_w, w_qk, w_in):
    b, lc, d = ctx.shape
    assert V_OFF % V_WIDTH == 0
    return pl.pallas_call(
        _ctx_kernel,
        grid=(b,),
        in_specs=[pl.BlockSpec((1, lc, d), lambda i: (i, 0, 0)),
                  pl.BlockSpec((1, N_MOD, d), lambda i: (2, 0, 0)),
                  _const_spec((1, d)),
                  pl.BlockSpec((d, QK_WIDTH), lambda i: (0, 1), pipeline_mode=pl.Buffered(1)),
                  pl.BlockSpec((d, V_WIDTH), lambda i: (0, V_OFF // V_WIDTH),
                               pipeline_mode=pl.Buffered(1))],
        out_specs=[pl.BlockSpec((1, lc, QK_WIDTH), lambda i: (i, 0, 0)),
                   pl.BlockSpec((1, lc, V_WIDTH), lambda i: (i, 0, 0))],
        out_shape=[jax.ShapeDtypeStruct((b, lc, QK_WIDTH), BF16),
                   jax.ShapeDtypeStruct((b, lc, V_WIDTH), BF16)],
        compiler_params=_params(("parallel",)),
        name="ctx_proj",
    )(ctx, mod3, norm_w, w_qk, w_in)


def _proj_kernel(x_ref, mod_ref, nw_ref, wqk_ref, w_ref, rtab_ref, ctab_ref, *refs, cast_scales):
    n_cast = len(cast_scales)
    q_ref, k_ref, v_ref, sg_ref, f_ref = refs[n_cast:n_cast + 5]
    _run_casts(refs[:n_cast] + refs[n_cast + 5:], cast_scales)
    lane = lax.broadcasted_iota(jnp.int32, (1, LANES), 1)
    by_row = (lane % (QK_DIM // 2)) < QK_DIM // 4

    def rope(t, trig, scale, out_ref, rows):
        for j in range(QK_WIDTH // LANES):
            tj = t[:, j * LANES:(j + 1) * LANES]
            r = tj * trig[0] + pltpu.roll(tj, LANES // 2, 1) * trig[1]
            out_ref[rows, j * LANES:(j + 1) * LANES] = (r * scale).astype(BF16)

    for r0 in range(0, x_ref.shape[0], PROJ_SUB):
        rows = slice(r0, r0 + PROJ_SUB)
        trig = []
        for cs in range(2):
            trig.append(jnp.concatenate(
                [jnp.where(by_row, rtab_ref[cs, g:g + 1, :], ctab_ref[cs])
                 for g in range(r0 // GRID_W, (r0 + PROJ_SUB) // GRID_W)], axis=0))
        x = x_ref[rows, :]
        h = _mod_norm(x, nw_ref[...], mod_ref[0, 0:1, :], mod_ref[0, 1:2, :])
        hb = h.astype(BF16)
        rope(_dot(hb, wqk_ref[:, :QK_WIDTH]), trig, QK_DIM ** -0.5, q_ref, rows)
        rope(_dot(hb, wqk_ref[:, QK_WIDTH:]), trig, 1.0, k_ref, rows)
        v_ref[rows, :] = _dot(hb, w_ref[:, V_OFF:G_OFF]).astype(BF16)
        g = _dot(hb, w_ref[:, G_OFF:F_OFF])
        hg = 0.5 * g
        sg_ref[rows, :] = (hg * jnp.tanh(hg) + hg).astype(BF16)
        f_ref[rows, :] = _dot(hb, w_ref[:, F_OFF:IN_COLS]).astype(BF16)


def _proj(x2, mod3, norm_w, w_qk, w_in, rtab, ctab, seq, to_cast, cast_scales):
    t, d = x2.shape
    tl = PROJ_TILE
    tpb = seq // tl
    tok = lambda w: pl.BlockSpec((tl, w), lambda i: (i, 0))
    cast_specs, cast_shapes = _cast_jobs(to_cast, t // tl)
    out = pl.pallas_call(
        functools.partial(_proj_kernel, cast_scales=tuple(cast_scales)),
        grid=(t // tl,),
        in_specs=[tok(d),
                  pl.BlockSpec((1, N_MOD, d), lambda i: (i // tpb, 0, 0)),
                  _const_spec((1, d)),
                  _const_spec(w_qk.shape), _const_spec(w_in.shape),
                  pl.BlockSpec((2, tl // GRID_W, LANES), lambda i: (0, i % tpb, 0)),
                  _const_spec(ctab.shape)] + cast_specs,
        out_specs=[tok(QK_WIDTH), tok(QK_WIDTH), tok(V_WIDTH), tok(V_WIDTH),
                   tok(F_WIDTH)] + cast_specs,
        out_shape=[jax.ShapeDtypeStruct((t, QK_WIDTH), BF16),
                   jax.ShapeDtypeStruct((t, QK_WIDTH), BF16),
                   jax.ShapeDtypeStruct((t, V_WIDTH), BF16),
                   jax.ShapeDtypeStruct((t, V_WIDTH), BF16),
                   jax.ShapeDtypeStruct((t, F_WIDTH), BF16)] + cast_shapes,
        compiler_params=_params(("arbitrary",)),
        name="proj",
    )(x2, mod3, norm_w, w_qk, w_in, rtab, ctab, *to_cast)
    return out[:5], out[5:]


def _retention_kernel(af_ref, ab_ref, q_ref, k_ref, v_ref, kc_ref, vc_ref,
                      tw_ref, w2_ref, cc_ref, y_ref,
                      o_ref, z_ref, ds_scr, st_scr, d_scr, tab_scr, z_scr):
    c = RET_CHUNK
    seq = q_ref.shape[1]
    lc = kc_ref.shape[1]
    n = seq // c
    lgf = [-jnp.exp(af_ref[hh]) for hh in range(2)]
    lgb = [-jnp.exp(ab_ref[hh]) for hh in range(2)]
    half = QK_DIM // 2

    lane_head = (lax.broadcasted_iota(jnp.int32, (1, LANES), 1) // half) % 2
    masks = [(lane_head == hh).astype(BF16) for hh in range(2)]
    lgf_lane = jnp.where(lane_head == 0, lgf[0], lgf[1])
    lgb_lane = jnp.where(lane_head == 0, lgb[0], lgb[1])
    row_head = (lax.broadcasted_iota(jnp.int32, (LANES, 1), 0) // half) % 2
    lgf_row = jnp.where(row_head == 0, lgf[0][:, 0:1], lgf[1][:, 0:1])
    lgb_row = jnp.where(row_head == 0, lgb[0][:, 0:1], lgb[1][:, 0:1])

    pos = lax.broadcasted_iota(jnp.int32, (c, LANES), 0).astype(F32)
    tab_scr[0] = jnp.exp(lgf_lane * (pos + 1.0))
    tab_scr[1] = jnp.exp(lgb_lane * (c - pos))
    diff = (lax.broadcasted_iota(jnp.int32, (c, c), 0)
            - lax.broadcasted_iota(jnp.int32, (c, c), 1)).astype(F32)
    for hh in range(2):
        d_scr[hh] = (jnp.where(diff >= 0, jnp.exp(lgf[hh][:, 0:1] * jnp.maximum(diff, 0.0)), 0.0)
                     + jnp.where(diff <= 0, jnp.exp(lgb[hh][:, 0:1] * jnp.maximum(-diff, 0.0)), 0.0))

    def k_decays(tokens):
        t = lax.broadcasted_iota(jnp.int32, (1, tokens), 1).astype(F32)
        return jnp.exp(lgf_row * (tokens - 1.0 - t)), jnp.exp(lgb_row * t)

    def state_increment(k_rows, v_rows, decays):
        kt = jnp.transpose(k_rows.astype(F32))
        lhs = jnp.concatenate([kt * decays[0], kt * decays[1]], axis=0).astype(BF16)
        return _dot(lhs, v_rows)

    kdec = k_decays(c)

    def incr(i, carry):
        rows = pl.ds(pl.multiple_of(i * c, c), c)
        ds_scr[i] = state_increment(k_ref[0, rows, :], v_ref[0, rows, :], kdec)
        return carry

    lax.fori_loop(0, n, incr, 0, unroll=RET_UNROLL)

    s0 = state_increment(kc_ref[0], vc_ref[0], k_decays(lc))
    col_head = lax.broadcasted_iota(jnp.int32, (1, 2 * V_DIM), 1) // V_DIM
    own = (row_head == col_head).astype(F32)
    decay_f = jnp.exp(lgf_row * c) * own
    decay_b = jnp.exp(lgb_row * c) * own

    def scan_f(i, s):
        st_scr[i, 0:LANES, :] = (s * own).astype(BF16)
        return decay_f * s + ds_scr[i, 0:LANES, :]

    lax.fori_loop(0, n, scan_f, s0[0:LANES])

    def scan_b(t, s):
        i = n - 1 - t
        st_scr[i, LANES:2 * LANES, :] = (s * own).astype(BF16)
        return decay_b * s + ds_scr[i, LANES:2 * LANES, :]

    lax.fori_loop(0, n, scan_b, s0[LANES:2 * LANES])

    def outputs(i):
        rows = pl.ds(pl.multiple_of(i * c, c), c)
        q = q_ref[0, rows, :]
        k = k_ref[0, rows, :]
        qf = q.astype(F32)
        qd = jnp.concatenate([qf * tab_scr[0], qf * tab_scr[1]], axis=1).astype(BF16)
        inter = _dot(qd, st_scr[i])
        for hh in range(2):
            cols = slice(hh * V_DIM, (hh + 1) * V_DIM)
            scores = lax.dot_general(q * masks[hh], k, (((1,), (1,)), ((), ())),
                                     preferred_element_type=F32)
            o = _dot((scores * d_scr[hh]).astype(BF16), v_ref[0, rows, cols]) + inter[:, cols]
            o_ref[0, rows, cols] = o.astype(BF16)

    per_k1 = n // DFT_ROWS
    cc = cc_ref[...].astype(BF16)

    def outputs_and_dft(j, carry):
        for u in range(per_k1):
            outputs(j * per_k1 + u)
        _dft_second_stage(j, tw_ref, w2_ref, cc, y_ref, z_scr)
        return carry

    lax.fori_loop(0, DFT_ROWS, outputs_and_dft, 0, unroll=RET_UNROLL // per_k1)
    _dft_emit(z_scr, z_ref)


def _retention_and_dft(a_f, a_b, q, k, v, kc, vc, y, tw, w2, w_c):
    b, seq, _ = v.shape
    lc = kc.shape[1]
    c = RET_CHUNK
    n = seq // c
    r = DFT_ROWS
    _, _, n1, n2, w = y.shape
    assert n % r == 0 and n1 // r == HEADS // 2
    dec = pl.BlockSpec((2, 1, LANES), lambda i, p: (p, 0, 0))
    qk = lambda rows: pl.BlockSpec((1, rows, LANES), lambda i, p: (i, 0, p))
    vv = lambda rows: pl.BlockSpec((1, rows, 2 * V_DIM), lambda i, p: (i, 0, p))
    return pl.pallas_call(
        _retention_kernel,
        grid=(b, HEADS // 2),
        in_specs=[dec, dec, qk(seq), qk(seq), vv(seq), qk(lc), vv(lc),
                  pl.BlockSpec((r, 2, n2), lambda i, p: (p, 0, 0)),
                  _const_spec((2, n2, n2)),
                  _const_spec((2 * F_GROUP_DIM, F_GROUP_DIM)),
                  pl.BlockSpec((1, 2, r, n2, w), lambda i, p: (i, 0, p, 0, 0))],
        out_specs=[vv(seq), pl.BlockSpec((1, n2, r, w), lambda i, p: (i, 0, p, 0))],
        out_shape=[jax.ShapeDtypeStruct((b, seq, V_WIDTH), BF16),
                   jax.ShapeDtypeStruct((b, n2, n1, w), BF16)],
        scratch_shapes=[pltpu.VMEM((n, 2 * LANES, 2 * V_DIM), F32),
                        pltpu.VMEM((n, 2 * LANES, 2 * V_DIM), BF16),
                        pltpu.VMEM((2, c, c), F32),
                        pltpu.VMEM((2, c, LANES), F32),
                        pltpu.VMEM((F_GROUPS, n2 * r, F_GROUP_DIM), F32)],
        compiler_params=_params(("parallel", "parallel")),
        name="retention_dft",
    )(a_f, a_b, q, k, v, kc, vc, tw, w2, w_c, y)


def _dft_a_kernel(w_ref, x_ref, y_ref):
    _, n1, rows, w = x_ref.shape
    r = DFT_ROWS
    nh = n1 // 2 + 1
    for r0 in range(0, rows, r):
        x = x_ref[0, :, r0:r0 + r, :].reshape(n1 * r, w)
        y = _dot(w_ref[...], x).reshape(2, nh, r, w).astype(BF16)
        y_ref[0, :, 0:nh, r0:r0 + r, :] = y
        for k1 in range(1, n1 // 2):
            y_ref[0, 0, n1 - k1, r0:r0 + r, :] = y[0, k1]
            y_ref[0, 1, n1 - k1, r0:r0 + r, :] = -y[1, k1]


def _dft_second_stage(j, tw_ref, w2_ref, cc, y_ref, z_scr):
    n2 = y_ref.shape[3]
    w2c = w2_ref[0]
    w2s = w2_ref[1]
    tc = tw_ref[j, 0:1, :]
    ts = tw_ref[j, 1:2, :]
    ec = w2c * tc - w2s * ts
    es = w2s * tc + w2c * ts
    m = jnp.concatenate([jnp.concatenate([ec, -es], axis=1),
                         jnp.concatenate([es, ec], axis=1)], axis=0).astype(BF16)
    y = jnp.concatenate([y_ref[0, 0, j], y_ref[0, 1, j]], axis=0)
    zz = _dot(m, y)
    for g in range(F_GROUPS):
        cols = slice(g * F_GROUP_DIM, (g + 1) * F_GROUP_DIM)
        zcs = jnp.concatenate([zz[:n2, cols], zz[n2:, cols]], axis=1).astype(BF16)
        z_scr[g, pl.ds(j, n2, stride=DFT_ROWS), :] = _dot(zcs, cc)


def _dft_emit(z_scr, z_ref):
    n2 = z_ref.shape[1]
    z = jnp.concatenate([z_scr[g] for g in range(F_GROUPS)], axis=1)
    z_ref[0] = z.reshape(n2, DFT_ROWS, F_WIDTH).astype(BF16)


def _dft_tables(seq):
    n1 = DFT_N1
    n2 = seq // n1

    def cs(num, den):
        ang = 2.0 * np.pi * (num % den) / den
        return np.cos(ang), np.sin(ang)

    a = np.arange(n1)
    w_a = np.concatenate(cs(a[:n1 // 2 + 1, None] * a[None, :], n1), axis=0)
    m = np.arange(n2)
    tw = np.stack(cs(a[:, None] * m[None, :], seq), axis=1)
    w2 = np.stack(cs(m[:, None] * m[None, :], n2), axis=0)
    ch = np.arange(F_GROUP_DIM)
    cc, sc = cs(ch[:, None] * ch[None, :], F_GROUP_DIM)
    scale = 1.0 / np.sqrt(seq * F_GROUP_DIM)
    w_c = np.concatenate([cc, -sc], axis=0) * scale
    return [jnp.asarray(t, dtype=F32) for t in (w_a, tw, w2, w_c)]


def _dft_first_stage(f):
    b, seq, w = f.shape
    n1 = DFT_N1
    n2 = seq // n1
    w_a, tw, w2, w_c = _dft_tables(seq)
    r = DFT_ROWS
    spread = (jnp.arange(n1 * r)[None, :] // r == jnp.arange(n1)[:, None]).astype(F32)
    w_rep = jnp.dot(w_a, spread, precision=lax.Precision.HIGHEST)
    same_r = jnp.arange(w_a.shape[0] * r)[:, None] % r == jnp.arange(n1 * r)[None, :] % r
    w_a = jnp.where(same_r, jnp.repeat(w_rep, r, axis=0), 0.0).astype(BF16)
    y = pl.pallas_call(
        _dft_a_kernel,
        grid=(b, n2 // DFT_A_ROWS),
        in_specs=[_const_spec(w_a.shape),
                  pl.BlockSpec((1, n1, DFT_A_ROWS, w), lambda i, j: (i, 0, j, 0))],
        out_specs=pl.BlockSpec((1, 2, n1, DFT_A_ROWS, w), lambda i, j: (i, 0, 0, j, 0)),
        out_shape=jax.ShapeDtypeStruct((b, 2, n1, n2, w), BF16),
        compiler_params=_params(("parallel", "parallel")),
        name="dft_a",
    )(w_a, f.reshape(b, n1, n2, w))
    return y, tw, w2, w_c


def _merge_kernel(x_ref, mod_ref, nw_ref, n2w_ref, ro_ref, sg_ref, z_ref, wbg_ref, bbg_ref,
                  wro_ref, wfo_ref, wout_ref, *refs, n_cast):
    o_ref, h2_ref = refs[n_cast:n_cast + 2]
    _run_casts(refs[:n_cast] + refs[n_cast + 2:])
    for r0 in range(0, x_ref.shape[0], MERGE_SUB):
        rows = slice(r0, r0 + MERGE_SUB)
        x = x_ref[rows, :]
        h = _mod_norm(x, nw_ref[...], mod_ref[0, 0:1, :], mod_ref[0, 1:2, :])
        t = jnp.tanh(_dot(h.astype(BF16), wbg_ref[...]) + 0.5 * bbg_ref[...])
        gated = []
        for hd in range(HEADS):
            cols = slice(hd * V_DIM, (hd + 1) * V_DIM)
            o = ro_ref[rows, cols].astype(F32)
            oc = o - jnp.mean(o, axis=-1, keepdims=True)
            var = jnp.mean(oc * oc, axis=-1, keepdims=True)
            gated.append((oc * lax.rsqrt(var + EPS) * sg_ref[rows, cols].astype(F32)).astype(BF16))
        ret_d = _dot(jnp.concatenate(gated, axis=1), wro_ref[...])
        four_d = _dot(z_ref[rows, :], wfo_ref[...])
        m2 = (t[:, :D_MODEL] * ret_d + ret_d) + (t[:, D_MODEL:] * four_d + four_d)
        y2 = _dot(m2.astype(BF16), wout_ref[...])
        x1 = x + (0.5 * mod_ref[0, 2:3, :]) * y2
        o_ref[rows, :] = x1
        h2 = _mod_norm(x1, n2w_ref[...], mod_ref[0, 3:4, :], mod_ref[0, 4:5, :])
        h2_ref[rows, :] = h2.astype(BF16)


def _merge(x2, mod3, norm_w, norm2_w, ro, sg, z, w_bg, b_bg, w_ro, w_fo, w_out, seq, to_cast):
    t, d = x2.shape
    tl = MERGE_TILE
    tpb = seq // tl
    tok = lambda w: pl.BlockSpec((tl, w), lambda i: (i, 0))
    cast_specs, cast_shapes = _cast_jobs(to_cast, t // tl)
    out = pl.pallas_call(
        functools.partial(_merge_kernel, n_cast=len(to_cast)),
        grid=(t // tl,),
        in_specs=[tok(d),
                  pl.BlockSpec((1, N_MOD, d), lambda i: (i // tpb, 0, 0)),
                  _const_spec((1, d)), _const_spec((1, d)),
                  tok(V_WIDTH), tok(V_WIDTH), tok(F_WIDTH),
                  _const_spec(w_bg.shape), _const_spec((1, 2 * d)),
                  _const_spec(w_ro.shape), _const_spec(w_fo.shape),
                  _const_spec(w_out.shape)] + cast_specs,
        out_specs=[tok(d), tok(d)] + cast_specs,
        out_shape=[jax.ShapeDtypeStruct((t, d), F32),
                   jax.ShapeDtypeStruct((t, d), BF16)] + cast_shapes,
        compiler_params=_params(("arbitrary",)),
        name="merge",
    )(x2, mod3, norm_w, norm2_w, ro, sg, z, w_bg, b_bg, w_ro, w_fo, w_out, *to_cast)
    return out[:2], out[2:]


HALO = 8
FFN_LEAD = 16
FFN_SUB = 256


def _two_gelu_tanh(x):
    c1 = np.sqrt(2.0 / np.pi)
    return x + x * jnp.tanh(x * (c1 + (c1 * 0.044715) * (x * x)))


def _ffn_kernel(xm_ref, hm_ref, xp_ref, xn_ref, mod_ref, nw_ref, wu_ref, cw_ref, cb_ref,
                wd_ref, fnw_ref, o_ref, h_scr, u_scr, act_scr, y_scr, *, tiles_per_seq):
    tl = xm_ref.shape[0]
    sub = FFN_SUB
    half = sub // 2
    nc = FFN_CHUNK
    i = pl.program_id(0)
    keep_prev = ((i % tiles_per_seq) != 0).astype(F32)
    keep_next = ((i % tiles_per_seq) != tiles_per_seq - 1).astype(F32)

    def pre(x):
        return _mod_norm(x, nw_ref[...], mod_ref[0, 3:4, :], mod_ref[0, 4:5, :])

    pad = jnp.zeros((FFN_LEAD - HALO, xm_ref.shape[1]), F32)
    h_scr[0:FFN_LEAD] = jnp.concatenate([pad, pre(xp_ref[...]) * keep_prev], axis=0).astype(BF16)
    h_scr[FFN_LEAD:FFN_LEAD + tl] = hm_ref[...]
    h_scr[FFN_LEAD + tl:] = jnp.concatenate([pre(xn_ref[...]) * keep_next, pad], axis=0).astype(BF16)

    def project(j):
        hb = h_scr[...]
        for part in range(2):
            lo = part * FFN_DIM + j * nc
            u = _dot(hb, wu_ref[:, lo:lo + nc])
            for s in range(nc // LANES):
                u_scr[j % 2, part, s] = u[:, s * LANES:(s + 1) * LANES]

    def conv(j, part, s, blk, scale):
        lo = part * FFN_DIM + j * nc + s * LANES
        w = cw_ref[:, lo:lo + LANES] * scale
        bias = cb_ref[:, lo:lo + LANES] * scale
        first = FFN_LEAD + blk * sub
        rows = lambda start: u_scr[j % 2, part, s, pl.ds(start, half, stride=2), :]
        before, even, odd, after = rows(first - 1), rows(first), rows(first + 1), rows(first + 2)
        return (before * w[0:1] + even * w[1:2] + odd * w[2:3] + bias,
                even * w[0:1] + odd * w[1:2] + after * w[2:3] + bias)

    def activate(j):
        for s in range(nc // LANES):
            cols = slice(j * nc + s * LANES, j * nc + (s + 1) * LANES)
            for blk in range(tl // sub):
                gate = conv(j, 0, s, blk, 1.0)
                val = conv(j, 1, s, blk, 0.5)
                for par in range(2):
                    r0 = blk * sub + par * half
                    act_scr[r0:r0 + half, cols] = (_two_gelu_tanh(gate[par]) * val[par]).astype(BF16)

    project(0)
    for j in range(N_FFN_CHUNKS):
        if j + 1 < N_FFN_CHUNKS:
            project(j + 1)
        activate(j)
    for blk in range(tl // sub):
        r0 = blk * sub
        y = _dot(act_scr[r0:r0 + sub, :], wd_ref[...])
        for s in range(y_scr.shape[0]):
            for par in range(2):
                y_scr[s, pl.ds(par, half, stride=2), :] = (
                    y[par * half:(par + 1) * half, s * LANES:(s + 1) * LANES])
        y = jnp.concatenate([y_scr[s] for s in range(y_scr.shape[0])], axis=1)
        x2 = xm_ref[r0:r0 + sub, :] + mod_ref[0, 5:6, :] * y
        o_ref[r0:r0 + sub, :] = _rms_norm(x2, fnw_ref[...])


def _ffn(x1, h2, mod3, norm_w, w_u, cw, cb, w_d, fnorm_w, seq):
    t, d = x1.shape
    tl = FFN_TILE
    tpb = seq // tl
    hb = tl // HALO
    last = t // HALO - 1
    rows = FFN_LEAD + tl + FFN_LEAD
    return pl.pallas_call(
        functools.partial(_ffn_kernel, tiles_per_seq=tpb),
        grid=(t // tl,),
        in_specs=[pl.BlockSpec((tl, d), lambda i: (i, 0)),
                  pl.BlockSpec((tl, d), lambda i: (i, 0)),
                  pl.BlockSpec((HALO, d), lambda i: (jnp.maximum(i * hb - 1, 0), 0)),
                  pl.BlockSpec((HALO, d), lambda i: (jnp.minimum((i + 1) * hb, last), 0)),
                  pl.BlockSpec((1, N_MOD, d), lambda i: (i // tpb, 0, 0)),
                  _const_spec((1, d)),
                  _const_spec(w_u.shape), _const_spec(cw.shape), _const_spec(cb.shape),
                  _const_spec(w_d.shape), _const_spec((1, d))],
        out_specs=pl.BlockSpec((tl, d), lambda i: (i, 0)),
        out_shape=jax.ShapeDtypeStruct((t, d), F32),
        scratch_shapes=[pltpu.VMEM((rows, d), BF16),
                        pltpu.VMEM((2, 2, FFN_CHUNK // LANES, rows, LANES), F32),
                        pltpu.VMEM((tl, FFN_DIM), BF16),
                        pltpu.VMEM((d // LANES, FFN_SUB, LANES), F32)],
        compiler_params=_params(("parallel",)),
        name="ffn",
    )(x1, h2, x1, x1, mod3, norm_w, w_u, cw, cb, w_d, fnorm_w)


def _rope_tables(seq):
    n_freq = QK_DIM // 4
    lane = jnp.arange(LANES)
    inv = ROPE_BASE ** (-(lane % n_freq).astype(F32) / n_freq)
    sign = jnp.where(lane < LANES // 2, -1.0, 1.0).astype(F32)

    def table(n):
        ang = jnp.arange(n, dtype=F32)[:, None] * inv[None, :]
        return jnp.stack([jnp.cos(ang), jnp.sin(ang) * sign])

    return table(seq // GRID_W), table(GRID_W)


def kernel(x, c, ctx, c_ctx, w_mod, b_mod, norm1_w, w_in, ret_decay_f, ret_decay_b,
           w_ret_out, w_four_out, w_branch_gate, b_branch_gate, w_out, norm2_w,
           w_up, conv_w, conv_b, w_down, final_norm_w):
    assert w_mod.shape[0] == 1, "single-layer block"
    b, seq, d = x.shape
    t = b * seq
    assert b == 2 and seq % RET_CHUNK == 0
    assert seq % FFN_TILE == 0 and seq % MERGE_TILE == 0 and seq % PROJ_TILE == 0

    c8 = jnp.concatenate([c, c_ctx[None, :], jnp.zeros((8 - b - 1, d), F32)], axis=0)
    mod, w_in_b, w_qk = _modulation(c8, w_mod[0], b_mod[0], w_in[0])
    mod3 = mod.reshape(8, N_MOD, d)
    n1w = norm1_w[0].reshape(1, d)

    kc, vc = _ctx_proj(ctx, mod3, n1w, w_qk, w_in_b)
    rtab, ctab = _rope_tables(seq)
    x2 = x.reshape(t, d)
    (q, k, v, sg, f), (w_bg, w_ro, w_fo, w_o) = _proj(
        x2, mod3, n1w, w_qk, w_in_b, rtab, ctab, seq,
        [w_branch_gate[0], w_ret_out[0], w_four_out[0], w_out[0]], (0.5, 1.0, 1.0, 1.0))

    a_f = jnp.broadcast_to(ret_decay_f[0][:, None, None], (HEADS, 1, LANES))
    a_b = jnp.broadcast_to(ret_decay_b[0][:, None, None], (HEADS, 1, LANES))
    y, tw, w2, w_c = _dft_first_stage(f.reshape(b, seq, F_WIDTH))
    ro, z = _retention_and_dft(a_f, a_b, q.reshape(b, seq, QK_WIDTH), k.reshape(b, seq, QK_WIDTH),
                               v.reshape(b, seq, V_WIDTH), kc, vc, y, tw, w2, w_c)

    n2w = norm2_w[0].reshape(1, d)
    (x1, h2), (w_u, w_d) = _merge(
        x2, mod3, n1w, n2w, ro.reshape(t, V_WIDTH), sg, z.reshape(t, F_WIDTH),
        w_bg, b_branch_gate[0].reshape(1, 2 * d), w_ro, w_fo, w_o, seq, [w_up[0], w_down[0]])

    out = _ffn(x1, h2, mod3, n2w, w_u, conv_w[0], conv_b[0].reshape(1, 2 * FFN_DIM), w_d,
               final_norm_w.reshape(1, d), seq)
    return out.reshape(b, seq, d)
```

```python
import functools

import numpy as np
import jax
import jax.numpy as jnp
from jax import lax
from jax.experimental import pallas as pl
from jax.experimental.pallas import tpu as pltpu

F32 = jnp.float32
BF16 = jnp.bfloat16

D_MODEL = 1024
GRID_W = 64
HEADS = 8
QK_DIM = 64
V_DIM = 128
QK_WIDTH = HEADS * QK_DIM
V_WIDTH = HEADS * V_DIM
ROPE_BASE = 10000.0
F_GROUPS = 4
F_GROUP_DIM = 128
F_WIDTH = F_GROUPS * F_GROUP_DIM
K_OFF = QK_WIDTH
V_OFF = K_OFF + QK_WIDTH
G_OFF = V_OFF + V_WIDTH
F_OFF = G_OFF + V_WIDTH
IN_COLS = F_OFF + F_WIDTH
FFN_DIM = 2816
N_MOD = 6
EPS = 1e-6

LANES = 128
RET_CHUNK = 256
RET_UNROLL = 16
FFN_CHUNK = 256
N_FFN_CHUNKS = FFN_DIM // FFN_CHUNK
FFN_TILE = 512
MERGE_TILE = 1024
MERGE_SUB = 512
PROJ_TILE = 1024
PROJ_SUB = 256
DFT_N1 = 64
BF16_ROWS = 16
DFT_ROWS = BF16_ROWS
DFT_A_ROWS = 2 * DFT_ROWS
DFT_PAD = 8
VMEM_LIMIT = 56 * 1024 * 1024


def _params(sem):
    return pltpu.CompilerParams(dimension_semantics=sem, vmem_limit_bytes=VMEM_LIMIT)


def _dot(a, b):
    return jnp.dot(a, b, preferred_element_type=F32)


def _rms_norm(x, w):
    return x * lax.rsqrt(jnp.mean(x * x, axis=-1, keepdims=True) + EPS) * w


def _mod_norm(x, w, shift, scale):
    return x * lax.rsqrt(jnp.mean(x * x, axis=-1, keepdims=True) + EPS) * (w * (1.0 + scale)) + shift


def _const_spec(shape):
    zeros = (0,) * len(shape)
    return pl.BlockSpec(shape, lambda *_: zeros, pipeline_mode=pl.Buffered(1))


def _cast_jobs(weights, steps):
    specs, shapes = [], []
    for w in weights:
        n_rows = w.shape[0]
        rows = -(-n_rows // steps)
        while rows % BF16_ROWS or n_rows % rows:
            rows += 1
        last = n_rows // rows - 1
        specs.append(pl.BlockSpec((rows, w.shape[1]), lambda i, last=last: (jnp.minimum(i, last), 0)))
        shapes.append(jax.ShapeDtypeStruct(w.shape, BF16))
    return specs, shapes


def _run_casts(refs, scales=None):
    n = len(refs) // 2
    for j, (src, dst) in enumerate(zip(refs[:n], refs[n:])):
        scale = 1.0 if scales is None else scales[j]
        dst[...] = (src[...] if scale == 1.0 else src[...] * scale).astype(BF16)


def _mod_kernel(c_ref, w_ref, b_ref, win_ref, o_ref, winb_ref, wqk_ref):
    c = c_ref[...]
    s = c * jax.nn.sigmoid(c)
    o_ref[...] = _dot(s.astype(BF16), w_ref[...].astype(BF16)) + b_ref[...]
    wb = win_ref[...].astype(BF16)
    winb_ref[...] = wb
    n_qk = 2 * QK_WIDTH
    src = lax.broadcasted_iota(jnp.int32, (n_qk, n_qk), 0)
    dst = lax.broadcasted_iota(jnp.int32, (n_qk, n_qk), 1)
    select = (src == _qk_source_column(dst)).astype(BF16)
    wqk_ref[...] = _dot(wb[:, :n_qk], select).astype(BF16)


def _qk_source_column(col):
    half = QK_DIM // 2
    lane = col % LANES
    head = 2 * ((col % QK_WIDTH) // LANES) + (lane // half) % 2
    return (col // QK_WIDTH) * QK_WIDTH + head * QK_DIM + (lane // (2 * half)) * half + lane % half


def _modulation(c8, w_mod, b_mod, w_in):
    n = w_mod.shape[1]
    steps = 2
    tn = n // steps
    d, n_in = w_in.shape
    rows = d // steps
    return pl.pallas_call(
        _mod_kernel,
        grid=(steps,),
        in_specs=[_const_spec((8, D_MODEL)),
                  pl.BlockSpec((D_MODEL, tn), lambda j: (0, j)),
                  pl.BlockSpec((1, tn), lambda j: (0, j)),
                  pl.BlockSpec((rows, n_in), lambda j: (j, 0))],
        out_specs=[pl.BlockSpec((8, tn), lambda j: (0, j)),
                   pl.BlockSpec((rows, n_in), lambda j: (j, 0)),
                   pl.BlockSpec((rows, 2 * QK_WIDTH), lambda j: (j, 0))],
        out_shape=[jax.ShapeDtypeStruct((8, n), F32),
                   jax.ShapeDtypeStruct((d, n_in), BF16),
                   jax.ShapeDtypeStruct((d, 2 * QK_WIDTH), BF16)],
        compiler_params=_params(("parallel",)),
        name="mod",
    )(c8, w_mod, b_mod.reshape(1, n), w_in)


def _ctx_kernel(x_ref, mod_ref, nw_ref, wk_ref, wv_ref, k_ref, v_ref):
    x = x_ref[0]
    h = _mod_norm(x, nw_ref[...], mod_ref[0, 0:1, :], mod_ref[0, 1:2, :])
    hb = h.astype(BF16)
    k_ref[0] = _dot(hb, wk_ref[...]).astype(BF16)
    v_ref[0] = _dot(hb, wv_ref[...]).astype(BF16)


def _ctx_proj(ctx, mod3, norm_w, w_qk, w_in):
    b, lc, d = ctx.shape
    assert V_OFF % V_WIDTH == 0
    return pl.pallas_call(
        _ctx_kernel,
        grid=(b,),
        in_specs=[pl.BlockSpec((1, lc, d), lambda i: (i, 0, 0)),
                  pl.BlockSpec((1, N_MOD, d), lambda i: (2, 0, 0)),
                  _const_spec((1, d)),
                  pl.BlockSpec((d, QK_WIDTH), lambda i: (0, 1), pipeline_mode=pl.Buffered(1)),
                  pl.BlockSpec((d, V_WIDTH), lambda i: (0, V_OFF // V_WIDTH),
                               pipeline_mode=pl.Buffered(1))],
        out_specs=[pl.BlockSpec((1, lc, QK_WIDTH), lambda i: (i, 0, 0)),
                   pl.BlockSpec((1, lc, V_WIDTH), lambda i: (i, 0, 0))],
        out_shape=[jax.ShapeDtypeStruct((b, lc, QK_WIDTH), BF16),
                   jax.ShapeDtypeStruct((b, lc, V_WIDTH), BF16)],
        compiler_params=_params(("parallel",)),
        name="ctx_proj",
    )(ctx, mod3, norm_w, w_qk, w_in)


def _proj_kernel(x_ref, mod_ref, nw_ref, wqk_ref, w_ref, rtab_ref, ctab_ref, *refs, cast_scales):
    n_cast = len(cast_scales)
    q_ref, k_ref, v_ref, sg_ref, f_ref = refs[n_cast:n_cast + 5]
    _run_casts(refs[:n_cast] + refs[n_cast + 5:], cast_scales)
    lane = lax.broadcasted_iota(jnp.int32, (1, LANES), 1)
    by_row = (lane % (QK_DIM // 2)) < QK_DIM // 4

    def rope(t, trig, scale, out_ref, rows):
        for j in range(QK_WIDTH // LANES):
            tj = t[:, j * LANES:(j + 1) * LANES]
            r = tj * trig[0] + pltpu.roll(tj, LANES // 2, 1) * trig[1]
            out_ref[rows, j * LANES:(j + 1) * LANES] = (r * scale).astype(BF16)

    for r0 in range(0, x_ref.shape[0], PROJ_SUB):
        rows = slice(r0, r0 + PROJ_SUB)
        trig = []
        for cs in range(2):
            trig.append(jnp.concatenate(
                [jnp.where(by_row, rtab_ref[cs, g:g + 1, :], ctab_ref[cs])
                 for g in range(r0 // GRID_W, (r0 + PROJ_SUB) // GRID_W)], axis=0))
        x = x_ref[rows, :]
        h = _mod_norm(x, nw_ref[...], mod_ref[0, 0:1, :], mod_ref[0, 1:2, :])
        hb = h.astype(BF16)
        rope(_dot(hb, wqk_ref[:, :QK_WIDTH]), trig, QK_DIM ** -0.5, q_ref, rows)
        rope(_dot(hb, wqk_ref[:, QK_WIDTH:]), trig, 1.0, k_ref, rows)
        v_ref[rows, :] = _dot(hb, w_ref[:, V_OFF:G_OFF]).astype(BF16)
        g = _dot(hb, w_ref[:, G_OFF:F_OFF])
        hg = 0.5 * g
        sg_ref[rows, :] = (hg * jnp.tanh(hg) + hg).astype(BF16)
        f_ref[rows, :] = _dot(hb, w_ref[:, F_OFF:IN_COLS]).astype(BF16)


def _proj(x2, mod3, norm_w, w_qk, w_in, rtab, ctab, seq, to_cast, cast_scales):
    t, d = x2.shape
    tl = PROJ_TILE
    tpb = seq // tl
    tok = lambda w: pl.BlockSpec((tl, w), lambda i: (i, 0))
    cast_specs, cast_shapes = _cast_jobs(to_cast, t // tl)
    out = pl.pallas_call(
        functools.partial(_proj_kernel, cast_scales=tuple(cast_scales)),
        grid=(t // tl,),
        in_specs=[tok(d),
                  pl.BlockSpec((1, N_MOD, d), lambda i: (i // tpb, 0, 0)),
                  _const_spec((1, d)),
                  _const_spec(w_qk.shape), _const_spec(w_in.shape),
                  pl.BlockSpec((2, tl // GRID_W, LANES), lambda i: (0, i % tpb, 0)),
                  _const_spec(ctab.shape)] + cast_specs,
        out_specs=[tok(QK_WIDTH), tok(QK_WIDTH), tok(V_WIDTH), tok(V_WIDTH),
                   tok(F_WIDTH)] + cast_specs,
        out_shape=[jax.ShapeDtypeStruct((t, QK_WIDTH), BF16),
                   jax.ShapeDtypeStruct((t, QK_WIDTH), BF16),
                   jax.ShapeDtypeStruct((t, V_WIDTH), BF16),
                   jax.ShapeDtypeStruct((t, V_WIDTH), BF16),
                   jax.ShapeDtypeStruct((t, F_WIDTH), BF16)] + cast_shapes,
        compiler_params=_params(("arbitrary",)),
        name="proj",
    )(x2, mod3, norm_w, w_qk, w_in, rtab, ctab, *to_cast)
    return out[:5], out[5:]


def _retention_kernel(af_ref, ab_ref, q_ref, k_ref, v_ref, kc_ref, vc_ref,
                      tw_ref, w2_ref, cc_ref, y_ref,
                      o_ref, z_ref, ds_scr, st_scr, d_scr, tab_scr, z_scr):
    c = RET_CHUNK
    seq = q_ref.shape[1]
    lc = kc_ref.shape[1]
    n = seq // c
    lgf = [-jnp.exp(af_ref[hh]) for hh in range(2)]
    lgb = [-jnp.exp(ab_ref[hh]) for hh in range(2)]
    half = QK_DIM // 2

    lane_head = (lax.broadcasted_iota(jnp.int32, (1, LANES), 1) // half) % 2
    masks = [(lane_head == hh).astype(BF16) for hh in range(2)]
    lgf_lane = jnp.where(lane_head == 0, lgf[0], lgf[1])
    lgb_lane = jnp.where(lane_head == 0, lgb[0], lgb[1])
    row_head = (lax.broadcasted_iota(jnp.int32, (LANES, 1), 0) // half) % 2
    lgf_row = jnp.where(row_head == 0, lgf[0][:, 0:1], lgf[1][:, 0:1])
    lgb_row = jnp.where(row_head == 0, lgb[0][:, 0:1], lgb[1][:, 0:1])

    pos = lax.broadcasted_iota(jnp.int32, (c, LANES), 0).astype(F32)
    tab_scr[0] = jnp.exp(lgf_lane * (pos + 1.0))
    tab_scr[1] = jnp.exp(lgb_lane * (c - pos))
    diff = (lax.broadcasted_iota(jnp.int32, (c, c), 0)
            - lax.broadcasted_iota(jnp.int32, (c, c), 1)).astype(F32)
    for hh in range(2):
        d_scr[hh] = (jnp.where(diff >= 0, jnp.exp(lgf[hh][:, 0:1] * jnp.maximum(diff, 0.0)), 0.0)
                     + jnp.where(diff <= 0, jnp.exp(lgb[hh][:, 0:1] * jnp.maximum(-diff, 0.0)), 0.0))

    def k_decays(tokens):
        t = lax.broadcasted_iota(jnp.int32, (1, tokens), 1).astype(F32)
        return jnp.exp(lgf_row * (tokens - 1.0 - t)), jnp.exp(lgb_row * t)

    def state_increment(k_rows, v_rows, decays):
        kt = jnp.transpose(k_rows.astype(F32))
        lhs = jnp.concatenate([kt * decays[0], kt * decays[1]], axis=0).astype(BF16)
        return _dot(lhs, v_rows)

    kdec = k_decays(c)

    def incr(i, carry):
        rows = pl.ds(pl.multiple_of(i * c, c), c)
        ds_scr[i] = state_increment(k_ref[0, rows, :], v_ref[0, rows, :], kdec)
        return carry

    lax.fori_loop(0, n, incr, 0, unroll=RET_UNROLL)

    s0 = state_increment(kc_ref[0], vc_ref[0], k_decays(lc))
    col_head = lax.broadcasted_iota(jnp.int32, (1, 2 * V_DIM), 1) // V_DIM
    own = (row_head == col_head).astype(F32)
    decay_f = jnp.exp(lgf_row * c) * own
    decay_b = jnp.exp(lgb_row * c) * own

    def scan_f(i, s):
        st_scr[i, 0:LANES, :] = (s * own).astype(BF16)
        return decay_f * s + ds_scr[i, 0:LANES, :]

    lax.fori_loop(0, n, scan_f, s0[0:LANES])

    def scan_b(t, s):
        i = n - 1 - t
        st_scr[i, LANES:2 * LANES, :] = (s * own).astype(BF16)
        return decay_b * s + ds_scr[i, LANES:2 * LANES, :]

    lax.fori_loop(0, n, scan_b, s0[LANES:2 * LANES])

    def outputs(i):
        rows = pl.ds(pl.multiple_of(i * c, c), c)
        q = q_ref[0, rows, :]
        k = k_ref[0, rows, :]
        qf = q.astype(F32)
        qd = jnp.concatenate([qf * tab_scr[0], qf * tab_scr[1]], axis=1).astype(BF16)
        inter = _dot(qd, st_scr[i])
        for hh in range(2):
            cols = slice(hh * V_DIM, (hh + 1) * V_DIM)
            scores = lax.dot_general(q * masks[hh], k, (((1,), (1,)), ((), ())),
                                     preferred_element_type=F32)
            o = _dot((scores * d_scr[hh]).astype(BF16), v_ref[0, rows, cols]) + inter[:, cols]
            o_ref[0, rows, cols] = o.astype(BF16)

    per_k1 = n // DFT_ROWS
    cc = cc_ref[...].astype(BF16)

    def outputs_and_dft(j, carry):
        for u in range(per_k1):
            outputs(j * per_k1 + u)
        _dft_second_stage(j, tw_ref, w2_ref, cc, y_ref, z_scr)
        return carry

    lax.fori_loop(0, DFT_ROWS, outputs_and_dft, 0, unroll=RET_UNROLL // per_k1)
    _dft_emit(z_scr, z_ref)


def _retention_and_dft(a_f, a_b, q, k, v, kc, vc, y, tw, w2, w_c):
    b, seq, _ = v.shape
    lc = kc.shape[1]
    c = RET_CHUNK
    n = seq // c
    r = DFT_ROWS
    _, _, n1, n2, w = y.shape
    assert n % r == 0 and n1 // r == HEADS // 2
    dec = pl.BlockSpec((2, 1, LANES), lambda i, p: (p, 0, 0))
    qk = lambda rows: pl.BlockSpec((1, rows, LANES), lambda i, p: (i, 0, p))
    vv = lambda rows: pl.BlockSpec((1, rows, 2 * V_DIM), lambda i, p: (i, 0, p))
    return pl.pallas_call(
        _retention_kernel,
        grid=(b, HEADS // 2),
        in_specs=[dec, dec, qk(seq), qk(seq), vv(seq), qk(lc), vv(lc),
                  pl.BlockSpec((r, 2, n2), lambda i, p: (p, 0, 0)),
                  _const_spec((2, n2, n2)),
                  _const_spec((2 * F_GROUP_DIM, F_GROUP_DIM)),
                  pl.BlockSpec((1, 2, r, n2, w), lambda i, p: (i, 0, p, 0, 0))],
        out_specs=[vv(seq), pl.BlockSpec((1, n2, r, w), lambda i, p: (i, 0, p, 0))],
        out_shape=[jax.ShapeDtypeStruct((b, seq, V_WIDTH), BF16),
                   jax.ShapeDtypeStruct((b, n2, n1, w), BF16)],
        scratch_shapes=[pltpu.VMEM((n, 2 * LANES, 2 * V_DIM), F32),
                        pltpu.VMEM((n, 2 * LANES, 2 * V_DIM), BF16),
                        pltpu.VMEM((2, c, c), F32),
                        pltpu.VMEM((2, c, LANES), F32),
                        pltpu.VMEM((F_GROUPS, (n2 + DFT_PAD) * r, F_GROUP_DIM), F32)],
        compiler_params=_params(("parallel", "parallel")),
        name="retention_dft",
    )(a_f, a_b, q, k, v, kc, vc, tw, w2, w_c, y)


def _dft_a_kernel(w_ref, x_ref, y_ref):
    _, n1, rows, w = x_ref.shape
    r = DFT_ROWS
    nh = n1 // 2 + 1
    for r0 in range(0, rows, r):
        x = x_ref[0, :, r0:r0 + r, :].reshape(n1 * r, w)
        y = _dot(w_ref[...], x).reshape(2, nh, r, w).astype(BF16)
        y_ref[0, :, 0:nh, r0:r0 + r, :] = y
        for k1 in range(1, n1 // 2):
            y_ref[0, 0, n1 - k1, r0:r0 + r, :] = y[0, k1]
            y_ref[0, 1, n1 - k1, r0:r0 + r, :] = -y[1, k1]


def _dft_second_stage(j, tw_ref, w2_ref, cc, y_ref, z_scr):
    n2 = y_ref.shape[3]
    w2c = w2_ref[0]
    w2s = w2_ref[1]
    tc = tw_ref[j, 0:1, :]
    ts = tw_ref[j, 1:2, :]
    ec = w2c * tc - w2s * ts
    es = w2s * tc + w2c * ts
    m = jnp.concatenate([jnp.concatenate([ec, -es], axis=1),
                         jnp.concatenate([es, ec], axis=1)], axis=0).astype(BF16)
    y = jnp.concatenate([y_ref[0, 0, j], y_ref[0, 1, j]], axis=0)
    zz = _dot(m, y)
    for g in range(F_GROUPS):
        cols = slice(g * F_GROUP_DIM, (g + 1) * F_GROUP_DIM)
        zcs = jnp.concatenate([zz[:n2, cols], zz[n2:, cols]], axis=1).astype(BF16)
        z_scr[g, pl.ds(pl.multiple_of(j * (n2 + DFT_PAD), DFT_PAD), n2), :] = _dot(zcs, cc)


def _dft_emit(z_scr, z_ref):
    n2 = z_ref.shape[1]
    pitch = n2 + DFT_PAD
    z = jnp.concatenate(
        [jnp.concatenate([z_scr[g, pl.ds(k2, DFT_ROWS, stride=pitch), :] for k2 in range(n2)], axis=0)
         for g in range(F_GROUPS)], axis=1)
    z_ref[0] = z.reshape(n2, DFT_ROWS, F_WIDTH).astype(BF16)


def _dft_tables(seq):
    n1 = DFT_N1
    n2 = seq // n1

    def cs(num, den):
        ang = 2.0 * np.pi * (num % den) / den
        return np.cos(ang), np.sin(ang)

    a = np.arange(n1)
    w_a = np.concatenate(cs(a[:n1 // 2 + 1, None] * a[None, :], n1), axis=0)
    m = np.arange(n2)
    tw = np.stack(cs(a[:, None] * m[None, :], seq), axis=1)
    w2 = np.stack(cs(m[:, None] * m[None, :], n2), axis=0)
    ch = np.arange(F_GROUP_DIM)
    cc, sc = cs(ch[:, None] * ch[None, :], F_GROUP_DIM)
    scale = 1.0 / np.sqrt(seq * F_GROUP_DIM)
    w_c = np.concatenate([cc, -sc], axis=0) * scale
    return [jnp.asarray(t, dtype=F32) for t in (w_a, tw, w2, w_c)]


def _dft_first_stage(f):
    b, seq, w = f.shape
    n1 = DFT_N1
    n2 = seq // n1
    w_a, tw, w2, w_c = _dft_tables(seq)
    r = DFT_ROWS
    spread = (jnp.arange(n1 * r)[None, :] // r == jnp.arange(n1)[:, None]).astype(F32)
    w_rep = jnp.dot(w_a, spread, precision=lax.Precision.HIGHEST)
    same_r = jnp.arange(w_a.shape[0] * r)[:, None] % r == jnp.arange(n1 * r)[None, :] % r
    w_a = jnp.where(same_r, jnp.repeat(w_rep, r, axis=0), 0.0).astype(BF16)
    y = pl.pallas_call(
        _dft_a_kernel,
        grid=(b, n2 // DFT_A_ROWS),
        in_specs=[_const_spec(w_a.shape),
                  pl.BlockSpec((1, n1, DFT_A_ROWS, w), lambda i, j: (i, 0, j, 0))],
        out_specs=pl.BlockSpec((1, 2, n1, DFT_A_ROWS, w), lambda i, j: (i, 0, 0, j, 0)),
        out_shape=jax.ShapeDtypeStruct((b, 2, n1, n2, w), BF16),
        compiler_params=_params(("parallel", "parallel")),
        name="dft_a",
    )(w_a, f.reshape(b, n1, n2, w))
    return y, tw, w2, w_c


def _merge_kernel(x_ref, mod_ref, nw_ref, n2w_ref, ro_ref, sg_ref, z_ref, wbg_ref, bbg_ref,
                  wro_ref, wfo_ref, wout_ref, *refs, n_cast):
    o_ref, h2_ref = refs[n_cast:n_cast + 2]
    _run_casts(refs[:n_cast] + refs[n_cast + 2:])
    for r0 in range(0, x_ref.shape[0], MERGE_SUB):
        rows = slice(r0, r0 + MERGE_SUB)
        x = x_ref[rows, :]
        h = _mod_norm(x, nw_ref[...], mod_ref[0, 0:1, :], mod_ref[0, 1:2, :])
        t = jnp.tanh(_dot(h.astype(BF16), wbg_ref[...]) + 0.5 * bbg_ref[...])
        gated = []
        for hd in range(HEADS):
            cols = slice(hd * V_DIM, (hd + 1) * V_DIM)
            o = ro_ref[rows, cols].astype(F32)
            oc = o - jnp.mean(o, axis=-1, keepdims=True)
            var = jnp.mean(oc * oc, axis=-1, keepdims=True)
            gated.append((oc * lax.rsqrt(var + EPS) * sg_ref[rows, cols].astype(F32)).astype(BF16))
        ret_d = _dot(jnp.concatenate(gated, axis=1), wro_ref[...])
        four_d = _dot(z_ref[rows, :], wfo_ref[...])
        m2 = (t[:, :D_MODEL] * ret_d + ret_d) + (t[:, D_MODEL:] * four_d + four_d)
        y2 = _dot(m2.astype(BF16), wout_ref[...])
        x1 = x + (0.5 * mod_ref[0, 2:3, :]) * y2
        o_ref[rows, :] = x1
        h2 = _mod_norm(x1, n2w_ref[...], mod_ref[0, 3:4, :], mod_ref[0, 4:5, :])
        h2_ref[rows, :] = h2.astype(BF16)


def _merge(x2, mod3, norm_w, norm2_w, ro, sg, z, w_bg, b_bg, w_ro, w_fo, w_out, seq, to_cast):
    t, d = x2.shape
    tl = MERGE_TILE
    tpb = seq // tl
    tok = lambda w: pl.BlockSpec((tl, w), lambda i: (i, 0))
    cast_specs, cast_shapes = _cast_jobs(to_cast, t // tl)
    out = pl.pallas_call(
        functools.partial(_merge_kernel, n_cast=len(to_cast)),
        grid=(t // tl,),
        in_specs=[tok(d),
                  pl.BlockSpec((1, N_MOD, d), lambda i: (i // tpb, 0, 0)),
                  _const_spec((1, d)), _const_spec((1, d)),
                  tok(V_WIDTH), tok(V_WIDTH), tok(F_WIDTH),
                  _const_spec(w_bg.shape), _const_spec((1, 2 * d)),
                  _const_spec(w_ro.shape), _const_spec(w_fo.shape),
                  _const_spec(w_out.shape)] + cast_specs,
        out_specs=[tok(d), tok(d)] + cast_specs,
        out_shape=[jax.ShapeDtypeStruct((t, d), F32),
                   jax.ShapeDtypeStruct((t, d), BF16)] + cast_shapes,
        compiler_params=_params(("arbitrary",)),
        name="merge",
    )(x2, mod3, norm_w, norm2_w, ro, sg, z, w_bg, b_bg, w_ro, w_fo, w_out, *to_cast)
    return out[:2], out[2:]


HALO = 8
FFN_LEAD = 16
FFN_SUB = 256


def _two_gelu_tanh(x):
    c1 = np.sqrt(2.0 / np.pi)
    return x + x * jnp.tanh(x * (c1 + (c1 * 0.044715) * (x * x)))


def _ffn_kernel(xm_ref, hm_ref, xp_ref, xn_ref, mod_ref, nw_ref, wu_ref, cw_ref, cb_ref,
                wd_ref, fnw_ref, o_ref, h_scr, u_scr, act_scr, y_scr, *, tiles_per_seq):
    tl = xm_ref.shape[0]
    sub = FFN_SUB
    half = sub // 2
    nc = FFN_CHUNK
    i = pl.program_id(0)
    keep_prev = ((i % tiles_per_seq) != 0).astype(F32)
    keep_next = ((i % tiles_per_seq) != tiles_per_seq - 1).astype(F32)

    def pre(x):
        return _mod_norm(x, nw_ref[...], mod_ref[0, 3:4, :], mod_ref[0, 4:5, :])

    pad = jnp.zeros((FFN_LEAD - HALO, xm_ref.shape[1]), F32)
    h_scr[0:FFN_LEAD] = jnp.concatenate([pad, pre(xp_ref[...]) * keep_prev], axis=0).astype(BF16)
    h_scr[FFN_LEAD:FFN_LEAD + tl] = hm_ref[...]
    h_scr[FFN_LEAD + tl:] = jnp.concatenate([pre(xn_ref[...]) * keep_next, pad], axis=0).astype(BF16)

    def project(j):
        hb = h_scr[...]
        for part in range(2):
            lo = part * FFN_DIM + j * nc
            u = _dot(hb, wu_ref[:, lo:lo + nc])
            for s in range(nc // LANES):
                u_scr[j % 2, part, s] = u[:, s * LANES:(s + 1) * LANES]

    def conv(j, part, s, blk, scale):
        lo = part * FFN_DIM + j * nc + s * LANES
        w = cw_ref[:, lo:lo + LANES] * scale
        bias = cb_ref[:, lo:lo + LANES] * scale
        first = FFN_LEAD + blk * sub
        rows = lambda start: u_scr[j % 2, part, s, pl.ds(start, half, stride=2), :]
        before, even, odd, after = rows(first - 1), rows(first), rows(first + 1), rows(first + 2)
        return (before * w[0:1] + even * w[1:2] + odd * w[2:3] + bias,
                even * w[0:1] + odd * w[1:2] + after * w[2:3] + bias)

    def activate(j):
        for s in range(nc // LANES):
            cols = slice(j * nc + s * LANES, j * nc + (s + 1) * LANES)
            for blk in range(tl // sub):
                gate = conv(j, 0, s, blk, 1.0)
                val = conv(j, 1, s, blk, 0.5)
                for par in range(2):
                    r0 = blk * sub + par * half
                    act_scr[r0:r0 + half, cols] = (_two_gelu_tanh(gate[par]) * val[par]).astype(BF16)

    project(0)
    for j in range(N_FFN_CHUNKS):
        if j + 1 < N_FFN_CHUNKS:
            project(j + 1)
        activate(j)
    for blk in range(tl // sub):
        r0 = blk * sub
        y = _dot(act_scr[r0:r0 + sub, :], wd_ref[...])
        for s in range(y_scr.shape[0]):
            for par in range(2):
                y_scr[s, pl.ds(par, half, stride=2), :] = (
                    y[par * half:(par + 1) * half, s * LANES:(s + 1) * LANES])
        y = jnp.concatenate([y_scr[s] for s in range(y_scr.shape[0])], axis=1)
        x2 = xm_ref[r0:r0 + sub, :] + mod_ref[0, 5:6, :] * y
        o_ref[r0:r0 + sub, :] = _rms_norm(x2, fnw_ref[...])


def _ffn(x1, h2, mod3, norm_w, w_u, cw, cb, w_d, fnorm_w, seq):
    t, d = x1.shape
    tl = FFN_TILE
    tpb = seq // tl
    hb = tl // HALO
    last = t // HALO - 1
    rows = FFN_LEAD + tl + FFN_LEAD
    return pl.pallas_call(
        functools.partial(_ffn_kernel, tiles_per_seq=tpb),
        grid=(t // tl,),
        in_specs=[pl.BlockSpec((tl, d), lambda i: (i, 0)),
                  pl.BlockSpec((tl, d), lambda i: (i, 0)),
                  pl.BlockSpec((HALO, d), lambda i: (jnp.maximum(i * hb - 1, 0), 0)),
                  pl.BlockSpec((HALO, d), lambda i: (jnp.minimum((i + 1) * hb, last), 0)),
                  pl.BlockSpec((1, N_MOD, d), lambda i: (i // tpb, 0, 0)),
                  _const_spec((1, d)),
                  _const_spec(w_u.shape), _const_spec(cw.shape), _const_spec(cb.shape),
                  _const_spec(w_d.shape), _const_spec((1, d))],
        out_specs=pl.BlockSpec((tl, d), lambda i: (i, 0)),
        out_shape=jax.ShapeDtypeStruct((t, d), F32),
        scratch_shapes=[pltpu.VMEM((rows, d), BF16),
                        pltpu.VMEM((2, 2, FFN_CHUNK // LANES, rows, LANES), F32),
                        pltpu.VMEM((tl, FFN_DIM), BF16),
                        pltpu.VMEM((d // LANES, FFN_SUB, LANES), F32)],
        compiler_params=_params(("parallel",)),
        name="ffn",
    )(x1, h2, x1, x1, mod3, norm_w, w_u, cw, cb, w_d, fnorm_w)


def _rope_tables(seq):
    n_freq = QK_DIM // 4
    lane = jnp.arange(LANES)
    inv = ROPE_BASE ** (-(lane % n_freq).astype(F32) / n_freq)
    sign = jnp.where(lane < LANES // 2, -1.0, 1.0).astype(F32)

    def table(n):
        ang = jnp.arange(n, dtype=F32)[:, None] * inv[None, :]
        return jnp.stack([jnp.cos(ang), jnp.sin(ang) * sign])

    return table(seq // GRID_W), table(GRID_W)


def kernel(x, c, ctx, c_ctx, w_mod, b_mod, norm1_w, w_in, ret_decay_f, ret_decay_b,
           w_ret_out, w_four_out, w_branch_gate, b_branch_gate, w_out, norm2_w,
           w_up, conv_w, conv_b, w_down, final_norm_w):
    assert w_mod.shape[0] == 1, "single-layer block"
    b, seq, d = x.shape
    t = b * seq
    assert b == 2 and seq % RET_CHUNK == 0
    assert seq % FFN_TILE == 0 and seq % MERGE_TILE == 0 and seq % PROJ_TILE == 0

    c8 = jnp.concatenate([c, c_ctx[None, :], jnp.zeros((8 - b - 1, d), F32)], axis=0)
    mod, w_in_b, w_qk = _modulation(c8, w_mod[0], b_mod[0], w_in[0])
    mod3 = mod.reshape(8, N_MOD, d)
    n1w = norm1_w[0].reshape(1, d)

    kc, vc = _ctx_proj(ctx, mod3, n1w, w_qk, w_in_b)
    rtab, ctab = _rope_tables(seq)
    x2 = x.reshape(t, d)
    (q, k, v, sg, f), (w_bg, w_ro, w_fo, w_o) = _proj(
        x2, mod3, n1w, w_qk, w_in_b, rtab, ctab, seq,
        [w_branch_gate[0], w_ret_out[0], w_four_out[0], w_out[0]], (0.5, 1.0, 1.0, 1.0))

    a_f = jnp.broadcast_to(ret_decay_f[0][:, None, None], (HEADS, 1, LANES))
    a_b = jnp.broadcast_to(ret_decay_b[0][:, None, None], (HEADS, 1, LANES))
    y, tw, w2, w_c = _dft_first_stage(f.reshape(b, seq, F_WIDTH))
    ro, z = _retention_and_dft(a_f, a_b, q.reshape(b, seq, QK_WIDTH), k.reshape(b, seq, QK_WIDTH),
                               v.reshape(b, seq, V_WIDTH), kc, vc, y, tw, w2, w_c)

    n2w = norm2_w[0].reshape(1, d)
    (x1, h2), (w_u, w_d) = _merge(
        x2, mod3, n1w, n2w, ro.reshape(t, V_WIDTH), sg, z.reshape(t, F_WIDTH),
        w_bg, b_branch_gate[0].reshape(1, 2 * d), w_ro, w_fo, w_o, seq, [w_up[0], w_down[0]])

    out = _ffn(x1, h2, mod3, n2w, w_u, conv_w[0], conv_b[0].reshape(1, 2 * FFN_DIM), w_d,
               final_norm_w.reshape(1, d), seq)
    return out.reshape(b, seq, d)
```

```python
import functools

import numpy as np
import jax
import jax.numpy as jnp
from jax import lax
from jax.experimental import pallas as pl
from jax.experimental.pallas import tpu as pltpu

F32 = jnp.float32
BF16 = jnp.bfloat16

D_MODEL = 1024
GRID_W = 64
HEADS = 8
QK_DIM = 64
V_DIM = 128
QK_WIDTH = HEADS * QK_DIM
V_WIDTH = HEADS * V_DIM
ROPE_BASE = 10000.0
F_GROUPS = 4
F_GROUP_DIM = 128
F_WIDTH = F_GROUPS * F_GROUP_DIM
K_OFF = QK_WIDTH
V_OFF = K_OFF + QK_WIDTH
G_OFF = V_OFF + V_WIDTH
F_OFF = G_OFF + V_WIDTH
IN_COLS = F_OFF + F_WIDTH
FFN_DIM = 2816
N_MOD = 6
EPS = 1e-6

LANES = 128
RET_CHUNK = 256
RET_UNROLL = 16
FFN_CHUNK = 256
N_FFN_CHUNKS = FFN_DIM // FFN_CHUNK
FFN_TILE = 512
MERGE_TILE = 1024
MERGE_SUB = 512
PROJ_TILE = 1024
PROJ_SUB = 256
MOD_STEPS = 2
DFT_N1 = 64
BF16_ROWS = 16
DFT_ROWS = BF16_ROWS
DFT_A_ROWS = 2 * DFT_ROWS
VMEM_LIMIT = 56 * 1024 * 1024


def _params(sem):
    return pltpu.CompilerParams(dimension_semantics=sem, vmem_limit_bytes=VMEM_LIMIT)


def _dot(a, b):
    return jnp.dot(a, b, preferred_element_type=F32)


def _rms_norm(x, w):
    return x * lax.rsqrt(jnp.mean(x * x, axis=-1, keepdims=True) + EPS) * w


def _mod_norm(x, w, shift, scale):
    return x * lax.rsqrt(jnp.mean(x * x, axis=-1, keepdims=True) + EPS) * (w * (1.0 + scale)) + shift


def _const_spec(shape):
    zeros = (0,) * len(shape)
    return pl.BlockSpec(shape, lambda *_: zeros, pipeline_mode=pl.Buffered(1))


def _cast_jobs(weights, steps):
    specs, shapes = [], []
    for w in weights:
        n_rows = w.shape[0]
        rows = -(-n_rows // steps)
        while rows % BF16_ROWS or n_rows % rows:
            rows += 1
        last = n_rows // rows - 1
        specs.append(pl.BlockSpec((rows, w.shape[1]), lambda i, last=last: (jnp.minimum(i, last), 0)))
        shapes.append(jax.ShapeDtypeStruct(w.shape, BF16))
    return specs, shapes


def _run_casts(refs, scales=None):
    n = len(refs) // 2
    for j, (src, dst) in enumerate(zip(refs[:n], refs[n:])):
        scale = 1.0 if scales is None else scales[j]
        dst[...] = (src[...] if scale == 1.0 else src[...] * scale).astype(BF16)


def _mod_kernel(c_ref, w_ref, b_ref, win_ref, o_ref, winb_ref, wqk_ref):
    c = c_ref[...]
    s = c * jax.nn.sigmoid(c)
    o_ref[...] = _dot(s.astype(BF16), w_ref[...].astype(BF16)) + b_ref[...]
    wb = win_ref[...].astype(BF16)
    winb_ref[...] = wb
    n_qk = 2 * QK_WIDTH
    src = lax.broadcasted_iota(jnp.int32, (n_qk, n_qk), 0)
    dst = lax.broadcasted_iota(jnp.int32, (n_qk, n_qk), 1)
    select = (src == _qk_source_column(dst)).astype(BF16)
    wqk_ref[...] = _dot(wb[:, :n_qk], select).astype(BF16)


def _qk_source_column(col):
    half = QK_DIM // 2
    lane = col % LANES
    head = 2 * ((col % QK_WIDTH) // LANES) + (lane // half) % 2
    return (col // QK_WIDTH) * QK_WIDTH + head * QK_DIM + (lane // (2 * half)) * half + lane % half


def _modulation(c8, w_mod, b_mod, w_in):
    n = w_mod.shape[1]
    steps = MOD_STEPS
    tn = n // steps
    d, n_in = w_in.shape
    rows = d // steps
    return pl.pallas_call(
        _mod_kernel,
        grid=(steps,),
        in_specs=[_const_spec((8, D_MODEL)),
                  pl.BlockSpec((D_MODEL, tn), lambda j: (0, j)),
                  pl.BlockSpec((1, tn), lambda j: (0, j)),
                  pl.BlockSpec((rows, n_in), lambda j: (j, 0))],
        out_specs=[pl.BlockSpec((8, tn), lambda j: (0, j)),
                   pl.BlockSpec((rows, n_in), lambda j: (j, 0)),
                   pl.BlockSpec((rows, 2 * QK_WIDTH), lambda j: (j, 0))],
        out_shape=[jax.ShapeDtypeStruct((8, n), F32),
                   jax.ShapeDtypeStruct((d, n_in), BF16),
                   jax.ShapeDtypeStruct((d, 2 * QK_WIDTH), BF16)],
        compiler_params=_params(("parallel",)),
        name="mod",
    )(c8, w_mod, b_mod.reshape(1, n), w_in)


def _ctx_kernel(x_ref, mod_ref, nw_ref, wk_ref, wv_ref, k_ref, v_ref):
    x = x_ref[0]
    h = _mod_norm(x, nw_ref[...], mod_ref[0, 0:1, :], mod_ref[0, 1:2, :])
    hb = h.astype(BF16)
    k_ref[0] = _dot(hb, wk_ref[...]).astype(BF16)
    v_ref[0] = _dot(hb, wv_ref[...]).astype(BF16)


def _ctx_proj(ctx, mod3, norm_w, w_qk, w_in):
    b, lc, d = ctx.shape
    assert V_OFF % V_WIDTH == 0
    return pl.pallas_call(
        _ctx_kernel,
        grid=(b,),
        in_specs=[pl.BlockSpec((1, lc, d), lambda i: (i, 0, 0)),
                  pl.BlockSpec((1, N_MOD, d), lambda i: (2, 0, 0)),
                  _const_spec((1, d)),
                  pl.BlockSpec((d, QK_WIDTH), lambda i: (0, 1), pipeline_mode=pl.Buffered(1)),
                  pl.BlockSpec((d, V_WIDTH), lambda i: (0, V_OFF // V_WIDTH),
                               pipeline_mode=pl.Buffered(1))],
        out_specs=[pl.BlockSpec((1, lc, QK_WIDTH), lambda i: (i, 0, 0)),
                   pl.BlockSpec((1, lc, V_WIDTH), lambda i: (i, 0, 0))],
        out_shape=[jax.ShapeDtypeStruct((b, lc, QK_WIDTH), BF16),
                   jax.ShapeDtypeStruct((b, lc, V_WIDTH), BF16)],
        compiler_params=_params(("parallel",)),
        name="ctx_proj",
    )(ctx, mod3, norm_w, w_qk, w_in)


def _proj_kernel(x_ref, mod_ref, nw_ref, wqk_ref, w_ref, rtab_ref, ctab_ref, *refs, cast_scales):
    n_cast = len(cast_scales)
    q_ref, k_ref, v_ref, sg_ref, f_ref = refs[n_cast:n_cast + 5]
    _run_casts(refs[:n_cast] + refs[n_cast + 5:], cast_scales)
    lane = lax.broadcasted_iota(jnp.int32, (1, LANES), 1)
    by_row = (lane % (QK_DIM // 2)) < QK_DIM // 4

    def rope(t, trig, scale, out_ref, rows):
        for j in range(QK_WIDTH // LANES):
            tj = t[:, j * LANES:(j + 1) * LANES]
            r = tj * trig[0] + pltpu.roll(tj, LANES // 2, 1) * trig[1]
            out_ref[rows, j * LANES:(j + 1) * LANES] = (r * scale).astype(BF16)

    for r0 in range(0, x_ref.shape[0], PROJ_SUB):
        rows = slice(r0, r0 + PROJ_SUB)
        trig = []
        for cs in range(2):
            trig.append(jnp.concatenate(
                [jnp.where(by_row, rtab_ref[cs, g:g + 1, :], ctab_ref[cs])
                 for g in range(r0 // GRID_W, (r0 + PROJ_SUB) // GRID_W)], axis=0))
        x = x_ref[rows, :]
        h = _mod_norm(x, nw_ref[...], mod_ref[0, 0:1, :], mod_ref[0, 1:2, :])
        hb = h.astype(BF16)
        rope(_dot(hb, wqk_ref[:, :QK_WIDTH]), trig, QK_DIM ** -0.5, q_ref, rows)
        rope(_dot(hb, wqk_ref[:, QK_WIDTH:]), trig, 1.0, k_ref, rows)
        v_ref[rows, :] = _dot(hb, w_ref[:, V_OFF:G_OFF]).astype(BF16)
        g = _dot(hb, w_ref[:, G_OFF:F_OFF])
        hg = 0.5 * g
        sg_ref[rows, :] = (hg * jnp.tanh(hg) + hg).astype(BF16)
        f_ref[rows, :] = _dot(hb, w_ref[:, F_OFF:IN_COLS]).astype(BF16)


def _proj(x2, mod3, norm_w, w_qk, w_in, rtab, ctab, seq, to_cast, cast_scales):
    t, d = x2.shape
    tl = PROJ_TILE
    tpb = seq // tl
    tok = lambda w: pl.BlockSpec((tl, w), lambda i: (i, 0))
    cast_specs, cast_shapes = _cast_jobs(to_cast, t // tl)
    out = pl.pallas_call(
        functools.partial(_proj_kernel, cast_scales=tuple(cast_scales)),
        grid=(t // tl,),
        in_specs=[tok(d),
                  pl.BlockSpec((1, N_MOD, d), lambda i: (i // tpb, 0, 0)),
                  _const_spec((1, d)),
                  _const_spec(w_qk.shape), _const_spec(w_in.shape),
                  pl.BlockSpec((2, tl // GRID_W, LANES), lambda i: (0, i % tpb, 0)),
                  _const_spec(ctab.shape)] + cast_specs,
        out_specs=[tok(QK_WIDTH), tok(QK_WIDTH), tok(V_WIDTH), tok(V_WIDTH),
                   tok(F_WIDTH)] + cast_specs,
        out_shape=[jax.ShapeDtypeStruct((t, QK_WIDTH), BF16),
                   jax.ShapeDtypeStruct((t, QK_WIDTH), BF16),
                   jax.ShapeDtypeStruct((t, V_WIDTH), BF16),
                   jax.ShapeDtypeStruct((t, V_WIDTH), BF16),
                   jax.ShapeDtypeStruct((t, F_WIDTH), BF16)] + cast_shapes,
        compiler_params=_params(("arbitrary",)),
        name="proj",
    )(x2, mod3, norm_w, w_qk, w_in, rtab, ctab, *to_cast)
    return out[:5], out[5:]


def _retention_kernel(af_ref, ab_ref, q_ref, k_ref, v_ref, kc_ref, vc_ref,
                      tw_ref, w2_ref, cc_ref, y_ref,
                      o_ref, z_ref, ds_scr, st_scr, d_scr, tab_scr, z_scr):
    c = RET_CHUNK
    seq = q_ref.shape[1]
    lc = kc_ref.shape[1]
    n = seq // c
    lgf = [-jnp.exp(af_ref[hh]) for hh in range(2)]
    lgb = [-jnp.exp(ab_ref[hh]) for hh in range(2)]
    half = QK_DIM // 2

    lane_head = (lax.broadcasted_iota(jnp.int32, (1, LANES), 1) // half) % 2
    masks = [(lane_head == hh).astype(BF16) for hh in range(2)]
    lgf_lane = jnp.where(lane_head == 0, lgf[0], lgf[1])
    lgb_lane = jnp.where(lane_head == 0, lgb[0], lgb[1])
    row_head = (lax.broadcasted_iota(jnp.int32, (LANES, 1), 0) // half) % 2
    lgf_row = jnp.where(row_head == 0, lgf[0][:, 0:1], lgf[1][:, 0:1])
    lgb_row = jnp.where(row_head == 0, lgb[0][:, 0:1], lgb[1][:, 0:1])

    pos = lax.broadcasted_iota(jnp.int32, (c, LANES), 0).astype(F32)
    tab_scr[0] = jnp.exp(lgf_lane * (pos + 1.0))
    tab_scr[1] = jnp.exp(lgb_lane * (c - pos))
    diff = (lax.broadcasted_iota(jnp.int32, (c, c), 0)
            - lax.broadcasted_iota(jnp.int32, (c, c), 1)).astype(F32)
    for hh in range(2):
        d_scr[hh] = (jnp.where(diff >= 0, jnp.exp(lgf[hh][:, 0:1] * jnp.maximum(diff, 0.0)), 0.0)
                     + jnp.where(diff <= 0, jnp.exp(lgb[hh][:, 0:1] * jnp.maximum(-diff, 0.0)), 0.0))

    def k_decays(tokens):
        t = lax.broadcasted_iota(jnp.int32, (1, tokens), 1).astype(F32)
        return jnp.exp(lgf_row * (tokens - 1.0 - t)), jnp.exp(lgb_row * t)

    def state_increment(k_rows, v_rows, decays):
        kt = jnp.transpose(k_rows.astype(F32))
        lhs = jnp.concatenate([kt * decays[0], kt * decays[1]], axis=0).astype(BF16)
        return _dot(lhs, v_rows)

    kdec = k_decays(c)

    def incr(i, carry):
        rows = pl.ds(pl.multiple_of(i * c, c), c)
        ds_scr[i] = state_increment(k_ref[0, rows, :], v_ref[0, rows, :], kdec)
        return carry

    lax.fori_loop(0, n, incr, 0, unroll=RET_UNROLL)

    s0 = state_increment(kc_ref[0], vc_ref[0], k_decays(lc))
    col_head = lax.broadcasted_iota(jnp.int32, (1, 2 * V_DIM), 1) // V_DIM
    own = (row_head == col_head).astype(F32)
    decay_f = jnp.exp(lgf_row * c) * own
    decay_b = jnp.exp(lgb_row * c) * own

    def scan_f(i, s):
        st_scr[i, 0:LANES, :] = (s * own).astype(BF16)
        return decay_f * s + ds_scr[i, 0:LANES, :]

    lax.fori_loop(0, n, scan_f, s0[0:LANES])

    def scan_b(t, s):
        i = n - 1 - t
        st_scr[i, LANES:2 * LANES, :] = (s * own).astype(BF16)
        return decay_b * s + ds_scr[i, LANES:2 * LANES, :]

    lax.fori_loop(0, n, scan_b, s0[LANES:2 * LANES])

    def outputs(i):
        rows = pl.ds(pl.multiple_of(i * c, c), c)
        q = q_ref[0, rows, :]
        k = k_ref[0, rows, :]
        qf = q.astype(F32)
        qd = jnp.concatenate([qf * tab_scr[0], qf * tab_scr[1]], axis=1).astype(BF16)
        inter = _dot(qd, st_scr[i])
        for hh in range(2):
            cols = slice(hh * V_DIM, (hh + 1) * V_DIM)
            scores = lax.dot_general(q * masks[hh], k, (((1,), (1,)), ((), ())),
                                     preferred_element_type=F32)
            o = _dot((scores * d_scr[hh]).astype(BF16), v_ref[0, rows, cols]) + inter[:, cols]
            o_ref[0, rows, cols] = o.astype(BF16)

    per_k1 = n // DFT_ROWS
    cc = cc_ref[...].astype(BF16)

    def outputs_and_dft(j, carry):
        for u in range(per_k1):
            outputs(j * per_k1 + u)
        _dft_second_stage(j, tw_ref, w2_ref, cc, y_ref, z_scr)
        return carry

    lax.fori_loop(0, DFT_ROWS, outputs_and_dft, 0, unroll=RET_UNROLL // per_k1)
    _dft_emit(z_scr, z_ref)


def _retention_and_dft(a_f, a_b, q, k, v, kc, vc, y, tw, w2, w_c):
    b, seq, _ = v.shape
    lc = kc.shape[1]
    c = RET_CHUNK
    n = seq // c
    r = DFT_ROWS
    _, _, n1, n2, w = y.shape
    assert n % r == 0 and n1 // r == HEADS // 2
    dec = pl.BlockSpec((2, 1, LANES), lambda i, p: (p, 0, 0))
    qk = lambda rows: pl.BlockSpec((1, rows, LANES), lambda i, p: (i, 0, p))
    vv = lambda rows: pl.BlockSpec((1, rows, 2 * V_DIM), lambda i, p: (i, 0, p))
    return pl.pallas_call(
        _retention_kernel,
        grid=(b, HEADS // 2),
        in_specs=[dec, dec, qk(seq), qk(seq), vv(seq), qk(lc), vv(lc),
                  pl.BlockSpec((r, 2, n2), lambda i, p: (p, 0, 0)),
                  _const_spec((2, n2, n2)),
                  _const_spec((2 * F_GROUP_DIM, F_GROUP_DIM)),
                  pl.BlockSpec((1, 2, r, n2, w), lambda i, p: (i, 0, p, 0, 0))],
        out_specs=[vv(seq), pl.BlockSpec((1, n2, r, w), lambda i, p: (i, 0, p, 0))],
        out_shape=[jax.ShapeDtypeStruct((b, seq, V_WIDTH), BF16),
                   jax.ShapeDtypeStruct((b, n2, n1, w), BF16)],
        scratch_shapes=[pltpu.VMEM((n, 2 * LANES, 2 * V_DIM), F32),
                        pltpu.VMEM((n, 2 * LANES, 2 * V_DIM), BF16),
                        pltpu.VMEM((2, c, c), F32),
                        pltpu.VMEM((2, c, LANES), F32),
                        pltpu.VMEM((F_GROUPS, n2 * r, F_GROUP_DIM), F32)],
        compiler_params=_params(("parallel", "parallel")),
        name="retention_dft",
    )(a_f, a_b, q, k, v, kc, vc, tw, w2, w_c, y)


def _dft_a_kernel(w_ref, x_ref, y_ref):
    _, n1, rows, w = x_ref.shape
    r = DFT_ROWS
    nh = n1 // 2 + 1
    for r0 in range(0, rows, r):
        x = x_ref[0, :, r0:r0 + r, :].reshape(n1 * r, w)
        y = _dot(w_ref[...], x).reshape(2, nh, r, w).astype(BF16)
        y_ref[0, :, 0:nh, r0:r0 + r, :] = y
        for k1 in range(1, n1 // 2):
            y_ref[0, 0, n1 - k1, r0:r0 + r, :] = y[0, k1]
            y_ref[0, 1, n1 - k1, r0:r0 + r, :] = -y[1, k1]


def _dft_second_stage(j, tw_ref, w2_ref, cc, y_ref, z_scr):
    n2 = y_ref.shape[3]
    w2c = w2_ref[0]
    w2s = w2_ref[1]
    tc = tw_ref[j, 0:1, :]
    ts = tw_ref[j, 1:2, :]
    ec = w2c * tc - w2s * ts
    es = w2s * tc + w2c * ts
    m = jnp.concatenate([jnp.concatenate([ec, -es], axis=1),
                         jnp.concatenate([es, ec], axis=1)], axis=0).astype(BF16)
    y = jnp.concatenate([y_ref[0, 0, j], y_ref[0, 1, j]], axis=0)
    zz = _dot(m, y)
    for g in range(F_GROUPS):
        cols = slice(g * F_GROUP_DIM, (g + 1) * F_GROUP_DIM)
        zcs = jnp.concatenate([zz[:n2, cols], zz[n2:, cols]], axis=1).astype(BF16)
        z_scr[g, pl.ds(j, n2, stride=DFT_ROWS), :] = _dot(zcs, cc)


def _dft_emit(z_scr, z_ref):
    n2 = z_ref.shape[1]
    z = jnp.concatenate([z_scr[g] for g in range(F_GROUPS)], axis=1)
    z_ref[0] = z.reshape(n2, DFT_ROWS, F_WIDTH).astype(BF16)


def _dft_tables(seq):
    n1 = DFT_N1
    n2 = seq // n1

    def cs(num, den):
        ang = 2.0 * np.pi * (num % den) / den
        return np.cos(ang), np.sin(ang)

    a = np.arange(n1)
    w_a = np.concatenate(cs(a[:n1 // 2 + 1, None] * a[None, :], n1), axis=0)
    m = np.arange(n2)
    tw = np.stack(cs(a[:, None] * m[None, :], seq), axis=1)
    w2 = np.stack(cs(m[:, None] * m[None, :], n2), axis=0)
    ch = np.arange(F_GROUP_DIM)
    cc, sc = cs(ch[:, None] * ch[None, :], F_GROUP_DIM)
    scale = 1.0 / np.sqrt(seq * F_GROUP_DIM)
    w_c = np.concatenate([cc, -sc], axis=0) * scale
    return [jnp.asarray(t, dtype=F32) for t in (w_a, tw, w2, w_c)]


def _dft_first_stage(f):
    b, seq, w = f.shape
    n1 = DFT_N1
    n2 = seq // n1
    w_a, tw, w2, w_c = _dft_tables(seq)
    r = DFT_ROWS
    spread = (jnp.arange(n1 * r)[None, :] // r == jnp.arange(n1)[:, None]).astype(F32)
    w_rep = jnp.dot(w_a, spread, precision=lax.Precision.HIGHEST)
    same_r = jnp.arange(w_a.shape[0] * r)[:, None] % r == jnp.arange(n1 * r)[None, :] % r
    w_a = jnp.where(same_r, jnp.repeat(w_rep, r, axis=0), 0.0).astype(BF16)
    y = pl.pallas_call(
        _dft_a_kernel,
        grid=(b, n2 // DFT_A_ROWS),
        in_specs=[_const_spec(w_a.shape),
                  pl.BlockSpec((1, n1, DFT_A_ROWS, w), lambda i, j: (i, 0, j, 0))],
        out_specs=pl.BlockSpec((1, 2, n1, DFT_A_ROWS, w), lambda i, j: (i, 0, 0, j, 0)),
        out_shape=jax.ShapeDtypeStruct((b, 2, n1, n2, w), BF16),
        compiler_params=_params(("parallel", "parallel")),
        name="dft_a",
    )(w_a, f.reshape(b, n1, n2, w))
    return y, tw, w2, w_c


def _merge_kernel(x_ref, mod_ref, nw_ref, n2w_ref, ro_ref, sg_ref, z_ref, wbg_ref, bbg_ref,
                  wro_ref, wfo_ref, wout_ref, *refs, n_cast):
    o_ref, h2_ref = refs[n_cast:n_cast + 2]
    _run_casts(refs[:n_cast] + refs[n_cast + 2:])
    for r0 in range(0, x_ref.shape[0], MERGE_SUB):
        rows = slice(r0, r0 + MERGE_SUB)
        x = x_ref[rows, :]
        h = _mod_norm(x, nw_ref[...], mod_ref[0, 0:1, :], mod_ref[0, 1:2, :])
        t = jnp.tanh(_dot(h.astype(BF16), wbg_ref[...]) + 0.5 * bbg_ref[...])
        gated = []
        for hd in range(HEADS):
            cols = slice(hd * V_DIM, (hd + 1) * V_DIM)
            o = ro_ref[rows, cols].astype(F32)
            oc = o - jnp.mean(o, axis=-1, keepdims=True)
            var = jnp.mean(oc * oc, axis=-1, keepdims=True)
            gated.append((oc * lax.rsqrt(var + EPS) * sg_ref[rows, cols].astype(F32)).astype(BF16))
        ret_d = _dot(jnp.concatenate(gated, axis=1), wro_ref[...])
        four_d = _dot(z_ref[rows, :], wfo_ref[...])
        m2 = (t[:, :D_MODEL] * ret_d + ret_d) + (t[:, D_MODEL:] * four_d + four_d)
        y2 = _dot(m2.astype(BF16), wout_ref[...])
        x1 = x + (0.5 * mod_ref[0, 2:3, :]) * y2
        o_ref[rows, :] = x1
        h2 = _mod_norm(x1, n2w_ref[...], mod_ref[0, 3:4, :], mod_ref[0, 4:5, :])
        h2_ref[rows, :] = h2.astype(BF16)


def _merge(x2, mod3, norm_w, norm2_w, ro, sg, z, w_bg, b_bg, w_ro, w_fo, w_out, seq, to_cast):
    t, d = x2.shape
    tl = MERGE_TILE
    tpb = seq // tl
    tok = lambda w: pl.BlockSpec((tl, w), lambda i: (i, 0))
    cast_specs, cast_shapes = _cast_jobs(to_cast, t // tl)
    out = pl.pallas_call(
        functools.partial(_merge_kernel, n_cast=len(to_cast)),
        grid=(t // tl,),
        in_specs=[tok(d),
                  pl.BlockSpec((1, N_MOD, d), lambda i: (i // tpb, 0, 0)),
                  _const_spec((1, d)), _const_spec((1, d)),
                  tok(V_WIDTH), tok(V_WIDTH), tok(F_WIDTH),
                  _const_spec(w_bg.shape), _const_spec((1, 2 * d)),
                  _const_spec(w_ro.shape), _const_spec(w_fo.shape),
                  _const_spec(w_out.shape)] + cast_specs,
        out_specs=[tok(d), tok(d)] + cast_specs,
        out_shape=[jax.ShapeDtypeStruct((t, d), F32),
                   jax.ShapeDtypeStruct((t, d), BF16)] + cast_shapes,
        compiler_params=_params(("arbitrary",)),
        name="merge",
    )(x2, mod3, norm_w, norm2_w, ro, sg, z, w_bg, b_bg, w_ro, w_fo, w_out, *to_cast)
    return out[:2], out[2:]


HALO = 8
FFN_LEAD = 16
FFN_SUB = 256


def _two_gelu_tanh(x):
    c1 = np.sqrt(2.0 / np.pi)
    return x + x * jnp.tanh(x * (c1 + (c1 * 0.044715) * (x * x)))


def _ffn_kernel(xm_ref, hm_ref, xp_ref, xn_ref, mod_ref, nw_ref, wu_ref, cw_ref, cb_ref,
                wd_ref, fnw_ref, o_ref, h_scr, u_scr, act_scr, y_scr, *, tiles_per_seq):
    tl = xm_ref.shape[0]
    sub = FFN_SUB
    half = sub // 2
    nc = FFN_CHUNK
    i = pl.program_id(0)
    keep_prev = ((i % tiles_per_seq) != 0).astype(F32)
    keep_next = ((i % tiles_per_seq) != tiles_per_seq - 1).astype(F32)

    def pre(x):
        return _mod_norm(x, nw_ref[...], mod_ref[0, 3:4, :], mod_ref[0, 4:5, :])

    pad = jnp.zeros((FFN_LEAD - HALO, xm_ref.shape[1]), F32)
    h_scr[0:FFN_LEAD] = jnp.concatenate([pad, pre(xp_ref[...]) * keep_prev], axis=0).astype(BF16)
    h_scr[FFN_LEAD:FFN_LEAD + tl] = hm_ref[...]
    h_scr[FFN_LEAD + tl:] = jnp.concatenate([pre(xn_ref[...]) * keep_next, pad], axis=0).astype(BF16)

    def project(j):
        hb = h_scr[...]
        for part in range(2):
            lo = part * FFN_DIM + j * nc
            u = _dot(hb, wu_ref[:, lo:lo + nc])
            for s in range(nc // LANES):
                u_scr[j % 2, part, s] = u[:, s * LANES:(s + 1) * LANES]

    def conv(j, part, s, blk, scale):
        lo = part * FFN_DIM + j * nc + s * LANES
        w = cw_ref[:, lo:lo + LANES] * scale
        bias = cb_ref[:, lo:lo + LANES] * scale
        first = FFN_LEAD + blk * sub
        rows = lambda start: u_scr[j % 2, part, s, pl.ds(start, half, stride=2), :]
        before, even, odd, after = rows(first - 1), rows(first), rows(first + 1), rows(first + 2)
        return (before * w[0:1] + even * w[1:2] + odd * w[2:3] + bias,
                even * w[0:1] + odd * w[1:2] + after * w[2:3] + bias)

    def activate(j):
        for s in range(nc // LANES):
            cols = slice(j * nc + s * LANES, j * nc + (s + 1) * LANES)
            for blk in range(tl // sub):
                gate = conv(j, 0, s, blk, 1.0)
                val = conv(j, 1, s, blk, 0.5)
                for par in range(2):
                    r0 = blk * sub + par * half
                    act_scr[r0:r0 + half, cols] = (_two_gelu_tanh(gate[par]) * val[par]).astype(BF16)

    project(0)
    for j in range(N_FFN_CHUNKS):
        if j + 1 < N_FFN_CHUNKS:
            project(j + 1)
        activate(j)
    for blk in range(tl // sub):
        r0 = blk * sub
        y = _dot(act_scr[r0:r0 + sub, :], wd_ref[...])
        for s in range(y_scr.shape[0]):
            for par in range(2):
                y_scr[s, pl.ds(par, half, stride=2), :] = (
                    y[par * half:(par + 1) * half, s * LANES:(s + 1) * LANES])
        y = jnp.concatenate([y_scr[s] for s in range(y_scr.shape[0])], axis=1)
        x2 = xm_ref[r0:r0 + sub, :] + mod_ref[0, 5:6, :] * y
        o_ref[r0:r0 + sub, :] = _rms_norm(x2, fnw_ref[...])


def _ffn(x1, h2, mod3, norm_w, w_u, cw, cb, w_d, fnorm_w, seq):
    t, d = x1.shape
    tl = FFN_TILE
    tpb = seq // tl
    hb = tl // HALO
    last = t // HALO - 1
    rows = FFN_LEAD + tl + FFN_LEAD
    return pl.pallas_call(
        functools.partial(_ffn_kernel, tiles_per_seq=tpb),
        grid=(t // tl,),
        in_specs=[pl.BlockSpec((tl, d), lambda i: (i, 0)),
                  pl.BlockSpec((tl, d), lambda i: (i, 0)),
                  pl.BlockSpec((HALO, d), lambda i: (jnp.maximum(i * hb - 1, 0), 0)),
                  pl.BlockSpec((HALO, d), lambda i: (jnp.minimum((i + 1) * hb, last), 0)),
                  pl.BlockSpec((1, N_MOD, d), lambda i: (i // tpb, 0, 0)),
                  _const_spec((1, d)),
                  _const_spec(w_u.shape), _const_spec(cw.shape), _const_spec(cb.shape),
                  _const_spec(w_d.shape), _const_spec((1, d))],
        out_specs=pl.BlockSpec((tl, d), lambda i: (i, 0)),
        out_shape=jax.ShapeDtypeStruct((t, d), F32),
        scratch_shapes=[pltpu.VMEM((rows, d), BF16),
                        pltpu.VMEM((2, 2, FFN_CHUNK // LANES, rows, LANES), F32),
                        pltpu.VMEM((tl, FFN_DIM), BF16),
                        pltpu.VMEM((d // LANES, FFN_SUB, LANES), F32)],
        compiler_params=_params(("parallel",)),
        name="ffn",
    )(x1, h2, x1, x1, mod3, norm_w, w_u, cw, cb, w_d, fnorm_w)


def _rope_tables(seq):
    n_freq = QK_DIM // 4
    lane = jnp.arange(LANES)
    inv = ROPE_BASE ** (-(lane % n_freq).astype(F32) / n_freq)
    sign = jnp.where(lane < LANES // 2, -1.0, 1.0).astype(F32)

    def table(n):
        ang = jnp.arange(n, dtype=F32)[:, None] * inv[None, :]
        return jnp.stack([jnp.cos(ang), jnp.sin(ang) * sign])

    return table(seq // GRID_W), table(GRID_W)


def kernel(x, c, ctx, c_ctx, w_mod, b_mod, norm1_w, w_in, ret_decay_f, ret_decay_b,
           w_ret_out, w_four_out, w_branch_gate, b_branch_gate, w_out, norm2_w,
           w_up, conv_w, conv_b, w_down, final_norm_w):
    assert w_mod.shape[0] == 1, "single-layer block"
    b, seq, d = x.shape
    t = b * seq
    assert b == 2 and seq % RET_CHUNK == 0
    assert seq % FFN_TILE == 0 and seq % MERGE_TILE == 0 and seq % PROJ_TILE == 0

    c8 = jnp.concatenate([c, c_ctx[None, :], jnp.zeros((8 - b - 1, d), F32)], axis=0)
    mod, w_in_b, w_qk = _modulation(c8, w_mod[0], b_mod[0], w_in[0])
    mod3 = mod.reshape(8, N_MOD, d)
    n1w = norm1_w[0].reshape(1, d)

    kc, vc = _ctx_proj(ctx, mod3, n1w, w_qk, w_in_b)
    rtab, ctab = _rope_tables(seq)
    x2 = x.reshape(t, d)
    (q, k, v, sg, f), (w_bg, w_ro, w_fo, w_o) = _proj(
        x2, mod3, n1w, w_qk, w_in_b, rtab, ctab, seq,
        [w_branch_gate[0], w_ret_out[0], w_four_out[0], w_out[0]], (0.5, 1.0, 1.0, 1.0))

    a_f = jnp.broadcast_to(ret_decay_f[0][:, None, None], (HEADS, 1, LANES))
    a_b = jnp.broadcast_to(ret_decay_b[0][:, None, None], (HEADS, 1, LANES))
    y, tw, w2, w_c = _dft_first_stage(f.reshape(b, seq, F_WIDTH))
    ro, z = _retention_and_dft(a_f, a_b, q.reshape(b, seq, QK_WIDTH), k.reshape(b, seq, QK_WIDTH),
                               v.reshape(b, seq, V_WIDTH), kc, vc, y, tw, w2, w_c)

    n2w = norm2_w[0].reshape(1, d)
    (x1, h2), (w_u, w_d) = _merge(
        x2, mod3, n1w, n2w, ro.reshape(t, V_WIDTH), sg, z.reshape(t, F_WIDTH),
        w_bg, b_branch_gate[0].reshape(1, 2 * d), w_ro, w_fo, w_o, seq, [w_up[0], w_down[0]])

    out = _ffn(x1, h2, mod3, n2w, w_u, conv_w[0], conv_b[0].reshape(1, 2 * FFN_DIM), w_d,
               final_norm_w.reshape(1, d), seq)
    return out.reshape(b, seq, d)
```

```python
import functools

import numpy as np
import jax
import jax.numpy as jnp
from jax import lax
from jax.experimental import pallas as pl
from jax.experimental.pallas import tpu as pltpu

F32 = jnp.float32
BF16 = jnp.bfloat16

D_MODEL = 1024
GRID_W = 64
HEADS = 8
QK_DIM = 64
V_DIM = 128
QK_WIDTH = HEADS * QK_DIM
V_WIDTH = HEADS * V_DIM
ROPE_BASE = 10000.0
F_GROUPS = 4
F_GROUP_DIM = 128
F_WIDTH = F_GROUPS * F_GROUP_DIM
K_OFF = QK_WIDTH
V_OFF = K_OFF + QK_WIDTH
G_OFF = V_OFF + V_WIDTH
F_OFF = G_OFF + V_WIDTH
IN_COLS = F_OFF + F_WIDTH
FFN_DIM = 2816
N_MOD = 6
EPS = 1e-6

LANES = 128
RET_CHUNK = 256
RET_UNROLL = 16
FFN_CHUNK = 256
N_FFN_CHUNKS = FFN_DIM // FFN_CHUNK
FFN_TILE = 512
MERGE_TILE = 1024
MERGE_SUB = 512
PROJ_TILE = 1024
PROJ_SUB = 256
MOD_STEPS = 2
DFT_N1 = 64
BF16_ROWS = 16
DFT_ROWS = BF16_ROWS
DFT_A_ROWS = 2 * DFT_ROWS
VMEM_LIMIT = 56 * 1024 * 1024


def _params(sem):
    return pltpu.CompilerParams(dimension_semantics=sem, vmem_limit_bytes=VMEM_LIMIT)


def _dot(a, b):
    return jnp.dot(a, b, preferred_element_type=F32)


def _rms_norm(x, w):
    return x * lax.rsqrt(jnp.mean(x * x, axis=-1, keepdims=True) + EPS) * w


def _mod_norm(x, w, shift, scale):
    return x * lax.rsqrt(jnp.mean(x * x, axis=-1, keepdims=True) + EPS) * (w * (1.0 + scale)) + shift


def _const_spec(shape):
    zeros = (0,) * len(shape)
    return pl.BlockSpec(shape, lambda *_: zeros, pipeline_mode=pl.Buffered(1))


def _cast_jobs(weights, steps):
    specs, shapes = [], []
    for w in weights:
        n_rows = w.shape[0]
        rows = -(-n_rows // steps)
        while rows % BF16_ROWS or n_rows % rows:
            rows += 1
        last = n_rows // rows - 1
        specs.append(pl.BlockSpec((rows, w.shape[1]), lambda i, last=last: (jnp.minimum(i, last), 0)))
        shapes.append(jax.ShapeDtypeStruct(w.shape, BF16))
    return specs, shapes


def _run_casts(refs, scales=None):
    n = len(refs) // 2
    for j, (src, dst) in enumerate(zip(refs[:n], refs[n:])):
        scale = 1.0 if scales is None else scales[j]
        dst[...] = (src[...] if scale == 1.0 else src[...] * scale).astype(BF16)


def _mod_kernel(c_ref, w_ref, b_ref, win_ref, o_ref, winb_ref, wqk_ref):
    c = c_ref[...]
    s = c * jax.nn.sigmoid(c)
    o_ref[...] = _dot(s.astype(BF16), w_ref[...].astype(BF16)) + b_ref[...]
    wb = win_ref[...].astype(BF16)
    winb_ref[...] = wb
    n_qk = 2 * QK_WIDTH
    src = lax.broadcasted_iota(jnp.int32, (n_qk, n_qk), 0)
    dst = lax.broadcasted_iota(jnp.int32, (n_qk, n_qk), 1)
    select = (src == _qk_source_column(dst)).astype(BF16)
    wqk_ref[...] = _dot(wb[:, :n_qk], select).astype(BF16)


def _qk_source_column(col):
    half = QK_DIM // 2
    lane = col % LANES
    head = 2 * ((col % QK_WIDTH) // LANES) + (lane // half) % 2
    return (col // QK_WIDTH) * QK_WIDTH + head * QK_DIM + (lane // (2 * half)) * half + lane % half


def _modulation(c8, w_mod, b_mod, w_in):
    n = w_mod.shape[1]
    steps = MOD_STEPS
    tn = n // steps
    d, n_in = w_in.shape
    rows = d // steps
    return pl.pallas_call(
        _mod_kernel,
        grid=(steps,),
        in_specs=[_const_spec((8, D_MODEL)),
                  pl.BlockSpec((D_MODEL, tn), lambda j: (0, j)),
                  pl.BlockSpec((1, tn), lambda j: (0, j)),
                  pl.BlockSpec((rows, n_in), lambda j: (j, 0))],
        out_specs=[pl.BlockSpec((8, tn), lambda j: (0, j)),
                   pl.BlockSpec((rows, n_in), lambda j: (j, 0)),
                   pl.BlockSpec((rows, 2 * QK_WIDTH), lambda j: (j, 0))],
        out_shape=[jax.ShapeDtypeStruct((8, n), F32),
                   jax.ShapeDtypeStruct((d, n_in), BF16),
                   jax.ShapeDtypeStruct((d, 2 * QK_WIDTH), BF16)],
        compiler_params=_params(("parallel",)),
        name="mod",
    )(c8, w_mod, b_mod.reshape(1, n), w_in)


def _ctx_kernel(x_ref, mod_ref, nw_ref, wk_ref, wv_ref, k_ref, v_ref):
    x = x_ref[0]
    h = _mod_norm(x, nw_ref[...], mod_ref[0, 0:1, :], mod_ref[0, 1:2, :])
    hb = h.astype(BF16)
    k_ref[0] = _dot(hb, wk_ref[...]).astype(BF16)
    v_ref[0] = _dot(hb, wv_ref[...]).astype(BF16)


def _ctx_proj(ctx, mod3, norm_w, w_qk, w_in):
    b, lc, d = ctx.shape
    assert V_OFF % V_WIDTH == 0
    return pl.pallas_call(
        _ctx_kernel,
        grid=(b,),
        in_specs=[pl.BlockSpec((1, lc, d), lambda i: (i, 0, 0)),
                  pl.BlockSpec((1, N_MOD, d), lambda i: (2, 0, 0)),
                  _const_spec((1, d)),
                  pl.BlockSpec((d, QK_WIDTH), lambda i: (0, 1), pipeline_mode=pl.Buffered(1)),
                  pl.BlockSpec((d, V_WIDTH), lambda i: (0, V_OFF // V_WIDTH),
                               pipeline_mode=pl.Buffered(1))],
        out_specs=[pl.BlockSpec((1, lc, QK_WIDTH), lambda i: (i, 0, 0)),
                   pl.BlockSpec((1, lc, V_WIDTH), lambda i: (i, 0, 0))],
        out_shape=[jax.ShapeDtypeStruct((b, lc, QK_WIDTH), BF16),
                   jax.ShapeDtypeStruct((b, lc, V_WIDTH), BF16)],
        compiler_params=_params(("parallel",)),
        name="ctx_proj",
    )(ctx, mod3, norm_w, w_qk, w_in)


def _proj_kernel(x_ref, mod_ref, nw_ref, wqk_ref, w_ref, rtab_ref, ctab_ref, *refs, cast_scales):
    n_cast = len(cast_scales)
    q_ref, k_ref, v_ref, sg_ref, f_ref = refs[n_cast:n_cast + 5]
    _run_casts(refs[:n_cast] + refs[n_cast + 5:], cast_scales)
    lane = lax.broadcasted_iota(jnp.int32, (1, LANES), 1)
    by_row = (lane % (QK_DIM // 2)) < QK_DIM // 4

    def rope(t, trig, scale, out_ref, rows):
        for j in range(QK_WIDTH // LANES):
            tj = t[:, j * LANES:(j + 1) * LANES]
            r = tj * trig[0] + pltpu.roll(tj, LANES // 2, 1) * trig[1]
            out_ref[rows, j * LANES:(j + 1) * LANES] = (r * scale).astype(BF16)

    for r0 in range(0, x_ref.shape[0], PROJ_SUB):
        rows = slice(r0, r0 + PROJ_SUB)
        trig = []
        for cs in range(2):
            trig.append(jnp.concatenate(
                [jnp.where(by_row, rtab_ref[cs, g:g + 1, :], ctab_ref[cs])
                 for g in range(r0 // GRID_W, (r0 + PROJ_SUB) // GRID_W)], axis=0))
        x = x_ref[rows, :]
        h = _mod_norm(x, nw_ref[...], mod_ref[0, 0:1, :], mod_ref[0, 1:2, :])
        hb = h.astype(BF16)
        rope(_dot(hb, wqk_ref[:, :QK_WIDTH]), trig, QK_DIM ** -0.5, q_ref, rows)
        rope(_dot(hb, wqk_ref[:, QK_WIDTH:]), trig, 1.0, k_ref, rows)
        v_ref[rows, :] = _dot(hb, w_ref[:, V_OFF:G_OFF]).astype(BF16)
        g = _dot(hb, w_ref[:, G_OFF:F_OFF])
        hg = 0.5 * g
        sg_ref[rows, :] = (hg * jnp.tanh(hg) + hg).astype(BF16)
        f_ref[rows, :] = _dot(hb, w_ref[:, F_OFF:IN_COLS]).astype(BF16)


def _proj(x2, mod3, norm_w, w_qk, w_in, rtab, ctab, seq, to_cast, cast_scales):
    t, d = x2.shape
    tl = PROJ_TILE
    tpb = seq // tl
    tok = lambda w: pl.BlockSpec((tl, w), lambda i: (i, 0))
    cast_specs, cast_shapes = _cast_jobs(to_cast, t // tl)
    out = pl.pallas_call(
        functools.partial(_proj_kernel, cast_scales=tuple(cast_scales)),
        grid=(t // tl,),
        in_specs=[tok(d),
                  pl.BlockSpec((1, N_MOD, d), lambda i: (i // tpb, 0, 0)),
                  _const_spec((1, d)),
                  _const_spec(w_qk.shape), _const_spec(w_in.shape),
                  pl.BlockSpec((2, tl // GRID_W, LANES), lambda i: (0, i % tpb, 0)),
                  _const_spec(ctab.shape)] + cast_specs,
        out_specs=[tok(QK_WIDTH), tok(QK_WIDTH), tok(V_WIDTH), tok(V_WIDTH),
                   tok(F_WIDTH)] + cast_specs,
        out_shape=[jax.ShapeDtypeStruct((t, QK_WIDTH), BF16),
                   jax.ShapeDtypeStruct((t, QK_WIDTH), BF16),
                   jax.ShapeDtypeStruct((t, V_WIDTH), BF16),
                   jax.ShapeDtypeStruct((t, V_WIDTH), BF16),
                   jax.ShapeDtypeStruct((t, F_WIDTH), BF16)] + cast_shapes,
        compiler_params=_params(("arbitrary",)),
        name="proj",
    )(x2, mod3, norm_w, w_qk, w_in, rtab, ctab, *to_cast)
    return out[:5], out[5:]


def _retention_kernel(af_ref, ab_ref, q_ref, k_ref, v_ref, kc_ref, vc_ref,
                      tw_ref, w2_ref, cc_ref, y_ref,
                      o_ref, z_ref, ds_scr, st_scr, d_scr, tab_scr, z_scr):
    c = RET_CHUNK
    seq = q_ref.shape[1]
    lc = kc_ref.shape[1]
    n = seq // c
    lgf = [-jnp.exp(af_ref[hh]) for hh in range(2)]
    lgb = [-jnp.exp(ab_ref[hh]) for hh in range(2)]
    half = QK_DIM // 2

    lane_head = (lax.broadcasted_iota(jnp.int32, (1, LANES), 1) // half) % 2
    masks = [(lane_head == hh).astype(BF16) for hh in range(2)]
    lgf_lane = jnp.where(lane_head == 0, lgf[0], lgf[1])
    lgb_lane = jnp.where(lane_head == 0, lgb[0], lgb[1])
    row_head = (lax.broadcasted_iota(jnp.int32, (LANES, 1), 0) // half) % 2
    lgf_row = jnp.where(row_head == 0, lgf[0][:, 0:1], lgf[1][:, 0:1])
    lgb_row = jnp.where(row_head == 0, lgb[0][:, 0:1], lgb[1][:, 0:1])

    pos = lax.broadcasted_iota(jnp.int32, (c, LANES), 0).astype(F32)
    tab_scr[0] = jnp.exp(lgf_lane * (pos + 1.0))
    tab_scr[1] = jnp.exp(lgb_lane * (c - pos))
    diff = (lax.broadcasted_iota(jnp.int32, (c, c), 0)
            - lax.broadcasted_iota(jnp.int32, (c, c), 1)).astype(F32)
    for hh in range(2):
        d_scr[hh] = (jnp.where(diff >= 0, jnp.exp(lgf[hh][:, 0:1] * jnp.maximum(diff, 0.0)), 0.0)
                     + jnp.where(diff <= 0, jnp.exp(lgb[hh][:, 0:1] * jnp.maximum(-diff, 0.0)), 0.0))

    def k_decays(tokens):
        t = lax.broadcasted_iota(jnp.int32, (1, tokens), 1).astype(F32)
        return jnp.exp(lgf_row * (tokens - 1.0 - t)), jnp.exp(lgb_row * t)

    def state_increment(k_rows, v_rows, decays):
        kt = jnp.transpose(k_rows.astype(F32))
        lhs = jnp.concatenate([kt * decays[0], kt * decays[1]], axis=0).astype(BF16)
        return _dot(lhs, v_rows)

    kdec = k_decays(c)

    def incr(i):
        rows = pl.ds(pl.multiple_of(i * c, c), c)
        ds_scr[i] = state_increment(k_ref[0, rows, :], v_ref[0, rows, :], kdec)

    half_k1 = DFT_ROWS // 2
    per_k1 = n // half_k1
    cc = cc_ref[...].astype(BF16)

    def incr_and_dft(j, carry):
        for u in range(per_k1):
            incr(j * per_k1 + u)
        _dft_second_stage(j, tw_ref, w2_ref, cc, y_ref, z_scr)
        return carry

    lax.fori_loop(0, half_k1, incr_and_dft, 0, unroll=RET_UNROLL // per_k1)

    s0 = state_increment(kc_ref[0], vc_ref[0], k_decays(lc))
    col_head = lax.broadcasted_iota(jnp.int32, (1, 2 * V_DIM), 1) // V_DIM
    own = (row_head == col_head).astype(F32)
    decay_f = jnp.exp(lgf_row * c) * own
    decay_b = jnp.exp(lgb_row * c) * own

    def scan_f(i, s):
        st_scr[i, 0:LANES, :] = (s * own).astype(BF16)
        return decay_f * s + ds_scr[i, 0:LANES, :]

    lax.fori_loop(0, n, scan_f, s0[0:LANES])

    def scan_b(t, s):
        i = n - 1 - t
        st_scr[i, LANES:2 * LANES, :] = (s * own).astype(BF16)
        return decay_b * s + ds_scr[i, LANES:2 * LANES, :]

    lax.fori_loop(0, n, scan_b, s0[LANES:2 * LANES])

    def outputs(i):
        rows = pl.ds(pl.multiple_of(i * c, c), c)
        q = q_ref[0, rows, :]
        k = k_ref[0, rows, :]
        qf = q.astype(F32)
        qd = jnp.concatenate([qf * tab_scr[0], qf * tab_scr[1]], axis=1).astype(BF16)
        inter = _dot(qd, st_scr[i])
        for hh in range(2):
            cols = slice(hh * V_DIM, (hh + 1) * V_DIM)
            scores = lax.dot_general(q * masks[hh], k, (((1,), (1,)), ((), ())),
                                     preferred_element_type=F32)
            o = _dot((scores * d_scr[hh]).astype(BF16), v_ref[0, rows, cols]) + inter[:, cols]
            o_ref[0, rows, cols] = o.astype(BF16)

    def outputs_and_dft(j, carry):
        for u in range(per_k1):
            outputs(j * per_k1 + u)
        _dft_second_stage(half_k1 + j, tw_ref, w2_ref, cc, y_ref, z_scr)
        return carry

    lax.fori_loop(0, half_k1, outputs_and_dft, 0, unroll=RET_UNROLL // per_k1)
    _dft_emit(z_scr, z_ref)


def _retention_and_dft(a_f, a_b, q, k, v, kc, vc, y, tw, w2, w_c):
    b, seq, _ = v.shape
    lc = kc.shape[1]
    c = RET_CHUNK
    n = seq // c
    r = DFT_ROWS
    _, _, n1, n2, w = y.shape
    assert n % r == 0 and n1 // r == HEADS // 2
    dec = pl.BlockSpec((2, 1, LANES), lambda i, p: (p, 0, 0))
    qk = lambda rows: pl.BlockSpec((1, rows, LANES), lambda i, p: (i, 0, p))
    vv = lambda rows: pl.BlockSpec((1, rows, 2 * V_DIM), lambda i, p: (i, 0, p))
    return pl.pallas_call(
        _retention_kernel,
        grid=(b, HEADS // 2),
        in_specs=[dec, dec, qk(seq), qk(seq), vv(seq), qk(lc), vv(lc),
                  pl.BlockSpec((r, 2, n2), lambda i, p: (p, 0, 0)),
                  _const_spec((2, n2, n2)),
                  _const_spec((2 * F_GROUP_DIM, F_GROUP_DIM)),
                  pl.BlockSpec((1, 2, r, n2, w), lambda i, p: (i, 0, p, 0, 0))],
        out_specs=[vv(seq), pl.BlockSpec((1, n2, r, w), lambda i, p: (i, 0, p, 0))],
        out_shape=[jax.ShapeDtypeStruct((b, seq, V_WIDTH), BF16),
                   jax.ShapeDtypeStruct((b, n2, n1, w), BF16)],
        scratch_shapes=[pltpu.VMEM((n, 2 * LANES, 2 * V_DIM), F32),
                        pltpu.VMEM((n, 2 * LANES, 2 * V_DIM), BF16),
                        pltpu.VMEM((2, c, c), F32),
                        pltpu.VMEM((2, c, LANES), F32),
                        pltpu.VMEM((F_GROUPS, n2 * r, F_GROUP_DIM), F32)],
        compiler_params=_params(("parallel", "parallel")),
        name="retention_dft",
    )(a_f, a_b, q, k, v, kc, vc, tw, w2, w_c, y)


def _dft_a_kernel(w_ref, x_ref, y_ref):
    _, n1, rows, w = x_ref.shape
    r = DFT_ROWS
    nh = n1 // 2 + 1
    for r0 in range(0, rows, r):
        x = x_ref[0, :, r0:r0 + r, :].reshape(n1 * r, w)
        y = _dot(w_ref[...], x).reshape(2, nh, r, w).astype(BF16)
        y_ref[0, :, 0:nh, r0:r0 + r, :] = y
        for k1 in range(1, n1 // 2):
            y_ref[0, 0, n1 - k1, r0:r0 + r, :] = y[0, k1]
            y_ref[0, 1, n1 - k1, r0:r0 + r, :] = -y[1, k1]


def _dft_second_stage(j, tw_ref, w2_ref, cc, y_ref, z_scr):
    n2 = y_ref.shape[3]
    w2c = w2_ref[0]
    w2s = w2_ref[1]
    tc = tw_ref[j, 0:1, :]
    ts = tw_ref[j, 1:2, :]
    ec = w2c * tc - w2s * ts
    es = w2s * tc + w2c * ts
    m = jnp.concatenate([jnp.concatenate([ec, -es], axis=1),
                         jnp.concatenate([es, ec], axis=1)], axis=0).astype(BF16)
    y = jnp.concatenate([y_ref[0, 0, j], y_ref[0, 1, j]], axis=0)
    zz = _dot(m, y)
    for g in range(F_GROUPS):
        cols = slice(g * F_GROUP_DIM, (g + 1) * F_GROUP_DIM)
        zcs = jnp.concatenate([zz[:n2, cols], zz[n2:, cols]], axis=1).astype(BF16)
        z_scr[g, pl.ds(j, n2, stride=DFT_ROWS), :] = _dot(zcs, cc)


def _dft_emit(z_scr, z_ref):
    n2 = z_ref.shape[1]
    z = jnp.concatenate([z_scr[g] for g in range(F_GROUPS)], axis=1)
    z_ref[0] = z.reshape(n2, DFT_ROWS, F_WIDTH).astype(BF16)


def _dft_tables(seq):
    n1 = DFT_N1
    n2 = seq // n1

    def cs(num, den):
        ang = 2.0 * np.pi * (num % den) / den
        return np.cos(ang), np.sin(ang)

    a = np.arange(n1)
    w_a = np.concatenate(cs(a[:n1 // 2 + 1, None] * a[None, :], n1), axis=0)
    m = np.arange(n2)
    tw = np.stack(cs(a[:, None] * m[None, :], seq), axis=1)
    w2 = np.stack(cs(m[:, None] * m[None, :], n2), axis=0)
    ch = np.arange(F_GROUP_DIM)
    cc, sc = cs(ch[:, None] * ch[None, :], F_GROUP_DIM)
    scale = 1.0 / np.sqrt(seq * F_GROUP_DIM)
    w_c = np.concatenate([cc, -sc], axis=0) * scale
    return [jnp.asarray(t, dtype=F32) for t in (w_a, tw, w2, w_c)]


def _dft_first_stage(f):
    b, seq, w = f.shape
    n1 = DFT_N1
    n2 = seq // n1
    w_a, tw, w2, w_c = _dft_tables(seq)
    r = DFT_ROWS
    spread = (jnp.arange(n1 * r)[None, :] // r == jnp.arange(n1)[:, None]).astype(F32)
    w_rep = jnp.dot(w_a, spread, precision=lax.Precision.HIGHEST)
    same_r = jnp.arange(w_a.shape[0] * r)[:, None] % r == jnp.arange(n1 * r)[None, :] % r
    w_a = jnp.where(same_r, jnp.repeat(w_rep, r, axis=0), 0.0).astype(BF16)
    y = pl.pallas_call(
        _dft_a_kernel,
        grid=(b, n2 // DFT_A_ROWS),
        in_specs=[_const_spec(w_a.shape),
                  pl.BlockSpec((1, n1, DFT_A_ROWS, w), lambda i, j: (i, 0, j, 0))],
        out_specs=pl.BlockSpec((1, 2, n1, DFT_A_ROWS, w), lambda i, j: (i, 0, 0, j, 0)),
        out_shape=jax.ShapeDtypeStruct((b, 2, n1, n2, w), BF16),
        compiler_params=_params(("parallel", "parallel")),
        name="dft_a",
    )(w_a, f.reshape(b, n1, n2, w))
    return y, tw, w2, w_c


def _merge_kernel(x_ref, mod_ref, nw_ref, n2w_ref, ro_ref, sg_ref, z_ref, wbg_ref, bbg_ref,
                  wro_ref, wfo_ref, wout_ref, *refs, n_cast):
    o_ref, h2_ref = refs[n_cast:n_cast + 2]
    _run_casts(refs[:n_cast] + refs[n_cast + 2:])
    for r0 in range(0, x_ref.shape[0], MERGE_SUB):
        rows = slice(r0, r0 + MERGE_SUB)
        x = x_ref[rows, :]
        h = _mod_norm(x, nw_ref[...], mod_ref[0, 0:1, :], mod_ref[0, 1:2, :])
        t = jnp.tanh(_dot(h.astype(BF16), wbg_ref[...]) + 0.5 * bbg_ref[...])
        gated = []
        for hd in range(HEADS):
            cols = slice(hd * V_DIM, (hd + 1) * V_DIM)
            o = ro_ref[rows, cols].astype(F32)
            oc = o - jnp.mean(o, axis=-1, keepdims=True)
            var = jnp.mean(oc * oc, axis=-1, keepdims=True)
            gated.append((oc * lax.rsqrt(var + EPS) * sg_ref[rows, cols].astype(F32)).astype(BF16))
        ret_d = _dot(jnp.concatenate(gated, axis=1), wro_ref[...])
        four_d = _dot(z_ref[rows, :], wfo_ref[...])
        m2 = (t[:, :D_MODEL] * ret_d + ret_d) + (t[:, D_MODEL:] * four_d + four_d)
        y2 = _dot(m2.astype(BF16), wout_ref[...])
        x1 = x + (0.5 * mod_ref[0, 2:3, :]) * y2
        o_ref[rows, :] = x1
        h2 = _mod_norm(x1, n2w_ref[...], mod_ref[0, 3:4, :], mod_ref[0, 4:5, :])
        h2_ref[rows, :] = h2.astype(BF16)


def _merge(x2, mod3, norm_w, norm2_w, ro, sg, z, w_bg, b_bg, w_ro, w_fo, w_out, seq, to_cast):
    t, d = x2.shape
    tl = MERGE_TILE
    tpb = seq // tl
    tok = lambda w: pl.BlockSpec((tl, w), lambda i: (i, 0))
    cast_specs, cast_shapes = _cast_jobs(to_cast, t // tl)
    out = pl.pallas_call(
        functools.partial(_merge_kernel, n_cast=len(to_cast)),
        grid=(t // tl,),
        in_specs=[tok(d),
                  pl.BlockSpec((1, N_MOD, d), lambda i: (i // tpb, 0, 0)),
                  _const_spec((1, d)), _const_spec((1, d)),
                  tok(V_WIDTH), tok(V_WIDTH), tok(F_WIDTH),
                  _const_spec(w_bg.shape), _const_spec((1, 2 * d)),
                  _const_spec(w_ro.shape), _const_spec(w_fo.shape),
                  _const_spec(w_out.shape)] + cast_specs,
        out_specs=[tok(d), tok(d)] + cast_specs,
        out_shape=[jax.ShapeDtypeStruct((t, d), F32),
                   jax.ShapeDtypeStruct((t, d), BF16)] + cast_shapes,
        compiler_params=_params(("arbitrary",)),
        name="merge",
    )(x2, mod3, norm_w, norm2_w, ro, sg, z, w_bg, b_bg, w_ro, w_fo, w_out, *to_cast)
    return out[:2], out[2:]


HALO = 8
FFN_LEAD = 16
FFN_SUB = 256


def _two_gelu_tanh(x):
    c1 = np.sqrt(2.0 / np.pi)
    return x + x * jnp.tanh(x * (c1 + (c1 * 0.044715) * (x * x)))


def _ffn_kernel(xm_ref, hm_ref, xp_ref, xn_ref, mod_ref, nw_ref, wu_ref, cw_ref, cb_ref,
                wd_ref, fnw_ref, o_ref, h_scr, u_scr, act_scr, y_scr, *, tiles_per_seq):
    tl = xm_ref.shape[0]
    sub = FFN_SUB
    half = sub // 2
    nc = FFN_CHUNK
    i = pl.program_id(0)
    keep_prev = ((i % tiles_per_seq) != 0).astype(F32)
    keep_next = ((i % tiles_per_seq) != tiles_per_seq - 1).astype(F32)

    def pre(x):
        return _mod_norm(x, nw_ref[...], mod_ref[0, 3:4, :], mod_ref[0, 4:5, :])

    pad = jnp.zeros((FFN_LEAD - HALO, xm_ref.shape[1]), F32)
    h_scr[0:FFN_LEAD] = jnp.concatenate([pad, pre(xp_ref[...]) * keep_prev], axis=0).astype(BF16)
    h_scr[FFN_LEAD:FFN_LEAD + tl] = hm_ref[...]
    h_scr[FFN_LEAD + tl:] = jnp.concatenate([pre(xn_ref[...]) * keep_next, pad], axis=0).astype(BF16)

    def project(j):
        hb = h_scr[...]
        for part in range(2):
            lo = part * FFN_DIM + j * nc
            u = _dot(hb, wu_ref[:, lo:lo + nc])
            for s in range(nc // LANES):
                u_scr[j % 2, part, s] = u[:, s * LANES:(s + 1) * LANES]

    def conv(j, part, s, blk, scale):
        lo = part * FFN_DIM + j * nc + s * LANES
        w = cw_ref[:, lo:lo + LANES] * scale
        bias = cb_ref[:, lo:lo + LANES] * scale
        first = FFN_LEAD + blk * sub
        rows = lambda start: u_scr[j % 2, part, s, pl.ds(start, half, stride=2), :]
        before, even, odd, after = rows(first - 1), rows(first), rows(first + 1), rows(first + 2)
        return (before * w[0:1] + even * w[1:2] + odd * w[2:3] + bias,
                even * w[0:1] + odd * w[1:2] + after * w[2:3] + bias)

    def activate(j):
        for s in range(nc // LANES):
            cols = slice(j * nc + s * LANES, j * nc + (s + 1) * LANES)
            for blk in range(tl // sub):
                gate = conv(j, 0, s, blk, 1.0)
                val = conv(j, 1, s, blk, 0.5)
                for par in range(2):
                    r0 = blk * sub + par * half
                    act_scr[r0:r0 + half, cols] = (_two_gelu_tanh(gate[par]) * val[par]).astype(BF16)

    project(0)
    for j in range(N_FFN_CHUNKS):
        if j + 1 < N_FFN_CHUNKS:
            project(j + 1)
        activate(j)
    for blk in range(tl // sub):
        r0 = blk * sub
        y = _dot(act_scr[r0:r0 + sub, :], wd_ref[...])
        for s in range(y_scr.shape[0]):
            for par in range(2):
                y_scr[s, pl.ds(par, half, stride=2), :] = (
                    y[par * half:(par + 1) * half, s * LANES:(s + 1) * LANES])
        y = jnp.concatenate([y_scr[s] for s in range(y_scr.shape[0])], axis=1)
        x2 = xm_ref[r0:r0 + sub, :] + mod_ref[0, 5:6, :] * y
        o_ref[r0:r0 + sub, :] = _rms_norm(x2, fnw_ref[...])


def _ffn(x1, h2, mod3, norm_w, w_u, cw, cb, w_d, fnorm_w, seq):
    t, d = x1.shape
    tl = FFN_TILE
    tpb = seq // tl
    hb = tl // HALO
    last = t // HALO - 1
    rows = FFN_LEAD + tl + FFN_LEAD
    return pl.pallas_call(
        functools.partial(_ffn_kernel, tiles_per_seq=tpb),
        grid=(t // tl,),
        in_specs=[pl.BlockSpec((tl, d), lambda i: (i, 0)),
                  pl.BlockSpec((tl, d), lambda i: (i, 0)),
                  pl.BlockSpec((HALO, d), lambda i: (jnp.maximum(i * hb - 1, 0), 0)),
                  pl.BlockSpec((HALO, d), lambda i: (jnp.minimum((i + 1) * hb, last), 0)),
                  pl.BlockSpec((1, N_MOD, d), lambda i: (i // tpb, 0, 0)),
                  _const_spec((1, d)),
                  _const_spec(w_u.shape), _const_spec(cw.shape), _const_spec(cb.shape),
                  _const_spec(w_d.shape), _const_spec((1, d))],
        out_specs=pl.BlockSpec((tl, d), lambda i: (i, 0)),
        out_shape=jax.ShapeDtypeStruct((t, d), F32),
        scratch_shapes=[pltpu.VMEM((rows, d), BF16),
                        pltpu.VMEM((2, 2, FFN_CHUNK // LANES, rows, LANES), F32),
                        pltpu.VMEM((tl, FFN_DIM), BF16),
                        pltpu.VMEM((d // LANES, FFN_SUB, LANES), F32)],
        compiler_params=_params(("parallel",)),
        name="ffn",
    )(x1, h2, x1, x1, mod3, norm_w, w_u, cw, cb, w_d, fnorm_w)


def _rope_tables(seq):
    n_freq = QK_DIM // 4
    lane = jnp.arange(LANES)
    inv = ROPE_BASE ** (-(lane % n_freq).astype(F32) / n_freq)
    sign = jnp.where(lane < LANES // 2, -1.0, 1.0).astype(F32)

    def table(n):
        ang = jnp.arange(n, dtype=F32)[:, None] * inv[None, :]
        return jnp.stack([jnp.cos(ang), jnp.sin(ang) * sign])

    return table(seq // GRID_W), table(GRID_W)


def kernel(x, c, ctx, c_ctx, w_mod, b_mod, norm1_w, w_in, ret_decay_f, ret_decay_b,
           w_ret_out, w_four_out, w_branch_gate, b_branch_gate, w_out, norm2_w,
           w_up, conv_w, conv_b, w_down, final_norm_w):
    assert w_mod.shape[0] == 1, "single-layer block"
    b, seq, d = x.shape
    t = b * seq
    assert b == 2 and seq % RET_CHUNK == 0
    assert seq % FFN_TILE == 0 and seq % MERGE_TILE == 0 and seq % PROJ_TILE == 0

    c8 = jnp.concatenate([c, c_ctx[None, :], jnp.zeros((8 - b - 1, d), F32)], axis=0)
    mod, w_in_b, w_qk = _modulation(c8, w_mod[0], b_mod[0], w_in[0])
    mod3 = mod.reshape(8, N_MOD, d)
    n1w = norm1_w[0].reshape(1, d)

    kc, vc = _ctx_proj(ctx, mod3, n1w, w_qk, w_in_b)
    rtab, ctab = _rope_tables(seq)
    x2 = x.reshape(t, d)
    (q, k, v, sg, f), (w_bg, w_ro, w_fo, w_o) = _proj(
        x2, mod3, n1w, w_qk, w_in_b, rtab, ctab, seq,
        [w_branch_gate[0], w_ret_out[0], w_four_out[0], w_out[0]], (0.5, 1.0, 1.0, 1.0))

    a_f = jnp.broadcast_to(ret_decay_f[0][:, None, None], (HEADS, 1, LANES))
    a_b = jnp.broadcast_to(ret_decay_b[0][:, None, None], (HEADS, 1, LANES))
    y, tw, w2, w_c = _dft_first_stage(f.reshape(b, seq, F_WIDTH))
    ro, z = _retention_and_dft(a_f, a_b, q.reshape(b, seq, QK_WIDTH), k.reshape(b, seq, QK_WIDTH),
                               v.reshape(b, seq, V_WIDTH), kc, vc, y, tw, w2, w_c)

    n2w = norm2_w[0].reshape(1, d)
    (x1, h2), (w_u, w_d) = _merge(
        x2, mod3, n1w, n2w, ro.reshape(t, V_WIDTH), sg, z.reshape(t, F_WIDTH),
        w_bg, b_branch_gate[0].reshape(1, 2 * d), w_ro, w_fo, w_o, seq, [w_up[0], w_down[0]])

    out = _ffn(x1, h2, mod3, n2w, w_u, conv_w[0], conv_b[0].reshape(1, 2 * FFN_DIM), w_d,
               final_norm_w.reshape(1, d), seq)
    return out.reshape(b, seq, d)
```

```python
import functools

import numpy as np
import jax
import jax.numpy as jnp
from jax import lax
from jax.experimental import pallas as pl
from jax.experimental.pallas import tpu as pltpu

F32 = jnp.float32
BF16 = jnp.bfloat16

D_MODEL = 1024
GRID_W = 64
HEADS = 8
QK_DIM = 64
V_DIM = 128
QK_WIDTH = HEADS * QK_DIM
V_WIDTH = HEADS * V_DIM
ROPE_BASE = 10000.0
F_GROUPS = 4
F_GROUP_DIM = 128
F_WIDTH = F_GROUPS * F_GROUP_DIM
K_OFF = QK_WIDTH
V_OFF = K_OFF + QK_WIDTH
G_OFF = V_OFF + V_WIDTH
F_OFF = G_OFF + V_WIDTH
IN_COLS = F_OFF + F_WIDTH
FFN_DIM = 2816
N_MOD = 6
EPS = 1e-6

LANES = 128
RET_CHUNK = 256
RET_UNROLL = 16
FFN_CHUNK = 256
N_FFN_CHUNKS = FFN_DIM // FFN_CHUNK
FFN_TILE = 512
MERGE_TILE = 1024
MERGE_SUB = 512
PROJ_TILE = 1024
PROJ_SUB = 256
MOD_STEPS = 2
DFT_N1 = 64
BF16_ROWS = 16
DFT_ROWS = BF16_ROWS
DFT_A_ROWS = 2 * DFT_ROWS
VMEM_LIMIT = 56 * 1024 * 1024


def _params(sem):
    return pltpu.CompilerParams(dimension_semantics=sem, vmem_limit_bytes=VMEM_LIMIT)


def _dot(a, b):
    return jnp.dot(a, b, preferred_element_type=F32)


def _rms_norm(x, w):
    return x * lax.rsqrt(jnp.mean(x * x, axis=-1, keepdims=True) + EPS) * w


def _mod_norm(x, w, shift, scale):
    return x * lax.rsqrt(jnp.mean(x * x, axis=-1, keepdims=True) + EPS) * (w * (1.0 + scale)) + shift


def _const_spec(shape):
    zeros = (0,) * len(shape)
    return pl.BlockSpec(shape, lambda *_: zeros, pipeline_mode=pl.Buffered(1))


def _cast_jobs(weights, steps):
    specs, shapes = [], []
    for w in weights:
        n_rows = w.shape[0]
        rows = -(-n_rows // steps)
        while rows % BF16_ROWS or n_rows % rows:
            rows += 1
        last = n_rows // rows - 1
        specs.append(pl.BlockSpec((rows, w.shape[1]), lambda i, last=last: (jnp.minimum(i, last), 0)))
        shapes.append(jax.ShapeDtypeStruct(w.shape, BF16))
    return specs, shapes


def _run_casts(refs, scales=None):
    n = len(refs) // 2
    for j, (src, dst) in enumerate(zip(refs[:n], refs[n:])):
        scale = 1.0 if scales is None else scales[j]
        dst[...] = (src[...] if scale == 1.0 else src[...] * scale).astype(BF16)


def _mod_kernel(c_ref, w_ref, b_ref, win_ref, o_ref, winb_ref, wqk_ref):
    c = c_ref[...]
    s = c * jax.nn.sigmoid(c)
    o_ref[...] = _dot(s.astype(BF16), w_ref[...].astype(BF16)) + b_ref[...]
    wb = win_ref[...].astype(BF16)
    winb_ref[...] = wb
    n_qk = 2 * QK_WIDTH
    src = lax.broadcasted_iota(jnp.int32, (n_qk, n_qk), 0)
    dst = lax.broadcasted_iota(jnp.int32, (n_qk, n_qk), 1)
    select = (src == _qk_source_column(dst)).astype(BF16)
    wqk_ref[...] = _dot(wb[:, :n_qk], select).astype(BF16)


def _qk_source_column(col):
    half = QK_DIM // 2
    lane = col % LANES
    head = 2 * ((col % QK_WIDTH) // LANES) + (lane // half) % 2
    return (col // QK_WIDTH) * QK_WIDTH + head * QK_DIM + (lane // (2 * half)) * half + lane % half


def _modulation(c8, w_mod, b_mod, w_in):
    n = w_mod.shape[1]
    steps = MOD_STEPS
    tn = n // steps
    d, n_in = w_in.shape
    rows = d // steps
    return pl.pallas_call(
        _mod_kernel,
        grid=(steps,),
        in_specs=[_const_spec((8, D_MODEL)),
                  pl.BlockSpec((D_MODEL, tn), lambda j: (0, j)),
                  pl.BlockSpec((1, tn), lambda j: (0, j)),
                  pl.BlockSpec((rows, n_in), lambda j: (j, 0))],
        out_specs=[pl.BlockSpec((8, tn), lambda j: (0, j)),
                   pl.BlockSpec((rows, n_in), lambda j: (j, 0)),
                   pl.BlockSpec((rows, 2 * QK_WIDTH), lambda j: (j, 0))],
        out_shape=[jax.ShapeDtypeStruct((8, n), F32),
                   jax.ShapeDtypeStruct((d, n_in), BF16),
                   jax.ShapeDtypeStruct((d, 2 * QK_WIDTH), BF16)],
        compiler_params=_params(("parallel",)),
        name="mod",
    )(c8, w_mod, b_mod.reshape(1, n), w_in)


def _ctx_kernel(x_ref, mod_ref, nw_ref, wk_ref, wv_ref, k_ref, v_ref):
    x = x_ref[0]
    h = _mod_norm(x, nw_ref[...], mod_ref[0, 0:1, :], mod_ref[0, 1:2, :])
    hb = h.astype(BF16)
    k_ref[0] = _dot(hb, wk_ref[...]).astype(BF16)
    v_ref[0] = _dot(hb, wv_ref[...]).astype(BF16)


def _ctx_proj(ctx, mod3, norm_w, w_qk, w_in):
    b, lc, d = ctx.shape
    assert V_OFF % V_WIDTH == 0
    return pl.pallas_call(
        _ctx_kernel,
        grid=(b,),
        in_specs=[pl.BlockSpec((1, lc, d), lambda i: (i, 0, 0)),
                  pl.BlockSpec((1, N_MOD, d), lambda i: (2, 0, 0)),
                  _const_spec((1, d)),
                  pl.BlockSpec((d, QK_WIDTH), lambda i: (0, 1), pipeline_mode=pl.Buffered(1)),
                  pl.BlockSpec((d, V_WIDTH), lambda i: (0, V_OFF // V_WIDTH),
                               pipeline_mode=pl.Buffered(1))],
        out_specs=[pl.BlockSpec((1, lc, QK_WIDTH), lambda i: (i, 0, 0)),
                   pl.BlockSpec((1, lc, V_WIDTH), lambda i: (i, 0, 0))],
        out_shape=[jax.ShapeDtypeStruct((b, lc, QK_WIDTH), BF16),
                   jax.ShapeDtypeStruct((b, lc, V_WIDTH), BF16)],
        compiler_params=_params(("parallel",)),
        name="ctx_proj",
    )(ctx, mod3, norm_w, w_qk, w_in)


def _proj_kernel(x_ref, mod_ref, nw_ref, wqk_ref, w_ref, rtab_ref, ctab_ref, *refs, cast_scales):
    n_cast = len(cast_scales)
    q_ref, k_ref, v_ref, sg_ref, f_ref = refs[n_cast:n_cast + 5]
    _run_casts(refs[:n_cast] + refs[n_cast + 5:], cast_scales)
    lane = lax.broadcasted_iota(jnp.int32, (1, LANES), 1)
    by_row = (lane % (QK_DIM // 2)) < QK_DIM // 4

    def rope(t, trig, scale, out_ref, rows):
        for j in range(QK_WIDTH // LANES):
            tj = t[:, j * LANES:(j + 1) * LANES]
            r = tj * trig[0] + pltpu.roll(tj, LANES // 2, 1) * trig[1]
            out_ref[rows, j * LANES:(j + 1) * LANES] = (r * scale).astype(BF16)

    for r0 in range(0, x_ref.shape[0], PROJ_SUB):
        rows = slice(r0, r0 + PROJ_SUB)
        trig = []
        for cs in range(2):
            trig.append(jnp.concatenate(
                [jnp.where(by_row, rtab_ref[cs, g:g + 1, :], ctab_ref[cs])
                 for g in range(r0 // GRID_W, (r0 + PROJ_SUB) // GRID_W)], axis=0))
        x = x_ref[rows, :]
        h = _mod_norm(x, nw_ref[...], mod_ref[0, 0:1, :], mod_ref[0, 1:2, :])
        hb = h.astype(BF16)
        rope(_dot(hb, wqk_ref[:, :QK_WIDTH]), trig, QK_DIM ** -0.5, q_ref, rows)
        rope(_dot(hb, wqk_ref[:, QK_WIDTH:]), trig, 1.0, k_ref, rows)
        v_ref[rows, :] = _dot(hb, w_ref[:, V_OFF:G_OFF]).astype(BF16)
        g = _dot(hb, w_ref[:, G_OFF:F_OFF])
        hg = 0.5 * g
        sg_ref[rows, :] = (hg * jnp.tanh(hg) + hg).astype(BF16)
        f_ref[rows, :] = _dot(hb, w_ref[:, F_OFF:IN_COLS]).astype(BF16)


def _proj(x2, mod3, norm_w, w_qk, w_in, rtab, ctab, seq, to_cast, cast_scales):
    t, d = x2.shape
    tl = PROJ_TILE
    tpb = seq // tl
    tok = lambda w: pl.BlockSpec((tl, w), lambda i: (i, 0))
    cast_specs, cast_shapes = _cast_jobs(to_cast, t // tl)
    out = pl.pallas_call(
        functools.partial(_proj_kernel, cast_scales=tuple(cast_scales)),
        grid=(t // tl,),
        in_specs=[tok(d),
                  pl.BlockSpec((1, N_MOD, d), lambda i: (i // tpb, 0, 0)),
                  _const_spec((1, d)),
                  _const_spec(w_qk.shape), _const_spec(w_in.shape),
                  pl.BlockSpec((2, tl // GRID_W, LANES), lambda i: (0, i % tpb, 0)),
                  _const_spec(ctab.shape)] + cast_specs,
        out_specs=[tok(QK_WIDTH), tok(QK_WIDTH), tok(V_WIDTH), tok(V_WIDTH),
                   tok(F_WIDTH)] + cast_specs,
        out_shape=[jax.ShapeDtypeStruct((t, QK_WIDTH), BF16),
                   jax.ShapeDtypeStruct((t, QK_WIDTH), BF16),
                   jax.ShapeDtypeStruct((t, V_WIDTH), BF16),
                   jax.ShapeDtypeStruct((t, V_WIDTH), BF16),
                   jax.ShapeDtypeStruct((t, F_WIDTH), BF16)] + cast_shapes,
        compiler_params=_params(("arbitrary",)),
        name="proj",
    )(x2, mod3, norm_w, w_qk, w_in, rtab, ctab, *to_cast)
    return out[:5], out[5:]


def _retention_kernel(af_ref, ab_ref, q_ref, k_ref, v_ref, kc_ref, vc_ref,
                      tw_ref, w2_ref, cc_ref, y_ref,
                      o_ref, z_ref, ds_scr, st_scr, d_scr, tab_scr, z_scr):
    c = RET_CHUNK
    seq = q_ref.shape[1]
    lc = kc_ref.shape[1]
    n = seq // c
    lgf = [-jnp.exp(af_ref[hh]) for hh in range(2)]
    lgb = [-jnp.exp(ab_ref[hh]) for hh in range(2)]
    half = QK_DIM // 2

    lane_head = (lax.broadcasted_iota(jnp.int32, (1, LANES), 1) // half) % 2
    masks = [(lane_head == hh).astype(BF16) for hh in range(2)]
    lgf_lane = jnp.where(lane_head == 0, lgf[0], lgf[1])
    lgb_lane = jnp.where(lane_head == 0, lgb[0], lgb[1])
    row_head = (lax.broadcasted_iota(jnp.int32, (LANES, 1), 0) // half) % 2
    lgf_row = jnp.where(row_head == 0, lgf[0][:, 0:1], lgf[1][:, 0:1])
    lgb_row = jnp.where(row_head == 0, lgb[0][:, 0:1], lgb[1][:, 0:1])

    pos = lax.broadcasted_iota(jnp.int32, (c, LANES), 0).astype(F32)
    tab_scr[0] = jnp.exp(lgf_lane * (pos + 1.0))
    tab_scr[1] = jnp.exp(lgb_lane * (c - pos))
    diff = (lax.broadcasted_iota(jnp.int32, (c, c), 0)
            - lax.broadcasted_iota(jnp.int32, (c, c), 1)).astype(F32)
    for hh in range(2):
        d_scr[hh] = (jnp.where(diff >= 0, jnp.exp(lgf[hh][:, 0:1] * jnp.maximum(diff, 0.0)), 0.0)
                     + jnp.where(diff <= 0, jnp.exp(lgb[hh][:, 0:1] * jnp.maximum(-diff, 0.0)), 0.0))

    def k_decays(tokens):
        t = lax.broadcasted_iota(jnp.int32, (1, tokens), 1).astype(F32)
        return jnp.exp(lgf_row * (tokens - 1.0 - t)), jnp.exp(lgb_row * t)

    def state_increment(k_rows, v_rows, decays):
        kt = jnp.transpose(k_rows.astype(F32))
        lhs = jnp.concatenate([kt * decays[0], kt * decays[1]], axis=0).astype(BF16)
        return _dot(lhs, v_rows)

    kdec = k_decays(c)

    def incr(i, carry):
        rows = pl.ds(pl.multiple_of(i * c, c), c)
        ds_scr[i] = state_increment(k_ref[0, rows, :], v_ref[0, rows, :], kdec)
        return carry

    lax.fori_loop(0, n, incr, 0, unroll=RET_UNROLL)

    s0 = state_increment(kc_ref[0], vc_ref[0], k_decays(lc))
    col_head = lax.broadcasted_iota(jnp.int32, (1, 2 * V_DIM), 1) // V_DIM
    own = (row_head == col_head).astype(F32)
    decay_f = jnp.exp(lgf_row * c) * own
    decay_b = jnp.exp(lgb_row * c) * own

    def scan_f(i, s):
        st_scr[i, 0:LANES, :] = (s * own).astype(BF16)
        return decay_f * s + ds_scr[i, 0:LANES, :]

    lax.fori_loop(0, n, scan_f, s0[0:LANES])

    def scan_b(t, s):
        i = n - 1 - t
        st_scr[i, LANES:2 * LANES, :] = (s * own).astype(BF16)
        return decay_b * s + ds_scr[i, LANES:2 * LANES, :]

    lax.fori_loop(0, n, scan_b, s0[LANES:2 * LANES])

    def outputs(i):
        rows = pl.ds(pl.multiple_of(i * c, c), c)
        q = q_ref[0, rows, :]
        k = k_ref[0, rows, :]
        qf = q.astype(F32)
        qd = jnp.concatenate([qf * tab_scr[0], qf * tab_scr[1]], axis=1).astype(BF16)
        inter = _dot(qd, st_scr[i])
        for hh in range(2):
            cols = slice(hh * V_DIM, (hh + 1) * V_DIM)
            scores = lax.dot_general(q * masks[hh], k, (((1,), (1,)), ((), ())),
                                     preferred_element_type=F32)
            o = _dot((scores * d_scr[hh]).astype(BF16), v_ref[0, rows, cols]) + inter[:, cols]
            o_ref[0, rows, cols] = o.astype(BF16)

    per_k1 = n // DFT_ROWS
    cc = cc_ref[...].astype(BF16)

    def outputs_and_dft(j, carry):
        _dft_second_stage(j, tw_ref, w2_ref, cc, y_ref, z_scr)
        for u in range(per_k1):
            outputs(j * per_k1 + u)
        return carry

    lax.fori_loop(0, DFT_ROWS, outputs_and_dft, 0, unroll=RET_UNROLL // per_k1)
    _dft_emit(z_scr, z_ref)


def _retention_and_dft(a_f, a_b, q, k, v, kc, vc, y, tw, w2, w_c):
    b, seq, _ = v.shape
    lc = kc.shape[1]
    c = RET_CHUNK
    n = seq // c
    r = DFT_ROWS
    _, _, n1, n2, w = y.shape
    assert n % r == 0 and n1 // r == HEADS // 2
    dec = pl.BlockSpec((2, 1, LANES), lambda i, p: (p, 0, 0))
    qk = lambda rows: pl.BlockSpec((1, rows, LANES), lambda i, p: (i, 0, p))
    vv = lambda rows: pl.BlockSpec((1, rows, 2 * V_DIM), lambda i, p: (i, 0, p))
    return pl.pallas_call(
        _retention_kernel,
        grid=(b, HEADS // 2),
        in_specs=[dec, dec, qk(seq), qk(seq), vv(seq), qk(lc), vv(lc),
                  pl.BlockSpec((r, 2, n2), lambda i, p: (p, 0, 0)),
                  _const_spec((2, n2, n2)),
                  _const_spec((2 * F_GROUP_DIM, F_GROUP_DIM)),
                  pl.BlockSpec((1, 2, r, n2, w), lambda i, p: (i, 0, p, 0, 0))],
        out_specs=[vv(seq), pl.BlockSpec((1, n2, r, w), lambda i, p: (i, 0, p, 0))],
        out_shape=[jax.ShapeDtypeStruct((b, seq, V_WIDTH), BF16),
                   jax.ShapeDtypeStruct((b, n2, n1, w), BF16)],
        scratch_shapes=[pltpu.VMEM((n, 2 * LANES, 2 * V_DIM), F32),
                        pltpu.VMEM((n, 2 * LANES, 2 * V_DIM), BF16),
                        pltpu.VMEM((2, c, c), F32),
                        pltpu.VMEM((2, c, LANES), F32),
                        pltpu.VMEM((F_GROUPS, n2 * r, F_GROUP_DIM), F32)],
        compiler_params=_params(("parallel", "parallel")),
        name="retention_dft",
    )(a_f, a_b, q, k, v, kc, vc, tw, w2, w_c, y)


def _dft_a_kernel(w_ref, x_ref, y_ref):
    _, n1, rows, w = x_ref.shape
    r = DFT_ROWS
    nh = n1 // 2 + 1
    for r0 in range(0, rows, r):
        x = x_ref[0, :, r0:r0 + r, :].reshape(n1 * r, w)
        y = _dot(w_ref[...], x).reshape(2, nh, r, w).astype(BF16)
        y_ref[0, :, 0:nh, r0:r0 + r, :] = y
        for k1 in range(1, n1 // 2):
            y_ref[0, 0, n1 - k1, r0:r0 + r, :] = y[0, k1]
            y_ref[0, 1, n1 - k1, r0:r0 + r, :] = -y[1, k1]


def _dft_second_stage(j, tw_ref, w2_ref, cc, y_ref, z_scr):
    n2 = y_ref.shape[3]
    w2c = w2_ref[0]
    w2s = w2_ref[1]
    tc = tw_ref[j, 0:1, :]
    ts = tw_ref[j, 1:2, :]
    ec = w2c * tc - w2s * ts
    es = w2s * tc + w2c * ts
    m = jnp.concatenate([jnp.concatenate([ec, -es], axis=1),
                         jnp.concatenate([es, ec], axis=1)], axis=0).astype(BF16)
    y = jnp.concatenate([y_ref[0, 0, j], y_ref[0, 1, j]], axis=0)
    zz = _dot(m, y)
    for g in range(F_GROUPS):
        cols = slice(g * F_GROUP_DIM, (g + 1) * F_GROUP_DIM)
        zcs = jnp.concatenate([zz[:n2, cols], zz[n2:, cols]], axis=1).astype(BF16)
        z_scr[g, pl.ds(j, n2, stride=DFT_ROWS), :] = _dot(zcs, cc)


def _dft_emit(z_scr, z_ref):
    n2 = z_ref.shape[1]
    z = jnp.concatenate([z_scr[g] for g in range(F_GROUPS)], axis=1)
    z_ref[0] = z.reshape(n2, DFT_ROWS, F_WIDTH).astype(BF16)


def _dft_tables(seq):
    n1 = DFT_N1
    n2 = seq // n1

    def cs(num, den):
        ang = 2.0 * np.pi * (num % den) / den
        return np.cos(ang), np.sin(ang)

    a = np.arange(n1)
    w_a = np.concatenate(cs(a[:n1 // 2 + 1, None] * a[None, :], n1), axis=0)
    m = np.arange(n2)
    tw = np.stack(cs(a[:, None] * m[None, :], seq), axis=1)
    w2 = np.stack(cs(m[:, None] * m[None, :], n2), axis=0)
    ch = np.arange(F_GROUP_DIM)
    cc, sc = cs(ch[:, None] * ch[None, :], F_GROUP_DIM)
    scale = 1.0 / np.sqrt(seq * F_GROUP_DIM)
    w_c = np.concatenate([cc, -sc], axis=0) * scale
    return [jnp.asarray(t, dtype=F32) for t in (w_a, tw, w2, w_c)]


def _dft_first_stage(f):
    b, seq, w = f.shape
    n1 = DFT_N1
    n2 = seq // n1
    w_a, tw, w2, w_c = _dft_tables(seq)
    r = DFT_ROWS
    spread = (jnp.arange(n1 * r)[None, :] // r == jnp.arange(n1)[:, None]).astype(F32)
    w_rep = jnp.dot(w_a, spread, precision=lax.Precision.HIGHEST)
    same_r = jnp.arange(w_a.shape[0] * r)[:, None] % r == jnp.arange(n1 * r)[None, :] % r
    w_a = jnp.where(same_r, jnp.repeat(w_rep, r, axis=0), 0.0).astype(BF16)
    y = pl.pallas_call(
        _dft_a_kernel,
        grid=(b, n2 // DFT_A_ROWS),
        in_specs=[_const_spec(w_a.shape),
                  pl.BlockSpec((1, n1, DFT_A_ROWS, w), lambda i, j: (i, 0, j, 0))],
        out_specs=pl.BlockSpec((1, 2, n1, DFT_A_ROWS, w), lambda i, j: (i, 0, 0, j, 0)),
        out_shape=jax.ShapeDtypeStruct((b, 2, n1, n2, w), BF16),
        compiler_params=_params(("parallel", "parallel")),
        name="dft_a",
    )(w_a, f.reshape(b, n1, n2, w))
    return y, tw, w2, w_c


def _merge_kernel(x_ref, mod_ref, nw_ref, n2w_ref, ro_ref, sg_ref, z_ref, wbg_ref, bbg_ref,
                  wro_ref, wfo_ref, wout_ref, *refs, n_cast):
    o_ref, h2_ref = refs[n_cast:n_cast + 2]
    _run_casts(refs[:n_cast] + refs[n_cast + 2:])
    for r0 in range(0, x_ref.shape[0], MERGE_SUB):
        rows = slice(r0, r0 + MERGE_SUB)
        x = x_ref[rows, :]
        h = _mod_norm(x, nw_ref[...], mod_ref[0, 0:1, :], mod_ref[0, 1:2, :])
        t = jnp.tanh(_dot(h.astype(BF16), wbg_ref[...]) + 0.5 * bbg_ref[...])
        gated = []
        for hd in range(HEADS):
            cols = slice(hd * V_DIM, (hd + 1) * V_DIM)
            o = ro_ref[rows, cols].astype(F32)
            oc = o - jnp.mean(o, axis=-1, keepdims=True)
            var = jnp.mean(oc * oc, axis=-1, keepdims=True)
            gated.append((oc * lax.rsqrt(var + EPS) * sg_ref[rows, cols].astype(F32)).astype(BF16))
        ret_d = _dot(jnp.concatenate(gated, axis=1), wro_ref[...])
        four_d = _dot(z_ref[rows, :], wfo_ref[...])
        m2 = (t[:, :D_MODEL] * ret_d + ret_d) + (t[:, D_MODEL:] * four_d + four_d)
        y2 = _dot(m2.astype(BF16), wout_ref[...])
        x1 = x + (0.5 * mod_ref[0, 2:3, :]) * y2
        o_ref[rows, :] = x1
        h2 = _mod_norm(x1, n2w_ref[...], mod_ref[0, 3:4, :], mod_ref[0, 4:5, :])
        h2_ref[rows, :] = h2.astype(BF16)


def _merge(x2, mod3, norm_w, norm2_w, ro, sg, z, w_bg, b_bg, w_ro, w_fo, w_out, seq, to_cast):
    t, d = x2.shape
    tl = MERGE_TILE
    tpb = seq // tl
    tok = lambda w: pl.BlockSpec((tl, w), lambda i: (i, 0))
    cast_specs, cast_shapes = _cast_jobs(to_cast, t // tl)
    out = pl.pallas_call(
        functools.partial(_merge_kernel, n_cast=len(to_cast)),
        grid=(t // tl,),
        in_specs=[tok(d),
                  pl.BlockSpec((1, N_MOD, d), lambda i: (i // tpb, 0, 0)),
                  _const_spec((1, d)), _const_spec((1, d)),
                  tok(V_WIDTH), tok(V_WIDTH), tok(F_WIDTH),
                  _const_spec(w_bg.shape), _const_spec((1, 2 * d)),
                  _const_spec(w_ro.shape), _const_spec(w_fo.shape),
                  _const_spec(w_out.shape)] + cast_specs,
        out_specs=[tok(d), tok(d)] + cast_specs,
        out_shape=[jax.ShapeDtypeStruct((t, d), F32),
                   jax.ShapeDtypeStruct((t, d), BF16)] + cast_shapes,
        compiler_params=_params(("arbitrary",)),
        name="merge",
    )(x2, mod3, norm_w, norm2_w, ro, sg, z, w_bg, b_bg, w_ro, w_fo, w_out, *to_cast)
    return out[:2], out[2:]


HALO = 8
FFN_LEAD = 16
FFN_SUB = 256


def _two_gelu_tanh(x):
    c1 = np.sqrt(2.0 / np.pi)
    return x + x * jnp.tanh(x * (c1 + (c1 * 0.044715) * (x * x)))


def _ffn_kernel(xm_ref, hm_ref, xp_ref, xn_ref, mod_ref, nw_ref, wu_ref, cw_ref, cb_ref,
                wd_ref, fnw_ref, o_ref, h_scr, u_scr, act_scr, y_scr, *, tiles_per_seq):
    tl = xm_ref.shape[0]
    sub = FFN_SUB
    half = sub // 2
    nc = FFN_CHUNK
    i = pl.program_id(0)
    keep_prev = ((i % tiles_per_seq) != 0).astype(F32)
    keep_next = ((i % tiles_per_seq) != tiles_per_seq - 1).astype(F32)

    def pre(x):
        return _mod_norm(x, nw_ref[...], mod_ref[0, 3:4, :], mod_ref[0, 4:5, :])

    pad = jnp.zeros((FFN_LEAD - HALO, xm_ref.shape[1]), F32)
    h_scr[0:FFN_LEAD] = jnp.concatenate([pad, pre(xp_ref[...]) * keep_prev], axis=0).astype(BF16)
    h_scr[FFN_LEAD:FFN_LEAD + tl] = hm_ref[...]
    h_scr[FFN_LEAD + tl:] = jnp.concatenate([pre(xn_ref[...]) * keep_next, pad], axis=0).astype(BF16)

    def project(j):
        hb = h_scr[...]
        for part in range(2):
            lo = part * FFN_DIM + j * nc
            u = _dot(hb, wu_ref[:, lo:lo + nc])
            for s in range(nc // LANES):
                u_scr[j % 2, part, s] = u[:, s * LANES:(s + 1) * LANES]

    def conv(j, part, s, blk, scale):
        lo = part * FFN_DIM + j * nc + s * LANES
        w = cw_ref[:, lo:lo + LANES] * scale
        bias = cb_ref[:, lo:lo + LANES] * scale
        first = FFN_LEAD + blk * sub
        rows = lambda start: u_scr[j % 2, part, s, pl.ds(start, half, stride=2), :]
        before, even, odd, after = rows(first - 1), rows(first), rows(first + 1), rows(first + 2)
        return (before * w[0:1] + even * w[1:2] + odd * w[2:3] + bias,
                even * w[0:1] + odd * w[1:2] + after * w[2:3] + bias)

    def activate(j):
        for s in range(nc // LANES):
            cols = slice(j * nc + s * LANES, j * nc + (s + 1) * LANES)
            for blk in range(tl // sub):
                gate = conv(j, 0, s, blk, 1.0)
                val = conv(j, 1, s, blk, 0.5)
                for par in range(2):
                    r0 = blk * sub + par * half
                    act_scr[r0:r0 + half, cols] = (_two_gelu_tanh(gate[par]) * val[par]).astype(BF16)

    project(0)
    for j in range(N_FFN_CHUNKS):
        if j + 1 < N_FFN_CHUNKS:
            project(j + 1)
        activate(j)
    for blk in range(tl // sub):
        r0 = blk * sub
        y = _dot(act_scr[r0:r0 + sub, :], wd_ref[...])
        for s in range(y_scr.shape[0]):
            for par in range(2):
                y_scr[s, pl.ds(par, half, stride=2), :] = (
                    y[par * half:(par + 1) * half, s * LANES:(s + 1) * LANES])
        y = jnp.concatenate([y_scr[s] for s in range(y_scr.shape[0])], axis=1)
        x2 = xm_ref[r0:r0 + sub, :] + mod_ref[0, 5:6, :] * y
        o_ref[r0:r0 + sub, :] = _rms_norm(x2, fnw_ref[...])


def _ffn(x1, h2, mod3, norm_w, w_u, cw, cb, w_d, fnorm_w, seq):
    t, d = x1.shape
    tl = FFN_TILE
    tpb = seq // tl
    hb = tl // HALO
    last = t // HALO - 1
    rows = FFN_LEAD + tl + FFN_LEAD
    return pl.pallas_call(
        functools.partial(_ffn_kernel, tiles_per_seq=tpb),
        grid=(t // tl,),
        in_specs=[pl.BlockSpec((tl, d), lambda i: (i, 0)),
                  pl.BlockSpec((tl, d), lambda i: (i, 0)),
                  pl.BlockSpec((HALO, d), lambda i: (jnp.maximum(i * hb - 1, 0), 0)),
                  pl.BlockSpec((HALO, d), lambda i: (jnp.minimum((i + 1) * hb, last), 0)),
                  pl.BlockSpec((1, N_MOD, d), lambda i: (i // tpb, 0, 0)),
                  _const_spec((1, d)),
                  _const_spec(w_u.shape), _const_spec(cw.shape), _const_spec(cb.shape),
                  _const_spec(w_d.shape), _const_spec((1, d))],
        out_specs=pl.BlockSpec((tl, d), lambda i: (i, 0)),
        out_shape=jax.ShapeDtypeStruct((t, d), F32),
        scratch_shapes=[pltpu.VMEM((rows, d), BF16),
                        pltpu.VMEM((2, 2, FFN_CHUNK // LANES, rows, LANES), F32),
                        pltpu.VMEM((tl, FFN_DIM), BF16),
                        pltpu.VMEM((d // LANES, FFN_SUB, LANES), F32)],
        compiler_params=_params(("parallel",)),
        name="ffn",
    )(x1, h2, x1, x1, mod3, norm_w, w_u, cw, cb, w_d, fnorm_w)


def _rope_tables(seq):
    n_freq = QK_DIM // 4
    lane = jnp.arange(LANES)
    inv = ROPE_BASE ** (-(lane % n_freq).astype(F32) / n_freq)
    sign = jnp.where(lane < LANES // 2, -1.0, 1.0).astype(F32)

    def table(n):
        ang = jnp.arange(n, dtype=F32)[:, None] * inv[None, :]
        return jnp.stack([jnp.cos(ang), jnp.sin(ang) * sign])

    return table(seq // GRID_W), table(GRID_W)


def kernel(x, c, ctx, c_ctx, w_mod, b_mod, norm1_w, w_in, ret_decay_f, ret_decay_b,
           w_ret_out, w_four_out, w_branch_gate, b_branch_gate, w_out, norm2_w,
           w_up, conv_w, conv_b, w_down, final_norm_w):
    assert w_mod.shape[0] == 1, "single-layer block"
    b, seq, d = x.shape
    t = b * seq
    assert b == 2 and seq % RET_CHUNK == 0
    assert seq % FFN_TILE == 0 and seq % MERGE_TILE == 0 and seq % PROJ_TILE == 0

    c8 = jnp.concatenate([c, c_ctx[None, :], jnp.zeros((8 - b - 1, d), F32)], axis=0)
    mod, w_in_b, w_qk = _modulation(c8, w_mod[0], b_mod[0], w_in[0])
    mod3 = mod.reshape(8, N_MOD, d)
    n1w = norm1_w[0].reshape(1, d)

    kc, vc = _ctx_proj(ctx, mod3, n1w, w_qk, w_in_b)
    rtab, ctab = _rope_tables(seq)
    x2 = x.reshape(t, d)
    (q, k, v, sg, f), (w_bg, w_ro, w_fo, w_o) = _proj(
        x2, mod3, n1w, w_qk, w_in_b, rtab, ctab, seq,
        [w_branch_gate[0], w_ret_out[0], w_four_out[0], w_out[0]], (0.5, 1.0, 1.0, 1.0))

    a_f = jnp.broadcast_to(ret_decay_f[0][:, None, None], (HEADS, 1, LANES))
    a_b = jnp.broadcast_to(ret_decay_b[0][:, None, None], (HEADS, 1, LANES))
    y, tw, w2, w_c = _dft_first_stage(f.reshape(b, seq, F_WIDTH))
    ro, z = _retention_and_dft(a_f, a_b, q.reshape(b, seq, QK_WIDTH), k.reshape(b, seq, QK_WIDTH),
                               v.reshape(b, seq, V_WIDTH), kc, vc, y, tw, w2, w_c)

    n2w = norm2_w[0].reshape(1, d)
    (x1, h2), (w_u, w_d) = _merge(
        x2, mod3, n1w, n2w, ro.reshape(t, V_WIDTH), sg, z.reshape(t, F_WIDTH),
        w_bg, b_branch_gate[0].reshape(1, 2 * d), w_ro, w_fo, w_o, seq, [w_up[0], w_down[0]])

    out = _ffn(x1, h2, mod3, n2w, w_u, conv_w[0], conv_b[0].reshape(1, 2 * FFN_DIM), w_d,
               final_norm_w.reshape(1, d), seq)
    return out.reshape(b, seq, d)
```

```python
import functools

import numpy as np
import jax
import jax.numpy as jnp
from jax import lax
from jax.experimental import pallas as pl
from jax.experimental.pallas import tpu as pltpu

F32 = jnp.float32
BF16 = jnp.bfloat16

D_MODEL = 1024
GRID_W = 64
HEADS = 8
QK_DIM = 64
V_DIM = 128
QK_WIDTH = HEADS * QK_DIM
V_WIDTH = HEADS * V_DIM
ROPE_BASE = 10000.0
F_GROUPS = 4
F_GROUP_DIM = 128
F_WIDTH = F_GROUPS * F_GROUP_DIM
K_OFF = QK_WIDTH
V_OFF = K_OFF + QK_WIDTH
G_OFF = V_OFF + V_WIDTH
F_OFF = G_OFF + V_WIDTH
IN_COLS = F_OFF + F_WIDTH
FFN_DIM = 2816
N_MOD = 6
EPS = 1e-6

LANES = 128
RET_CHUNK = 256
RET_UNROLL = 16
FFN_CHUNK = 256
N_FFN_CHUNKS = FFN_DIM // FFN_CHUNK
FFN_TILE = 512
MERGE_TILE = 1024
MERGE_SUB = 512
PROJ_TILE = 1024
PROJ_SUB = 256
MOD_STEPS = 2
DFT_N1 = 64
BF16_ROWS = 16
DFT_ROWS = BF16_ROWS
DFT_A_ROWS = 2 * DFT_ROWS
VMEM_LIMIT = 56 * 1024 * 1024


def _params(sem):
    return pltpu.CompilerParams(dimension_semantics=sem, vmem_limit_bytes=VMEM_LIMIT)


def _dot(a, b):
    return jnp.dot(a, b, preferred_element_type=F32)


def _rms_norm(x, w):
    return x * lax.rsqrt(jnp.mean(x * x, axis=-1, keepdims=True) + EPS) * w


def _mod_norm(x, w, shift, scale):
    return x * lax.rsqrt(jnp.mean(x * x, axis=-1, keepdims=True) + EPS) * (w * (1.0 + scale)) + shift


def _const_spec(shape):
    zeros = (0,) * len(shape)
    return pl.BlockSpec(shape, lambda *_: zeros, pipeline_mode=pl.Buffered(1))


def _cast_jobs(weights, steps):
    specs, shapes = [], []
    for w in weights:
        n_rows = w.shape[0]
        rows = -(-n_rows // steps)
        while rows % BF16_ROWS or n_rows % rows:
            rows += 1
        last = n_rows // rows - 1
        specs.append(pl.BlockSpec((rows, w.shape[1]), lambda i, last=last: (jnp.minimum(i, last), 0)))
        shapes.append(jax.ShapeDtypeStruct(w.shape, BF16))
    return specs, shapes


def _run_casts(refs, scales=None):
    n = len(refs) // 2
    for j, (src, dst) in enumerate(zip(refs[:n], refs[n:])):
        scale = 1.0 if scales is None else scales[j]
        dst[...] = (src[...] if scale == 1.0 else src[...] * scale).astype(BF16)


def _mod_kernel(c_ref, w_ref, b_ref, win_ref, o_ref, winb_ref, wqk_ref):
    c = c_ref[...]
    s = c * jax.nn.sigmoid(c)
    o_ref[...] = _dot(s.astype(BF16), w_ref[...].astype(BF16)) + b_ref[...]
    wb = win_ref[...].astype(BF16)
    winb_ref[...] = wb
    n_qk = 2 * QK_WIDTH
    src = lax.broadcasted_iota(jnp.int32, (n_qk, n_qk), 0)
    dst = lax.broadcasted_iota(jnp.int32, (n_qk, n_qk), 1)
    select = (src == _qk_source_column(dst)).astype(BF16)
    wqk_ref[...] = _dot(wb[:, :n_qk], select).astype(BF16)


def _qk_source_column(col):
    half = QK_DIM // 2
    lane = col % LANES
    head = 2 * ((col % QK_WIDTH) // LANES) + (lane // half) % 2
    return (col // QK_WIDTH) * QK_WIDTH + head * QK_DIM + (lane // (2 * half)) * half + lane % half


def _modulation(c8, w_mod, b_mod, w_in):
    n = w_mod.shape[1]
    steps = MOD_STEPS
    tn = n // steps
    d, n_in = w_in.shape
    rows = d // steps
    return pl.pallas_call(
        _mod_kernel,
        grid=(steps,),
        in_specs=[_const_spec((8, D_MODEL)),
                  pl.BlockSpec((D_MODEL, tn), lambda j: (0, j)),
                  pl.BlockSpec((1, tn), lambda j: (0, j)),
                  pl.BlockSpec((rows, n_in), lambda j: (j, 0))],
        out_specs=[pl.BlockSpec((8, tn), lambda j: (0, j)),
                   pl.BlockSpec((rows, n_in), lambda j: (j, 0)),
                   pl.BlockSpec((rows, 2 * QK_WIDTH), lambda j: (j, 0))],
        out_shape=[jax.ShapeDtypeStruct((8, n), F32),
                   jax.ShapeDtypeStruct((d, n_in), BF16),
                   jax.ShapeDtypeStruct((d, 2 * QK_WIDTH), BF16)],
        compiler_params=_params(("parallel",)),
        name="mod",
    )(c8, w_mod, b_mod.reshape(1, n), w_in)


def _ctx_kernel(x_ref, mod_ref, nw_ref, wk_ref, wv_ref, k_ref, v_ref):
    x = x_ref[0]
    h = _mod_norm(x, nw_ref[...], mod_ref[0, 0:1, :], mod_ref[0, 1:2, :])
    hb = h.astype(BF16)
    k_ref[0] = _dot(hb, wk_ref[...]).astype(BF16)
    v_ref[0] = _dot(hb, wv_ref[...]).astype(BF16)


def _ctx_proj(ctx, mod3, norm_w, w_qk, w_in):
    b, lc, d = ctx.shape
    assert V_OFF % V_WIDTH == 0
    return pl.pallas_call(
        _ctx_kernel,
        grid=(b,),
        in_specs=[pl.BlockSpec((1, lc, d), lambda i: (i, 0, 0)),
                  pl.BlockSpec((1, N_MOD, d), lambda i: (2, 0, 0)),
                  _const_spec((1, d)),
                  pl.BlockSpec((d, QK_WIDTH), lambda i: (0, 1), pipeline_mode=pl.Buffered(1)),
                  pl.BlockSpec((d, V_WIDTH), lambda i: (0, V_OFF // V_WIDTH),
                               pipeline_mode=pl.Buffered(1))],
        out_specs=[pl.BlockSpec((1, lc, QK_WIDTH), lambda i: (i, 0, 0)),
                   pl.BlockSpec((1, lc, V_WIDTH), lambda i: (i, 0, 0))],
        out_shape=[jax.ShapeDtypeStruct((b, lc, QK_WIDTH), BF16),
                   jax.ShapeDtypeStruct((b, lc, V_WIDTH), BF16)],
        compiler_params=_params(("parallel",)),
        name="ctx_proj",
    )(ctx, mod3, norm_w, w_qk, w_in)


def _proj_kernel(x_ref, mod_ref, nw_ref, wqk_ref, w_ref, rtab_ref, ctab_ref, *refs, cast_scales):
    n_cast = len(cast_scales)
    q_ref, k_ref, v_ref, sg_ref, f_ref = refs[n_cast:n_cast + 5]
    _run_casts(refs[:n_cast] + refs[n_cast + 5:], cast_scales)
    lane = lax.broadcasted_iota(jnp.int32, (1, LANES), 1)
    by_row = (lane % (QK_DIM // 2)) < QK_DIM // 4

    def rope(t, trig, scale, out_ref, rows):
        for j in range(QK_WIDTH // LANES):
            tj = t[:, j * LANES:(j + 1) * LANES]
            r = tj * trig[0] + pltpu.roll(tj, LANES // 2, 1) * trig[1]
            out_ref[rows, j * LANES:(j + 1) * LANES] = (r * scale).astype(BF16)

    for r0 in range(0, x_ref.shape[0], PROJ_SUB):
        rows = slice(r0, r0 + PROJ_SUB)
        trig = []
        for cs in range(2):
            trig.append(jnp.concatenate(
                [jnp.where(by_row, rtab_ref[cs, g:g + 1, :], ctab_ref[cs])
                 for g in range(r0 // GRID_W, (r0 + PROJ_SUB) // GRID_W)], axis=0))
        x = x_ref[rows, :]
        h = _mod_norm(x, nw_ref[...], mod_ref[0, 0:1, :], mod_ref[0, 1:2, :])
        hb = h.astype(BF16)
        rope(_dot(hb, wqk_ref[:, :QK_WIDTH]), trig, QK_DIM ** -0.5, q_ref, rows)
        rope(_dot(hb, wqk_ref[:, QK_WIDTH:]), trig, 1.0, k_ref, rows)
        v_ref[rows, :] = _dot(hb, w_ref[:, V_OFF:G_OFF]).astype(BF16)
        g = _dot(hb, w_ref[:, G_OFF:F_OFF])
        hg = 0.5 * g
        sg_ref[rows, :] = (hg * jnp.tanh(hg) + hg).astype(BF16)
        f_ref[rows, :] = _dot(hb, w_ref[:, F_OFF:IN_COLS]).astype(BF16)


def _proj(x2, mod3, norm_w, w_qk, w_in, rtab, ctab, seq, to_cast, cast_scales):
    t, d = x2.shape
    tl = PROJ_TILE
    tpb = seq // tl
    tok = lambda w: pl.BlockSpec((tl, w), lambda i: (i, 0))
    cast_specs, cast_shapes = _cast_jobs(to_cast, t // tl)
    out = pl.pallas_call(
        functools.partial(_proj_kernel, cast_scales=tuple(cast_scales)),
        grid=(t // tl,),
        in_specs=[tok(d),
                  pl.BlockSpec((1, N_MOD, d), lambda i: (i // tpb, 0, 0)),
                  _const_spec((1, d)),
                  _const_spec(w_qk.shape), _const_spec(w_in.shape),
                  pl.BlockSpec((2, tl // GRID_W, LANES), lambda i: (0, i % tpb, 0)),
                  _const_spec(ctab.shape)] + cast_specs,
        out_specs=[tok(QK_WIDTH), tok(QK_WIDTH), tok(V_WIDTH), tok(V_WIDTH),
                   tok(F_WIDTH)] + cast_specs,
        out_shape=[jax.ShapeDtypeStruct((t, QK_WIDTH), BF16),
                   jax.ShapeDtypeStruct((t, QK_WIDTH), BF16),
                   jax.ShapeDtypeStruct((t, V_WIDTH), BF16),
                   jax.ShapeDtypeStruct((t, V_WIDTH), BF16),
                   jax.ShapeDtypeStruct((t, F_WIDTH), BF16)] + cast_shapes,
        compiler_params=_params(("arbitrary",)),
        name="proj",
    )(x2, mod3, norm_w, w_qk, w_in, rtab, ctab, *to_cast)
    return out[:5], out[5:]


def _retention_kernel(af_ref, ab_ref, q_ref, k_ref, v_ref, kc_ref, vc_ref,
                      tw_ref, w2_ref, cc_ref, y_ref,
                      o_ref, z_ref, ds_scr, st_scr, d_scr, tab_scr, z_scr):
    c = RET_CHUNK
    seq = q_ref.shape[1]
    lc = kc_ref.shape[1]
    n = seq // c
    lgf = [-jnp.exp(af_ref[hh]) for hh in range(2)]
    lgb = [-jnp.exp(ab_ref[hh]) for hh in range(2)]
    half = QK_DIM // 2

    lane_head = (lax.broadcasted_iota(jnp.int32, (1, LANES), 1) // half) % 2
    masks = [(lane_head == hh).astype(BF16) for hh in range(2)]
    v_head = lax.broadcasted_iota(jnp.int32, (1, 2 * V_DIM), 1) // V_DIM
    vmasks = [(v_head == hh).astype(BF16) for hh in range(2)]
    lgf_lane = jnp.where(lane_head == 0, lgf[0], lgf[1])
    lgb_lane = jnp.where(lane_head == 0, lgb[0], lgb[1])
    row_head = (lax.broadcasted_iota(jnp.int32, (LANES, 1), 0) // half) % 2
    lgf_row = jnp.where(row_head == 0, lgf[0][:, 0:1], lgf[1][:, 0:1])
    lgb_row = jnp.where(row_head == 0, lgb[0][:, 0:1], lgb[1][:, 0:1])

    pos = lax.broadcasted_iota(jnp.int32, (c, LANES), 0).astype(F32)
    tab_scr[0] = jnp.exp(lgf_lane * (pos + 1.0))
    tab_scr[1] = jnp.exp(lgb_lane * (c - pos))
    diff = (lax.broadcasted_iota(jnp.int32, (c, c), 0)
            - lax.broadcasted_iota(jnp.int32, (c, c), 1)).astype(F32)
    for hh in range(2):
        d_scr[hh] = (jnp.where(diff >= 0, jnp.exp(lgf[hh][:, 0:1] * jnp.maximum(diff, 0.0)), 0.0)
                     + jnp.where(diff <= 0, jnp.exp(lgb[hh][:, 0:1] * jnp.maximum(-diff, 0.0)), 0.0))

    def k_decays(tokens):
        t = lax.broadcasted_iota(jnp.int32, (1, tokens), 1).astype(F32)
        return jnp.exp(lgf_row * (tokens - 1.0 - t)), jnp.exp(lgb_row * t)

    def state_increment(k_rows, v_rows, decays):
        kt = jnp.transpose(k_rows.astype(F32))
        lhs = jnp.concatenate([kt * decays[0], kt * decays[1]], axis=0).astype(BF16)
        return _dot(lhs, v_rows)

    kdec = k_decays(c)

    def incr(i, carry):
        rows = pl.ds(pl.multiple_of(i * c, c), c)
        ds_scr[i] = state_increment(k_ref[0, rows, :], v_ref[0, rows, :], kdec)
        return carry

    lax.fori_loop(0, n, incr, 0, unroll=RET_UNROLL)

    s0 = state_increment(kc_ref[0], vc_ref[0], k_decays(lc))
    col_head = lax.broadcasted_iota(jnp.int32, (1, 2 * V_DIM), 1) // V_DIM
    own = (row_head == col_head).astype(F32)
    decay_f = jnp.exp(lgf_row * c) * own
    decay_b = jnp.exp(lgb_row * c) * own

    def scan_f(i, s):
        st_scr[i, 0:LANES, :] = (s * own).astype(BF16)
        return decay_f * s + ds_scr[i, 0:LANES, :]

    lax.fori_loop(0, n, scan_f, s0[0:LANES])

    def scan_b(t, s):
        i = n - 1 - t
        st_scr[i, LANES:2 * LANES, :] = (s * own).astype(BF16)
        return decay_b * s + ds_scr[i, LANES:2 * LANES, :]

    lax.fori_loop(0, n, scan_b, s0[LANES:2 * LANES])

    def outputs(i):
        rows = pl.ds(pl.multiple_of(i * c, c), c)
        q = q_ref[0, rows, :]
        k = k_ref[0, rows, :]
        qf = q.astype(F32)
        qd = jnp.concatenate([qf * tab_scr[0], qf * tab_scr[1]], axis=1).astype(BF16)
        inter = _dot(qd, st_scr[i])
        k2 = jnp.concatenate([k * masks[0], k * masks[1]], axis=0)
        scores = lax.dot_general(q, k2, (((1,), (1,)), ((), ())), preferred_element_type=F32)
        p = (scores * jnp.concatenate([d_scr[0], d_scr[1]], axis=1)).astype(BF16)
        v = v_ref[0, rows, :]
        v2 = jnp.concatenate([v * vmasks[0], v * vmasks[1]], axis=0)
        o_ref[0, rows, :] = (_dot(p, v2) + inter).astype(BF16)

    per_k1 = n // DFT_ROWS
    cc = cc_ref[...].astype(BF16)

    def outputs_and_dft(j, carry):
        for u in range(per_k1):
            outputs(j * per_k1 + u)
        _dft_second_stage(j, tw_ref, w2_ref, cc, y_ref, z_scr)
        return carry

    lax.fori_loop(0, DFT_ROWS, outputs_and_dft, 0, unroll=RET_UNROLL // per_k1)
    _dft_emit(z_scr, z_ref)


def _retention_and_dft(a_f, a_b, q, k, v, kc, vc, y, tw, w2, w_c):
    b, seq, _ = v.shape
    lc = kc.shape[1]
    c = RET_CHUNK
    n = seq // c
    r = DFT_ROWS
    _, _, n1, n2, w = y.shape
    assert n % r == 0 and n1 // r == HEADS // 2
    dec = pl.BlockSpec((2, 1, LANES), lambda i, p: (p, 0, 0))
    qk = lambda rows: pl.BlockSpec((1, rows, LANES), lambda i, p: (i, 0, p))
    vv = lambda rows: pl.BlockSpec((1, rows, 2 * V_DIM), lambda i, p: (i, 0, p))
    return pl.pallas_call(
        _retention_kernel,
        grid=(b, HEADS // 2),
        in_specs=[dec, dec, qk(seq), qk(seq), vv(seq), qk(lc), vv(lc),
                  pl.BlockSpec((r, 2, n2), lambda i, p: (p, 0, 0)),
                  _const_spec((2, n2, n2)),
                  _const_spec((2 * F_GROUP_DIM, F_GROUP_DIM)),
                  pl.BlockSpec((1, 2, r, n2, w), lambda i, p: (i, 0, p, 0, 0))],
        out_specs=[vv(seq), pl.BlockSpec((1, n2, r, w), lambda i, p: (i, 0, p, 0))],
        out_shape=[jax.ShapeDtypeStruct((b, seq, V_WIDTH), BF16),
                   jax.ShapeDtypeStruct((b, n2, n1, w), BF16)],
        scratch_shapes=[pltpu.VMEM((n, 2 * LANES, 2 * V_DIM), F32),
                        pltpu.VMEM((n, 2 * LANES, 2 * V_DIM), BF16),
                        pltpu.VMEM((2, c, c), F32),
                        pltpu.VMEM((2, c, LANES), F32),
                        pltpu.VMEM((F_GROUPS, n2 * r, F_GROUP_DIM), F32)],
        compiler_params=_params(("parallel", "parallel")),
        name="retention_dft",
    )(a_f, a_b, q, k, v, kc, vc, tw, w2, w_c, y)


def _dft_a_kernel(w_ref, x_ref, y_ref):
    _, n1, rows, w = x_ref.shape
    r = DFT_ROWS
    nh = n1 // 2 + 1
    for r0 in range(0, rows, r):
        x = x_ref[0, :, r0:r0 + r, :].reshape(n1 * r, w)
        y = _dot(w_ref[...], x).reshape(2, nh, r, w).astype(BF16)
        y_ref[0, :, 0:nh, r0:r0 + r, :] = y
        for k1 in range(1, n1 // 2):
            y_ref[0, 0, n1 - k1, r0:r0 + r, :] = y[0, k1]
            y_ref[0, 1, n1 - k1, r0:r0 + r, :] = -y[1, k1]


def _dft_second_stage(j, tw_ref, w2_ref, cc, y_ref, z_scr):
    n2 = y_ref.shape[3]
    w2c = w2_ref[0]
    w2s = w2_ref[1]
    tc = tw_ref[j, 0:1, :]
    ts = tw_ref[j, 1:2, :]
    ec = w2c * tc - w2s * ts
    es = w2s * tc + w2c * ts
    m = jnp.concatenate([jnp.concatenate([ec, -es], axis=1),
                         jnp.concatenate([es, ec], axis=1)], axis=0).astype(BF16)
    y = jnp.concatenate([y_ref[0, 0, j], y_ref[0, 1, j]], axis=0)
    zz = _dot(m, y)
    for g in range(F_GROUPS):
        cols = slice(g * F_GROUP_DIM, (g + 1) * F_GROUP_DIM)
        zcs = jnp.concatenate([zz[:n2, cols], zz[n2:, cols]], axis=1).astype(BF16)
        z_scr[g, pl.ds(j, n2, stride=DFT_ROWS), :] = _dot(zcs, cc)


def _dft_emit(z_scr, z_ref):
    n2 = z_ref.shape[1]
    z = jnp.concatenate([z_scr[g] for g in range(F_GROUPS)], axis=1)
    z_ref[0] = z.reshape(n2, DFT_ROWS, F_WIDTH).astype(BF16)


def _dft_tables(seq):
    n1 = DFT_N1
    n2 = seq // n1

    def cs(num, den):
        ang = 2.0 * np.pi * (num % den) / den
        return np.cos(ang), np.sin(ang)

    a = np.arange(n1)
    w_a = np.concatenate(cs(a[:n1 // 2 + 1, None] * a[None, :], n1), axis=0)
    m = np.arange(n2)
    tw = np.stack(cs(a[:, None] * m[None, :], seq), axis=1)
    w2 = np.stack(cs(m[:, None] * m[None, :], n2), axis=0)
    ch = np.arange(F_GROUP_DIM)
    cc, sc = cs(ch[:, None] * ch[None, :], F_GROUP_DIM)
    scale = 1.0 / np.sqrt(seq * F_GROUP_DIM)
    w_c = np.concatenate([cc, -sc], axis=0) * scale
    return [jnp.asarray(t, dtype=F32) for t in (w_a, tw, w2, w_c)]


def _dft_first_stage(f):
    b, seq, w = f.shape
    n1 = DFT_N1
    n2 = seq // n1
    w_a, tw, w2, w_c = _dft_tables(seq)
    r = DFT_ROWS
    spread = (jnp.arange(n1 * r)[None, :] // r == jnp.arange(n1)[:, None]).astype(F32)
    w_rep = jnp.dot(w_a, spread, precision=lax.Precision.HIGHEST)
    same_r = jnp.arange(w_a.shape[0] * r)[:, None] % r == jnp.arange(n1 * r)[None, :] % r
    w_a = jnp.where(same_r, jnp.repeat(w_rep, r, axis=0), 0.0).astype(BF16)
    y = pl.pallas_call(
        _dft_a_kernel,
        grid=(b, n2 // DFT_A_ROWS),
        in_specs=[_const_spec(w_a.shape),
                  pl.BlockSpec((1, n1, DFT_A_ROWS, w), lambda i, j: (i, 0, j, 0))],
        out_specs=pl.BlockSpec((1, 2, n1, DFT_A_ROWS, w), lambda i, j: (i, 0, 0, j, 0)),
        out_shape=jax.ShapeDtypeStruct((b, 2, n1, n2, w), BF16),
        compiler_params=_params(("parallel", "parallel")),
        name="dft_a",
    )(w_a, f.reshape(b, n1, n2, w))
    return y, tw, w2, w_c


def _merge_kernel(x_ref, mod_ref, nw_ref, n2w_ref, ro_ref, sg_ref, z_ref, wbg_ref, bbg_ref,
                  wro_ref, wfo_ref, wout_ref, *refs, n_cast):
    o_ref, h2_ref = refs[n_cast:n_cast + 2]
    _run_casts(refs[:n_cast] + refs[n_cast + 2:])
    for r0 in range(0, x_ref.shape[0], MERGE_SUB):
        rows = slice(r0, r0 + MERGE_SUB)
        x = x_ref[rows, :]
        h = _mod_norm(x, nw_ref[...], mod_ref[0, 0:1, :], mod_ref[0, 1:2, :])
        t = jnp.tanh(_dot(h.astype(BF16), wbg_ref[...]) + 0.5 * bbg_ref[...])
        gated = []
        for hd in range(HEADS):
            cols = slice(hd * V_DIM, (hd + 1) * V_DIM)
            o = ro_ref[rows, cols].astype(F32)
            oc = o - jnp.mean(o, axis=-1, keepdims=True)
            var = jnp.mean(oc * oc, axis=-1, keepdims=True)
            gated.append((oc * lax.rsqrt(var + EPS) * sg_ref[rows, cols].astype(F32)).astype(BF16))
        ret_d = _dot(jnp.concatenate(gated, axis=1), wro_ref[...])
        four_d = _dot(z_ref[rows, :], wfo_ref[...])
        m2 = (t[:, :D_MODEL] * ret_d + ret_d) + (t[:, D_MODEL:] * four_d + four_d)
        y2 = _dot(m2.astype(BF16), wout_ref[...])
        x1 = x + (0.5 * mod_ref[0, 2:3, :]) * y2
        o_ref[rows, :] = x1
        h2 = _mod_norm(x1, n2w_ref[...], mod_ref[0, 3:4, :], mod_ref[0, 4:5, :])
        h2_ref[rows, :] = h2.astype(BF16)


def _merge(x2, mod3, norm_w, norm2_w, ro, sg, z, w_bg, b_bg, w_ro, w_fo, w_out, seq, to_cast):
    t, d = x2.shape
    tl = MERGE_TILE
    tpb = seq // tl
    tok = lambda w: pl.BlockSpec((tl, w), lambda i: (i, 0))
    cast_specs, cast_shapes = _cast_jobs(to_cast, t // tl)
    out = pl.pallas_call(
        functools.partial(_merge_kernel, n_cast=len(to_cast)),
        grid=(t // tl,),
        in_specs=[tok(d),
                  pl.BlockSpec((1, N_MOD, d), lambda i: (i // tpb, 0, 0)),
                  _const_spec((1, d)), _const_spec((1, d)),
                  tok(V_WIDTH), tok(V_WIDTH), tok(F_WIDTH),
                  _const_spec(w_bg.shape), _const_spec((1, 2 * d)),
                  _const_spec(w_ro.shape), _const_spec(w_fo.shape),
                  _const_spec(w_out.shape)] + cast_specs,
        out_specs=[tok(d), tok(d)] + cast_specs,
        out_shape=[jax.ShapeDtypeStruct((t, d), F32),
                   jax.ShapeDtypeStruct((t, d), BF16)] + cast_shapes,
        compiler_params=_params(("arbitrary",)),
        name="merge",
    )(x2, mod3, norm_w, norm2_w, ro, sg, z, w_bg, b_bg, w_ro, w_fo, w_out, *to_cast)
    return out[:2], out[2:]


HALO = 8
FFN_LEAD = 16
FFN_SUB = 256


def _two_gelu_tanh(x):
    c1 = np.sqrt(2.0 / np.pi)
    return x + x * jnp.tanh(x * (c1 + (c1 * 0.044715) * (x * x)))


def _ffn_kernel(xm_ref, hm_ref, xp_ref, xn_ref, mod_ref, nw_ref, wu_ref, cw_ref, cb_ref,
                wd_ref, fnw_ref, o_ref, h_scr, u_scr, act_scr, y_scr, *, tiles_per_seq):
    tl = xm_ref.shape[0]
    sub = FFN_SUB
    half = sub // 2
    nc = FFN_CHUNK
    i = pl.program_id(0)
    keep_prev = ((i % tiles_per_seq) != 0).astype(F32)
    keep_next = ((i % tiles_per_seq) != tiles_per_seq - 1).astype(F32)

    def pre(x):
        return _mod_norm(x, nw_ref[...], mod_ref[0, 3:4, :], mod_ref[0, 4:5, :])

    pad = jnp.zeros((FFN_LEAD - HALO, xm_ref.shape[1]), F32)
    h_scr[0:FFN_LEAD] = jnp.concatenate([pad, pre(xp_ref[...]) * keep_prev], axis=0).astype(BF16)
    h_scr[FFN_LEAD:FFN_LEAD + tl] = hm_ref[...]
    h_scr[FFN_LEAD + tl:] = jnp.concatenate([pre(xn_ref[...]) * keep_next, pad], axis=0).astype(BF16)

    def project(j):
        hb = h_scr[...]
        for part in range(2):
            lo = part * FFN_DIM + j * nc
            u = _dot(hb, wu_ref[:, lo:lo + nc])
            for s in range(nc // LANES):
                u_scr[j % 2, part, s] = u[:, s * LANES:(s + 1) * LANES]

    def conv(j, part, s, blk, scale):
        lo = part * FFN_DIM + j * nc + s * LANES
        w = cw_ref[:, lo:lo + LANES] * scale
        bias = cb_ref[:, lo:lo + LANES] * scale
        first = FFN_LEAD + blk * sub
        rows = lambda start: u_scr[j % 2, part, s, pl.ds(start, half, stride=2), :]
        before, even, odd, after = rows(first - 1), rows(first), rows(first + 1), rows(first + 2)
        return (before * w[0:1] + even * w[1:2] + odd * w[2:3] + bias,
                even * w[0:1] + odd * w[1:2] + after * w[2:3] + bias)

    def activate(j):
        for s in range(nc // LANES):
            cols = slice(j * nc + s * LANES, j * nc + (s + 1) * LANES)
            for blk in range(tl // sub):
                gate = conv(j, 0, s, blk, 1.0)
                val = conv(j, 1, s, blk, 0.5)
                for par in range(2):
                    r0 = blk * sub + par * half
                    act_scr[r0:r0 + half, cols] = (_two_gelu_tanh(gate[par]) * val[par]).astype(BF16)

    project(0)
    for j in range(N_FFN_CHUNKS):
        if j + 1 < N_FFN_CHUNKS:
            project(j + 1)
        activate(j)
    for blk in range(tl // sub):
        r0 = blk * sub
        y = _dot(act_scr[r0:r0 + sub, :], wd_ref[...])
        for s in range(y_scr.shape[0]):
            for par in range(2):
                y_scr[s, pl.ds(par, half, stride=2), :] = (
                    y[par * half:(par + 1) * half, s * LANES:(s + 1) * LANES])
        y = jnp.concatenate([y_scr[s] for s in range(y_scr.shape[0])], axis=1)
        x2 = xm_ref[r0:r0 + sub, :] + mod_ref[0, 5:6, :] * y
        o_ref[r0:r0 + sub, :] = _rms_norm(x2, fnw_ref[...])


def _ffn(x1, h2, mod3, norm_w, w_u, cw, cb, w_d, fnorm_w, seq):
    t, d = x1.shape
    tl = FFN_TILE
    tpb = seq // tl
    hb = tl // HALO
    last = t // HALO - 1
    rows = FFN_LEAD + tl + FFN_LEAD
    return pl.pallas_call(
        functools.partial(_ffn_kernel, tiles_per_seq=tpb),
        grid=(t // tl,),
        in_specs=[pl.BlockSpec((tl, d), lambda i: (i, 0)),
                  pl.BlockSpec((tl, d), lambda i: (i, 0)),
                  pl.BlockSpec((HALO, d), lambda i: (jnp.maximum(i * hb - 1, 0), 0)),
                  pl.BlockSpec((HALO, d), lambda i: (jnp.minimum((i + 1) * hb, last), 0)),
                  pl.BlockSpec((1, N_MOD, d), lambda i: (i // tpb, 0, 0)),
                  _const_spec((1, d)),
                  _const_spec(w_u.shape), _const_spec(cw.shape), _const_spec(cb.shape),
                  _const_spec(w_d.shape), _const_spec((1, d))],
        out_specs=pl.BlockSpec((tl, d), lambda i: (i, 0)),
        out_shape=jax.ShapeDtypeStruct((t, d), F32),
        scratch_shapes=[pltpu.VMEM((rows, d), BF16),
                        pltpu.VMEM((2, 2, FFN_CHUNK // LANES, rows, LANES), F32),
                        pltpu.VMEM((tl, FFN_DIM), BF16),
                        pltpu.VMEM((d // LANES, FFN_SUB, LANES), F32)],
        compiler_params=_params(("parallel",)),
        name="ffn",
    )(x1, h2, x1, x1, mod3, norm_w, w_u, cw, cb, w_d, fnorm_w)


def _rope_tables(seq):
    n_freq = QK_DIM // 4
    lane = jnp.arange(LANES)
    inv = ROPE_BASE ** (-(lane % n_freq).astype(F32) / n_freq)
    sign = jnp.where(lane < LANES // 2, -1.0, 1.0).astype(F32)

    def table(n):
        ang = jnp.arange(n, dtype=F32)[:, None] * inv[None, :]
        return jnp.stack([jnp.cos(ang), jnp.sin(ang) * sign])

    return table(seq // GRID_W), table(GRID_W)


def kernel(x, c, ctx, c_ctx, w_mod, b_mod, norm1_w, w_in, ret_decay_f, ret_decay_b,
           w_ret_out, w_four_out, w_branch_gate, b_branch_gate, w_out, norm2_w,
           w_up, conv_w, conv_b, w_down, final_norm_w):
    assert w_mod.shape[0] == 1, "single-layer block"
    b, seq, d = x.shape
    t = b * seq
    assert b == 2 and seq % RET_CHUNK == 0
    assert seq % FFN_TILE == 0 and seq % MERGE_TILE == 0 and seq % PROJ_TILE == 0

    c8 = jnp.concatenate([c, c_ctx[None, :], jnp.zeros((8 - b - 1, d), F32)], axis=0)
    mod, w_in_b, w_qk = _modulation(c8, w_mod[0], b_mod[0], w_in[0])
    mod3 = mod.reshape(8, N_MOD, d)
    n1w = norm1_w[0].reshape(1, d)

    kc, vc = _ctx_proj(ctx, mod3, n1w, w_qk, w_in_b)
    rtab, ctab = _rope_tables(seq)
    x2 = x.reshape(t, d)
    (q, k, v, sg, f), (w_bg, w_ro, w_fo, w_o) = _proj(
        x2, mod3, n1w, w_qk, w_in_b, rtab, ctab, seq,
        [w_branch_gate[0], w_ret_out[0], w_four_out[0], w_out[0]], (0.5, 1.0, 1.0, 1.0))

    a_f = jnp.broadcast_to(ret_decay_f[0][:, None, None], (HEADS, 1, LANES))
    a_b = jnp.broadcast_to(ret_decay_b[0][:, None, None], (HEADS, 1, LANES))
    y, tw, w2, w_c = _dft_first_stage(f.reshape(b, seq, F_WIDTH))
    ro, z = _retention_and_dft(a_f, a_b, q.reshape(b, seq, QK_WIDTH), k.reshape(b, seq, QK_WIDTH),
                               v.reshape(b, seq, V_WIDTH), kc, vc, y, tw, w2, w_c)

    n2w = norm2_w[0].reshape(1, d)
    (x1, h2), (w_u, w_d) = _merge(
        x2, mod3, n1w, n2w, ro.reshape(t, V_WIDTH), sg, z.reshape(t, F_WIDTH),
        w_bg, b_branch_gate[0].reshape(1, 2 * d), w_ro, w_fo, w_o, seq, [w_up[0], w_down[0]])

    out = _ffn(x1, h2, mod3, n2w, w_u, conv_w[0], conv_b[0].reshape(1, 2 * FFN_DIM), w_d,
               final_norm_w.reshape(1, d), seq)
    return out.reshape(b, seq, d)
```

```python
import functools

import numpy as np
import jax
import jax.numpy as jnp
from jax import lax
from jax.experimental import pallas as pl
from jax.experimental.pallas import tpu as pltpu

F32 = jnp.float32
BF16 = jnp.bfloat16

D_MODEL = 1024
GRID_W = 64
HEADS = 8
QK_DIM = 64
V_DIM = 128
QK_WIDTH = HEADS * QK_DIM
V_WIDTH = HEADS * V_DIM
ROPE_BASE = 10000.0
F_GROUPS = 4
F_GROUP_DIM = 128
F_WIDTH = F_GROUPS * F_GROUP_DIM
K_OFF = QK_WIDTH
V_OFF = K_OFF + QK_WIDTH
G_OFF = V_OFF + V_WIDTH
F_OFF = G_OFF + V_WIDTH
IN_COLS = F_OFF + F_WIDTH
FFN_DIM = 2816
N_MOD = 6
EPS = 1e-6

LANES = 128
RET_CHUNK = 256
RET_UNROLL = 16
FFN_CHUNK = 256
N_FFN_CHUNKS = FFN_DIM // FFN_CHUNK
FFN_TILE = 512
MERGE_TILE = 1024
MERGE_SUB = 512
PROJ_TILE = 1024
PROJ_SUB = 256
MOD_STEPS = 2
DFT_N1 = 64
BF16_ROWS = 16
DFT_ROWS = BF16_ROWS
DFT_A_ROWS = 2 * DFT_ROWS
VMEM_LIMIT = 56 * 1024 * 1024


def _params(sem):
    return pltpu.CompilerParams(dimension_semantics=sem, vmem_limit_bytes=VMEM_LIMIT)


def _dot(a, b):
    return jnp.dot(a, b, preferred_element_type=F32)


def _rms_norm(x, w):
    return x * lax.rsqrt(jnp.mean(x * x, axis=-1, keepdims=True) + EPS) * w


def _mod_norm(x, w, shift, scale):
    return x * lax.rsqrt(jnp.mean(x * x, axis=-1, keepdims=True) + EPS) * (w * (1.0 + scale)) + shift


def _const_spec(shape):
    zeros = (0,) * len(shape)
    return pl.BlockSpec(shape, lambda *_: zeros, pipeline_mode=pl.Buffered(1))


def _cast_jobs(weights, steps):
    specs, shapes = [], []
    for w in weights:
        n_rows = w.shape[0]
        rows = -(-n_rows // steps)
        while rows % BF16_ROWS or n_rows % rows:
            rows += 1
        last = n_rows // rows - 1
        specs.append(pl.BlockSpec((rows, w.shape[1]), lambda i, last=last: (jnp.minimum(i, last), 0)))
        shapes.append(jax.ShapeDtypeStruct(w.shape, BF16))
    return specs, shapes


def _run_casts(refs, scales=None):
    n = len(refs) // 2
    for j, (src, dst) in enumerate(zip(refs[:n], refs[n:])):
        scale = 1.0 if scales is None else scales[j]
        dst[...] = (src[...] if scale == 1.0 else src[...] * scale).astype(BF16)


def _mod_kernel(c_ref, w_ref, b_ref, win_ref, o_ref, winb_ref, wqk_ref):
    c = c_ref[...]
    s = c * jax.nn.sigmoid(c)
    o_ref[...] = _dot(s.astype(BF16), w_ref[...].astype(BF16)) + b_ref[...]
    wb = win_ref[...].astype(BF16)
    winb_ref[...] = wb
    n_qk = 2 * QK_WIDTH
    src = lax.broadcasted_iota(jnp.int32, (n_qk, n_qk), 0)
    dst = lax.broadcasted_iota(jnp.int32, (n_qk, n_qk), 1)
    select = (src == _qk_source_column(dst)).astype(BF16)
    wqk_ref[...] = _dot(wb[:, :n_qk], select).astype(BF16)


def _qk_source_column(col):
    half = QK_DIM // 2
    lane = col % LANES
    head = 2 * ((col % QK_WIDTH) // LANES) + (lane // half) % 2
    return (col // QK_WIDTH) * QK_WIDTH + head * QK_DIM + (lane // (2 * half)) * half + lane % half


def _modulation(c8, w_mod, b_mod, w_in):
    n = w_mod.shape[1]
    steps = MOD_STEPS
    tn = n // steps
    d, n_in = w_in.shape
    rows = d // steps
    return pl.pallas_call(
        _mod_kernel,
        grid=(steps,),
        in_specs=[_const_spec((8, D_MODEL)),
                  pl.BlockSpec((D_MODEL, tn), lambda j: (0, j)),
                  pl.BlockSpec((1, tn), lambda j: (0, j)),
                  pl.BlockSpec((rows, n_in), lambda j: (j, 0))],
        out_specs=[pl.BlockSpec((8, tn), lambda j: (0, j)),
                   pl.BlockSpec((rows, n_in), lambda j: (j, 0)),
                   pl.BlockSpec((rows, 2 * QK_WIDTH), lambda j: (j, 0))],
        out_shape=[jax.ShapeDtypeStruct((8, n), F32),
                   jax.ShapeDtypeStruct((d, n_in), BF16),
                   jax.ShapeDtypeStruct((d, 2 * QK_WIDTH), BF16)],
        compiler_params=_params(("parallel",)),
        name="mod",
    )(c8, w_mod, b_mod.reshape(1, n), w_in)


def _ctx_kernel(x_ref, mod_ref, nw_ref, wk_ref, wv_ref, k_ref, v_ref):
    x = x_ref[0]
    h = _mod_norm(x, nw_ref[...], mod_ref[0, 0:1, :], mod_ref[0, 1:2, :])
    hb = h.astype(BF16)
    k_ref[0] = _dot(hb, wk_ref[...]).astype(BF16)
    v_ref[0] = _dot(hb, wv_ref[...]).astype(BF16)


def _ctx_proj(ctx, mod3, norm_w, w_qk, w_in):
    b, lc, d = ctx.shape
    assert V_OFF % V_WIDTH == 0
    return pl.pallas_call(
        _ctx_kernel,
        grid=(b,),
        in_specs=[pl.BlockSpec((1, lc, d), lambda i: (i, 0, 0)),
                  pl.BlockSpec((1, N_MOD, d), lambda i: (2, 0, 0)),
                  _const_spec((1, d)),
                  pl.BlockSpec((d, QK_WIDTH), lambda i: (0, 1), pipeline_mode=pl.Buffered(1)),
                  pl.BlockSpec((d, V_WIDTH), lambda i: (0, V_OFF // V_WIDTH),
                               pipeline_mode=pl.Buffered(1))],
        out_specs=[pl.BlockSpec((1, lc, QK_WIDTH), lambda i: (i, 0, 0)),
                   pl.BlockSpec((1, lc, V_WIDTH), lambda i: (i, 0, 0))],
        out_shape=[jax.ShapeDtypeStruct((b, lc, QK_WIDTH), BF16),
                   jax.ShapeDtypeStruct((b, lc, V_WIDTH), BF16)],
        compiler_params=_params(("parallel",)),
        name="ctx_proj",
    )(ctx, mod3, norm_w, w_qk, w_in)


def _proj_kernel(x_ref, mod_ref, nw_ref, wqk_ref, w_ref, rtab_ref, ctab_ref, *refs, cast_scales):
    n_cast = len(cast_scales)
    q_ref, k_ref, v_ref, sg_ref, f_ref = refs[n_cast:n_cast + 5]
    _run_casts(refs[:n_cast] + refs[n_cast + 5:], cast_scales)
    lane = lax.broadcasted_iota(jnp.int32, (1, LANES), 1)
    by_row = (lane % (QK_DIM // 2)) < QK_DIM // 4

    def rope(t, trig, scale, out_ref, rows):
        for j in range(QK_WIDTH // LANES):
            tj = t[:, j * LANES:(j + 1) * LANES]
            r = tj * trig[0] + pltpu.roll(tj, LANES // 2, 1) * trig[1]
            out_ref[rows, j * LANES:(j + 1) * LANES] = (r * scale).astype(BF16)

    for r0 in range(0, x_ref.shape[0], PROJ_SUB):
        rows = slice(r0, r0 + PROJ_SUB)
        trig = []
        for cs in range(2):
            trig.append(jnp.concatenate(
                [jnp.where(by_row, rtab_ref[cs, g:g + 1, :], ctab_ref[cs])
                 for g in range(r0 // GRID_W, (r0 + PROJ_SUB) // GRID_W)], axis=0))
        x = x_ref[rows, :]
        h = _mod_norm(x, nw_ref[...], mod_ref[0, 0:1, :], mod_ref[0, 1:2, :])
        hb = h.astype(BF16)
        rope(_dot(hb, wqk_ref[:, :QK_WIDTH]), trig, QK_DIM ** -0.5, q_ref, rows)
        rope(_dot(hb, wqk_ref[:, QK_WIDTH:]), trig, 1.0, k_ref, rows)
        v_ref[rows, :] = _dot(hb, w_ref[:, V_OFF:G_OFF]).astype(BF16)
        g = _dot(hb, w_ref[:, G_OFF:F_OFF])
        hg = 0.5 * g
        sg_ref[rows, :] = (hg * jnp.tanh(hg) + hg).astype(BF16)
        f_ref[rows, :] = _dot(hb, w_ref[:, F_OFF:IN_COLS]).astype(BF16)


def _proj(x2, mod3, norm_w, w_qk, w_in, rtab, ctab, seq, to_cast, cast_scales):
    t, d = x2.shape
    tl = PROJ_TILE
    tpb = seq // tl
    tok = lambda w: pl.BlockSpec((tl, w), lambda i: (i, 0))
    cast_specs, cast_shapes = _cast_jobs(to_cast, t // tl)
    out = pl.pallas_call(
        functools.partial(_proj_kernel, cast_scales=tuple(cast_scales)),
        grid=(t // tl,),
        in_specs=[tok(d),
                  pl.BlockSpec((1, N_MOD, d), lambda i: (i // tpb, 0, 0)),
                  _const_spec((1, d)),
                  _const_spec(w_qk.shape), _const_spec(w_in.shape),
                  pl.BlockSpec((2, tl // GRID_W, LANES), lambda i: (0, i % tpb, 0)),
                  _const_spec(ctab.shape)] + cast_specs,
        out_specs=[tok(QK_WIDTH), tok(QK_WIDTH), tok(V_WIDTH), tok(V_WIDTH),
                   tok(F_WIDTH)] + cast_specs,
        out_shape=[jax.ShapeDtypeStruct((t, QK_WIDTH), BF16),
                   jax.ShapeDtypeStruct((t, QK_WIDTH), BF16),
                   jax.ShapeDtypeStruct((t, V_WIDTH), BF16),
                   jax.ShapeDtypeStruct((t, V_WIDTH), BF16),
                   jax.ShapeDtypeStruct((t, F_WIDTH), BF16)] + cast_shapes,
        compiler_params=_params(("arbitrary",)),
        name="proj",
    )(x2, mod3, norm_w, w_qk, w_in, rtab, ctab, *to_cast)
    return out[:5], out[5:]


def _retention_kernel(af_ref, ab_ref, q_ref, k_ref, v_ref, kc_ref, vc_ref,
                      tw_ref, w2_ref, cc_ref, y_ref,
                      o_ref, z_ref, ds_scr, st_scr, d_scr, tab_scr, z_scr):
    c = RET_CHUNK
    seq = q_ref.shape[1]
    lc = kc_ref.shape[1]
    n = seq // c
    lgf = [-jnp.exp(af_ref[hh]) for hh in range(2)]
    lgb = [-jnp.exp(ab_ref[hh]) for hh in range(2)]
    half = QK_DIM // 2

    lane_head = (lax.broadcasted_iota(jnp.int32, (1, LANES), 1) // half) % 2
    masks = [(lane_head == hh).astype(BF16) for hh in range(2)]
    v_head = lax.broadcasted_iota(jnp.int32, (1, 2 * V_DIM), 1) // V_DIM
    vmasks = [(v_head == hh).astype(BF16) for hh in range(2)]
    lgf_lane = jnp.where(lane_head == 0, lgf[0], lgf[1])
    lgb_lane = jnp.where(lane_head == 0, lgb[0], lgb[1])
    row_head = (lax.broadcasted_iota(jnp.int32, (LANES, 1), 0) // half) % 2
    lgf_row = jnp.where(row_head == 0, lgf[0][:, 0:1], lgf[1][:, 0:1])
    lgb_row = jnp.where(row_head == 0, lgb[0][:, 0:1], lgb[1][:, 0:1])

    pos = lax.broadcasted_iota(jnp.int32, (c, LANES), 0).astype(F32)
    tab_scr[0] = jnp.exp(lgf_lane * (pos + 1.0))
    tab_scr[1] = jnp.exp(lgb_lane * (c - pos))
    diff = (lax.broadcasted_iota(jnp.int32, (c, c), 0)
            - lax.broadcasted_iota(jnp.int32, (c, c), 1)).astype(F32)
    for hh in range(2):
        d_scr[hh] = (jnp.where(diff >= 0, jnp.exp(lgf[hh][:, 0:1] * jnp.maximum(diff, 0.0)), 0.0)
                     + jnp.where(diff <= 0, jnp.exp(lgb[hh][:, 0:1] * jnp.maximum(-diff, 0.0)), 0.0))

    def k_decays(tokens):
        t = lax.broadcasted_iota(jnp.int32, (1, tokens), 1).astype(F32)
        return jnp.exp(lgf_row * (tokens - 1.0 - t)), jnp.exp(lgb_row * t)

    def state_increment(k_rows, v_rows, decays):
        kt = jnp.transpose(k_rows.astype(F32))
        lhs = jnp.concatenate([kt * decays[0], kt * decays[1]], axis=0).astype(BF16)
        return _dot(lhs, v_rows)

    kdec = k_decays(c)

    def incr(i, carry):
        rows = pl.ds(pl.multiple_of(i * c, c), c)
        ds_scr[i] = state_increment(k_ref[0, rows, :], v_ref[0, rows, :], kdec)
        return carry

    lax.fori_loop(0, n, incr, 0, unroll=RET_UNROLL)

    s0 = state_increment(kc_ref[0], vc_ref[0], k_decays(lc))
    col_head = lax.broadcasted_iota(jnp.int32, (1, 2 * V_DIM), 1) // V_DIM
    own = (row_head == col_head).astype(F32)
    decay_f = jnp.exp(lgf_row * c) * own
    decay_b = jnp.exp(lgb_row * c) * own

    def scan_f(i, s):
        st_scr[i, 0:LANES, :] = (s * own).astype(BF16)
        return decay_f * s + ds_scr[i, 0:LANES, :]

    lax.fori_loop(0, n, scan_f, s0[0:LANES])

    def scan_b(t, s):
        i = n - 1 - t
        st_scr[i, LANES:2 * LANES, :] = (s * own).astype(BF16)
        return decay_b * s + ds_scr[i, LANES:2 * LANES, :]

    lax.fori_loop(0, n, scan_b, s0[LANES:2 * LANES])

    def outputs(i):
        rows = pl.ds(pl.multiple_of(i * c, c), c)
        q = q_ref[0, rows, :]
        k = k_ref[0, rows, :]
        qf = q.astype(F32)
        qd = jnp.concatenate([qf * tab_scr[0], qf * tab_scr[1]], axis=1).astype(BF16)
        inter = _dot(qd, st_scr[i])
        k2 = jnp.concatenate([k * masks[0], k * masks[1]], axis=0)
        scores = lax.dot_general(q, k2, (((1,), (1,)), ((), ())), preferred_element_type=F32)
        p = (scores * jnp.concatenate([d_scr[0], d_scr[1]], axis=1)).astype(BF16)
        v = v_ref[0, rows, :]
        v2 = jnp.concatenate([v * vmasks[0], v * vmasks[1]], axis=0)
        o_ref[0, rows, :] = (_dot(p, v2) + inter).astype(BF16)

    per_k1 = n // DFT_ROWS
    cc = cc_ref[...].astype(BF16)

    def outputs_and_dft(j, carry):
        for u in range(per_k1):
            outputs(j * per_k1 + u)
        _dft_second_stage(j, tw_ref, w2_ref, cc, y_ref, z_scr)
        return carry

    lax.fori_loop(0, DFT_ROWS, outputs_and_dft, 0, unroll=RET_UNROLL // per_k1)
    _dft_emit(z_scr, z_ref)


def _retention_and_dft(a_f, a_b, q, k, v, kc, vc, y, tw, w2, w_c):
    b, seq, _ = v.shape
    lc = kc.shape[1]
    c = RET_CHUNK
    n = seq // c
    r = DFT_ROWS
    _, _, n1, n2, w = y.shape
    assert n % r == 0 and n1 // r == HEADS // 2
    dec = pl.BlockSpec((2, 1, LANES), lambda i, p: (p, 0, 0))
    qk = lambda rows: pl.BlockSpec((1, rows, LANES), lambda i, p: (i, 0, p))
    vv = lambda rows: pl.BlockSpec((1, rows, 2 * V_DIM), lambda i, p: (i, 0, p))
    return pl.pallas_call(
        _retention_kernel,
        grid=(b, HEADS // 2),
        in_specs=[dec, dec, qk(seq), qk(seq), vv(seq), qk(lc), vv(lc),
                  pl.BlockSpec((r, 2, n2), lambda i, p: (p, 0, 0)),
                  _const_spec((2, n2, n2)),
                  _const_spec((2 * F_GROUP_DIM, F_GROUP_DIM)),
                  pl.BlockSpec((1, 2, r, n2, w), lambda i, p: (i, 0, p, 0, 0))],
        out_specs=[vv(seq), pl.BlockSpec((1, n2, r, w), lambda i, p: (i, 0, p, 0))],
        out_shape=[jax.ShapeDtypeStruct((b, seq, V_WIDTH), BF16),
                   jax.ShapeDtypeStruct((b, n2, n1, w), BF16)],
        scratch_shapes=[pltpu.VMEM((n, 2 * LANES, 2 * V_DIM), F32),
                        pltpu.VMEM((n, 2 * LANES, 2 * V_DIM), BF16),
                        pltpu.VMEM((2, c, c), F32),
                        pltpu.VMEM((2, c, LANES), F32),
                        pltpu.VMEM((F_GROUPS, n2 * r, F_GROUP_DIM), F32)],
        compiler_params=_params(("parallel", "parallel")),
        name="retention_dft",
    )(a_f, a_b, q, k, v, kc, vc, tw, w2, w_c, y)


def _dft_a_kernel(w_ref, x_ref, y_ref):
    _, n1, rows, w = x_ref.shape
    r = DFT_ROWS
    nh = n1 // 2 + 1
    for r0 in range(0, rows, r):
        x = x_ref[0, :, r0:r0 + r, :].reshape(n1 * r, w)
        y = _dot(w_ref[...], x).reshape(2, nh, r, w).astype(BF16)
        y_ref[0, :, 0:nh, r0:r0 + r, :] = y
        for k1 in range(1, n1 // 2):
            y_ref[0, 0, n1 - k1, r0:r0 + r, :] = y[0, k1]
            y_ref[0, 1, n1 - k1, r0:r0 + r, :] = -y[1, k1]


def _dft_second_stage(j, tw_ref, w2_ref, cc, y_ref, z_scr):
    n2 = y_ref.shape[3]
    w2c = w2_ref[0]
    w2s = w2_ref[1]
    tc = tw_ref[j, 0:1, :]
    ts = tw_ref[j, 1:2, :]
    ec = w2c * tc - w2s * ts
    es = w2s * tc + w2c * ts
    m = jnp.concatenate([jnp.concatenate([ec, -es], axis=1),
                         jnp.concatenate([es, ec], axis=1)], axis=0).astype(BF16)
    y = jnp.concatenate([y_ref[0, 0, j], y_ref[0, 1, j]], axis=0)
    zz = _dot(m, y)
    zero = jnp.zeros_like(cc)
    cc2 = jnp.concatenate([jnp.concatenate([cc, zero], axis=1),
                           jnp.concatenate([zero, cc], axis=1)], axis=0)
    for g in range(0, F_GROUPS, 2):
        parts = []
        for gg in (g, g + 1):
            cols = slice(gg * F_GROUP_DIM, (gg + 1) * F_GROUP_DIM)
            parts += [zz[:n2, cols], zz[n2:, cols]]
        out = _dot(jnp.concatenate(parts, axis=1).astype(BF16), cc2)
        z_scr[g, pl.ds(j, n2, stride=DFT_ROWS), :] = out[:, :F_GROUP_DIM]
        z_scr[g + 1, pl.ds(j, n2, stride=DFT_ROWS), :] = out[:, F_GROUP_DIM:]


def _dft_emit(z_scr, z_ref):
    n2 = z_ref.shape[1]
    z = jnp.concatenate([z_scr[g] for g in range(F_GROUPS)], axis=1)
    z_ref[0] = z.reshape(n2, DFT_ROWS, F_WIDTH).astype(BF16)


def _dft_tables(seq):
    n1 = DFT_N1
    n2 = seq // n1

    def cs(num, den):
        ang = 2.0 * np.pi * (num % den) / den
        return np.cos(ang), np.sin(ang)

    a = np.arange(n1)
    w_a = np.concatenate(cs(a[:n1 // 2 + 1, None] * a[None, :], n1), axis=0)
    m = np.arange(n2)
    tw = np.stack(cs(a[:, None] * m[None, :], seq), axis=1)
    w2 = np.stack(cs(m[:, None] * m[None, :], n2), axis=0)
    ch = np.arange(F_GROUP_DIM)
    cc, sc = cs(ch[:, None] * ch[None, :], F_GROUP_DIM)
    scale = 1.0 / np.sqrt(seq * F_GROUP_DIM)
    w_c = np.concatenate([cc, -sc], axis=0) * scale
    return [jnp.asarray(t, dtype=F32) for t in (w_a, tw, w2, w_c)]


def _dft_first_stage(f):
    b, seq, w = f.shape
    n1 = DFT_N1
    n2 = seq // n1
    w_a, tw, w2, w_c = _dft_tables(seq)
    r = DFT_ROWS
    spread = (jnp.arange(n1 * r)[None, :] // r == jnp.arange(n1)[:, None]).astype(F32)
    w_rep = jnp.dot(w_a, spread, precision=lax.Precision.HIGHEST)
    same_r = jnp.arange(w_a.shape[0] * r)[:, None] % r == jnp.arange(n1 * r)[None, :] % r
    w_a = jnp.where(same_r, jnp.repeat(w_rep, r, axis=0), 0.0).astype(BF16)
    y = pl.pallas_call(
        _dft_a_kernel,
        grid=(b, n2 // DFT_A_ROWS),
        in_specs=[_const_spec(w_a.shape),
                  pl.BlockSpec((1, n1, DFT_A_ROWS, w), lambda i, j: (i, 0, j, 0))],
        out_specs=pl.BlockSpec((1, 2, n1, DFT_A_ROWS, w), lambda i, j: (i, 0, 0, j, 0)),
        out_shape=jax.ShapeDtypeStruct((b, 2, n1, n2, w), BF16),
        compiler_params=_params(("parallel", "parallel")),
        name="dft_a",
    )(w_a, f.reshape(b, n1, n2, w))
    return y, tw, w2, w_c


def _merge_kernel(x_ref, mod_ref, nw_ref, n2w_ref, ro_ref, sg_ref, z_ref, wbg_ref, bbg_ref,
                  wro_ref, wfo_ref, wout_ref, *refs, n_cast):
    o_ref, h2_ref = refs[n_cast:n_cast + 2]
    _run_casts(refs[:n_cast] + refs[n_cast + 2:])
    for r0 in range(0, x_ref.shape[0], MERGE_SUB):
        rows = slice(r0, r0 + MERGE_SUB)
        x = x_ref[rows, :]
        h = _mod_norm(x, nw_ref[...], mod_ref[0, 0:1, :], mod_ref[0, 1:2, :])
        t = jnp.tanh(_dot(h.astype(BF16), wbg_ref[...]) + 0.5 * bbg_ref[...])
        gated = []
        for hd in range(HEADS):
            cols = slice(hd * V_DIM, (hd + 1) * V_DIM)
            o = ro_ref[rows, cols].astype(F32)
            oc = o - jnp.mean(o, axis=-1, keepdims=True)
            var = jnp.mean(oc * oc, axis=-1, keepdims=True)
            gated.append((oc * lax.rsqrt(var + EPS) * sg_ref[rows, cols].astype(F32)).astype(BF16))
        ret_d = _dot(jnp.concatenate(gated, axis=1), wro_ref[...])
        four_d = _dot(z_ref[rows, :], wfo_ref[...])
        m2 = (t[:, :D_MODEL] * ret_d + ret_d) + (t[:, D_MODEL:] * four_d + four_d)
        y2 = _dot(m2.astype(BF16), wout_ref[...])
        x1 = x + (0.5 * mod_ref[0, 2:3, :]) * y2
        o_ref[rows, :] = x1
        h2 = _mod_norm(x1, n2w_ref[...], mod_ref[0, 3:4, :], mod_ref[0, 4:5, :])
        h2_ref[rows, :] = h2.astype(BF16)


def _merge(x2, mod3, norm_w, norm2_w, ro, sg, z, w_bg, b_bg, w_ro, w_fo, w_out, seq, to_cast):
    t, d = x2.shape
    tl = MERGE_TILE
    tpb = seq // tl
    tok = lambda w: pl.BlockSpec((tl, w), lambda i: (i, 0))
    cast_specs, cast_shapes = _cast_jobs(to_cast, t // tl)
    out = pl.pallas_call(
        functools.partial(_merge_kernel, n_cast=len(to_cast)),
        grid=(t // tl,),
        in_specs=[tok(d),
                  pl.BlockSpec((1, N_MOD, d), lambda i: (i // tpb, 0, 0)),
                  _const_spec((1, d)), _const_spec((1, d)),
                  tok(V_WIDTH), tok(V_WIDTH), tok(F_WIDTH),
                  _const_spec(w_bg.shape), _const_spec((1, 2 * d)),
                  _const_spec(w_ro.shape), _const_spec(w_fo.shape),
                  _const_spec(w_out.shape)] + cast_specs,
        out_specs=[tok(d), tok(d)] + cast_specs,
        out_shape=[jax.ShapeDtypeStruct((t, d), F32),
                   jax.ShapeDtypeStruct((t, d), BF16)] + cast_shapes,
        compiler_params=_params(("arbitrary",)),
        name="merge",
    )(x2, mod3, norm_w, norm2_w, ro, sg, z, w_bg, b_bg, w_ro, w_fo, w_out, *to_cast)
    return out[:2], out[2:]


HALO = 8
FFN_LEAD = 16
FFN_SUB = 256


def _two_gelu_tanh(x):
    c1 = np.sqrt(2.0 / np.pi)
    return x + x * jnp.tanh(x * (c1 + (c1 * 0.044715) * (x * x)))


def _ffn_kernel(xm_ref, hm_ref, xp_ref, xn_ref, mod_ref, nw_ref, wu_ref, cw_ref, cb_ref,
                wd_ref, fnw_ref, o_ref, h_scr, u_scr, act_scr, y_scr, *, tiles_per_seq):
    tl = xm_ref.shape[0]
    sub = FFN_SUB
    half = sub // 2
    nc = FFN_CHUNK
    i = pl.program_id(0)
    keep_prev = ((i % tiles_per_seq) != 0).astype(F32)
    keep_next = ((i % tiles_per_seq) != tiles_per_seq - 1).astype(F32)

    def pre(x):
        return _mod_norm(x, nw_ref[...], mod_ref[0, 3:4, :], mod_ref[0, 4:5, :])

    pad = jnp.zeros((FFN_LEAD - HALO, xm_ref.shape[1]), F32)
    h_scr[0:FFN_LEAD] = jnp.concatenate([pad, pre(xp_ref[...]) * keep_prev], axis=0).astype(BF16)
    h_scr[FFN_LEAD:FFN_LEAD + tl] = hm_ref[...]
    h_scr[FFN_LEAD + tl:] = jnp.concatenate([pre(xn_ref[...]) * keep_next, pad], axis=0).astype(BF16)

    def project(j):
        hb = h_scr[...]
        for part in range(2):
            lo = part * FFN_DIM + j * nc
            u = _dot(hb, wu_ref[:, lo:lo + nc])
            for s in range(nc // LANES):
                u_scr[j % 2, part, s] = u[:, s * LANES:(s + 1) * LANES]

    def conv(j, part, s, blk, scale):
        lo = part * FFN_DIM + j * nc + s * LANES
        w = cw_ref[:, lo:lo + LANES] * scale
        bias = cb_ref[:, lo:lo + LANES] * scale
        first = FFN_LEAD + blk * sub
        rows = lambda start: u_scr[j % 2, part, s, pl.ds(start, half, stride=2), :]
        before, even, odd, after = rows(first - 1), rows(first), rows(first + 1), rows(first + 2)
        return (before * w[0:1] + even * w[1:2] + odd * w[2:3] + bias,
                even * w[0:1] + odd * w[1:2] + after * w[2:3] + bias)

    def activate(j):
        for s in range(nc // LANES):
            cols = slice(j * nc + s * LANES, j * nc + (s + 1) * LANES)
            for blk in range(tl // sub):
                gate = conv(j, 0, s, blk, 1.0)
                val = conv(j, 1, s, blk, 0.5)
                for par in range(2):
                    r0 = blk * sub + par * half
                    act_scr[r0:r0 + half, cols] = (_two_gelu_tanh(gate[par]) * val[par]).astype(BF16)

    project(0)
    for j in range(N_FFN_CHUNKS):
        if j + 1 < N_FFN_CHUNKS:
            project(j + 1)
        activate(j)
    for blk in range(tl // sub):
        r0 = blk * sub
        y = _dot(act_scr[r0:r0 + sub, :], wd_ref[...])
        for s in range(y_scr.shape[0]):
            for par in range(2):
                y_scr[s, pl.ds(par, half, stride=2), :] = (
                    y[par * half:(par + 1) * half, s * LANES:(s + 1) * LANES])
        y = jnp.concatenate([y_scr[s] for s in range(y_scr.shape[0])], axis=1)
        x2 = xm_ref[r0:r0 + sub, :] + mod_ref[0, 5:6, :] * y
        o_ref[r0:r0 + sub, :] = _rms_norm(x2, fnw_ref[...])


def _ffn(x1, h2, mod3, norm_w, w_u, cw, cb, w_d, fnorm_w, seq):
    t, d = x1.shape
    tl = FFN_TILE
    tpb = seq // tl
    hb = tl // HALO
    last = t // HALO - 1
    rows = FFN_LEAD + tl + FFN_LEAD
    return pl.pallas_call(
        functools.partial(_ffn_kernel, tiles_per_seq=tpb),
        grid=(t // tl,),
        in_specs=[pl.BlockSpec((tl, d), lambda i: (i, 0)),
                  pl.BlockSpec((tl, d), lambda i: (i, 0)),
                  pl.BlockSpec((HALO, d), lambda i: (jnp.maximum(i * hb - 1, 0), 0)),
                  pl.BlockSpec((HALO, d), lambda i: (jnp.minimum((i + 1) * hb, last), 0)),
                  pl.BlockSpec((1, N_MOD, d), lambda i: (i // tpb, 0, 0)),
                  _const_spec((1, d)),
                  _const_spec(w_u.shape), _const_spec(cw.shape), _const_spec(cb.shape),
                  _const_spec(w_d.shape), _const_spec((1, d))],
        out_specs=pl.BlockSpec((tl, d), lambda i: (i, 0)),
        out_shape=jax.ShapeDtypeStruct((t, d), F32),
        scratch_shapes=[pltpu.VMEM((rows, d), BF16),
                        pltpu.VMEM((2, 2, FFN_CHUNK // LANES, rows, LANES), F32),
                        pltpu.VMEM((tl, FFN_DIM), BF16),
                        pltpu.VMEM((d // LANES, FFN_SUB, LANES), F32)],
        compiler_params=_params(("parallel",)),
        name="ffn",
    )(x1, h2, x1, x1, mod3, norm_w, w_u, cw, cb, w_d, fnorm_w)


def _rope_tables(seq):
    n_freq = QK_DIM // 4
    lane = jnp.arange(LANES)
    inv = ROPE_BASE ** (-(lane % n_freq).astype(F32) / n_freq)
    sign = jnp.where(lane < LANES // 2, -1.0, 1.0).astype(F32)

    def table(n):
        ang = jnp.arange(n, dtype=F32)[:, None] * inv[None, :]
        return jnp.stack([jnp.cos(ang), jnp.sin(ang) * sign])

    return table(seq // GRID_W), table(GRID_W)


def kernel(x, c, ctx, c_ctx, w_mod, b_mod, norm1_w, w_in, ret_decay_f, ret_decay_b,
           w_ret_out, w_four_out, w_branch_gate, b_branch_gate, w_out, norm2_w,
           w_up, conv_w, conv_b, w_down, final_norm_w):
    assert w_mod.shape[0] == 1, "single-layer block"
    b, seq, d = x.shape
    t = b * seq
    assert b == 2 and seq % RET_CHUNK == 0
    assert seq % FFN_TILE == 0 and seq % MERGE_TILE == 0 and seq % PROJ_TILE == 0

    c8 = jnp.concatenate([c, c_ctx[None, :], jnp.zeros((8 - b - 1, d), F32)], axis=0)
    mod, w_in_b, w_qk = _modulation(c8, w_mod[0], b_mod[0], w_in[0])
    mod3 = mod.reshape(8, N_MOD, d)
    n1w = norm1_w[0].reshape(1, d)

    kc, vc = _ctx_proj(ctx, mod3, n1w, w_qk, w_in_b)
    rtab, ctab = _rope_tables(seq)
    x2 = x.reshape(t, d)
    (q, k, v, sg, f), (w_bg, w_ro, w_fo, w_o) = _proj(
        x2, mod3, n1w, w_qk, w_in_b, rtab, ctab, seq,
        [w_branch_gate[0], w_ret_out[0], w_four_out[0], w_out[0]], (0.5, 1.0, 1.0, 1.0))

    a_f = jnp.broadcast_to(ret_decay_f[0][:, None, None], (HEADS, 1, LANES))
    a_b = jnp.broadcast_to(ret_decay_b[0][:, None, None], (HEADS, 1, LANES))
    y, tw, w2, w_c = _dft_first_stage(f.reshape(b, seq, F_WIDTH))
    ro, z = _retention_and_dft(a_f, a_b, q.reshape(b, seq, QK_WIDTH), k.reshape(b, seq, QK_WIDTH),
                               v.reshape(b, seq, V_WIDTH), kc, vc, y, tw, w2, w_c)

    n2w = norm2_w[0].reshape(1, d)
    (x1, h2), (w_u, w_d) = _merge(
        x2, mod3, n1w, n2w, ro.reshape(t, V_WIDTH), sg, z.reshape(t, F_WIDTH),
        w_bg, b_branch_gate[0].reshape(1, 2 * d), w_ro, w_fo, w_o, seq, [w_up[0], w_down[0]])

    out = _ffn(x1, h2, mod3, n2w, w_u, conv_w[0], conv_b[0].reshape(1, 2 * FFN_DIM), w_d,
               final_norm_w.reshape(1, d), seq)
    return out.reshape(b, seq, d)
```

```python
import functools

import numpy as np
import jax
import jax.numpy as jnp
from jax import lax
from jax.experimental import pallas as pl
from jax.experimental.pallas import tpu as pltpu

F32 = jnp.float32
BF16 = jnp.bfloat16

D_MODEL = 1024
GRID_W = 64
HEADS = 8
QK_DIM = 64
V_DIM = 128
QK_WIDTH = HEADS * QK_DIM
V_WIDTH = HEADS * V_DIM
ROPE_BASE = 10000.0
F_GROUPS = 4
F_GROUP_DIM = 128
F_WIDTH = F_GROUPS * F_GROUP_DIM
K_OFF = QK_WIDTH
V_OFF = K_OFF + QK_WIDTH
G_OFF = V_OFF + V_WIDTH
F_OFF = G_OFF + V_WIDTH
IN_COLS = F_OFF + F_WIDTH
FFN_DIM = 2816
N_MOD = 6
EPS = 1e-6

LANES = 128
RET_CHUNK = 256
RET_UNROLL = 8
FFN_CHUNK = 256
N_FFN_CHUNKS = FFN_DIM // FFN_CHUNK
FFN_TILE = 512
MERGE_TILE = 1024
MERGE_SUB = 512
PROJ_TILE = 1024
PROJ_SUB = 256
MOD_STEPS = 2
DFT_N1 = 64
BF16_ROWS = 16
DFT_ROWS = BF16_ROWS
DFT_A_ROWS = 2 * DFT_ROWS
VMEM_LIMIT = 56 * 1024 * 1024


def _params(sem):
    return pltpu.CompilerParams(dimension_semantics=sem, vmem_limit_bytes=VMEM_LIMIT)


def _dot(a, b):
    return jnp.dot(a, b, preferred_element_type=F32)


def _rms_norm(x, w):
    return x * lax.rsqrt(jnp.mean(x * x, axis=-1, keepdims=True) + EPS) * w


def _mod_norm(x, w, shift, scale):
    return x * lax.rsqrt(jnp.mean(x * x, axis=-1, keepdims=True) + EPS) * (w * (1.0 + scale)) + shift


def _const_spec(shape):
    zeros = (0,) * len(shape)
    return pl.BlockSpec(shape, lambda *_: zeros, pipeline_mode=pl.Buffered(1))


def _cast_jobs(weights, steps):
    specs, shapes = [], []
    for w in weights:
        n_rows = w.shape[0]
        rows = -(-n_rows // steps)
        while rows % BF16_ROWS or n_rows % rows:
            rows += 1
        last = n_rows // rows - 1
        specs.append(pl.BlockSpec((rows, w.shape[1]), lambda i, last=last: (jnp.minimum(i, last), 0)))
        shapes.append(jax.ShapeDtypeStruct(w.shape, BF16))
    return specs, shapes


def _run_casts(refs, scales=None):
    n = len(refs) // 2
    for j, (src, dst) in enumerate(zip(refs[:n], refs[n:])):
        scale = 1.0 if scales is None else scales[j]
        dst[...] = (src[...] if scale == 1.0 else src[...] * scale).astype(BF16)


def _mod_kernel(c_ref, w_ref, b_ref, win_ref, o_ref, winb_ref, wqk_ref):
    c = c_ref[...]
    s = c * jax.nn.sigmoid(c)
    o_ref[...] = _dot(s.astype(BF16), w_ref[...].astype(BF16)) + b_ref[...]
    wb = win_ref[...].astype(BF16)
    winb_ref[...] = wb
    n_qk = 2 * QK_WIDTH
    src = lax.broadcasted_iota(jnp.int32, (n_qk, n_qk), 0)
    dst = lax.broadcasted_iota(jnp.int32, (n_qk, n_qk), 1)
    select = (src == _qk_source_column(dst)).astype(BF16)
    wqk_ref[...] = _dot(wb[:, :n_qk], select).astype(BF16)


def _qk_source_column(col):
    half = QK_DIM // 2
    lane = col % LANES
    head = 2 * ((col % QK_WIDTH) // LANES) + (lane // half) % 2
    return (col // QK_WIDTH) * QK_WIDTH + head * QK_DIM + (lane // (2 * half)) * half + lane % half


def _modulation(c8, w_mod, b_mod, w_in):
    n = w_mod.shape[1]
    steps = MOD_STEPS
    tn = n // steps
    d, n_in = w_in.shape
    rows = d // steps
    return pl.pallas_call(
        _mod_kernel,
        grid=(steps,),
        in_specs=[_const_spec((8, D_MODEL)),
                  pl.BlockSpec((D_MODEL, tn), lambda j: (0, j)),
                  pl.BlockSpec((1, tn), lambda j: (0, j)),
                  pl.BlockSpec((rows, n_in), lambda j: (j, 0))],
        out_specs=[pl.BlockSpec((8, tn), lambda j: (0, j)),
                   pl.BlockSpec((rows, n_in), lambda j: (j, 0)),
                   pl.BlockSpec((rows, 2 * QK_WIDTH), lambda j: (j, 0))],
        out_shape=[jax.ShapeDtypeStruct((8, n), F32),
                   jax.ShapeDtypeStruct((d, n_in), BF16),
                   jax.ShapeDtypeStruct((d, 2 * QK_WIDTH), BF16)],
        compiler_params=_params(("parallel",)),
        name="mod",
    )(c8, w_mod, b_mod.reshape(1, n), w_in)


def _ctx_kernel(x_ref, mod_ref, nw_ref, wk_ref, wv_ref, k_ref, v_ref):
    x = x_ref[0]
    h = _mod_norm(x, nw_ref[...], mod_ref[0, 0:1, :], mod_ref[0, 1:2, :])
    hb = h.astype(BF16)
    k_ref[0] = _dot(hb, wk_ref[...]).astype(BF16)
    v_ref[0] = _dot(hb, wv_ref[...]).astype(BF16)


def _ctx_proj(ctx, mod3, norm_w, w_qk, w_in):
    b, lc, d = ctx.shape
    assert V_OFF % V_WIDTH == 0
    return pl.pallas_call(
        _ctx_kernel,
        grid=(b,),
        in_specs=[pl.BlockSpec((1, lc, d), lambda i: (i, 0, 0)),
                  pl.BlockSpec((1, N_MOD, d), lambda i: (2, 0, 0)),
                  _const_spec((1, d)),
                  pl.BlockSpec((d, QK_WIDTH), lambda i: (0, 1), pipeline_mode=pl.Buffered(1)),
                  pl.BlockSpec((d, V_WIDTH), lambda i: (0, V_OFF // V_WIDTH),
                               pipeline_mode=pl.Buffered(1))],
        out_specs=[pl.BlockSpec((1, lc, QK_WIDTH), lambda i: (i, 0, 0)),
                   pl.BlockSpec((1, lc, V_WIDTH), lambda i: (i, 0, 0))],
        out_shape=[jax.ShapeDtypeStruct((b, lc, QK_WIDTH), BF16),
                   jax.ShapeDtypeStruct((b, lc, V_WIDTH), BF16)],
        compiler_params=_params(("parallel",)),
        name="ctx_proj",
    )(ctx, mod3, norm_w, w_qk, w_in)


def _proj_kernel(x_ref, mod_ref, nw_ref, wqk_ref, w_ref, rtab_ref, ctab_ref, *refs, cast_scales):
    n_cast = len(cast_scales)
    q_ref, k_ref, v_ref, sg_ref, f_ref = refs[n_cast:n_cast + 5]
    _run_casts(refs[:n_cast] + refs[n_cast + 5:], cast_scales)
    lane = lax.broadcasted_iota(jnp.int32, (1, LANES), 1)
    by_row = (lane % (QK_DIM // 2)) < QK_DIM // 4

    def rope(t, trig, scale, out_ref, rows):
        for j in range(QK_WIDTH // LANES):
            tj = t[:, j * LANES:(j + 1) * LANES]
            r = tj * trig[0] + pltpu.roll(tj, LANES // 2, 1) * trig[1]
            out_ref[rows, j * LANES:(j + 1) * LANES] = (r * scale).astype(BF16)

    for r0 in range(0, x_ref.shape[0], PROJ_SUB):
        rows = slice(r0, r0 + PROJ_SUB)
        trig = []
        for cs in range(2):
            trig.append(jnp.concatenate(
                [jnp.where(by_row, rtab_ref[cs, g:g + 1, :], ctab_ref[cs])
                 for g in range(r0 // GRID_W, (r0 + PROJ_SUB) // GRID_W)], axis=0))
        x = x_ref[rows, :]
        h = _mod_norm(x, nw_ref[...], mod_ref[0, 0:1, :], mod_ref[0, 1:2, :])
        hb = h.astype(BF16)
        rope(_dot(hb, wqk_ref[:, :QK_WIDTH]), trig, QK_DIM ** -0.5, q_ref, rows)
        rope(_dot(hb, wqk_ref[:, QK_WIDTH:]), trig, 1.0, k_ref, rows)
        v_ref[rows, :] = _dot(hb, w_ref[:, V_OFF:G_OFF]).astype(BF16)
        g = _dot(hb, w_ref[:, G_OFF:F_OFF])
        hg = 0.5 * g
        sg_ref[rows, :] = (hg * jnp.tanh(hg) + hg).astype(BF16)
        f_ref[rows, :] = _dot(hb, w_ref[:, F_OFF:IN_COLS]).astype(BF16)


def _proj(x2, mod3, norm_w, w_qk, w_in, rtab, ctab, seq, to_cast, cast_scales):
    t, d = x2.shape
    tl = PROJ_TILE
    tpb = seq // tl
    tok = lambda w: pl.BlockSpec((tl, w), lambda i: (i, 0))
    cast_specs, cast_shapes = _cast_jobs(to_cast, t // tl)
    out = pl.pallas_call(
        functools.partial(_proj_kernel, cast_scales=tuple(cast_scales)),
        grid=(t // tl,),
        in_specs=[tok(d),
                  pl.BlockSpec((1, N_MOD, d), lambda i: (i // tpb, 0, 0)),
                  _const_spec((1, d)),
                  _const_spec(w_qk.shape), _const_spec(w_in.shape),
                  pl.BlockSpec((2, tl // GRID_W, LANES), lambda i: (0, i % tpb, 0)),
                  _const_spec(ctab.shape)] + cast_specs,
        out_specs=[tok(QK_WIDTH), tok(QK_WIDTH), tok(V_WIDTH), tok(V_WIDTH),
                   tok(F_WIDTH)] + cast_specs,
        out_shape=[jax.ShapeDtypeStruct((t, QK_WIDTH), BF16),
                   jax.ShapeDtypeStruct((t, QK_WIDTH), BF16),
                   jax.ShapeDtypeStruct((t, V_WIDTH), BF16),
                   jax.ShapeDtypeStruct((t, V_WIDTH), BF16),
                   jax.ShapeDtypeStruct((t, F_WIDTH), BF16)] + cast_shapes,
        compiler_params=_params(("arbitrary",)),
        name="proj",
    )(x2, mod3, norm_w, w_qk, w_in, rtab, ctab, *to_cast)
    return out[:5], out[5:]


def _retention_kernel(af_ref, ab_ref, q_ref, k_ref, v_ref, kc_ref, vc_ref,
                      tw_ref, w2_ref, cc_ref, y_ref,
                      o_ref, z_ref, ds_scr, st_scr, d_scr, tab_scr, z_scr):
    c = RET_CHUNK
    seq = q_ref.shape[1]
    lc = kc_ref.shape[1]
    n = seq // c
    lgf = [-jnp.exp(af_ref[hh]) for hh in range(2)]
    lgb = [-jnp.exp(ab_ref[hh]) for hh in range(2)]
    half = QK_DIM // 2

    lane_head = (lax.broadcasted_iota(jnp.int32, (1, LANES), 1) // half) % 2
    masks = [(lane_head == hh).astype(BF16) for hh in range(2)]
    v_head = lax.broadcasted_iota(jnp.int32, (1, 2 * V_DIM), 1) // V_DIM
    vmasks = [(v_head == hh).astype(BF16) for hh in range(2)]
    lgf_lane = jnp.where(lane_head == 0, lgf[0], lgf[1])
    lgb_lane = jnp.where(lane_head == 0, lgb[0], lgb[1])
    row_head = (lax.broadcasted_iota(jnp.int32, (LANES, 1), 0) // half) % 2
    lgf_row = jnp.where(row_head == 0, lgf[0][:, 0:1], lgf[1][:, 0:1])
    lgb_row = jnp.where(row_head == 0, lgb[0][:, 0:1], lgb[1][:, 0:1])

    pos = lax.broadcasted_iota(jnp.int32, (c, LANES), 0).astype(F32)
    tab_scr[0] = jnp.exp(lgf_lane * (pos + 1.0))
    tab_scr[1] = jnp.exp(lgb_lane * (c - pos))
    diff = (lax.broadcasted_iota(jnp.int32, (c, c), 0)
            - lax.broadcasted_iota(jnp.int32, (c, c), 1)).astype(F32)
    for hh in range(2):
        d_scr[hh] = (jnp.where(diff >= 0, jnp.exp(lgf[hh][:, 0:1] * jnp.maximum(diff, 0.0)), 0.0)
                     + jnp.where(diff <= 0, jnp.exp(lgb[hh][:, 0:1] * jnp.maximum(-diff, 0.0)), 0.0))

    def k_decays(tokens):
        t = lax.broadcasted_iota(jnp.int32, (1, tokens), 1).astype(F32)
        return jnp.exp(lgf_row * (tokens - 1.0 - t)), jnp.exp(lgb_row * t)

    def state_increment(k_rows, v_rows, decays):
        kt = jnp.transpose(k_rows.astype(F32))
        lhs = jnp.concatenate([kt * decays[0], kt * decays[1]], axis=0).astype(BF16)
        return _dot(lhs, v_rows)

    kdec = k_decays(c)

    def incr(i, carry):
        rows = pl.ds(pl.multiple_of(i * c, c), c)
        ds_scr[i] = state_increment(k_ref[0, rows, :], v_ref[0, rows, :], kdec)
        return carry

    lax.fori_loop(0, n, incr, 0, unroll=RET_UNROLL)

    s0 = state_increment(kc_ref[0], vc_ref[0], k_decays(lc))
    col_head = lax.broadcasted_iota(jnp.int32, (1, 2 * V_DIM), 1) // V_DIM
    own = (row_head == col_head).astype(F32)
    decay_f = jnp.exp(lgf_row * c) * own
    decay_b = jnp.exp(lgb_row * c) * own

    def scan_f(i, s):
        st_scr[i, 0:LANES, :] = (s * own).astype(BF16)
        return decay_f * s + ds_scr[i, 0:LANES, :]

    lax.fori_loop(0, n, scan_f, s0[0:LANES])

    def scan_b(t, s):
        i = n - 1 - t
        st_scr[i, LANES:2 * LANES, :] = (s * own).astype(BF16)
        return decay_b * s + ds_scr[i, LANES:2 * LANES, :]

    lax.fori_loop(0, n, scan_b, s0[LANES:2 * LANES])

    def outputs(i):
        rows = pl.ds(pl.multiple_of(i * c, c), c)
        q = q_ref[0, rows, :]
        k = k_ref[0, rows, :]
        qf = q.astype(F32)
        qd = jnp.concatenate([qf * tab_scr[0], qf * tab_scr[1]], axis=1).astype(BF16)
        inter = _dot(qd, st_scr[i])
        k2 = jnp.concatenate([k * masks[0], k * masks[1]], axis=0)
        scores = lax.dot_general(q, k2, (((1,), (1,)), ((), ())), preferred_element_type=F32)
        p = (scores * jnp.concatenate([d_scr[0], d_scr[1]], axis=1)).astype(BF16)
        v = v_ref[0, rows, :]
        v2 = jnp.concatenate([v * vmasks[0], v * vmasks[1]], axis=0)
        o_ref[0, rows, :] = (_dot(p, v2) + inter).astype(BF16)

    per_k1 = n // DFT_ROWS
    cc = cc_ref[...].astype(BF16)

    def outputs_and_dft(j, carry):
        for u in range(per_k1):
            outputs(j * per_k1 + u)
        _dft_second_stage(j, tw_ref, w2_ref, cc, y_ref, z_scr)
        return carry

    lax.fori_loop(0, DFT_ROWS, outputs_and_dft, 0, unroll=RET_UNROLL // per_k1)
    _dft_emit(z_scr, z_ref)


def _retention_and_dft(a_f, a_b, q, k, v, kc, vc, y, tw, w2, w_c):
    b, seq, _ = v.shape
    lc = kc.shape[1]
    c = RET_CHUNK
    n = seq // c
    r = DFT_ROWS
    _, _, n1, n2, w = y.shape
    assert n % r == 0 and n1 // r == HEADS // 2
    dec = pl.BlockSpec((2, 1, LANES), lambda i, p: (p, 0, 0))
    qk = lambda rows: pl.BlockSpec((1, rows, LANES), lambda i, p: (i, 0, p))
    vv = lambda rows: pl.BlockSpec((1, rows, 2 * V_DIM), lambda i, p: (i, 0, p))
    return pl.pallas_call(
        _retention_kernel,
        grid=(b, HEADS // 2),
        in_specs=[dec, dec, qk(seq), qk(seq), vv(seq), qk(lc), vv(lc),
                  pl.BlockSpec((r, 2, n2), lambda i, p: (p, 0, 0)),
                  _const_spec((2, n2, n2)),
                  _const_spec((2 * F_GROUP_DIM, F_GROUP_DIM)),
                  pl.BlockSpec((1, 2, r, n2, w), lambda i, p: (i, 0, p, 0, 0))],
        out_specs=[vv(seq), pl.BlockSpec((1, n2, r, w), lambda i, p: (i, 0, p, 0))],
        out_shape=[jax.ShapeDtypeStruct((b, seq, V_WIDTH), BF16),
                   jax.ShapeDtypeStruct((b, n2, n1, w), BF16)],
        scratch_shapes=[pltpu.VMEM((n, 2 * LANES, 2 * V_DIM), F32),
                        pltpu.VMEM((n, 2 * LANES, 2 * V_DIM), BF16),
                        pltpu.VMEM((2, c, c), F32),
                        pltpu.VMEM((2, c, LANES), F32),
                        pltpu.VMEM((F_GROUPS, n2 * r, F_GROUP_DIM), F32)],
        compiler_params=_params(("parallel", "parallel")),
        name="retention_dft",
    )(a_f, a_b, q, k, v, kc, vc, tw, w2, w_c, y)


def _dft_a_kernel(w_ref, x_ref, y_ref):
    _, n1, rows, w = x_ref.shape
    r = DFT_ROWS
    nh = n1 // 2 + 1
    for r0 in range(0, rows, r):
        x = x_ref[0, :, r0:r0 + r, :].reshape(n1 * r, w)
        y = _dot(w_ref[...], x).reshape(2, nh, r, w).astype(BF16)
        y_ref[0, :, 0:nh, r0:r0 + r, :] = y
        for k1 in range(1, n1 // 2):
            y_ref[0, 0, n1 - k1, r0:r0 + r, :] = y[0, k1]
            y_ref[0, 1, n1 - k1, r0:r0 + r, :] = -y[1, k1]


def _dft_second_stage(j, tw_ref, w2_ref, cc, y_ref, z_scr):
    n2 = y_ref.shape[3]
    w2c = w2_ref[0]
    w2s = w2_ref[1]
    tc = tw_ref[j, 0:1, :]
    ts = tw_ref[j, 1:2, :]
    ec = w2c * tc - w2s * ts
    es = w2s * tc + w2c * ts
    m = jnp.concatenate([jnp.concatenate([ec, -es], axis=1),
                         jnp.concatenate([es, ec], axis=1)], axis=0).astype(BF16)
    y = jnp.concatenate([y_ref[0, 0, j], y_ref[0, 1, j]], axis=0)
    zz = _dot(m, y)
    for g in range(F_GROUPS):
        cols = slice(g * F_GROUP_DIM, (g + 1) * F_GROUP_DIM)
        zcs = jnp.concatenate([zz[:n2, cols], zz[n2:, cols]], axis=1).astype(BF16)
        z_scr[g, pl.ds(j, n2, stride=DFT_ROWS), :] = _dot(zcs, cc)


def _dft_emit(z_scr, z_ref):
    n2 = z_ref.shape[1]
    z = jnp.concatenate([z_scr[g] for g in range(F_GROUPS)], axis=1)
    z_ref[0] = z.reshape(n2, DFT_ROWS, F_WIDTH).astype(BF16)


def _dft_tables(seq):
    n1 = DFT_N1
    n2 = seq // n1

    def cs(num, den):
        ang = 2.0 * np.pi * (num % den) / den
        return np.cos(ang), np.sin(ang)

    a = np.arange(n1)
    w_a = np.concatenate(cs(a[:n1 // 2 + 1, None] * a[None, :], n1), axis=0)
    m = np.arange(n2)
    tw = np.stack(cs(a[:, None] * m[None, :], seq), axis=1)
    w2 = np.stack(cs(m[:, None] * m[None, :], n2), axis=0)
    ch = np.arange(F_GROUP_DIM)
    cc, sc = cs(ch[:, None] * ch[None, :], F_GROUP_DIM)
    scale = 1.0 / np.sqrt(seq * F_GROUP_DIM)
    w_c = np.concatenate([cc, -sc], axis=0) * scale
    return [jnp.asarray(t, dtype=F32) for t in (w_a, tw, w2, w_c)]


def _dft_first_stage(f):
    b, seq, w = f.shape
    n1 = DFT_N1
    n2 = seq // n1
    w_a, tw, w2, w_c = _dft_tables(seq)
    r = DFT_ROWS
    spread = (jnp.arange(n1 * r)[None, :] // r == jnp.arange(n1)[:, None]).astype(F32)
    w_rep = jnp.dot(w_a, spread, precision=lax.Precision.HIGHEST)
    same_r = jnp.arange(w_a.shape[0] * r)[:, None] % r == jnp.arange(n1 * r)[None, :] % r
    w_a = jnp.where(same_r, jnp.repeat(w_rep, r, axis=0), 0.0).astype(BF16)
    y = pl.pallas_call(
        _dft_a_kernel,
        grid=(b, n2 // DFT_A_ROWS),
        in_specs=[_const_spec(w_a.shape),
                  pl.BlockSpec((1, n1, DFT_A_ROWS, w), lambda i, j: (i, 0, j, 0))],
        out_specs=pl.BlockSpec((1, 2, n1, DFT_A_ROWS, w), lambda i, j: (i, 0, 0, j, 0)),
        out_shape=jax.ShapeDtypeStruct((b, 2, n1, n2, w), BF16),
        compiler_params=_params(("parallel", "parallel")),
        name="dft_a",
    )(w_a, f.reshape(b, n1, n2, w))
    return y, tw, w2, w_c


def _merge_kernel(x_ref, mod_ref, nw_ref, n2w_ref, ro_ref, sg_ref, z_ref, wbg_ref, bbg_ref,
                  wro_ref, wfo_ref, wout_ref, *refs, n_cast):
    o_ref, h2_ref = refs[n_cast:n_cast + 2]
    _run_casts(refs[:n_cast] + refs[n_cast + 2:])
    for r0 in range(0, x_ref.shape[0], MERGE_SUB):
        rows = slice(r0, r0 + MERGE_SUB)
        x = x_ref[rows, :]
        h = _mod_norm(x, nw_ref[...], mod_ref[0, 0:1, :], mod_ref[0, 1:2, :])
        t = jnp.tanh(_dot(h.astype(BF16), wbg_ref[...]) + 0.5 * bbg_ref[...])
        gated = []
        for hd in range(HEADS):
            cols = slice(hd * V_DIM, (hd + 1) * V_DIM)
            o = ro_ref[rows, cols].astype(F32)
            oc = o - jnp.mean(o, axis=-1, keepdims=True)
            var = jnp.mean(oc * oc, axis=-1, keepdims=True)
            gated.append((oc * lax.rsqrt(var + EPS) * sg_ref[rows, cols].astype(F32)).astype(BF16))
        ret_d = _dot(jnp.concatenate(gated, axis=1), wro_ref[...])
        four_d = _dot(z_ref[rows, :], wfo_ref[...])
        m2 = (t[:, :D_MODEL] * ret_d + ret_d) + (t[:, D_MODEL:] * four_d + four_d)
        y2 = _dot(m2.astype(BF16), wout_ref[...])
        x1 = x + (0.5 * mod_ref[0, 2:3, :]) * y2
        o_ref[rows, :] = x1
        h2 = _mod_norm(x1, n2w_ref[...], mod_ref[0, 3:4, :], mod_ref[0, 4:5, :])
        h2_ref[rows, :] = h2.astype(BF16)


def _merge(x2, mod3, norm_w, norm2_w, ro, sg, z, w_bg, b_bg, w_ro, w_fo, w_out, seq, to_cast):
    t, d = x2.shape
    tl = MERGE_TILE
    tpb = seq // tl
    tok = lambda w: pl.BlockSpec((tl, w), lambda i: (i, 0))
    cast_specs, cast_shapes = _cast_jobs(to_cast, t // tl)
    out = pl.pallas_call(
        functools.partial(_merge_kernel, n_cast=len(to_cast)),
        grid=(t // tl,),
        in_specs=[tok(d),
                  pl.BlockSpec((1, N_MOD, d), lambda i: (i // tpb, 0, 0)),
                  _const_spec((1, d)), _const_spec((1, d)),
                  tok(V_WIDTH), tok(V_WIDTH), tok(F_WIDTH),
                  _const_spec(w_bg.shape), _const_spec((1, 2 * d)),
                  _const_spec(w_ro.shape), _const_spec(w_fo.shape),
                  _const_spec(w_out.shape)] + cast_specs,
        out_specs=[tok(d), tok(d)] + cast_specs,
        out_shape=[jax.ShapeDtypeStruct((t, d), F32),
                   jax.ShapeDtypeStruct((t, d), BF16)] + cast_shapes,
        compiler_params=_params(("arbitrary",)),
        name="merge",
    )(x2, mod3, norm_w, norm2_w, ro, sg, z, w_bg, b_bg, w_ro, w_fo, w_out, *to_cast)
    return out[:2], out[2:]


HALO = 8
FFN_LEAD = 16
FFN_SUB = 256


def _two_gelu_tanh(x):
    c1 = np.sqrt(2.0 / np.pi)
    return x + x * jnp.tanh(x * (c1 + (c1 * 0.044715) * (x * x)))


def _ffn_kernel(xm_ref, hm_ref, xp_ref, xn_ref, mod_ref, nw_ref, wu_ref, cw_ref, cb_ref,
                wd_ref, fnw_ref, o_ref, h_scr, u_scr, act_scr, y_scr, *, tiles_per_seq):
    tl = xm_ref.shape[0]
    sub = FFN_SUB
    half = sub // 2
    nc = FFN_CHUNK
    i = pl.program_id(0)
    keep_prev = ((i % tiles_per_seq) != 0).astype(F32)
    keep_next = ((i % tiles_per_seq) != tiles_per_seq - 1).astype(F32)

    def pre(x):
        return _mod_norm(x, nw_ref[...], mod_ref[0, 3:4, :], mod_ref[0, 4:5, :])

    pad = jnp.zeros((FFN_LEAD - HALO, xm_ref.shape[1]), F32)
    h_scr[0:FFN_LEAD] = jnp.concatenate([pad, pre(xp_ref[...]) * keep_prev], axis=0).astype(BF16)
    h_scr[FFN_LEAD:FFN_LEAD + tl] = hm_ref[...]
    h_scr[FFN_LEAD + tl:] = jnp.concatenate([pre(xn_ref[...]) * keep_next, pad], axis=0).astype(BF16)

    def project(j):
        hb = h_scr[...]
        for part in range(2):
            lo = part * FFN_DIM + j * nc
            u = _dot(hb, wu_ref[:, lo:lo + nc])
            for s in range(nc // LANES):
                u_scr[j % 2, part, s] = u[:, s * LANES:(s + 1) * LANES]

    def conv(j, part, s, blk, scale):
        lo = part * FFN_DIM + j * nc + s * LANES
        w = cw_ref[:, lo:lo + LANES] * scale
        bias = cb_ref[:, lo:lo + LANES] * scale
        first = FFN_LEAD + blk * sub
        rows = lambda start: u_scr[j % 2, part, s, pl.ds(start, half, stride=2), :]
        before, even, odd, after = rows(first - 1), rows(first), rows(first + 1), rows(first + 2)
        return (before * w[0:1] + even * w[1:2] + odd * w[2:3] + bias,
                even * w[0:1] + odd * w[1:2] + after * w[2:3] + bias)

    def activate(j):
        for s in range(nc // LANES):
            cols = slice(j * nc + s * LANES, j * nc + (s + 1) * LANES)
            for blk in range(tl // sub):
                gate = conv(j, 0, s, blk, 1.0)
                val = conv(j, 1, s, blk, 0.5)
                for par in range(2):
                    r0 = blk * sub + par * half
                    act_scr[r0:r0 + half, cols] = (_two_gelu_tanh(gate[par]) * val[par]).astype(BF16)

    project(0)
    for j in range(N_FFN_CHUNKS):
        if j + 1 < N_FFN_CHUNKS:
            project(j + 1)
        activate(j)
    for blk in range(tl // sub):
        r0 = blk * sub
        y = _dot(act_scr[r0:r0 + sub, :], wd_ref[...])
        for s in range(y_scr.shape[0]):
            for par in range(2):
                y_scr[s, pl.ds(par, half, stride=2), :] = (
                    y[par * half:(par + 1) * half, s * LANES:(s + 1) * LANES])
        y = jnp.concatenate([y_scr[s] for s in range(y_scr.shape[0])], axis=1)
        x2 = xm_ref[r0:r0 + sub, :] + mod_ref[0, 5:6, :] * y
        o_ref[r0:r0 + sub, :] = _rms_norm(x2, fnw_ref[...])


def _ffn(x1, h2, mod3, norm_w, w_u, cw, cb, w_d, fnorm_w, seq):
    t, d = x1.shape
    tl = FFN_TILE
    tpb = seq // tl
    hb = tl // HALO
    last = t // HALO - 1
    rows = FFN_LEAD + tl + FFN_LEAD
    return pl.pallas_call(
        functools.partial(_ffn_kernel, tiles_per_seq=tpb),
        grid=(t // tl,),
        in_specs=[pl.BlockSpec((tl, d), lambda i: (i, 0)),
                  pl.BlockSpec((tl, d), lambda i: (i, 0)),
                  pl.BlockSpec((HALO, d), lambda i: (jnp.maximum(i * hb - 1, 0), 0)),
                  pl.BlockSpec((HALO, d), lambda i: (jnp.minimum((i + 1) * hb, last), 0)),
                  pl.BlockSpec((1, N_MOD, d), lambda i: (i // tpb, 0, 0)),
                  _const_spec((1, d)),
                  _const_spec(w_u.shape), _const_spec(cw.shape), _const_spec(cb.shape),
                  _const_spec(w_d.shape), _const_spec((1, d))],
        out_specs=pl.BlockSpec((tl, d), lambda i: (i, 0)),
        out_shape=jax.ShapeDtypeStruct((t, d), F32),
        scratch_shapes=[pltpu.VMEM((rows, d), BF16),
                        pltpu.VMEM((2, 2, FFN_CHUNK // LANES, rows, LANES), F32),
                        pltpu.VMEM((tl, FFN_DIM), BF16),
                        pltpu.VMEM((d // LANES, FFN_SUB, LANES), F32)],
        compiler_params=_params(("parallel",)),
        name="ffn",
    )(x1, h2, x1, x1, mod3, norm_w, w_u, cw, cb, w_d, fnorm_w)


def _rope_tables(seq):
    n_freq = QK_DIM // 4
    lane = jnp.arange(LANES)
    inv = ROPE_BASE ** (-(lane % n_freq).astype(F32) / n_freq)
    sign = jnp.where(lane < LANES // 2, -1.0, 1.0).astype(F32)

    def table(n):
        ang = jnp.arange(n, dtype=F32)[:, None] * inv[None, :]
        return jnp.stack([jnp.cos(ang), jnp.sin(ang) * sign])

    return table(seq // GRID_W), table(GRID_W)


def kernel(x, c, ctx, c_ctx, w_mod, b_mod, norm1_w, w_in, ret_decay_f, ret_decay_b,
           w_ret_out, w_four_out, w_branch_gate, b_branch_gate, w_out, norm2_w,
           w_up, conv_w, conv_b, w_down, final_norm_w):
    assert w_mod.shape[0] == 1, "single-layer block"
    b, seq, d = x.shape
    t = b * seq
    assert b == 2 and seq % RET_CHUNK == 0
    assert seq % FFN_TILE == 0 and seq % MERGE_TILE == 0 and seq % PROJ_TILE == 0

    c8 = jnp.concatenate([c, c_ctx[None, :], jnp.zeros((8 - b - 1, d), F32)], axis=0)
    mod, w_in_b, w_qk = _modulation(c8, w_mod[0], b_mod[0], w_in[0])
    mod3 = mod.reshape(8, N_MOD, d)
    n1w = norm1_w[0].reshape(1, d)

    kc, vc = _ctx_proj(ctx, mod3, n1w, w_qk, w_in_b)
    rtab, ctab = _rope_tables(seq)
    x2 = x.reshape(t, d)
    (q, k, v, sg, f), (w_bg, w_ro, w_fo, w_o) = _proj(
        x2, mod3, n1w, w_qk, w_in_b, rtab, ctab, seq,
        [w_branch_gate[0], w_ret_out[0], w_four_out[0], w_out[0]], (0.5, 1.0, 1.0, 1.0))

    a_f = jnp.broadcast_to(ret_decay_f[0][:, None, None], (HEADS, 1, LANES))
    a_b = jnp.broadcast_to(ret_decay_b[0][:, None, None], (HEADS, 1, LANES))
    y, tw, w2, w_c = _dft_first_stage(f.reshape(b, seq, F_WIDTH))
    ro, z = _retention_and_dft(a_f, a_b, q.reshape(b, seq, QK_WIDTH), k.reshape(b, seq, QK_WIDTH),
                               v.reshape(b, seq, V_WIDTH), kc, vc, y, tw, w2, w_c)

    n2w = norm2_w[0].reshape(1, d)
    (x1, h2), (w_u, w_d) = _merge(
        x2, mod3, n1w, n2w, ro.reshape(t, V_WIDTH), sg, z.reshape(t, F_WIDTH),
        w_bg, b_branch_gate[0].reshape(1, 2 * d), w_ro, w_fo, w_o, seq, [w_up[0], w_down[0]])

    out = _ffn(x1, h2, mod3, n2w, w_u, conv_w[0], conv_b[0].reshape(1, 2 * FFN_DIM), w_d,
               final_norm_w.reshape(1, d), seq)
    return out.reshape(b, seq, d)
```

```python
import functools

import numpy as np
import jax
import jax.numpy as jnp
from jax import lax
from jax.experimental import pallas as pl
from jax.experimental.pallas import tpu as pltpu

F32 = jnp.float32
BF16 = jnp.bfloat16

D_MODEL = 1024
GRID_W = 64
HEADS = 8
QK_DIM = 64
V_DIM = 128
QK_WIDTH = HEADS * QK_DIM
V_WIDTH = HEADS * V_DIM
ROPE_BASE = 10000.0
F_GROUPS = 4
F_GROUP_DIM = 128
F_WIDTH = F_GROUPS * F_GROUP_DIM
K_OFF = QK_WIDTH
V_OFF = K_OFF + QK_WIDTH
G_OFF = V_OFF + V_WIDTH
F_OFF = G_OFF + V_WIDTH
IN_COLS = F_OFF + F_WIDTH
FFN_DIM = 2816
N_MOD = 6
EPS = 1e-6

LANES = 128
RET_CHUNK = 256
RET_UNROLL = 16
FFN_CHUNK = 256
N_FFN_CHUNKS = FFN_DIM // FFN_CHUNK
FFN_TILE = 512
MERGE_TILE = 1024
MERGE_SUB = 512
PROJ_TILE = 1024
PROJ_SUB = 256
MOD_STEPS = 2
DFT_N1 = 64
BF16_ROWS = 16
DFT_ROWS = BF16_ROWS
DFT_A_ROWS = 2 * DFT_ROWS
DFT_PAD = 8
VMEM_LIMIT = 56 * 1024 * 1024


def _params(sem):
    return pltpu.CompilerParams(dimension_semantics=sem, vmem_limit_bytes=VMEM_LIMIT)


def _dot(a, b):
    return jnp.dot(a, b, preferred_element_type=F32)


def _rms_norm(x, w):
    return x * lax.rsqrt(jnp.mean(x * x, axis=-1, keepdims=True) + EPS) * w


def _mod_norm(x, w, shift, scale):
    return x * lax.rsqrt(jnp.mean(x * x, axis=-1, keepdims=True) + EPS) * (w * (1.0 + scale)) + shift


def _const_spec(shape):
    zeros = (0,) * len(shape)
    return pl.BlockSpec(shape, lambda *_: zeros, pipeline_mode=pl.Buffered(1))


def _cast_jobs(weights, steps):
    specs, shapes = [], []
    for w in weights:
        n_rows = w.shape[0]
        rows = -(-n_rows // steps)
        while rows % BF16_ROWS or n_rows % rows:
            rows += 1
        last = n_rows // rows - 1
        specs.append(pl.BlockSpec((rows, w.shape[1]), lambda i, last=last: (jnp.minimum(i, last), 0)))
        shapes.append(jax.ShapeDtypeStruct(w.shape, BF16))
    return specs, shapes


def _run_casts(refs, scales=None):
    n = len(refs) // 2
    for j, (src, dst) in enumerate(zip(refs[:n], refs[n:])):
        scale = 1.0 if scales is None else scales[j]
        dst[...] = (src[...] if scale == 1.0 else src[...] * scale).astype(BF16)


def _mod_kernel(c_ref, w_ref, b_ref, win_ref, o_ref, winb_ref, wqk_ref):
    c = c_ref[...]
    s = c * jax.nn.sigmoid(c)
    o_ref[...] = _dot(s.astype(BF16), w_ref[...].astype(BF16)) + b_ref[...]
    wb = win_ref[...].astype(BF16)
    winb_ref[...] = wb
    n_qk = 2 * QK_WIDTH
    src = lax.broadcasted_iota(jnp.int32, (n_qk, n_qk), 0)
    dst = lax.broadcasted_iota(jnp.int32, (n_qk, n_qk), 1)
    select = (src == _qk_source_column(dst)).astype(BF16)
    wqk_ref[...] = _dot(wb[:, :n_qk], select).astype(BF16)


def _qk_source_column(col):
    half = QK_DIM // 2
    lane = col % LANES
    head = 2 * ((col % QK_WIDTH) // LANES) + (lane // half) % 2
    return (col // QK_WIDTH) * QK_WIDTH + head * QK_DIM + (lane // (2 * half)) * half + lane % half


def _modulation(c8, w_mod, b_mod, w_in):
    n = w_mod.shape[1]
    steps = MOD_STEPS
    tn = n // steps
    d, n_in = w_in.shape
    rows = d // steps
    return pl.pallas_call(
        _mod_kernel,
        grid=(steps,),
        in_specs=[_const_spec((8, D_MODEL)),
                  pl.BlockSpec((D_MODEL, tn), lambda j: (0, j)),
                  pl.BlockSpec((1, tn), lambda j: (0, j)),
                  pl.BlockSpec((rows, n_in), lambda j: (j, 0))],
        out_specs=[pl.BlockSpec((8, tn), lambda j: (0, j)),
                   pl.BlockSpec((rows, n_in), lambda j: (j, 0)),
                   pl.BlockSpec((rows, 2 * QK_WIDTH), lambda j: (j, 0))],
        out_shape=[jax.ShapeDtypeStruct((8, n), F32),
                   jax.ShapeDtypeStruct((d, n_in), BF16),
                   jax.ShapeDtypeStruct((d, 2 * QK_WIDTH), BF16)],
        compiler_params=_params(("parallel",)),
        name="mod",
    )(c8, w_mod, b_mod.reshape(1, n), w_in)


def _ctx_kernel(x_ref, mod_ref, nw_ref, wk_ref, wv_ref, k_ref, v_ref):
    x = x_ref[0]
    h = _mod_norm(x, nw_ref[...], mod_ref[0, 0:1, :], mod_ref[0, 1:2, :])
    hb = h.astype(BF16)
    k_ref[0] = _dot(hb, wk_ref[...]).astype(BF16)
    v_ref[0] = _dot(hb, wv_ref[...]).astype(BF16)


def _ctx_proj(ctx, mod3, norm_w, w_qk, w_in):
    b, lc, d = ctx.shape
    assert V_OFF % V_WIDTH == 0
    return pl.pallas_call(
        _ctx_kernel,
        grid=(b,),
        in_specs=[pl.BlockSpec((1, lc, d), lambda i: (i, 0, 0)),
                  pl.BlockSpec((1, N_MOD, d), lambda i: (2, 0, 0)),
                  _const_spec((1, d)),
                  pl.BlockSpec((d, QK_WIDTH), lambda i: (0, 1), pipeline_mode=pl.Buffered(1)),
                  pl.BlockSpec((d, V_WIDTH), lambda i: (0, V_OFF // V_WIDTH),
                               pipeline_mode=pl.Buffered(1))],
        out_specs=[pl.BlockSpec((1, lc, QK_WIDTH), lambda i: (i, 0, 0)),
                   pl.BlockSpec((1, lc, V_WIDTH), lambda i: (i, 0, 0))],
        out_shape=[jax.ShapeDtypeStruct((b, lc, QK_WIDTH), BF16),
                   jax.ShapeDtypeStruct((b, lc, V_WIDTH), BF16)],
        compiler_params=_params(("parallel",)),
        name="ctx_proj",
    )(ctx, mod3, norm_w, w_qk, w_in)


def _proj_kernel(x_ref, mod_ref, nw_ref, wqk_ref, w_ref, rtab_ref, ctab_ref, *refs, cast_scales):
    n_cast = len(cast_scales)
    q_ref, k_ref, v_ref, sg_ref, f_ref = refs[n_cast:n_cast + 5]
    _run_casts(refs[:n_cast] + refs[n_cast + 5:], cast_scales)
    lane = lax.broadcasted_iota(jnp.int32, (1, LANES), 1)
    by_row = (lane % (QK_DIM // 2)) < QK_DIM // 4

    def rope(t, trig, scale, out_ref, rows):
        for j in range(QK_WIDTH // LANES):
            tj = t[:, j * LANES:(j + 1) * LANES]
            r = tj * trig[0] + pltpu.roll(tj, LANES // 2, 1) * trig[1]
            out_ref[rows, j * LANES:(j + 1) * LANES] = (r * scale).astype(BF16)

    for r0 in range(0, x_ref.shape[0], PROJ_SUB):
        rows = slice(r0, r0 + PROJ_SUB)
        trig = []
        for cs in range(2):
            trig.append(jnp.concatenate(
                [jnp.where(by_row, rtab_ref[cs, g:g + 1, :], ctab_ref[cs])
                 for g in range(r0 // GRID_W, (r0 + PROJ_SUB) // GRID_W)], axis=0))
        x = x_ref[rows, :]
        h = _mod_norm(x, nw_ref[...], mod_ref[0, 0:1, :], mod_ref[0, 1:2, :])
        hb = h.astype(BF16)
        rope(_dot(hb, wqk_ref[:, :QK_WIDTH]), trig, QK_DIM ** -0.5, q_ref, rows)
        rope(_dot(hb, wqk_ref[:, QK_WIDTH:]), trig, 1.0, k_ref, rows)
        v_ref[rows, :] = _dot(hb, w_ref[:, V_OFF:G_OFF]).astype(BF16)
        g = _dot(hb, w_ref[:, G_OFF:F_OFF])
        hg = 0.5 * g
        sg_ref[rows, :] = (hg * jnp.tanh(hg) + hg).astype(BF16)
        f_ref[rows, :] = _dot(hb, w_ref[:, F_OFF:IN_COLS]).astype(BF16)


def _proj(x2, mod3, norm_w, w_qk, w_in, rtab, ctab, seq, to_cast, cast_scales):
    t, d = x2.shape
    tl = PROJ_TILE
    tpb = seq // tl
    tok = lambda w: pl.BlockSpec((tl, w), lambda i: (i, 0))
    cast_specs, cast_shapes = _cast_jobs(to_cast, t // tl)
    out = pl.pallas_call(
        functools.partial(_proj_kernel, cast_scales=tuple(cast_scales)),
        grid=(t // tl,),
        in_specs=[tok(d),
                  pl.BlockSpec((1, N_MOD, d), lambda i: (i // tpb, 0, 0)),
                  _const_spec((1, d)),
                  _const_spec(w_qk.shape), _const_spec(w_in.shape),
                  pl.BlockSpec((2, tl // GRID_W, LANES), lambda i: (0, i % tpb, 0)),
                  _const_spec(ctab.shape)] + cast_specs,
        out_specs=[tok(QK_WIDTH), tok(QK_WIDTH), tok(V_WIDTH), tok(V_WIDTH),
                   tok(F_WIDTH)] + cast_specs,
        out_shape=[jax.ShapeDtypeStruct((t, QK_WIDTH), BF16),
                   jax.ShapeDtypeStruct((t, QK_WIDTH), BF16),
                   jax.ShapeDtypeStruct((t, V_WIDTH), BF16),
                   jax.ShapeDtypeStruct((t, V_WIDTH), BF16),
                   jax.ShapeDtypeStruct((t, F_WIDTH), BF16)] + cast_shapes,
        compiler_params=_params(("arbitrary",)),
        name="proj",
    )(x2, mod3, norm_w, w_qk, w_in, rtab, ctab, *to_cast)
    return out[:5], out[5:]


def _retention_kernel(af_ref, ab_ref, q_ref, k_ref, v_ref, kc_ref, vc_ref,
                      tw_ref, w2_ref, cc_ref, y_ref,
                      o_ref, z_ref, ds_scr, st_scr, d_scr, tab_scr, z_scr):
    c = RET_CHUNK
    seq = q_ref.shape[1]
    lc = kc_ref.shape[1]
    n = seq // c
    lgf = [-jnp.exp(af_ref[hh]) for hh in range(2)]
    lgb = [-jnp.exp(ab_ref[hh]) for hh in range(2)]
    half = QK_DIM // 2

    lane_head = (lax.broadcasted_iota(jnp.int32, (1, LANES), 1) // half) % 2
    masks = [(lane_head == hh).astype(BF16) for hh in range(2)]
    v_head = lax.broadcasted_iota(jnp.int32, (1, 2 * V_DIM), 1) // V_DIM
    vmasks = [(v_head == hh).astype(BF16) for hh in range(2)]
    lgf_lane = jnp.where(lane_head == 0, lgf[0], lgf[1])
    lgb_lane = jnp.where(lane_head == 0, lgb[0], lgb[1])
    row_head = (lax.broadcasted_iota(jnp.int32, (LANES, 1), 0) // half) % 2
    lgf_row = jnp.where(row_head == 0, lgf[0][:, 0:1], lgf[1][:, 0:1])
    lgb_row = jnp.where(row_head == 0, lgb[0][:, 0:1], lgb[1][:, 0:1])

    pos = lax.broadcasted_iota(jnp.int32, (c, LANES), 0).astype(F32)
    tab_scr[0] = jnp.exp(lgf_lane * (pos + 1.0))
    tab_scr[1] = jnp.exp(lgb_lane * (c - pos))
    diff = (lax.broadcasted_iota(jnp.int32, (c, c), 0)
            - lax.broadcasted_iota(jnp.int32, (c, c), 1)).astype(F32)
    for hh in range(2):
        d_scr[hh] = (jnp.where(diff >= 0, jnp.exp(lgf[hh][:, 0:1] * jnp.maximum(diff, 0.0)), 0.0)
                     + jnp.where(diff <= 0, jnp.exp(lgb[hh][:, 0:1] * jnp.maximum(-diff, 0.0)), 0.0))

    def k_decays(tokens):
        t = lax.broadcasted_iota(jnp.int32, (1, tokens), 1).astype(F32)
        return jnp.exp(lgf_row * (tokens - 1.0 - t)), jnp.exp(lgb_row * t)

    def state_increment(k_rows, v_rows, decays):
        kt = jnp.transpose(k_rows.astype(F32))
        lhs = jnp.concatenate([kt * decays[0], kt * decays[1]], axis=0).astype(BF16)
        return _dot(lhs, v_rows)

    kdec = k_decays(c)

    def incr(i, carry):
        rows = pl.ds(pl.multiple_of(i * c, c), c)
        ds_scr[i] = state_increment(k_ref[0, rows, :], v_ref[0, rows, :], kdec)
        return carry

    lax.fori_loop(0, n, incr, 0, unroll=RET_UNROLL)

    s0 = state_increment(kc_ref[0], vc_ref[0], k_decays(lc))
    col_head = lax.broadcasted_iota(jnp.int32, (1, 2 * V_DIM), 1) // V_DIM
    own = (row_head == col_head).astype(F32)
    decay_f = jnp.exp(lgf_row * c) * own
    decay_b = jnp.exp(lgb_row * c) * own

    def scan_f(i, s):
        st_scr[i, 0:LANES, :] = (s * own).astype(BF16)
        return decay_f * s + ds_scr[i, 0:LANES, :]

    lax.fori_loop(0, n, scan_f, s0[0:LANES])

    def scan_b(t, s):
        i = n - 1 - t
        st_scr[i, LANES:2 * LANES, :] = (s * own).astype(BF16)
        return decay_b * s + ds_scr[i, LANES:2 * LANES, :]

    lax.fori_loop(0, n, scan_b, s0[LANES:2 * LANES])

    def outputs(i):
        rows = pl.ds(pl.multiple_of(i * c, c), c)
        q = q_ref[0, rows, :]
        k = k_ref[0, rows, :]
        qf = q.astype(F32)
        qd = jnp.concatenate([qf * tab_scr[0], qf * tab_scr[1]], axis=1).astype(BF16)
        inter = _dot(qd, st_scr[i])
        k2 = jnp.concatenate([k * masks[0], k * masks[1]], axis=0)
        scores = lax.dot_general(q, k2, (((1,), (1,)), ((), ())), preferred_element_type=F32)
        p = (scores * jnp.concatenate([d_scr[0], d_scr[1]], axis=1)).astype(BF16)
        v = v_ref[0, rows, :]
        v2 = jnp.concatenate([v * vmasks[0], v * vmasks[1]], axis=0)
        o_ref[0, rows, :] = (_dot(p, v2) + inter).astype(BF16)

    per_k1 = n // DFT_ROWS
    cc = cc_ref[...].astype(BF16)

    def outputs_and_dft(j, carry):
        for u in range(per_k1):
            outputs(j * per_k1 + u)
        _dft_second_stage(j, tw_ref, w2_ref, cc, y_ref, z_scr)
        return carry

    lax.fori_loop(0, DFT_ROWS, outputs_and_dft, 0, unroll=RET_UNROLL // per_k1)
    _dft_emit(z_scr, z_ref)


def _retention_and_dft(a_f, a_b, q, k, v, kc, vc, y, tw, w2, w_c):
    b, seq, _ = v.shape
    lc = kc.shape[1]
    c = RET_CHUNK
    n = seq // c
    r = DFT_ROWS
    _, _, n1, n2, w = y.shape
    assert n % r == 0 and n1 // r == HEADS // 2
    dec = pl.BlockSpec((2, 1, LANES), lambda i, p: (p, 0, 0))
    qk = lambda rows: pl.BlockSpec((1, rows, LANES), lambda i, p: (i, 0, p))
    vv = lambda rows: pl.BlockSpec((1, rows, 2 * V_DIM), lambda i, p: (i, 0, p))
    return pl.pallas_call(
        _retention_kernel,
        grid=(b, HEADS // 2),
        in_specs=[dec, dec, qk(seq), qk(seq), vv(seq), qk(lc), vv(lc),
                  pl.BlockSpec((r, 2, n2), lambda i, p: (p, 0, 0)),
                  _const_spec((2, n2, n2)),
                  _const_spec((2 * F_GROUP_DIM, F_GROUP_DIM)),
                  pl.BlockSpec((1, 2, r, n2, w), lambda i, p: (i, 0, p, 0, 0))],
        out_specs=[vv(seq), pl.BlockSpec((1, n2, r, w), lambda i, p: (i, 0, p, 0))],
        out_shape=[jax.ShapeDtypeStruct((b, seq, V_WIDTH), BF16),
                   jax.ShapeDtypeStruct((b, n2, n1, w), BF16)],
        scratch_shapes=[pltpu.VMEM((n, 2 * LANES, 2 * V_DIM), F32),
                        pltpu.VMEM((n, 2 * LANES, 2 * V_DIM), BF16),
                        pltpu.VMEM((2, c, c), F32),
                        pltpu.VMEM((2, c, LANES), F32),
                        pltpu.VMEM((F_GROUPS, (n2 + DFT_PAD) * r, F_GROUP_DIM), F32)],
        compiler_params=_params(("parallel", "parallel")),
        name="retention_dft",
    )(a_f, a_b, q, k, v, kc, vc, tw, w2, w_c, y)


def _dft_a_kernel(w_ref, x_ref, y_ref):
    _, n1, rows, w = x_ref.shape
    r = DFT_ROWS
    nh = n1 // 2 + 1
    for r0 in range(0, rows, r):
        x = x_ref[0, :, r0:r0 + r, :].reshape(n1 * r, w)
        y = _dot(w_ref[...], x).reshape(2, nh, r, w).astype(BF16)
        y_ref[0, :, 0:nh, r0:r0 + r, :] = y
        for k1 in range(1, n1 // 2):
            y_ref[0, 0, n1 - k1, r0:r0 + r, :] = y[0, k1]
            y_ref[0, 1, n1 - k1, r0:r0 + r, :] = -y[1, k1]


def _dft_second_stage(j, tw_ref, w2_ref, cc, y_ref, z_scr):
    n2 = y_ref.shape[3]
    w2c = w2_ref[0]
    w2s = w2_ref[1]
    tc = tw_ref[j, 0:1, :]
    ts = tw_ref[j, 1:2, :]
    ec = w2c * tc - w2s * ts
    es = w2s * tc + w2c * ts
    m = jnp.concatenate([jnp.concatenate([ec, -es], axis=1),
                         jnp.concatenate([es, ec], axis=1)], axis=0).astype(BF16)
    y = jnp.concatenate([y_ref[0, 0, j], y_ref[0, 1, j]], axis=0)
    zz = _dot(m, y)
    for g in range(F_GROUPS):
        cols = slice(g * F_GROUP_DIM, (g + 1) * F_GROUP_DIM)
        zcs = jnp.concatenate([zz[:n2, cols], zz[n2:, cols]], axis=1).astype(BF16)
        z_scr[g, pl.ds(pl.multiple_of(j * (n2 + DFT_PAD), DFT_PAD), n2), :] = _dot(zcs, cc)


def _dft_emit(z_scr, z_ref):
    n2 = z_ref.shape[1]
    pitch = n2 + DFT_PAD
    z = jnp.concatenate(
        [jnp.concatenate([z_scr[g, pl.ds(k2, DFT_ROWS, stride=pitch), :] for k2 in range(n2)], axis=0)
         for g in range(F_GROUPS)], axis=1)
    z_ref[0] = z.reshape(n2, DFT_ROWS, F_WIDTH).astype(BF16)


def _dft_tables(seq):
    n1 = DFT_N1
    n2 = seq // n1

    def cs(num, den):
        ang = 2.0 * np.pi * (num % den) / den
        return np.cos(ang), np.sin(ang)

    a = np.arange(n1)
    w_a = np.concatenate(cs(a[:n1 // 2 + 1, None] * a[None, :], n1), axis=0)
    m = np.arange(n2)
    tw = np.stack(cs(a[:, None] * m[None, :], seq), axis=1)
    w2 = np.stack(cs(m[:, None] * m[None, :], n2), axis=0)
    ch = np.arange(F_GROUP_DIM)
    cc, sc = cs(ch[:, None] * ch[None, :], F_GROUP_DIM)
    scale = 1.0 / np.sqrt(seq * F_GROUP_DIM)
    w_c = np.concatenate([cc, -sc], axis=0) * scale
    return [jnp.asarray(t, dtype=F32) for t in (w_a, tw, w2, w_c)]


def _dft_first_stage(f):
    b, seq, w = f.shape
    n1 = DFT_N1
    n2 = seq // n1
    w_a, tw, w2, w_c = _dft_tables(seq)
    r = DFT_ROWS
    spread = (jnp.arange(n1 * r)[None, :] // r == jnp.arange(n1)[:, None]).astype(F32)
    w_rep = jnp.dot(w_a, spread, precision=lax.Precision.HIGHEST)
    same_r = jnp.arange(w_a.shape[0] * r)[:, None] % r == jnp.arange(n1 * r)[None, :] % r
    w_a = jnp.where(same_r, jnp.repeat(w_rep, r, axis=0), 0.0).astype(BF16)
    y = pl.pallas_call(
        _dft_a_kernel,
        grid=(b, n2 // DFT_A_ROWS),
        in_specs=[_const_spec(w_a.shape),
                  pl.BlockSpec((1, n1, DFT_A_ROWS, w), lambda i, j: (i, 0, j, 0))],
        out_specs=pl.BlockSpec((1, 2, n1, DFT_A_ROWS, w), lambda i, j: (i, 0, 0, j, 0)),
        out_shape=jax.ShapeDtypeStruct((b, 2, n1, n2, w), BF16),
        compiler_params=_params(("parallel", "parallel")),
        name="dft_a",
    )(w_a, f.reshape(b, n1, n2, w))
    return y, tw, w2, w_c


def _merge_kernel(x_ref, mod_ref, nw_ref, n2w_ref, ro_ref, sg_ref, z_ref, wbg_ref, bbg_ref,
                  wro_ref, wfo_ref, wout_ref, *refs, n_cast):
    o_ref, h2_ref = refs[n_cast:n_cast + 2]
    _run_casts(refs[:n_cast] + refs[n_cast + 2:])
    for r0 in range(0, x_ref.shape[0], MERGE_SUB):
        rows = slice(r0, r0 + MERGE_SUB)
        x = x_ref[rows, :]
        h = _mod_norm(x, nw_ref[...], mod_ref[0, 0:1, :], mod_ref[0, 1:2, :])
        t = jnp.tanh(_dot(h.astype(BF16), wbg_ref[...]) + 0.5 * bbg_ref[...])
        gated = []
        for hd in range(HEADS):
            cols = slice(hd * V_DIM, (hd + 1) * V_DIM)
            o = ro_ref[rows, cols].astype(F32)
            oc = o - jnp.mean(o, axis=-1, keepdims=True)
            var = jnp.mean(oc * oc, axis=-1, keepdims=True)
            gated.append((oc * lax.rsqrt(var + EPS) * sg_ref[rows, cols].astype(F32)).astype(BF16))
        ret_d = _dot(jnp.concatenate(gated, axis=1), wro_ref[...])
        four_d = _dot(z_ref[rows, :], wfo_ref[...])
        m2 = (t[:, :D_MODEL] * ret_d + ret_d) + (t[:, D_MODEL:] * four_d + four_d)
        y2 = _dot(m2.astype(BF16), wout_ref[...])
        x1 = x + (0.5 * mod_ref[0, 2:3, :]) * y2
        o_ref[rows, :] = x1
        h2 = _mod_norm(x1, n2w_ref[...], mod_ref[0, 3:4, :], mod_ref[0, 4:5, :])
        h2_ref[rows, :] = h2.astype(BF16)


def _merge(x2, mod3, norm_w, norm2_w, ro, sg, z, w_bg, b_bg, w_ro, w_fo, w_out, seq, to_cast):
    t, d = x2.shape
    tl = MERGE_TILE
    tpb = seq // tl
    tok = lambda w: pl.BlockSpec((tl, w), lambda i: (i, 0))
    cast_specs, cast_shapes = _cast_jobs(to_cast, t // tl)
    out = pl.pallas_call(
        functools.partial(_merge_kernel, n_cast=len(to_cast)),
        grid=(t // tl,),
        in_specs=[tok(d),
                  pl.BlockSpec((1, N_MOD, d), lambda i: (i // tpb, 0, 0)),
                  _const_spec((1, d)), _const_spec((1, d)),
                  tok(V_WIDTH), tok(V_WIDTH), tok(F_WIDTH),
                  _const_spec(w_bg.shape), _const_spec((1, 2 * d)),
                  _const_spec(w_ro.shape), _const_spec(w_fo.shape),
                  _const_spec(w_out.shape)] + cast_specs,
        out_specs=[tok(d), tok(d)] + cast_specs,
        out_shape=[jax.ShapeDtypeStruct((t, d), F32),
                   jax.ShapeDtypeStruct((t, d), BF16)] + cast_shapes,
        compiler_params=_params(("arbitrary",)),
        name="merge",
    )(x2, mod3, norm_w, norm2_w, ro, sg, z, w_bg, b_bg, w_ro, w_fo, w_out, *to_cast)
    return out[:2], out[2:]


HALO = 8
FFN_LEAD = 16
FFN_SUB = 256


def _two_gelu_tanh(x):
    c1 = np.sqrt(2.0 / np.pi)
    return x + x * jnp.tanh(x * (c1 + (c1 * 0.044715) * (x * x)))


def _ffn_kernel(xm_ref, hm_ref, xp_ref, xn_ref, mod_ref, nw_ref, wu_ref, cw_ref, cb_ref,
                wd_ref, fnw_ref, o_ref, h_scr, u_scr, act_scr, y_scr, *, tiles_per_seq):
    tl = xm_ref.shape[0]
    sub = FFN_SUB
    half = sub // 2
    nc = FFN_CHUNK
    i = pl.program_id(0)
    keep_prev = ((i % tiles_per_seq) != 0).astype(F32)
    keep_next = ((i % tiles_per_seq) != tiles_per_seq - 1).astype(F32)

    def pre(x):
        return _mod_norm(x, nw_ref[...], mod_ref[0, 3:4, :], mod_ref[0, 4:5, :])

    pad = jnp.zeros((FFN_LEAD - HALO, xm_ref.shape[1]), F32)
    h_scr[0:FFN_LEAD] = jnp.concatenate([pad, pre(xp_ref[...]) * keep_prev], axis=0).astype(BF16)
    h_scr[FFN_LEAD:FFN_LEAD + tl] = hm_ref[...]
    h_scr[FFN_LEAD + tl:] = jnp.concatenate([pre(xn_ref[...]) * keep_next, pad], axis=0).astype(BF16)

    def project(j):
        hb = h_scr[...]
        for part in range(2):
            lo = part * FFN_DIM + j * nc
            u = _dot(hb, wu_ref[:, lo:lo + nc])
            for s in range(nc // LANES):
                u_scr[j % 2, part, s] = u[:, s * LANES:(s + 1) * LANES]

    def conv(j, part, s, blk, scale):
        lo = part * FFN_DIM + j * nc + s * LANES
        w = cw_ref[:, lo:lo + LANES] * scale
        bias = cb_ref[:, lo:lo + LANES] * scale
        first = FFN_LEAD + blk * sub
        rows = lambda start: u_scr[j % 2, part, s, pl.ds(start, half, stride=2), :]
        before, even, odd, after = rows(first - 1), rows(first), rows(first + 1), rows(first + 2)
        return (before * w[0:1] + even * w[1:2] + odd * w[2:3] + bias,
                even * w[0:1] + odd * w[1:2] + after * w[2:3] + bias)

    def activate(j):
        for s in range(nc // LANES):
            cols = slice(j * nc + s * LANES, j * nc + (s + 1) * LANES)
            for blk in range(tl // sub):
                gate = conv(j, 0, s, blk, 1.0)
                val = conv(j, 1, s, blk, 0.5)
                for par in range(2):
                    r0 = blk * sub + par * half
                    act_scr[r0:r0 + half, cols] = (_two_gelu_tanh(gate[par]) * val[par]).astype(BF16)

    project(0)
    for j in range(N_FFN_CHUNKS):
        if j + 1 < N_FFN_CHUNKS:
            project(j + 1)
        activate(j)
    for blk in range(tl // sub):
        r0 = blk * sub
        y = _dot(act_scr[r0:r0 + sub, :], wd_ref[...])
        for s in range(y_scr.shape[0]):
            for par in range(2):
                y_scr[s, pl.ds(par, half, stride=2), :] = (
                    y[par * half:(par + 1) * half, s * LANES:(s + 1) * LANES])
        y = jnp.concatenate([y_scr[s] for s in range(y_scr.shape[0])], axis=1)
        x2 = xm_ref[r0:r0 + sub, :] + mod_ref[0, 5:6, :] * y
        o_ref[r0:r0 + sub, :] = _rms_norm(x2, fnw_ref[...])


def _ffn(x1, h2, mod3, norm_w, w_u, cw, cb, w_d, fnorm_w, seq):
    t, d = x1.shape
    tl = FFN_TILE
    tpb = seq // tl
    hb = tl // HALO
    last = t // HALO - 1
    rows = FFN_LEAD + tl + FFN_LEAD
    return pl.pallas_call(
        functools.partial(_ffn_kernel, tiles_per_seq=tpb),
        grid=(t // tl,),
        in_specs=[pl.BlockSpec((tl, d), lambda i: (i, 0)),
                  pl.BlockSpec((tl, d), lambda i: (i, 0)),
                  pl.BlockSpec((HALO, d), lambda i: (jnp.maximum(i * hb - 1, 0), 0)),
                  pl.BlockSpec((HALO, d), lambda i: (jnp.minimum((i + 1) * hb, last), 0)),
                  pl.BlockSpec((1, N_MOD, d), lambda i: (i // tpb, 0, 0)),
                  _const_spec((1, d)),
                  _const_spec(w_u.shape), _const_spec(cw.shape), _const_spec(cb.shape),
                  _const_spec(w_d.shape), _const_spec((1, d))],
        out_specs=pl.BlockSpec((tl, d), lambda i: (i, 0)),
        out_shape=jax.ShapeDtypeStruct((t, d), F32),
        scratch_shapes=[pltpu.VMEM((rows, d), BF16),
                        pltpu.VMEM((2, 2, FFN_CHUNK // LANES, rows, LANES), F32),
                        pltpu.VMEM((tl, FFN_DIM), BF16),
                        pltpu.VMEM((d // LANES, FFN_SUB, LANES), F32)],
        compiler_params=_params(("parallel",)),
        name="ffn",
    )(x1, h2, x1, x1, mod3, norm_w, w_u, cw, cb, w_d, fnorm_w)


def _rope_tables(seq):
    n_freq = QK_DIM // 4
    lane = jnp.arange(LANES)
    inv = ROPE_BASE ** (-(lane % n_freq).astype(F32) / n_freq)
    sign = jnp.where(lane < LANES // 2, -1.0, 1.0).astype(F32)

    def table(n):
        ang = jnp.arange(n, dtype=F32)[:, None] * inv[None, :]
        return jnp.stack([jnp.cos(ang), jnp.sin(ang) * sign])

    return table(seq // GRID_W), table(GRID_W)


def kernel(x, c, ctx, c_ctx, w_mod, b_mod, norm1_w, w_in, ret_decay_f, ret_decay_b,
           w_ret_out, w_four_out, w_branch_gate, b_branch_gate, w_out, norm2_w,
           w_up, conv_w, conv_b, w_down, final_norm_w):
    assert w_mod.shape[0] == 1, "single-layer block"
    b, seq, d = x.shape
    t = b * seq
    assert b == 2 and seq % RET_CHUNK == 0
    assert seq % FFN_TILE == 0 and seq % MERGE_TILE == 0 and seq % PROJ_TILE == 0

    c8 = jnp.concatenate([c, c_ctx[None, :], jnp.zeros((8 - b - 1, d), F32)], axis=0)
    mod, w_in_b, w_qk = _modulation(c8, w_mod[0], b_mod[0], w_in[0])
    mod3 = mod.reshape(8, N_MOD, d)
    n1w = norm1_w[0].reshape(1, d)

    kc, vc = _ctx_proj(ctx, mod3, n1w, w_qk, w_in_b)
    rtab, ctab = _rope_tables(seq)
    x2 = x.reshape(t, d)
    (q, k, v, sg, f), (w_bg, w_ro, w_fo, w_o) = _proj(
        x2, mod3, n1w, w_qk, w_in_b, rtab, ctab, seq,
        [w_branch_gate[0], w_ret_out[0], w_four_out[0], w_out[0]], (0.5, 1.0, 1.0, 1.0))

    a_f = jnp.broadcast_to(ret_decay_f[0][:, None, None], (HEADS, 1, LANES))
    a_b = jnp.broadcast_to(ret_decay_b[0][:, None, None], (HEADS, 1, LANES))
    y, tw, w2, w_c = _dft_first_stage(f.reshape(b, seq, F_WIDTH))
    ro, z = _retention_and_dft(a_f, a_b, q.reshape(b, seq, QK_WIDTH), k.reshape(b, seq, QK_WIDTH),
                               v.reshape(b, seq, V_WIDTH), kc, vc, y, tw, w2, w_c)

    n2w = norm2_w[0].reshape(1, d)
    (x1, h2), (w_u, w_d) = _merge(
        x2, mod3, n1w, n2w, ro.reshape(t, V_WIDTH), sg, z.reshape(t, F_WIDTH),
        w_bg, b_branch_gate[0].reshape(1, 2 * d), w_ro, w_fo, w_o, seq, [w_up[0], w_down[0]])

    out = _ffn(x1, h2, mod3, n2w, w_u, conv_w[0], conv_b[0].reshape(1, 2 * FFN_DIM), w_d,
               final_norm_w.reshape(1, d), seq)
    return out.reshape(b, seq, d)
```

```python
import functools

import numpy as np
import jax
import jax.numpy as jnp
from jax import lax
from jax.experimental import pallas as pl
from jax.experimental.pallas import tpu as pltpu

F32 = jnp.float32
BF16 = jnp.bfloat16

D_MODEL = 1024
GRID_W = 64
HEADS = 8
QK_DIM = 64
V_DIM = 128
QK_WIDTH = HEADS * QK_DIM
V_WIDTH = HEADS * V_DIM
ROPE_BASE = 10000.0
F_GROUPS = 4
F_GROUP_DIM = 128
F_WIDTH = F_GROUPS * F_GROUP_DIM
K_OFF = QK_WIDTH
V_OFF = K_OFF + QK_WIDTH
G_OFF = V_OFF + V_WIDTH
F_OFF = G_OFF + V_WIDTH
IN_COLS = F_OFF + F_WIDTH
FFN_DIM = 2816
N_MOD = 6
EPS = 1e-6

LANES = 128
RET_CHUNK = 256
RET_UNROLL = 16
FFN_CHUNK = 256
N_FFN_CHUNKS = FFN_DIM // FFN_CHUNK
FFN_TILE = 512
MERGE_TILE = 1024
MERGE_SUB = 512
PROJ_TILE = 1024
PROJ_SUB = 256
MOD_STEPS = 2
DFT_N1 = 64
BF16_ROWS = 16
DFT_ROWS = BF16_ROWS
DFT_A_ROWS = 2 * DFT_ROWS
DFT_PAD = 8
VMEM_LIMIT = 56 * 1024 * 1024


def _params(sem):
    return pltpu.CompilerParams(dimension_semantics=sem, vmem_limit_bytes=VMEM_LIMIT)


def _dot(a, b):
    return jnp.dot(a, b, preferred_element_type=F32)


def _rms_norm(x, w):
    return x * lax.rsqrt(jnp.mean(x * x, axis=-1, keepdims=True) + EPS) * w


def _mod_norm(x, w, shift, scale):
    return x * lax.rsqrt(jnp.mean(x * x, axis=-1, keepdims=True) + EPS) * (w * (1.0 + scale)) + shift


def _const_spec(shape):
    zeros = (0,) * len(shape)
    return pl.BlockSpec(shape, lambda *_: zeros, pipeline_mode=pl.Buffered(1))


def _cast_jobs(weights, steps):
    specs, shapes = [], []
    for w in weights:
        n_rows = w.shape[0]
        rows = -(-n_rows // steps)
        while rows % BF16_ROWS or n_rows % rows:
            rows += 1
        last = n_rows // rows - 1
        specs.append(pl.BlockSpec((rows, w.shape[1]), lambda i, last=last: (jnp.minimum(i, last), 0)))
        shapes.append(jax.ShapeDtypeStruct(w.shape, BF16))
    return specs, shapes


def _run_casts(refs, scales=None):
    n = len(refs) // 2
    for j, (src, dst) in enumerate(zip(refs[:n], refs[n:])):
        scale = 1.0 if scales is None else scales[j]
        dst[...] = (src[...] if scale == 1.0 else src[...] * scale).astype(BF16)


def _mod_kernel(c_ref, w_ref, b_ref, win_ref, o_ref, winb_ref, wqk_ref):
    c = c_ref[...]
    s = c * jax.nn.sigmoid(c)
    o_ref[...] = _dot(s.astype(BF16), w_ref[...].astype(BF16)) + b_ref[...]
    wb = win_ref[...].astype(BF16)
    winb_ref[...] = wb
    n_qk = 2 * QK_WIDTH
    src = lax.broadcasted_iota(jnp.int32, (n_qk, n_qk), 0)
    dst = lax.broadcasted_iota(jnp.int32, (n_qk, n_qk), 1)
    select = (src == _qk_source_column(dst)).astype(BF16)
    wqk_ref[...] = _dot(wb[:, :n_qk], select).astype(BF16)


def _qk_source_column(col):
    half = QK_DIM // 2
    lane = col % LANES
    head = 2 * ((col % QK_WIDTH) // LANES) + (lane // half) % 2
    return (col // QK_WIDTH) * QK_WIDTH + head * QK_DIM + (lane // (2 * half)) * half + lane % half


def _modulation(c8, w_mod, b_mod, w_in):
    n = w_mod.shape[1]
    steps = MOD_STEPS
    tn = n // steps
    d, n_in = w_in.shape
    rows = d // steps
    return pl.pallas_call(
        _mod_kernel,
        grid=(steps,),
        in_specs=[_const_spec((8, D_MODEL)),
                  pl.BlockSpec((D_MODEL, tn), lambda j: (0, j)),
                  pl.BlockSpec((1, tn), lambda j: (0, j)),
                  pl.BlockSpec((rows, n_in), lambda j: (j, 0))],
        out_specs=[pl.BlockSpec((8, tn), lambda j: (0, j)),
                   pl.BlockSpec((rows, n_in), lambda j: (j, 0)),
                   pl.BlockSpec((rows, 2 * QK_WIDTH), lambda j: (j, 0))],
        out_shape=[jax.ShapeDtypeStruct((8, n), F32),
                   jax.ShapeDtypeStruct((d, n_in), BF16),
                   jax.ShapeDtypeStruct((d, 2 * QK_WIDTH), BF16)],
        compiler_params=_params(("parallel",)),
        name="mod",
    )(c8, w_mod, b_mod.reshape(1, n), w_in)


def _proj_kernel(x_ref, mod_ref, nw_ref, wqk_ref, w_ref, rtab_ref, ctab_ref, ctx_ref, cmod_ref,
                 *refs, cast_scales):
    n_cast = len(cast_scales)
    q_ref, k_ref, v_ref, sg_ref, f_ref, kc_ref, vc_ref = refs[n_cast:n_cast + 7]
    _run_casts(refs[:n_cast] + refs[n_cast + 7:], cast_scales)

    @pl.when(pl.program_id(0) == 0)
    def _():
        for bb in range(ctx_ref.shape[0]):
            hc = _mod_norm(ctx_ref[bb], nw_ref[...], cmod_ref[0, 0:1, :], cmod_ref[0, 1:2, :])
            hcb = hc.astype(BF16)
            kc_ref[bb] = _dot(hcb, wqk_ref[:, QK_WIDTH:]).astype(BF16)
            vc_ref[bb] = _dot(hcb, w_ref[:, V_OFF:G_OFF]).astype(BF16)

    lane = lax.broadcasted_iota(jnp.int32, (1, LANES), 1)
    by_row = (lane % (QK_DIM // 2)) < QK_DIM // 4

    def rope(t, trig, scale, out_ref, rows):
        for j in range(QK_WIDTH // LANES):
            tj = t[:, j * LANES:(j + 1) * LANES]
            r = tj * trig[0] + pltpu.roll(tj, LANES // 2, 1) * trig[1]
            out_ref[rows, j * LANES:(j + 1) * LANES] = (r * scale).astype(BF16)

    for r0 in range(0, x_ref.shape[0], PROJ_SUB):
        rows = slice(r0, r0 + PROJ_SUB)
        trig = []
        for cs in range(2):
            trig.append(jnp.concatenate(
                [jnp.where(by_row, rtab_ref[cs, g:g + 1, :], ctab_ref[cs])
                 for g in range(r0 // GRID_W, (r0 + PROJ_SUB) // GRID_W)], axis=0))
        x = x_ref[rows, :]
        h = _mod_norm(x, nw_ref[...], mod_ref[0, 0:1, :], mod_ref[0, 1:2, :])
        hb = h.astype(BF16)
        rope(_dot(hb, wqk_ref[:, :QK_WIDTH]), trig, QK_DIM ** -0.5, q_ref, rows)
        rope(_dot(hb, wqk_ref[:, QK_WIDTH:]), trig, 1.0, k_ref, rows)
        v_ref[rows, :] = _dot(hb, w_ref[:, V_OFF:G_OFF]).astype(BF16)
        g = _dot(hb, w_ref[:, G_OFF:F_OFF])
        hg = 0.5 * g
        sg_ref[rows, :] = (hg * jnp.tanh(hg) + hg).astype(BF16)
        f_ref[rows, :] = _dot(hb, w_ref[:, F_OFF:IN_COLS]).astype(BF16)


def _proj(x2, mod3, norm_w, w_qk, w_in, rtab, ctab, ctx, seq, to_cast, cast_scales):
    t, d = x2.shape
    b, lc, _ = ctx.shape
    tl = PROJ_TILE
    tpb = seq // tl
    tok = lambda w: pl.BlockSpec((tl, w), lambda i: (i, 0))
    held = lambda w: pl.BlockSpec((b, lc, w), lambda i: (0, 0, 0))
    cast_specs, cast_shapes = _cast_jobs(to_cast, t // tl)
    out = pl.pallas_call(
        functools.partial(_proj_kernel, cast_scales=tuple(cast_scales)),
        grid=(t // tl,),
        in_specs=[tok(d),
                  pl.BlockSpec((1, N_MOD, d), lambda i: (i // tpb, 0, 0)),
                  _const_spec((1, d)),
                  _const_spec(w_qk.shape), _const_spec(w_in.shape),
                  pl.BlockSpec((2, tl // GRID_W, LANES), lambda i: (0, i % tpb, 0)),
                  _const_spec(ctab.shape),
                  _const_spec(ctx.shape),
                  pl.BlockSpec((1, N_MOD, d), lambda i: (b, 0, 0),
                               pipeline_mode=pl.Buffered(1))] + cast_specs,
        out_specs=[tok(QK_WIDTH), tok(QK_WIDTH), tok(V_WIDTH), tok(V_WIDTH),
                   tok(F_WIDTH), held(QK_WIDTH), held(V_WIDTH)] + cast_specs,
        out_shape=[jax.ShapeDtypeStruct((t, QK_WIDTH), BF16),
                   jax.ShapeDtypeStruct((t, QK_WIDTH), BF16),
                   jax.ShapeDtypeStruct((t, V_WIDTH), BF16),
                   jax.ShapeDtypeStruct((t, V_WIDTH), BF16),
                   jax.ShapeDtypeStruct((t, F_WIDTH), BF16),
                   jax.ShapeDtypeStruct((b, lc, QK_WIDTH), BF16),
                   jax.ShapeDtypeStruct((b, lc, V_WIDTH), BF16)] + cast_shapes,
        compiler_params=_params(("arbitrary",)),
        name="proj",
    )(x2, mod3, norm_w, w_qk, w_in, rtab, ctab, ctx, mod3, *to_cast)
    return out[:7], out[7:]


def _retention_kernel(af_ref, ab_ref, q_ref, k_ref, v_ref, kc_ref, vc_ref,
                      tw_ref, w2_ref, cc_ref, y_ref,
                      o_ref, z_ref, ds_scr, st_scr, d_scr, tab_scr, z_scr):
    c = RET_CHUNK
    seq = q_ref.shape[1]
    lc = kc_ref.shape[1]
    n = seq // c
    lgf = [-jnp.exp(af_ref[hh]) for hh in range(2)]
    lgb = [-jnp.exp(ab_ref[hh]) for hh in range(2)]
    half = QK_DIM // 2

    lane_head = (lax.broadcasted_iota(jnp.int32, (1, LANES), 1) // half) % 2
    masks = [(lane_head == hh).astype(BF16) for hh in range(2)]
    v_head = lax.broadcasted_iota(jnp.int32, (1, 2 * V_DIM), 1) // V_DIM
    vmasks = [(v_head == hh).astype(BF16) for hh in range(2)]
    lgf_lane = jnp.where(lane_head == 0, lgf[0], lgf[1])
    lgb_lane = jnp.where(lane_head == 0, lgb[0], lgb[1])
    row_head = (lax.broadcasted_iota(jnp.int32, (LANES, 1), 0) // half) % 2
    lgf_row = jnp.where(row_head == 0, lgf[0][:, 0:1], lgf[1][:, 0:1])
    lgb_row = jnp.where(row_head == 0, lgb[0][:, 0:1], lgb[1][:, 0:1])

    pos = lax.broadcasted_iota(jnp.int32, (c, LANES), 0).astype(F32)
    tab_scr[0] = jnp.exp(lgf_lane * (pos + 1.0))
    tab_scr[1] = jnp.exp(lgb_lane * (c - pos))
    diff = (lax.broadcasted_iota(jnp.int32, (c, c), 0)
            - lax.broadcasted_iota(jnp.int32, (c, c), 1)).astype(F32)
    for hh in range(2):
        d_scr[hh] = (jnp.where(diff >= 0, jnp.exp(lgf[hh][:, 0:1] * jnp.maximum(diff, 0.0)), 0.0)
                     + jnp.where(diff <= 0, jnp.exp(lgb[hh][:, 0:1] * jnp.maximum(-diff, 0.0)), 0.0))

    def k_decays(tokens):
        t = lax.broadcasted_iota(jnp.int32, (1, tokens), 1).astype(F32)
        return jnp.exp(lgf_row * (tokens - 1.0 - t)), jnp.exp(lgb_row * t)

    def state_increment(k_rows, v_rows, decays):
        kt = jnp.transpose(k_rows.astype(F32))
        lhs = jnp.concatenate([kt * decays[0], kt * decays[1]], axis=0).astype(BF16)
        return _dot(lhs, v_rows)

    kdec = k_decays(c)

    def incr(i, carry):
        rows = pl.ds(pl.multiple_of(i * c, c), c)
        ds_scr[i] = state_increment(k_ref[0, rows, :], v_ref[0, rows, :], kdec)
        return carry

    lax.fori_loop(0, n, incr, 0, unroll=RET_UNROLL)

    s0 = state_increment(kc_ref[0], vc_ref[0], k_decays(lc))
    col_head = lax.broadcasted_iota(jnp.int32, (1, 2 * V_DIM), 1) // V_DIM
    own = (row_head == col_head).astype(F32)
    decay_f = jnp.exp(lgf_row * c) * own
    decay_b = jnp.exp(lgb_row * c) * own

    def scan_f(i, s):
        st_scr[i, 0:LANES, :] = (s * own).astype(BF16)
        return decay_f * s + ds_scr[i, 0:LANES, :]

    lax.fori_loop(0, n, scan_f, s0[0:LANES])

    def scan_b(t, s):
        i = n - 1 - t
        st_scr[i, LANES:2 * LANES, :] = (s * own).astype(BF16)
        return decay_b * s + ds_scr[i, LANES:2 * LANES, :]

    lax.fori_loop(0, n, scan_b, s0[LANES:2 * LANES])

    def outputs(i):
        rows = pl.ds(pl.multiple_of(i * c, c), c)
        q = q_ref[0, rows, :]
        k = k_ref[0, rows, :]
        qf = q.astype(F32)
        qd = jnp.concatenate([qf * tab_scr[0], qf * tab_scr[1]], axis=1).astype(BF16)
        inter = _dot(qd, st_scr[i])
        k2 = jnp.concatenate([k * masks[0], k * masks[1]], axis=0)
        scores = lax.dot_general(q, k2, (((1,), (1,)), ((), ())), preferred_element_type=F32)
        p = (scores * jnp.concatenate([d_scr[0], d_scr[1]], axis=1)).astype(BF16)
        v = v_ref[0, rows, :]
        v2 = jnp.concatenate([v * vmasks[0], v * vmasks[1]], axis=0)
        o_ref[0, rows, :] = (_dot(p, v2) + inter).astype(BF16)

    per_k1 = n // DFT_ROWS
    cc = cc_ref[...].astype(BF16)

    def outputs_and_dft(j, carry):
        for u in range(per_k1):
            outputs(j * per_k1 + u)
        _dft_second_stage(j, tw_ref, w2_ref, cc, y_ref, z_scr)
        return carry

    lax.fori_loop(0, DFT_ROWS, outputs_and_dft, 0, unroll=RET_UNROLL // per_k1)
    _dft_emit(z_scr, z_ref)


def _retention_and_dft(a_f, a_b, q, k, v, kc, vc, y, tw, w2, w_c):
    b, seq, _ = v.shape
    lc = kc.shape[1]
    c = RET_CHUNK
    n = seq // c
    r = DFT_ROWS
    _, _, n1, n2, w = y.shape
    assert n % r == 0 and n1 // r == HEADS // 2
    dec = pl.BlockSpec((2, 1, LANES), lambda i, p: (p, 0, 0))
    qk = lambda rows: pl.BlockSpec((1, rows, LANES), lambda i, p: (i, 0, p))
    vv = lambda rows: pl.BlockSpec((1, rows, 2 * V_DIM), lambda i, p: (i, 0, p))
    return pl.pallas_call(
        _retention_kernel,
        grid=(b, HEADS // 2),
        in_specs=[dec, dec, qk(seq), qk(seq), vv(seq), qk(lc), vv(lc),
                  pl.BlockSpec((r, 2, n2), lambda i, p: (p, 0, 0)),
                  _const_spec((2, n2, n2)),
                  _const_spec((2 * F_GROUP_DIM, F_GROUP_DIM)),
                  pl.BlockSpec((1, 2, r, n2, w), lambda i, p: (i, 0, p, 0, 0))],
        out_specs=[vv(seq), pl.BlockSpec((1, n2, r, w), lambda i, p: (i, 0, p, 0))],
        out_shape=[jax.ShapeDtypeStruct((b, seq, V_WIDTH), BF16),
                   jax.ShapeDtypeStruct((b, n2, n1, w), BF16)],
        scratch_shapes=[pltpu.VMEM((n, 2 * LANES, 2 * V_DIM), F32),
                        pltpu.VMEM((n, 2 * LANES, 2 * V_DIM), BF16),
                        pltpu.VMEM((2, c, c), F32),
                        pltpu.VMEM((2, c, LANES), F32),
                        pltpu.VMEM((F_GROUPS, (n2 + DFT_PAD) * r, F_GROUP_DIM), F32)],
        compiler_params=_params(("parallel", "parallel")),
        name="retention_dft",
    )(a_f, a_b, q, k, v, kc, vc, tw, w2, w_c, y)


def _dft_a_kernel(w_ref, x_ref, y_ref):
    _, n1, rows, w = x_ref.shape
    r = DFT_ROWS
    nh = n1 // 2 + 1
    for r0 in range(0, rows, r):
        x = x_ref[0, :, r0:r0 + r, :].reshape(n1 * r, w)
        y = _dot(w_ref[...], x).reshape(2, nh, r, w).astype(BF16)
        y_ref[0, :, 0:nh, r0:r0 + r, :] = y
        for k1 in range(1, n1 // 2):
            y_ref[0, 0, n1 - k1, r0:r0 + r, :] = y[0, k1]
            y_ref[0, 1, n1 - k1, r0:r0 + r, :] = -y[1, k1]


def _dft_second_stage(j, tw_ref, w2_ref, cc, y_ref, z_scr):
    n2 = y_ref.shape[3]
    w2c = w2_ref[0]
    w2s = w2_ref[1]
    tc = tw_ref[j, 0:1, :]
    ts = tw_ref[j, 1:2, :]
    ec = w2c * tc - w2s * ts
    es = w2s * tc + w2c * ts
    m = jnp.concatenate([jnp.concatenate([ec, -es], axis=1),
                         jnp.concatenate([es, ec], axis=1)], axis=0).astype(BF16)
    y = jnp.concatenate([y_ref[0, 0, j], y_ref[0, 1, j]], axis=0)
    zz = _dot(m, y)
    for g in range(F_GROUPS):
        cols = slice(g * F_GROUP_DIM, (g + 1) * F_GROUP_DIM)
        zcs = jnp.concatenate([zz[:n2, cols], zz[n2:, cols]], axis=1).astype(BF16)
        z_scr[g, pl.ds(pl.multiple_of(j * (n2 + DFT_PAD), DFT_PAD), n2), :] = _dot(zcs, cc)


def _dft_emit(z_scr, z_ref):
    n2 = z_ref.shape[1]
    pitch = n2 + DFT_PAD
    z = jnp.concatenate(
        [jnp.concatenate([z_scr[g, pl.ds(k2, DFT_ROWS, stride=pitch), :] for k2 in range(n2)], axis=0)
         for g in range(F_GROUPS)], axis=1)
    z_ref[0] = z.reshape(n2, DFT_ROWS, F_WIDTH).astype(BF16)


def _dft_tables(seq):
    n1 = DFT_N1
    n2 = seq // n1

    def cs(num, den):
        ang = 2.0 * np.pi * (num % den) / den
        return np.cos(ang), np.sin(ang)

    a = np.arange(n1)
    w_a = np.concatenate(cs(a[:n1 // 2 + 1, None] * a[None, :], n1), axis=0)
    m = np.arange(n2)
    tw = np.stack(cs(a[:, None] * m[None, :], seq), axis=1)
    w2 = np.stack(cs(m[:, None] * m[None, :], n2), axis=0)
    ch = np.arange(F_GROUP_DIM)
    cc, sc = cs(ch[:, None] * ch[None, :], F_GROUP_DIM)
    scale = 1.0 / np.sqrt(seq * F_GROUP_DIM)
    w_c = np.concatenate([cc, -sc], axis=0) * scale
    return [jnp.asarray(t, dtype=F32) for t in (w_a, tw, w2, w_c)]


def _dft_first_stage(f):
    b, seq, w = f.shape
    n1 = DFT_N1
    n2 = seq // n1
    w_a, tw, w2, w_c = _dft_tables(seq)
    r = DFT_ROWS
    spread = (jnp.arange(n1 * r)[None, :] // r == jnp.arange(n1)[:, None]).astype(F32)
    w_rep = jnp.dot(w_a, spread, precision=lax.Precision.HIGHEST)
    same_r = jnp.arange(w_a.shape[0] * r)[:, None] % r == jnp.arange(n1 * r)[None, :] % r
    w_a = jnp.where(same_r, jnp.repeat(w_rep, r, axis=0), 0.0).astype(BF16)
    y = pl.pallas_call(
        _dft_a_kernel,
        grid=(b, n2 // DFT_A_ROWS),
        in_specs=[_const_spec(w_a.shape),
                  pl.BlockSpec((1, n1, DFT_A_ROWS, w), lambda i, j: (i, 0, j, 0))],
        out_specs=pl.BlockSpec((1, 2, n1, DFT_A_ROWS, w), lambda i, j: (i, 0, 0, j, 0)),
        out_shape=jax.ShapeDtypeStruct((b, 2, n1, n2, w), BF16),
        compiler_params=_params(("parallel", "parallel")),
        name="dft_a",
    )(w_a, f.reshape(b, n1, n2, w))
    return y, tw, w2, w_c


def _merge_kernel(x_ref, mod_ref, nw_ref, n2w_ref, ro_ref, sg_ref, z_ref, wbg_ref, bbg_ref,
                  wro_ref, wfo_ref, wout_ref, *refs, n_cast):
    o_ref, h2_ref = refs[n_cast:n_cast + 2]
    _run_casts(refs[:n_cast] + refs[n_cast + 2:])
    for r0 in range(0, x_ref.shape[0], MERGE_SUB):
        rows = slice(r0, r0 + MERGE_SUB)
        x = x_ref[rows, :]
        h = _mod_norm(x, nw_ref[...], mod_ref[0, 0:1, :], mod_ref[0, 1:2, :])
        t = jnp.tanh(_dot(h.astype(BF16), wbg_ref[...]) + 0.5 * bbg_ref[...])
        gated = []
        for hd in range(HEADS):
            cols = slice(hd * V_DIM, (hd + 1) * V_DIM)
            o = ro_ref[rows, cols].astype(F32)
            oc = o - jnp.mean(o, axis=-1, keepdims=True)
            var = jnp.mean(oc * oc, axis=-1, keepdims=True)
            gated.append((oc * lax.rsqrt(var + EPS) * sg_ref[rows, cols].astype(F32)).astype(BF16))
        ret_d = _dot(jnp.concatenate(gated, axis=1), wro_ref[...])
        four_d = _dot(z_ref[rows, :], wfo_ref[...])
        m2 = (t[:, :D_MODEL] * ret_d + ret_d) + (t[:, D_MODEL:] * four_d + four_d)
        y2 = _dot(m2.astype(BF16), wout_ref[...])
        x1 = x + (0.5 * mod_ref[0, 2:3, :]) * y2
        o_ref[rows, :] = x1
        h2 = _mod_norm(x1, n2w_ref[...], mod_ref[0, 3:4, :], mod_ref[0, 4:5, :])
        h2_ref[rows, :] = h2.astype(BF16)


def _merge(x2, mod3, norm_w, norm2_w, ro, sg, z, w_bg, b_bg, w_ro, w_fo, w_out, seq, to_cast):
    t, d = x2.shape
    tl = MERGE_TILE
    tpb = seq // tl
    tok = lambda w: pl.BlockSpec((tl, w), lambda i: (i, 0))
    cast_specs, cast_shapes = _cast_jobs(to_cast, t // tl)
    out = pl.pallas_call(
        functools.partial(_merge_kernel, n_cast=len(to_cast)),
        grid=(t // tl,),
        in_specs=[tok(d),
                  pl.BlockSpec((1, N_MOD, d), lambda i: (i // tpb, 0, 0)),
                  _const_spec((1, d)), _const_spec((1, d)),
                  tok(V_WIDTH), tok(V_WIDTH), tok(F_WIDTH),
                  _const_spec(w_bg.shape), _const_spec((1, 2 * d)),
                  _const_spec(w_ro.shape), _const_spec(w_fo.shape),
                  _const_spec(w_out.shape)] + cast_specs,
        out_specs=[tok(d), tok(d)] + cast_specs,
        out_shape=[jax.ShapeDtypeStruct((t, d), F32),
                   jax.ShapeDtypeStruct((t, d), BF16)] + cast_shapes,
        compiler_params=_params(("arbitrary",)),
        name="merge",
    )(x2, mod3, norm_w, norm2_w, ro, sg, z, w_bg, b_bg, w_ro, w_fo, w_out, *to_cast)
    return out[:2], out[2:]


HALO = 8
FFN_LEAD = 16
FFN_SUB = 256


def _two_gelu_tanh(x):
    c1 = np.sqrt(2.0 / np.pi)
    return x + x * jnp.tanh(x * (c1 + (c1 * 0.044715) * (x * x)))


def _ffn_kernel(xm_ref, hm_ref, xp_ref, xn_ref, mod_ref, nw_ref, wu_ref, cw_ref, cb_ref,
                wd_ref, fnw_ref, o_ref, h_scr, u_scr, act_scr, y_scr, *, tiles_per_seq):
    tl = xm_ref.shape[0]
    sub = FFN_SUB
    half = sub // 2
    nc = FFN_CHUNK
    i = pl.program_id(0)
    keep_prev = ((i % tiles_per_seq) != 0).astype(F32)
    keep_next = ((i % tiles_per_seq) != tiles_per_seq - 1).astype(F32)

    def pre(x):
        return _mod_norm(x, nw_ref[...], mod_ref[0, 3:4, :], mod_ref[0, 4:5, :])

    pad = jnp.zeros((FFN_LEAD - HALO, xm_ref.shape[1]), F32)
    h_scr[0:FFN_LEAD] = jnp.concatenate([pad, pre(xp_ref[...]) * keep_prev], axis=0).astype(BF16)
    h_scr[FFN_LEAD:FFN_LEAD + tl] = hm_ref[...]
    h_scr[FFN_LEAD + tl:] = jnp.concatenate([pre(xn_ref[...]) * keep_next, pad], axis=0).astype(BF16)

    def project(j):
        hb = h_scr[...]
        for part in range(2):
            lo = part * FFN_DIM + j * nc
            u = _dot(hb, wu_ref[:, lo:lo + nc])
            for s in range(nc // LANES):
                u_scr[j % 2, part, s] = u[:, s * LANES:(s + 1) * LANES]

    def conv(j, part, s, blk, scale):
        lo = part * FFN_DIM + j * nc + s * LANES
        w = cw_ref[:, lo:lo + LANES] * scale
        bias = cb_ref[:, lo:lo + LANES] * scale
        first = FFN_LEAD + blk * sub
        rows = lambda start: u_scr[j % 2, part, s, pl.ds(start, half, stride=2), :]
        before, even, odd, after = rows(first - 1), rows(first), rows(first + 1), rows(first + 2)
        return (before * w[0:1] + even * w[1:2] + odd * w[2:3] + bias,
                even * w[0:1] + odd * w[1:2] + after * w[2:3] + bias)

    def activate(j):
        for s in range(nc // LANES):
            cols = slice(j * nc + s * LANES, j * nc + (s + 1) * LANES)
            for blk in range(tl // sub):
                gate = conv(j, 0, s, blk, 1.0)
                val = conv(j, 1, s, blk, 0.5)
                for par in range(2):
                    r0 = blk * sub + par * half
                    act_scr[r0:r0 + half, cols] = (_two_gelu_tanh(gate[par]) * val[par]).astype(BF16)

    project(0)
    for j in range(N_FFN_CHUNKS):
        if j + 1 < N_FFN_CHUNKS:
            project(j + 1)
        activate(j)
    for blk in range(tl // sub):
        r0 = blk * sub
        y = _dot(act_scr[r0:r0 + sub, :], wd_ref[...])
        for s in range(y_scr.shape[0]):
            for par in range(2):
                y_scr[s, pl.ds(par, half, stride=2), :] = (
                    y[par * half:(par + 1) * half, s * LANES:(s + 1) * LANES])
        y = jnp.concatenate([y_scr[s] for s in range(y_scr.shape[0])], axis=1)
        x2 = xm_ref[r0:r0 + sub, :] + mod_ref[0, 5:6, :] * y
        o_ref[r0:r0 + sub, :] = _rms_norm(x2, fnw_ref[...])


def _ffn(x1, h2, mod3, norm_w, w_u, cw, cb, w_d, fnorm_w, seq):
    t, d = x1.shape
    tl = FFN_TILE
    tpb = seq // tl
    hb = tl // HALO
    last = t // HALO - 1
    rows = FFN_LEAD + tl + FFN_LEAD
    return pl.pallas_call(
        functools.partial(_ffn_kernel, tiles_per_seq=tpb),
        grid=(t // tl,),
        in_specs=[pl.BlockSpec((tl, d), lambda i: (i, 0)),
                  pl.BlockSpec((tl, d), lambda i: (i, 0)),
                  pl.BlockSpec((HALO, d), lambda i: (jnp.maximum(i * hb - 1, 0), 0)),
                  pl.BlockSpec((HALO, d), lambda i: (jnp.minimum((i + 1) * hb, last), 0)),
                  pl.BlockSpec((1, N_MOD, d), lambda i: (i // tpb, 0, 0)),
                  _const_spec((1, d)),
                  _const_spec(w_u.shape), _const_spec(cw.shape), _const_spec(cb.shape),
                  _const_spec(w_d.shape), _const_spec((1, d))],
        out_specs=pl.BlockSpec((tl, d), lambda i: (i, 0)),
        out_shape=jax.ShapeDtypeStruct((t, d), F32),
        scratch_shapes=[pltpu.VMEM((rows, d), BF16),
                        pltpu.VMEM((2, 2, FFN_CHUNK // LANES, rows, LANES), F32),
                        pltpu.VMEM((tl, FFN_DIM), BF16),
                        pltpu.VMEM((d // LANES, FFN_SUB, LANES), F32)],
        compiler_params=_params(("parallel",)),
        name="ffn",
    )(x1, h2, x1, x1, mod3, norm_w, w_u, cw, cb, w_d, fnorm_w)


def _rope_tables(seq):
    n_freq = QK_DIM // 4
    lane = jnp.arange(LANES)
    inv = ROPE_BASE ** (-(lane % n_freq).astype(F32) / n_freq)
    sign = jnp.where(lane < LANES // 2, -1.0, 1.0).astype(F32)

    def table(n):
        ang = jnp.arange(n, dtype=F32)[:, None] * inv[None, :]
        return jnp.stack([jnp.cos(ang), jnp.sin(ang) * sign])

    return table(seq // GRID_W), table(GRID_W)


def kernel(x, c, ctx, c_ctx, w_mod, b_mod, norm1_w, w_in, ret_decay_f, ret_decay_b,
           w_ret_out, w_four_out, w_branch_gate, b_branch_gate, w_out, norm2_w,
           w_up, conv_w, conv_b, w_down, final_norm_w):
    assert w_mod.shape[0] == 1, "single-layer block"
    b, seq, d = x.shape
    t = b * seq
    assert b == 2 and seq % RET_CHUNK == 0
    assert seq % FFN_TILE == 0 and seq % MERGE_TILE == 0 and seq % PROJ_TILE == 0

    c8 = jnp.concatenate([c, c_ctx[None, :], jnp.zeros((8 - b - 1, d), F32)], axis=0)
    mod, w_in_b, w_qk = _modulation(c8, w_mod[0], b_mod[0], w_in[0])
    mod3 = mod.reshape(8, N_MOD, d)
    n1w = norm1_w[0].reshape(1, d)

    rtab, ctab = _rope_tables(seq)
    x2 = x.reshape(t, d)
    (q, k, v, sg, f, kc, vc), (w_bg, w_ro, w_fo, w_o) = _proj(
        x2, mod3, n1w, w_qk, w_in_b, rtab, ctab, ctx, seq,
        [w_branch_gate[0], w_ret_out[0], w_four_out[0], w_out[0]], (0.5, 1.0, 1.0, 1.0))

    a_f = jnp.broadcast_to(ret_decay_f[0][:, None, None], (HEADS, 1, LANES))
    a_b = jnp.broadcast_to(ret_decay_b[0][:, None, None], (HEADS, 1, LANES))
    y, tw, w2, w_c = _dft_first_stage(f.reshape(b, seq, F_WIDTH))
    ro, z = _retention_and_dft(a_f, a_b, q.reshape(b, seq, QK_WIDTH), k.reshape(b, seq, QK_WIDTH),
                               v.reshape(b, seq, V_WIDTH), kc, vc, y, tw, w2, w_c)

    n2w = norm2_w[0].reshape(1, d)
    (x1, h2), (w_u, w_d) = _merge(
        x2, mod3, n1w, n2w, ro.reshape(t, V_WIDTH), sg, z.reshape(t, F_WIDTH),
        w_bg, b_branch_gate[0].reshape(1, 2 * d), w_ro, w_fo, w_o, seq, [w_up[0], w_down[0]])

    out = _ffn(x1, h2, mod3, n2w, w_u, conv_w[0], conv_b[0].reshape(1, 2 * FFN_DIM), w_d,
               final_norm_w.reshape(1, d), seq)
    return out.reshape(b, seq, d)
```

```python
import functools

import numpy as np
import jax
import jax.numpy as jnp
from jax import lax
from jax.experimental import pallas as pl
from jax.experimental.pallas import tpu as pltpu

F32 = jnp.float32
BF16 = jnp.bfloat16

D_MODEL = 1024
GRID_W = 64
HEADS = 8
QK_DIM = 64
V_DIM = 128
QK_WIDTH = HEADS * QK_DIM
V_WIDTH = HEADS * V_DIM
ROPE_BASE = 10000.0
F_GROUPS = 4
F_GROUP_DIM = 128
F_WIDTH = F_GROUPS * F_GROUP_DIM
K_OFF = QK_WIDTH
V_OFF = K_OFF + QK_WIDTH
G_OFF = V_OFF + V_WIDTH
F_OFF = G_OFF + V_WIDTH
IN_COLS = F_OFF + F_WIDTH
FFN_DIM = 2816
N_MOD = 6
EPS = 1e-6

LANES = 128
RET_CHUNK = 256
RET_UNROLL = 16
FFN_CHUNK = 256
N_FFN_CHUNKS = FFN_DIM // FFN_CHUNK
FFN_TILE = 512
MERGE_TILE = 1024
MERGE_SUB = 512
PROJ_TILE = 1024
PROJ_SUB = 256
MOD_STEPS = 2
DFT_N1 = 64
BF16_ROWS = 16
DFT_ROWS = BF16_ROWS
DFT_A_ROWS = 2 * DFT_ROWS
DFT_PAD = 8
VMEM_LIMIT = 56 * 1024 * 1024


def _params(sem):
    return pltpu.CompilerParams(dimension_semantics=sem, vmem_limit_bytes=VMEM_LIMIT)


def _dot(a, b):
    return jnp.dot(a, b, preferred_element_type=F32)


def _rms_norm(x, w):
    return x * lax.rsqrt(jnp.mean(x * x, axis=-1, keepdims=True) + EPS) * w


def _mod_norm(x, w, shift, scale):
    return x * lax.rsqrt(jnp.mean(x * x, axis=-1, keepdims=True) + EPS) * (w * (1.0 + scale)) + shift


def _const_spec(shape):
    zeros = (0,) * len(shape)
    return pl.BlockSpec(shape, lambda *_: zeros, pipeline_mode=pl.Buffered(1))


def _cast_jobs(weights, steps):
    specs, shapes = [], []
    for w in weights:
        n_rows = w.shape[0]
        rows = -(-n_rows // steps)
        while rows % BF16_ROWS or n_rows % rows:
            rows += 1
        last = n_rows // rows - 1
        specs.append(pl.BlockSpec((rows, w.shape[1]), lambda i, last=last: (jnp.minimum(i, last), 0)))
        shapes.append(jax.ShapeDtypeStruct(w.shape, BF16))
    return specs, shapes


def _run_casts(refs, scales=None):
    n = len(refs) // 2
    for j, (src, dst) in enumerate(zip(refs[:n], refs[n:])):
        scale = 1.0 if scales is None else scales[j]
        dst[...] = (src[...] if scale == 1.0 else src[...] * scale).astype(BF16)


def _mod_kernel(c_ref, w_ref, b_ref, win_ref, o_ref, winb_ref, wqk_ref):
    c = c_ref[...]
    s = c * jax.nn.sigmoid(c)
    o_ref[...] = _dot(s.astype(BF16), w_ref[...].astype(BF16)) + b_ref[...]
    wb = win_ref[...].astype(BF16)
    winb_ref[...] = wb
    n_qk = 2 * QK_WIDTH
    src = lax.broadcasted_iota(jnp.int32, (n_qk, n_qk), 0)
    dst = lax.broadcasted_iota(jnp.int32, (n_qk, n_qk), 1)
    select = (src == _qk_source_column(dst)).astype(BF16)
    wqk_ref[...] = _dot(wb[:, :n_qk], select).astype(BF16)


def _qk_source_column(col):
    half = QK_DIM // 2
    lane = col % LANES
    head = 2 * ((col % QK_WIDTH) // LANES) + (lane // half) % 2
    return (col // QK_WIDTH) * QK_WIDTH + head * QK_DIM + (lane // (2 * half)) * half + lane % half


def _modulation(c8, w_mod, b_mod, w_in):
    n = w_mod.shape[1]
    steps = MOD_STEPS
    tn = n // steps
    d, n_in = w_in.shape
    rows = d // steps
    return pl.pallas_call(
        _mod_kernel,
        grid=(steps,),
        in_specs=[_const_spec((8, D_MODEL)),
                  pl.BlockSpec((D_MODEL, tn), lambda j: (0, j)),
                  pl.BlockSpec((1, tn), lambda j: (0, j)),
                  pl.BlockSpec((rows, n_in), lambda j: (j, 0))],
        out_specs=[pl.BlockSpec((8, tn), lambda j: (0, j)),
                   pl.BlockSpec((rows, n_in), lambda j: (j, 0)),
                   pl.BlockSpec((rows, 2 * QK_WIDTH), lambda j: (j, 0))],
        out_shape=[jax.ShapeDtypeStruct((8, n), F32),
                   jax.ShapeDtypeStruct((d, n_in), BF16),
                   jax.ShapeDtypeStruct((d, 2 * QK_WIDTH), BF16)],
        compiler_params=_params(("parallel",)),
        name="mod",
    )(c8, w_mod, b_mod.reshape(1, n), w_in)


def _proj_kernel(x_ref, mod_ref, nw_ref, wqk_ref, w_ref, rtab_ref, ctab_ref, ctx_ref, cmod_ref,
                 *refs, cast_scales):
    n_cast = len(cast_scales)
    q_ref, k_ref, v_ref, sg_ref, f_ref, kc_ref, vc_ref = refs[n_cast:n_cast + 7]
    _run_casts(refs[:n_cast] + refs[n_cast + 7:], cast_scales)

    @pl.when(pl.program_id(0) == 0)
    def _():
        for bb in range(ctx_ref.shape[0]):
            hc = _mod_norm(ctx_ref[bb], nw_ref[...], cmod_ref[0, 0:1, :], cmod_ref[0, 1:2, :])
            hcb = hc.astype(BF16)
            kc_ref[bb] = _dot(hcb, wqk_ref[:, QK_WIDTH:]).astype(BF16)
            vc_ref[bb] = _dot(hcb, w_ref[:, V_OFF:G_OFF]).astype(BF16)

    lane = lax.broadcasted_iota(jnp.int32, (1, LANES), 1)
    by_row = (lane % (QK_DIM // 2)) < QK_DIM // 4

    def rope(t, trig, scale, out_ref, rows):
        for j in range(QK_WIDTH // LANES):
            tj = t[:, j * LANES:(j + 1) * LANES]
            r = tj * trig[0] + pltpu.roll(tj, LANES // 2, 1) * trig[1]
            out_ref[rows, j * LANES:(j + 1) * LANES] = (r * scale).astype(BF16)

    for r0 in range(0, x_ref.shape[0], PROJ_SUB):
        rows = slice(r0, r0 + PROJ_SUB)
        trig = []
        for cs in range(2):
            trig.append(jnp.concatenate(
                [jnp.where(by_row, rtab_ref[cs, g:g + 1, :], ctab_ref[cs])
                 for g in range(r0 // GRID_W, (r0 + PROJ_SUB) // GRID_W)], axis=0))
        x = x_ref[rows, :]
        h = _mod_norm(x, nw_ref[...], mod_ref[0, 0:1, :], mod_ref[0, 1:2, :])
        hb = h.astype(BF16)
        rope(_dot(hb, wqk_ref[:, :QK_WIDTH]), trig, QK_DIM ** -0.5, q_ref, rows)
        rope(_dot(hb, wqk_ref[:, QK_WIDTH:]), trig, 1.0, k_ref, rows)
        v_ref[rows, :] = _dot(hb, w_ref[:, V_OFF:G_OFF]).astype(BF16)
        g = _dot(hb, w_ref[:, G_OFF:F_OFF])
        hg = 0.5 * g
        sg_ref[rows, :] = (hg * jnp.tanh(hg) + hg).astype(BF16)
        f_ref[rows, :] = _dot(hb, w_ref[:, F_OFF:IN_COLS]).astype(BF16)


def _proj(x2, mod3, norm_w, w_qk, w_in, rtab, ctab, ctx, seq, to_cast, cast_scales):
    t, d = x2.shape
    b, lc, _ = ctx.shape
    tl = PROJ_TILE
    tpb = seq // tl
    tok = lambda w: pl.BlockSpec((tl, w), lambda i: (i, 0))
    held = lambda w: pl.BlockSpec((b, lc, w), lambda i: (0, 0, 0))
    cast_specs, cast_shapes = _cast_jobs(to_cast, t // tl)
    out = pl.pallas_call(
        functools.partial(_proj_kernel, cast_scales=tuple(cast_scales)),
        grid=(t // tl,),
        in_specs=[tok(d),
                  pl.BlockSpec((1, N_MOD, d), lambda i: (i // tpb, 0, 0)),
                  _const_spec((1, d)),
                  _const_spec(w_qk.shape), _const_spec(w_in.shape),
                  pl.BlockSpec((2, tl // GRID_W, LANES), lambda i: (0, i % tpb, 0)),
                  _const_spec(ctab.shape),
                  _const_spec(ctx.shape),
                  pl.BlockSpec((1, N_MOD, d), lambda i: (b, 0, 0),
                               pipeline_mode=pl.Buffered(1))] + cast_specs,
        out_specs=[tok(QK_WIDTH), tok(QK_WIDTH), tok(V_WIDTH), tok(V_WIDTH),
                   tok(F_WIDTH), held(QK_WIDTH), held(V_WIDTH)] + cast_specs,
        out_shape=[jax.ShapeDtypeStruct((t, QK_WIDTH), BF16),
                   jax.ShapeDtypeStruct((t, QK_WIDTH), BF16),
                   jax.ShapeDtypeStruct((t, V_WIDTH), BF16),
                   jax.ShapeDtypeStruct((t, V_WIDTH), BF16),
                   jax.ShapeDtypeStruct((t, F_WIDTH), BF16),
                   jax.ShapeDtypeStruct((b, lc, QK_WIDTH), BF16),
                   jax.ShapeDtypeStruct((b, lc, V_WIDTH), BF16)] + cast_shapes,
        compiler_params=_params(("arbitrary",)),
        name="proj",
    )(x2, mod3, norm_w, w_qk, w_in, rtab, ctab, ctx, mod3, *to_cast)
    return out[:7], out[7:]


def _retention_kernel(af_ref, ab_ref, q_ref, k_ref, v_ref, kc_ref, vc_ref,
                      tw_ref, w2_ref, cc_ref, y_ref,
                      o_ref, z_ref, ds_scr, st_scr, d_scr, tab_scr, z_scr):
    c = RET_CHUNK
    seq = q_ref.shape[1]
    lc = kc_ref.shape[1]
    n = seq // c
    lgf = [-jnp.exp(af_ref[hh]) for hh in range(2)]
    lgb = [-jnp.exp(ab_ref[hh]) for hh in range(2)]
    half = QK_DIM // 2

    lane_head = (lax.broadcasted_iota(jnp.int32, (1, LANES), 1) // half) % 2
    masks = [(lane_head == hh).astype(BF16) for hh in range(2)]
    v_head = lax.broadcasted_iota(jnp.int32, (1, 2 * V_DIM), 1) // V_DIM
    vmasks = [(v_head == hh).astype(BF16) for hh in range(2)]
    lgf_lane = jnp.where(lane_head == 0, lgf[0], lgf[1])
    lgb_lane = jnp.where(lane_head == 0, lgb[0], lgb[1])
    row_head = (lax.broadcasted_iota(jnp.int32, (LANES, 1), 0) // half) % 2
    lgf_row = jnp.where(row_head == 0, lgf[0][:, 0:1], lgf[1][:, 0:1])
    lgb_row = jnp.where(row_head == 0, lgb[0][:, 0:1], lgb[1][:, 0:1])

    pos = lax.broadcasted_iota(jnp.int32, (c, LANES), 0).astype(F32)
    tab_scr[0] = jnp.exp(lgf_lane * (pos + 1.0))
    tab_scr[1] = jnp.exp(lgb_lane * (c - pos))
    diff = (lax.broadcasted_iota(jnp.int32, (c, c), 0)
            - lax.broadcasted_iota(jnp.int32, (c, c), 1)).astype(F32)
    for hh in range(2):
        d_scr[hh] = (jnp.where(diff >= 0, jnp.exp(lgf[hh][:, 0:1] * jnp.maximum(diff, 0.0)), 0.0)
                     + jnp.where(diff <= 0, jnp.exp(lgb[hh][:, 0:1] * jnp.maximum(-diff, 0.0)), 0.0))

    def k_decays(tokens):
        t = lax.broadcasted_iota(jnp.int32, (1, tokens), 1).astype(F32)
        return jnp.exp(lgf_row * (tokens - 1.0 - t)), jnp.exp(lgb_row * t)

    def state_increment(k_rows, v_rows, decays):
        kt = jnp.transpose(k_rows.astype(F32))
        lhs = jnp.concatenate([kt * decays[0], kt * decays[1]], axis=0).astype(BF16)
        return _dot(lhs, v_rows)

    kdec = k_decays(c)

    def incr(i, carry):
        rows = pl.ds(pl.multiple_of(i * c, c), c)
        ds_scr[i] = state_increment(k_ref[0, rows, :], v_ref[0, rows, :], kdec)
        return carry

    lax.fori_loop(0, n, incr, 0, unroll=RET_UNROLL)

    s0 = state_increment(kc_ref[0], vc_ref[0], k_decays(lc))
    col_head = lax.broadcasted_iota(jnp.int32, (1, 2 * V_DIM), 1) // V_DIM
    own = (row_head == col_head).astype(F32)
    decay_f = jnp.exp(lgf_row * c) * own
    decay_b = jnp.exp(lgb_row * c) * own

    def scan_f(i, s):
        st_scr[i, 0:LANES, :] = (s * own).astype(BF16)
        return decay_f * s + ds_scr[i, 0:LANES, :]

    lax.fori_loop(0, n, scan_f, s0[0:LANES])

    def scan_b(t, s):
        i = n - 1 - t
        st_scr[i, LANES:2 * LANES, :] = (s * own).astype(BF16)
        return decay_b * s + ds_scr[i, LANES:2 * LANES, :]

    lax.fori_loop(0, n, scan_b, s0[LANES:2 * LANES])

    def outputs(i):
        rows = pl.ds(pl.multiple_of(i * c, c), c)
        q = q_ref[0, rows, :]
        k = k_ref[0, rows, :]
        qf = q.astype(F32)
        qd = jnp.concatenate([qf * tab_scr[0], qf * tab_scr[1]], axis=1).astype(BF16)
        inter = _dot(qd, st_scr[i])
        k2 = jnp.concatenate([k * masks[0], k * masks[1]], axis=0)
        scores = lax.dot_general(q, k2, (((1,), (1,)), ((), ())), preferred_element_type=F32)
        p = (scores * jnp.concatenate([d_scr[0], d_scr[1]], axis=1)).astype(BF16)
        v = v_ref[0, rows, :]
        v2 = jnp.concatenate([v * vmasks[0], v * vmasks[1]], axis=0)
        o_ref[0, rows, :] = (_dot(p, v2) + inter).astype(BF16)

    per_k1 = n // DFT_ROWS
    cc = cc_ref[...].astype(BF16)

    def outputs_and_dft(j, carry):
        for u in range(per_k1):
            outputs(j * per_k1 + u)
        _dft_second_stage(j, tw_ref, w2_ref, cc, y_ref, z_scr)
        return carry

    lax.fori_loop(0, DFT_ROWS, outputs_and_dft, 0, unroll=RET_UNROLL // per_k1)
    _dft_emit(z_scr, z_ref)


def _retention_and_dft(a_f, a_b, q, k, v, kc, vc, y, tw, w2, w_c):
    b, seq, _ = v.shape
    lc = kc.shape[1]
    c = RET_CHUNK
    n = seq // c
    r = DFT_ROWS
    _, _, n1, n2, w = y.shape
    assert n % r == 0 and n1 // r == HEADS // 2
    dec = pl.BlockSpec((2, 1, LANES), lambda i, p: (p, 0, 0))
    qk = lambda rows: pl.BlockSpec((1, rows, LANES), lambda i, p: (i, 0, p))
    vv = lambda rows: pl.BlockSpec((1, rows, 2 * V_DIM), lambda i, p: (i, 0, p))
    return pl.pallas_call(
        _retention_kernel,
        grid=(b, HEADS // 2),
        in_specs=[dec, dec, qk(seq), qk(seq), vv(seq), qk(lc), vv(lc),
                  pl.BlockSpec((r, 2, n2), lambda i, p: (p, 0, 0)),
                  _const_spec((2, n2, n2)),
                  _const_spec((2 * F_GROUP_DIM, F_GROUP_DIM)),
                  pl.BlockSpec((1, 2, r, n2, w), lambda i, p: (i, 0, p, 0, 0))],
        out_specs=[vv(seq), pl.BlockSpec((1, n2, r, w), lambda i, p: (i, 0, p, 0))],
        out_shape=[jax.ShapeDtypeStruct((b, seq, V_WIDTH), BF16),
                   jax.ShapeDtypeStruct((b, n2, n1, w), BF16)],
        scratch_shapes=[pltpu.VMEM((n, 2 * LANES, 2 * V_DIM), F32),
                        pltpu.VMEM((n, 2 * LANES, 2 * V_DIM), BF16),
                        pltpu.VMEM((2, c, c), F32),
                        pltpu.VMEM((2, c, LANES), F32),
                        pltpu.VMEM((F_GROUPS, (n2 + DFT_PAD) * r, F_GROUP_DIM), F32)],
        compiler_params=_params(("parallel", "parallel")),
        name="retention_dft",
    )(a_f, a_b, q, k, v, kc, vc, tw, w2, w_c, y)


def _dft_a_kernel(w_ref, x_ref, y_ref):
    _, n1, rows, w = x_ref.shape
    r = DFT_ROWS
    nh = n1 // 2 + 1
    for r0 in range(0, rows, r):
        x = x_ref[0, :, r0:r0 + r, :].reshape(n1 * r, w)
        y = _dot(w_ref[...], x).reshape(2, nh, r, w).astype(BF16)
        y_ref[0, :, 0:nh, r0:r0 + r, :] = y
        for k1 in range(1, n1 // 2):
            y_ref[0, 0, n1 - k1, r0:r0 + r, :] = y[0, k1]
            y_ref[0, 1, n1 - k1, r0:r0 + r, :] = -y[1, k1]


def _dft_second_stage(j, tw_ref, w2_ref, cc, y_ref, z_scr):
    n2 = y_ref.shape[3]
    w2c = w2_ref[0]
    w2s = w2_ref[1]
    tc = tw_ref[j, 0:1, :]
    ts = tw_ref[j, 1:2, :]
    ec = w2c * tc - w2s * ts
    es = w2s * tc + w2c * ts
    m = jnp.concatenate([jnp.concatenate([ec, -es], axis=1),
                         jnp.concatenate([es, ec], axis=1)], axis=0).astype(BF16)
    y = jnp.concatenate([y_ref[0, 0, j], y_ref[0, 1, j]], axis=0)
    zz = _dot(m, y)
    for g in range(F_GROUPS):
        cols = slice(g * F_GROUP_DIM, (g + 1) * F_GROUP_DIM)
        zcs = jnp.concatenate([zz[:n2, cols], zz[n2:, cols]], axis=1).astype(BF16)
        z_scr[g, pl.ds(pl.multiple_of(j * (n2 + DFT_PAD), DFT_PAD), n2), :] = _dot(zcs, cc)


def _dft_emit(z_scr, z_ref):
    n2 = z_ref.shape[1]
    pitch = n2 + DFT_PAD
    z = jnp.concatenate(
        [jnp.concatenate([z_scr[g, pl.ds(k2, DFT_ROWS, stride=pitch), :] for k2 in range(n2)], axis=0)
         for g in range(F_GROUPS)], axis=1)
    z_ref[0] = z.reshape(n2, DFT_ROWS, F_WIDTH).astype(BF16)


def _dft_tables(seq):
    n1 = DFT_N1
    n2 = seq // n1

    def cs(num, den):
        ang = 2.0 * np.pi * (num % den) / den
        return np.cos(ang), np.sin(ang)

    a = np.arange(n1)
    w_a = np.concatenate(cs(a[:n1 // 2 + 1, None] * a[None, :], n1), axis=0)
    w_a = np.kron(w_a.astype(np.float32), np.eye(DFT_ROWS, dtype=np.float32))
    m = np.arange(n2)
    tw = np.stack(cs(a[:, None] * m[None, :], seq), axis=1)
    w2 = np.stack(cs(m[:, None] * m[None, :], n2), axis=0)
    ch = np.arange(F_GROUP_DIM)
    cc, sc = cs(ch[:, None] * ch[None, :], F_GROUP_DIM)
    scale = 1.0 / np.sqrt(seq * F_GROUP_DIM)
    w_c = np.concatenate([cc, -sc], axis=0) * scale
    return [jnp.asarray(t, dtype=F32) for t in (w_a, tw, w2, w_c)]


def _dft_first_stage(f):
    b, seq, w = f.shape
    n1 = DFT_N1
    n2 = seq // n1
    w_a, tw, w2, w_c = _dft_tables(seq)
    w_a = w_a.astype(BF16)
    y = pl.pallas_call(
        _dft_a_kernel,
        grid=(b, n2 // DFT_A_ROWS),
        in_specs=[_const_spec(w_a.shape),
                  pl.BlockSpec((1, n1, DFT_A_ROWS, w), lambda i, j: (i, 0, j, 0))],
        out_specs=pl.BlockSpec((1, 2, n1, DFT_A_ROWS, w), lambda i, j: (i, 0, 0, j, 0)),
        out_shape=jax.ShapeDtypeStruct((b, 2, n1, n2, w), BF16),
        compiler_params=_params(("parallel", "parallel")),
        name="dft_a",
    )(w_a, f.reshape(b, n1, n2, w))
    return y, tw, w2, w_c


def _merge_kernel(x_ref, mod_ref, nw_ref, n2w_ref, ro_ref, sg_ref, z_ref, wbg_ref, bbg_ref,
                  wro_ref, wfo_ref, wout_ref, *refs, n_cast):
    o_ref, h2_ref = refs[n_cast:n_cast + 2]
    _run_casts(refs[:n_cast] + refs[n_cast + 2:])
    for r0 in range(0, x_ref.shape[0], MERGE_SUB):
        rows = slice(r0, r0 + MERGE_SUB)
        x = x_ref[rows, :]
        h = _mod_norm(x, nw_ref[...], mod_ref[0, 0:1, :], mod_ref[0, 1:2, :])
        t = jnp.tanh(_dot(h.astype(BF16), wbg_ref[...]) + 0.5 * bbg_ref[...])
        gated = []
        for hd in range(HEADS):
            cols = slice(hd * V_DIM, (hd + 1) * V_DIM)
            o = ro_ref[rows, cols].astype(F32)
            oc = o - jnp.mean(o, axis=-1, keepdims=True)
            var = jnp.mean(oc * oc, axis=-1, keepdims=True)
            gated.append((oc * lax.rsqrt(var + EPS) * sg_ref[rows, cols].astype(F32)).astype(BF16))
        ret_d = _dot(jnp.concatenate(gated, axis=1), wro_ref[...])
        four_d = _dot(z_ref[rows, :], wfo_ref[...])
        m2 = (t[:, :D_MODEL] * ret_d + ret_d) + (t[:, D_MODEL:] * four_d + four_d)
        y2 = _dot(m2.astype(BF16), wout_ref[...])
        x1 = x + (0.5 * mod_ref[0, 2:3, :]) * y2
        o_ref[rows, :] = x1
        h2 = _mod_norm(x1, n2w_ref[...], mod_ref[0, 3:4, :], mod_ref[0, 4:5, :])
        h2_ref[rows, :] = h2.astype(BF16)


def _merge(x2, mod3, norm_w, norm2_w, ro, sg, z, w_bg, b_bg, w_ro, w_fo, w_out, seq, to_cast):
    t, d = x2.shape
    tl = MERGE_TILE
    tpb = seq // tl
    tok = lambda w: pl.BlockSpec((tl, w), lambda i: (i, 0))
    cast_specs, cast_shapes = _cast_jobs(to_cast, t // tl)
    out = pl.pallas_call(
        functools.partial(_merge_kernel, n_cast=len(to_cast)),
        grid=(t // tl,),
        in_specs=[tok(d),
                  pl.BlockSpec((1, N_MOD, d), lambda i: (i // tpb, 0, 0)),
                  _const_spec((1, d)), _const_spec((1, d)),
                  tok(V_WIDTH), tok(V_WIDTH), tok(F_WIDTH),
                  _const_spec(w_bg.shape), _const_spec((1, 2 * d)),
                  _const_spec(w_ro.shape), _const_spec(w_fo.shape),
                  _const_spec(w_out.shape)] + cast_specs,
        out_specs=[tok(d), tok(d)] + cast_specs,
        out_shape=[jax.ShapeDtypeStruct((t, d), F32),
                   jax.ShapeDtypeStruct((t, d), BF16)] + cast_shapes,
        compiler_params=_params(("arbitrary",)),
        name="merge",
    )(x2, mod3, norm_w, norm2_w, ro, sg, z, w_bg, b_bg, w_ro, w_fo, w_out, *to_cast)
    return out[:2], out[2:]


HALO = 8
FFN_LEAD = 16
FFN_SUB = 256


def _two_gelu_tanh(x):
    c1 = np.sqrt(2.0 / np.pi)
    return x + x * jnp.tanh(x * (c1 + (c1 * 0.044715) * (x * x)))


def _ffn_kernel(xm_ref, hm_ref, xp_ref, xn_ref, mod_ref, nw_ref, wu_ref, cw_ref, cb_ref,
                wd_ref, fnw_ref, o_ref, h_scr, u_scr, act_scr, y_scr, *, tiles_per_seq):
    tl = xm_ref.shape[0]
    sub = FFN_SUB
    half = sub // 2
    nc = FFN_CHUNK
    i = pl.program_id(0)
    keep_prev = ((i % tiles_per_seq) != 0).astype(F32)
    keep_next = ((i % tiles_per_seq) != tiles_per_seq - 1).astype(F32)

    def pre(x):
        return _mod_norm(x, nw_ref[...], mod_ref[0, 3:4, :], mod_ref[0, 4:5, :])

    pad = jnp.zeros((FFN_LEAD - HALO, xm_ref.shape[1]), F32)
    h_scr[0:FFN_LEAD] = jnp.concatenate([pad, pre(xp_ref[...]) * keep_prev], axis=0).astype(BF16)
    h_scr[FFN_LEAD:FFN_LEAD + tl] = hm_ref[...]
    h_scr[FFN_LEAD + tl:] = jnp.concatenate([pre(xn_ref[...]) * keep_next, pad], axis=0).astype(BF16)

    def project(j):
        hb = h_scr[...]
        for part in range(2):
            lo = part * FFN_DIM + j * nc
            u = _dot(hb, wu_ref[:, lo:lo + nc])
            for s in range(nc // LANES):
                u_scr[j % 2, part, s] = u[:, s * LANES:(s + 1) * LANES]

    def conv(j, part, s, blk, scale):
        lo = part * FFN_DIM + j * nc + s * LANES
        w = cw_ref[:, lo:lo + LANES] * scale
        bias = cb_ref[:, lo:lo + LANES] * scale
        first = FFN_LEAD + blk * sub
        rows = lambda start: u_scr[j % 2, part, s, pl.ds(start, half, stride=2), :]
        before, even, odd, after = rows(first - 1), rows(first), rows(first + 1), rows(first + 2)
        return (before * w[0:1] + even * w[1:2] + odd * w[2:3] + bias,
                even * w[0:1] + odd * w[1:2] + after * w[2:3] + bias)

    def activate(j):
        for s in range(nc // LANES):
            cols = slice(j * nc + s * LANES, j * nc + (s + 1) * LANES)
            for blk in range(tl // sub):
                gate = conv(j, 0, s, blk, 1.0)
                val = conv(j, 1, s, blk, 0.5)
                for par in range(2):
                    r0 = blk * sub + par * half
                    act_scr[r0:r0 + half, cols] = (_two_gelu_tanh(gate[par]) * val[par]).astype(BF16)

    project(0)
    for j in range(N_FFN_CHUNKS):
        if j + 1 < N_FFN_CHUNKS:
            project(j + 1)
        activate(j)
    for blk in range(tl // sub):
        r0 = blk * sub
        y = _dot(act_scr[r0:r0 + sub, :], wd_ref[...])
        for s in range(y_scr.shape[0]):
            for par in range(2):
                y_scr[s, pl.ds(par, half, stride=2), :] = (
                    y[par * half:(par + 1) * half, s * LANES:(s + 1) * LANES])
        y = jnp.concatenate([y_scr[s] for s in range(y_scr.shape[0])], axis=1)
        x2 = xm_ref[r0:r0 + sub, :] + mod_ref[0, 5:6, :] * y
        o_ref[r0:r0 + sub, :] = _rms_norm(x2, fnw_ref[...])


def _ffn(x1, h2, mod3, norm_w, w_u, cw, cb, w_d, fnorm_w, seq):
    t, d = x1.shape
    tl = FFN_TILE
    tpb = seq // tl
    hb = tl // HALO
    last = t // HALO - 1
    rows = FFN_LEAD + tl + FFN_LEAD
    return pl.pallas_call(
        functools.partial(_ffn_kernel, tiles_per_seq=tpb),
        grid=(t // tl,),
        in_specs=[pl.BlockSpec((tl, d), lambda i: (i, 0)),
                  pl.BlockSpec((tl, d), lambda i: (i, 0)),
                  pl.BlockSpec((HALO, d), lambda i: (jnp.maximum(i * hb - 1, 0), 0)),
                  pl.BlockSpec((HALO, d), lambda i: (jnp.minimum((i + 1) * hb, last), 0)),
                  pl.BlockSpec((1, N_MOD, d), lambda i: (i // tpb, 0, 0)),
                  _const_spec((1, d)),
                  _const_spec(w_u.shape), _const_spec(cw.shape), _const_spec(cb.shape),
                  _const_spec(w_d.shape), _const_spec((1, d))],
        out_specs=pl.BlockSpec((tl, d), lambda i: (i, 0)),
        out_shape=jax.ShapeDtypeStruct((t, d), F32),
        scratch_shapes=[pltpu.VMEM((rows, d), BF16),
                        pltpu.VMEM((2, 2, FFN_CHUNK // LANES, rows, LANES), F32),
                        pltpu.VMEM((tl, FFN_DIM), BF16),
                        pltpu.VMEM((d // LANES, FFN_SUB, LANES), F32)],
        compiler_params=_params(("parallel",)),
        name="ffn",
    )(x1, h2, x1, x1, mod3, norm_w, w_u, cw, cb, w_d, fnorm_w)


def _rope_tables(seq):
    n_freq = QK_DIM // 4
    lane = jnp.arange(LANES)
    inv = ROPE_BASE ** (-(lane % n_freq).astype(F32) / n_freq)
    sign = jnp.where(lane < LANES // 2, -1.0, 1.0).astype(F32)

    def table(n):
        ang = jnp.arange(n, dtype=F32)[:, None] * inv[None, :]
        return jnp.stack([jnp.cos(ang), jnp.sin(ang) * sign])

    return table(seq // GRID_W), table(GRID_W)


def kernel(x, c, ctx, c_ctx, w_mod, b_mod, norm1_w, w_in, ret_decay_f, ret_decay_b,
           w_ret_out, w_four_out, w_branch_gate, b_branch_gate, w_out, norm2_w,
           w_up, conv_w, conv_b, w_down, final_norm_w):
    assert w_mod.shape[0] == 1, "single-layer block"
    b, seq, d = x.shape
    t = b * seq
    assert b == 2 and seq % RET_CHUNK == 0
    assert seq % FFN_TILE == 0 and seq % MERGE_TILE == 0 and seq % PROJ_TILE == 0

    c8 = jnp.concatenate([c, c_ctx[None, :], jnp.zeros((8 - b - 1, d), F32)], axis=0)
    mod, w_in_b, w_qk = _modulation(c8, w_mod[0], b_mod[0], w_in[0])
    mod3 = mod.reshape(8, N_MOD, d)
    n1w = norm1_w[0].reshape(1, d)

    rtab, ctab = _rope_tables(seq)
    x2 = x.reshape(t, d)
    (q, k, v, sg, f, kc, vc), (w_bg, w_ro, w_fo, w_o) = _proj(
        x2, mod3, n1w, w_qk, w_in_b, rtab, ctab, ctx, seq,
        [w_branch_gate[0], w_ret_out[0], w_four_out[0], w_out[0]], (0.5, 1.0, 1.0, 1.0))

    a_f = jnp.broadcast_to(ret_decay_f[0][:, None, None], (HEADS, 1, LANES))
    a_b = jnp.broadcast_to(ret_decay_b[0][:, None, None], (HEADS, 1, LANES))
    y, tw, w2, w_c = _dft_first_stage(f.reshape(b, seq, F_WIDTH))
    ro, z = _retention_and_dft(a_f, a_b, q.reshape(b, seq, QK_WIDTH), k.reshape(b, seq, QK_WIDTH),
                               v.reshape(b, seq, V_WIDTH), kc, vc, y, tw, w2, w_c)

    n2w = norm2_w[0].reshape(1, d)
    (x1, h2), (w_u, w_d) = _merge(
        x2, mod3, n1w, n2w, ro.reshape(t, V_WIDTH), sg, z.reshape(t, F_WIDTH),
        w_bg, b_branch_gate[0].reshape(1, 2 * d), w_ro, w_fo, w_o, seq, [w_up[0], w_down[0]])

    out = _ffn(x1, h2, mod3, n2w, w_u, conv_w[0], conv_b[0].reshape(1, 2 * FFN_DIM), w_d,
               final_norm_w.reshape(1, d), seq)
    return out.reshape(b, seq, d)
```
